```python
import math
import jax, jax.numpy as jnp
from jax import lax
import numpy as np

D_MODEL = 1024
BATCH = 8
SEQ = 2048
DEPTH = 1
DEC_BATCH = 8
DEC_SEQ = 16
PAST_LEN = 1024

CHUNK = 64
N_META = 16
Q_BLOCK = 128
H_A = 8
DH_A = 64
H_I = 4
D_IDX = 64
TOPK_MAX = 256
H_B = 4
DH_B = 64
DV_B = 2 * DH_B
D_MIX = H_A * DH_A + H_B * DV_B
D_FF = 4 * D_MODEL
SPLITS = (H_A * DH_A, DH_A, DH_A, H_I * D_IDX, D_IDX, H_I, H_B * 2 * DH_B, H_B * 2 * DH_B, H_B * DV_B)
D_IN = H_A * DH_A + 2 * DH_A + H_I * D_IDX + D_IDX + H_I + 2 * H_B * 2 * DH_B + H_B * DV_B
EPS = 1e-6

kernel_name = "hybrid_dsa_diffattn_streaming_step"


def _alibi_slopes(n):
    return jnp.asarray([2.0 ** (-8.0 * (i + 1) / n) for i in range(n)], jnp.float32)


def _rmsnorm(x, g):
    x32 = x.astype(jnp.float32)
    y = x32 * lax.rsqrt(jnp.mean(x32 * x32, axis=-1, keepdims=True) + EPS)
    return (y * g.astype(jnp.float32)).astype(x.dtype)


def _layernorm(x, g, b):
    x32 = x.astype(jnp.float32)
    mu = jnp.mean(x32, axis=-1, keepdims=True)
    xc = x32 - mu
    y = xc * lax.rsqrt(jnp.mean(xc * xc, axis=-1, keepdims=True) + EPS)
    return (y * g.astype(jnp.float32) + b.astype(jnp.float32)).astype(x.dtype)


def _project(h, g, w_in, kn_g, kn_b):
    B, T, _ = h.shape
    z = _rmsnorm(h, g) @ w_in
    cuts = [int(c) for c in np.cumsum(SPLITS)[:-1]]
    qa, ka, va, qi, ki, wi, qb, kb, vb = jnp.split(z, cuts, axis=-1)
    qa = qa.reshape(B, T, H_A, DH_A)
    qi = qi.reshape(B, T, H_I, D_IDX)
    ki = _layernorm(ki, kn_g, kn_b)
    qb = qb.reshape(B, T, H_B, 2, DH_B)
    kb = kb.reshape(B, T, H_B, 2, DH_B)
    vb = vb.reshape(B, T, H_B, DV_B)
    return qa, ka, va, qi, ki, wi, qb, kb, vb


def _dsa_attend(qa, qi, wi, qpos, qchunk, ka, va, ki, kpos, kchunk, k_top):
    f32 = jnp.float32
    rel = jax.nn.relu(jnp.einsum('bqhd,bld->bqhl', qi, ki).astype(f32) * (D_IDX ** -0.5))
    score = jnp.einsum('bqhl,bqh->bql', rel, wi.astype(f32) * (H_I ** -0.5))
    adm = kchunk[None, :] <= qchunk[:, None]
    score = jnp.where(adm[None], score, -jnp.inf)
    _, sel = lax.top_k(score, k_top)
    gather = jax.vmap(lambda t, i: t[i])
    k_sel = gather(ka, sel)
    v_sel = gather(va, sel)
    valid = jnp.take_along_axis(jnp.broadcast_to(adm[None], score.shape), sel, axis=-1)
    dist = jnp.abs(qpos[None, :, None] - kpos[sel]).astype(f32)
    logits = jnp.einsum('bqhd,bqkd->bqhk', qa, k_sel).astype(f32) * (DH_A ** -0.5)
    logits = logits - _alibi_slopes(H_A)[None, None, :, None] * dist[:, :, None, :]
    logits = jnp.where(valid[:, :, None, :], logits, -jnp.inf)
    p = jax.nn.softmax(logits, axis=-1).astype(va.dtype)
    return jnp.einsum('bqhk,bqkd->bqhd', p, v_sel)


def _diff_attend(qb, qpos, qchunk, kb, vb, kpos, kchunk, lam):
    f32 = jnp.float32
    logits = jnp.einsum('bqhcd,blhcd->bhcql', qb, kb).astype(f32) * (DH_B ** -0.5)
    dist = jnp.abs(qpos[:, None] - kpos[None, :]).astype(f32)
    logits = logits - _alibi_slopes(H_B)[None, :, None, None, None] * dist
    adm = kchunk[None, :] <= qchunk[:, None]
    logits = jnp.where(adm, logits, -jnp.inf)
    p = jax.nn.softmax(logits, axis=-1)
    a = (p[:, :, 0] - lam * p[:, :, 1]).astype(vb.dtype)
    return jnp.einsum('bhql,blhe->bqhe', a, vb)


def _blocked(fn, q_args, qpos, qchunk):
    n = qpos.shape[0]
    nb = -(-n // Q_BLOCK)
    pad = nb * Q_BLOCK - n

    def prep(a):
        a = jnp.pad(a, [(0, 0), (0, pad)] + [(0, 0)] * (a.ndim - 2))
        a = a.reshape((a.shape[0], nb, Q_BLOCK) + a.shape[2:])
        return jnp.moveaxis(a, 1, 0)

    posb = jnp.pad(qpos, (0, pad), mode='edge').reshape(nb, Q_BLOCK)
    chb = jnp.pad(qchunk, (0, pad), mode='edge').reshape(nb, Q_BLOCK)
    qs = tuple(prep(a) for a in q_args)
    out = lax.map(lambda xs: fn(*xs[0], xs[1], xs[2]), (qs, posb, chb))
    out = jnp.moveaxis(out, 0, 1)
    out = out.reshape((out.shape[0], nb * Q_BLOCK) + out.shape[3:])
    return out[:, :n]


def _finish(h, o_a, o_b, subln_g, lam_init, w_o, g_mlp, w_up, w_down):
    B, T, _ = h.shape
    o_b = _rmsnorm(o_b, subln_g) * (1.0 - lam_init)
    o = jnp.concatenate([o_a.reshape(B, T, H_A * DH_A), o_b.reshape(B, T, H_B * DV_B)], axis=-1)
    h = h + o @ w_o
    u = jax.nn.relu(_rmsnorm(h, g_mlp) @ w_up)
    return h + (u * u) @ w_down


def setup_inputs(seed: int = 0) -> dict:
    key = jax.random.key(seed)
    ks = jax.random.split(key, 24)
    f32 = jnp.float32
    nrm = lambda k, s, sc: jax.random.normal(k, s, f32) * sc
    return {
        'x_prompt': nrm(ks[0], (BATCH, SEQ, D_MODEL), 1.0),
        'x_sample': nrm(ks[1], (DEC_BATCH, DEC_SEQ, D_MODEL), 1.0),
        'cache_a_k': nrm(ks[2], (DEPTH, DEC_BATCH, PAST_LEN, DH_A), 1.0),
        'cache_a_v': nrm(ks[3], (DEPTH, DEC_BATCH, PAST_LEN, DH_A), 1.0),
        'cache_a_idx_k': nrm(ks[4], (DEPTH, DEC_BATCH, PAST_LEN, D_IDX), 1.0),
        'cache_b_k': nrm(ks[5], (DEPTH, DEC_BATCH, PAST_LEN, H_B, 2, DH_B), 1.0),
        'cache_b_v': nrm(ks[6], (DEPTH, DEC_BATCH, PAST_LEN, H_B, DV_B), 1.0),
        'meta_tokens': nrm(ks[7], (N_META, D_MODEL), 1.0),
        'attn_norm_g': 1.0 + nrm(ks[8], (DEPTH, D_MODEL), 0.02),
        'w_in': nrm(ks[9], (DEPTH, D_MODEL, D_IN), D_MODEL ** -0.5),
        'idx_k_norm_g': 1.0 + nrm(ks[10], (DEPTH, D_IDX), 0.02),
        'idx_k_norm_b': nrm(ks[11], (DEPTH, D_IDX), 0.02),
        'lambda_q1': nrm(ks[12], (DEPTH, DH_B), 0.1),
        'lambda_k1': nrm(ks[13], (DEPTH, DH_B), 0.1),
        'lambda_q2': nrm(ks[14], (DEPTH, DH_B), 0.1),
        'lambda_k2': nrm(ks[15], (DEPTH, DH_B), 0.1),
        'subln_g': 1.0 + nrm(ks[16], (DEPTH, DV_B), 0.02),
        'w_o': nrm(ks[17], (DEPTH, D_MIX, D_MODEL), D_MIX ** -0.5),
        'mlp_norm_g': 1.0 + nrm(ks[18], (DEPTH, D_MODEL), 0.02),
        'w_up': nrm(ks[19], (DEPTH, D_MODEL, D_FF), D_MODEL ** -0.5),
        'w_down': nrm(ks[20], (DEPTH, D_FF, D_MODEL), D_FF ** -0.5),
        'final_norm_g': 1.0 + nrm(ks[21], (D_MODEL,), 0.02),
    }


def reference(x_prompt, x_sample, cache_a_k, cache_a_v, cache_a_idx_k, cache_b_k, cache_b_v,
              meta_tokens, attn_norm_g, w_in, idx_k_norm_g, idx_k_norm_b,
              lambda_q1, lambda_k1, lambda_q2, lambda_k2, subln_g, w_o,
              mlp_norm_g, w_up, w_down, final_norm_g):
    f32 = jnp.float32
    B, S, _ = x_prompt.shape
    n = N_META + S
    meta = jnp.broadcast_to(meta_tokens[None].astype(x_prompt.dtype), (B, N_META, D_MODEL))
    h_p = jnp.concatenate([meta, x_prompt], axis=1)
    pos_p = jnp.arange(n, dtype=jnp.int32)
    chunk_p = jnp.where(pos_p < N_META, -1, (pos_p - N_META) // CHUNK)
    k_top_p = min(TOPK_MAX, S // 4)
    Ts = x_sample.shape[1]
    past = cache_a_k.shape[2]
    L = past + Ts
    pos_s = jnp.arange(L, dtype=jnp.int32)
    chunk_s = pos_s // CHUNK
    qpos_s = pos_s[past:]
    qchunk_s = chunk_s[past:]
    k_top_s = min(TOPK_MAX, L // 4)
    h_s = x_sample

    pak, pav, paik, pbk, pbv = [], [], [], [], []
    sak, sav, saik, sbk, sbv = [], [], [], [], []
    for l in range(DEPTH):
        lam_init = 0.8 - 0.6 * math.exp(-0.3 * l)
        lam = (jnp.exp(jnp.sum(lambda_q1[l].astype(f32) * lambda_k1[l].astype(f32)))
               - jnp.exp(jnp.sum(lambda_q2[l].astype(f32) * lambda_k2[l].astype(f32))) + lam_init)

        qa, ka, va, qi, ki, wi, qb, kb, vb = _project(h_p, attn_norm_g[l], w_in[l], idx_k_norm_g[l], idx_k_norm_b[l])
        o_a = _blocked(lambda a, b, c, p, cc: _dsa_attend(a, b, c, p, cc, ka, va, ki, pos_p, chunk_p, k_top_p),
                       (qa, qi, wi), pos_p, chunk_p)
        o_b = _blocked(lambda a, p, cc: _diff_attend(a, p, cc, kb, vb, pos_p, chunk_p, lam),
                       (qb,), pos_p, chunk_p)
        h_p = _finish(h_p, o_a, o_b, subln_g[l], lam_init, w_o[l], mlp_norm_g[l], w_up[l], w_down[l])
        pak.append(ka); pav.append(va); paik.append(ki); pbk.append(kb); pbv.append(vb)

        qa, ka_n, va_n, qi, ki_n, wi, qb, kb_n, vb_n = _project(h_s, attn_norm_g[l], w_in[l], idx_k_norm_g[l], idx_k_norm_b[l])
        ka_all = jnp.concatenate([cache_a_k[l].astype(ka_n.dtype), ka_n], axis=1)
        va_all = jnp.concatenate([cache_a_v[l].astype(va_n.dtype), va_n], axis=1)
        ki_all = jnp.concatenate([cache_a_idx_k[l].astype(ki_n.dtype), ki_n], axis=1)
        kb_all = jnp.concatenate([cache_b_k[l].astype(kb_n.dtype), kb_n], axis=1)
        vb_all = jnp.concatenate([cache_b_v[l].astype(vb_n.dtype), vb_n], axis=1)
        o_a = _dsa_attend(qa, qi, wi, qpos_s, qchunk_s, ka_all, va_all, ki_all, pos_s, chunk_s, k_top_s)
        o_b = _diff_attend(qb, qpos_s, qchunk_s, kb_all, vb_all, pos_s, chunk_s, lam)
        h_s = _finish(h_s, o_a, o_b, subln_g[l], lam_init, w_o[l], mlp_norm_g[l], w_up[l], w_down[l])
        sak.append(ka_n); sav.append(va_n); saik.append(ki_n); sbk.append(kb_n); sbv.append(vb_n)

    y_prompt = _rmsnorm(h_p[:, N_META:], final_norm_g)
    y_sample = _rmsnorm(h_s, final_norm_g)
    return (y_prompt, y_sample,
            jnp.stack(pak), jnp.stack(pav), jnp.stack(paik), jnp.stack(pbk), jnp.stack(pbv),
            jnp.stack(sak), jnp.stack(sav), jnp.stack(saik), jnp.stack(sbk), jnp.stack(sbv))
```

```python
import functools

import jax
import jax.numpy as jnp
from jax import lax
from jax.experimental import pallas as pl
from jax.experimental.pallas import tpu as pltpu

F32 = jnp.float32
BF16 = jnp.bfloat16
I32 = jnp.int32

D_MODEL = 1024
CHUNK = 64
N_META = 16
H_A = 8
DH_A = 64
H_I = 4
D_IDX = 64
TOPK_MAX = 256
H_B = 4
DH_B = 64
DV_B = 2 * DH_B
D_FF = 4 * D_MODEL
EPS = 1e-6
LAM_INIT = 0.2

LANES = 128
C_QA, C_QB, C_KB, C_VB, C_QI, C_SM = 0, 512, 1024, 1536, 2048, 2304
W_COLS = 2560
SM_KA, SM_VA, SM_KI, SM_WI = 0, 64, 128, 192

Q_SCALE = DH_A ** -0.5
W_SCALE = (H_I ** -0.5) * (D_IDX ** -0.5)
SLOPES_A = tuple(2.0 ** (-8.0 * (i + 1) / H_A) for i in range(H_A))
SLOPES_B = tuple(2.0 ** (-8.0 * (i + 1) / H_B) for i in range(H_B))

INT_MIN = -(2 ** 31)
NEG_INIT = -1e30
TQ = 256
RS = 128
VMEM_LIMIT = 56 * 1024 * 1024

_NT = (((1,), (1,)), ((), ()))


def _dot(a, b):
    return jnp.dot(a, b, preferred_element_type=F32)


def _dot_nt(a, b):
    return lax.dot_general(a, b, _NT, preferred_element_type=F32)


def _rms(x, g):
    return (x * lax.rsqrt(jnp.mean(x * x, axis=-1, keepdims=True) + EPS)) * g


def _to_key(score):
    b = lax.bitcast_convert_type(score, I32)
    return b ^ ((b >> 31) & jnp.int32(0x7FFFFFFF))


def _wide(x, width):
    reps = width // LANES
    return x if reps == 1 else jnp.concatenate([x] * reps, axis=1)


def _proj_body(x_ref, g_ref, w_ref, kng_ref, knb_ref,
               qa_ref, qi_ref, qb_ref, kbh_ref, kb_ref, vb_ref, vbb_ref,
               sm_ref, kab_ref, vaa_ref, kib_ref):
    x = x_ref[...]
    hn = _rms(x, g_ref[...]).astype(BF16)
    rows = x.shape[0]

    def mm(lo, width):
        return _dot(hn, w_ref[:, lo:lo + width])

    z = mm(C_QA, 512) * Q_SCALE
    for h in range(H_A):
        qa_ref[h] = z[:, h * DH_A:(h + 1) * DH_A].astype(BF16)
    z = mm(C_QB, 512) * Q_SCALE
    for hc in range(2 * H_B):
        qb_ref[hc] = z[:, hc * DH_B:(hc + 1) * DH_B].astype(BF16)
    z = mm(C_KB, 512)
    kb_ref[...] = z
    for hc in range(2 * H_B):
        kbh_ref[hc] = z[:, hc * DH_B:(hc + 1) * DH_B].astype(BF16)
    z = mm(C_VB, 512)
    vb_ref[...] = z
    vbb_ref[...] = z.astype(BF16)
    z = mm(C_QI, 256)
    for h in range(H_I):
        qi_ref[h] = z[:, h * D_IDX:(h + 1) * D_IDX].astype(BF16)
    z = mm(C_SM, 256)
    ka = z[:, SM_KA:SM_KA + 64]
    va = z[:, SM_VA:SM_VA + 64]
    ki = z[:, SM_KI:SM_KI + 64]
    xc = ki - jnp.mean(ki, axis=-1, keepdims=True)
    ki = xc * lax.rsqrt(jnp.mean(xc * xc, axis=-1, keepdims=True) + EPS) * kng_ref[...] + knb_ref[...]
    sm_ref[:, 0:128] = z[:, 0:128]
    sm_ref[:, SM_KI:SM_KI + 64] = ki
    sm_ref[:, SM_WI:SM_WI + 64] = z[:, SM_WI:SM_WI + 64] * W_SCALE
    kab_ref[...] = ka.astype(BF16)
    kib_ref[...] = ki.astype(BF16)
    vaa_ref[...] = jnp.concatenate([va.astype(BF16), jnp.ones((rows, 64), BF16)], axis=1)


def _proj(x, g, w, kng, knb, tm):
    m = x.shape[0]
    assert m % tm == 0
    row = lambda i: (i, 0)
    head = lambda i: (0, i, 0)
    const = lambda i: (0, 0)
    out_shape = (
        jax.ShapeDtypeStruct((H_A, m, DH_A), BF16),
        jax.ShapeDtypeStruct((H_I, m, D_IDX), BF16),
        jax.ShapeDtypeStruct((2 * H_B, m, DH_B), BF16),
        jax.ShapeDtypeStruct((2 * H_B, m, DH_B), BF16),
        jax.ShapeDtypeStruct((m, 512), F32),
        jax.ShapeDtypeStruct((m, 512), F32),
        jax.ShapeDtypeStruct((m, 512), BF16),
        jax.ShapeDtypeStruct((m, 256), F32),
        jax.ShapeDtypeStruct((m, 64), BF16),
        jax.ShapeDtypeStruct((m, 128), BF16),
        jax.ShapeDtypeStruct((m, 64), BF16),
    )
    out_specs = (
        pl.BlockSpec((H_A, tm, DH_A), head),
        pl.BlockSpec((H_I, tm, D_IDX), head),
        pl.BlockSpec((2 * H_B, tm, DH_B), head),
        pl.BlockSpec((2 * H_B, tm, DH_B), head),
        pl.BlockSpec((tm, 512), row),
        pl.BlockSpec((tm, 512), row),
        pl.BlockSpec((tm, 512), row),
        pl.BlockSpec((tm, 256), row),
        pl.BlockSpec((tm, 64), row),
        pl.BlockSpec((tm, 128), row),
        pl.BlockSpec((tm, 64), row),
    )
    return pl.pallas_call(
        _proj_body,
        out_shape=out_shape,
        grid=(m // tm,),
        in_specs=[
            pl.BlockSpec((tm, D_MODEL), row),
            pl.BlockSpec((1, D_MODEL), const),
            pl.BlockSpec((D_MODEL, W_COLS), const),
            pl.BlockSpec((1, D_IDX), const),
            pl.BlockSpec((1, D_IDX), const),
        ],
        out_specs=out_specs,
        compiler_params=pltpu.CompilerParams(
            dimension_semantics=("arbitrary",), vmem_limit_bytes=VMEM_LIMIT),
        name="proj",
    )(x, g, w, kng, knb)


def _topk_threshold(count_rows, k_top, idx_bits):
    rows = RS

    def value_step(s, t):
        cand = t ^ jnp.left_shift(jnp.int32(1), 31 - s)
        cand_b = jnp.broadcast_to(cand, (rows, LANES))
        cnt = count_rows(lambda k, idx: k >= cand_b)
        return jnp.where(cnt >= k_top, cand, t)

    t = lax.fori_loop(0, 32, value_step, jnp.full((rows, 1), INT_MIN, I32))
    t_b = jnp.broadcast_to(t, (rows, LANES))
    need = k_top - count_rows(lambda k, idx: k > t_b)

    def tie_step(s, x):
        cand = x | jnp.left_shift(jnp.int32(1), idx_bits - 1 - s)
        cand_b = jnp.broadcast_to(cand, (rows, LANES))
        cnt = count_rows(lambda k, idx: (k == t_b) & (idx < cand_b))
        return jnp.where(cnt < need, cand, x)

    x = lax.fori_loop(0, idx_bits, tie_step, jnp.zeros((rows, 1), I32))
    x = jnp.where(t == INT_MIN, -1, x)
    return t, x


def _dsa_body(qa_ref, qi_ref, sm_ref, ka_ref, va_ref, ki_ref, mka_ref, mva_ref, mki_ref,
              o_ref, smeta_ref, smain_ref, wb_ref, tsel_ref, m_ref, acc_ref, p_ref, *, k_top):
    i = pl.program_id(1)
    row_m = lax.broadcasted_iota(I32, (TQ, LANES), 0)
    col_m = lax.broadcasted_iota(I32, (TQ, LANES), 1)
    row_k = lax.broadcasted_iota(I32, (TQ, TQ), 0)
    col_k = lax.broadcasted_iota(I32, (TQ, TQ), 1)

    qi_all = qi_ref[...].reshape(H_I * TQ, D_IDX)
    w = sm_ref[:, SM_WI:SM_WI + H_I]
    for h in range(H_I):
        wb_ref[h] = jnp.broadcast_to(w[:, h:h + 1], (TQ, LANES))

    def score_keys(ki_blk, width):
        s4 = _dot_nt(qi_all, ki_blk)
        sc = None
        for h in range(H_I):
            t = jnp.maximum(s4[h * TQ:(h + 1) * TQ], 0.0) * _wide(wb_ref[h], width)
            sc = t if sc is None else sc + t
        return _to_key(sc)

    smeta_ref[...] = jnp.where(col_m < N_META, score_keys(mki_ref[...], LANES), INT_MIN)

    def score_full(j, carry):
        smain_ref[j] = score_keys(ki_ref[pl.ds(pl.multiple_of(j * TQ, TQ), TQ), :], TQ)
        return carry

    lax.fori_loop(0, i, score_full, 0)
    diag_ok = (col_k // CHUNK) <= (row_k // CHUNK)
    smain_ref[i] = jnp.where(
        diag_ok, score_keys(ki_ref[pl.ds(pl.multiple_of(i * TQ, TQ), TQ), :], TQ), INT_MIN)

    idx_m = lax.broadcasted_iota(I32, (RS, LANES), 1)
    for r0 in range(0, TQ, RS):

        def count_rows(pred, r0=r0, idx_m=idx_m):
            c = jnp.where(pred(smeta_ref[r0:r0 + RS, :], idx_m), 1.0, 0.0)

            def body(j, c):
                for t in range(TQ // LANES):
                    blk = smain_ref[j, r0:r0 + RS, t * LANES:(t + 1) * LANES]
                    idx = idx_m + (LANES + j * TQ + t * LANES)
                    c = c + jnp.where(pred(blk, idx), 1.0, 0.0)
                return c

            c = lax.fori_loop(0, i + 1, body, c)
            return jnp.sum(c, axis=1, keepdims=True)

        t_sel, x_tie = _topk_threshold(count_rows, float(k_top), 12)
        t_b = jnp.broadcast_to(t_sel, (RS, LANES))
        x_b = jnp.broadcast_to(x_tie, (RS, LANES))

        def demote(keys, idx):
            return jnp.where((keys == t_b) & (idx > x_b), INT_MIN, keys)

        smeta_ref[r0:r0 + RS, :] = demote(smeta_ref[r0:r0 + RS, :], idx_m)

        def demote_body(j, carry, r0=r0, idx_m=idx_m, demote=demote):
            for t in range(TQ // LANES):
                blk = smain_ref[j, r0:r0 + RS, t * LANES:(t + 1) * LANES]
                idx = idx_m + (LANES + j * TQ + t * LANES)
                smain_ref[j, r0:r0 + RS, t * LANES:(t + 1) * LANES] = demote(blk, idx)
            return carry

        lax.fori_loop(0, i + 1, demote_body, 0)
        tsel_ref[r0:r0 + RS, :] = jnp.maximum(t_b, INT_MIN + 1)

    q_all = qa_ref[...].reshape(H_A * TQ, DH_A)
    m_ref[...] = jnp.full(m_ref.shape, NEG_INIT, F32)
    acc_ref[...] = jnp.zeros(acc_ref.shape, F32)
    tsel = tsel_ref[...]

    def attend(k_blk, v_blk, keys, dist, width):
        negm = jnp.where(keys >= _wide(tsel, width), 0.0, -jnp.inf)
        s_all = _dot_nt(q_all, k_blk)
        for h in range(H_A):
            rows = slice(h * TQ, (h + 1) * TQ)
            s = s_all[rows] + (negm - SLOPES_A[h] * dist)
            m_old = m_ref[rows]
            m_new = jnp.maximum(m_old, jnp.max(s, axis=1, keepdims=True))
            p = jnp.exp(s - _wide(m_new, width))
            m_ref[rows] = m_new
            p_ref[rows, 0:width] = p.astype(BF16)
            acc_ref[rows] = acc_ref[rows] * jnp.exp(m_old - m_new)
        acc_ref[...] += _dot(p_ref[:, 0:width], v_blk)

    d_meta = (row_m - col_m + N_META + i * TQ).astype(F32)
    attend(mka_ref[...], mva_ref[...], smeta_ref[...], d_meta, LANES)
    d0 = row_k - col_k

    def attend_main(j, carry):
        r = pl.ds(pl.multiple_of(j * TQ, TQ), TQ)
        dist = jnp.abs(d0 + (i - j) * TQ).astype(F32)
        attend(ka_ref[r, :], va_ref[r, :], smain_ref[j], dist, TQ)
        return carry

    lax.fori_loop(0, i + 1, attend_main, 0)

    outs = []
    for h in range(H_A):
        a = acc_ref[h * TQ:(h + 1) * TQ]
        o = a / pltpu.roll(a, DH_A, axis=1)
        outs.append(o[:, :DH_A])
    o_ref[...] = jnp.concatenate(outs, axis=1).astype(BF16)


def _dsa(qa, qi, sm, kab, vaa, kib, mka, mva, mki, batch, seq, k_top):
    nq = seq // TQ
    qrow = lambda b, i: (b * nq + i, 0)
    qhead = lambda b, i: (0, b * nq + i, 0)
    kv = lambda b, i: (b, 0)
    const = lambda b, i: (0, 0)
    return pl.pallas_call(
        functools.partial(_dsa_body, k_top=k_top),
        out_shape=jax.ShapeDtypeStruct((batch * seq, H_A * DH_A), BF16),
        grid=(batch, nq),
        in_specs=[
            pl.BlockSpec((H_A, TQ, DH_A), qhead),
            pl.BlockSpec((H_I, TQ, D_IDX), qhead),
            pl.BlockSpec((TQ, 256), qrow),
            pl.BlockSpec((seq, DH_A), kv),
            pl.BlockSpec((seq, 128), kv),
            pl.BlockSpec((seq, D_IDX), kv),
            pl.BlockSpec((LANES, DH_A), const),
            pl.BlockSpec((LANES, 128), const),
            pl.BlockSpec((LANES, D_IDX), const),
        ],
        out_specs=pl.BlockSpec((TQ, H_A * DH_A), qrow),
        scratch_shapes=[
            pltpu.VMEM((TQ, LANES), I32),
            pltpu.VMEM((nq, TQ, TQ), I32),
            pltpu.VMEM((H_I, TQ, LANES), F32),
            pltpu.VMEM((TQ, LANES), I32),
            pltpu.VMEM((H_A * TQ, LANES), F32),
            pltpu.VMEM((H_A * TQ, LANES), F32),
            pltpu.VMEM((H_A * TQ, TQ), BF16),
        ],
        compiler_params=pltpu.CompilerParams(
            dimension_semantics=("arbitrary", "arbitrary"), vmem_limit_bytes=VMEM_LIMIT),
        name="dsa",
    )(qa, qi, sm, kab, vaa, kib, mka, mva, mki)


def _lambda(lq1_ref, lk1_ref, lq2_ref, lk2_ref):
    s1 = jnp.sum(lq1_ref[...] * lk1_ref[...], axis=-1, keepdims=True)
    s2 = jnp.sum(lq2_ref[...] * lk2_ref[...], axis=-1, keepdims=True)
    return jnp.exp(s1) - jnp.exp(s2) + LAM_INIT


def _diff_body(qb_ref, kbh_ref, vb_ref, mkb_ref, mvb_ref, lq1_ref, lk1_ref, lq2_ref, lk2_ref,
               sg_ref, o_ref, m_ref, l_ref, acc_ref, p_ref):
    i = pl.program_id(1)
    row_m = lax.broadcasted_iota(I32, (TQ, LANES), 0)
    col_m = lax.broadcasted_iota(I32, (TQ, LANES), 1)
    row_k = lax.broadcasted_iota(I32, (TQ, TQ), 0)
    col_k = lax.broadcasted_iota(I32, (TQ, TQ), 1)
    m_ref[...] = jnp.full(m_ref.shape, NEG_INIT, F32)
    l_ref[...] = jnp.zeros(l_ref.shape, F32)
    acc_ref[...] = jnp.zeros(acc_ref.shape, F32)

    def attend(k_of, v_blk, dist, negm, width):
        for h in range(H_B):
            bias = -SLOPES_B[h] * dist
            if negm is not None:
                bias = bias + negm
            for c in range(2):
                hc = 2 * h + c
                s = _dot_nt(qb_ref[hc], k_of(hc)) + bias
                m_old = m_ref[hc]
                m_new = jnp.maximum(m_old, jnp.max(s, axis=1, keepdims=True))
                p = jnp.exp(s - _wide(m_new, width))
                alpha = jnp.exp(m_old - m_new)
                m_ref[hc] = m_new
                psum = p[:, 0:LANES]
                for t in range(1, width // LANES):
                    psum = psum + p[:, t * LANES:(t + 1) * LANES]
                l_ref[hc] = l_ref[hc] * alpha + psum
                acc_ref[hc] = acc_ref[hc] * alpha
                p_ref[c * TQ:(c + 1) * TQ, 0:width] = p.astype(BF16)
            pv = _dot(p_ref[:, 0:width], v_blk[:, h * DV_B:(h + 1) * DV_B])
            acc_ref[2 * h] += pv[0:TQ]
            acc_ref[2 * h + 1] += pv[TQ:2 * TQ]

    d_meta = (row_m - col_m + N_META + i * TQ).astype(F32)
    neg_meta = jnp.where(col_m < N_META, 0.0, -jnp.inf)
    attend(lambda hc: mkb_ref[hc], mvb_ref[...], d_meta, neg_meta, LANES)
    d0 = row_k - col_k

    def attend_full(j, carry):
        r = pl.ds(pl.multiple_of(j * TQ, TQ), TQ)
        dist = (d0 + (i - j) * TQ).astype(F32)
        attend(lambda hc: kbh_ref[hc, r, :], vb_ref[r, :], dist, None, TQ)
        return carry

    lax.fori_loop(0, i, attend_full, 0)
    r = pl.ds(pl.multiple_of(i * TQ, TQ), TQ)
    neg_diag = jnp.where((col_k // CHUNK) <= (row_k // CHUNK), 0.0, -jnp.inf)
    attend(lambda hc: kbh_ref[hc, r, :], vb_ref[r, :], jnp.abs(d0).astype(F32), neg_diag, TQ)

    lam = _lambda(lq1_ref, lk1_ref, lq2_ref, lk2_ref)
    for h in range(H_B):
        l0 = jnp.sum(l_ref[2 * h], axis=1, keepdims=True)
        l1 = jnp.sum(l_ref[2 * h + 1], axis=1, keepdims=True)
        o = acc_ref[2 * h] / l0 - lam * (acc_ref[2 * h + 1] / l1)
        o = _rms(o, sg_ref[...]) * (1.0 - LAM_INIT)
        o_ref[:, h * DV_B:(h + 1) * DV_B] = o.astype(BF16)


def _diff(qb, kbh, vbb, mkb, mvb, lq1, lk1, lq2, lk2, sg, batch, seq):
    nq = seq // TQ
    qrow = lambda b, i: (b * nq + i, 0)
    qhead = lambda b, i: (0, b * nq + i, 0)
    const2 = lambda b, i: (0, 0)
    const3 = lambda b, i: (0, 0, 0)
    return pl.pallas_call(
        _diff_body,
        out_shape=jax.ShapeDtypeStruct((batch * seq, H_B * DV_B), BF16),
        grid=(batch, nq),
        in_specs=[
            pl.BlockSpec((2 * H_B, TQ, DH_B), qhead),
            pl.BlockSpec((2 * H_B, seq, DH_B), lambda b, i: (0, b, 0)),
            pl.BlockSpec((seq, H_B * DV_B), lambda b, i: (b, 0)),
            pl.BlockSpec((2 * H_B, LANES, DH_B), const3),
            pl.BlockSpec((LANES, H_B * DV_B), const2),
            pl.BlockSpec((1, DH_B), const2),
            pl.BlockSpec((1, DH_B), const2),
            pl.BlockSpec((1, DH_B), const2),
            pl.BlockSpec((1, DH_B), const2),
            pl.BlockSpec((1, DV_B), const2),
        ],
        out_specs=pl.BlockSpec((TQ, H_B * DV_B), qrow),
        scratch_shapes=[
            pltpu.VMEM((2 * H_B, TQ, LANES), F32),
            pltpu.VMEM((2 * H_B, TQ, LANES), F32),
            pltpu.VMEM((2 * H_B, TQ, DV_B), F32),
            pltpu.VMEM((2 * TQ, TQ), BF16),
        ],
        compiler_params=pltpu.CompilerParams(
            dimension_semantics=("arbitrary", "arbitrary"), vmem_limit_bytes=VMEM_LIMIT),
        name="diff",
    )(qb, kbh, vbb, mkb, mvb, lq1, lk1, lq2, lk2, sg)


def _sample_body(qa_ref, qi_ref, sm_ref, nka_ref, nva_ref, nki_ref, qb_ref, nkb_ref, nvb_ref,
                 cak_ref, cav_ref, cai_ref, cbk_ref, cbv_ref,
                 lq1_ref, lk1_ref, lq2_ref, lk2_ref, sg_ref,
                 oa_ref, ob_ref, pc_ref, pn_ref, *, k_top, past, ts):
    pad = LANES - ts
    row_c = lax.broadcasted_iota(I32, (ts, past), 0)
    col_c = lax.broadcasted_iota(I32, (ts, past), 1)
    row_n = lax.broadcasted_iota(I32, (ts, LANES), 0)
    col_n = lax.broadcasted_iota(I32, (ts, LANES), 1)
    new_ok = col_n < ts

    def pad_rows(x):
        return jnp.concatenate([x, jnp.zeros((pad,) + x.shape[1:], x.dtype)], axis=0)

    qi_all = qi_ref[...].reshape(H_I * ts, D_IDX)
    w = sm_ref[:, SM_WI:SM_WI + H_I]
    s4c = _dot_nt(qi_all, cai_ref[...].astype(BF16))
    s4n = _dot_nt(qi_all, pad_rows(nki_ref[...]))
    sc_c = None
    sc_n = None
    for h in range(H_I):
        wh = w[:, h:h + 1]
        tc = jnp.maximum(s4c[h * ts:(h + 1) * ts], 0.0) * wh
        tn = jnp.maximum(s4n[h * ts:(h + 1) * ts], 0.0) * wh
        sc_c = tc if sc_c is None else sc_c + tc
        sc_n = tn if sc_n is None else sc_n + tn
    keys_c = _to_key(sc_c)
    keys_n = jnp.where(new_ok, _to_key(sc_n), INT_MIN)

    def count(pred_c, pred_n):
        c = jnp.sum(jnp.where(pred_c, 1.0, 0.0), axis=1, keepdims=True)
        return c + jnp.sum(jnp.where(pred_n, 1.0, 0.0), axis=1, keepdims=True)

    def value_step(s, t):
        cand = t ^ jnp.left_shift(jnp.int32(1), 31 - s)
        cnt = count(keys_c >= cand, keys_n >= cand)
        return jnp.where(cnt >= float(k_top), cand, t)

    t = lax.fori_loop(0, 32, value_step, jnp.full((ts, 1), INT_MIN, I32))
    need = float(k_top) - count(keys_c > t, keys_n > t)
    idx_n = col_n + past

    def tie_step(s, x):
        cand = x | jnp.left_shift(jnp.int32(1), 11 - s)
        cnt = count((keys_c == t) & (col_c < cand), (keys_n == t) & (idx_n < cand))
        return jnp.where(cnt < need, cand, x)

    x = lax.fori_loop(0, 12, tie_step, jnp.zeros((ts, 1), I32))
    x = jnp.where(t == INT_MIN, -1, x)
    negm_c = jnp.where((keys_c > t) | ((keys_c == t) & (col_c <= x)), 0.0, -jnp.inf)
    negm_n = jnp.where((keys_n > t) | ((keys_n == t) & (idx_n <= x)), 0.0, -jnp.inf)

    dist_c = (row_c - col_c + past).astype(F32)
    dist_n = jnp.abs(row_n - col_n).astype(F32)
    q_all = qa_ref[...].reshape(H_A * ts, DH_A)
    s_c = _dot_nt(q_all, cak_ref[...].astype(BF16))
    s_n = _dot_nt(q_all, pad_rows(nka_ref[...]))
    for h in range(H_A):
        rows = slice(h * ts, (h + 1) * ts)
        lc = s_c[rows] + (negm_c - SLOPES_A[h] * dist_c)
        ln = s_n[rows] + (negm_n - SLOPES_A[h] * dist_n)
        m = jnp.maximum(jnp.max(lc, axis=1, keepdims=True), jnp.max(ln, axis=1, keepdims=True))
        m = jnp.maximum(m, NEG_INIT)
        pc_ref[rows, :] = jnp.exp(lc - m).astype(BF16)
        pn_ref[rows, :] = jnp.exp(ln - m).astype(BF16)
    vc_aug = jnp.concatenate([cav_ref[...].astype(BF16), jnp.ones((past, 64), BF16)], axis=1)
    pv = _dot(pc_ref[...], vc_aug) + _dot(pn_ref[...], pad_rows(nva_ref[...]))
    pv = pv / pltpu.roll(pv, DH_A, axis=1)
    oa_ref[...] = jnp.concatenate(
        [pv[h * ts:(h + 1) * ts, :DH_A] for h in range(H_A)], axis=1).astype(BF16)

    lam = _lambda(lq1_ref, lk1_ref, lq2_ref, lk2_ref)
    neg_new = jnp.where(new_ok, 0.0, -jnp.inf)
    for h in range(H_B):
        a_c = None
        a_n = None
        for c in range(2):
            hc = 2 * h + c
            q = qb_ref[hc]
            kc = cbk_ref[:, hc * DH_B:(hc + 1) * DH_B].astype(BF16)
            lc = _dot_nt(q, kc) - SLOPES_B[h] * dist_c
            ln = _dot_nt(q, pad_rows(nkb_ref[hc])) + (neg_new - SLOPES_B[h] * dist_n)
            m = jnp.maximum(jnp.max(lc, axis=1, keepdims=True), jnp.max(ln, axis=1, keepdims=True))
            pc = jnp.exp(lc - m)
            pn = jnp.exp(ln - m)
            l = jnp.sum(pc, axis=1, keepdims=True) + jnp.sum(pn, axis=1, keepdims=True)
            pc = pc / l
            pn = pn / l
            if c == 0:
                a_c, a_n = pc, pn
            else:
                a_c, a_n = a_c - lam * pc, a_n - lam * pn
        vc = cbv_ref[:, h * DV_B:(h + 1) * DV_B].astype(BF16)
        vn = pad_rows(nvb_ref[:, h * DV_B:(h + 1) * DV_B])
        o = _dot(a_c.astype(BF16), vc) + _dot(a_n.astype(BF16), vn)
        o = _rms(o, sg_ref[...]) * (1.0 - LAM_INIT)
        ob_ref[:, h * DV_B:(h + 1) * DV_B] = o.astype(BF16)


def _sample(qa, qi, sm, nka, nva, nki, qb, nkb, nvb, cak, cav, cai, cbk, cbv,
            lq1, lk1, lq2, lk2, sg, batch, ts, past, k_top):
    row = lambda b: (b, 0)
    head = lambda b: (0, b, 0)
    cache = lambda b: (b, 0, 0)
    const = lambda b: (0, 0)
    return pl.pallas_call(
        functools.partial(_sample_body, k_top=k_top, past=past, ts=ts),
        out_shape=(jax.ShapeDtypeStruct((batch * ts, H_A * DH_A), BF16),
                   jax.ShapeDtypeStruct((batch * ts, H_B * DV_B), BF16)),
        grid=(batch,),
        in_specs=[
            pl.BlockSpec((H_A, ts, DH_A), head),
            pl.BlockSpec((H_I, ts, D_IDX), head),
            pl.BlockSpec((ts, 256), row),
            pl.BlockSpec((ts, DH_A), row),
            pl.BlockSpec((ts, 128), row),
            pl.BlockSpec((ts, D_IDX), row),
            pl.BlockSpec((2 * H_B, ts, DH_B), head),
            pl.BlockSpec((2 * H_B, ts, DH_B), head),
            pl.BlockSpec((ts, H_B * DV_B), row),
            pl.BlockSpec((None, past, DH_A), cache),
            pl.BlockSpec((None, past, DH_A), cache),
            pl.BlockSpec((None, past, D_IDX), cache),
            pl.BlockSpec((None, past, 2 * H_B * DH_B), cache),
            pl.BlockSpec((None, past, H_B * DV_B), cache),
            pl.BlockSpec((1, DH_B), const),
            pl.BlockSpec((1, DH_B), const),
            pl.BlockSpec((1, DH_B), const),
            pl.BlockSpec((1, DH_B), const),
            pl.BlockSpec((1, DV_B), const),
        ],
        out_specs=(pl.BlockSpec((ts, H_A * DH_A), row), pl.BlockSpec((ts, H_B * DV_B), row)),
        scratch_shapes=[
            pltpu.VMEM((H_A * ts, past), BF16),
            pltpu.VMEM((H_A * ts, LANES), BF16),
        ],
        compiler_params=pltpu.CompilerParams(
            dimension_semantics=("arbitrary",), vmem_limit_bytes=VMEM_LIMIT),
        name="sample",
    )(qa, qi, sm, nka, nva, nki, qb, nkb, nvb, cak, cav, cai, cbk, cbv, lq1, lk1, lq2, lk2, sg)


def _finish_body(x_ref, oa_ref, ob_ref, wo_ref, gm_ref, wu_ref, wd_ref, gf_ref, y_ref):
    o = jnp.concatenate([oa_ref[...], ob_ref[...]], axis=1)
    h1 = x_ref[...] + _dot(o, wo_ref[...])
    hn = _rms(h1, gm_ref[...]).astype(BF16)
    acc = h1
    for c in range(D_FF // D_MODEL):
        u = jnp.maximum(_dot(hn, wu_ref[:, c * D_MODEL:(c + 1) * D_MODEL]), 0.0)
        acc = acc + _dot((u * u).astype(BF16), wd_ref[c * D_MODEL:(c + 1) * D_MODEL, :])
    y_ref[...] = _rms(acc, gf_ref[...])


def _finish(x, oa, ob, wo, gm, wu, wd, gf, tm):
    m = x.shape[0]
    assert m % tm == 0
    row = lambda i: (i, 0)
    const = lambda i: (0, 0)
    resident = dict(pipeline_mode=pl.Buffered(1))
    return pl.pallas_call(
        _finish_body,
        out_shape=jax.ShapeDtypeStruct((m, D_MODEL), F32),
        grid=(m // tm,),
        in_specs=[
            pl.BlockSpec((tm, D_MODEL), row),
            pl.BlockSpec((tm, 512), row),
            pl.BlockSpec((tm, 512), row),
            pl.BlockSpec((D_MODEL, D_MODEL), const, **resident),
            pl.BlockSpec((1, D_MODEL), const),
            pl.BlockSpec((D_MODEL, D_FF), const, **resident),
            pl.BlockSpec((D_FF, D_MODEL), const, **resident),
            pl.BlockSpec((1, D_MODEL), const),
        ],
        out_specs=pl.BlockSpec((tm, D_MODEL), row),
        compiler_params=pltpu.CompilerParams(
            dimension_semantics=("arbitrary",), vmem_limit_bytes=VMEM_LIMIT),
        name="finish",
    )(x, oa, ob, wo, gm, wu, wd, gf)


def _pad_rows_to(x, rows, axis):
    widths = [(0, 0)] * x.ndim
    widths[axis] = (0, rows - x.shape[axis])
    return jnp.pad(x, widths)


def kernel(x_prompt, x_sample, cache_a_k, cache_a_v, cache_a_idx_k, cache_b_k, cache_b_v,
           meta_tokens, attn_norm_g, w_in, idx_k_norm_g, idx_k_norm_b,
           lambda_q1, lambda_k1, lambda_q2, lambda_k2, subln_g, w_o,
           mlp_norm_g, w_up, w_down, final_norm_g):
    batch, seq, _ = x_prompt.shape
    dec_batch, ts, _ = x_sample.shape
    past = cache_a_k.shape[2]
    assert attn_norm_g.shape[0] == 1, "single-layer step"
    assert seq % TQ == 0 and ts == 16 and past % LANES == 0 and past // CHUNK == (past + ts - 1) // CHUNK
    n = N_META + seq
    k_top_p = min(TOPK_MAX, seq // 4)
    k_top_s = min(TOPK_MAX, (past + ts) // 4)

    w = w_in[0]
    w = jnp.concatenate(
        [w[:, 0:512],
         w[:, 964:1476],
         w[:, 1476:1988],
         w[:, 1988:2500],
         w[:, 640:896],
         w[:, 512:640],
         w[:, 896:960],
         w[:, 960:964],
         jnp.zeros((D_MODEL, W_COLS - 2500), w.dtype)], axis=1).astype(BF16)
    wo = w_o[0].astype(BF16)
    wu = w_up[0].astype(BF16)
    wd = w_down[0].astype(BF16)
    g_attn = attn_norm_g[0][None]
    g_mlp = mlp_norm_g[0][None]
    g_fin = final_norm_g[None]
    kng = idx_k_norm_g[0][None]
    knb = idx_k_norm_b[0][None]
    lq1, lk1, lq2, lk2 = lambda_q1[0][None], lambda_k1[0][None], lambda_q2[0][None], lambda_k2[0][None]
    sg = subln_g[0][None]

    xp = x_prompt.reshape(batch * seq, D_MODEL)
    xs = x_sample.reshape(dec_batch * ts, D_MODEL)

    (qa_p, qi_p, qb_p, kbh_p, kb_p, vb_p, vbb_p, sm_p, kab_p, vaa_p, kib_p) = _proj(
        xp, g_attn, w, kng, knb, 512)
    (_, _, _, kbh_m, kb_m, vb_m, vbb_m, sm_m, kab_m, vaa_m, kib_m) = _proj(
        meta_tokens, g_attn, w, kng, knb, N_META)
    (qa_s, qi_s, qb_s, kbh_s, kb_s, vb_s, vbb_s, sm_s, kab_s, vaa_s, kib_s) = _proj(
        xs, g_attn, w, kng, knb, dec_batch * ts)

    mka, mva, mki = (_pad_rows_to(a, LANES, 0) for a in (kab_m, vaa_m, kib_m))
    oa_p = _dsa(qa_p, qi_p, sm_p, kab_p, vaa_p, kib_p, mka, mva, mki, batch, seq, k_top_p)
    ob_p = _diff(qb_p, kbh_p, vbb_p, _pad_rows_to(kbh_m, LANES, 1), _pad_rows_to(vbb_m, LANES, 0),
                 lq1, lk1, lq2, lk2, sg, batch, seq)
    y_prompt = _finish(xp, oa_p, ob_p, wo, g_mlp, wu, wd, g_fin, 512).reshape(batch, seq, D_MODEL)

    oa_s, ob_s = _sample(
        qa_s, qi_s, sm_s, kab_s, vaa_s, kib_s, qb_s, kbh_s, vbb_s,
        cache_a_k[0], cache_a_v[0], cache_a_idx_k[0],
        cache_b_k[0].reshape(dec_batch, past, 2 * H_B * DH_B),
        cache_b_v[0].reshape(dec_batch, past, H_B * DV_B),
        lq1, lk1, lq2, lk2, sg, dec_batch, ts, past, k_top_s)
    y_sample = _finish(xs, oa_s, ob_s, wo, g_mlp, wu, wd, g_fin, dec_batch * ts).reshape(
        dec_batch, ts, D_MODEL)

    def with_meta(meta_rows, main_rows, width):
        meta_b = jnp.broadcast_to(meta_rows[None], (batch, N_META, width))
        return jnp.concatenate([meta_b, main_rows.reshape(batch, seq, width)], axis=1)[None]

    new_a_k_p = with_meta(sm_m[:, SM_KA:SM_KA + 64], sm_p[:, SM_KA:SM_KA + 64], 64)
    new_a_v_p = with_meta(sm_m[:, SM_VA:SM_VA + 64], sm_p[:, SM_VA:SM_VA + 64], 64)
    new_a_i_p = with_meta(sm_m[:, SM_KI:SM_KI + 64], sm_p[:, SM_KI:SM_KI + 64], 64)
    new_b_k_p = with_meta(kb_m, kb_p, 512).reshape(1, batch, n, H_B, 2, DH_B)
    new_b_v_p = with_meta(vb_m, vb_p, 512).reshape(1, batch, n, H_B, DV_B)
    new_a_k_s = sm_s[:, SM_KA:SM_KA + 64].reshape(1, dec_batch, ts, 64)
    new_a_v_s = sm_s[:, SM_VA:SM_VA + 64].reshape(1, dec_batch, ts, 64)
    new_a_i_s = sm_s[:, SM_KI:SM_KI + 64].reshape(1, dec_batch, ts, 64)
    new_b_k_s = kb_s.reshape(1, dec_batch, ts, H_B, 2, DH_B)
    new_b_v_s = vb_s.reshape(1, dec_batch, ts, H_B, DV_B)
    return (y_prompt, y_sample, new_a_k_p, new_a_v_p, new_a_i_p, new_b_k_p, new_b_v_p,
            new_a_k_s, new_a_v_s, new_a_i_s, new_b_k_s, new_b_v_s)
```

```python
import functools

import jax
import jax.numpy as jnp
from jax import lax
from jax.experimental import pallas as pl
from jax.experimental.pallas import tpu as pltpu

F32 = jnp.float32
BF16 = jnp.bfloat16
I32 = jnp.int32

D_MODEL = 1024
CHUNK = 64
N_META = 16
H_A = 8
DH_A = 64
H_I = 4
D_IDX = 64
TOPK_MAX = 256
H_B = 4
DH_B = 64
DV_B = 2 * DH_B
D_FF = 4 * D_MODEL
EPS = 1e-6
LAM_INIT = 0.2

LANES = 128
C_QA, C_QB, C_KB, C_VB, C_QI, C_SM = 0, 512, 1024, 1536, 2048, 2304
W_COLS = 2560
SM_KA, SM_VA, SM_KI, SM_WI = 0, 64, 128, 192

Q_SCALE = DH_A ** -0.5
W_SCALE = (H_I ** -0.5) * (D_IDX ** -0.5)
SLOPES_A = tuple(2.0 ** (-8.0 * (i + 1) / H_A) for i in range(H_A))
SLOPES_B = tuple(2.0 ** (-8.0 * (i + 1) / H_B) for i in range(H_B))

INT_MIN = -(2 ** 31)
NEG_INIT = -1e30
TQ = 256
CNT_ROWS = 32
VMEM_LIMIT = 56 * 1024 * 1024

_NT = (((1,), (1,)), ((), ()))


def _dot(a, b):
    return jnp.dot(a, b, preferred_element_type=F32)


def _dot_nt(a, b):
    return lax.dot_general(a, b, _NT, preferred_element_type=F32)


def _rms(x, g):
    return (x * lax.rsqrt(jnp.mean(x * x, axis=-1, keepdims=True) + EPS)) * g


def _to_key(score):
    b = lax.bitcast_convert_type(score, I32)
    return b ^ ((b >> 31) & jnp.int32(0x7FFFFFFF))


def _wide(x, width):
    reps = width // LANES
    return x if reps == 1 else jnp.concatenate([x] * reps, axis=1)


def _proj_body(x_ref, g_ref, w_ref, kng_ref, knb_ref,
               qa_ref, qi_ref, qb_ref, kbh_ref, kb_ref, vb_ref, vbb_ref,
               sm_ref, kab_ref, vaa_ref, kib_ref, *maybe_t_refs, pos0, period):
    x = x_ref[...]
    hn = _rms(x, g_ref[...]).astype(BF16)
    rows = x.shape[0]

    def mm(lo, width):
        return _dot(hn, w_ref[:, lo:lo + width])

    lane = lax.broadcasted_iota(I32, (rows, 64), 1)
    z = mm(C_QA, 512) * Q_SCALE
    for h in range(H_A):
        slope_cols = jnp.where(lane < 2, SLOPES_A[h], 0.0).astype(BF16)
        qa_ref[h] = jnp.concatenate([z[:, h * DH_A:(h + 1) * DH_A].astype(BF16), slope_cols], axis=1)
    z = mm(C_QB, 512) * Q_SCALE
    for hc in range(2 * H_B):
        qb_ref[hc] = z[:, hc * DH_B:(hc + 1) * DH_B].astype(BF16)
    z = mm(C_KB, 512)
    kb_ref[...] = z
    for hc in range(2 * H_B):
        kbh_ref[hc] = z[:, hc * DH_B:(hc + 1) * DH_B].astype(BF16)
    z = mm(C_VB, 512)
    vb_ref[...] = z
    vbb_ref[...] = z.astype(BF16)
    z = mm(C_QI, 256)
    for h in range(H_I):
        qi_ref[h] = z[:, h * D_IDX:(h + 1) * D_IDX].astype(BF16)
    z = mm(C_SM, 256)
    ka = z[:, SM_KA:SM_KA + 64]
    va = z[:, SM_VA:SM_VA + 64]
    ki = z[:, SM_KI:SM_KI + 64]
    xc = ki - jnp.mean(ki, axis=-1, keepdims=True)
    ki = xc * lax.rsqrt(jnp.mean(xc * xc, axis=-1, keepdims=True) + EPS) * kng_ref[...] + knb_ref[...]
    sm_ref[:, 0:128] = z[:, 0:128]
    sm_ref[:, SM_KI:SM_KI + 64] = ki
    sm_ref[:, SM_WI:SM_WI + 64] = z[:, SM_WI:SM_WI + 64] * W_SCALE
    r = pl.program_id(0) * rows + lax.broadcasted_iota(I32, (rows, 64), 0)
    pos = pos0 + r % period
    pos_cols = jnp.where(lane == 0, (pos // 256) * 256, jnp.where(lane == 1, pos % 256, 0))
    kab_ref[...] = jnp.concatenate([ka.astype(BF16), pos_cols.astype(F32).astype(BF16)], axis=1)
    kib_ref[...] = ki.astype(BF16)
    vaa_ref[...] = jnp.concatenate([va.astype(BF16), jnp.ones((rows, 64), BF16)], axis=1)
    if maybe_t_refs:
        vt_ref, wt_ref = maybe_t_refs
        kv_t = z[:, 0:128].T
        for c in range(rows // TQ):
            vt_ref[c] = jnp.concatenate(
                [kv_t[64:128, c * TQ:(c + 1) * TQ].astype(BF16), jnp.ones((64, TQ), BF16)], axis=0)
        wt_ref[...] = z[:, 128:256].T[64:72, :] * W_SCALE


def _proj(x, g, w, kng, knb, tm, pos0, period, key_major):
    m = x.shape[0]
    assert m % tm == 0
    row = lambda i: (i, 0)
    head = lambda i: (0, i, 0)
    const = lambda i: (0, 0)
    out_shape = (
        jax.ShapeDtypeStruct((H_A, m, 128), BF16),
        jax.ShapeDtypeStruct((H_I, m, D_IDX), BF16),
        jax.ShapeDtypeStruct((2 * H_B, m, DH_B), BF16),
        jax.ShapeDtypeStruct((2 * H_B, m, DH_B), BF16),
        jax.ShapeDtypeStruct((m, 512), F32),
        jax.ShapeDtypeStruct((m, 512), F32),
        jax.ShapeDtypeStruct((m, 512), BF16),
        jax.ShapeDtypeStruct((m, 256), F32),
        jax.ShapeDtypeStruct((m, 128), BF16),
        jax.ShapeDtypeStruct((m, 128), BF16),
        jax.ShapeDtypeStruct((m, 64), BF16),
    )
    out_specs = (
        pl.BlockSpec((H_A, tm, 128), head),
        pl.BlockSpec((H_I, tm, D_IDX), head),
        pl.BlockSpec((2 * H_B, tm, DH_B), head),
        pl.BlockSpec((2 * H_B, tm, DH_B), head),
        pl.BlockSpec((tm, 512), row),
        pl.BlockSpec((tm, 512), row),
        pl.BlockSpec((tm, 512), row),
        pl.BlockSpec((tm, 256), row),
        pl.BlockSpec((tm, 128), row),
        pl.BlockSpec((tm, 128), row),
        pl.BlockSpec((tm, 64), row),
    )
    if key_major:
        assert tm % TQ == 0
        out_shape += (
            jax.ShapeDtypeStruct((m // TQ, 128, TQ), BF16),
            jax.ShapeDtypeStruct((8, m), F32),
        )
        out_specs += (
            pl.BlockSpec((tm // TQ, 128, TQ), lambda i: (i, 0, 0)),
            pl.BlockSpec((8, tm), lambda i: (0, i)),
        )
    return pl.pallas_call(
        functools.partial(_proj_body, pos0=pos0, period=period),
        out_shape=out_shape,
        grid=(m // tm,),
        in_specs=[
            pl.BlockSpec((tm, D_MODEL), row),
            pl.BlockSpec((1, D_MODEL), const),
            pl.BlockSpec((D_MODEL, W_COLS), const),
            pl.BlockSpec((1, D_IDX), const),
            pl.BlockSpec((1, D_IDX), const),
        ],
        out_specs=out_specs,
        compiler_params=pltpu.CompilerParams(
            dimension_semantics=("arbitrary",), vmem_limit_bytes=VMEM_LIMIT),
        name="proj",
    )(x, g, w, kng, knb)


def _dsa_body(qa_ref, qi_ref, wt_ref, ka_ref, ki_ref, vt_ref, mka_ref, mki_ref, mva_ref,
              o_ref, kmeta_ref, kmain_ref, tsel_ref, m_ref, al_ref, acc_ref, p_ref, *, k_top):
    i = pl.program_id(1)
    krow_m = lax.broadcasted_iota(I32, (LANES, TQ), 0)
    krow = lax.broadcasted_iota(I32, (TQ, TQ), 0)
    qcol = lax.broadcasted_iota(I32, (TQ, TQ), 1)

    def main_rows(j):
        return pl.ds(pl.multiple_of(j * TQ, TQ), TQ)

    qi_all = qi_ref[...].reshape(H_I * TQ, D_IDX)

    def score_keys(ki_blk):
        s4 = _dot_nt(ki_blk, qi_all)
        sc = None
        for h in range(H_I):
            t = jnp.maximum(s4[:, h * TQ:(h + 1) * TQ], 0.0) * wt_ref[h:h + 1, :]
            sc = t if sc is None else sc + t
        return _to_key(sc)

    kmeta_ref[...] = jnp.where(krow_m < N_META, score_keys(mki_ref[...]), INT_MIN)

    def score_full(j, carry):
        kmain_ref[j] = score_keys(ki_ref[main_rows(j), :])
        return carry

    lax.fori_loop(0, i, score_full, 0)
    kmain_ref[i] = jnp.where((krow // CHUNK) <= (qcol // CHUNK),
                             score_keys(ki_ref[main_rows(i), :]), INT_MIN)

    def count(pred):
        def part(hit):
            ones = jnp.where(hit, 1.0, 0.0)
            return jnp.sum(ones.reshape(ones.shape[0] // CNT_ROWS, CNT_ROWS, TQ), axis=0)

        c = part(pred(kmeta_ref[...], lambda: krow_m))

        def body(j, c):
            return c + part(pred(kmain_ref[j], lambda: krow + (LANES + j * TQ)))

        c = lax.fori_loop(0, i + 1, body, c)
        return jnp.sum(c, axis=0, keepdims=True)

    kf = float(k_top)

    def value_step(s, t):
        cand = t ^ jnp.left_shift(jnp.int32(1), 31 - s)
        return jnp.where(count(lambda k, idx: k >= cand) >= kf, cand, t)

    t_sel = lax.fori_loop(0, 32, value_step, jnp.full((1, TQ), INT_MIN, I32))
    excess = (count(lambda k, idx: k >= t_sel) > kf) & (t_sel > INT_MIN)

    @pl.when(jnp.max(jnp.where(excess, 1.0, 0.0)) > 0.5)
    def _demote_excess_ties():
        need = kf - count(lambda k, idx: k > t_sel)

        def tie_step(s, x):
            cand = x | jnp.left_shift(jnp.int32(1), 11 - s)
            cnt = count(lambda k, idx: (k == t_sel) & (idx() < cand))
            return jnp.where(cnt < need, cand, x)

        x_tie = lax.fori_loop(0, 12, tie_step, jnp.zeros((1, TQ), I32))

        def demote(keys, idx):
            return jnp.where((keys == t_sel) & (idx > x_tie), INT_MIN, keys)

        kmeta_ref[...] = demote(kmeta_ref[...], krow_m)

        def demote_body(j, carry):
            kmain_ref[j] = demote(kmain_ref[j], krow + (LANES + j * TQ))
            return carry

        lax.fori_loop(0, i + 1, demote_body, 0)

    tsel_ref[...] = jnp.broadcast_to(jnp.maximum(t_sel, INT_MIN + 1), tsel_ref.shape)

    q_all = qa_ref[...].reshape(H_A * TQ, 128)
    m_ref[...] = jnp.full(m_ref.shape, NEG_INIT, F32)
    acc_ref[...] = jnp.zeros(acc_ref.shape, F32)
    thr = tsel_ref[0:1, :]

    def attend(k_blk, vt_blk, keys, future):
        nk = k_blk.shape[0]
        negm = jnp.where(keys >= thr, 0.0, -jnp.inf)
        s_all = _dot_nt(k_blk, q_all)
        for h in range(H_A):
            cols = slice(h * TQ, (h + 1) * TQ)
            s = s_all[:, cols] + negm
            if future is not None:
                s = s - (2.0 * SLOPES_A[h]) * future
            m_old = m_ref[:, cols]
            m_new = jnp.maximum(m_old, jnp.max(s, axis=0, keepdims=True))
            p_ref[0:nk, cols] = jnp.exp(s - m_new).astype(BF16)
            al_ref[:, cols] = jnp.exp(m_old - m_new)
            m_ref[:, cols] = m_new
        acc_ref[...] = acc_ref[...] * al_ref[...] + _dot(vt_blk, p_ref[0:nk, :])

    attend(mka_ref[...], mva_ref[...].astype(F32).T.astype(BF16), kmeta_ref[...], None)

    def attend_full(j, carry):
        attend(ka_ref[main_rows(j), :], vt_ref[j], kmain_ref[j], None)
        return carry

    lax.fori_loop(0, i, attend_full, 0)
    attend(ka_ref[main_rows(i), :], vt_ref[i], kmain_ref[i],
           jnp.maximum(krow - qcol, 0).astype(F32))

    outs = []
    for h in range(H_A):
        a = acc_ref[:, h * TQ:(h + 1) * TQ].T
        o = a / pltpu.roll(a, DH_A, axis=1)
        outs.append(o[:, :DH_A])
    o_ref[...] = jnp.concatenate(outs, axis=1).astype(BF16)


def _dsa(qa, qi, wt, kab, kib, vt, mka, mki, mva, batch, seq, k_top):
    nq = seq // TQ
    qrow = lambda b, i: (b * nq + i, 0)
    qhead = lambda b, i: (0, b * nq + i, 0)
    kv = lambda b, i: (b, 0)
    const = lambda b, i: (0, 0)
    return pl.pallas_call(
        functools.partial(_dsa_body, k_top=k_top),
        out_shape=jax.ShapeDtypeStruct((batch * seq, H_A * DH_A), BF16),
        grid=(batch, nq),
        in_specs=[
            pl.BlockSpec((H_A, TQ, 128), qhead),
            pl.BlockSpec((H_I, TQ, D_IDX), qhead),
            pl.BlockSpec((8, TQ), lambda b, i: (0, b * nq + i)),
            pl.BlockSpec((seq, 128), kv),
            pl.BlockSpec((seq, D_IDX), kv),
            pl.BlockSpec((nq, 128, TQ), lambda b, i: (b, 0, 0)),
            pl.BlockSpec((LANES, 128), const),
            pl.BlockSpec((LANES, D_IDX), const),
            pl.BlockSpec((LANES, 128), const),
        ],
        out_specs=pl.BlockSpec((TQ, H_A * DH_A), qrow),
        scratch_shapes=[
            pltpu.VMEM((LANES, TQ), I32),
            pltpu.VMEM((nq, TQ, TQ), I32),
            pltpu.VMEM((8, TQ), I32),
            pltpu.VMEM((1, H_A * TQ), F32),
            pltpu.VMEM((1, H_A * TQ), F32),
            pltpu.VMEM((128, H_A * TQ), F32),
            pltpu.VMEM((TQ, H_A * TQ), BF16),
        ],
        compiler_params=pltpu.CompilerParams(
            dimension_semantics=("arbitrary", "arbitrary"), vmem_limit_bytes=VMEM_LIMIT),
        name="dsa",
    )(qa, qi, wt, kab, kib, vt, mka, mki, mva)


def _lambda(lq1_ref, lk1_ref, lq2_ref, lk2_ref):
    s1 = jnp.sum(lq1_ref[...] * lk1_ref[...], axis=-1, keepdims=True)
    s2 = jnp.sum(lq2_ref[...] * lk2_ref[...], axis=-1, keepdims=True)
    return jnp.exp(s1) - jnp.exp(s2) + LAM_INIT


def _diff_body(qb_ref, kbh_ref, vb_ref, mkb_ref, mvb_ref, lq1_ref, lk1_ref, lq2_ref, lk2_ref,
               sg_ref, o_ref, m_ref, l_ref, acc_ref, p_ref):
    i = pl.program_id(1)
    row_m = lax.broadcasted_iota(I32, (TQ, LANES), 0)
    col_m = lax.broadcasted_iota(I32, (TQ, LANES), 1)
    row_k = lax.broadcasted_iota(I32, (TQ, TQ), 0)
    col_k = lax.broadcasted_iota(I32, (TQ, TQ), 1)
    m_ref[...] = jnp.full(m_ref.shape, NEG_INIT, F32)
    l_ref[...] = jnp.zeros(l_ref.shape, F32)
    acc_ref[...] = jnp.zeros(acc_ref.shape, F32)

    def attend(k_of, v_blk, dist, negm, width):
        for h in range(H_B):
            bias = -SLOPES_B[h] * dist
            if negm is not None:
                bias = bias + negm
            for c in range(2):
                hc = 2 * h + c
                s = _dot_nt(qb_ref[hc], k_of(hc)) + bias
                m_old = m_ref[hc]
                m_new = jnp.maximum(m_old, jnp.max(s, axis=1, keepdims=True))
                p = jnp.exp(s - _wide(m_new, width))
                alpha = jnp.exp(m_old - m_new)
                m_ref[hc] = m_new
                psum = p[:, 0:LANES]
                for t in range(1, width // LANES):
                    psum = psum + p[:, t * LANES:(t + 1) * LANES]
                l_ref[hc] = l_ref[hc] * alpha + psum
                acc_ref[hc] = acc_ref[hc] * alpha
                p_ref[c * TQ:(c + 1) * TQ, 0:width] = p.astype(BF16)
            pv = _dot(p_ref[:, 0:width], v_blk[:, h * DV_B:(h + 1) * DV_B])
            acc_ref[2 * h] += pv[0:TQ]
            acc_ref[2 * h + 1] += pv[TQ:2 * TQ]

    d_meta = (row_m - col_m + N_META + i * TQ).astype(F32)
    neg_meta = jnp.where(col_m < N_META, 0.0, -jnp.inf)
    attend(lambda hc: mkb_ref[hc], mvb_ref[...], d_meta, neg_meta, LANES)
    d0 = row_k - col_k

    def attend_full(j, carry):
        r = pl.ds(pl.multiple_of(j * TQ, TQ), TQ)
        dist = (d0 + (i - j) * TQ).astype(F32)
        attend(lambda hc: kbh_ref[hc, r, :], vb_ref[r, :], dist, None, TQ)
        return carry

    lax.fori_loop(0, i, attend_full, 0)
    r = pl.ds(pl.multiple_of(i * TQ, TQ), TQ)
    neg_diag = jnp.where((col_k // CHUNK) <= (row_k // CHUNK), 0.0, -jnp.inf)
    attend(lambda hc: kbh_ref[hc, r, :], vb_ref[r, :], jnp.abs(d0).astype(F32), neg_diag, TQ)

    lam = _lambda(lq1_ref, lk1_ref, lq2_ref, lk2_ref)
    for h in range(H_B):
        l0 = jnp.sum(l_ref[2 * h], axis=1, keepdims=True)
        l1 = jnp.sum(l_ref[2 * h + 1], axis=1, keepdims=True)
        o = acc_ref[2 * h] / l0 - lam * (acc_ref[2 * h + 1] / l1)
        o = _rms(o, sg_ref[...]) * (1.0 - LAM_INIT)
        o_ref[:, h * DV_B:(h + 1) * DV_B] = o.astype(BF16)


def _diff(qb, kbh, vbb, mkb, mvb, lq1, lk1, lq2, lk2, sg, batch, seq):
    nq = seq // TQ
    qrow = lambda b, i: (b * nq + i, 0)
    qhead = lambda b, i: (0, b * nq + i, 0)
    const2 = lambda b, i: (0, 0)
    const3 = lambda b, i: (0, 0, 0)
    return pl.pallas_call(
        _diff_body,
        out_shape=jax.ShapeDtypeStruct((batch * seq, H_B * DV_B), BF16),
        grid=(batch, nq),
        in_specs=[
            pl.BlockSpec((2 * H_B, TQ, DH_B), qhead),
            pl.BlockSpec((2 * H_B, seq, DH_B), lambda b, i: (0, b, 0)),
            pl.BlockSpec((seq, H_B * DV_B), lambda b, i: (b, 0)),
            pl.BlockSpec((2 * H_B, LANES, DH_B), const3),
            pl.BlockSpec((LANES, H_B * DV_B), const2),
            pl.BlockSpec((1, DH_B), const2),
            pl.BlockSpec((1, DH_B), const2),
            pl.BlockSpec((1, DH_B), const2),
            pl.BlockSpec((1, DH_B), const2),
            pl.BlockSpec((1, DV_B), const2),
        ],
        out_specs=pl.BlockSpec((TQ, H_B * DV_B), qrow),
        scratch_shapes=[
            pltpu.VMEM((2 * H_B, TQ, LANES), F32),
            pltpu.VMEM((2 * H_B, TQ, LANES), F32),
            pltpu.VMEM((2 * H_B, TQ, DV_B), F32),
            pltpu.VMEM((2 * TQ, TQ), BF16),
        ],
        compiler_params=pltpu.CompilerParams(
            dimension_semantics=("arbitrary", "arbitrary"), vmem_limit_bytes=VMEM_LIMIT),
        name="diff",
    )(qb, kbh, vbb, mkb, mvb, lq1, lk1, lq2, lk2, sg)


def _sample_body(qa_ref, qi_ref, sm_ref, nka_ref, nva_ref, nki_ref, qb_ref, nkb_ref, nvb_ref,
                 cak_ref, cav_ref, cai_ref, cbk_ref, cbv_ref,
                 lq1_ref, lk1_ref, lq2_ref, lk2_ref, sg_ref,
                 oa_ref, ob_ref, pc_ref, pn_ref, *, k_top, past, ts):
    pad = LANES - ts
    row_c = lax.broadcasted_iota(I32, (ts, past), 0)
    col_c = lax.broadcasted_iota(I32, (ts, past), 1)
    row_n = lax.broadcasted_iota(I32, (ts, LANES), 0)
    col_n = lax.broadcasted_iota(I32, (ts, LANES), 1)
    new_ok = col_n < ts

    def pad_rows(x):
        return jnp.concatenate([x, jnp.zeros((pad,) + x.shape[1:], x.dtype)], axis=0)

    qi_all = qi_ref[...].reshape(H_I * ts, D_IDX)
    w = sm_ref[:, SM_WI:SM_WI + H_I]
    s4c = _dot_nt(qi_all, cai_ref[...].astype(BF16))
    s4n = _dot_nt(qi_all, pad_rows(nki_ref[...]))
    sc_c = None
    sc_n = None
    for h in range(H_I):
        wh = w[:, h:h + 1]
        tc = jnp.maximum(s4c[h * ts:(h + 1) * ts], 0.0) * wh
        tn = jnp.maximum(s4n[h * ts:(h + 1) * ts], 0.0) * wh
        sc_c = tc if sc_c is None else sc_c + tc
        sc_n = tn if sc_n is None else sc_n + tn
    keys_c = _to_key(sc_c)
    keys_n = jnp.where(new_ok, _to_key(sc_n), INT_MIN)

    def count(pred_c, pred_n):
        c = jnp.sum(jnp.where(pred_c, 1.0, 0.0), axis=1, keepdims=True)
        return c + jnp.sum(jnp.where(pred_n, 1.0, 0.0), axis=1, keepdims=True)

    def value_step(s, t):
        cand = t ^ jnp.left_shift(jnp.int32(1), 31 - s)
        cnt = count(keys_c >= cand, keys_n >= cand)
        return jnp.where(cnt >= float(k_top), cand, t)

    t = lax.fori_loop(0, 32, value_step, jnp.full((ts, 1), INT_MIN, I32))
    need = float(k_top) - count(keys_c > t, keys_n > t)
    idx_n = col_n + past

    def tie_step(s, x):
        cand = x | jnp.left_shift(jnp.int32(1), 11 - s)
        cnt = count((keys_c == t) & (col_c < cand), (keys_n == t) & (idx_n < cand))
        return jnp.where(cnt < need, cand, x)

    x = lax.fori_loop(0, 12, tie_step, jnp.zeros((ts, 1), I32))
    x = jnp.where(t == INT_MIN, -1, x)
    negm_c = jnp.where((keys_c > t) | ((keys_c == t) & (col_c <= x)), 0.0, -jnp.inf)
    negm_n = jnp.where((keys_n > t) | ((keys_n == t) & (idx_n <= x)), 0.0, -jnp.inf)

    dist_c = (row_c - col_c + past).astype(F32)
    dist_n = jnp.abs(row_n - col_n).astype(F32)
    q_all = qa_ref[...][:, :, 0:DH_A].reshape(H_A * ts, DH_A)
    s_c = _dot_nt(q_all, cak_ref[...].astype(BF16))
    s_n = _dot_nt(q_all, pad_rows(nka_ref[...][:, 0:DH_A]))
    for h in range(H_A):
        rows = slice(h * ts, (h + 1) * ts)
        lc = s_c[rows] + (negm_c - SLOPES_A[h] * dist_c)
        ln = s_n[rows] + (negm_n - SLOPES_A[h] * dist_n)
        m = jnp.maximum(jnp.max(lc, axis=1, keepdims=True), jnp.max(ln, axis=1, keepdims=True))
        m = jnp.maximum(m, NEG_INIT)
        pc_ref[rows, :] = jnp.exp(lc - m).astype(BF16)
        pn_ref[rows, :] = jnp.exp(ln - m).astype(BF16)
    vc_aug = jnp.concatenate([cav_ref[...].astype(BF16), jnp.ones((past, 64), BF16)], axis=1)
    pv = _dot(pc_ref[...], vc_aug) + _dot(pn_ref[...], pad_rows(nva_ref[...]))
    pv = pv / pltpu.roll(pv, DH_A, axis=1)
    oa_ref[...] = jnp.concatenate(
        [pv[h * ts:(h + 1) * ts, :DH_A] for h in range(H_A)], axis=1).astype(BF16)

    lam = _lambda(lq1_ref, lk1_ref, lq2_ref, lk2_ref)
    neg_new = jnp.where(new_ok, 0.0, -jnp.inf)
    for h in range(H_B):
        a_c = None
        a_n = None
        for c in range(2):
            hc = 2 * h + c
            q = qb_ref[hc]
            kc = cbk_ref[:, hc * DH_B:(hc + 1) * DH_B].astype(BF16)
            lc = _dot_nt(q, kc) - SLOPES_B[h] * dist_c
            ln = _dot_nt(q, pad_rows(nkb_ref[hc])) + (neg_new - SLOPES_B[h] * dist_n)
            m = jnp.maximum(jnp.max(lc, axis=1, keepdims=True), jnp.max(ln, axis=1, keepdims=True))
            pc = jnp.exp(lc - m)
            pn = jnp.exp(ln - m)
            l = jnp.sum(pc, axis=1, keepdims=True) + jnp.sum(pn, axis=1, keepdims=True)
            pc = pc / l
            pn = pn / l
            if c == 0:
                a_c, a_n = pc, pn
            else:
                a_c, a_n = a_c - lam * pc, a_n - lam * pn
        vc = cbv_ref[:, h * DV_B:(h + 1) * DV_B].astype(BF16)
        vn = pad_rows(nvb_ref[:, h * DV_B:(h + 1) * DV_B])
        o = _dot(a_c.astype(BF16), vc) + _dot(a_n.astype(BF16), vn)
        o = _rms(o, sg_ref[...]) * (1.0 - LAM_INIT)
        ob_ref[:, h * DV_B:(h + 1) * DV_B] = o.astype(BF16)


def _sample(qa, qi, sm, nka, nva, nki, qb, nkb, nvb, cak, cav, cai, cbk, cbv,
            lq1, lk1, lq2, lk2, sg, batch, ts, past, k_top):
    row = lambda b: (b, 0)
    head = lambda b: (0, b, 0)
    cache = lambda b: (b, 0, 0)
    const = lambda b: (0, 0)
    return pl.pallas_call(
        functools.partial(_sample_body, k_top=k_top, past=past, ts=ts),
        out_shape=(jax.ShapeDtypeStruct((batch * ts, H_A * DH_A), BF16),
                   jax.ShapeDtypeStruct((batch * ts, H_B * DV_B), BF16)),
        grid=(batch,),
        in_specs=[
            pl.BlockSpec((H_A, ts, 128), head),
            pl.BlockSpec((H_I, ts, D_IDX), head),
            pl.BlockSpec((ts, 256), row),
            pl.BlockSpec((ts, 128), row),
            pl.BlockSpec((ts, 128), row),
            pl.BlockSpec((ts, D_IDX), row),
            pl.BlockSpec((2 * H_B, ts, DH_B), head),
            pl.BlockSpec((2 * H_B, ts, DH_B), head),
            pl.BlockSpec((ts, H_B * DV_B), row),
            pl.BlockSpec((None, past, DH_A), cache),
            pl.BlockSpec((None, past, DH_A), cache),
            pl.BlockSpec((None, past, D_IDX), cache),
            pl.BlockSpec((None, past, 2 * H_B * DH_B), cache),
            pl.BlockSpec((None, past, H_B * DV_B), cache),
            pl.BlockSpec((1, DH_B), const),
            pl.BlockSpec((1, DH_B), const),
            pl.BlockSpec((1, DH_B), const),
            pl.BlockSpec((1, DH_B), const),
            pl.BlockSpec((1, DV_B), const),
        ],
        out_specs=(pl.BlockSpec((ts, H_A * DH_A), row), pl.BlockSpec((ts, H_B * DV_B), row)),
        scratch_shapes=[
            pltpu.VMEM((H_A * ts, past), BF16),
            pltpu.VMEM((H_A * ts, LANES), BF16),
        ],
        compiler_params=pltpu.CompilerParams(
            dimension_semantics=("arbitrary",), vmem_limit_bytes=VMEM_LIMIT),
        name="sample",
    )(qa, qi, sm, nka, nva, nki, qb, nkb, nvb, cak, cav, cai, cbk, cbv, lq1, lk1, lq2, lk2, sg)


def _finish_body(x_ref, oa_ref, ob_ref, wo_ref, gm_ref, wu_ref, wd_ref, gf_ref, y_ref):
    o = jnp.concatenate([oa_ref[...], ob_ref[...]], axis=1)
    h1 = x_ref[...] + _dot(o, wo_ref[...])
    hn = _rms(h1, gm_ref[...]).astype(BF16)
    acc = h1
    for c in range(D_FF // D_MODEL):
        u = jnp.maximum(_dot(hn, wu_ref[:, c * D_MODEL:(c + 1) * D_MODEL]), 0.0)
        acc = acc + _dot((u * u).astype(BF16), wd_ref[c * D_MODEL:(c + 1) * D_MODEL, :])
    y_ref[...] = _rms(acc, gf_ref[...])


def _finish(x, oa, ob, wo, gm, wu, wd, gf, tm):
    m = x.shape[0]
    assert m % tm == 0
    row = lambda i: (i, 0)
    const = lambda i: (0, 0)
    resident = dict(pipeline_mode=pl.Buffered(1))
    return pl.pallas_call(
        _finish_body,
        out_shape=jax.ShapeDtypeStruct((m, D_MODEL), F32),
        grid=(m // tm,),
        in_specs=[
            pl.BlockSpec((tm, D_MODEL), row),
            pl.BlockSpec((tm, 512), row),
            pl.BlockSpec((tm, 512), row),
            pl.BlockSpec((D_MODEL, D_MODEL), const, **resident),
            pl.BlockSpec((1, D_MODEL), const),
            pl.BlockSpec((D_MODEL, D_FF), const, **resident),
            pl.BlockSpec((D_FF, D_MODEL), const, **resident),
            pl.BlockSpec((1, D_MODEL), const),
        ],
        out_specs=pl.BlockSpec((tm, D_MODEL), row),
        compiler_params=pltpu.CompilerParams(
            dimension_semantics=("arbitrary",), vmem_limit_bytes=VMEM_LIMIT),
        name="finish",
    )(x, oa, ob, wo, gm, wu, wd, gf)


def _pad_rows_to(x, rows, axis):
    widths = [(0, 0)] * x.ndim
    widths[axis] = (0, rows - x.shape[axis])
    return jnp.pad(x, widths)


def kernel(x_prompt, x_sample, cache_a_k, cache_a_v, cache_a_idx_k, cache_b_k, cache_b_v,
           meta_tokens, attn_norm_g, w_in, idx_k_norm_g, idx_k_norm_b,
           lambda_q1, lambda_k1, lambda_q2, lambda_k2, subln_g, w_o,
           mlp_norm_g, w_up, w_down, final_norm_g):
    batch, seq, _ = x_prompt.shape
    dec_batch, ts, _ = x_sample.shape
    past = cache_a_k.shape[2]
    assert attn_norm_g.shape[0] == 1, "single-layer step"
    assert seq % TQ == 0 and ts == 16 and past % LANES == 0 and past // CHUNK == (past + ts - 1) // CHUNK
    n = N_META + seq
    k_top_p = min(TOPK_MAX, seq // 4)
    k_top_s = min(TOPK_MAX, (past + ts) // 4)

    w = w_in[0]
    w = jnp.concatenate(
        [w[:, 0:512],
         w[:, 964:1476],
         w[:, 1476:1988],
         w[:, 1988:2500],
         w[:, 640:896],
         w[:, 512:640],
         w[:, 896:960],
         w[:, 960:964],
         jnp.zeros((D_MODEL, W_COLS - 2500), w.dtype)], axis=1).astype(BF16)
    wo = w_o[0].astype(BF16)
    wu = w_up[0].astype(BF16)
    wd = w_down[0].astype(BF16)
    g_attn = attn_norm_g[0][None]
    g_mlp = mlp_norm_g[0][None]
    g_fin = final_norm_g[None]
    kng = idx_k_norm_g[0][None]
    knb = idx_k_norm_b[0][None]
    lq1, lk1, lq2, lk2 = lambda_q1[0][None], lambda_k1[0][None], lambda_q2[0][None], lambda_k2[0][None]
    sg = subln_g[0][None]

    xp = x_prompt.reshape(batch * seq, D_MODEL)
    xs = x_sample.reshape(dec_batch * ts, D_MODEL)

    (qa_p, qi_p, qb_p, kbh_p, kb_p, vb_p, vbb_p, sm_p, kab_p, _, kib_p, vt_p, wt_p) = _proj(
        xp, g_attn, w, kng, knb, 512, N_META, seq, True)
    (_, _, _, kbh_m, kb_m, vb_m, vbb_m, sm_m, kab_m, vaa_m, kib_m) = _proj(
        meta_tokens, g_attn, w, kng, knb, N_META, 0, N_META, False)
    (qa_s, qi_s, qb_s, kbh_s, kb_s, vb_s, vbb_s, sm_s, kab_s, vaa_s, kib_s) = _proj(
        xs, g_attn, w, kng, knb, dec_batch * ts, past, dec_batch * ts, False)

    mka, mva, mki = (_pad_rows_to(a, LANES, 0) for a in (kab_m, vaa_m, kib_m))
    oa_p = _dsa(qa_p, qi_p, wt_p, kab_p, kib_p, vt_p, mka, mki, mva, batch, seq, k_top_p)
    ob_p = _diff(qb_p, kbh_p, vbb_p, _pad_rows_to(kbh_m, LANES, 1), _pad_rows_to(vbb_m, LANES, 0),
                 lq1, lk1, lq2, lk2, sg, batch, seq)
    y_prompt = _finish(xp, oa_p, ob_p, wo, g_mlp, wu, wd, g_fin, 512).reshape(batch, seq, D_MODEL)

    oa_s, ob_s = _sample(
        qa_s, qi_s, sm_s, kab_s, vaa_s, kib_s, qb_s, kbh_s, vbb_s,
        cache_a_k[0], cache_a_v[0], cache_a_idx_k[0],
        cache_b_k[0].reshape(dec_batch, past, 2 * H_B * DH_B),
        cache_b_v[0].reshape(dec_batch, past, H_B * DV_B),
        lq1, lk1, lq2, lk2, sg, dec_batch, ts, past, k_top_s)
    y_sample = _finish(xs, oa_s, ob_s, wo, g_mlp, wu, wd, g_fin, dec_batch * ts).reshape(
        dec_batch, ts, D_MODEL)

    def with_meta(meta_rows, main_rows, width):
        meta_b = jnp.broadcast_to(meta_rows[None], (batch, N_META, width))
        return jnp.concatenate([meta_b, main_rows.reshape(batch, seq, width)], axis=1)[None]

    new_a_k_p = with_meta(sm_m[:, SM_KA:SM_KA + 64], sm_p[:, SM_KA:SM_KA + 64], 64)
    new_a_v_p = with_meta(sm_m[:, SM_VA:SM_VA + 64], sm_p[:, SM_VA:SM_VA + 64], 64)
    new_a_i_p = with_meta(sm_m[:, SM_KI:SM_KI + 64], sm_p[:, SM_KI:SM_KI + 64], 64)
    new_b_k_p = with_meta(kb_m, kb_p, 512).reshape(1, batch, n, H_B, 2, DH_B)
    new_b_v_p = with_meta(vb_m, vb_p, 512).reshape(1, batch, n, H_B, DV_B)
    new_a_k_s = sm_s[:, SM_KA:SM_KA + 64].reshape(1, dec_batch, ts, 64)
    new_a_v_s = sm_s[:, SM_VA:SM_VA + 64].reshape(1, dec_batch, ts, 64)
    new_a_i_s = sm_s[:, SM_KI:SM_KI + 64].reshape(1, dec_batch, ts, 64)
    new_b_k_s = kb_s.reshape(1, dec_batch, ts, H_B, 2, DH_B)
    new_b_v_s = vb_s.reshape(1, dec_batch, ts, H_B, DV_B)
    return (y_prompt, y_sample, new_a_k_p, new_a_v_p, new_a_i_p, new_b_k_p, new_b_v_p,
            new_a_k_s, new_a_v_s, new_a_i_s, new_b_k_s, new_b_v_s)
```

```python
import functools

import jax
import jax.numpy as jnp
from jax import lax
from jax.experimental import pallas as pl
from jax.experimental.pallas import tpu as pltpu

F32 = jnp.float32
BF16 = jnp.bfloat16
I32 = jnp.int32
I16 = jnp.int16
HALF16 = 32768

D_MODEL = 1024
CHUNK = 64
N_META = 16
H_A = 8
DH_A = 64
H_I = 4
D_IDX = 64
TOPK_MAX = 256
H_B = 4
DH_B = 64
DV_B = 2 * DH_B
D_FF = 4 * D_MODEL
EPS = 1e-6
LAM_INIT = 0.2

LANES = 128
C_QA, C_QB, C_KB, C_VB, C_QI, C_SM = 0, 512, 1024, 1536, 2048, 2304
W_COLS = 2560
SM_KA, SM_VA, SM_KI, SM_WI = 0, 64, 128, 192

Q_SCALE = DH_A ** -0.5
W_SCALE = (H_I ** -0.5) * (D_IDX ** -0.5)
SLOPES_A = tuple(2.0 ** (-8.0 * (i + 1) / H_A) for i in range(H_A))
SLOPES_B = tuple(2.0 ** (-8.0 * (i + 1) / H_B) for i in range(H_B))

INT_MIN = -(2 ** 31)
NEG_INIT = -1e30
TQ = 256
CNT_ROWS = 32
VMEM_LIMIT = 56 * 1024 * 1024

_NT = (((1,), (1,)), ((), ()))


def _dot(a, b):
    return jnp.dot(a, b, preferred_element_type=F32)


def _dot_nt(a, b):
    return lax.dot_general(a, b, _NT, preferred_element_type=F32)


def _rms(x, g):
    return (x * lax.rsqrt(jnp.mean(x * x, axis=-1, keepdims=True) + EPS)) * g


def _to_key(score):
    b = lax.bitcast_convert_type(score, I32)
    return b ^ ((b >> 31) & jnp.int32(0x7FFFFFFF))


def _wide(x, width):
    reps = width // LANES
    return x if reps == 1 else jnp.concatenate([x] * reps, axis=1)


def _proj_body(x_ref, g_ref, w_ref, kng_ref, knb_ref,
               qa_ref, qi_ref, qb_ref, kbh_ref, kb_ref, vb_ref, vbb_ref,
               sm_ref, kab_ref, vaa_ref, kib_ref, *maybe_t_refs, pos0, period):
    x = x_ref[...]
    hn = _rms(x, g_ref[...]).astype(BF16)
    rows = x.shape[0]

    def mm(lo, width):
        return _dot(hn, w_ref[:, lo:lo + width])

    lane = lax.broadcasted_iota(I32, (rows, 64), 1)
    z = mm(C_QA, 512) * Q_SCALE
    for h in range(H_A):
        slope_cols = jnp.where(lane < 2, SLOPES_A[h], 0.0).astype(BF16)
        qa_ref[h] = jnp.concatenate([z[:, h * DH_A:(h + 1) * DH_A].astype(BF16), slope_cols], axis=1)
    z = mm(C_QB, 512) * Q_SCALE
    for hc in range(2 * H_B):
        qb_ref[hc] = z[:, hc * DH_B:(hc + 1) * DH_B].astype(BF16)
    z = mm(C_KB, 512)
    kb_ref[...] = z
    for hc in range(2 * H_B):
        kbh_ref[hc] = z[:, hc * DH_B:(hc + 1) * DH_B].astype(BF16)
    z = mm(C_VB, 512)
    vb_ref[...] = z
    vbb_ref[...] = z.astype(BF16)
    z = mm(C_QI, 256)
    for h in range(H_I):
        qi_ref[h] = z[:, h * D_IDX:(h + 1) * D_IDX].astype(BF16)
    z = mm(C_SM, 256)
    ka = z[:, SM_KA:SM_KA + 64]
    va = z[:, SM_VA:SM_VA + 64]
    ki = z[:, SM_KI:SM_KI + 64]
    xc = ki - jnp.mean(ki, axis=-1, keepdims=True)
    ki = xc * lax.rsqrt(jnp.mean(xc * xc, axis=-1, keepdims=True) + EPS) * kng_ref[...] + knb_ref[...]
    sm_ref[:, 0:128] = z[:, 0:128]
    sm_ref[:, SM_KI:SM_KI + 64] = ki
    sm_ref[:, SM_WI:SM_WI + 64] = z[:, SM_WI:SM_WI + 64] * W_SCALE
    r = pl.program_id(0) * rows + lax.broadcasted_iota(I32, (rows, 64), 0)
    pos = pos0 + r % period
    pos_cols = jnp.where(lane == 0, (pos // 256) * 256, jnp.where(lane == 1, pos % 256, 0))
    kab_ref[...] = jnp.concatenate([ka.astype(BF16), pos_cols.astype(F32).astype(BF16)], axis=1)
    kib_ref[...] = ki.astype(BF16)
    vaa_ref[...] = jnp.concatenate([va.astype(BF16), jnp.ones((rows, 64), BF16)], axis=1)
    if maybe_t_refs:
        vt_ref, wt_ref = maybe_t_refs
        kv_t = z[:, 0:128].T
        for c in range(rows // TQ):
            vt_ref[c] = jnp.concatenate(
                [kv_t[64:128, c * TQ:(c + 1) * TQ].astype(BF16), jnp.ones((64, TQ), BF16)], axis=0)
        wt_ref[...] = z[:, 128:256].T[64:72, :] * W_SCALE


def _proj(x, g, w, kng, knb, tm, pos0, period, key_major):
    m = x.shape[0]
    assert m % tm == 0
    row = lambda i: (i, 0)
    head = lambda i: (0, i, 0)
    const = lambda i: (0, 0)
    out_shape = (
        jax.ShapeDtypeStruct((H_A, m, 128), BF16),
        jax.ShapeDtypeStruct((H_I, m, D_IDX), BF16),
        jax.ShapeDtypeStruct((2 * H_B, m, DH_B), BF16),
        jax.ShapeDtypeStruct((2 * H_B, m, DH_B), BF16),
        jax.ShapeDtypeStruct((m, 512), F32),
        jax.ShapeDtypeStruct((m, 512), F32),
        jax.ShapeDtypeStruct((m, 512), BF16),
        jax.ShapeDtypeStruct((m, 256), F32),
        jax.ShapeDtypeStruct((m, 128), BF16),
        jax.ShapeDtypeStruct((m, 128), BF16),
        jax.ShapeDtypeStruct((m, 64), BF16),
    )
    out_specs = (
        pl.BlockSpec((H_A, tm, 128), head),
        pl.BlockSpec((H_I, tm, D_IDX), head),
        pl.BlockSpec((2 * H_B, tm, DH_B), head),
        pl.BlockSpec((2 * H_B, tm, DH_B), head),
        pl.BlockSpec((tm, 512), row),
        pl.BlockSpec((tm, 512), row),
        pl.BlockSpec((tm, 512), row),
        pl.BlockSpec((tm, 256), row),
        pl.BlockSpec((tm, 128), row),
        pl.BlockSpec((tm, 128), row),
        pl.BlockSpec((tm, 64), row),
    )
    if key_major:
        assert tm % TQ == 0
        out_shape += (
            jax.ShapeDtypeStruct((m // TQ, 128, TQ), BF16),
            jax.ShapeDtypeStruct((8, m), F32),
        )
        out_specs += (
            pl.BlockSpec((tm // TQ, 128, TQ), lambda i: (i, 0, 0)),
            pl.BlockSpec((8, tm), lambda i: (0, i)),
        )
    return pl.pallas_call(
        functools.partial(_proj_body, pos0=pos0, period=period),
        out_shape=out_shape,
        grid=(m // tm,),
        in_specs=[
            pl.BlockSpec((tm, D_MODEL), row),
            pl.BlockSpec((1, D_MODEL), const),
            pl.BlockSpec((D_MODEL, W_COLS), const),
            pl.BlockSpec((1, D_IDX), const),
            pl.BlockSpec((1, D_IDX), const),
        ],
        out_specs=out_specs,
        compiler_params=pltpu.CompilerParams(
            dimension_semantics=("arbitrary",), vmem_limit_bytes=VMEM_LIMIT),
        name="proj",
    )(x, g, w, kng, knb)


def _dsa_body(qa_ref, qi_ref, wt_ref, ka_ref, ki_ref, vt_ref, mka_ref, mki_ref, mva_ref,
              o_ref, kmeta_ref, kmain_ref, hmeta_ref, hmain_ref, lmeta_ref, lmain_ref,
              tsel_ref, m_ref, al_ref, acc_ref, p_ref, *, k_top):
    i = pl.program_id(1)
    krow_m = lax.broadcasted_iota(I32, (N_META, TQ), 0)
    krow = lax.broadcasted_iota(I32, (TQ, TQ), 0)
    qcol = lax.broadcasted_iota(I32, (TQ, TQ), 1)

    def main_rows(j):
        return pl.ds(pl.multiple_of(j * TQ, TQ), TQ)

    qi_all = qi_ref[...].reshape(H_I * TQ, D_IDX)

    def score_keys(ki_blk):
        s4 = _dot_nt(ki_blk, qi_all)
        sc = None
        for h in range(H_I):
            t = jnp.maximum(s4[:, h * TQ:(h + 1) * TQ], 0.0) * wt_ref[h:h + 1, :]
            sc = t if sc is None else sc + t
        return _to_key(sc)

    def halves(keys):
        return (keys >> 16).astype(I16), ((keys & 0xFFFF) - HALF16).astype(I16)

    keys = score_keys(mki_ref[...])
    kmeta_ref[...] = keys
    hmeta_ref[...], lmeta_ref[...] = halves(keys)

    def score_block(j, keys):
        kmain_ref[j] = keys
        hmain_ref[j], lmain_ref[j] = halves(keys)

    def score_full(j, carry):
        score_block(j, score_keys(ki_ref[main_rows(j), :]))
        return carry

    lax.fori_loop(0, i, score_full, 0)
    score_block(i, jnp.where((krow // CHUNK) <= (qcol // CHUNK),
                             score_keys(ki_ref[main_rows(i), :]), INT_MIN))

    def count(pred):
        def part(hit):
            ones = jnp.where(hit, 1.0, 0.0)
            return jnp.sum(ones.reshape(ones.shape[0] // CNT_ROWS, CNT_ROWS, TQ), axis=0)

        def body(j, c):
            return c + part(pred(kmain_ref[j], lambda: krow + (N_META + j * TQ)))

        c = lax.fori_loop(0, i + 1, body, jnp.zeros((CNT_ROWS, TQ), F32))
        cm = jnp.where(pred(kmeta_ref[...], lambda: krow_m), 1.0, 0.0)
        return jnp.sum(c, axis=0, keepdims=True) + jnp.sum(cm, axis=0, keepdims=True)

    def count16(meta_ref, main_ref, pred):
        def part(hit):
            ones = jnp.where(hit, jnp.bfloat16(1), jnp.bfloat16(0))
            acc = ones[0:CNT_ROWS]
            for t in range(1, TQ // CNT_ROWS):
                acc = acc + ones[t * CNT_ROWS:(t + 1) * CNT_ROWS]
            return acc.astype(F32)

        def body(j, c):
            return c + part(pred(main_ref[j]))

        c = lax.fori_loop(0, i + 1, body, jnp.zeros((CNT_ROWS, TQ), F32))
        cm = jnp.where(pred(meta_ref[...]), jnp.bfloat16(1), jnp.bfloat16(0)).astype(F32)
        return jnp.sum(c, axis=0, keepdims=True) + jnp.sum(cm, axis=0, keepdims=True)

    def search16(meta_ref, main_ref, target):
        def step(s, u):
            cand_u = u | jnp.left_shift(jnp.int32(1), 15 - s)
            cand = (cand_u - HALF16).astype(I16)
            cnt = count16(meta_ref, main_ref, lambda k: k >= cand)
            return jnp.where(cnt >= target, cand_u, u)

        return lax.fori_loop(0, 16, step, jnp.zeros((1, TQ), I32))

    kf = float(k_top)
    u_hi = search16(hmeta_ref, hmain_ref, kf)
    t_hi = (u_hi - HALF16).astype(I16)
    need_lo = kf - count16(hmeta_ref, hmain_ref, lambda k: k > t_hi)
    lmeta_ref[...] = jnp.where(hmeta_ref[...] == t_hi, lmeta_ref[...], jnp.int16(-HALF16))

    def bucket_body(j, carry):
        lmain_ref[j] = jnp.where(hmain_ref[j] == t_hi, lmain_ref[j], jnp.int16(-HALF16))
        return carry

    lax.fori_loop(0, i + 1, bucket_body, 0)
    u_lo = search16(lmeta_ref, lmain_ref, need_lo)
    t_sel = jnp.left_shift(u_hi - HALF16, 16) + u_lo
    excess = (count(lambda k, idx: k >= t_sel) > kf) & (t_sel > INT_MIN)

    @pl.when(jnp.max(jnp.where(excess, 1.0, 0.0)) > 0.5)
    def _demote_excess_ties():
        need = kf - count(lambda k, idx: k > t_sel)

        def tie_step(s, x):
            cand = x | jnp.left_shift(jnp.int32(1), 11 - s)
            cnt = count(lambda k, idx: (k == t_sel) & (idx() < cand))
            return jnp.where(cnt < need, cand, x)

        x_tie = lax.fori_loop(0, 12, tie_step, jnp.zeros((1, TQ), I32))

        def demote(keys, idx):
            return jnp.where((keys == t_sel) & (idx > x_tie), INT_MIN, keys)

        kmeta_ref[...] = demote(kmeta_ref[...], krow_m)

        def demote_body(j, carry):
            kmain_ref[j] = demote(kmain_ref[j], krow + (N_META + j * TQ))
            return carry

        lax.fori_loop(0, i + 1, demote_body, 0)

    tsel_ref[...] = jnp.broadcast_to(jnp.maximum(t_sel, INT_MIN + 1), tsel_ref.shape)

    q_all = qa_ref[...].reshape(H_A * TQ, 128)
    m_ref[...] = jnp.full(m_ref.shape, NEG_INIT, F32)
    acc_ref[...] = jnp.zeros(acc_ref.shape, F32)
    thr = tsel_ref[0:1, :]

    def attend(k_blk, vt_blk, keys, future):
        nk = k_blk.shape[0]
        negm = jnp.where(keys >= thr, 0.0, -jnp.inf)
        s_all = _dot_nt(k_blk, q_all)
        for h in range(H_A):
            cols = slice(h * TQ, (h + 1) * TQ)
            s = s_all[:, cols] + negm
            if future is not None:
                s = s - (2.0 * SLOPES_A[h]) * future
            m_old = m_ref[:, cols]
            m_new = jnp.maximum(m_old, jnp.max(s, axis=0, keepdims=True))
            p_ref[0:nk, cols] = jnp.exp(s - m_new).astype(BF16)
            al_ref[:, cols] = jnp.exp(m_old - m_new)
            m_ref[:, cols] = m_new
        acc_ref[...] = acc_ref[...] * al_ref[...] + _dot(vt_blk, p_ref[0:nk, :])

    eye = (lax.broadcasted_iota(I32, (128, 128), 0)
           == lax.broadcasted_iota(I32, (128, 128), 1)).astype(BF16)
    attend(mka_ref[...], _dot_nt(eye, mva_ref[...]).astype(BF16), kmeta_ref[...], None)

    def attend_full(j, carry):
        attend(ka_ref[main_rows(j), :], vt_ref[j], kmain_ref[j], None)
        return carry

    lax.fori_loop(0, i, attend_full, 0)
    attend(ka_ref[main_rows(i), :], vt_ref[i], kmain_ref[i],
           jnp.maximum(krow - qcol, 0).astype(F32))

    outs = []
    for h in range(H_A):
        a = acc_ref[:, h * TQ:(h + 1) * TQ].T
        o = a / pltpu.roll(a, DH_A, axis=1)
        outs.append(o[:, :DH_A])
    o_ref[...] = jnp.concatenate(outs, axis=1).astype(BF16)


def _dsa(qa, qi, wt, kab, kib, vt, mka, mki, mva, batch, seq, k_top):
    nq = seq // TQ
    qrow = lambda b, i: (b * nq + i, 0)
    qhead = lambda b, i: (0, b * nq + i, 0)
    kv = lambda b, i: (b, 0)
    const = lambda b, i: (0, 0)
    return pl.pallas_call(
        functools.partial(_dsa_body, k_top=k_top),
        out_shape=jax.ShapeDtypeStruct((batch * seq, H_A * DH_A), BF16),
        grid=(batch, nq),
        in_specs=[
            pl.BlockSpec((H_A, TQ, 128), qhead),
            pl.BlockSpec((H_I, TQ, D_IDX), qhead),
            pl.BlockSpec((8, TQ), lambda b, i: (0, b * nq + i)),
            pl.BlockSpec((seq, 128), kv),
            pl.BlockSpec((seq, D_IDX), kv),
            pl.BlockSpec((nq, 128, TQ), lambda b, i: (b, 0, 0)),
            pl.BlockSpec((N_META, 128), const),
            pl.BlockSpec((N_META, D_IDX), const),
            pl.BlockSpec((N_META, 128), const),
        ],
        out_specs=pl.BlockSpec((TQ, H_A * DH_A), qrow),
        scratch_shapes=[
            pltpu.VMEM((N_META, TQ), I32),
            pltpu.VMEM((nq, TQ, TQ), I32),
            pltpu.VMEM((N_META, TQ), I16),
            pltpu.VMEM((nq, TQ, TQ), I16),
            pltpu.VMEM((N_META, TQ), I16),
            pltpu.VMEM((nq, TQ, TQ), I16),
            pltpu.VMEM((8, TQ), I32),
            pltpu.VMEM((1, H_A * TQ), F32),
            pltpu.VMEM((1, H_A * TQ), F32),
            pltpu.VMEM((128, H_A * TQ), F32),
            pltpu.VMEM((TQ, H_A * TQ), BF16),
        ],
        compiler_params=pltpu.CompilerParams(
            dimension_semantics=("arbitrary", "arbitrary"), vmem_limit_bytes=VMEM_LIMIT),
        name="dsa",
    )(qa, qi, wt, kab, kib, vt, mka, mki, mva)


def _lambda(lq1_ref, lk1_ref, lq2_ref, lk2_ref):
    s1 = jnp.sum(lq1_ref[...] * lk1_ref[...], axis=-1, keepdims=True)
    s2 = jnp.sum(lq2_ref[...] * lk2_ref[...], axis=-1, keepdims=True)
    return jnp.exp(s1) - jnp.exp(s2) + LAM_INIT


def _diff_body(qb_ref, kbh_ref, vb_ref, mkb_ref, mvb_ref, lq1_ref, lk1_ref, lq2_ref, lk2_ref,
               sg_ref, o_ref, m_ref, l_ref, acc_ref, p_ref):
    i = pl.program_id(1)
    row_m = lax.broadcasted_iota(I32, (TQ, LANES), 0)
    col_m = lax.broadcasted_iota(I32, (TQ, LANES), 1)
    row_k = lax.broadcasted_iota(I32, (TQ, TQ), 0)
    col_k = lax.broadcasted_iota(I32, (TQ, TQ), 1)
    m_ref[...] = jnp.full(m_ref.shape, NEG_INIT, F32)
    l_ref[...] = jnp.zeros(l_ref.shape, F32)
    acc_ref[...] = jnp.zeros(acc_ref.shape, F32)

    def attend(k_of, v_blk, dist, negm, width):
        for h in range(H_B):
            bias = -SLOPES_B[h] * dist
            if negm is not None:
                bias = bias + negm
            for c in range(2):
                hc = 2 * h + c
                s = _dot_nt(qb_ref[hc], k_of(hc)) + bias
                m_old = m_ref[hc]
                m_new = jnp.maximum(m_old, jnp.max(s, axis=1, keepdims=True))
                p = jnp.exp(s - _wide(m_new, width))
                alpha = jnp.exp(m_old - m_new)
                m_ref[hc] = m_new
                psum = p[:, 0:LANES]
                for t in range(1, width // LANES):
                    psum = psum + p[:, t * LANES:(t + 1) * LANES]
                l_ref[hc] = l_ref[hc] * alpha + psum
                acc_ref[hc] = acc_ref[hc] * alpha
                p_ref[c * TQ:(c + 1) * TQ, 0:width] = p.astype(BF16)
            pv = _dot(p_ref[:, 0:width], v_blk[:, h * DV_B:(h + 1) * DV_B])
            acc_ref[2 * h] += pv[0:TQ]
            acc_ref[2 * h + 1] += pv[TQ:2 * TQ]

    d_meta = (row_m - col_m + N_META + i * TQ).astype(F32)
    neg_meta = jnp.where(col_m < N_META, 0.0, -jnp.inf)
    attend(lambda hc: mkb_ref[hc], mvb_ref[...], d_meta, neg_meta, LANES)
    d0 = row_k - col_k

    def attend_full(j, carry):
        r = pl.ds(pl.multiple_of(j * TQ, TQ), TQ)
        dist = (d0 + (i - j) * TQ).astype(F32)
        attend(lambda hc: kbh_ref[hc, r, :], vb_ref[r, :], dist, None, TQ)
        return carry

    lax.fori_loop(0, i, attend_full, 0)
    r = pl.ds(pl.multiple_of(i * TQ, TQ), TQ)
    neg_diag = jnp.where((col_k // CHUNK) <= (row_k // CHUNK), 0.0, -jnp.inf)
    attend(lambda hc: kbh_ref[hc, r, :], vb_ref[r, :], jnp.abs(d0).astype(F32), neg_diag, TQ)

    lam = _lambda(lq1_ref, lk1_ref, lq2_ref, lk2_ref)
    for h in range(H_B):
        l0 = jnp.sum(l_ref[2 * h], axis=1, keepdims=True)
        l1 = jnp.sum(l_ref[2 * h + 1], axis=1, keepdims=True)
        o = acc_ref[2 * h] / l0 - lam * (acc_ref[2 * h + 1] / l1)
        o = _rms(o, sg_ref[...]) * (1.0 - LAM_INIT)
        o_ref[:, h * DV_B:(h + 1) * DV_B] = o.astype(BF16)


def _diff(qb, kbh, vbb, mkb, mvb, lq1, lk1, lq2, lk2, sg, batch, seq):
    nq = seq // TQ
    qrow = lambda b, i: (b * nq + i, 0)
    qhead = lambda b, i: (0, b * nq + i, 0)
    const2 = lambda b, i: (0, 0)
    const3 = lambda b, i: (0, 0, 0)
    return pl.pallas_call(
        _diff_body,
        out_shape=jax.ShapeDtypeStruct((batch * seq, H_B * DV_B), BF16),
        grid=(batch, nq),
        in_specs=[
            pl.BlockSpec((2 * H_B, TQ, DH_B), qhead),
            pl.BlockSpec((2 * H_B, seq, DH_B), lambda b, i: (0, b, 0)),
            pl.BlockSpec((seq, H_B * DV_B), lambda b, i: (b, 0)),
            pl.BlockSpec((2 * H_B, LANES, DH_B), const3),
            pl.BlockSpec((LANES, H_B * DV_B), const2),
            pl.BlockSpec((1, DH_B), const2),
            pl.BlockSpec((1, DH_B), const2),
            pl.BlockSpec((1, DH_B), const2),
            pl.BlockSpec((1, DH_B), const2),
            pl.BlockSpec((1, DV_B), const2),
        ],
        out_specs=pl.BlockSpec((TQ, H_B * DV_B), qrow),
        scratch_shapes=[
            pltpu.VMEM((2 * H_B, TQ, LANES), F32),
            pltpu.VMEM((2 * H_B, TQ, LANES), F32),
            pltpu.VMEM((2 * H_B, TQ, DV_B), F32),
            pltpu.VMEM((2 * TQ, TQ), BF16),
        ],
        compiler_params=pltpu.CompilerParams(
            dimension_semantics=("arbitrary", "arbitrary"), vmem_limit_bytes=VMEM_LIMIT),
        name="diff",
    )(qb, kbh, vbb, mkb, mvb, lq1, lk1, lq2, lk2, sg)


def _sample_body(qa_ref, qi_ref, sm_ref, nka_ref, nva_ref, nki_ref, qb_ref, nkb_ref, nvb_ref,
                 cak_ref, cav_ref, cai_ref, cbk_ref, cbv_ref,
                 lq1_ref, lk1_ref, lq2_ref, lk2_ref, sg_ref,
                 oa_ref, ob_ref, pc_ref, pn_ref, *, k_top, past, ts):
    pad = LANES - ts
    row_c = lax.broadcasted_iota(I32, (ts, past), 0)
    col_c = lax.broadcasted_iota(I32, (ts, past), 1)
    row_n = lax.broadcasted_iota(I32, (ts, LANES), 0)
    col_n = lax.broadcasted_iota(I32, (ts, LANES), 1)
    new_ok = col_n < ts

    def pad_rows(x):
        return jnp.concatenate([x, jnp.zeros((pad,) + x.shape[1:], x.dtype)], axis=0)

    qi_all = qi_ref[...].reshape(H_I * ts, D_IDX)
    w = sm_ref[:, SM_WI:SM_WI + H_I]
    s4c = _dot_nt(qi_all, cai_ref[...].astype(BF16))
    s4n = _dot_nt(qi_all, pad_rows(nki_ref[...]))
    sc_c = None
    sc_n = None
    for h in range(H_I):
        wh = w[:, h:h + 1]
        tc = jnp.maximum(s4c[h * ts:(h + 1) * ts], 0.0) * wh
        tn = jnp.maximum(s4n[h * ts:(h + 1) * ts], 0.0) * wh
        sc_c = tc if sc_c is None else sc_c + tc
        sc_n = tn if sc_n is None else sc_n + tn
    keys_c = _to_key(sc_c)
    keys_n = jnp.where(new_ok, _to_key(sc_n), INT_MIN)

    def count(pred_c, pred_n):
        c = jnp.sum(jnp.where(pred_c, 1.0, 0.0), axis=1, keepdims=True)
        return c + jnp.sum(jnp.where(pred_n, 1.0, 0.0), axis=1, keepdims=True)

    def value_step(s, t):
        cand = t ^ jnp.left_shift(jnp.int32(1), 31 - s)
        cnt = count(keys_c >= cand, keys_n >= cand)
        return jnp.where(cnt >= float(k_top), cand, t)

    t = lax.fori_loop(0, 32, value_step, jnp.full((ts, 1), INT_MIN, I32))
    need = float(k_top) - count(keys_c > t, keys_n > t)
    idx_n = col_n + past

    def tie_step(s, x):
        cand = x | jnp.left_shift(jnp.int32(1), 11 - s)
        cnt = count((keys_c == t) & (col_c < cand), (keys_n == t) & (idx_n < cand))
        return jnp.where(cnt < need, cand, x)

    x = lax.fori_loop(0, 12, tie_step, jnp.zeros((ts, 1), I32))
    x = jnp.where(t == INT_MIN, -1, x)
    negm_c = jnp.where((keys_c > t) | ((keys_c == t) & (col_c <= x)), 0.0, -jnp.inf)
    negm_n = jnp.where((keys_n > t) | ((keys_n == t) & (idx_n <= x)), 0.0, -jnp.inf)

    dist_c = (row_c - col_c + past).astype(F32)
    dist_n = jnp.abs(row_n - col_n).astype(F32)
    q_all = qa_ref[...][:, :, 0:DH_A].reshape(H_A * ts, DH_A)
    s_c = _dot_nt(q_all, cak_ref[...].astype(BF16))
    s_n = _dot_nt(q_all, pad_rows(nka_ref[...][:, 0:DH_A]))
    for h in range(H_A):
        rows = slice(h * ts, (h + 1) * ts)
        lc = s_c[rows] + (negm_c - SLOPES_A[h] * dist_c)
        ln = s_n[rows] + (negm_n - SLOPES_A[h] * dist_n)
        m = jnp.maximum(jnp.max(lc, axis=1, keepdims=True), jnp.max(ln, axis=1, keepdims=True))
        m = jnp.maximum(m, NEG_INIT)
        pc_ref[rows, :] = jnp.exp(lc - m).astype(BF16)
        pn_ref[rows, :] = jnp.exp(ln - m).astype(BF16)
    vc_aug = jnp.concatenate([cav_ref[...].astype(BF16), jnp.ones((past, 64), BF16)], axis=1)
    pv = _dot(pc_ref[...], vc_aug) + _dot(pn_ref[...], pad_rows(nva_ref[...]))
    pv = pv / pltpu.roll(pv, DH_A, axis=1)
    oa_ref[...] = jnp.concatenate(
        [pv[h * ts:(h + 1) * ts, :DH_A] for h in range(H_A)], axis=1).astype(BF16)

    lam = _lambda(lq1_ref, lk1_ref, lq2_ref, lk2_ref)
    neg_new = jnp.where(new_ok, 0.0, -jnp.inf)
    for h in range(H_B):
        a_c = None
        a_n = None
        for c in range(2):
            hc = 2 * h + c
            q = qb_ref[hc]
            kc = cbk_ref[:, hc * DH_B:(hc + 1) * DH_B].astype(BF16)
            lc = _dot_nt(q, kc) - SLOPES_B[h] * dist_c
            ln = _dot_nt(q, pad_rows(nkb_ref[hc])) + (neg_new - SLOPES_B[h] * dist_n)
            m = jnp.maximum(jnp.max(lc, axis=1, keepdims=True), jnp.max(ln, axis=1, keepdims=True))
            pc = jnp.exp(lc - m)
            pn = jnp.exp(ln - m)
            l = jnp.sum(pc, axis=1, keepdims=True) + jnp.sum(pn, axis=1, keepdims=True)
            pc = pc / l
            pn = pn / l
            if c == 0:
                a_c, a_n = pc, pn
            else:
                a_c, a_n = a_c - lam * pc, a_n - lam * pn
        vc = cbv_ref[:, h * DV_B:(h + 1) * DV_B].astype(BF16)
        vn = pad_rows(nvb_ref[:, h * DV_B:(h + 1) * DV_B])
        o = _dot(a_c.astype(BF16), vc) + _dot(a_n.astype(BF16), vn)
        o = _rms(o, sg_ref[...]) * (1.0 - LAM_INIT)
        ob_ref[:, h * DV_B:(h + 1) * DV_B] = o.astype(BF16)


def _sample(qa, qi, sm, nka, nva, nki, qb, nkb, nvb, cak, cav, cai, cbk, cbv,
            lq1, lk1, lq2, lk2, sg, batch, ts, past, k_top):
    row = lambda b: (b, 0)
    head = lambda b: (0, b, 0)
    cache = lambda b: (b, 0, 0)
    const = lambda b: (0, 0)
    return pl.pallas_call(
        functools.partial(_sample_body, k_top=k_top, past=past, ts=ts),
        out_shape=(jax.ShapeDtypeStruct((batch * ts, H_A * DH_A), BF16),
                   jax.ShapeDtypeStruct((batch * ts, H_B * DV_B), BF16)),
        grid=(batch,),
        in_specs=[
            pl.BlockSpec((H_A, ts, 128), head),
            pl.BlockSpec((H_I, ts, D_IDX), head),
            pl.BlockSpec((ts, 256), row),
            pl.BlockSpec((ts, 128), row),
            pl.BlockSpec((ts, 128), row),
            pl.BlockSpec((ts, D_IDX), row),
            pl.BlockSpec((2 * H_B, ts, DH_B), head),
            pl.BlockSpec((2 * H_B, ts, DH_B), head),
            pl.BlockSpec((ts, H_B * DV_B), row),
            pl.BlockSpec((None, past, DH_A), cache),
            pl.BlockSpec((None, past, DH_A), cache),
            pl.BlockSpec((None, past, D_IDX), cache),
            pl.BlockSpec((None, past, 2 * H_B * DH_B), cache),
            pl.BlockSpec((None, past, H_B * DV_B), cache),
            pl.BlockSpec((1, DH_B), const),
            pl.BlockSpec((1, DH_B), const),
            pl.BlockSpec((1, DH_B), const),
            pl.BlockSpec((1, DH_B), const),
            pl.BlockSpec((1, DV_B), const),
        ],
        out_specs=(pl.BlockSpec((ts, H_A * DH_A), row), pl.BlockSpec((ts, H_B * DV_B), row)),
        scratch_shapes=[
            pltpu.VMEM((H_A * ts, past), BF16),
            pltpu.VMEM((H_A * ts, LANES), BF16),
        ],
        compiler_params=pltpu.CompilerParams(
            dimension_semantics=("arbitrary",), vmem_limit_bytes=VMEM_LIMIT),
        name="sample",
    )(qa, qi, sm, nka, nva, nki, qb, nkb, nvb, cak, cav, cai, cbk, cbv, lq1, lk1, lq2, lk2, sg)


def _finish_body(x_ref, oa_ref, ob_ref, wo_ref, gm_ref, wu_ref, wd_ref, gf_ref, y_ref):
    o = jnp.concatenate([oa_ref[...], ob_ref[...]], axis=1)
    h1 = x_ref[...] + _dot(o, wo_ref[...])
    hn = _rms(h1, gm_ref[...]).astype(BF16)
    acc = h1
    for c in range(D_FF // D_MODEL):
        u = jnp.maximum(_dot(hn, wu_ref[:, c * D_MODEL:(c + 1) * D_MODEL]), 0.0)
        acc = acc + _dot((u * u).astype(BF16), wd_ref[c * D_MODEL:(c + 1) * D_MODEL, :])
    y_ref[...] = _rms(acc, gf_ref[...])


def _finish(x, oa, ob, wo, gm, wu, wd, gf, tm):
    m = x.shape[0]
    assert m % tm == 0
    row = lambda i: (i, 0)
    const = lambda i: (0, 0)
    resident = dict(pipeline_mode=pl.Buffered(1))
    return pl.pallas_call(
        _finish_body,
        out_shape=jax.ShapeDtypeStruct((m, D_MODEL), F32),
        grid=(m // tm,),
        in_specs=[
            pl.BlockSpec((tm, D_MODEL), row),
            pl.BlockSpec((tm, 512), row),
            pl.BlockSpec((tm, 512), row),
            pl.BlockSpec((D_MODEL, D_MODEL), const, **resident),
            pl.BlockSpec((1, D_MODEL), const),
            pl.BlockSpec((D_MODEL, D_FF), const, **resident),
            pl.BlockSpec((D_FF, D_MODEL), const, **resident),
            pl.BlockSpec((1, D_MODEL), const),
        ],
        out_specs=pl.BlockSpec((tm, D_MODEL), row),
        compiler_params=pltpu.CompilerParams(
            dimension_semantics=("arbitrary",), vmem_limit_bytes=VMEM_LIMIT),
        name="finish",
    )(x, oa, ob, wo, gm, wu, wd, gf)


def _pad_rows_to(x, rows, axis):
    widths = [(0, 0)] * x.ndim
    widths[axis] = (0, rows - x.shape[axis])
    return jnp.pad(x, widths)


def kernel(x_prompt, x_sample, cache_a_k, cache_a_v, cache_a_idx_k, cache_b_k, cache_b_v,
           meta_tokens, attn_norm_g, w_in, idx_k_norm_g, idx_k_norm_b,
           lambda_q1, lambda_k1, lambda_q2, lambda_k2, subln_g, w_o,
           mlp_norm_g, w_up, w_down, final_norm_g):
    batch, seq, _ = x_prompt.shape
    dec_batch, ts, _ = x_sample.shape
    past = cache_a_k.shape[2]
    assert attn_norm_g.shape[0] == 1, "single-layer step"
    assert seq % TQ == 0 and ts == 16 and past % LANES == 0 and past // CHUNK == (past + ts - 1) // CHUNK
    n = N_META + seq
    k_top_p = min(TOPK_MAX, seq // 4)
    k_top_s = min(TOPK_MAX, (past + ts) // 4)

    w = w_in[0]
    w = jnp.concatenate(
        [w[:, 0:512],
         w[:, 964:1476],
         w[:, 1476:1988],
         w[:, 1988:2500],
         w[:, 640:896],
         w[:, 512:640],
         w[:, 896:960],
         w[:, 960:964],
         jnp.zeros((D_MODEL, W_COLS - 2500), w.dtype)], axis=1).astype(BF16)
    wo = w_o[0].astype(BF16)
    wu = w_up[0].astype(BF16)
    wd = w_down[0].astype(BF16)
    g_attn = attn_norm_g[0][None]
    g_mlp = mlp_norm_g[0][None]
    g_fin = final_norm_g[None]
    kng = idx_k_norm_g[0][None]
    knb = idx_k_norm_b[0][None]
    lq1, lk1, lq2, lk2 = lambda_q1[0][None], lambda_k1[0][None], lambda_q2[0][None], lambda_k2[0][None]
    sg = subln_g[0][None]

    xp = x_prompt.reshape(batch * seq, D_MODEL)
    xs = x_sample.reshape(dec_batch * ts, D_MODEL)

    (qa_p, qi_p, qb_p, kbh_p, kb_p, vb_p, vbb_p, sm_p, kab_p, _, kib_p, vt_p, wt_p) = _proj(
        xp, g_attn, w, kng, knb, 512, N_META, seq, True)
    (_, _, _, kbh_m, kb_m, vb_m, vbb_m, sm_m, kab_m, vaa_m, kib_m) = _proj(
        meta_tokens, g_attn, w, kng, knb, N_META, 0, N_META, False)
    (qa_s, qi_s, qb_s, kbh_s, kb_s, vb_s, vbb_s, sm_s, kab_s, vaa_s, kib_s) = _proj(
        xs, g_attn, w, kng, knb, dec_batch * ts, past, dec_batch * ts, False)

    oa_p = _dsa(qa_p, qi_p, wt_p, kab_p, kib_p, vt_p, kab_m, kib_m, vaa_m, batch, seq, k_top_p)
    ob_p = _diff(qb_p, kbh_p, vbb_p, _pad_rows_to(kbh_m, LANES, 1), _pad_rows_to(vbb_m, LANES, 0),
                 lq1, lk1, lq2, lk2, sg, batch, seq)
    y_prompt = _finish(xp, oa_p, ob_p, wo, g_mlp, wu, wd, g_fin, 512).reshape(batch, seq, D_MODEL)

    oa_s, ob_s = _sample(
        qa_s, qi_s, sm_s, kab_s, vaa_s, kib_s, qb_s, kbh_s, vbb_s,
        cache_a_k[0], cache_a_v[0], cache_a_idx_k[0],
        cache_b_k[0].reshape(dec_batch, past, 2 * H_B * DH_B),
        cache_b_v[0].reshape(dec_batch, past, H_B * DV_B),
        lq1, lk1, lq2, lk2, sg, dec_batch, ts, past, k_top_s)
    y_sample = _finish(xs, oa_s, ob_s, wo, g_mlp, wu, wd, g_fin, dec_batch * ts).reshape(
        dec_batch, ts, D_MODEL)

    def with_meta(meta_rows, main_rows, width):
        meta_b = jnp.broadcast_to(meta_rows[None], (batch, N_META, width))
        return jnp.concatenate([meta_b, main_rows.reshape(batch, seq, width)], axis=1)[None]

    new_a_k_p = with_meta(sm_m[:, SM_KA:SM_KA + 64], sm_p[:, SM_KA:SM_KA + 64], 64)
    new_a_v_p = with_meta(sm_m[:, SM_VA:SM_VA + 64], sm_p[:, SM_VA:SM_VA + 64], 64)
    new_a_i_p = with_meta(sm_m[:, SM_KI:SM_KI + 64], sm_p[:, SM_KI:SM_KI + 64], 64)
    new_b_k_p = with_meta(kb_m, kb_p, 512).reshape(1, batch, n, H_B, 2, DH_B)
    new_b_v_p = with_meta(vb_m, vb_p, 512).reshape(1, batch, n, H_B, DV_B)
    new_a_k_s = sm_s[:, SM_KA:SM_KA + 64].reshape(1, dec_batch, ts, 64)
    new_a_v_s = sm_s[:, SM_VA:SM_VA + 64].reshape(1, dec_batch, ts, 64)
    new_a_i_s = sm_s[:, SM_KI:SM_KI + 64].reshape(1, dec_batch, ts, 64)
    new_b_k_s = kb_s.reshape(1, dec_batch, ts, H_B, 2, DH_B)
    new_b_v_s = vb_s.reshape(1, dec_batch, ts, H_B, DV_B)
    return (y_prompt, y_sample, new_a_k_p, new_a_v_p, new_a_i_p, new_b_k_p, new_b_v_p,
            new_a_k_s, new_a_v_s, new_a_i_s, new_b_k_s, new_b_v_s)
```

```python
import functools

import jax
import jax.numpy as jnp
from jax import lax
from jax.experimental import pallas as pl
from jax.experimental.pallas import tpu as pltpu

F32 = jnp.float32
BF16 = jnp.bfloat16
I32 = jnp.int32
I16 = jnp.int16
HALF16 = 32768

D_MODEL = 1024
CHUNK = 64
N_META = 16
H_A = 8
DH_A = 64
H_I = 4
D_IDX = 64
TOPK_MAX = 256
H_B = 4
DH_B = 64
DV_B = 2 * DH_B
D_FF = 4 * D_MODEL
EPS = 1e-6
LAM_INIT = 0.2

LANES = 128
C_QA, C_QB, C_KB, C_VB, C_QI, C_SM = 0, 512, 1024, 1536, 2048, 2304
W_COLS = 2560
SM_KA, SM_VA, SM_KI, SM_WI = 0, 64, 128, 192

Q_SCALE = DH_A ** -0.5
W_SCALE = (H_I ** -0.5) * (D_IDX ** -0.5)
SLOPES_A = tuple(2.0 ** (-8.0 * (i + 1) / H_A) for i in range(H_A))
SLOPES_B = tuple(2.0 ** (-8.0 * (i + 1) / H_B) for i in range(H_B))

INT_MIN = -(2 ** 31)
NEG_INIT = -1e30
TQ = 256
CNT_ROWS = 32
VMEM_LIMIT = 56 * 1024 * 1024

_NT = (((1,), (1,)), ((), ()))


def _dot(a, b):
    return jnp.dot(a, b, preferred_element_type=F32)


def _dot_nt(a, b):
    return lax.dot_general(a, b, _NT, preferred_element_type=F32)


def _rms(x, g):
    return (x * lax.rsqrt(jnp.mean(x * x, axis=-1, keepdims=True) + EPS)) * g


def _to_key(score):
    b = lax.bitcast_convert_type(score, I32)
    return b ^ ((b >> 31) & jnp.int32(0x7FFFFFFF))


def _f32_from_key(key):
    return lax.bitcast_convert_type(key ^ ((key >> 31) & jnp.int32(0x7FFFFFFF)), F32)


def _bf16_from_key(key):
    bits = key ^ ((key >> 15) & jnp.int32(0x7FFF))
    return lax.bitcast_convert_type(jnp.left_shift(bits, 16), F32).astype(BF16)


KEY16_LOWEST = -32640
KEY32_LOWEST = -2139095040
KEY32_HIGHEST = 2139095039


def _wide(x, width):
    reps = width // LANES
    return x if reps == 1 else jnp.concatenate([x] * reps, axis=1)


def _proj_body(x_ref, g_ref, w_ref, kng_ref, knb_ref,
               qa_ref, qi_ref, qb_ref, kbh_ref, kb_ref, vb_ref, vbb_ref,
               sm_ref, kab_ref, vaa_ref, kib_ref, *maybe_t_refs, pos0, period):
    x = x_ref[...]
    hn = _rms(x, g_ref[...]).astype(BF16)
    rows = x.shape[0]

    def mm(lo, width):
        return _dot(hn, w_ref[:, lo:lo + width])

    lane = lax.broadcasted_iota(I32, (rows, 64), 1)
    z = mm(C_QA, 512) * Q_SCALE
    for h in range(H_A):
        slope_cols = jnp.where(lane < 2, SLOPES_A[h], 0.0).astype(BF16)
        qa_ref[h] = jnp.concatenate([z[:, h * DH_A:(h + 1) * DH_A].astype(BF16), slope_cols], axis=1)
    z = mm(C_QB, 512) * Q_SCALE
    for hc in range(2 * H_B):
        qb_ref[hc] = z[:, hc * DH_B:(hc + 1) * DH_B].astype(BF16)
    z = mm(C_KB, 512)
    kb_ref[...] = z
    for hc in range(2 * H_B):
        kbh_ref[hc] = z[:, hc * DH_B:(hc + 1) * DH_B].astype(BF16)
    z = mm(C_VB, 512)
    vb_ref[...] = z
    vbb_ref[...] = z.astype(BF16)
    z = mm(C_QI, 256)
    for h in range(H_I):
        qi_ref[h] = z[:, h * D_IDX:(h + 1) * D_IDX].astype(BF16)
    z = mm(C_SM, 256)
    ka = z[:, SM_KA:SM_KA + 64]
    va = z[:, SM_VA:SM_VA + 64]
    ki = z[:, SM_KI:SM_KI + 64]
    xc = ki - jnp.mean(ki, axis=-1, keepdims=True)
    ki = xc * lax.rsqrt(jnp.mean(xc * xc, axis=-1, keepdims=True) + EPS) * kng_ref[...] + knb_ref[...]
    sm_ref[:, 0:128] = z[:, 0:128]
    sm_ref[:, SM_KI:SM_KI + 64] = ki
    sm_ref[:, SM_WI:SM_WI + 64] = z[:, SM_WI:SM_WI + 64] * W_SCALE
    r = pl.program_id(0) * rows + lax.broadcasted_iota(I32, (rows, 64), 0)
    pos = pos0 + r % period
    pos_cols = jnp.where(lane == 0, (pos // 256) * 256, jnp.where(lane == 1, pos % 256, 0))
    kab_ref[...] = jnp.concatenate([ka.astype(BF16), pos_cols.astype(F32).astype(BF16)], axis=1)
    kib_ref[...] = ki.astype(BF16)
    vaa_ref[...] = jnp.concatenate([va.astype(BF16), jnp.ones((rows, 64), BF16)], axis=1)
    if maybe_t_refs:
        vt_ref, wt_ref = maybe_t_refs
        kv_t = z[:, 0:128].T
        for c in range(rows // TQ):
            vt_ref[c] = jnp.concatenate(
                [kv_t[64:128, c * TQ:(c + 1) * TQ].astype(BF16), jnp.ones((64, TQ), BF16)], axis=0)
        wt_ref[...] = z[:, 128:256].T[64:72, :] * W_SCALE


def _proj(x, g, w, kng, knb, tm, pos0, period, key_major):
    m = x.shape[0]
    assert m % tm == 0
    row = lambda i: (i, 0)
    head = lambda i: (0, i, 0)
    const = lambda i: (0, 0)
    out_shape = (
        jax.ShapeDtypeStruct((H_A, m, 128), BF16),
        jax.ShapeDtypeStruct((H_I, m, D_IDX), BF16),
        jax.ShapeDtypeStruct((2 * H_B, m, DH_B), BF16),
        jax.ShapeDtypeStruct((2 * H_B, m, DH_B), BF16),
        jax.ShapeDtypeStruct((m, 512), F32),
        jax.ShapeDtypeStruct((m, 512), F32),
        jax.ShapeDtypeStruct((m, 512), BF16),
        jax.ShapeDtypeStruct((m, 256), F32),
        jax.ShapeDtypeStruct((m, 128), BF16),
        jax.ShapeDtypeStruct((m, 128), BF16),
        jax.ShapeDtypeStruct((m, 64), BF16),
    )
    out_specs = (
        pl.BlockSpec((H_A, tm, 128), head),
        pl.BlockSpec((H_I, tm, D_IDX), head),
        pl.BlockSpec((2 * H_B, tm, DH_B), head),
        pl.BlockSpec((2 * H_B, tm, DH_B), head),
        pl.BlockSpec((tm, 512), row),
        pl.BlockSpec((tm, 512), row),
        pl.BlockSpec((tm, 512), row),
        pl.BlockSpec((tm, 256), row),
        pl.BlockSpec((tm, 128), row),
        pl.BlockSpec((tm, 128), row),
        pl.BlockSpec((tm, 64), row),
    )
    if key_major:
        assert tm % TQ == 0
        out_shape += (
            jax.ShapeDtypeStruct((m // TQ, 128, TQ), BF16),
            jax.ShapeDtypeStruct((8, m), F32),
        )
        out_specs += (
            pl.BlockSpec((tm // TQ, 128, TQ), lambda i: (i, 0, 0)),
            pl.BlockSpec((8, tm), lambda i: (0, i)),
        )
    return pl.pallas_call(
        functools.partial(_proj_body, pos0=pos0, period=period),
        out_shape=out_shape,
        grid=(m // tm,),
        in_specs=[
            pl.BlockSpec((tm, D_MODEL), row),
            pl.BlockSpec((1, D_MODEL), const),
            pl.BlockSpec((D_MODEL, W_COLS), const),
            pl.BlockSpec((1, D_IDX), const),
            pl.BlockSpec((1, D_IDX), const),
        ],
        out_specs=out_specs,
        compiler_params=pltpu.CompilerParams(
            dimension_semantics=("arbitrary",), vmem_limit_bytes=VMEM_LIMIT),
        name="proj",
    )(x, g, w, kng, knb)


def _dsa_body(qa_ref, qi_ref, wt_ref, ka_ref, ki_ref, vt_ref, mka_ref, mki_ref, mva_ref,
              o_ref, smeta_ref, smain_ref, bmeta_ref, bmain_ref,
              tsel_ref, m_ref, al_ref, acc_ref, p_ref, *, k_top):
    i = pl.program_id(1)
    krow = lax.broadcasted_iota(I32, (TQ, TQ), 0)
    qcol = lax.broadcasted_iota(I32, (TQ, TQ), 1)

    def main_rows(j):
        return pl.ds(pl.multiple_of(j * TQ, TQ), TQ)

    qi_all = qi_ref[...].reshape(H_I * TQ, D_IDX)

    def scores(ki_blk):
        s4 = _dot_nt(ki_blk, qi_all)
        sc = None
        for h in range(H_I):
            t = jnp.maximum(s4[:, h * TQ:(h + 1) * TQ], 0.0) * wt_ref[h:h + 1, :]
            sc = t if sc is None else sc + t
        return sc

    sc = scores(mki_ref[...])
    smeta_ref[...] = sc
    bmeta_ref[...] = sc.astype(BF16)

    def score_block(j, sc):
        smain_ref[j] = sc
        bmain_ref[j] = sc.astype(BF16)

    def score_full(j, carry):
        score_block(j, scores(ki_ref[main_rows(j), :]))
        return carry

    lax.fori_loop(0, i, score_full, 0)
    score_block(i, jnp.where((krow // CHUNK) <= (qcol // CHUNK),
                             scores(ki_ref[main_rows(i), :]), -jnp.inf))

    def count(meta_ref, main_ref, part, pred):
        def body(j, c):
            return c + part(pred(main_ref[j]))

        c = lax.fori_loop(0, i + 1, body, jnp.zeros((CNT_ROWS, TQ), F32))
        cm = part(pred(meta_ref[...]), rows=N_META)
        return jnp.sum(c, axis=0, keepdims=True) + jnp.sum(cm, axis=0, keepdims=True)

    def part32(hit, rows=CNT_ROWS):
        ones = jnp.where(hit, 1.0, 0.0)
        return jnp.sum(ones.reshape(ones.shape[0] // rows, rows, TQ), axis=0)

    def part16(hit, rows=CNT_ROWS):
        ones = jnp.where(hit, jnp.bfloat16(1), jnp.bfloat16(0))
        acc = ones[0:rows]
        for t in range(1, ones.shape[0] // rows):
            acc = acc + ones[t * rows:(t + 1) * rows]
        return acc.astype(F32)

    kf = float(k_top)

    def step16(s, u):
        cand_u = u | jnp.left_shift(jnp.int32(1), 15 - s)
        cand = _bf16_from_key(jnp.maximum(cand_u, KEY16_LOWEST + HALF16) - HALF16)
        cnt = count(bmeta_ref, bmain_ref, part16, lambda x: x >= cand)
        return jnp.where(cnt >= kf, cand_u, u)

    u_hi = lax.fori_loop(0, 16, step16, jnp.zeros((1, TQ), I32))
    t_hi = _bf16_from_key(jnp.maximum(u_hi, KEY16_LOWEST + HALF16) - HALF16).astype(F32)
    base = jnp.maximum(_to_key(t_hi) - 65536, KEY32_LOWEST)

    def step32(s, d):
        cand_d = d | jnp.left_shift(jnp.int32(1), 16 - s)
        cand = _f32_from_key(jnp.minimum(base + cand_d, KEY32_HIGHEST))
        cnt = count(smeta_ref, smain_ref, part32, lambda x: x >= cand)
        return jnp.where(cnt >= kf, cand_d, d)

    d_lo = lax.fori_loop(0, 17, step32, jnp.zeros((1, TQ), I32))
    t_sel = _f32_from_key(jnp.minimum(base + d_lo, KEY32_HIGHEST))
    need = kf - count(smeta_ref, smain_ref, part32, lambda x: x > t_sel)

    def keep_first_ties(sc, seen):
        nk = sc.shape[0]
        tie = sc == t_sel
        tri = (lax.broadcasted_iota(I32, (nk, nk), 1)
               <= lax.broadcasted_iota(I32, (nk, nk), 0)).astype(BF16)
        rank = _dot(tri, jnp.where(tie, 1.0, 0.0).astype(BF16)) + seen
        kept = jnp.where(tie, jnp.where(rank > need, -jnp.inf, sc), sc)
        return kept, rank[nk - 1:nk, :]

    smeta_ref[...], seen = keep_first_ties(smeta_ref[...], jnp.zeros((1, TQ), F32))

    def ties_body(j, seen):
        smain_ref[j], seen = keep_first_ties(smain_ref[j], seen)
        return seen

    lax.fori_loop(0, i + 1, ties_body, seen)
    tsel_ref[...] = jnp.broadcast_to(t_sel, tsel_ref.shape)

    q_all = qa_ref[...].reshape(H_A * TQ, 128)
    m_ref[...] = jnp.full(m_ref.shape, NEG_INIT, F32)
    acc_ref[...] = jnp.zeros(acc_ref.shape, F32)
    thr = tsel_ref[0:1, :]

    def attend(k_blk, vt_blk, keys, future):
        nk = k_blk.shape[0]
        negm = jnp.where(keys >= thr, 0.0, -jnp.inf)
        s_all = _dot_nt(k_blk, q_all)
        for h in range(H_A):
            cols = slice(h * TQ, (h + 1) * TQ)
            s = s_all[:, cols] + negm
            if future is not None:
                s = s - (2.0 * SLOPES_A[h]) * future
            m_old = m_ref[:, cols]
            m_new = jnp.maximum(m_old, jnp.max(s, axis=0, keepdims=True))
            p_ref[0:nk, cols] = jnp.exp(s - m_new).astype(BF16)
            al_ref[:, cols] = jnp.exp(m_old - m_new)
            m_ref[:, cols] = m_new
        acc_ref[...] = acc_ref[...] * al_ref[...] + _dot(vt_blk, p_ref[0:nk, :])

    eye = (lax.broadcasted_iota(I32, (128, 128), 0)
           == lax.broadcasted_iota(I32, (128, 128), 1)).astype(BF16)
    attend(mka_ref[...], _dot_nt(eye, mva_ref[...]).astype(BF16), smeta_ref[...], None)

    def attend_full(j, carry):
        attend(ka_ref[main_rows(j), :], vt_ref[j], smain_ref[j], None)
        return carry

    lax.fori_loop(0, i, attend_full, 0)
    attend(ka_ref[main_rows(i), :], vt_ref[i], smain_ref[i],
           jnp.maximum(krow - qcol, 0).astype(F32))

    outs = []
    for h in range(H_A):
        a = acc_ref[:, h * TQ:(h + 1) * TQ].T
        o = a / pltpu.roll(a, DH_A, axis=1)
        outs.append(o[:, :DH_A])
    o_ref[...] = jnp.concatenate(outs, axis=1).astype(BF16)


def _dsa(qa, qi, wt, kab, kib, vt, mka, mki, mva, batch, seq, k_top):
    nq = seq // TQ
    qrow = lambda b, i: (b * nq + i, 0)
    qhead = lambda b, i: (0, b * nq + i, 0)
    kv = lambda b, i: (b, 0)
    const = lambda b, i: (0, 0)
    return pl.pallas_call(
        functools.partial(_dsa_body, k_top=k_top),
        out_shape=jax.ShapeDtypeStruct((batch * seq, H_A * DH_A), BF16),
        grid=(batch, nq),
        in_specs=[
            pl.BlockSpec((H_A, TQ, 128), qhead),
            pl.BlockSpec((H_I, TQ, D_IDX), qhead),
            pl.BlockSpec((8, TQ), lambda b, i: (0, b * nq + i)),
            pl.BlockSpec((seq, 128), kv),
            pl.BlockSpec((seq, D_IDX), kv),
            pl.BlockSpec((nq, 128, TQ), lambda b, i: (b, 0, 0)),
            pl.BlockSpec((N_META, 128), const),
            pl.BlockSpec((N_META, D_IDX), const),
            pl.BlockSpec((N_META, 128), const),
        ],
        out_specs=pl.BlockSpec((TQ, H_A * DH_A), qrow),
        scratch_shapes=[
            pltpu.VMEM((N_META, TQ), F32),
            pltpu.VMEM((nq, TQ, TQ), F32),
            pltpu.VMEM((N_META, TQ), BF16),
            pltpu.VMEM((nq, TQ, TQ), BF16),
            pltpu.VMEM((8, TQ), F32),
            pltpu.VMEM((1, H_A * TQ), F32),
            pltpu.VMEM((1, H_A * TQ), F32),
            pltpu.VMEM((128, H_A * TQ), F32),
            pltpu.VMEM((TQ, H_A * TQ), BF16),
        ],
        compiler_params=pltpu.CompilerParams(
            dimension_semantics=("arbitrary", "arbitrary"), vmem_limit_bytes=VMEM_LIMIT),
        name="dsa",
    )(qa, qi, wt, kab, kib, vt, mka, mki, mva)


def _lambda(lq1_ref, lk1_ref, lq2_ref, lk2_ref):
    s1 = jnp.sum(lq1_ref[...] * lk1_ref[...], axis=-1, keepdims=True)
    s2 = jnp.sum(lq2_ref[...] * lk2_ref[...], axis=-1, keepdims=True)
    return jnp.exp(s1) - jnp.exp(s2) + LAM_INIT


def _diff_body(qb_ref, kbh_ref, vb_ref, mkb_ref, mvb_ref, lq1_ref, lk1_ref, lq2_ref, lk2_ref,
               sg_ref, o_ref, m_ref, l_ref, acc_ref, p_ref):
    i = pl.program_id(1)
    row_m = lax.broadcasted_iota(I32, (TQ, LANES), 0)
    col_m = lax.broadcasted_iota(I32, (TQ, LANES), 1)
    row_k = lax.broadcasted_iota(I32, (TQ, TQ), 0)
    col_k = lax.broadcasted_iota(I32, (TQ, TQ), 1)
    m_ref[...] = jnp.full(m_ref.shape, NEG_INIT, F32)
    l_ref[...] = jnp.zeros(l_ref.shape, F32)
    acc_ref[...] = jnp.zeros(acc_ref.shape, F32)

    def attend(k_of, v_blk, dist, negm, width):
        for h in range(H_B):
            bias = -SLOPES_B[h] * dist
            if negm is not None:
                bias = bias + negm
            for c in range(2):
                hc = 2 * h + c
                s = _dot_nt(qb_ref[hc], k_of(hc)) + bias
                m_old = m_ref[hc]
                m_new = jnp.maximum(m_old, jnp.max(s, axis=1, keepdims=True))
                p = jnp.exp(s - _wide(m_new, width))
                alpha = jnp.exp(m_old - m_new)
                m_ref[hc] = m_new
                psum = p[:, 0:LANES]
                for t in range(1, width // LANES):
                    psum = psum + p[:, t * LANES:(t + 1) * LANES]
                l_ref[hc] = l_ref[hc] * alpha + psum
                acc_ref[hc] = acc_ref[hc] * alpha
                p_ref[c * TQ:(c + 1) * TQ, 0:width] = p.astype(BF16)
            pv = _dot(p_ref[:, 0:width], v_blk[:, h * DV_B:(h + 1) * DV_B])
            acc_ref[2 * h] += pv[0:TQ]
            acc_ref[2 * h + 1] += pv[TQ:2 * TQ]

    d_meta = (row_m - col_m + N_META + i * TQ).astype(F32)
    neg_meta = jnp.where(col_m < N_META, 0.0, -jnp.inf)
    attend(lambda hc: mkb_ref[hc], mvb_ref[...], d_meta, neg_meta, LANES)
    d0 = row_k - col_k

    def attend_full(j, carry):
        r = pl.ds(pl.multiple_of(j * TQ, TQ), TQ)
        dist = (d0 + (i - j) * TQ).astype(F32)
        attend(lambda hc: kbh_ref[hc, r, :], vb_ref[r, :], dist, None, TQ)
        return carry

    lax.fori_loop(0, i, attend_full, 0)
    r = pl.ds(pl.multiple_of(i * TQ, TQ), TQ)
    neg_diag = jnp.where((col_k // CHUNK) <= (row_k // CHUNK), 0.0, -jnp.inf)
    attend(lambda hc: kbh_ref[hc, r, :], vb_ref[r, :], jnp.abs(d0).astype(F32), neg_diag, TQ)

    lam = _lambda(lq1_ref, lk1_ref, lq2_ref, lk2_ref)
    for h in range(H_B):
        l0 = jnp.sum(l_ref[2 * h], axis=1, keepdims=True)
        l1 = jnp.sum(l_ref[2 * h + 1], axis=1, keepdims=True)
        o = acc_ref[2 * h] / l0 - lam * (acc_ref[2 * h + 1] / l1)
        o = _rms(o, sg_ref[...]) * (1.0 - LAM_INIT)
        o_ref[:, h * DV_B:(h + 1) * DV_B] = o.astype(BF16)


def _diff(qb, kbh, vbb, mkb, mvb, lq1, lk1, lq2, lk2, sg, batch, seq):
    nq = seq // TQ
    qrow = lambda b, i: (b * nq + i, 0)
    qhead = lambda b, i: (0, b * nq + i, 0)
    const2 = lambda b, i: (0, 0)
    const3 = lambda b, i: (0, 0, 0)
    return pl.pallas_call(
        _diff_body,
        out_shape=jax.ShapeDtypeStruct((batch * seq, H_B * DV_B), BF16),
        grid=(batch, nq),
        in_specs=[
            pl.BlockSpec((2 * H_B, TQ, DH_B), qhead),
            pl.BlockSpec((2 * H_B, seq, DH_B), lambda b, i: (0, b, 0)),
            pl.BlockSpec((seq, H_B * DV_B), lambda b, i: (b, 0)),
            pl.BlockSpec((2 * H_B, LANES, DH_B), const3),
            pl.BlockSpec((LANES, H_B * DV_B), const2),
            pl.BlockSpec((1, DH_B), const2),
            pl.BlockSpec((1, DH_B), const2),
            pl.BlockSpec((1, DH_B), const2),
            pl.BlockSpec((1, DH_B), const2),
            pl.BlockSpec((1, DV_B), const2),
        ],
        out_specs=pl.BlockSpec((TQ, H_B * DV_B), qrow),
        scratch_shapes=[
            pltpu.VMEM((2 * H_B, TQ, LANES), F32),
            pltpu.VMEM((2 * H_B, TQ, LANES), F32),
            pltpu.VMEM((2 * H_B, TQ, DV_B), F32),
            pltpu.VMEM((2 * TQ, TQ), BF16),
        ],
        compiler_params=pltpu.CompilerParams(
            dimension_semantics=("arbitrary", "arbitrary"), vmem_limit_bytes=VMEM_LIMIT),
        name="diff",
    )(qb, kbh, vbb, mkb, mvb, lq1, lk1, lq2, lk2, sg)


def _sample_body(qa_ref, qi_ref, sm_ref, nka_ref, nva_ref, nki_ref, qb_ref, nkb_ref, nvb_ref,
                 cak_ref, cav_ref, cai_ref, cbk_ref, cbv_ref,
                 lq1_ref, lk1_ref, lq2_ref, lk2_ref, sg_ref,
                 oa_ref, ob_ref, pc_ref, pn_ref, *, k_top, past, ts):
    pad = LANES - ts
    row_c = lax.broadcasted_iota(I32, (ts, past), 0)
    col_c = lax.broadcasted_iota(I32, (ts, past), 1)
    row_n = lax.broadcasted_iota(I32, (ts, LANES), 0)
    col_n = lax.broadcasted_iota(I32, (ts, LANES), 1)
    new_ok = col_n < ts

    def pad_rows(x):
        return jnp.concatenate([x, jnp.zeros((pad,) + x.shape[1:], x.dtype)], axis=0)

    qi_all = qi_ref[...].reshape(H_I * ts, D_IDX)
    w = sm_ref[:, SM_WI:SM_WI + H_I]
    s4c = _dot_nt(qi_all, cai_ref[...].astype(BF16))
    s4n = _dot_nt(qi_all, pad_rows(nki_ref[...]))
    sc_c = None
    sc_n = None
    for h in range(H_I):
        wh = w[:, h:h + 1]
        tc = jnp.maximum(s4c[h * ts:(h + 1) * ts], 0.0) * wh
        tn = jnp.maximum(s4n[h * ts:(h + 1) * ts], 0.0) * wh
        sc_c = tc if sc_c is None else sc_c + tc
        sc_n = tn if sc_n is None else sc_n + tn
    sc_n = jnp.where(new_ok, sc_n, -jnp.inf)

    def count(pred_c, pred_n):
        c = jnp.sum(jnp.where(pred_c, 1.0, 0.0), axis=1, keepdims=True)
        return c + jnp.sum(jnp.where(pred_n, 1.0, 0.0), axis=1, keepdims=True)

    def threshold(key):
        return _f32_from_key(jnp.clip(key, KEY32_LOWEST, KEY32_HIGHEST))

    def value_step(s, key):
        cand_key = key ^ jnp.left_shift(jnp.int32(1), 31 - s)
        cand = threshold(cand_key)
        cnt = count(sc_c >= cand, sc_n >= cand)
        return jnp.where(cnt >= float(k_top), cand_key, key)

    t = threshold(lax.fori_loop(0, 32, value_step, jnp.full((ts, 1), INT_MIN, I32)))
    need = float(k_top) - count(sc_c > t, sc_n > t)
    idx_n = col_n + past

    def tie_step(s, x):
        cand = x | jnp.left_shift(jnp.int32(1), 11 - s)
        cnt = count((sc_c == t) & (col_c < cand), (sc_n == t) & (idx_n < cand))
        return jnp.where(cnt < need, cand, x)

    x = lax.fori_loop(0, 12, tie_step, jnp.zeros((ts, 1), I32))
    negm_c = jnp.where((sc_c > t) | ((sc_c == t) & (col_c <= x)), 0.0, -jnp.inf)
    negm_n = jnp.where((sc_n > t) | ((sc_n == t) & (idx_n <= x)), 0.0, -jnp.inf)

    dist_c = (row_c - col_c + past).astype(F32)
    dist_n = jnp.abs(row_n - col_n).astype(F32)
    q_all = qa_ref[...][:, :, 0:DH_A].reshape(H_A * ts, DH_A)
    s_c = _dot_nt(q_all, cak_ref[...].astype(BF16))
    s_n = _dot_nt(q_all, pad_rows(nka_ref[...][:, 0:DH_A]))
    for h in range(H_A):
        rows = slice(h * ts, (h + 1) * ts)
        lc = s_c[rows] + (negm_c - SLOPES_A[h] * dist_c)
        ln = s_n[rows] + (negm_n - SLOPES_A[h] * dist_n)
        m = jnp.maximum(jnp.max(lc, axis=1, keepdims=True), jnp.max(ln, axis=1, keepdims=True))
        m = jnp.maximum(m, NEG_INIT)
        pc_ref[rows, :] = jnp.exp(lc - m).astype(BF16)
        pn_ref[rows, :] = jnp.exp(ln - m).astype(BF16)
    vc_aug = jnp.concatenate([cav_ref[...].astype(BF16), jnp.ones((past, 64), BF16)], axis=1)
    pv = _dot(pc_ref[...], vc_aug) + _dot(pn_ref[...], pad_rows(nva_ref[...]))
    pv = pv / pltpu.roll(pv, DH_A, axis=1)
    oa_ref[...] = jnp.concatenate(
        [pv[h * ts:(h + 1) * ts, :DH_A] for h in range(H_A)], axis=1).astype(BF16)

    lam = _lambda(lq1_ref, lk1_ref, lq2_ref, lk2_ref)
    neg_new = jnp.where(new_ok, 0.0, -jnp.inf)
    for h in range(H_B):
        a_c = None
        a_n = None
        for c in range(2):
            hc = 2 * h + c
            q = qb_ref[hc]
            kc = cbk_ref[:, hc * DH_B:(hc + 1) * DH_B].astype(BF16)
            lc = _dot_nt(q, kc) - SLOPES_B[h] * dist_c
            ln = _dot_nt(q, pad_rows(nkb_ref[hc])) + (neg_new - SLOPES_B[h] * dist_n)
            m = jnp.maximum(jnp.max(lc, axis=1, keepdims=True), jnp.max(ln, axis=1, keepdims=True))
            pc = jnp.exp(lc - m)
            pn = jnp.exp(ln - m)
            l = jnp.sum(pc, axis=1, keepdims=True) + jnp.sum(pn, axis=1, keepdims=True)
            pc = pc / l
            pn = pn / l
            if c == 0:
                a_c, a_n = pc, pn
            else:
                a_c, a_n = a_c - lam * pc, a_n - lam * pn
        vc = cbv_ref[:, h * DV_B:(h + 1) * DV_B].astype(BF16)
        vn = pad_rows(nvb_ref[:, h * DV_B:(h + 1) * DV_B])
        o = _dot(a_c.astype(BF16), vc) + _dot(a_n.astype(BF16), vn)
        o = _rms(o, sg_ref[...]) * (1.0 - LAM_INIT)
        ob_ref[:, h * DV_B:(h + 1) * DV_B] = o.astype(BF16)


def _sample(qa, qi, sm, nka, nva, nki, qb, nkb, nvb, cak, cav, cai, cbk, cbv,
            lq1, lk1, lq2, lk2, sg, batch, ts, past, k_top):
    row = lambda b: (b, 0)
    head = lambda b: (0, b, 0)
    cache = lambda b: (b, 0, 0)
    const = lambda b: (0, 0)
    return pl.pallas_call(
        functools.partial(_sample_body, k_top=k_top, past=past, ts=ts),
        out_shape=(jax.ShapeDtypeStruct((batch * ts, H_A * DH_A), BF16),
                   jax.ShapeDtypeStruct((batch * ts, H_B * DV_B), BF16)),
        grid=(batch,),
        in_specs=[
            pl.BlockSpec((H_A, ts, 128), head),
            pl.BlockSpec((H_I, ts, D_IDX), head),
            pl.BlockSpec((ts, 256), row),
            pl.BlockSpec((ts, 128), row),
            pl.BlockSpec((ts, 128), row),
            pl.BlockSpec((ts, D_IDX), row),
            pl.BlockSpec((2 * H_B, ts, DH_B), head),
            pl.BlockSpec((2 * H_B, ts, DH_B), head),
            pl.BlockSpec((ts, H_B * DV_B), row),
            pl.BlockSpec((None, past, DH_A), cache),
            pl.BlockSpec((None, past, DH_A), cache),
            pl.BlockSpec((None, past, D_IDX), cache),
            pl.BlockSpec((None, past, 2 * H_B * DH_B), cache),
            pl.BlockSpec((None, past, H_B * DV_B), cache),
            pl.BlockSpec((1, DH_B), const),
            pl.BlockSpec((1, DH_B), const),
            pl.BlockSpec((1, DH_B), const),
            pl.BlockSpec((1, DH_B), const),
            pl.BlockSpec((1, DV_B), const),
        ],
        out_specs=(pl.BlockSpec((ts, H_A * DH_A), row), pl.BlockSpec((ts, H_B * DV_B), row)),
        scratch_shapes=[
            pltpu.VMEM((H_A * ts, past), BF16),
            pltpu.VMEM((H_A * ts, LANES), BF16),
        ],
        compiler_params=pltpu.CompilerParams(
            dimension_semantics=("arbitrary",), vmem_limit_bytes=VMEM_LIMIT),
        name="sample",
    )(qa, qi, sm, nka, nva, nki, qb, nkb, nvb, cak, cav, cai, cbk, cbv, lq1, lk1, lq2, lk2, sg)


def _finish_body(x_ref, oa_ref, ob_ref, wo_ref, gm_ref, wu_ref, wd_ref, gf_ref, y_ref):
    o = jnp.concatenate([oa_ref[...], ob_ref[...]], axis=1)
    h1 = x_ref[...] + _dot(o, wo_ref[...])
    hn = _rms(h1, gm_ref[...]).astype(BF16)
    acc = h1
    for c in range(D_FF // D_MODEL):
        u = jnp.maximum(_dot(hn, wu_ref[:, c * D_MODEL:(c + 1) * D_MODEL]), 0.0)
        acc = acc + _dot((u * u).astype(BF16), wd_ref[c * D_MODEL:(c + 1) * D_MODEL, :])
    y_ref[...] = _rms(acc, gf_ref[...])


def _finish(x, oa, ob, wo, gm, wu, wd, gf, tm):
    m = x.shape[0]
    assert m % tm == 0
    row = lambda i: (i, 0)
    const = lambda i: (0, 0)
    resident = dict(pipeline_mode=pl.Buffered(1))
    return pl.pallas_call(
        _finish_body,
        out_shape=jax.ShapeDtypeStruct((m, D_MODEL), F32),
        grid=(m // tm,),
        in_specs=[
            pl.BlockSpec((tm, D_MODEL), row),
            pl.BlockSpec((tm, 512), row),
            pl.BlockSpec((tm, 512), row),
            pl.BlockSpec((D_MODEL, D_MODEL), const, **resident),
            pl.BlockSpec((1, D_MODEL), const),
            pl.BlockSpec((D_MODEL, D_FF), const, **resident),
            pl.BlockSpec((D_FF, D_MODEL), const, **resident),
            pl.BlockSpec((1, D_MODEL), const),
        ],
        out_specs=pl.BlockSpec((tm, D_MODEL), row),
        compiler_params=pltpu.CompilerParams(
            dimension_semantics=("arbitrary",), vmem_limit_bytes=VMEM_LIMIT),
        name="finish",
    )(x, oa, ob, wo, gm, wu, wd, gf)


def _pad_rows_to(x, rows, axis):
    widths = [(0, 0)] * x.ndim
    widths[axis] = (0, rows - x.shape[axis])
    return jnp.pad(x, widths)


def kernel(x_prompt, x_sample, cache_a_k, cache_a_v, cache_a_idx_k, cache_b_k, cache_b_v,
           meta_tokens, attn_norm_g, w_in, idx_k_norm_g, idx_k_norm_b,
           lambda_q1, lambda_k1, lambda_q2, lambda_k2, subln_g, w_o,
           mlp_norm_g, w_up, w_down, final_norm_g):
    batch, seq, _ = x_prompt.shape
    dec_batch, ts, _ = x_sample.shape
    past = cache_a_k.shape[2]
    assert attn_norm_g.shape[0] == 1, "single-layer step"
    assert seq % TQ == 0 and ts == 16 and past % LANES == 0 and past // CHUNK == (past + ts - 1) // CHUNK
    n = N_META + seq
    k_top_p = min(TOPK_MAX, seq // 4)
    k_top_s = min(TOPK_MAX, (past + ts) // 4)

    w = w_in[0]
    w = jnp.concatenate(
        [w[:, 0:512],
         w[:, 964:1476],
         w[:, 1476:1988],
         w[:, 1988:2500],
         w[:, 640:896],
         w[:, 512:640],
         w[:, 896:960],
         w[:, 960:964],
         jnp.zeros((D_MODEL, W_COLS - 2500), w.dtype)], axis=1).astype(BF16)
    wo = w_o[0].astype(BF16)
    wu = w_up[0].astype(BF16)
    wd = w_down[0].astype(BF16)
    g_attn = attn_norm_g[0][None]
    g_mlp = mlp_norm_g[0][None]
    g_fin = final_norm_g[None]
    kng = idx_k_norm_g[0][None]
    knb = idx_k_norm_b[0][None]
    lq1, lk1, lq2, lk2 = lambda_q1[0][None], lambda_k1[0][None], lambda_q2[0][None], lambda_k2[0][None]
    sg = subln_g[0][None]

    xp = x_prompt.reshape(batch * seq, D_MODEL)
    xs = x_sample.reshape(dec_batch * ts, D_MODEL)

    (qa_p, qi_p, qb_p, kbh_p, kb_p, vb_p, vbb_p, sm_p, kab_p, _, kib_p, vt_p, wt_p) = _proj(
        xp, g_attn, w, kng, knb, 512, N_META, seq, True)
    (_, _, _, kbh_m, kb_m, vb_m, vbb_m, sm_m, kab_m, vaa_m, kib_m) = _proj(
        meta_tokens, g_attn, w, kng, knb, N_META, 0, N_META, False)
    (qa_s, qi_s, qb_s, kbh_s, kb_s, vb_s, vbb_s, sm_s, kab_s, vaa_s, kib_s) = _proj(
        xs, g_attn, w, kng, knb, dec_batch * ts, past, dec_batch * ts, False)

    oa_p = _dsa(qa_p, qi_p, wt_p, kab_p, kib_p, vt_p, kab_m, kib_m, vaa_m, batch, seq, k_top_p)
    ob_p = _diff(qb_p, kbh_p, vbb_p, _pad_rows_to(kbh_m, LANES, 1), _pad_rows_to(vbb_m, LANES, 0),
                 lq1, lk1, lq2, lk2, sg, batch, seq)
    y_prompt = _finish(xp, oa_p, ob_p, wo, g_mlp, wu, wd, g_fin, 512).reshape(batch, seq, D_MODEL)

    oa_s, ob_s = _sample(
        qa_s, qi_s, sm_s, kab_s, vaa_s, kib_s, qb_s, kbh_s, vbb_s,
        cache_a_k[0], cache_a_v[0], cache_a_idx_k[0],
        cache_b_k[0].reshape(dec_batch, past, 2 * H_B * DH_B),
        cache_b_v[0].reshape(dec_batch, past, H_B * DV_B),
        lq1, lk1, lq2, lk2, sg, dec_batch, ts, past, k_top_s)
    y_sample = _finish(xs, oa_s, ob_s, wo, g_mlp, wu, wd, g_fin, dec_batch * ts).reshape(
        dec_batch, ts, D_MODEL)

    def with_meta(meta_rows, main_rows, width):
        meta_b = jnp.broadcast_to(meta_rows[None], (batch, N_META, width))
        return jnp.concatenate([meta_b, main_rows.reshape(batch, seq, width)], axis=1)[None]

    new_a_k_p = with_meta(sm_m[:, SM_KA:SM_KA + 64], sm_p[:, SM_KA:SM_KA + 64], 64)
    new_a_v_p = with_meta(sm_m[:, SM_VA:SM_VA + 64], sm_p[:, SM_VA:SM_VA + 64], 64)
    new_a_i_p = with_meta(sm_m[:, SM_KI:SM_KI + 64], sm_p[:, SM_KI:SM_KI + 64], 64)
    new_b_k_p = with_meta(kb_m, kb_p, 512).reshape(1, batch, n, H_B, 2, DH_B)
    new_b_v_p = with_meta(vb_m, vb_p, 512).reshape(1, batch, n, H_B, DV_B)
    new_a_k_s = sm_s[:, SM_KA:SM_KA + 64].reshape(1, dec_batch, ts, 64)
    new_a_v_s = sm_s[:, SM_VA:SM_VA + 64].reshape(1, dec_batch, ts, 64)
    new_a_i_s = sm_s[:, SM_KI:SM_KI + 64].reshape(1, dec_batch, ts, 64)
    new_b_k_s = kb_s.reshape(1, dec_batch, ts, H_B, 2, DH_B)
    new_b_v_s = vb_s.reshape(1, dec_batch, ts, H_B, DV_B)
    return (y_prompt, y_sample, new_a_k_p, new_a_v_p, new_a_i_p, new_b_k_p, new_b_v_p,
            new_a_k_s, new_a_v_s, new_a_i_s, new_b_k_s, new_b_v_s)
```

```python
import functools

import jax
import jax.numpy as jnp
from jax import lax
from jax.experimental import pallas as pl
from jax.experimental.pallas import tpu as pltpu

F32 = jnp.float32
BF16 = jnp.bfloat16
I32 = jnp.int32
I16 = jnp.int16
HALF16 = 32768

D_MODEL = 1024
CHUNK = 64
N_META = 16
H_A = 8
DH_A = 64
H_I = 4
D_IDX = 64
TOPK_MAX = 256
H_B = 4
DH_B = 64
DV_B = 2 * DH_B
D_FF = 4 * D_MODEL
EPS = 1e-6
LAM_INIT = 0.2

LANES = 128
C_QA, C_QB, C_KB, C_VB, C_QI, C_SM = 0, 512, 1024, 1536, 2048, 2304
W_COLS = 2560
SM_KA, SM_VA, SM_KI, SM_WI = 0, 64, 128, 192

Q_SCALE = DH_A ** -0.5
W_SCALE = (H_I ** -0.5) * (D_IDX ** -0.5)
SLOPES_A = tuple(2.0 ** (-8.0 * (i + 1) / H_A) for i in range(H_A))
SLOPES_B = tuple(2.0 ** (-8.0 * (i + 1) / H_B) for i in range(H_B))

INT_MIN = -(2 ** 31)
NEG_INIT = -1e30
TQ = 256
CNT_ROWS = 32
VMEM_LIMIT = 56 * 1024 * 1024

_NT = (((1,), (1,)), ((), ()))


def _dot(a, b):
    return jnp.dot(a, b, preferred_element_type=F32)


def _dot_nt(a, b):
    return lax.dot_general(a, b, _NT, preferred_element_type=F32)


def _rms(x, g):
    return (x * lax.rsqrt(jnp.mean(x * x, axis=-1, keepdims=True) + EPS)) * g


def _to_key(score):
    b = lax.bitcast_convert_type(score, I32)
    return b ^ ((b >> 31) & jnp.int32(0x7FFFFFFF))


def _f32_from_key(key):
    return lax.bitcast_convert_type(key ^ ((key >> 31) & jnp.int32(0x7FFFFFFF)), F32)


def _bf16_from_key(key):
    bits = key ^ ((key >> 15) & jnp.int32(0x7FFF))
    return lax.bitcast_convert_type(jnp.left_shift(bits, 16), F32).astype(BF16)


KEY16_LOWEST = -32640
KEY32_LOWEST = -2139095040
KEY32_HIGHEST = 2139095039


def _wide(x, width):
    reps = width // LANES
    return x if reps == 1 else jnp.concatenate([x] * reps, axis=1)


def _proj_body(x_ref, g_ref, w_ref, kng_ref, knb_ref,
               qa_ref, qi_ref, qb_ref, kbh_ref, kb_ref, vb_ref, vbb_ref,
               sm_ref, kab_ref, vaa_ref, kib_ref, *maybe_t_refs, pos0, period):
    x = x_ref[...]
    hn = _rms(x, g_ref[...]).astype(BF16)
    rows = x.shape[0]

    def mm(lo, width):
        return _dot(hn, w_ref[:, lo:lo + width])

    lane = lax.broadcasted_iota(I32, (rows, 64), 1)
    z = mm(C_QA, 512) * Q_SCALE
    for h in range(H_A):
        slope_cols = jnp.where(lane < 2, SLOPES_A[h], 0.0).astype(BF16)
        qa_ref[h] = jnp.concatenate([z[:, h * DH_A:(h + 1) * DH_A].astype(BF16), slope_cols], axis=1)
    z = mm(C_QB, 512) * Q_SCALE
    for hc in range(2 * H_B):
        qb_ref[hc] = z[:, hc * DH_B:(hc + 1) * DH_B].astype(BF16)
    z = mm(C_KB, 512)
    kb_ref[...] = z
    for hc in range(2 * H_B):
        kbh_ref[hc] = z[:, hc * DH_B:(hc + 1) * DH_B].astype(BF16)
    z = mm(C_VB, 512)
    vb_ref[...] = z
    vbb_ref[...] = z.astype(BF16)
    z = mm(C_QI, 256)
    for h in range(H_I):
        qi_ref[h] = z[:, h * D_IDX:(h + 1) * D_IDX].astype(BF16)
    z = mm(C_SM, 256)
    ka = z[:, SM_KA:SM_KA + 64]
    va = z[:, SM_VA:SM_VA + 64]
    ki = z[:, SM_KI:SM_KI + 64]
    xc = ki - jnp.mean(ki, axis=-1, keepdims=True)
    ki = xc * lax.rsqrt(jnp.mean(xc * xc, axis=-1, keepdims=True) + EPS) * kng_ref[...] + knb_ref[...]
    sm_ref[:, 0:128] = z[:, 0:128]
    sm_ref[:, SM_KI:SM_KI + 64] = ki
    sm_ref[:, SM_WI:SM_WI + 64] = z[:, SM_WI:SM_WI + 64] * W_SCALE
    r = pl.program_id(0) * rows + lax.broadcasted_iota(I32, (rows, 64), 0)
    pos = pos0 + r % period
    pos_cols = jnp.where(lane == 0, (pos // 256) * 256, jnp.where(lane == 1, pos % 256, 0))
    kab_ref[...] = jnp.concatenate([ka.astype(BF16), pos_cols.astype(F32).astype(BF16)], axis=1)
    kib_ref[...] = ki.astype(BF16)
    vaa_ref[...] = jnp.concatenate([va.astype(BF16), jnp.ones((rows, 64), BF16)], axis=1)
    if maybe_t_refs:
        vt_ref, wt_ref = maybe_t_refs
        kv_t = z[:, 0:128].T
        for c in range(rows // TQ):
            vt_ref[c] = jnp.concatenate(
                [kv_t[64:128, c * TQ:(c + 1) * TQ].astype(BF16), jnp.ones((64, TQ), BF16)], axis=0)
        wt_ref[...] = z[:, 128:256].T[64:72, :] * W_SCALE


def _proj(x, g, w, kng, knb, tm, pos0, period, key_major):
    m = x.shape[0]
    assert m % tm == 0
    row = lambda i: (i, 0)
    head = lambda i: (0, i, 0)
    const = lambda i: (0, 0)
    out_shape = (
        jax.ShapeDtypeStruct((H_A, m, 128), BF16),
        jax.ShapeDtypeStruct((H_I, m, D_IDX), BF16),
        jax.ShapeDtypeStruct((2 * H_B, m, DH_B), BF16),
        jax.ShapeDtypeStruct((2 * H_B, m, DH_B), BF16),
        jax.ShapeDtypeStruct((m, 512), F32),
        jax.ShapeDtypeStruct((m, 512), F32),
        jax.ShapeDtypeStruct((m, 512), BF16),
        jax.ShapeDtypeStruct((m, 256), F32),
        jax.ShapeDtypeStruct((m, 128), BF16),
        jax.ShapeDtypeStruct((m, 128), BF16),
        jax.ShapeDtypeStruct((m, 64), BF16),
    )
    out_specs = (
        pl.BlockSpec((H_A, tm, 128), head),
        pl.BlockSpec((H_I, tm, D_IDX), head),
        pl.BlockSpec((2 * H_B, tm, DH_B), head),
        pl.BlockSpec((2 * H_B, tm, DH_B), head),
        pl.BlockSpec((tm, 512), row),
        pl.BlockSpec((tm, 512), row),
        pl.BlockSpec((tm, 512), row),
        pl.BlockSpec((tm, 256), row),
        pl.BlockSpec((tm, 128), row),
        pl.BlockSpec((tm, 128), row),
        pl.BlockSpec((tm, 64), row),
    )
    if key_major:
        assert tm % TQ == 0
        out_shape += (
            jax.ShapeDtypeStruct((m // TQ, 128, TQ), BF16),
            jax.ShapeDtypeStruct((8, m), F32),
        )
        out_specs += (
            pl.BlockSpec((tm // TQ, 128, TQ), lambda i: (i, 0, 0)),
            pl.BlockSpec((8, tm), lambda i: (0, i)),
        )
    return pl.pallas_call(
        functools.partial(_proj_body, pos0=pos0, period=period),
        out_shape=out_shape,
        grid=(m // tm,),
        in_specs=[
            pl.BlockSpec((tm, D_MODEL), row),
            pl.BlockSpec((1, D_MODEL), const),
            pl.BlockSpec((D_MODEL, W_COLS), const),
            pl.BlockSpec((1, D_IDX), const),
            pl.BlockSpec((1, D_IDX), const),
        ],
        out_specs=out_specs,
        compiler_params=pltpu.CompilerParams(
            dimension_semantics=("arbitrary",), vmem_limit_bytes=VMEM_LIMIT),
        name="proj",
    )(x, g, w, kng, knb)


def _dsa_body(qa_ref, qi_ref, wt_ref, ka_ref, ki_ref, vt_ref, mka_ref, mki_ref, mva_ref,
              o_ref, smeta_ref, smain_ref, bmeta_ref, bmain_ref,
              tsel_ref, m_ref, al_ref, acc_ref, p_ref, s_ref, seen_ref, *, k_top):
    i = pl.program_id(1)
    krow = lax.broadcasted_iota(I32, (TQ, TQ), 0)
    qcol = lax.broadcasted_iota(I32, (TQ, TQ), 1)

    def main_rows(j):
        return pl.ds(pl.multiple_of(j * TQ, TQ), TQ)

    qi_all = qi_ref[...].reshape(H_I * TQ, D_IDX)

    def scores(ki_blk):
        s4 = _dot_nt(ki_blk, qi_all)
        sc = None
        for h in range(H_I):
            t = jnp.maximum(s4[:, h * TQ:(h + 1) * TQ], 0.0) * wt_ref[h:h + 1, :]
            sc = t if sc is None else sc + t
        return sc

    sc = scores(mki_ref[...])
    smeta_ref[...] = sc
    bmeta_ref[...] = sc.astype(BF16)

    def score_block(j, sc):
        smain_ref[j] = sc
        bmain_ref[j] = sc.astype(BF16)

    def score_full(j, carry):
        score_block(j, scores(ki_ref[main_rows(j), :]))
        return carry

    lax.fori_loop(0, i, score_full, 0)
    score_block(i, jnp.where((krow // CHUNK) <= (qcol // CHUNK),
                             scores(ki_ref[main_rows(i), :]), -jnp.inf))

    def count(meta_ref, main_ref, part, pred):
        def body(j, c):
            return c + part(pred(main_ref[j]))

        c = lax.fori_loop(0, i + 1, body, jnp.zeros((CNT_ROWS, TQ), F32))
        cm = part(pred(meta_ref[...]), rows=N_META)
        return jnp.sum(c, axis=0, keepdims=True) + jnp.sum(cm, axis=0, keepdims=True)

    def part32(hit, rows=CNT_ROWS):
        ones = jnp.where(hit, 1.0, 0.0)
        return jnp.sum(ones.reshape(ones.shape[0] // rows, rows, TQ), axis=0)

    def part16(hit, rows=CNT_ROWS):
        ones = jnp.where(hit, jnp.bfloat16(1), jnp.bfloat16(0))
        acc = ones[0:rows]
        for t in range(1, ones.shape[0] // rows):
            acc = acc + ones[t * rows:(t + 1) * rows]
        return acc.astype(F32)

    kf = float(k_top)

    def step16(s, u):
        cand_u = u | jnp.left_shift(jnp.int32(1), 15 - s)
        cand = _bf16_from_key(jnp.maximum(cand_u, KEY16_LOWEST + HALF16) - HALF16)
        cnt = count(bmeta_ref, bmain_ref, part16, lambda x: x >= cand)
        return jnp.where(cnt >= kf, cand_u, u)

    u_hi = lax.fori_loop(0, 16, step16, jnp.zeros((1, TQ), I32))
    t_hi = _bf16_from_key(jnp.maximum(u_hi, KEY16_LOWEST + HALF16) - HALF16).astype(F32)
    base = jnp.maximum(_to_key(t_hi) - 65536, KEY32_LOWEST)

    def step32(s, d):
        cand_d = d | jnp.left_shift(jnp.int32(1), 16 - s)
        cand = _f32_from_key(jnp.minimum(base + cand_d, KEY32_HIGHEST))
        cnt = count(smeta_ref, smain_ref, part32, lambda x: x >= cand)
        return jnp.where(cnt >= kf, cand_d, d)

    d_lo = lax.fori_loop(0, 17, step32, jnp.zeros((1, TQ), I32))
    t_sel = _f32_from_key(jnp.minimum(base + d_lo, KEY32_HIGHEST))
    need = kf - count(smeta_ref, smain_ref, part32, lambda x: x > t_sel)
    tsel_ref[...] = jnp.broadcast_to(t_sel, tsel_ref.shape)

    q_all = qa_ref[...].reshape(H_A * TQ, 128)
    m_ref[...] = jnp.full(m_ref.shape, NEG_INIT, F32)
    acc_ref[...] = jnp.zeros(acc_ref.shape, F32)
    thr = tsel_ref[0:1, :]

    def logits(k_blk):
        return _dot_nt(k_blk, q_all)

    def softmax(sc, s_all, future, seen, slot):
        nk = sc.shape[0]
        tie = sc == thr
        tri = (lax.broadcasted_iota(I32, (nk, nk), 1)
               <= lax.broadcasted_iota(I32, (nk, nk), 0)).astype(BF16)
        rank = _dot(tri, jnp.where(tie, 1.0, 0.0).astype(BF16)) + seen
        negm = jnp.where(sc > thr, 0.0, jnp.where(tie, jnp.where(rank <= need, 0.0, -jnp.inf), -jnp.inf))
        for h in range(H_A):
            cols = slice(h * TQ, (h + 1) * TQ)
            s = s_all[:, cols] + negm
            if future is not None:
                s = s - (2.0 * SLOPES_A[h]) * future
            m_old = m_ref[:, cols]
            m_new = jnp.maximum(m_old, jnp.max(s, axis=0, keepdims=True))
            p_ref[slot, 0:nk, cols] = jnp.exp(s - m_new).astype(BF16)
            al_ref[slot, :, cols] = jnp.exp(m_old - m_new)
            m_ref[:, cols] = m_new
        return rank[nk - 1:nk, :]

    def accumulate(vt_blk, slot, nk):
        acc_ref[...] = acc_ref[...] * al_ref[slot] + _dot(vt_blk, p_ref[slot, 0:nk, :])

    eye = (lax.broadcasted_iota(I32, (128, 128), 0)
           == lax.broadcasted_iota(I32, (128, 128), 1)).astype(BF16)
    seen_ref[...] = softmax(smeta_ref[...], logits(mka_ref[...]), None, jnp.zeros((1, TQ), F32), 0)
    accumulate(_dot_nt(eye, mva_ref[...]).astype(BF16), 0, N_META)

    p_ref[1] = jnp.zeros(p_ref.shape[1:], BF16)
    al_ref[1] = jnp.ones(al_ref.shape[1:], F32)
    s_ref[0] = logits(ka_ref[main_rows(0), :])

    def full_step(t, slot):
        s_ref[1 - slot] = logits(ka_ref[main_rows(t + 1), :])
        accumulate(vt_ref[jnp.maximum(t - 1, 0)], 1 - slot, TQ)
        seen_ref[...] = softmax(smain_ref[t], s_ref[slot], None, seen_ref[...], slot)

    def two_steps(u, carry):
        full_step(2 * u, 0)
        full_step(2 * u + 1, 1)
        return carry

    lax.fori_loop(0, i // 2, two_steps, 0)

    def last_step(slot):
        accumulate(vt_ref[jnp.maximum(i - 1, 0)], 1 - slot, TQ)
        softmax(smain_ref[i], s_ref[slot], jnp.maximum(krow - qcol, 0).astype(F32),
                seen_ref[...], slot)
        accumulate(vt_ref[i], slot, TQ)

    @pl.when(i % 2 == 1)
    def _odd():
        full_step(i - 1, 0)
        last_step(1)

    @pl.when(i % 2 == 0)
    def _even():
        last_step(0)

    outs = []
    for h in range(H_A):
        a = acc_ref[:, h * TQ:(h + 1) * TQ].T
        o = a / pltpu.roll(a, DH_A, axis=1)
        outs.append(o[:, :DH_A])
    o_ref[...] = jnp.concatenate(outs, axis=1).astype(BF16)


def _dsa(qa, qi, wt, kab, kib, vt, mka, mki, mva, batch, seq, k_top):
    nq = seq // TQ
    qrow = lambda b, i: (b * nq + i, 0)
    qhead = lambda b, i: (0, b * nq + i, 0)
    kv = lambda b, i: (b, 0)
    const = lambda b, i: (0, 0)
    return pl.pallas_call(
        functools.partial(_dsa_body, k_top=k_top),
        out_shape=jax.ShapeDtypeStruct((batch * seq, H_A * DH_A), BF16),
        grid=(batch, nq),
        in_specs=[
            pl.BlockSpec((H_A, TQ, 128), qhead),
            pl.BlockSpec((H_I, TQ, D_IDX), qhead),
            pl.BlockSpec((8, TQ), lambda b, i: (0, b * nq + i)),
            pl.BlockSpec((seq, 128), kv),
            pl.BlockSpec((seq, D_IDX), kv),
            pl.BlockSpec((nq, 128, TQ), lambda b, i: (b, 0, 0)),
            pl.BlockSpec((N_META, 128), const),
            pl.BlockSpec((N_META, D_IDX), const),
            pl.BlockSpec((N_META, 128), const),
        ],
        out_specs=pl.BlockSpec((TQ, H_A * DH_A), qrow),
        scratch_shapes=[
            pltpu.VMEM((N_META, TQ), F32),
            pltpu.VMEM((nq, TQ, TQ), F32),
            pltpu.VMEM((N_META, TQ), BF16),
            pltpu.VMEM((nq, TQ, TQ), BF16),
            pltpu.VMEM((8, TQ), F32),
            pltpu.VMEM((1, H_A * TQ), F32),
            pltpu.VMEM((2, 1, H_A * TQ), F32),
            pltpu.VMEM((128, H_A * TQ), F32),
            pltpu.VMEM((2, TQ, H_A * TQ), BF16),
            pltpu.VMEM((2, TQ, H_A * TQ), F32),
            pltpu.VMEM((1, TQ), F32),
        ],
        compiler_params=pltpu.CompilerParams(
            dimension_semantics=("arbitrary", "arbitrary"), vmem_limit_bytes=VMEM_LIMIT),
        name="dsa",
    )(qa, qi, wt, kab, kib, vt, mka, mki, mva)


def _lambda(lq1_ref, lk1_ref, lq2_ref, lk2_ref):
    s1 = jnp.sum(lq1_ref[...] * lk1_ref[...], axis=-1, keepdims=True)
    s2 = jnp.sum(lq2_ref[...] * lk2_ref[...], axis=-1, keepdims=True)
    return jnp.exp(s1) - jnp.exp(s2) + LAM_INIT


def _diff_body(qb_ref, kbh_ref, vb_ref, mkb_ref, mvb_ref, lq1_ref, lk1_ref, lq2_ref, lk2_ref,
               sg_ref, o_ref, m_ref, l_ref, acc_ref, p_ref):
    i = pl.program_id(1)
    row_m = lax.broadcasted_iota(I32, (TQ, LANES), 0)
    col_m = lax.broadcasted_iota(I32, (TQ, LANES), 1)
    row_k = lax.broadcasted_iota(I32, (TQ, TQ), 0)
    col_k = lax.broadcasted_iota(I32, (TQ, TQ), 1)
    m_ref[...] = jnp.full(m_ref.shape, NEG_INIT, F32)
    l_ref[...] = jnp.zeros(l_ref.shape, F32)
    acc_ref[...] = jnp.zeros(acc_ref.shape, F32)

    def attend(k_of, v_blk, dist, negm, width):
        for h in range(H_B):
            bias = -SLOPES_B[h] * dist
            if negm is not None:
                bias = bias + negm
            for c in range(2):
                hc = 2 * h + c
                s = _dot_nt(qb_ref[hc], k_of(hc)) + bias
                m_old = m_ref[hc]
                m_new = jnp.maximum(m_old, jnp.max(s, axis=1, keepdims=True))
                p = jnp.exp(s - _wide(m_new, width))
                alpha = jnp.exp(m_old - m_new)
                m_ref[hc] = m_new
                psum = p[:, 0:LANES]
                for t in range(1, width // LANES):
                    psum = psum + p[:, t * LANES:(t + 1) * LANES]
                l_ref[hc] = l_ref[hc] * alpha + psum
                acc_ref[hc] = acc_ref[hc] * alpha
                p_ref[c * TQ:(c + 1) * TQ, 0:width] = p.astype(BF16)
            pv = _dot(p_ref[:, 0:width], v_blk[:, h * DV_B:(h + 1) * DV_B])
            acc_ref[2 * h] += pv[0:TQ]
            acc_ref[2 * h + 1] += pv[TQ:2 * TQ]

    d_meta = (row_m - col_m + N_META + i * TQ).astype(F32)
    neg_meta = jnp.where(col_m < N_META, 0.0, -jnp.inf)
    attend(lambda hc: mkb_ref[hc], mvb_ref[...], d_meta, neg_meta, LANES)
    d0 = row_k - col_k

    def attend_full(j, carry):
        r = pl.ds(pl.multiple_of(j * TQ, TQ), TQ)
        dist = (d0 + (i - j) * TQ).astype(F32)
        attend(lambda hc: kbh_ref[hc, r, :], vb_ref[r, :], dist, None, TQ)
        return carry

    lax.fori_loop(0, i, attend_full, 0)
    r = pl.ds(pl.multiple_of(i * TQ, TQ), TQ)
    neg_diag = jnp.where((col_k // CHUNK) <= (row_k // CHUNK), 0.0, -jnp.inf)
    attend(lambda hc: kbh_ref[hc, r, :], vb_ref[r, :], jnp.abs(d0).astype(F32), neg_diag, TQ)

    lam = _lambda(lq1_ref, lk1_ref, lq2_ref, lk2_ref)
    for h in range(H_B):
        l0 = jnp.sum(l_ref[2 * h], axis=1, keepdims=True)
        l1 = jnp.sum(l_ref[2 * h + 1], axis=1, keepdims=True)
        o = acc_ref[2 * h] / l0 - lam * (acc_ref[2 * h + 1] / l1)
        o = _rms(o, sg_ref[...]) * (1.0 - LAM_INIT)
        o_ref[:, h * DV_B:(h + 1) * DV_B] = o.astype(BF16)


def _diff(qb, kbh, vbb, mkb, mvb, lq1, lk1, lq2, lk2, sg, batch, seq):
    nq = seq // TQ
    qrow = lambda b, i: (b * nq + i, 0)
    qhead = lambda b, i: (0, b * nq + i, 0)
    const2 = lambda b, i: (0, 0)
    const3 = lambda b, i: (0, 0, 0)
    return pl.pallas_call(
        _diff_body,
        out_shape=jax.ShapeDtypeStruct((batch * seq, H_B * DV_B), BF16),
        grid=(batch, nq),
        in_specs=[
            pl.BlockSpec((2 * H_B, TQ, DH_B), qhead),
            pl.BlockSpec((2 * H_B, seq, DH_B), lambda b, i: (0, b, 0)),
            pl.BlockSpec((seq, H_B * DV_B), lambda b, i: (b, 0)),
            pl.BlockSpec((2 * H_B, LANES, DH_B), const3),
            pl.BlockSpec((LANES, H_B * DV_B), const2),
            pl.BlockSpec((1, DH_B), const2),
            pl.BlockSpec((1, DH_B), const2),
            pl.BlockSpec((1, DH_B), const2),
            pl.BlockSpec((1, DH_B), const2),
            pl.BlockSpec((1, DV_B), const2),
        ],
        out_specs=pl.BlockSpec((TQ, H_B * DV_B), qrow),
        scratch_shapes=[
            pltpu.VMEM((2 * H_B, TQ, LANES), F32),
            pltpu.VMEM((2 * H_B, TQ, LANES), F32),
            pltpu.VMEM((2 * H_B, TQ, DV_B), F32),
            pltpu.VMEM((2 * TQ, TQ), BF16),
        ],
        compiler_params=pltpu.CompilerParams(
            dimension_semantics=("arbitrary", "arbitrary"), vmem_limit_bytes=VMEM_LIMIT),
        name="diff",
    )(qb, kbh, vbb, mkb, mvb, lq1, lk1, lq2, lk2, sg)


def _sample_body(qa_ref, qi_ref, sm_ref, nka_ref, nva_ref, nki_ref, qb_ref, nkb_ref, nvb_ref,
                 cak_ref, cav_ref, cai_ref, cbk_ref, cbv_ref,
                 lq1_ref, lk1_ref, lq2_ref, lk2_ref, sg_ref,
                 oa_ref, ob_ref, pc_ref, pn_ref, *, k_top, past, ts):
    pad = LANES - ts
    row_c = lax.broadcasted_iota(I32, (ts, past), 0)
    col_c = lax.broadcasted_iota(I32, (ts, past), 1)
    row_n = lax.broadcasted_iota(I32, (ts, LANES), 0)
    col_n = lax.broadcasted_iota(I32, (ts, LANES), 1)
    new_ok = col_n < ts

    def pad_rows(x):
        return jnp.concatenate([x, jnp.zeros((pad,) + x.shape[1:], x.dtype)], axis=0)

    qi_all = qi_ref[...].reshape(H_I * ts, D_IDX)
    w = sm_ref[:, SM_WI:SM_WI + H_I]
    s4c = _dot_nt(qi_all, cai_ref[...].astype(BF16))
    s4n = _dot_nt(qi_all, pad_rows(nki_ref[...]))
    sc_c = None
    sc_n = None
    for h in range(H_I):
        wh = w[:, h:h + 1]
        tc = jnp.maximum(s4c[h * ts:(h + 1) * ts], 0.0) * wh
        tn = jnp.maximum(s4n[h * ts:(h + 1) * ts], 0.0) * wh
        sc_c = tc if sc_c is None else sc_c + tc
        sc_n = tn if sc_n is None else sc_n + tn
    sc_n = jnp.where(new_ok, sc_n, -jnp.inf)

    def count(pred_c, pred_n):
        c = jnp.sum(jnp.where(pred_c, 1.0, 0.0), axis=1, keepdims=True)
        return c + jnp.sum(jnp.where(pred_n, 1.0, 0.0), axis=1, keepdims=True)

    def threshold(key):
        return _f32_from_key(jnp.clip(key, KEY32_LOWEST, KEY32_HIGHEST))

    def value_step(s, key):
        cand_key = key ^ jnp.left_shift(jnp.int32(1), 31 - s)
        cand = threshold(cand_key)
        cnt = count(sc_c >= cand, sc_n >= cand)
        return jnp.where(cnt >= float(k_top), cand_key, key)

    t = threshold(lax.fori_loop(0, 32, value_step, jnp.full((ts, 1), INT_MIN, I32)))
    need = float(k_top) - count(sc_c > t, sc_n > t)
    idx_n = col_n + past

    def tie_step(s, x):
        cand = x | jnp.left_shift(jnp.int32(1), 11 - s)
        cnt = count((sc_c == t) & (col_c < cand), (sc_n == t) & (idx_n < cand))
        return jnp.where(cnt < need, cand, x)

    x = lax.fori_loop(0, 12, tie_step, jnp.zeros((ts, 1), I32))
    negm_c = jnp.where((sc_c > t) | ((sc_c == t) & (col_c <= x)), 0.0, -jnp.inf)
    negm_n = jnp.where((sc_n > t) | ((sc_n == t) & (idx_n <= x)), 0.0, -jnp.inf)

    dist_c = (row_c - col_c + past).astype(F32)
    dist_n = jnp.abs(row_n - col_n).astype(F32)
    q_all = qa_ref[...][:, :, 0:DH_A].reshape(H_A * ts, DH_A)
    s_c = _dot_nt(q_all, cak_ref[...].astype(BF16))
    s_n = _dot_nt(q_all, pad_rows(nka_ref[...][:, 0:DH_A]))
    for h in range(H_A):
        rows = slice(h * ts, (h + 1) * ts)
        lc = s_c[rows] + (negm_c - SLOPES_A[h] * dist_c)
        ln = s_n[rows] + (negm_n - SLOPES_A[h] * dist_n)
        m = jnp.maximum(jnp.max(lc, axis=1, keepdims=True), jnp.max(ln, axis=1, keepdims=True))
        m = jnp.maximum(m, NEG_INIT)
        pc_ref[rows, :] = jnp.exp(lc - m).astype(BF16)
        pn_ref[rows, :] = jnp.exp(ln - m).astype(BF16)
    vc_aug = jnp.concatenate([cav_ref[...].astype(BF16), jnp.ones((past, 64), BF16)], axis=1)
    pv = _dot(pc_ref[...], vc_aug) + _dot(pn_ref[...], pad_rows(nva_ref[...]))
    pv = pv / pltpu.roll(pv, DH_A, axis=1)
    oa_ref[...] = jnp.concatenate(
        [pv[h * ts:(h + 1) * ts, :DH_A] for h in range(H_A)], axis=1).astype(BF16)

    lam = _lambda(lq1_ref, lk1_ref, lq2_ref, lk2_ref)
    neg_new = jnp.where(new_ok, 0.0, -jnp.inf)
    for h in range(H_B):
        a_c = None
        a_n = None
        for c in range(2):
            hc = 2 * h + c
            q = qb_ref[hc]
            kc = cbk_ref[:, hc * DH_B:(hc + 1) * DH_B].astype(BF16)
            lc = _dot_nt(q, kc) - SLOPES_B[h] * dist_c
            ln = _dot_nt(q, pad_rows(nkb_ref[hc])) + (neg_new - SLOPES_B[h] * dist_n)
            m = jnp.maximum(jnp.max(lc, axis=1, keepdims=True), jnp.max(ln, axis=1, keepdims=True))
            pc = jnp.exp(lc - m)
            pn = jnp.exp(ln - m)
            l = jnp.sum(pc, axis=1, keepdims=True) + jnp.sum(pn, axis=1, keepdims=True)
            pc = pc / l
            pn = pn / l
            if c == 0:
                a_c, a_n = pc, pn
            else:
                a_c, a_n = a_c - lam * pc, a_n - lam * pn
        vc = cbv_ref[:, h * DV_B:(h + 1) * DV_B].astype(BF16)
        vn = pad_rows(nvb_ref[:, h * DV_B:(h + 1) * DV_B])
        o = _dot(a_c.astype(BF16), vc) + _dot(a_n.astype(BF16), vn)
        o = _rms(o, sg_ref[...]) * (1.0 - LAM_INIT)
        ob_ref[:, h * DV_B:(h + 1) * DV_B] = o.astype(BF16)


def _sample(qa, qi, sm, nka, nva, nki, qb, nkb, nvb, cak, cav, cai, cbk, cbv,
            lq1, lk1, lq2, lk2, sg, batch, ts, past, k_top):
    row = lambda b: (b, 0)
    head = lambda b: (0, b, 0)
    cache = lambda b: (b, 0, 0)
    const = lambda b: (0, 0)
    return pl.pallas_call(
        functools.partial(_sample_body, k_top=k_top, past=past, ts=ts),
        out_shape=(jax.ShapeDtypeStruct((batch * ts, H_A * DH_A), BF16),
                   jax.ShapeDtypeStruct((batch * ts, H_B * DV_B), BF16)),
        grid=(batch,),
        in_specs=[
            pl.BlockSpec((H_A, ts, 128), head),
            pl.BlockSpec((H_I, ts, D_IDX), head),
            pl.BlockSpec((ts, 256), row),
            pl.BlockSpec((ts, 128), row),
            pl.BlockSpec((ts, 128), row),
            pl.BlockSpec((ts, D_IDX), row),
            pl.BlockSpec((2 * H_B, ts, DH_B), head),
            pl.BlockSpec((2 * H_B, ts, DH_B), head),
            pl.BlockSpec((ts, H_B * DV_B), row),
            pl.BlockSpec((None, past, DH_A), cache),
            pl.BlockSpec((None, past, DH_A), cache),
            pl.BlockSpec((None, past, D_IDX), cache),
            pl.BlockSpec((None, past, 2 * H_B * DH_B), cache),
            pl.BlockSpec((None, past, H_B * DV_B), cache),
            pl.BlockSpec((1, DH_B), const),
            pl.BlockSpec((1, DH_B), const),
            pl.BlockSpec((1, DH_B), const),
            pl.BlockSpec((1, DH_B), const),
            pl.BlockSpec((1, DV_B), const),
        ],
        out_specs=(pl.BlockSpec((ts, H_A * DH_A), row), pl.BlockSpec((ts, H_B * DV_B), row)),
        scratch_shapes=[
            pltpu.VMEM((H_A * ts, past), BF16),
            pltpu.VMEM((H_A * ts, LANES), BF16),
        ],
        compiler_params=pltpu.CompilerParams(
            dimension_semantics=("arbitrary",), vmem_limit_bytes=VMEM_LIMIT),
        name="sample",
    )(qa, qi, sm, nka, nva, nki, qb, nkb, nvb, cak, cav, cai, cbk, cbv, lq1, lk1, lq2, lk2, sg)


def _finish_body(x_ref, oa_ref, ob_ref, wo_ref, gm_ref, wu_ref, wd_ref, gf_ref, y_ref):
    o = jnp.concatenate([oa_ref[...], ob_ref[...]], axis=1)
    h1 = x_ref[...] + _dot(o, wo_ref[...])
    hn = _rms(h1, gm_ref[...]).astype(BF16)
    acc = h1
    for c in range(D_FF // D_MODEL):
        u = jnp.maximum(_dot(hn, wu_ref[:, c * D_MODEL:(c + 1) * D_MODEL]), 0.0)
        acc = acc + _dot((u * u).astype(BF16), wd_ref[c * D_MODEL:(c + 1) * D_MODEL, :])
    y_ref[...] = _rms(acc, gf_ref[...])


def _finish(x, oa, ob, wo, gm, wu, wd, gf, tm):
    m = x.shape[0]
    assert m % tm == 0
    row = lambda i: (i, 0)
    const = lambda i: (0, 0)
    resident = dict(pipeline_mode=pl.Buffered(1))
    return pl.pallas_call(
        _finish_body,
        out_shape=jax.ShapeDtypeStruct((m, D_MODEL), F32),
        grid=(m // tm,),
        in_specs=[
            pl.BlockSpec((tm, D_MODEL), row),
            pl.BlockSpec((tm, 512), row),
            pl.BlockSpec((tm, 512), row),
            pl.BlockSpec((D_MODEL, D_MODEL), const, **resident),
            pl.BlockSpec((1, D_MODEL), const),
            pl.BlockSpec((D_MODEL, D_FF), const, **resident),
            pl.BlockSpec((D_FF, D_MODEL), const, **resident),
            pl.BlockSpec((1, D_MODEL), const),
        ],
        out_specs=pl.BlockSpec((tm, D_MODEL), row),
        compiler_params=pltpu.CompilerParams(
            dimension_semantics=("arbitrary",), vmem_limit_bytes=VMEM_LIMIT),
        name="finish",
    )(x, oa, ob, wo, gm, wu, wd, gf)


def _pad_rows_to(x, rows, axis):
    widths = [(0, 0)] * x.ndim
    widths[axis] = (0, rows - x.shape[axis])
    return jnp.pad(x, widths)


def kernel(x_prompt, x_sample, cache_a_k, cache_a_v, cache_a_idx_k, cache_b_k, cache_b_v,
           meta_tokens, attn_norm_g, w_in, idx_k_norm_g, idx_k_norm_b,
           lambda_q1, lambda_k1, lambda_q2, lambda_k2, subln_g, w_o,
           mlp_norm_g, w_up, w_down, final_norm_g):
    batch, seq, _ = x_prompt.shape
    dec_batch, ts, _ = x_sample.shape
    past = cache_a_k.shape[2]
    assert attn_norm_g.shape[0] == 1, "single-layer step"
    assert seq % TQ == 0 and ts == 16 and past % LANES == 0 and past // CHUNK == (past + ts - 1) // CHUNK
    n = N_META + seq
    k_top_p = min(TOPK_MAX, seq // 4)
    k_top_s = min(TOPK_MAX, (past + ts) // 4)

    w = w_in[0]
    w = jnp.concatenate(
        [w[:, 0:512],
         w[:, 964:1476],
         w[:, 1476:1988],
         w[:, 1988:2500],
         w[:, 640:896],
         w[:, 512:640],
         w[:, 896:960],
         w[:, 960:964],
         jnp.zeros((D_MODEL, W_COLS - 2500), w.dtype)], axis=1).astype(BF16)
    wo = w_o[0].astype(BF16)
    wu = w_up[0].astype(BF16)
    wd = w_down[0].astype(BF16)
    g_attn = attn_norm_g[0][None]
    g_mlp = mlp_norm_g[0][None]
    g_fin = final_norm_g[None]
    kng = idx_k_norm_g[0][None]
    knb = idx_k_norm_b[0][None]
    lq1, lk1, lq2, lk2 = lambda_q1[0][None], lambda_k1[0][None], lambda_q2[0][None], lambda_k2[0][None]
    sg = subln_g[0][None]

    xp = x_prompt.reshape(batch * seq, D_MODEL)
    xs = x_sample.reshape(dec_batch * ts, D_MODEL)

    (qa_p, qi_p, qb_p, kbh_p, kb_p, vb_p, vbb_p, sm_p, kab_p, _, kib_p, vt_p, wt_p) = _proj(
        xp, g_attn, w, kng, knb, 512, N_META, seq, True)
    (_, _, _, kbh_m, kb_m, vb_m, vbb_m, sm_m, kab_m, vaa_m, kib_m) = _proj(
        meta_tokens, g_attn, w, kng, knb, N_META, 0, N_META, False)
    (qa_s, qi_s, qb_s, kbh_s, kb_s, vb_s, vbb_s, sm_s, kab_s, vaa_s, kib_s) = _proj(
        xs, g_attn, w, kng, knb, dec_batch * ts, past, dec_batch * ts, False)

    oa_p = _dsa(qa_p, qi_p, wt_p, kab_p, kib_p, vt_p, kab_m, kib_m, vaa_m, batch, seq, k_top_p)
    ob_p = _diff(qb_p, kbh_p, vbb_p, _pad_rows_to(kbh_m, LANES, 1), _pad_rows_to(vbb_m, LANES, 0),
                 lq1, lk1, lq2, lk2, sg, batch, seq)
    y_prompt = _finish(xp, oa_p, ob_p, wo, g_mlp, wu, wd, g_fin, 512).reshape(batch, seq, D_MODEL)

    oa_s, ob_s = _sample(
        qa_s, qi_s, sm_s, kab_s, vaa_s, kib_s, qb_s, kbh_s, vbb_s,
        cache_a_k[0], cache_a_v[0], cache_a_idx_k[0],
        cache_b_k[0].reshape(dec_batch, past, 2 * H_B * DH_B),
        cache_b_v[0].reshape(dec_batch, past, H_B * DV_B),
        lq1, lk1, lq2, lk2, sg, dec_batch, ts, past, k_top_s)
    y_sample = _finish(xs, oa_s, ob_s, wo, g_mlp, wu, wd, g_fin, dec_batch * ts).reshape(
        dec_batch, ts, D_MODEL)

    def with_meta(meta_rows, main_rows, width):
        meta_b = jnp.broadcast_to(meta_rows[None], (batch, N_META, width))
        return jnp.concatenate([meta_b, main_rows.reshape(batch, seq, width)], axis=1)[None]

    new_a_k_p = with_meta(sm_m[:, SM_KA:SM_KA + 64], sm_p[:, SM_KA:SM_KA + 64], 64)
    new_a_v_p = with_meta(sm_m[:, SM_VA:SM_VA + 64], sm_p[:, SM_VA:SM_VA + 64], 64)
    new_a_i_p = with_meta(sm_m[:, SM_KI:SM_KI + 64], sm_p[:, SM_KI:SM_KI + 64], 64)
    new_b_k_p = with_meta(kb_m, kb_p, 512).reshape(1, batch, n, H_B, 2, DH_B)
    new_b_v_p = with_meta(vb_m, vb_p, 512).reshape(1, batch, n, H_B, DV_B)
    new_a_k_s = sm_s[:, SM_KA:SM_KA + 64].reshape(1, dec_batch, ts, 64)
    new_a_v_s = sm_s[:, SM_VA:SM_VA + 64].reshape(1, dec_batch, ts, 64)
    new_a_i_s = sm_s[:, SM_KI:SM_KI + 64].reshape(1, dec_batch, ts, 64)
    new_b_k_s = kb_s.reshape(1, dec_batch, ts, H_B, 2, DH_B)
    new_b_v_s = vb_s.reshape(1, dec_batch, ts, H_B, DV_B)
    return (y_prompt, y_sample, new_a_k_p, new_a_v_p, new_a_i_p, new_b_k_p, new_b_v_p,
            new_a_k_s, new_a_v_s, new_a_i_s, new_b_k_s, new_b_v_s)
```

```python
import functools

import jax
import jax.numpy as jnp
from jax import lax
from jax.experimental import pallas as pl
from jax.experimental.pallas import tpu as pltpu

F32 = jnp.float32
BF16 = jnp.bfloat16
I32 = jnp.int32
I16 = jnp.int16
HALF16 = 32768

D_MODEL = 1024
CHUNK = 64
N_META = 16
H_A = 8
DH_A = 64
H_I = 4
D_IDX = 64
TOPK_MAX = 256
H_B = 4
DH_B = 64
DV_B = 2 * DH_B
D_FF = 4 * D_MODEL
EPS = 1e-6
LAM_INIT = 0.2

LANES = 128
C_QA, C_QB, C_KB, C_VB, C_QI, C_SM = 0, 512, 1024, 1536, 2048, 2304
W_COLS = 2560
SM_KA, SM_VA, SM_KI, SM_WI = 0, 64, 128, 192

Q_SCALE = DH_A ** -0.5
W_SCALE = (H_I ** -0.5) * (D_IDX ** -0.5)
SLOPES_A = tuple(2.0 ** (-8.0 * (i + 1) / H_A) for i in range(H_A))
SLOPES_B = tuple(2.0 ** (-8.0 * (i + 1) / H_B) for i in range(H_B))

INT_MIN = -(2 ** 31)
NEG_INIT = -1e30
TQ = 256
VT_ROWS = DV_B + 16
CNT_ROWS = 32
VMEM_LIMIT = 56 * 1024 * 1024

_NT = (((1,), (1,)), ((), ()))


def _dot(a, b):
    return jnp.dot(a, b, preferred_element_type=F32)


def _dot_nt(a, b):
    return lax.dot_general(a, b, _NT, preferred_element_type=F32)


def _rms(x, g):
    return (x * lax.rsqrt(jnp.mean(x * x, axis=-1, keepdims=True) + EPS)) * g


def _to_key(score):
    b = lax.bitcast_convert_type(score, I32)
    return b ^ ((b >> 31) & jnp.int32(0x7FFFFFFF))


def _f32_from_key(key):
    return lax.bitcast_convert_type(key ^ ((key >> 31) & jnp.int32(0x7FFFFFFF)), F32)


def _bf16_from_key(key):
    bits = key ^ ((key >> 15) & jnp.int32(0x7FFF))
    return lax.bitcast_convert_type(jnp.left_shift(bits, 16), F32).astype(BF16)


KEY16_LOWEST = -32640
KEY32_LOWEST = -2139095040
KEY32_HIGHEST = 2139095039


def _wide(x, width):
    reps = width // LANES
    return x if reps == 1 else jnp.concatenate([x] * reps, axis=1)


def _proj_body(x_ref, g_ref, w_ref, wvt_ref, kng_ref, knb_ref,
               qa_ref, qi_ref, qb_ref, kbh_ref, kb_ref, vb_ref, vbb_ref,
               sm_ref, kab_ref, vaa_ref, kib_ref, *maybe_t_refs, pos0, period):
    x = x_ref[...]
    hn = _rms(x, g_ref[...]).astype(BF16)
    rows = x.shape[0]

    def mm(lo, width):
        return _dot(hn, w_ref[:, lo:lo + width])

    lane = lax.broadcasted_iota(I32, (rows, 64), 1)
    r = pl.program_id(0) * rows + lax.broadcasted_iota(I32, (rows, 64), 0)
    pos = pos0 + r % period
    pos_cols = jnp.where(lane == 0, (pos // 256) * 256, jnp.where(lane == 1, pos % 256, 0))
    pos_cols = pos_cols.astype(F32).astype(BF16)

    def slope_cols(slope):
        return jnp.where(lane < 2, slope, 0.0).astype(BF16)

    z = mm(C_QA, 512) * Q_SCALE
    for h in range(H_A):
        qa_ref[h] = jnp.concatenate(
            [z[:, h * DH_A:(h + 1) * DH_A].astype(BF16), slope_cols(SLOPES_A[h])], axis=1)
    z = mm(C_QB, 512) * Q_SCALE
    for hc in range(2 * H_B):
        qb_ref[hc] = jnp.concatenate(
            [z[:, hc * DH_B:(hc + 1) * DH_B].astype(BF16), slope_cols(SLOPES_B[hc // 2])], axis=1)
    z = mm(C_KB, 512)
    kb_ref[...] = z
    for hc in range(2 * H_B):
        kbh_ref[hc] = jnp.concatenate([z[:, hc * DH_B:(hc + 1) * DH_B].astype(BF16), pos_cols], axis=1)
    z = mm(C_VB, 512)
    vb_ref[...] = z
    vbb_ref[...] = z.astype(BF16)
    z = mm(C_QI, 256)
    for h in range(H_I):
        qi_ref[h] = z[:, h * D_IDX:(h + 1) * D_IDX].astype(BF16)
    z = mm(C_SM, 256)
    ka = z[:, SM_KA:SM_KA + 64]
    va = z[:, SM_VA:SM_VA + 64]
    ki = z[:, SM_KI:SM_KI + 64]
    xc = ki - jnp.mean(ki, axis=-1, keepdims=True)
    ki = xc * lax.rsqrt(jnp.mean(xc * xc, axis=-1, keepdims=True) + EPS) * kng_ref[...] + knb_ref[...]
    sm_ref[:, 0:128] = z[:, 0:128]
    sm_ref[:, SM_KI:SM_KI + 64] = ki
    sm_ref[:, SM_WI:SM_WI + 64] = z[:, SM_WI:SM_WI + 64] * W_SCALE
    kab_ref[...] = jnp.concatenate([ka.astype(BF16), pos_cols], axis=1)
    kib_ref[...] = ki.astype(BF16)
    vaa_ref[...] = jnp.concatenate([va.astype(BF16), jnp.ones((rows, 64), BF16)], axis=1)
    if maybe_t_refs:
        vt_ref, wt_ref, vbt_ref, smt_ref, kbt_ref = maybe_t_refs
        kv_t = z[:, 0:128].T
        for c in range(rows // TQ):
            vt_ref[c] = jnp.concatenate(
                [kv_t[64:128, c * TQ:(c + 1) * TQ].astype(BF16), jnp.ones((64, TQ), BF16)], axis=0)
        kw_t = jnp.concatenate([ki, z[:, SM_WI:SM_WI + 64] * W_SCALE], axis=1).T
        wt_ref[...] = kw_t[64:72, :]
        smt_ref[0:128, :] = kv_t
        smt_ref[128:192, :] = kw_t[0:64, :]
        kbt_ref[...] = _dot_nt(wvt_ref[0:512, :], hn)
        vb_t = _dot_nt(wvt_ref[512:1024, :], hn)
        for c in range(rows // TQ):
            for h in range(H_B):
                vbt_ref[c, h] = jnp.concatenate(
                    [vb_t[h * DV_B:(h + 1) * DV_B, c * TQ:(c + 1) * TQ].astype(BF16),
                     jnp.ones((VT_ROWS - DV_B, TQ), BF16)], axis=0)


def _proj(x, g, w, wvt, kng, knb, tm, pos0, period, key_major):
    m = x.shape[0]
    assert m % tm == 0
    row = lambda i: (i, 0)
    head = lambda i: (0, i, 0)
    const = lambda i: (0, 0)
    out_shape = (
        jax.ShapeDtypeStruct((H_A, m, 128), BF16),
        jax.ShapeDtypeStruct((H_I, m, D_IDX), BF16),
        jax.ShapeDtypeStruct((2 * H_B, m, 128), BF16),
        jax.ShapeDtypeStruct((2 * H_B, m, 128), BF16),
        jax.ShapeDtypeStruct((m, 512), F32),
        jax.ShapeDtypeStruct((m, 512), F32),
        jax.ShapeDtypeStruct((m, 512), BF16),
        jax.ShapeDtypeStruct((m, 256), F32),
        jax.ShapeDtypeStruct((m, 128), BF16),
        jax.ShapeDtypeStruct((m, 128), BF16),
        jax.ShapeDtypeStruct((m, 64), BF16),
    )
    out_specs = (
        pl.BlockSpec((H_A, tm, 128), head),
        pl.BlockSpec((H_I, tm, D_IDX), head),
        pl.BlockSpec((2 * H_B, tm, 128), head),
        pl.BlockSpec((2 * H_B, tm, 128), head),
        pl.BlockSpec((tm, 512), row),
        pl.BlockSpec((tm, 512), row),
        pl.BlockSpec((tm, 512), row),
        pl.BlockSpec((tm, 256), row),
        pl.BlockSpec((tm, 128), row),
        pl.BlockSpec((tm, 128), row),
        pl.BlockSpec((tm, 64), row),
    )
    if key_major:
        assert tm % TQ == 0
        out_shape += (
            jax.ShapeDtypeStruct((m // TQ, 128, TQ), BF16),
            jax.ShapeDtypeStruct((8, m), F32),
            jax.ShapeDtypeStruct((m // TQ, H_B, VT_ROWS, TQ), BF16),
            jax.ShapeDtypeStruct((192, m), F32),
            jax.ShapeDtypeStruct((512, m), F32),
        )
        out_specs += (
            pl.BlockSpec((tm // TQ, 128, TQ), lambda i: (i, 0, 0)),
            pl.BlockSpec((8, tm), lambda i: (0, i)),
            pl.BlockSpec((tm // TQ, H_B, VT_ROWS, TQ), lambda i: (i, 0, 0, 0)),
            pl.BlockSpec((192, tm), lambda i: (0, i)),
            pl.BlockSpec((512, tm), lambda i: (0, i)),
        )
    return pl.pallas_call(
        functools.partial(_proj_body, pos0=pos0, period=period),
        out_shape=out_shape,
        grid=(m // tm,),
        in_specs=[
            pl.BlockSpec((tm, D_MODEL), row),
            pl.BlockSpec((1, D_MODEL), const),
            pl.BlockSpec((D_MODEL, W_COLS), const),
            pl.BlockSpec((2 * H_B * DV_B, D_MODEL), const),
            pl.BlockSpec((1, D_IDX), const),
            pl.BlockSpec((1, D_IDX), const),
        ],
        out_specs=out_specs,
        compiler_params=pltpu.CompilerParams(
            dimension_semantics=("arbitrary",), vmem_limit_bytes=VMEM_LIMIT),
        name="proj",
    )(x, g, w, wvt, kng, knb)


def _dsa_body(qa_ref, qi_ref, wt_ref, ka_ref, ki_ref, vt_ref, mka_ref, mki_ref, mva_ref,
              o_ref, smeta_ref, smain_ref, bmeta_ref, bmain_ref,
              tsel_ref, m_ref, al_ref, acc_ref, p_ref, s_ref, seen_ref, *, k_top):
    i = pl.program_id(1)
    krow = lax.broadcasted_iota(I32, (TQ, TQ), 0)
    qcol = lax.broadcasted_iota(I32, (TQ, TQ), 1)

    def main_rows(j):
        return pl.ds(pl.multiple_of(j * TQ, TQ), TQ)

    qi_all = qi_ref[...].reshape(H_I * TQ, D_IDX)

    def scores(ki_blk):
        s4 = _dot_nt(ki_blk, qi_all)
        sc = None
        for h in range(H_I):
            t = jnp.maximum(s4[:, h * TQ:(h + 1) * TQ], 0.0) * wt_ref[h:h + 1, :]
            sc = t if sc is None else sc + t
        return sc

    sc = scores(mki_ref[...])
    smeta_ref[...] = sc
    bmeta_ref[...] = sc.astype(BF16)

    def score_block(j, sc):
        smain_ref[j] = sc
        bmain_ref[j] = sc.astype(BF16)

    def score_full(j, carry):
        score_block(j, scores(ki_ref[main_rows(j), :]))
        return carry

    lax.fori_loop(0, i, score_full, 0)
    score_block(i, jnp.where((krow // CHUNK) <= (qcol // CHUNK),
                             scores(ki_ref[main_rows(i), :]), -jnp.inf))

    def count(meta_ref, main_ref, part, pred):
        def body(j, c):
            return c + part(pred(main_ref[j]))

        c = lax.fori_loop(0, i + 1, body, jnp.zeros((CNT_ROWS, TQ), F32))
        cm = part(pred(meta_ref[...]), rows=N_META)
        return jnp.sum(c, axis=0, keepdims=True) + jnp.sum(cm, axis=0, keepdims=True)

    def part32(hit, rows=CNT_ROWS):
        ones = jnp.where(hit, 1.0, 0.0)
        return jnp.sum(ones.reshape(ones.shape[0] // rows, rows, TQ), axis=0)

    def part16(hit, rows=CNT_ROWS):
        ones = jnp.where(hit, jnp.bfloat16(1), jnp.bfloat16(0))
        acc = ones[0:rows]
        for t in range(1, ones.shape[0] // rows):
            acc = acc + ones[t * rows:(t + 1) * rows]
        return acc.astype(F32)

    kf = float(k_top)

    def step16(s, u):
        cand_u = u | jnp.left_shift(jnp.int32(1), 15 - s)
        cand = _bf16_from_key(jnp.maximum(cand_u, KEY16_LOWEST + HALF16) - HALF16)
        cnt = count(bmeta_ref, bmain_ref, part16, lambda x: x >= cand)
        return jnp.where(cnt >= kf, cand_u, u)

    u_hi = lax.fori_loop(0, 16, step16, jnp.zeros((1, TQ), I32))
    t_hi = _bf16_from_key(jnp.maximum(u_hi, KEY16_LOWEST + HALF16) - HALF16).astype(F32)
    base = jnp.maximum(_to_key(t_hi) - 65536, KEY32_LOWEST)

    def step32(s, d):
        cand_d = d | jnp.left_shift(jnp.int32(1), 16 - s)
        cand = _f32_from_key(jnp.minimum(base + cand_d, KEY32_HIGHEST))
        cnt = count(smeta_ref, smain_ref, part32, lambda x: x >= cand)
        return jnp.where(cnt >= kf, cand_d, d)

    d_lo = lax.fori_loop(0, 17, step32, jnp.zeros((1, TQ), I32))
    t_sel = _f32_from_key(jnp.minimum(base + d_lo, KEY32_HIGHEST))
    need = kf - count(smeta_ref, smain_ref, part32, lambda x: x > t_sel)
    tsel_ref[...] = jnp.broadcast_to(t_sel, tsel_ref.shape)

    q_all = qa_ref[...].reshape(H_A * TQ, 128)
    m_ref[...] = jnp.full(m_ref.shape, NEG_INIT, F32)
    acc_ref[...] = jnp.zeros(acc_ref.shape, F32)
    thr = tsel_ref[0:1, :]

    def logits(k_blk):
        return _dot_nt(k_blk, q_all)

    def softmax(sc, s_all, future, seen, slot):
        nk = sc.shape[0]
        tie = sc == thr
        tri = (lax.broadcasted_iota(I32, (nk, nk), 1)
               <= lax.broadcasted_iota(I32, (nk, nk), 0)).astype(BF16)
        rank = _dot(tri, jnp.where(tie, 1.0, 0.0).astype(BF16)) + seen
        negm = jnp.where(sc > thr, 0.0, jnp.where(tie, jnp.where(rank <= need, 0.0, -jnp.inf), -jnp.inf))
        for h in range(H_A):
            cols = slice(h * TQ, (h + 1) * TQ)
            s = s_all[:, cols] + negm
            if future is not None:
                s = s - (2.0 * SLOPES_A[h]) * future
            m_old = m_ref[:, cols]
            m_new = jnp.maximum(m_old, jnp.max(s, axis=0, keepdims=True))
            p_ref[slot, 0:nk, cols] = jnp.exp(s - m_new).astype(BF16)
            al_ref[slot, :, cols] = jnp.exp(m_old - m_new)
            m_ref[:, cols] = m_new
        return rank[nk - 1:nk, :]

    def accumulate(vt_blk, slot, nk):
        acc_ref[...] = acc_ref[...] * al_ref[slot] + _dot(vt_blk, p_ref[slot, 0:nk, :])

    eye = (lax.broadcasted_iota(I32, (128, 128), 0)
           == lax.broadcasted_iota(I32, (128, 128), 1)).astype(BF16)
    seen_ref[...] = softmax(smeta_ref[...], logits(mka_ref[...]), None, jnp.zeros((1, TQ), F32), 0)
    accumulate(_dot_nt(eye, mva_ref[...]).astype(BF16), 0, N_META)

    p_ref[1] = jnp.zeros(p_ref.shape[1:], BF16)
    al_ref[1] = jnp.ones(al_ref.shape[1:], F32)
    s_ref[0] = logits(ka_ref[main_rows(0), :])

    def full_step(t, slot):
        s_ref[1 - slot] = logits(ka_ref[main_rows(t + 1), :])
        accumulate(vt_ref[jnp.maximum(t - 1, 0)], 1 - slot, TQ)
        seen_ref[...] = softmax(smain_ref[t], s_ref[slot], None, seen_ref[...], slot)

    def two_steps(u, carry):
        full_step(2 * u, 0)
        full_step(2 * u + 1, 1)
        return carry

    lax.fori_loop(0, i // 2, two_steps, 0)

    def last_step(slot):
        accumulate(vt_ref[jnp.maximum(i - 1, 0)], 1 - slot, TQ)
        softmax(smain_ref[i], s_ref[slot], jnp.maximum(krow - qcol, 0).astype(F32),
                seen_ref[...], slot)
        accumulate(vt_ref[i], slot, TQ)

    @pl.when(i % 2 == 1)
    def _odd():
        full_step(i - 1, 0)
        last_step(1)

    @pl.when(i % 2 == 0)
    def _even():
        last_step(0)

    outs = []
    for h in range(H_A):
        a = acc_ref[:, h * TQ:(h + 1) * TQ].T
        o = a / pltpu.roll(a, DH_A, axis=1)
        outs.append(o[:, :DH_A])
    o_ref[...] = jnp.concatenate(outs, axis=1).astype(BF16)


def _dsa(qa, qi, wt, kab, kib, vt, mka, mki, mva, batch, seq, k_top):
    nq = seq // TQ
    qrow = lambda b, i: (b * nq + i, 0)
    qhead = lambda b, i: (0, b * nq + i, 0)
    kv = lambda b, i: (b, 0)
    const = lambda b, i: (0, 0)
    return pl.pallas_call(
        functools.partial(_dsa_body, k_top=k_top),
        out_shape=jax.ShapeDtypeStruct((batch * seq, H_A * DH_A), BF16),
        grid=(batch, nq),
        in_specs=[
            pl.BlockSpec((H_A, TQ, 128), qhead),
            pl.BlockSpec((H_I, TQ, D_IDX), qhead),
            pl.BlockSpec((8, TQ), lambda b, i: (0, b * nq + i)),
            pl.BlockSpec((seq, 128), kv),
            pl.BlockSpec((seq, D_IDX), kv),
            pl.BlockSpec((nq, 128, TQ), lambda b, i: (b, 0, 0)),
            pl.BlockSpec((N_META, 128), const),
            pl.BlockSpec((N_META, D_IDX), const),
            pl.BlockSpec((N_META, 128), const),
        ],
        out_specs=pl.BlockSpec((TQ, H_A * DH_A), qrow),
        scratch_shapes=[
            pltpu.VMEM((N_META, TQ), F32),
            pltpu.VMEM((nq, TQ, TQ), F32),
            pltpu.VMEM((N_META, TQ), BF16),
            pltpu.VMEM((nq, TQ, TQ), BF16),
            pltpu.VMEM((8, TQ), F32),
            pltpu.VMEM((1, H_A * TQ), F32),
            pltpu.VMEM((2, 1, H_A * TQ), F32),
            pltpu.VMEM((128, H_A * TQ), F32),
            pltpu.VMEM((2, TQ, H_A * TQ), BF16),
            pltpu.VMEM((2, TQ, H_A * TQ), F32),
            pltpu.VMEM((1, TQ), F32),
        ],
        compiler_params=pltpu.CompilerParams(
            dimension_semantics=("arbitrary", "arbitrary"), vmem_limit_bytes=VMEM_LIMIT),
        name="dsa",
    )(qa, qi, wt, kab, kib, vt, mka, mki, mva)


def _lambda(lq1_ref, lk1_ref, lq2_ref, lk2_ref):
    s1 = jnp.sum(lq1_ref[...] * lk1_ref[...], axis=-1, keepdims=True)
    s2 = jnp.sum(lq2_ref[...] * lk2_ref[...], axis=-1, keepdims=True)
    return jnp.exp(s1) - jnp.exp(s2) + LAM_INIT


def _diff_body(qb_ref, kb_ref, vt_ref, mkb_ref, mvb_ref, lq1_ref, lk1_ref, lq2_ref, lk2_ref,
               sg_ref, o_ref, m_ref, al_ref, acc_ref, p_ref, s_ref):
    i = pl.program_id(1)
    n_hc = 2 * H_B
    krow = lax.broadcasted_iota(I32, (TQ, TQ), 0)
    qcol = lax.broadcasted_iota(I32, (TQ, TQ), 1)
    m_ref[...] = jnp.full(m_ref.shape, NEG_INIT, F32)
    acc_ref[...] = jnp.zeros(acc_ref.shape, F32)

    def main_rows(j):
        return pl.ds(pl.multiple_of(j * TQ, TQ), TQ)

    def logits(k_of, slot, nk):
        for hc in range(n_hc):
            s_ref[slot, hc, 0:nk, :] = _dot_nt(k_of(hc), qb_ref[hc])

    def softmax(slot, nk, negm, future):
        for hc in range(n_hc):
            cols = slice(hc * TQ, (hc + 1) * TQ)
            s = s_ref[slot, hc, 0:nk, :]
            if negm is not None:
                s = s + negm
            if future is not None:
                s = s - (2.0 * SLOPES_B[hc // 2]) * future
            m_old = m_ref[:, cols]
            m_new = jnp.maximum(m_old, jnp.max(s, axis=0, keepdims=True))
            p_ref[slot, 0:nk, cols] = jnp.exp(s - m_new).astype(BF16)
            al_ref[slot, :, cols] = jnp.exp(m_old - m_new)
            m_ref[:, cols] = m_new

    def accumulate(vt_of, slot, nk):
        for h in range(H_B):
            cols = slice(2 * h * TQ, (2 * h + 2) * TQ)
            acc_ref[h] = acc_ref[h] * al_ref[slot, :, cols] + _dot(vt_of(h), p_ref[slot, 0:nk, cols])

    eye = (lax.broadcasted_iota(I32, (DV_B, DV_B), 0)
           == lax.broadcasted_iota(I32, (DV_B, DV_B), 1)).astype(BF16)

    def meta_vt(h):
        v_t = _dot_nt(eye, mvb_ref[:, h * DV_B:(h + 1) * DV_B]).astype(BF16)
        return jnp.concatenate([v_t, jnp.ones((VT_ROWS - DV_B, N_META), BF16)], axis=0)

    logits(lambda hc: mkb_ref[hc], 0, N_META)
    softmax(0, N_META, None, None)
    accumulate(meta_vt, 0, N_META)

    p_ref[1] = jnp.zeros(p_ref.shape[1:], BF16)
    al_ref[1] = jnp.ones(al_ref.shape[1:], F32)
    logits(lambda hc: kb_ref[hc, main_rows(0), :], 0, TQ)

    def main_vt(t):
        return lambda h: vt_ref[t, h]

    def full_step(t, slot):
        logits(lambda hc: kb_ref[hc, main_rows(t + 1), :], 1 - slot, TQ)
        accumulate(main_vt(jnp.maximum(t - 1, 0)), 1 - slot, TQ)
        softmax(slot, TQ, None, None)

    def two_steps(u, carry):
        full_step(2 * u, 0)
        full_step(2 * u + 1, 1)
        return carry

    lax.fori_loop(0, i // 2, two_steps, 0)

    def last_step(slot):
        accumulate(main_vt(jnp.maximum(i - 1, 0)), 1 - slot, TQ)
        softmax(slot, TQ, jnp.where((krow // CHUNK) <= (qcol // CHUNK), 0.0, -jnp.inf),
                jnp.maximum(krow - qcol, 0).astype(F32))
        accumulate(main_vt(i), slot, TQ)

    @pl.when(i % 2 == 1)
    def _odd():
        full_step(i - 1, 0)
        last_step(1)

    @pl.when(i % 2 == 0)
    def _even():
        last_step(0)

    lam = _lambda(lq1_ref, lk1_ref, lq2_ref, lk2_ref)
    for h in range(H_B):
        a = acc_ref[h]
        o0 = a[0:DV_B, 0:TQ] / a[DV_B:DV_B + 1, 0:TQ]
        o1 = a[0:DV_B, TQ:2 * TQ] / a[DV_B:DV_B + 1, TQ:2 * TQ]
        o = _rms((o0 - lam * o1).T, sg_ref[...]) * (1.0 - LAM_INIT)
        o_ref[:, h * DV_B:(h + 1) * DV_B] = o.astype(BF16)


def _diff(qb, kbh, vbt, mkb, mvb, lq1, lk1, lq2, lk2, sg, batch, seq):
    nq = seq // TQ
    qrow = lambda b, i: (b * nq + i, 0)
    qhead = lambda b, i: (0, b * nq + i, 0)
    const2 = lambda b, i: (0, 0)
    const3 = lambda b, i: (0, 0, 0)
    return pl.pallas_call(
        _diff_body,
        out_shape=jax.ShapeDtypeStruct((batch * seq, H_B * DV_B), BF16),
        grid=(batch, nq),
        in_specs=[
            pl.BlockSpec((2 * H_B, TQ, 128), qhead),
            pl.BlockSpec((2 * H_B, seq, 128), lambda b, i: (0, b, 0)),
            pl.BlockSpec((nq, H_B, VT_ROWS, TQ), lambda b, i: (b, 0, 0, 0)),
            pl.BlockSpec((2 * H_B, N_META, 128), const3),
            pl.BlockSpec((N_META, H_B * DV_B), const2),
            pl.BlockSpec((1, DH_B), const2),
            pl.BlockSpec((1, DH_B), const2),
            pl.BlockSpec((1, DH_B), const2),
            pl.BlockSpec((1, DH_B), const2),
            pl.BlockSpec((1, DV_B), const2),
        ],
        out_specs=pl.BlockSpec((TQ, H_B * DV_B), qrow),
        scratch_shapes=[
            pltpu.VMEM((1, 2 * H_B * TQ), F32),
            pltpu.VMEM((2, 1, 2 * H_B * TQ), F32),
            pltpu.VMEM((H_B, VT_ROWS, 2 * TQ), F32),
            pltpu.VMEM((2, TQ, 2 * H_B * TQ), BF16),
            pltpu.VMEM((2, 2 * H_B, TQ, TQ), F32),
        ],
        compiler_params=pltpu.CompilerParams(
            dimension_semantics=("arbitrary", "arbitrary"), vmem_limit_bytes=VMEM_LIMIT),
        name="diff",
    )(qb, kbh, vbt, mkb, mvb, lq1, lk1, lq2, lk2, sg)


def _sample_body(qa_ref, qi_ref, sm_ref, nka_ref, nva_ref, nki_ref, qb_ref, nkb_ref, nvb_ref,
                 cak_ref, cav_ref, cai_ref, cbk_ref, cbv_ref,
                 lq1_ref, lk1_ref, lq2_ref, lk2_ref, sg_ref,
                 oa_ref, ob_ref, pc_ref, pn_ref, *, k_top, past, ts):
    pad = LANES - ts
    row_c = lax.broadcasted_iota(I32, (ts, past), 0)
    col_c = lax.broadcasted_iota(I32, (ts, past), 1)
    row_n = lax.broadcasted_iota(I32, (ts, LANES), 0)
    col_n = lax.broadcasted_iota(I32, (ts, LANES), 1)
    new_ok = col_n < ts

    def pad_rows(x):
        return jnp.concatenate([x, jnp.zeros((pad,) + x.shape[1:], x.dtype)], axis=0)

    qi_all = qi_ref[...].reshape(H_I * ts, D_IDX)
    w = sm_ref[:, SM_WI:SM_WI + H_I]
    s4c = _dot(qi_all, cai_ref[...].astype(BF16))
    s4n = _dot_nt(qi_all, pad_rows(nki_ref[...]))
    sc_c = None
    sc_n = None
    for h in range(H_I):
        wh = w[:, h:h + 1]
        tc = jnp.maximum(s4c[h * ts:(h + 1) * ts], 0.0) * wh
        tn = jnp.maximum(s4n[h * ts:(h + 1) * ts], 0.0) * wh
        sc_c = tc if sc_c is None else sc_c + tc
        sc_n = tn if sc_n is None else sc_n + tn
    sc_n = jnp.where(new_ok, sc_n, -jnp.inf)

    def count(pred_c, pred_n):
        c = jnp.sum(jnp.where(pred_c, 1.0, 0.0), axis=1, keepdims=True)
        return c + jnp.sum(jnp.where(pred_n, 1.0, 0.0), axis=1, keepdims=True)

    def threshold(key):
        return _f32_from_key(jnp.clip(key, KEY32_LOWEST, KEY32_HIGHEST))

    def value_step(s, key):
        cand_key = key ^ jnp.left_shift(jnp.int32(1), 31 - s)
        cand = threshold(cand_key)
        cnt = count(sc_c >= cand, sc_n >= cand)
        return jnp.where(cnt >= float(k_top), cand_key, key)

    t = threshold(lax.fori_loop(0, 32, value_step, jnp.full((ts, 1), INT_MIN, I32)))
    need = float(k_top) - count(sc_c > t, sc_n > t)
    idx_n = col_n + past

    def tie_step(s, x):
        cand = x | jnp.left_shift(jnp.int32(1), 11 - s)
        cnt = count((sc_c == t) & (col_c < cand), (sc_n == t) & (idx_n < cand))
        return jnp.where(cnt < need, cand, x)

    x = lax.fori_loop(0, 12, tie_step, jnp.zeros((ts, 1), I32))
    negm_c = jnp.where((sc_c > t) | ((sc_c == t) & (col_c <= x)), 0.0, -jnp.inf)
    negm_n = jnp.where((sc_n > t) | ((sc_n == t) & (idx_n <= x)), 0.0, -jnp.inf)

    dist_c = (row_c - col_c + past).astype(F32)
    dist_n = jnp.abs(row_n - col_n).astype(F32)
    q_all = qa_ref[...][:, :, 0:DH_A].reshape(H_A * ts, DH_A)
    s_c = _dot(q_all, cak_ref[...].astype(BF16))
    s_n = _dot_nt(q_all, pad_rows(nka_ref[...][:, 0:DH_A]))
    for h in range(H_A):
        rows = slice(h * ts, (h + 1) * ts)
        lc = s_c[rows] + (negm_c - SLOPES_A[h] * dist_c)
        ln = s_n[rows] + (negm_n - SLOPES_A[h] * dist_n)
        m = jnp.maximum(jnp.max(lc, axis=1, keepdims=True), jnp.max(ln, axis=1, keepdims=True))
        m = jnp.maximum(m, NEG_INIT)
        pc_ref[rows, :] = jnp.exp(lc - m).astype(BF16)
        pn_ref[rows, :] = jnp.exp(ln - m).astype(BF16)
    vc_aug_t = jnp.concatenate([cav_ref[...].astype(BF16), jnp.ones((64, past), BF16)], axis=0)
    pv = _dot_nt(pc_ref[...], vc_aug_t) + _dot(pn_ref[...], pad_rows(nva_ref[...]))
    pv = pv / pltpu.roll(pv, DH_A, axis=1)
    oa_ref[...] = jnp.concatenate(
        [pv[h * ts:(h + 1) * ts, :DH_A] for h in range(H_A)], axis=1).astype(BF16)

    lam = _lambda(lq1_ref, lk1_ref, lq2_ref, lk2_ref)
    neg_new = jnp.where(new_ok, 0.0, -jnp.inf)
    for h in range(H_B):
        a_c = None
        a_n = None
        for c in range(2):
            hc = 2 * h + c
            q = qb_ref[hc][:, 0:DH_B]
            kc_t = cbk_ref[hc * DH_B:(hc + 1) * DH_B, :].astype(BF16)
            lc = _dot(q, kc_t) - SLOPES_B[h] * dist_c
            ln = _dot_nt(q, pad_rows(nkb_ref[hc][:, 0:DH_B])) + (neg_new - SLOPES_B[h] * dist_n)
            m = jnp.maximum(jnp.max(lc, axis=1, keepdims=True), jnp.max(ln, axis=1, keepdims=True))
            pc = jnp.exp(lc - m)
            pn = jnp.exp(ln - m)
            l = jnp.sum(pc, axis=1, keepdims=True) + jnp.sum(pn, axis=1, keepdims=True)
            pc = pc / l
            pn = pn / l
            if c == 0:
                a_c, a_n = pc, pn
            else:
                a_c, a_n = a_c - lam * pc, a_n - lam * pn
        vc = cbv_ref[:, h * DV_B:(h + 1) * DV_B].astype(BF16)
        vn = pad_rows(nvb_ref[:, h * DV_B:(h + 1) * DV_B])
        o = _dot(a_c.astype(BF16), vc) + _dot(a_n.astype(BF16), vn)
        o = _rms(o, sg_ref[...]) * (1.0 - LAM_INIT)
        ob_ref[:, h * DV_B:(h + 1) * DV_B] = o.astype(BF16)


def _sample(qa, qi, sm, nka, nva, nki, qb, nkb, nvb, cak, cav, cai, cbk, cbv,
            lq1, lk1, lq2, lk2, sg, batch, ts, past, k_top):
    row = lambda b: (b, 0)
    head = lambda b: (0, b, 0)
    cache = lambda b: (b, 0, 0)
    const = lambda b: (0, 0)
    return pl.pallas_call(
        functools.partial(_sample_body, k_top=k_top, past=past, ts=ts),
        out_shape=(jax.ShapeDtypeStruct((batch * ts, H_A * DH_A), BF16),
                   jax.ShapeDtypeStruct((batch * ts, H_B * DV_B), BF16)),
        grid=(batch,),
        in_specs=[
            pl.BlockSpec((H_A, ts, 128), head),
            pl.BlockSpec((H_I, ts, D_IDX), head),
            pl.BlockSpec((ts, 256), row),
            pl.BlockSpec((ts, 128), row),
            pl.BlockSpec((ts, 128), row),
            pl.BlockSpec((ts, D_IDX), row),
            pl.BlockSpec((2 * H_B, ts, 128), head),
            pl.BlockSpec((2 * H_B, ts, 128), head),
            pl.BlockSpec((ts, H_B * DV_B), row),
            pl.BlockSpec((None, DH_A, past), cache),
            pl.BlockSpec((None, DH_A, past), cache),
            pl.BlockSpec((None, D_IDX, past), cache),
            pl.BlockSpec((None, 2 * H_B * DH_B, past), cache),
            pl.BlockSpec((None, past, H_B * DV_B), cache),
            pl.BlockSpec((1, DH_B), const),
            pl.BlockSpec((1, DH_B), const),
            pl.BlockSpec((1, DH_B), const),
            pl.BlockSpec((1, DH_B), const),
            pl.BlockSpec((1, DV_B), const),
        ],
        out_specs=(pl.BlockSpec((ts, H_A * DH_A), row), pl.BlockSpec((ts, H_B * DV_B), row)),
        scratch_shapes=[
            pltpu.VMEM((H_A * ts, past), BF16),
            pltpu.VMEM((H_A * ts, LANES), BF16),
        ],
        compiler_params=pltpu.CompilerParams(
            dimension_semantics=("arbitrary",), vmem_limit_bytes=VMEM_LIMIT),
        name="sample",
    )(qa, qi, sm, nka, nva, nki, qb, nkb, nvb, cak, cav, cai, cbk, cbv, lq1, lk1, lq2, lk2, sg)


def _finish_body(x_ref, oa_ref, ob_ref, wo_ref, gm_ref, wu_ref, wd_ref, gf_ref, y_ref):
    o = jnp.concatenate([oa_ref[...], ob_ref[...]], axis=1)
    h1 = x_ref[...] + _dot(o, wo_ref[...])
    hn = _rms(h1, gm_ref[...]).astype(BF16)
    acc = h1
    for c in range(D_FF // D_MODEL):
        u = jnp.maximum(_dot(hn, wu_ref[:, c * D_MODEL:(c + 1) * D_MODEL]), 0.0)
        acc = acc + _dot((u * u).astype(BF16), wd_ref[c * D_MODEL:(c + 1) * D_MODEL, :])
    y_ref[...] = _rms(acc, gf_ref[...])


def _finish(x, oa, ob, wo, gm, wu, wd, gf, tm):
    m = x.shape[0]
    assert m % tm == 0
    row = lambda i: (i, 0)
    const = lambda i: (0, 0)
    resident = dict(pipeline_mode=pl.Buffered(1))
    return pl.pallas_call(
        _finish_body,
        out_shape=jax.ShapeDtypeStruct((m, D_MODEL), F32),
        grid=(m // tm,),
        in_specs=[
            pl.BlockSpec((tm, D_MODEL), row),
            pl.BlockSpec((tm, 512), row),
            pl.BlockSpec((tm, 512), row),
            pl.BlockSpec((D_MODEL, D_MODEL), const, **resident),
            pl.BlockSpec((1, D_MODEL), const),
            pl.BlockSpec((D_MODEL, D_FF), const, **resident),
            pl.BlockSpec((D_FF, D_MODEL), const, **resident),
            pl.BlockSpec((1, D_MODEL), const),
        ],
        out_specs=pl.BlockSpec((tm, D_MODEL), row),
        compiler_params=pltpu.CompilerParams(
            dimension_semantics=("arbitrary",), vmem_limit_bytes=VMEM_LIMIT),
        name="finish",
    )(x, oa, ob, wo, gm, wu, wd, gf)


def kernel(x_prompt, x_sample, cache_a_k, cache_a_v, cache_a_idx_k, cache_b_k, cache_b_v,
           meta_tokens, attn_norm_g, w_in, idx_k_norm_g, idx_k_norm_b,
           lambda_q1, lambda_k1, lambda_q2, lambda_k2, subln_g, w_o,
           mlp_norm_g, w_up, w_down, final_norm_g):
    batch, seq, _ = x_prompt.shape
    dec_batch, ts, _ = x_sample.shape
    past = cache_a_k.shape[2]
    assert attn_norm_g.shape[0] == 1, "single-layer step"
    assert seq % TQ == 0 and ts == 16 and past % LANES == 0 and past // CHUNK == (past + ts - 1) // CHUNK
    n = N_META + seq
    k_top_p = min(TOPK_MAX, seq // 4)
    k_top_s = min(TOPK_MAX, (past + ts) // 4)

    w = w_in[0]
    w = jnp.concatenate(
        [w[:, 0:512],
         w[:, 964:1476],
         w[:, 1476:1988],
         w[:, 1988:2500],
         w[:, 640:896],
         w[:, 512:640],
         w[:, 896:960],
         w[:, 960:964],
         jnp.zeros((D_MODEL, W_COLS - 2500), w.dtype)], axis=1).astype(BF16)
    wvt = w_in[0][:, 1476:2500].T.astype(BF16)
    wo = w_o[0].astype(BF16)
    wu = w_up[0].astype(BF16)
    wd = w_down[0].astype(BF16)
    g_attn = attn_norm_g[0][None]
    g_mlp = mlp_norm_g[0][None]
    g_fin = final_norm_g[None]
    kng = idx_k_norm_g[0][None]
    knb = idx_k_norm_b[0][None]
    lq1, lk1, lq2, lk2 = lambda_q1[0][None], lambda_k1[0][None], lambda_q2[0][None], lambda_k2[0][None]
    sg = subln_g[0][None]

    xp = x_prompt.reshape(batch * seq, D_MODEL)
    xs = x_sample.reshape(dec_batch * ts, D_MODEL)

    (qa_p, qi_p, qb_p, kbh_p, _, vb_p, _, _, kab_p, _, kib_p, vt_p, wt_p, vbt_p, smt_p, kbt_p) = _proj(
        xp, g_attn, w, wvt, kng, knb, 512, N_META, seq, True)
    (_, _, _, kbh_m, kb_m, vb_m, vbb_m, sm_m, kab_m, vaa_m, kib_m) = _proj(
        meta_tokens, g_attn, w, wvt, kng, knb, N_META, 0, N_META, False)
    (qa_s, qi_s, qb_s, kbh_s, kb_s, vb_s, vbb_s, sm_s, kab_s, vaa_s, kib_s) = _proj(
        xs, g_attn, w, wvt, kng, knb, dec_batch * ts, past, dec_batch * ts, False)

    oa_p = _dsa(qa_p, qi_p, wt_p, kab_p, kib_p, vt_p, kab_m, kib_m, vaa_m, batch, seq, k_top_p)
    ob_p = _diff(qb_p, kbh_p, vbt_p, kbh_m, vbb_m, lq1, lk1, lq2, lk2, sg, batch, seq)
    y_prompt = _finish(xp, oa_p, ob_p, wo, g_mlp, wu, wd, g_fin, 512).reshape(batch, seq, D_MODEL)

    oa_s, ob_s = _sample(
        qa_s, qi_s, sm_s, kab_s, vaa_s, kib_s, qb_s, kbh_s, vbb_s,
        jnp.swapaxes(cache_a_k[0], 1, 2), jnp.swapaxes(cache_a_v[0], 1, 2),
        jnp.swapaxes(cache_a_idx_k[0], 1, 2),
        jnp.transpose(cache_b_k[0], (0, 2, 3, 4, 1)).reshape(dec_batch, 2 * H_B * DH_B, past),
        cache_b_v[0].reshape(dec_batch, past, H_B * DV_B),
        lq1, lk1, lq2, lk2, sg, dec_batch, ts, past, k_top_s)
    y_sample = _finish(xs, oa_s, ob_s, wo, g_mlp, wu, wd, g_fin, dec_batch * ts).reshape(
        dec_batch, ts, D_MODEL)

    def with_meta(meta_rows, main_rows, width):
        meta_b = jnp.broadcast_to(meta_rows[None], (batch, N_META, width))
        return jnp.concatenate([meta_b, main_rows.reshape(batch, seq, width)], axis=1)[None]

    def with_meta_t(meta_rows, main_t):
        meta_b = jnp.broadcast_to(meta_rows.T[None], (batch, 64, N_META))
        main = jnp.swapaxes(main_t.reshape(64, batch, seq), 0, 1)
        return jnp.swapaxes(jnp.concatenate([meta_b, main], axis=2), 1, 2)[None]

    new_a_k_p = with_meta_t(sm_m[:, SM_KA:SM_KA + 64], smt_p[SM_KA:SM_KA + 64])
    new_a_v_p = with_meta_t(sm_m[:, SM_VA:SM_VA + 64], smt_p[SM_VA:SM_VA + 64])
    new_a_i_p = with_meta_t(sm_m[:, SM_KI:SM_KI + 64], smt_p[SM_KI:SM_KI + 64])
    kb_meta_t = jnp.broadcast_to(kb_m.T.reshape(1, H_B, 2, DH_B, N_META), (batch, H_B, 2, DH_B, N_META))
    kb_main_t = jnp.transpose(kbt_p.reshape(H_B, 2, DH_B, batch, seq), (3, 0, 1, 2, 4))
    new_b_k_p = jnp.transpose(
        jnp.concatenate([kb_meta_t, kb_main_t], axis=4), (0, 4, 1, 2, 3))[None]
    new_b_v_p = with_meta(vb_m, vb_p, 512).reshape(1, batch, n, H_B, DV_B)
    new_a_k_s = sm_s[:, SM_KA:SM_KA + 64].reshape(1, dec_batch, ts, 64)
    new_a_v_s = sm_s[:, SM_VA:SM_VA + 64].reshape(1, dec_batch, ts, 64)
    new_a_i_s = sm_s[:, SM_KI:SM_KI + 64].reshape(1, dec_batch, ts, 64)
    new_b_k_s = kb_s.reshape(1, dec_batch, ts, H_B, 2, DH_B)
    new_b_v_s = vb_s.reshape(1, dec_batch, ts, H_B, DV_B)
    return (y_prompt, y_sample, new_a_k_p, new_a_v_p, new_a_i_p, new_b_k_p, new_b_v_p,
            new_a_k_s, new_a_v_s, new_a_i_s, new_b_k_s, new_b_v_s)
```

```python
import functools

import jax
import jax.numpy as jnp
from jax import lax
from jax.experimental import pallas as pl
from jax.experimental.pallas import tpu as pltpu

F32 = jnp.float32
BF16 = jnp.bfloat16
I32 = jnp.int32
I16 = jnp.int16
HALF16 = 32768

D_MODEL = 1024
CHUNK = 64
N_META = 16
H_A = 8
DH_A = 64
H_I = 4
D_IDX = 64
TOPK_MAX = 256
H_B = 4
DH_B = 64
DV_B = 2 * DH_B
D_FF = 4 * D_MODEL
EPS = 1e-6
LAM_INIT = 0.2

LANES = 128
C_QA, C_QB, C_KB, C_VB, C_QI, C_SM = 0, 512, 1024, 1536, 2048, 2304
W_COLS = 2560
SM_KA, SM_VA, SM_KI, SM_WI = 0, 64, 128, 192

Q_SCALE = DH_A ** -0.5
W_SCALE = (H_I ** -0.5) * (D_IDX ** -0.5)
SLOPES_A = tuple(2.0 ** (-8.0 * (i + 1) / H_A) for i in range(H_A))
SLOPES_B = tuple(2.0 ** (-8.0 * (i + 1) / H_B) for i in range(H_B))

INT_MIN = -(2 ** 31)
NEG_INIT = -1e30
TQ = 256
VT_ROWS = DV_B + 16
CNT_ROWS = 32
VMEM_LIMIT = 56 * 1024 * 1024

_NT = (((1,), (1,)), ((), ()))


def _dot(a, b):
    return jnp.dot(a, b, preferred_element_type=F32)


def _dot_nt(a, b):
    return lax.dot_general(a, b, _NT, preferred_element_type=F32)


def _rms(x, g):
    return (x * lax.rsqrt(jnp.mean(x * x, axis=-1, keepdims=True) + EPS)) * g


def _to_key(score):
    b = lax.bitcast_convert_type(score, I32)
    return b ^ ((b >> 31) & jnp.int32(0x7FFFFFFF))


def _f32_from_key(key):
    return lax.bitcast_convert_type(key ^ ((key >> 31) & jnp.int32(0x7FFFFFFF)), F32)


def _bf16_from_key(key):
    bits = key ^ ((key >> 15) & jnp.int32(0x7FFF))
    return lax.bitcast_convert_type(jnp.left_shift(bits, 16), F32).astype(BF16)


KEY16_LOWEST = -32640
KEY32_LOWEST = -2139095040
KEY32_HIGHEST = 2139095039


def _wide(x, width):
    reps = width // LANES
    return x if reps == 1 else jnp.concatenate([x] * reps, axis=1)


def _proj_body(x_ref, g_ref, w_ref, wvt_ref, kng_ref, knb_ref,
               qa_ref, qi_ref, qb_ref, kbh_ref, kb_ref, vb_ref, vbb_ref,
               sm_ref, kab_ref, vaa_ref, kib_ref, *maybe_t_refs, pos0, period):
    x = x_ref[...]
    hn = _rms(x, g_ref[...]).astype(BF16)
    rows = x.shape[0]

    def mm(lo, width):
        return _dot(hn, w_ref[:, lo:lo + width])

    lane = lax.broadcasted_iota(I32, (rows, 64), 1)
    r = pl.program_id(0) * rows + lax.broadcasted_iota(I32, (rows, 64), 0)
    pos = pos0 + r % period
    pos_cols = jnp.where(lane == 0, (pos // 256) * 256, jnp.where(lane == 1, pos % 256, 0))
    pos_cols = pos_cols.astype(F32).astype(BF16)

    def slope_cols(slope):
        return jnp.where(lane < 2, slope, 0.0).astype(BF16)

    z = mm(C_QA, 512) * Q_SCALE
    for h in range(H_A):
        qa_ref[h] = jnp.concatenate(
            [z[:, h * DH_A:(h + 1) * DH_A].astype(BF16), slope_cols(SLOPES_A[h])], axis=1)
    z = mm(C_QB, 512) * Q_SCALE
    for hc in range(2 * H_B):
        qb_ref[hc] = jnp.concatenate(
            [z[:, hc * DH_B:(hc + 1) * DH_B].astype(BF16), slope_cols(SLOPES_B[hc // 2])], axis=1)
    z = mm(C_KB, 512)
    kb_ref[...] = z
    for hc in range(2 * H_B):
        kbh_ref[hc] = jnp.concatenate([z[:, hc * DH_B:(hc + 1) * DH_B].astype(BF16), pos_cols], axis=1)
    z = mm(C_VB, 512)
    for h in range(H_B):
        vb_ref[pl.ds(h, rows, stride=H_B), :] = z[:, h * DV_B:(h + 1) * DV_B]
    vbb_ref[...] = z.astype(BF16)
    z = mm(C_QI, 256)
    for h in range(H_I):
        qi_ref[h] = z[:, h * D_IDX:(h + 1) * D_IDX].astype(BF16)
    z = mm(C_SM, 256)
    ka = z[:, SM_KA:SM_KA + 64]
    va = z[:, SM_VA:SM_VA + 64]
    ki = z[:, SM_KI:SM_KI + 64]
    xc = ki - jnp.mean(ki, axis=-1, keepdims=True)
    ki = xc * lax.rsqrt(jnp.mean(xc * xc, axis=-1, keepdims=True) + EPS) * kng_ref[...] + knb_ref[...]
    sm_ref[:, 0:128] = z[:, 0:128]
    sm_ref[:, SM_KI:SM_KI + 64] = ki
    sm_ref[:, SM_WI:SM_WI + 64] = z[:, SM_WI:SM_WI + 64] * W_SCALE
    kab_ref[...] = jnp.concatenate([ka.astype(BF16), pos_cols], axis=1)
    kib_ref[...] = ki.astype(BF16)
    vaa_ref[...] = jnp.concatenate([va.astype(BF16), jnp.ones((rows, 64), BF16)], axis=1)
    if maybe_t_refs:
        vt_ref, wt_ref, vbt_ref, smt_ref, kbt_ref = maybe_t_refs
        kv_t = z[:, 0:128].T
        for c in range(rows // TQ):
            vt_ref[c] = jnp.concatenate(
                [kv_t[64:128, c * TQ:(c + 1) * TQ].astype(BF16), jnp.ones((64, TQ), BF16)], axis=0)
        kw_t = jnp.concatenate([ki, z[:, SM_WI:SM_WI + 64] * W_SCALE], axis=1).T
        wt_ref[...] = kw_t[64:72, :]
        smt_ref[0:128, :] = kv_t
        smt_ref[128:192, :] = kw_t[0:64, :]
        kbt_ref[...] = _dot_nt(wvt_ref[0:512, :], hn)
        vb_t = _dot_nt(wvt_ref[512:1024, :], hn)
        for c in range(rows // TQ):
            for h in range(H_B):
                vbt_ref[c, h] = jnp.concatenate(
                    [vb_t[h * DV_B:(h + 1) * DV_B, c * TQ:(c + 1) * TQ].astype(BF16),
                     jnp.ones((VT_ROWS - DV_B, TQ), BF16)], axis=0)


def _proj(x, g, w, wvt, kng, knb, tm, pos0, period, key_major):
    m = x.shape[0]
    assert m % tm == 0
    row = lambda i: (i, 0)
    head = lambda i: (0, i, 0)
    const = lambda i: (0, 0)
    out_shape = (
        jax.ShapeDtypeStruct((H_A, m, 128), BF16),
        jax.ShapeDtypeStruct((H_I, m, D_IDX), BF16),
        jax.ShapeDtypeStruct((2 * H_B, m, 128), BF16),
        jax.ShapeDtypeStruct((2 * H_B, m, 128), BF16),
        jax.ShapeDtypeStruct((m, 512), F32),
        jax.ShapeDtypeStruct((m * H_B, DV_B), F32),
        jax.ShapeDtypeStruct((m, 512), BF16),
        jax.ShapeDtypeStruct((m, 256), F32),
        jax.ShapeDtypeStruct((m, 128), BF16),
        jax.ShapeDtypeStruct((m, 128), BF16),
        jax.ShapeDtypeStruct((m, 64), BF16),
    )
    out_specs = (
        pl.BlockSpec((H_A, tm, 128), head),
        pl.BlockSpec((H_I, tm, D_IDX), head),
        pl.BlockSpec((2 * H_B, tm, 128), head),
        pl.BlockSpec((2 * H_B, tm, 128), head),
        pl.BlockSpec((tm, 512), row),
        pl.BlockSpec((tm * H_B, DV_B), row),
        pl.BlockSpec((tm, 512), row),
        pl.BlockSpec((tm, 256), row),
        pl.BlockSpec((tm, 128), row),
        pl.BlockSpec((tm, 128), row),
        pl.BlockSpec((tm, 64), row),
    )
    if key_major:
        assert tm % TQ == 0
        out_shape += (
            jax.ShapeDtypeStruct((m // TQ, 128, TQ), BF16),
            jax.ShapeDtypeStruct((8, m), F32),
            jax.ShapeDtypeStruct((m // TQ, H_B, VT_ROWS, TQ), BF16),
            jax.ShapeDtypeStruct((192, m), F32),
            jax.ShapeDtypeStruct((512, m), F32),
        )
        out_specs += (
            pl.BlockSpec((tm // TQ, 128, TQ), lambda i: (i, 0, 0)),
            pl.BlockSpec((8, tm), lambda i: (0, i)),
            pl.BlockSpec((tm // TQ, H_B, VT_ROWS, TQ), lambda i: (i, 0, 0, 0)),
            pl.BlockSpec((192, tm), lambda i: (0, i)),
            pl.BlockSpec((512, tm), lambda i: (0, i)),
        )
    return pl.pallas_call(
        functools.partial(_proj_body, pos0=pos0, period=period),
        out_shape=out_shape,
        grid=(m // tm,),
        in_specs=[
            pl.BlockSpec((tm, D_MODEL), row),
            pl.BlockSpec((1, D_MODEL), const),
            pl.BlockSpec((D_MODEL, W_COLS), const),
            pl.BlockSpec((2 * H_B * DV_B, D_MODEL), const),
            pl.BlockSpec((1, D_IDX), const),
            pl.BlockSpec((1, D_IDX), const),
        ],
        out_specs=out_specs,
        compiler_params=pltpu.CompilerParams(
            dimension_semantics=("arbitrary",), vmem_limit_bytes=VMEM_LIMIT),
        name="proj",
    )(x, g, w, wvt, kng, knb)


def _dsa_body(qa_ref, qi_ref, wt_ref, ka_ref, ki_ref, vt_ref, mka_ref, mki_ref, mva_ref,
              o_ref, smeta_ref, smain_ref, bmeta_ref, bmain_ref,
              tsel_ref, m_ref, al_ref, acc_ref, p_ref, s_ref, seen_ref, *, k_top):
    i = pl.program_id(1)
    krow = lax.broadcasted_iota(I32, (TQ, TQ), 0)
    qcol = lax.broadcasted_iota(I32, (TQ, TQ), 1)

    def main_rows(j):
        return pl.ds(pl.multiple_of(j * TQ, TQ), TQ)

    qi_all = qi_ref[...].reshape(H_I * TQ, D_IDX)

    def scores(ki_blk):
        s4 = _dot_nt(ki_blk, qi_all)
        sc = None
        for h in range(H_I):
            t = jnp.maximum(s4[:, h * TQ:(h + 1) * TQ], 0.0) * wt_ref[h:h + 1, :]
            sc = t if sc is None else sc + t
        return sc

    sc = scores(mki_ref[...])
    smeta_ref[...] = sc
    bmeta_ref[...] = sc.astype(BF16)

    def score_block(j, sc):
        smain_ref[j] = sc
        bmain_ref[j] = sc.astype(BF16)

    def score_full(j, carry):
        score_block(j, scores(ki_ref[main_rows(j), :]))
        return carry

    lax.fori_loop(0, i, score_full, 0)
    score_block(i, jnp.where((krow // CHUNK) <= (qcol // CHUNK),
                             scores(ki_ref[main_rows(i), :]), -jnp.inf))

    def count(meta_ref, main_ref, part, pred):
        def body(j, c):
            return c + part(pred(main_ref[j]))

        c = lax.fori_loop(0, i + 1, body, jnp.zeros((CNT_ROWS, TQ), F32))
        cm = part(pred(meta_ref[...]), rows=N_META)
        return jnp.sum(c, axis=0, keepdims=True) + jnp.sum(cm, axis=0, keepdims=True)

    def part32(hit, rows=CNT_ROWS):
        ones = jnp.where(hit, 1.0, 0.0)
        return jnp.sum(ones.reshape(ones.shape[0] // rows, rows, TQ), axis=0)

    def part16(hit, rows=CNT_ROWS):
        ones = jnp.where(hit, jnp.bfloat16(1), jnp.bfloat16(0))
        acc = ones[0:rows]
        for t in range(1, ones.shape[0] // rows):
            acc = acc + ones[t * rows:(t + 1) * rows]
        return acc.astype(F32)

    kf = float(k_top)

    def step16(s, u):
        cand_u = u | jnp.left_shift(jnp.int32(1), 15 - s)
        cand = _bf16_from_key(jnp.maximum(cand_u, KEY16_LOWEST + HALF16) - HALF16)
        cnt = count(bmeta_ref, bmain_ref, part16, lambda x: x >= cand)
        return jnp.where(cnt >= kf, cand_u, u)

    u_hi = lax.fori_loop(0, 16, step16, jnp.zeros((1, TQ), I32))
    t_hi = _bf16_from_key(jnp.maximum(u_hi, KEY16_LOWEST + HALF16) - HALF16).astype(F32)
    base = jnp.maximum(_to_key(t_hi) - 65536, KEY32_LOWEST)

    def step32(s, d):
        cand_d = d | jnp.left_shift(jnp.int32(1), 16 - s)
        cand = _f32_from_key(jnp.minimum(base + cand_d, KEY32_HIGHEST))
        cnt = count(smeta_ref, smain_ref, part32, lambda x: x >= cand)
        return jnp.where(cnt >= kf, cand_d, d)

    d_lo = lax.fori_loop(0, 17, step32, jnp.zeros((1, TQ), I32))
    t_sel = _f32_from_key(jnp.minimum(base + d_lo, KEY32_HIGHEST))
    need = kf - count(smeta_ref, smain_ref, part32, lambda x: x > t_sel)
    tsel_ref[...] = jnp.broadcast_to(t_sel, tsel_ref.shape)

    q_all = qa_ref[...].reshape(H_A * TQ, 128)
    m_ref[...] = jnp.full(m_ref.shape, NEG_INIT, F32)
    acc_ref[...] = jnp.zeros(acc_ref.shape, F32)
    thr = tsel_ref[0:1, :]

    def logits(k_blk):
        return _dot_nt(k_blk, q_all)

    def softmax(sc, s_all, future, seen, slot):
        nk = sc.shape[0]
        tie = sc == thr
        tri = (lax.broadcasted_iota(I32, (nk, nk), 1)
               <= lax.broadcasted_iota(I32, (nk, nk), 0)).astype(BF16)
        rank = _dot(tri, jnp.where(tie, 1.0, 0.0).astype(BF16)) + seen
        negm = jnp.where(sc > thr, 0.0, jnp.where(tie, jnp.where(rank <= need, 0.0, -jnp.inf), -jnp.inf))
        for h in range(H_A):
            cols = slice(h * TQ, (h + 1) * TQ)
            s = s_all[:, cols] + negm
            if future is not None:
                s = s - (2.0 * SLOPES_A[h]) * future
            m_old = m_ref[:, cols]
            m_new = jnp.maximum(m_old, jnp.max(s, axis=0, keepdims=True))
            p_ref[slot, 0:nk, cols] = jnp.exp(s - m_new).astype(BF16)
            al_ref[slot, :, cols] = jnp.exp(m_old - m_new)
            m_ref[:, cols] = m_new
        return rank[nk - 1:nk, :]

    def accumulate(vt_blk, slot, nk):
        acc_ref[...] = acc_ref[...] * al_ref[slot] + _dot(vt_blk, p_ref[slot, 0:nk, :])

    eye = (lax.broadcasted_iota(I32, (128, 128), 0)
           == lax.broadcasted_iota(I32, (128, 128), 1)).astype(BF16)
    seen_ref[...] = softmax(smeta_ref[...], logits(mka_ref[...]), None, jnp.zeros((1, TQ), F32), 0)
    accumulate(_dot_nt(eye, mva_ref[...]).astype(BF16), 0, N_META)

    p_ref[1] = jnp.zeros(p_ref.shape[1:], BF16)
    al_ref[1] = jnp.ones(al_ref.shape[1:], F32)
    s_ref[0] = logits(ka_ref[main_rows(0), :])

    def full_step(t, slot):
        s_ref[1 - slot] = logits(ka_ref[main_rows(t + 1), :])
        accumulate(vt_ref[jnp.maximum(t - 1, 0)], 1 - slot, TQ)
        seen_ref[...] = softmax(smain_ref[t], s_ref[slot], None, seen_ref[...], slot)

    def two_steps(u, carry):
        full_step(2 * u, 0)
        full_step(2 * u + 1, 1)
        return carry

    lax.fori_loop(0, i // 2, two_steps, 0)

    def last_step(slot):
        accumulate(vt_ref[jnp.maximum(i - 1, 0)], 1 - slot, TQ)
        softmax(smain_ref[i], s_ref[slot], jnp.maximum(krow - qcol, 0).astype(F32),
                seen_ref[...], slot)
        accumulate(vt_ref[i], slot, TQ)

    @pl.when(i % 2 == 1)
    def _odd():
        full_step(i - 1, 0)
        last_step(1)

    @pl.when(i % 2 == 0)
    def _even():
        last_step(0)

    outs = []
    for h in range(H_A):
        a = acc_ref[:, h * TQ:(h + 1) * TQ].T
        o = a / pltpu.roll(a, DH_A, axis=1)
        outs.append(o[:, :DH_A])
    o_ref[...] = jnp.concatenate(outs, axis=1).astype(BF16)


def _dsa(qa, qi, wt, kab, kib, vt, mka, mki, mva, batch, seq, k_top):
    nq = seq // TQ
    qrow = lambda b, i: (b * nq + i, 0)
    qhead = lambda b, i: (0, b * nq + i, 0)
    kv = lambda b, i: (b, 0)
    const = lambda b, i: (0, 0)
    return pl.pallas_call(
        functools.partial(_dsa_body, k_top=k_top),
        out_shape=jax.ShapeDtypeStruct((batch * seq, H_A * DH_A), BF16),
        grid=(batch, nq),
        in_specs=[
            pl.BlockSpec((H_A, TQ, 128), qhead),
            pl.BlockSpec((H_I, TQ, D_IDX), qhead),
            pl.BlockSpec((8, TQ), lambda b, i: (0, b * nq + i)),
            pl.BlockSpec((seq, 128), kv),
            pl.BlockSpec((seq, D_IDX), kv),
            pl.BlockSpec((nq, 128, TQ), lambda b, i: (b, 0, 0)),
            pl.BlockSpec((N_META, 128), const),
            pl.BlockSpec((N_META, D_IDX), const),
            pl.BlockSpec((N_META, 128), const),
        ],
        out_specs=pl.BlockSpec((TQ, H_A * DH_A), qrow),
        scratch_shapes=[
            pltpu.VMEM((N_META, TQ), F32),
            pltpu.VMEM((nq, TQ, TQ), F32),
            pltpu.VMEM((N_META, TQ), BF16),
            pltpu.VMEM((nq, TQ, TQ), BF16),
            pltpu.VMEM((8, TQ), F32),
            pltpu.VMEM((1, H_A * TQ), F32),
            pltpu.VMEM((2, 1, H_A * TQ), F32),
            pltpu.VMEM((128, H_A * TQ), F32),
            pltpu.VMEM((2, TQ, H_A * TQ), BF16),
            pltpu.VMEM((2, TQ, H_A * TQ), F32),
            pltpu.VMEM((1, TQ), F32),
        ],
        compiler_params=pltpu.CompilerParams(
            dimension_semantics=("arbitrary", "arbitrary"), vmem_limit_bytes=VMEM_LIMIT),
        name="dsa",
    )(qa, qi, wt, kab, kib, vt, mka, mki, mva)


def _lambda(lq1_ref, lk1_ref, lq2_ref, lk2_ref):
    s1 = jnp.sum(lq1_ref[...] * lk1_ref[...], axis=-1, keepdims=True)
    s2 = jnp.sum(lq2_ref[...] * lk2_ref[...], axis=-1, keepdims=True)
    return jnp.exp(s1) - jnp.exp(s2) + LAM_INIT


def _diff_body(qb_ref, kb_ref, vt_ref, mkb_ref, mvb_ref, lq1_ref, lk1_ref, lq2_ref, lk2_ref,
               sg_ref, o_ref, m_ref, al_ref, acc_ref, p_ref, s_ref):
    i = pl.program_id(1)
    n_hc = 2 * H_B
    krow = lax.broadcasted_iota(I32, (TQ, TQ), 0)
    qcol = lax.broadcasted_iota(I32, (TQ, TQ), 1)
    m_ref[...] = jnp.full(m_ref.shape, NEG_INIT, F32)
    acc_ref[...] = jnp.zeros(acc_ref.shape, F32)

    def main_rows(j):
        return pl.ds(pl.multiple_of(j * TQ, TQ), TQ)

    def logits(k_of, slot, nk):
        for hc in range(n_hc):
            s_ref[slot, hc, 0:nk, :] = _dot_nt(k_of(hc), qb_ref[hc])

    def softmax(slot, nk, negm, future):
        for hc in range(n_hc):
            cols = slice(hc * TQ, (hc + 1) * TQ)
            s = s_ref[slot, hc, 0:nk, :]
            if negm is not None:
                s = s + negm
            if future is not None:
                s = s - (2.0 * SLOPES_B[hc // 2]) * future
            m_old = m_ref[:, cols]
            m_new = jnp.maximum(m_old, jnp.max(s, axis=0, keepdims=True))
            p_ref[slot, 0:nk, cols] = jnp.exp(s - m_new).astype(BF16)
            al_ref[slot, :, cols] = jnp.exp(m_old - m_new)
            m_ref[:, cols] = m_new

    def accumulate(vt_of, slot, nk):
        for h in range(H_B):
            cols = slice(2 * h * TQ, (2 * h + 2) * TQ)
            acc_ref[h] = acc_ref[h] * al_ref[slot, :, cols] + _dot(vt_of(h), p_ref[slot, 0:nk, cols])

    eye = (lax.broadcasted_iota(I32, (DV_B, DV_B), 0)
           == lax.broadcasted_iota(I32, (DV_B, DV_B), 1)).astype(BF16)

    def meta_vt(h):
        v_t = _dot_nt(eye, mvb_ref[:, h * DV_B:(h + 1) * DV_B]).astype(BF16)
        return jnp.concatenate([v_t, jnp.ones((VT_ROWS - DV_B, N_META), BF16)], axis=0)

    logits(lambda hc: mkb_ref[hc], 0, N_META)
    softmax(0, N_META, None, None)
    accumulate(meta_vt, 0, N_META)

    p_ref[1] = jnp.zeros(p_ref.shape[1:], BF16)
    al_ref[1] = jnp.ones(al_ref.shape[1:], F32)
    logits(lambda hc: kb_ref[hc, main_rows(0), :], 0, TQ)

    def main_vt(t):
        return lambda h: vt_ref[t, h]

    def full_step(t, slot):
        logits(lambda hc: kb_ref[hc, main_rows(t + 1), :], 1 - slot, TQ)
        accumulate(main_vt(jnp.maximum(t - 1, 0)), 1 - slot, TQ)
        softmax(slot, TQ, None, None)

    def two_steps(u, carry):
        full_step(2 * u, 0)
        full_step(2 * u + 1, 1)
        return carry

    lax.fori_loop(0, i // 2, two_steps, 0)

    def last_step(slot):
        accumulate(main_vt(jnp.maximum(i - 1, 0)), 1 - slot, TQ)
        softmax(slot, TQ, jnp.where((krow // CHUNK) <= (qcol // CHUNK), 0.0, -jnp.inf),
                jnp.maximum(krow - qcol, 0).astype(F32))
        accumulate(main_vt(i), slot, TQ)

    @pl.when(i % 2 == 1)
    def _odd():
        full_step(i - 1, 0)
        last_step(1)

    @pl.when(i % 2 == 0)
    def _even():
        last_step(0)

    lam = _lambda(lq1_ref, lk1_ref, lq2_ref, lk2_ref)
    for h in range(H_B):
        a = acc_ref[h]
        o0 = a[0:DV_B, 0:TQ] / a[DV_B:DV_B + 1, 0:TQ]
        o1 = a[0:DV_B, TQ:2 * TQ] / a[DV_B:DV_B + 1, TQ:2 * TQ]
        o = _rms((o0 - lam * o1).T, sg_ref[...]) * (1.0 - LAM_INIT)
        o_ref[:, h * DV_B:(h + 1) * DV_B] = o.astype(BF16)


def _diff(qb, kbh, vbt, mkb, mvb, lq1, lk1, lq2, lk2, sg, batch, seq):
    nq = seq // TQ
    qrow = lambda b, i: (b * nq + i, 0)
    qhead = lambda b, i: (0, b * nq + i, 0)
    const2 = lambda b, i: (0, 0)
    const3 = lambda b, i: (0, 0, 0)
    return pl.pallas_call(
        _diff_body,
        out_shape=jax.ShapeDtypeStruct((batch * seq, H_B * DV_B), BF16),
        grid=(batch, nq),
        in_specs=[
            pl.BlockSpec((2 * H_B, TQ, 128), qhead),
            pl.BlockSpec((2 * H_B, seq, 128), lambda b, i: (0, b, 0)),
            pl.BlockSpec((nq, H_B, VT_ROWS, TQ), lambda b, i: (b, 0, 0, 0)),
            pl.BlockSpec((2 * H_B, N_META, 128), const3),
            pl.BlockSpec((N_META, H_B * DV_B), const2),
            pl.BlockSpec((1, DH_B), const2),
            pl.BlockSpec((1, DH_B), const2),
            pl.BlockSpec((1, DH_B), const2),
            pl.BlockSpec((1, DH_B), const2),
            pl.BlockSpec((1, DV_B), const2),
        ],
        out_specs=pl.BlockSpec((TQ, H_B * DV_B), qrow),
        scratch_shapes=[
            pltpu.VMEM((1, 2 * H_B * TQ), F32),
            pltpu.VMEM((2, 1, 2 * H_B * TQ), F32),
            pltpu.VMEM((H_B, VT_ROWS, 2 * TQ), F32),
            pltpu.VMEM((2, TQ, 2 * H_B * TQ), BF16),
            pltpu.VMEM((2, 2 * H_B, TQ, TQ), F32),
        ],
        compiler_params=pltpu.CompilerParams(
            dimension_semantics=("arbitrary", "arbitrary"), vmem_limit_bytes=VMEM_LIMIT),
        name="diff",
    )(qb, kbh, vbt, mkb, mvb, lq1, lk1, lq2, lk2, sg)


def _sample_body(qa_ref, qi_ref, sm_ref, nka_ref, nva_ref, nki_ref, qb_ref, nkb_ref, nvb_ref,
                 cak_ref, cav_ref, cai_ref, cbk_ref, cbv_ref,
                 lq1_ref, lk1_ref, lq2_ref, lk2_ref, sg_ref,
                 oa_ref, ob_ref, pc_ref, pn_ref, *, k_top, past, ts):
    pad = LANES - ts
    row_c = lax.broadcasted_iota(I32, (ts, past), 0)
    col_c = lax.broadcasted_iota(I32, (ts, past), 1)
    row_n = lax.broadcasted_iota(I32, (ts, LANES), 0)
    col_n = lax.broadcasted_iota(I32, (ts, LANES), 1)
    new_ok = col_n < ts

    def pad_rows(x):
        return jnp.concatenate([x, jnp.zeros((pad,) + x.shape[1:], x.dtype)], axis=0)

    qi_all = qi_ref[...].reshape(H_I * ts, D_IDX)
    w = sm_ref[:, SM_WI:SM_WI + H_I]
    s4c = _dot(qi_all, cai_ref[...].astype(BF16))
    s4n = _dot_nt(qi_all, pad_rows(nki_ref[...]))
    sc_c = None
    sc_n = None
    for h in range(H_I):
        wh = w[:, h:h + 1]
        tc = jnp.maximum(s4c[h * ts:(h + 1) * ts], 0.0) * wh
        tn = jnp.maximum(s4n[h * ts:(h + 1) * ts], 0.0) * wh
        sc_c = tc if sc_c is None else sc_c + tc
        sc_n = tn if sc_n is None else sc_n + tn
    sc_n = jnp.where(new_ok, sc_n, -jnp.inf)

    def count(pred_c, pred_n):
        c = jnp.sum(jnp.where(pred_c, 1.0, 0.0), axis=1, keepdims=True)
        return c + jnp.sum(jnp.where(pred_n, 1.0, 0.0), axis=1, keepdims=True)

    def threshold(key):
        return _f32_from_key(jnp.clip(key, KEY32_LOWEST, KEY32_HIGHEST))

    def value_step(s, key):
        cand_key = key ^ jnp.left_shift(jnp.int32(1), 31 - s)
        cand = threshold(cand_key)
        cnt = count(sc_c >= cand, sc_n >= cand)
        return jnp.where(cnt >= float(k_top), cand_key, key)

    t = threshold(lax.fori_loop(0, 32, value_step, jnp.full((ts, 1), INT_MIN, I32)))
    need = float(k_top) - count(sc_c > t, sc_n > t)
    idx_n = col_n + past

    def tie_step(s, x):
        cand = x | jnp.left_shift(jnp.int32(1), 11 - s)
        cnt = count((sc_c == t) & (col_c < cand), (sc_n == t) & (idx_n < cand))
        return jnp.where(cnt < need, cand, x)

    x = lax.fori_loop(0, 12, tie_step, jnp.zeros((ts, 1), I32))
    negm_c = jnp.where((sc_c > t) | ((sc_c == t) & (col_c <= x)), 0.0, -jnp.inf)
    negm_n = jnp.where((sc_n > t) | ((sc_n == t) & (idx_n <= x)), 0.0, -jnp.inf)

    dist_c = (row_c - col_c + past).astype(F32)
    dist_n = jnp.abs(row_n - col_n).astype(F32)
    q_all = qa_ref[...][:, :, 0:DH_A].reshape(H_A * ts, DH_A)
    s_c = _dot(q_all, cak_ref[...].astype(BF16))
    s_n = _dot_nt(q_all, pad_rows(nka_ref[...][:, 0:DH_A]))
    for h in range(H_A):
        rows = slice(h * ts, (h + 1) * ts)
        lc = s_c[rows] + (negm_c - SLOPES_A[h] * dist_c)
        ln = s_n[rows] + (negm_n - SLOPES_A[h] * dist_n)
        m = jnp.maximum(jnp.max(lc, axis=1, keepdims=True), jnp.max(ln, axis=1, keepdims=True))
        m = jnp.maximum(m, NEG_INIT)
        pc_ref[rows, :] = jnp.exp(lc - m).astype(BF16)
        pn_ref[rows, :] = jnp.exp(ln - m).astype(BF16)
    vc_aug_t = jnp.concatenate([cav_ref[...].astype(BF16), jnp.ones((64, past), BF16)], axis=0)
    pv = _dot_nt(pc_ref[...], vc_aug_t) + _dot(pn_ref[...], pad_rows(nva_ref[...]))
    pv = pv / pltpu.roll(pv, DH_A, axis=1)
    oa_ref[...] = jnp.concatenate(
        [pv[h * ts:(h + 1) * ts, :DH_A] for h in range(H_A)], axis=1).astype(BF16)

    lam = _lambda(lq1_ref, lk1_ref, lq2_ref, lk2_ref)
    neg_new = jnp.where(new_ok, 0.0, -jnp.inf)
    for h in range(H_B):
        a_c = None
        a_n = None
        for c in range(2):
            hc = 2 * h + c
            q = qb_ref[hc][:, 0:DH_B]
            kc_t = cbk_ref[hc * DH_B:(hc + 1) * DH_B, :].astype(BF16)
            lc = _dot(q, kc_t) - SLOPES_B[h] * dist_c
            ln = _dot_nt(q, pad_rows(nkb_ref[hc][:, 0:DH_B])) + (neg_new - SLOPES_B[h] * dist_n)
            m = jnp.maximum(jnp.max(lc, axis=1, keepdims=True), jnp.max(ln, axis=1, keepdims=True))
            pc = jnp.exp(lc - m)
            pn = jnp.exp(ln - m)
            l = jnp.sum(pc, axis=1, keepdims=True) + jnp.sum(pn, axis=1, keepdims=True)
            pc = pc / l
            pn = pn / l
            if c == 0:
                a_c, a_n = pc, pn
            else:
                a_c, a_n = a_c - lam * pc, a_n - lam * pn
        vc = cbv_ref[pl.ds(h, past, stride=H_B), :].astype(BF16)
        vn = pad_rows(nvb_ref[:, h * DV_B:(h + 1) * DV_B])
        o = _dot(a_c.astype(BF16), vc) + _dot(a_n.astype(BF16), vn)
        o = _rms(o, sg_ref[...]) * (1.0 - LAM_INIT)
        ob_ref[:, h * DV_B:(h + 1) * DV_B] = o.astype(BF16)


def _sample(qa, qi, sm, nka, nva, nki, qb, nkb, nvb, cak, cav, cai, cbk, cbv,
            lq1, lk1, lq2, lk2, sg, batch, ts, past, k_top):
    row = lambda b: (b, 0)
    head = lambda b: (0, b, 0)
    cache = lambda b: (b, 0, 0)
    const = lambda b: (0, 0)
    return pl.pallas_call(
        functools.partial(_sample_body, k_top=k_top, past=past, ts=ts),
        out_shape=(jax.ShapeDtypeStruct((batch * ts, H_A * DH_A), BF16),
                   jax.ShapeDtypeStruct((batch * ts, H_B * DV_B), BF16)),
        grid=(batch,),
        in_specs=[
            pl.BlockSpec((H_A, ts, 128), head),
            pl.BlockSpec((H_I, ts, D_IDX), head),
            pl.BlockSpec((ts, 256), row),
            pl.BlockSpec((ts, 128), row),
            pl.BlockSpec((ts, 128), row),
            pl.BlockSpec((ts, D_IDX), row),
            pl.BlockSpec((2 * H_B, ts, 128), head),
            pl.BlockSpec((2 * H_B, ts, 128), head),
            pl.BlockSpec((ts, H_B * DV_B), row),
            pl.BlockSpec((None, DH_A, past), cache),
            pl.BlockSpec((None, DH_A, past), cache),
            pl.BlockSpec((None, D_IDX, past), cache),
            pl.BlockSpec((None, 2 * H_B * DH_B, past), cache),
            pl.BlockSpec((None, past * H_B, DV_B), cache),
            pl.BlockSpec((1, DH_B), const),
            pl.BlockSpec((1, DH_B), const),
            pl.BlockSpec((1, DH_B), const),
            pl.BlockSpec((1, DH_B), const),
            pl.BlockSpec((1, DV_B), const),
        ],
        out_specs=(pl.BlockSpec((ts, H_A * DH_A), row), pl.BlockSpec((ts, H_B * DV_B), row)),
        scratch_shapes=[
            pltpu.VMEM((H_A * ts, past), BF16),
            pltpu.VMEM((H_A * ts, LANES), BF16),
        ],
        compiler_params=pltpu.CompilerParams(
            dimension_semantics=("arbitrary",), vmem_limit_bytes=VMEM_LIMIT),
        name="sample",
    )(qa, qi, sm, nka, nva, nki, qb, nkb, nvb, cak, cav, cai, cbk, cbv, lq1, lk1, lq2, lk2, sg)


def _finish_body(x_ref, oa_ref, ob_ref, wo_ref, gm_ref, wu_ref, wd_ref, gf_ref, y_ref):
    o = jnp.concatenate([oa_ref[...], ob_ref[...]], axis=1)
    h1 = x_ref[...] + _dot(o, wo_ref[...])
    hn = _rms(h1, gm_ref[...]).astype(BF16)
    acc = h1
    for c in range(D_FF // D_MODEL):
        u = jnp.maximum(_dot(hn, wu_ref[:, c * D_MODEL:(c + 1) * D_MODEL]), 0.0)
        acc = acc + _dot((u * u).astype(BF16), wd_ref[c * D_MODEL:(c + 1) * D_MODEL, :])
    y_ref[...] = _rms(acc, gf_ref[...])


def _finish(x, oa, ob, wo, gm, wu, wd, gf, tm):
    m = x.shape[0]
    assert m % tm == 0
    row = lambda i: (i, 0)
    const = lambda i: (0, 0)
    resident = dict(pipeline_mode=pl.Buffered(1))
    return pl.pallas_call(
        _finish_body,
        out_shape=jax.ShapeDtypeStruct((m, D_MODEL), F32),
        grid=(m // tm,),
        in_specs=[
            pl.BlockSpec((tm, D_MODEL), row),
            pl.BlockSpec((tm, 512), row),
            pl.BlockSpec((tm, 512), row),
            pl.BlockSpec((D_MODEL, D_MODEL), const, **resident),
            pl.BlockSpec((1, D_MODEL), const),
            pl.BlockSpec((D_MODEL, D_FF), const, **resident),
            pl.BlockSpec((D_FF, D_MODEL), const, **resident),
            pl.BlockSpec((1, D_MODEL), const),
        ],
        out_specs=pl.BlockSpec((tm, D_MODEL), row),
        compiler_params=pltpu.CompilerParams(
            dimension_semantics=("arbitrary",), vmem_limit_bytes=VMEM_LIMIT),
        name="finish",
    )(x, oa, ob, wo, gm, wu, wd, gf)


def kernel(x_prompt, x_sample, cache_a_k, cache_a_v, cache_a_idx_k, cache_b_k, cache_b_v,
           meta_tokens, attn_norm_g, w_in, idx_k_norm_g, idx_k_norm_b,
           lambda_q1, lambda_k1, lambda_q2, lambda_k2, subln_g, w_o,
           mlp_norm_g, w_up, w_down, final_norm_g):
    batch, seq, _ = x_prompt.shape
    dec_batch, ts, _ = x_sample.shape
    past = cache_a_k.shape[2]
    assert attn_norm_g.shape[0] == 1, "single-layer step"
    assert seq % TQ == 0 and ts == 16 and past % LANES == 0 and past // CHUNK == (past + ts - 1) // CHUNK
    n = N_META + seq
    k_top_p = min(TOPK_MAX, seq // 4)
    k_top_s = min(TOPK_MAX, (past + ts) // 4)

    w = w_in[0]
    w = jnp.concatenate(
        [w[:, 0:512],
         w[:, 964:1476],
         w[:, 1476:1988],
         w[:, 1988:2500],
         w[:, 640:896],
         w[:, 512:640],
         w[:, 896:960],
         w[:, 960:964],
         jnp.zeros((D_MODEL, W_COLS - 2500), w.dtype)], axis=1).astype(BF16)
    wvt = w_in[0][:, 1476:2500].T.astype(BF16)
    wo = w_o[0].astype(BF16)
    wu = w_up[0].astype(BF16)
    wd = w_down[0].astype(BF16)
    g_attn = attn_norm_g[0][None]
    g_mlp = mlp_norm_g[0][None]
    g_fin = final_norm_g[None]
    kng = idx_k_norm_g[0][None]
    knb = idx_k_norm_b[0][None]
    lq1, lk1, lq2, lk2 = lambda_q1[0][None], lambda_k1[0][None], lambda_q2[0][None], lambda_k2[0][None]
    sg = subln_g[0][None]

    xp = x_prompt.reshape(batch * seq, D_MODEL)
    xs = x_sample.reshape(dec_batch * ts, D_MODEL)

    (qa_p, qi_p, qb_p, kbh_p, _, vb_p, _, _, kab_p, _, kib_p, vt_p, wt_p, vbt_p, smt_p, kbt_p) = _proj(
        xp, g_attn, w, wvt, kng, knb, 512, N_META, seq, True)
    (_, _, _, kbh_m, kb_m, vb_m, vbb_m, sm_m, kab_m, vaa_m, kib_m) = _proj(
        meta_tokens, g_attn, w, wvt, kng, knb, N_META, 0, N_META, False)
    (qa_s, qi_s, qb_s, kbh_s, kb_s, vb_s, vbb_s, sm_s, kab_s, vaa_s, kib_s) = _proj(
        xs, g_attn, w, wvt, kng, knb, dec_batch * ts, past, dec_batch * ts, False)

    oa_p = _dsa(qa_p, qi_p, wt_p, kab_p, kib_p, vt_p, kab_m, kib_m, vaa_m, batch, seq, k_top_p)
    ob_p = _diff(qb_p, kbh_p, vbt_p, kbh_m, vbb_m, lq1, lk1, lq2, lk2, sg, batch, seq)
    y_prompt = _finish(xp, oa_p, ob_p, wo, g_mlp, wu, wd, g_fin, 512).reshape(batch, seq, D_MODEL)

    oa_s, ob_s = _sample(
        qa_s, qi_s, sm_s, kab_s, vaa_s, kib_s, qb_s, kbh_s, vbb_s,
        jnp.swapaxes(cache_a_k[0], 1, 2), jnp.swapaxes(cache_a_v[0], 1, 2),
        jnp.swapaxes(cache_a_idx_k[0], 1, 2),
        jnp.transpose(cache_b_k[0], (0, 2, 3, 4, 1)).reshape(dec_batch, 2 * H_B * DH_B, past),
        cache_b_v[0].reshape(dec_batch, past * H_B, DV_B),
        lq1, lk1, lq2, lk2, sg, dec_batch, ts, past, k_top_s)
    y_sample = _finish(xs, oa_s, ob_s, wo, g_mlp, wu, wd, g_fin, dec_batch * ts).reshape(
        dec_batch, ts, D_MODEL)

    def with_meta(meta_rows, main_rows, width):
        meta_b = jnp.broadcast_to(meta_rows[None], (batch, N_META, width))
        return jnp.concatenate([meta_b, main_rows.reshape(batch, seq, width)], axis=1)[None]

    def with_meta_t(meta_rows, main_t):
        meta_b = jnp.broadcast_to(meta_rows.T[None], (batch, 64, N_META))
        main = jnp.swapaxes(main_t.reshape(64, batch, seq), 0, 1)
        return jnp.swapaxes(jnp.concatenate([meta_b, main], axis=2), 1, 2)[None]

    new_a_k_p = with_meta_t(sm_m[:, SM_KA:SM_KA + 64], smt_p[SM_KA:SM_KA + 64])
    new_a_v_p = with_meta_t(sm_m[:, SM_VA:SM_VA + 64], smt_p[SM_VA:SM_VA + 64])
    new_a_i_p = with_meta_t(sm_m[:, SM_KI:SM_KI + 64], smt_p[SM_KI:SM_KI + 64])
    kb_meta_t = jnp.broadcast_to(kb_m.T.reshape(1, H_B, 2, DH_B, N_META), (batch, H_B, 2, DH_B, N_META))
    kb_main_t = jnp.transpose(kbt_p.reshape(H_B, 2, DH_B, batch, seq), (3, 0, 1, 2, 4))
    new_b_k_p = jnp.transpose(
        jnp.concatenate([kb_meta_t, kb_main_t], axis=4), (0, 4, 1, 2, 3))[None]
    new_b_v_p = jnp.concatenate(
        [jnp.broadcast_to(vb_m.reshape(1, N_META, H_B, DV_B), (batch, N_META, H_B, DV_B)),
         vb_p.reshape(batch, seq, H_B, DV_B)], axis=1)[None]
    new_a_k_s = sm_s[:, SM_KA:SM_KA + 64].reshape(1, dec_batch, ts, 64)
    new_a_v_s = sm_s[:, SM_VA:SM_VA + 64].reshape(1, dec_batch, ts, 64)
    new_a_i_s = sm_s[:, SM_KI:SM_KI + 64].reshape(1, dec_batch, ts, 64)
    new_b_k_s = kb_s.reshape(1, dec_batch, ts, H_B, 2, DH_B)
    new_b_v_s = vb_s.reshape(1, dec_batch, ts, H_B, DV_B)
    return (y_prompt, y_sample, new_a_k_p, new_a_v_p, new_a_i_p, new_b_k_p, new_b_v_p,
            new_a_k_s, new_a_v_s, new_a_i_s, new_b_k_s, new_b_v_s)
```

```python
import functools

import jax
import jax.numpy as jnp
from jax import lax
from jax.experimental import pallas as pl
from jax.experimental.pallas import tpu as pltpu

F32 = jnp.float32
BF16 = jnp.bfloat16
I32 = jnp.int32
I16 = jnp.int16
HALF16 = 32768

D_MODEL = 1024
CHUNK = 64
N_META = 16
H_A = 8
DH_A = 64
H_I = 4
D_IDX = 64
TOPK_MAX = 256
H_B = 4
DH_B = 64
DV_B = 2 * DH_B
D_FF = 4 * D_MODEL
EPS = 1e-6
LAM_INIT = 0.2

LANES = 128
C_QA, C_QB, C_KB, C_VB, C_QI, C_SM = 0, 512, 1024, 1536, 2048, 2304
W_COLS = 2560
SM_KA, SM_VA, SM_KI, SM_WI = 0, 64, 128, 192

Q_SCALE = DH_A ** -0.5
W_SCALE = (H_I ** -0.5) * (D_IDX ** -0.5)
SLOPES_A = tuple(2.0 ** (-8.0 * (i + 1) / H_A) for i in range(H_A))
SLOPES_B = tuple(2.0 ** (-8.0 * (i + 1) / H_B) for i in range(H_B))

INT_MIN = -(2 ** 31)
NEG_INIT = -1e30
TQ = 256
VT_ROWS = DV_B + 16
CNT_ROWS = 32
VMEM_LIMIT = 56 * 1024 * 1024

_NT = (((1,), (1,)), ((), ()))


def _dot(a, b):
    return jnp.dot(a, b, preferred_element_type=F32)


def _dot_nt(a, b):
    return lax.dot_general(a, b, _NT, preferred_element_type=F32)


def _rms(x, g):
    return (x * lax.rsqrt(jnp.mean(x * x, axis=-1, keepdims=True) + EPS)) * g


def _to_key(score):
    b = lax.bitcast_convert_type(score, I32)
    return b ^ ((b >> 31) & jnp.int32(0x7FFFFFFF))


def _f32_from_key(key):
    return lax.bitcast_convert_type(key ^ ((key >> 31) & jnp.int32(0x7FFFFFFF)), F32)


def _bf16_from_key(key):
    bits = key ^ ((key >> 15) & jnp.int32(0x7FFF))
    return lax.bitcast_convert_type(jnp.left_shift(bits, 16), F32).astype(BF16)


KEY16_LOWEST = -32640
KEY32_LOWEST = -2139095040
KEY32_HIGHEST = 2139095039


def _wide(x, width):
    reps = width // LANES
    return x if reps == 1 else jnp.concatenate([x] * reps, axis=1)


def _proj_body(x_ref, g_ref, w_ref, wvt_ref, kng_ref, knb_ref,
               qa_ref, qi_ref, qb_ref, kbh_ref, vb_ref, kab_ref, kib_ref, *mode_refs,
               pos0, period, key_major):
    x = x_ref[...]
    hn = _rms(x, g_ref[...]).astype(BF16)
    rows = x.shape[0]

    def mm(lo, width):
        return _dot(hn, w_ref[:, lo:lo + width])

    lane = lax.broadcasted_iota(I32, (rows, 64), 1)
    r = pl.program_id(0) * rows + lax.broadcasted_iota(I32, (rows, 64), 0)
    pos = pos0 + r % period
    pos_cols = jnp.where(lane == 0, (pos // 256) * 256, jnp.where(lane == 1, pos % 256, 0))
    pos_cols = pos_cols.astype(F32).astype(BF16)

    def slope_cols(slope):
        return jnp.where(lane < 2, slope, 0.0).astype(BF16)

    z = mm(C_QA, 512) * Q_SCALE
    for h in range(H_A):
        qa_ref[h] = jnp.concatenate(
            [z[:, h * DH_A:(h + 1) * DH_A].astype(BF16), slope_cols(SLOPES_A[h])], axis=1)
    z = mm(C_QB, 512) * Q_SCALE
    for hc in range(2 * H_B):
        qb_ref[hc] = jnp.concatenate(
            [z[:, hc * DH_B:(hc + 1) * DH_B].astype(BF16), slope_cols(SLOPES_B[hc // 2])], axis=1)
    if key_major:
        vt_ref, wt_ref, vbt_ref, smt_ref, kbt_ref = mode_refs
    else:
        kb_ref, vbb_ref, sm_ref, vaa_ref = mode_refs
    z = mm(C_KB, 512)
    if not key_major:
        kb_ref[...] = z
    for hc in range(2 * H_B):
        kbh_ref[hc] = jnp.concatenate([z[:, hc * DH_B:(hc + 1) * DH_B].astype(BF16), pos_cols], axis=1)
    z = mm(C_VB, 512)
    for h in range(H_B):
        vb_ref[pl.ds(h, rows, stride=H_B), :] = z[:, h * DV_B:(h + 1) * DV_B]
    if not key_major:
        vbb_ref[...] = z.astype(BF16)
    z = mm(C_QI, 256)
    for h in range(H_I):
        qi_ref[h] = z[:, h * D_IDX:(h + 1) * D_IDX].astype(BF16)
    z = mm(C_SM, 256)
    ka = z[:, SM_KA:SM_KA + 64]
    va = z[:, SM_VA:SM_VA + 64]
    ki = z[:, SM_KI:SM_KI + 64]
    xc = ki - jnp.mean(ki, axis=-1, keepdims=True)
    ki = xc * lax.rsqrt(jnp.mean(xc * xc, axis=-1, keepdims=True) + EPS) * kng_ref[...] + knb_ref[...]
    kab_ref[...] = jnp.concatenate([ka.astype(BF16), pos_cols], axis=1)
    kib_ref[...] = ki.astype(BF16)
    if not key_major:
        sm_ref[:, 0:128] = z[:, 0:128]
        sm_ref[:, SM_KI:SM_KI + 64] = ki
        sm_ref[:, SM_WI:SM_WI + 64] = z[:, SM_WI:SM_WI + 64] * W_SCALE
        vaa_ref[...] = jnp.concatenate([va.astype(BF16), jnp.ones((rows, 64), BF16)], axis=1)
    else:
        kv_t = z[:, 0:128].T
        for c in range(rows // TQ):
            vt_ref[c] = jnp.concatenate(
                [kv_t[64:128, c * TQ:(c + 1) * TQ].astype(BF16), jnp.ones((64, TQ), BF16)], axis=0)
        kw_t = jnp.concatenate([ki, z[:, SM_WI:SM_WI + 64] * W_SCALE], axis=1).T
        wt_ref[...] = kw_t[64:72, :]
        smt_ref[0:128, :] = kv_t
        smt_ref[128:192, :] = kw_t[0:64, :]
        kbt_ref[...] = _dot_nt(wvt_ref[0:512, :], hn)
        vb_t = _dot_nt(wvt_ref[512:1024, :], hn)
        for c in range(rows // TQ):
            for h in range(H_B):
                vbt_ref[c, h] = jnp.concatenate(
                    [vb_t[h * DV_B:(h + 1) * DV_B, c * TQ:(c + 1) * TQ].astype(BF16),
                     jnp.ones((VT_ROWS - DV_B, TQ), BF16)], axis=0)


def _proj(x, g, w, wvt, kng, knb, tm, pos0, period, key_major):
    m = x.shape[0]
    assert m % tm == 0
    row = lambda i: (i, 0)
    head = lambda i: (0, i, 0)
    const = lambda i: (0, 0)
    out_shape = (
        jax.ShapeDtypeStruct((H_A, m, 128), BF16),
        jax.ShapeDtypeStruct((H_I, m, D_IDX), BF16),
        jax.ShapeDtypeStruct((2 * H_B, m, 128), BF16),
        jax.ShapeDtypeStruct((2 * H_B, m, 128), BF16),
        jax.ShapeDtypeStruct((m * H_B, DV_B), F32),
        jax.ShapeDtypeStruct((m, 128), BF16),
        jax.ShapeDtypeStruct((m, 64), BF16),
    )
    out_specs = (
        pl.BlockSpec((H_A, tm, 128), head),
        pl.BlockSpec((H_I, tm, D_IDX), head),
        pl.BlockSpec((2 * H_B, tm, 128), head),
        pl.BlockSpec((2 * H_B, tm, 128), head),
        pl.BlockSpec((tm * H_B, DV_B), row),
        pl.BlockSpec((tm, 128), row),
        pl.BlockSpec((tm, 64), row),
    )
    if not key_major:
        out_shape += (
            jax.ShapeDtypeStruct((m, 512), F32),
            jax.ShapeDtypeStruct((m, 512), BF16),
            jax.ShapeDtypeStruct((m, 256), F32),
            jax.ShapeDtypeStruct((m, 128), BF16),
        )
        out_specs += (
            pl.BlockSpec((tm, 512), row),
            pl.BlockSpec((tm, 512), row),
            pl.BlockSpec((tm, 256), row),
            pl.BlockSpec((tm, 128), row),
        )
    else:
        assert tm % TQ == 0
        out_shape += (
            jax.ShapeDtypeStruct((m // TQ, 128, TQ), BF16),
            jax.ShapeDtypeStruct((8, m), F32),
            jax.ShapeDtypeStruct((m // TQ, H_B, VT_ROWS, TQ), BF16),
            jax.ShapeDtypeStruct((192, m), F32),
            jax.ShapeDtypeStruct((512, m), F32),
        )
        out_specs += (
            pl.BlockSpec((tm // TQ, 128, TQ), lambda i: (i, 0, 0)),
            pl.BlockSpec((8, tm), lambda i: (0, i)),
            pl.BlockSpec((tm // TQ, H_B, VT_ROWS, TQ), lambda i: (i, 0, 0, 0)),
            pl.BlockSpec((192, tm), lambda i: (0, i)),
            pl.BlockSpec((512, tm), lambda i: (0, i)),
        )
    return pl.pallas_call(
        functools.partial(_proj_body, pos0=pos0, period=period, key_major=key_major),
        out_shape=out_shape,
        grid=(m // tm,),
        in_specs=[
            pl.BlockSpec((tm, D_MODEL), row),
            pl.BlockSpec((1, D_MODEL), const),
            pl.BlockSpec((D_MODEL, W_COLS), const),
            pl.BlockSpec((2 * H_B * DV_B, D_MODEL), const),
            pl.BlockSpec((1, D_IDX), const),
            pl.BlockSpec((1, D_IDX), const),
        ],
        out_specs=out_specs,
        compiler_params=pltpu.CompilerParams(
            dimension_semantics=("arbitrary",), vmem_limit_bytes=VMEM_LIMIT),
        name="proj",
    )(x, g, w, wvt, kng, knb)


def _dsa_body(qa_ref, qi_ref, wt_ref, ka_ref, ki_ref, vt_ref, mka_ref, mki_ref, mva_ref,
              o_ref, smeta_ref, smain_ref, bmeta_ref, bmain_ref,
              tsel_ref, m_ref, al_ref, acc_ref, p_ref, s_ref, seen_ref, *, k_top):
    i = pl.program_id(1)
    krow = lax.broadcasted_iota(I32, (TQ, TQ), 0)
    qcol = lax.broadcasted_iota(I32, (TQ, TQ), 1)

    def main_rows(j):
        return pl.ds(pl.multiple_of(j * TQ, TQ), TQ)

    qi_all = qi_ref[...].reshape(H_I * TQ, D_IDX)

    def scores(ki_blk):
        s4 = _dot_nt(ki_blk, qi_all)
        sc = None
        for h in range(H_I):
            t = jnp.maximum(s4[:, h * TQ:(h + 1) * TQ], 0.0) * wt_ref[h:h + 1, :]
            sc = t if sc is None else sc + t
        return sc

    sc = scores(mki_ref[...])
    smeta_ref[...] = sc
    bmeta_ref[...] = sc.astype(BF16)

    def score_block(j, sc):
        smain_ref[j] = sc
        bmain_ref[j] = sc.astype(BF16)

    def score_full(j, carry):
        score_block(j, scores(ki_ref[main_rows(j), :]))
        return carry

    lax.fori_loop(0, i, score_full, 0)
    score_block(i, jnp.where((krow // CHUNK) <= (qcol // CHUNK),
                             scores(ki_ref[main_rows(i), :]), -jnp.inf))

    def count(meta_ref, main_ref, part, pred):
        def body(j, c):
            return c + part(pred(main_ref[j]))

        c = lax.fori_loop(0, i + 1, body, jnp.zeros((CNT_ROWS, TQ), F32))
        cm = part(pred(meta_ref[...]), rows=N_META)
        return jnp.sum(c, axis=0, keepdims=True) + jnp.sum(cm, axis=0, keepdims=True)

    def part32(hit, rows=CNT_ROWS):
        ones = jnp.where(hit, 1.0, 0.0)
        return jnp.sum(ones.reshape(ones.shape[0] // rows, rows, TQ), axis=0)

    def part16(hit, rows=CNT_ROWS):
        ones = jnp.where(hit, jnp.bfloat16(1), jnp.bfloat16(0))
        acc = ones[0:rows]
        for t in range(1, ones.shape[0] // rows):
            acc = acc + ones[t * rows:(t + 1) * rows]
        return acc.astype(F32)

    kf = float(k_top)

    def step16(s, u):
        cand_u = u | jnp.left_shift(jnp.int32(1), 15 - s)
        cand = _bf16_from_key(jnp.maximum(cand_u, KEY16_LOWEST + HALF16) - HALF16)
        cnt = count(bmeta_ref, bmain_ref, part16, lambda x: x >= cand)
        return jnp.where(cnt >= kf, cand_u, u)

    u_hi = lax.fori_loop(0, 16, step16, jnp.zeros((1, TQ), I32))
    t_hi = _bf16_from_key(jnp.maximum(u_hi, KEY16_LOWEST + HALF16) - HALF16).astype(F32)
    base = jnp.maximum(_to_key(t_hi) - 65536, KEY32_LOWEST)

    def step32(s, d):
        cand_d = d | jnp.left_shift(jnp.int32(1), 16 - s)
        cand = _f32_from_key(jnp.minimum(base + cand_d, KEY32_HIGHEST))
        cnt = count(smeta_ref, smain_ref, part32, lambda x: x >= cand)
        return jnp.where(cnt >= kf, cand_d, d)

    d_lo = lax.fori_loop(0, 17, step32, jnp.zeros((1, TQ), I32))
    t_sel = _f32_from_key(jnp.minimum(base + d_lo, KEY32_HIGHEST))
    need = kf - count(smeta_ref, smain_ref, part32, lambda x: x > t_sel)
    tsel_ref[...] = jnp.broadcast_to(t_sel, tsel_ref.shape)

    q_all = qa_ref[...].reshape(H_A * TQ, 128)
    m_ref[...] = jnp.full(m_ref.shape, NEG_INIT, F32)
    acc_ref[...] = jnp.zeros(acc_ref.shape, F32)
    thr = tsel_ref[0:1, :]

    def logits(k_blk):
        return _dot_nt(k_blk, q_all)

    def softmax(sc, s_all, future, seen, slot):
        nk = sc.shape[0]
        tie = sc == thr
        tri = (lax.broadcasted_iota(I32, (nk, nk), 1)
               <= lax.broadcasted_iota(I32, (nk, nk), 0)).astype(BF16)
        rank = _dot(tri, jnp.where(tie, 1.0, 0.0).astype(BF16)) + seen
        negm = jnp.where(sc > thr, 0.0, jnp.where(tie, jnp.where(rank <= need, 0.0, -jnp.inf), -jnp.inf))
        for h in range(H_A):
            cols = slice(h * TQ, (h + 1) * TQ)
            s = s_all[:, cols] + negm
            if future is not None:
                s = s - (2.0 * SLOPES_A[h]) * future
            m_old = m_ref[:, cols]
            m_new = jnp.maximum(m_old, jnp.max(s, axis=0, keepdims=True))
            p_ref[slot, 0:nk, cols] = jnp.exp(s - m_new).astype(BF16)
            al_ref[slot, :, cols] = jnp.exp(m_old - m_new)
            m_ref[:, cols] = m_new
        return rank[nk - 1:nk, :]

    def accumulate(vt_blk, slot, nk):
        acc_ref[...] = acc_ref[...] * al_ref[slot] + _dot(vt_blk, p_ref[slot, 0:nk, :])

    eye = (lax.broadcasted_iota(I32, (128, 128), 0)
           == lax.broadcasted_iota(I32, (128, 128), 1)).astype(BF16)
    seen_ref[...] = softmax(smeta_ref[...], logits(mka_ref[...]), None, jnp.zeros((1, TQ), F32), 0)
    accumulate(_dot_nt(eye, mva_ref[...]).astype(BF16), 0, N_META)

    p_ref[1] = jnp.zeros(p_ref.shape[1:], BF16)
    al_ref[1] = jnp.ones(al_ref.shape[1:], F32)
    s_ref[0] = logits(ka_ref[main_rows(0), :])

    def full_step(t, slot):
        s_ref[1 - slot] = logits(ka_ref[main_rows(t + 1), :])
        accumulate(vt_ref[jnp.maximum(t - 1, 0)], 1 - slot, TQ)
        seen_ref[...] = softmax(smain_ref[t], s_ref[slot], None, seen_ref[...], slot)

    def two_steps(u, carry):
        full_step(2 * u, 0)
        full_step(2 * u + 1, 1)
        return carry

    lax.fori_loop(0, i // 2, two_steps, 0)

    def last_step(slot):
        accumulate(vt_ref[jnp.maximum(i - 1, 0)], 1 - slot, TQ)
        softmax(smain_ref[i], s_ref[slot], jnp.maximum(krow - qcol, 0).astype(F32),
                seen_ref[...], slot)
        accumulate(vt_ref[i], slot, TQ)

    @pl.when(i % 2 == 1)
    def _odd():
        full_step(i - 1, 0)
        last_step(1)

    @pl.when(i % 2 == 0)
    def _even():
        last_step(0)

    outs = []
    for h in range(H_A):
        a = acc_ref[:, h * TQ:(h + 1) * TQ].T
        o = a / pltpu.roll(a, DH_A, axis=1)
        outs.append(o[:, :DH_A])
    o_ref[...] = jnp.concatenate(outs, axis=1).astype(BF16)


def _dsa(qa, qi, wt, kab, kib, vt, mka, mki, mva, batch, seq, k_top):
    nq = seq // TQ
    qrow = lambda b, i: (b * nq + i, 0)
    qhead = lambda b, i: (0, b * nq + i, 0)
    kv = lambda b, i: (b, 0)
    const = lambda b, i: (0, 0)
    return pl.pallas_call(
        functools.partial(_dsa_body, k_top=k_top),
        out_shape=jax.ShapeDtypeStruct((batch * seq, H_A * DH_A), BF16),
        grid=(batch, nq),
        in_specs=[
            pl.BlockSpec((H_A, TQ, 128), qhead),
            pl.BlockSpec((H_I, TQ, D_IDX), qhead),
            pl.BlockSpec((8, TQ), lambda b, i: (0, b * nq + i)),
            pl.BlockSpec((seq, 128), kv),
            pl.BlockSpec((seq, D_IDX), kv),
            pl.BlockSpec((nq, 128, TQ), lambda b, i: (b, 0, 0)),
            pl.BlockSpec((N_META, 128), const),
            pl.BlockSpec((N_META, D_IDX), const),
            pl.BlockSpec((N_META, 128), const),
        ],
        out_specs=pl.BlockSpec((TQ, H_A * DH_A), qrow),
        scratch_shapes=[
            pltpu.VMEM((N_META, TQ), F32),
            pltpu.VMEM((nq, TQ, TQ), F32),
            pltpu.VMEM((N_META, TQ), BF16),
            pltpu.VMEM((nq, TQ, TQ), BF16),
            pltpu.VMEM((8, TQ), F32),
            pltpu.VMEM((1, H_A * TQ), F32),
            pltpu.VMEM((2, 1, H_A * TQ), F32),
            pltpu.VMEM((128, H_A * TQ), F32),
            pltpu.VMEM((2, TQ, H_A * TQ), BF16),
            pltpu.VMEM((2, TQ, H_A * TQ), F32),
            pltpu.VMEM((1, TQ), F32),
        ],
        compiler_params=pltpu.CompilerParams(
            dimension_semantics=("arbitrary", "arbitrary"), vmem_limit_bytes=VMEM_LIMIT),
        name="dsa",
    )(qa, qi, wt, kab, kib, vt, mka, mki, mva)


def _lambda(lq1_ref, lk1_ref, lq2_ref, lk2_ref):
    s1 = jnp.sum(lq1_ref[...] * lk1_ref[...], axis=-1, keepdims=True)
    s2 = jnp.sum(lq2_ref[...] * lk2_ref[...], axis=-1, keepdims=True)
    return jnp.exp(s1) - jnp.exp(s2) + LAM_INIT


def _diff_body(qb_ref, kb_ref, vt_ref, mkb_ref, mvb_ref, lq1_ref, lk1_ref, lq2_ref, lk2_ref,
               sg_ref, o_ref, m_ref, al_ref, acc_ref, p_ref, s_ref):
    i = pl.program_id(1)
    n_hc = 2 * H_B
    krow = lax.broadcasted_iota(I32, (TQ, TQ), 0)
    qcol = lax.broadcasted_iota(I32, (TQ, TQ), 1)
    m_ref[...] = jnp.full(m_ref.shape, NEG_INIT, F32)
    acc_ref[...] = jnp.zeros(acc_ref.shape, F32)

    def main_rows(j):
        return pl.ds(pl.multiple_of(j * TQ, TQ), TQ)

    def logits(k_of, slot, nk):
        for hc in range(n_hc):
            s_ref[slot, hc, 0:nk, :] = _dot_nt(k_of(hc), qb_ref[hc])

    def softmax(slot, nk, negm, future):
        for hc in range(n_hc):
            cols = slice(hc * TQ, (hc + 1) * TQ)
            s = s_ref[slot, hc, 0:nk, :]
            if negm is not None:
                s = s + negm
            if future is not None:
                s = s - (2.0 * SLOPES_B[hc // 2]) * future
            m_old = m_ref[:, cols]
            m_new = jnp.maximum(m_old, jnp.max(s, axis=0, keepdims=True))
            p_ref[slot, 0:nk, cols] = jnp.exp(s - m_new).astype(BF16)
            al_ref[slot, :, cols] = jnp.exp(m_old - m_new)
            m_ref[:, cols] = m_new

    def accumulate(vt_of, slot, nk):
        for h in range(H_B):
            cols = slice(2 * h * TQ, (2 * h + 2) * TQ)
            acc_ref[h] = acc_ref[h] * al_ref[slot, :, cols] + _dot(vt_of(h), p_ref[slot, 0:nk, cols])

    eye = (lax.broadcasted_iota(I32, (DV_B, DV_B), 0)
           == lax.broadcasted_iota(I32, (DV_B, DV_B), 1)).astype(BF16)

    def meta_vt(h):
        v_t = _dot_nt(eye, mvb_ref[:, h * DV_B:(h + 1) * DV_B]).astype(BF16)
        return jnp.concatenate([v_t, jnp.ones((VT_ROWS - DV_B, N_META), BF16)], axis=0)

    logits(lambda hc: mkb_ref[hc], 0, N_META)
    softmax(0, N_META, None, None)
    accumulate(meta_vt, 0, N_META)

    p_ref[1] = jnp.zeros(p_ref.shape[1:], BF16)
    al_ref[1] = jnp.ones(al_ref.shape[1:], F32)
    logits(lambda hc: kb_ref[hc, main_rows(0), :], 0, TQ)

    def main_vt(t):
        return lambda h: vt_ref[t, h]

    def full_step(t, slot):
        logits(lambda hc: kb_ref[hc, main_rows(t + 1), :], 1 - slot, TQ)
        accumulate(main_vt(jnp.maximum(t - 1, 0)), 1 - slot, TQ)
        softmax(slot, TQ, None, None)

    def two_steps(u, carry):
        full_step(2 * u, 0)
        full_step(2 * u + 1, 1)
        return carry

    lax.fori_loop(0, i // 2, two_steps, 0)

    def last_step(slot):
        accumulate(main_vt(jnp.maximum(i - 1, 0)), 1 - slot, TQ)
        softmax(slot, TQ, jnp.where((krow // CHUNK) <= (qcol // CHUNK), 0.0, -jnp.inf),
                jnp.maximum(krow - qcol, 0).astype(F32))
        accumulate(main_vt(i), slot, TQ)

    @pl.when(i % 2 == 1)
    def _odd():
        full_step(i - 1, 0)
        last_step(1)

    @pl.when(i % 2 == 0)
    def _even():
        last_step(0)

    lam = _lambda(lq1_ref, lk1_ref, lq2_ref, lk2_ref)
    for h in range(H_B):
        a = acc_ref[h]
        o0 = a[0:DV_B, 0:TQ] / a[DV_B:DV_B + 1, 0:TQ]
        o1 = a[0:DV_B, TQ:2 * TQ] / a[DV_B:DV_B + 1, TQ:2 * TQ]
        o = _rms((o0 - lam * o1).T, sg_ref[...]) * (1.0 - LAM_INIT)
        o_ref[:, h * DV_B:(h + 1) * DV_B] = o.astype(BF16)


def _diff(qb, kbh, vbt, mkb, mvb, lq1, lk1, lq2, lk2, sg, batch, seq):
    nq = seq // TQ
    qrow = lambda b, i: (b * nq + i, 0)
    qhead = lambda b, i: (0, b * nq + i, 0)
    const2 = lambda b, i: (0, 0)
    const3 = lambda b, i: (0, 0, 0)
    return pl.pallas_call(
        _diff_body,
        out_shape=jax.ShapeDtypeStruct((batch * seq, H_B * DV_B), BF16),
        grid=(batch, nq),
        in_specs=[
            pl.BlockSpec((2 * H_B, TQ, 128), qhead),
            pl.BlockSpec((2 * H_B, seq, 128), lambda b, i: (0, b, 0)),
            pl.BlockSpec((nq, H_B, VT_ROWS, TQ), lambda b, i: (b, 0, 0, 0)),
            pl.BlockSpec((2 * H_B, N_META, 128), const3),
            pl.BlockSpec((N_META, H_B * DV_B), const2),
            pl.BlockSpec((1, DH_B), const2),
            pl.BlockSpec((1, DH_B), const2),
            pl.BlockSpec((1, DH_B), const2),
            pl.BlockSpec((1, DH_B), const2),
            pl.BlockSpec((1, DV_B), const2),
        ],
        out_specs=pl.BlockSpec((TQ, H_B * DV_B), qrow),
        scratch_shapes=[
            pltpu.VMEM((1, 2 * H_B * TQ), F32),
            pltpu.VMEM((2, 1, 2 * H_B * TQ), F32),
            pltpu.VMEM((H_B, VT_ROWS, 2 * TQ), F32),
            pltpu.VMEM((2, TQ, 2 * H_B * TQ), BF16),
            pltpu.VMEM((2, 2 * H_B, TQ, TQ), F32),
        ],
        compiler_params=pltpu.CompilerParams(
            dimension_semantics=("arbitrary", "arbitrary"), vmem_limit_bytes=VMEM_LIMIT),
        name="diff",
    )(qb, kbh, vbt, mkb, mvb, lq1, lk1, lq2, lk2, sg)


def _sample_body(qa_ref, qi_ref, sm_ref, nka_ref, nva_ref, nki_ref, qb_ref, nkb_ref, nvb_ref,
                 cak_ref, cav_ref, cai_ref, cbk_ref, cbv_ref,
                 lq1_ref, lk1_ref, lq2_ref, lk2_ref, sg_ref,
                 oa_ref, ob_ref, pc_ref, pn_ref, *, k_top, past, ts):
    pad = LANES - ts
    row_c = lax.broadcasted_iota(I32, (ts, past), 0)
    col_c = lax.broadcasted_iota(I32, (ts, past), 1)
    row_n = lax.broadcasted_iota(I32, (ts, LANES), 0)
    col_n = lax.broadcasted_iota(I32, (ts, LANES), 1)
    new_ok = col_n < ts

    def pad_rows(x):
        return jnp.concatenate([x, jnp.zeros((pad,) + x.shape[1:], x.dtype)], axis=0)

    qi_all = qi_ref[...].reshape(H_I * ts, D_IDX)
    w = sm_ref[:, SM_WI:SM_WI + H_I]
    s4c = _dot(qi_all, cai_ref[...].astype(BF16))
    s4n = _dot_nt(qi_all, pad_rows(nki_ref[...]))
    sc_c = None
    sc_n = None
    for h in range(H_I):
        wh = w[:, h:h + 1]
        tc = jnp.maximum(s4c[h * ts:(h + 1) * ts], 0.0) * wh
        tn = jnp.maximum(s4n[h * ts:(h + 1) * ts], 0.0) * wh
        sc_c = tc if sc_c is None else sc_c + tc
        sc_n = tn if sc_n is None else sc_n + tn
    sc_n = jnp.where(new_ok, sc_n, -jnp.inf)

    def count(pred_c, pred_n):
        c = jnp.sum(jnp.where(pred_c, 1.0, 0.0), axis=1, keepdims=True)
        return c + jnp.sum(jnp.where(pred_n, 1.0, 0.0), axis=1, keepdims=True)

    def threshold(key):
        return _f32_from_key(jnp.clip(key, KEY32_LOWEST, KEY32_HIGHEST))

    def value_step(s, key):
        cand_key = key ^ jnp.left_shift(jnp.int32(1), 31 - s)
        cand = threshold(cand_key)
        cnt = count(sc_c >= cand, sc_n >= cand)
        return jnp.where(cnt >= float(k_top), cand_key, key)

    t = threshold(lax.fori_loop(0, 32, value_step, jnp.full((ts, 1), INT_MIN, I32)))
    need = float(k_top) - count(sc_c > t, sc_n > t)
    idx_n = col_n + past

    def tie_step(s, x):
        cand = x | jnp.left_shift(jnp.int32(1), 11 - s)
        cnt = count((sc_c == t) & (col_c < cand), (sc_n == t) & (idx_n < cand))
        return jnp.where(cnt < need, cand, x)

    x = lax.fori_loop(0, 12, tie_step, jnp.zeros((ts, 1), I32))
    negm_c = jnp.where((sc_c > t) | ((sc_c == t) & (col_c <= x)), 0.0, -jnp.inf)
    negm_n = jnp.where((sc_n > t) | ((sc_n == t) & (idx_n <= x)), 0.0, -jnp.inf)

    dist_c = (row_c - col_c + past).astype(F32)
    dist_n = jnp.abs(row_n - col_n).astype(F32)
    q_all = qa_ref[...][:, :, 0:DH_A].reshape(H_A * ts, DH_A)
    s_c = _dot(q_all, cak_ref[...].astype(BF16))
    s_n = _dot_nt(q_all, pad_rows(nka_ref[...][:, 0:DH_A]))
    for h in range(H_A):
        rows = slice(h * ts, (h + 1) * ts)
        lc = s_c[rows] + (negm_c - SLOPES_A[h] * dist_c)
        ln = s_n[rows] + (negm_n - SLOPES_A[h] * dist_n)
        m = jnp.maximum(jnp.max(lc, axis=1, keepdims=True), jnp.max(ln, axis=1, keepdims=True))
        m = jnp.maximum(m, NEG_INIT)
        pc_ref[rows, :] = jnp.exp(lc - m).astype(BF16)
        pn_ref[rows, :] = jnp.exp(ln - m).astype(BF16)
    vc_aug_t = jnp.concatenate([cav_ref[...].astype(BF16), jnp.ones((64, past), BF16)], axis=0)
    pv = _dot_nt(pc_ref[...], vc_aug_t) + _dot(pn_ref[...], pad_rows(nva_ref[...]))
    pv = pv / pltpu.roll(pv, DH_A, axis=1)
    oa_ref[...] = jnp.concatenate(
        [pv[h * ts:(h + 1) * ts, :DH_A] for h in range(H_A)], axis=1).astype(BF16)

    lam = _lambda(lq1_ref, lk1_ref, lq2_ref, lk2_ref)
    neg_new = jnp.where(new_ok, 0.0, -jnp.inf)
    for h in range(H_B):
        a_c = None
        a_n = None
        for c in range(2):
            hc = 2 * h + c
            q = qb_ref[hc][:, 0:DH_B]
            kc_t = cbk_ref[hc * DH_B:(hc + 1) * DH_B, :].astype(BF16)
            lc = _dot(q, kc_t) - SLOPES_B[h] * dist_c
            ln = _dot_nt(q, pad_rows(nkb_ref[hc][:, 0:DH_B])) + (neg_new - SLOPES_B[h] * dist_n)
            m = jnp.maximum(jnp.max(lc, axis=1, keepdims=True), jnp.max(ln, axis=1, keepdims=True))
            pc = jnp.exp(lc - m)
            pn = jnp.exp(ln - m)
            l = jnp.sum(pc, axis=1, keepdims=True) + jnp.sum(pn, axis=1, keepdims=True)
            pc = pc / l
            pn = pn / l
            if c == 0:
                a_c, a_n = pc, pn
            else:
                a_c, a_n = a_c - lam * pc, a_n - lam * pn
        vc = cbv_ref[pl.ds(h, past, stride=H_B), :].astype(BF16)
        vn = pad_rows(nvb_ref[:, h * DV_B:(h + 1) * DV_B])
        o = _dot(a_c.astype(BF16), vc) + _dot(a_n.astype(BF16), vn)
        o = _rms(o, sg_ref[...]) * (1.0 - LAM_INIT)
        ob_ref[:, h * DV_B:(h + 1) * DV_B] = o.astype(BF16)


def _sample(qa, qi, sm, nka, nva, nki, qb, nkb, nvb, cak, cav, cai, cbk, cbv,
            lq1, lk1, lq2, lk2, sg, batch, ts, past, k_top):
    row = lambda b: (b, 0)
    head = lambda b: (0, b, 0)
    cache = lambda b: (b, 0, 0)
    const = lambda b: (0, 0)
    return pl.pallas_call(
        functools.partial(_sample_body, k_top=k_top, past=past, ts=ts),
        out_shape=(jax.ShapeDtypeStruct((batch * ts, H_A * DH_A), BF16),
                   jax.ShapeDtypeStruct((batch * ts, H_B * DV_B), BF16)),
        grid=(batch,),
        in_specs=[
            pl.BlockSpec((H_A, ts, 128), head),
            pl.BlockSpec((H_I, ts, D_IDX), head),
            pl.BlockSpec((ts, 256), row),
            pl.BlockSpec((ts, 128), row),
            pl.BlockSpec((ts, 128), row),
            pl.BlockSpec((ts, D_IDX), row),
            pl.BlockSpec((2 * H_B, ts, 128), head),
            pl.BlockSpec((2 * H_B, ts, 128), head),
            pl.BlockSpec((ts, H_B * DV_B), row),
            pl.BlockSpec((None, DH_A, past), cache),
            pl.BlockSpec((None, DH_A, past), cache),
            pl.BlockSpec((None, D_IDX, past), cache),
            pl.BlockSpec((None, 2 * H_B * DH_B, past), cache),
            pl.BlockSpec((None, past * H_B, DV_B), cache),
            pl.BlockSpec((1, DH_B), const),
            pl.BlockSpec((1, DH_B), const),
            pl.BlockSpec((1, DH_B), const),
            pl.BlockSpec((1, DH_B), const),
            pl.BlockSpec((1, DV_B), const),
        ],
        out_specs=(pl.BlockSpec((ts, H_A * DH_A), row), pl.BlockSpec((ts, H_B * DV_B), row)),
        scratch_shapes=[
            pltpu.VMEM((H_A * ts, past), BF16),
            pltpu.VMEM((H_A * ts, LANES), BF16),
        ],
        compiler_params=pltpu.CompilerParams(
            dimension_semantics=("arbitrary",), vmem_limit_bytes=VMEM_LIMIT),
        name="sample",
    )(qa, qi, sm, nka, nva, nki, qb, nkb, nvb, cak, cav, cai, cbk, cbv, lq1, lk1, lq2, lk2, sg)


def _finish_body(x_ref, oa_ref, ob_ref, wo_ref, gm_ref, wu_ref, wd_ref, gf_ref, y_ref):
    o = jnp.concatenate([oa_ref[...], ob_ref[...]], axis=1)
    h1 = x_ref[...] + _dot(o, wo_ref[...])
    hn = _rms(h1, gm_ref[...]).astype(BF16)
    acc = h1
    for c in range(D_FF // D_MODEL):
        u = jnp.maximum(_dot(hn, wu_ref[:, c * D_MODEL:(c + 1) * D_MODEL]), 0.0)
        acc = acc + _dot((u * u).astype(BF16), wd_ref[c * D_MODEL:(c + 1) * D_MODEL, :])
    y_ref[...] = _rms(acc, gf_ref[...])


def _finish(x, oa, ob, wo, gm, wu, wd, gf, tm):
    m = x.shape[0]
    assert m % tm == 0
    row = lambda i: (i, 0)
    const = lambda i: (0, 0)
    resident = dict(pipeline_mode=pl.Buffered(1))
    return pl.pallas_call(
        _finish_body,
        out_shape=jax.ShapeDtypeStruct((m, D_MODEL), F32),
        grid=(m // tm,),
        in_specs=[
            pl.BlockSpec((tm, D_MODEL), row),
            pl.BlockSpec((tm, 512), row),
            pl.BlockSpec((tm, 512), row),
            pl.BlockSpec((D_MODEL, D_MODEL), const, **resident),
            pl.BlockSpec((1, D_MODEL), const),
            pl.BlockSpec((D_MODEL, D_FF), const, **resident),
            pl.BlockSpec((D_FF, D_MODEL), const, **resident),
            pl.BlockSpec((1, D_MODEL), const),
        ],
        out_specs=pl.BlockSpec((tm, D_MODEL), row),
        compiler_params=pltpu.CompilerParams(
            dimension_semantics=("arbitrary",), vmem_limit_bytes=VMEM_LIMIT),
        name="finish",
    )(x, oa, ob, wo, gm, wu, wd, gf)


def kernel(x_prompt, x_sample, cache_a_k, cache_a_v, cache_a_idx_k, cache_b_k, cache_b_v,
           meta_tokens, attn_norm_g, w_in, idx_k_norm_g, idx_k_norm_b,
           lambda_q1, lambda_k1, lambda_q2, lambda_k2, subln_g, w_o,
           mlp_norm_g, w_up, w_down, final_norm_g):
    batch, seq, _ = x_prompt.shape
    dec_batch, ts, _ = x_sample.shape
    past = cache_a_k.shape[2]
    assert attn_norm_g.shape[0] == 1, "single-layer step"
    assert seq % TQ == 0 and ts == 16 and past % LANES == 0 and past // CHUNK == (past + ts - 1) // CHUNK
    n = N_META + seq
    k_top_p = min(TOPK_MAX, seq // 4)
    k_top_s = min(TOPK_MAX, (past + ts) // 4)

    w_t = w_in[0].T.astype(BF16)
    w = jnp.concatenate(
        [w_t[0:512],
         w_t[964:1476],
         w_t[1476:1988],
         w_t[1988:2500],
         w_t[640:896],
         w_t[512:640],
         w_t[896:960],
         w_t[960:964],
         jnp.zeros((W_COLS - 2500, D_MODEL), BF16)], axis=0).T
    wvt = w_t[1476:2500]
    wo = w_o[0].astype(BF16)
    wu = w_up[0].astype(BF16)
    wd = w_down[0].astype(BF16)
    g_attn = attn_norm_g[0][None]
    g_mlp = mlp_norm_g[0][None]
    g_fin = final_norm_g[None]
    kng = idx_k_norm_g[0][None]
    knb = idx_k_norm_b[0][None]
    lq1, lk1, lq2, lk2 = lambda_q1[0][None], lambda_k1[0][None], lambda_q2[0][None], lambda_k2[0][None]
    sg = subln_g[0][None]

    xp = x_prompt.reshape(batch * seq, D_MODEL)
    xs = x_sample.reshape(dec_batch * ts, D_MODEL)

    (qa_p, qi_p, qb_p, kbh_p, vb_p, kab_p, kib_p, vt_p, wt_p, vbt_p, smt_p, kbt_p) = _proj(
        xp, g_attn, w, wvt, kng, knb, 512, N_META, seq, True)
    (_, _, _, kbh_m, vb_m, kab_m, kib_m, kb_m, vbb_m, sm_m, vaa_m) = _proj(
        meta_tokens, g_attn, w, wvt, kng, knb, N_META, 0, N_META, False)
    (qa_s, qi_s, qb_s, kbh_s, vb_s, kab_s, kib_s, kb_s, vbb_s, sm_s, vaa_s) = _proj(
        xs, g_attn, w, wvt, kng, knb, dec_batch * ts, past, dec_batch * ts, False)

    oa_p = _dsa(qa_p, qi_p, wt_p, kab_p, kib_p, vt_p, kab_m, kib_m, vaa_m, batch, seq, k_top_p)
    ob_p = _diff(qb_p, kbh_p, vbt_p, kbh_m, vbb_m, lq1, lk1, lq2, lk2, sg, batch, seq)
    y_prompt = _finish(xp, oa_p, ob_p, wo, g_mlp, wu, wd, g_fin, 512).reshape(batch, seq, D_MODEL)

    oa_s, ob_s = _sample(
        qa_s, qi_s, sm_s, kab_s, vaa_s, kib_s, qb_s, kbh_s, vbb_s,
        jnp.swapaxes(cache_a_k[0], 1, 2), jnp.swapaxes(cache_a_v[0], 1, 2),
        jnp.swapaxes(cache_a_idx_k[0], 1, 2),
        jnp.transpose(cache_b_k[0], (0, 2, 3, 4, 1)).reshape(dec_batch, 2 * H_B * DH_B, past),
        cache_b_v[0].reshape(dec_batch, past * H_B, DV_B),
        lq1, lk1, lq2, lk2, sg, dec_batch, ts, past, k_top_s)
    y_sample = _finish(xs, oa_s, ob_s, wo, g_mlp, wu, wd, g_fin, dec_batch * ts).reshape(
        dec_batch, ts, D_MODEL)

    def with_meta_t(meta_rows, main_t):
        meta_b = jnp.broadcast_to(meta_rows.T[None], (batch, 64, N_META))
        main = jnp.swapaxes(main_t.reshape(64, batch, seq), 0, 1)
        return jnp.swapaxes(jnp.concatenate([meta_b, main], axis=2), 1, 2)[None]

    new_a_k_p = with_meta_t(sm_m[:, SM_KA:SM_KA + 64], smt_p[SM_KA:SM_KA + 64])
    new_a_v_p = with_meta_t(sm_m[:, SM_VA:SM_VA + 64], smt_p[SM_VA:SM_VA + 64])
    new_a_i_p = with_meta_t(sm_m[:, SM_KI:SM_KI + 64], smt_p[SM_KI:SM_KI + 64])
    kb_meta_t = jnp.broadcast_to(kb_m.T.reshape(1, H_B, 2, DH_B, N_META), (batch, H_B, 2, DH_B, N_META))
    kb_main_t = jnp.transpose(kbt_p.reshape(H_B, 2, DH_B, batch, seq), (3, 0, 1, 2, 4))
    new_b_k_p = jnp.transpose(
        jnp.concatenate([kb_meta_t, kb_main_t], axis=4), (0, 4, 1, 2, 3))[None]
    new_b_v_p = jnp.concatenate(
        [jnp.broadcast_to(vb_m.reshape(1, N_META, H_B, DV_B), (batch, N_META, H_B, DV_B)),
         vb_p.reshape(batch, seq, H_B, DV_B)], axis=1)[None]
    new_a_k_s = sm_s[:, SM_KA:SM_KA + 64].reshape(1, dec_batch, ts, 64)
    new_a_v_s = sm_s[:, SM_VA:SM_VA + 64].reshape(1, dec_batch, ts, 64)
    new_a_i_s = sm_s[:, SM_KI:SM_KI + 64].reshape(1, dec_batch, ts, 64)
    new_b_k_s = kb_s.reshape(1, dec_batch, ts, H_B, 2, DH_B)
    new_b_v_s = vb_s.reshape(1, dec_batch, ts, H_B, DV_B)
    return (y_prompt, y_sample, new_a_k_p, new_a_v_p, new_a_i_p, new_b_k_p, new_b_v_p,
            new_a_k_s, new_a_v_s, new_a_i_s, new_b_k_s, new_b_v_s)
```

```python
import functools

import jax
import jax.numpy as jnp
import numpy as np
from jax import lax
from jax.experimental import pallas as pl
from jax.experimental.pallas import tpu as pltpu

F32 = jnp.float32
BF16 = jnp.bfloat16
I32 = jnp.int32
I16 = jnp.int16
HALF16 = 32768

D_MODEL = 1024
CHUNK = 64
N_META = 16
H_A = 8
DH_A = 64
H_I = 4
D_IDX = 64
TOPK_MAX = 256
H_B = 4
DH_B = 64
DV_B = 2 * DH_B
D_FF = 4 * D_MODEL
EPS = 1e-6
LAM_INIT = 0.2

LANES = 128
C_QA, C_QB, C_KB, C_VB, C_QI, C_SM = 0, 512, 1024, 1536, 2048, 2304
W_COLS = 2560
SM_KA, SM_VA, SM_KI, SM_WI = 0, 64, 128, 192

LOG2E = float(np.float32(1.4426950408889634))
Q_SCALE = (DH_A ** -0.5) * LOG2E
N_SLOPE_PIECES = 3


def _bf16_pieces(value):
    rest = np.float32(value)
    pieces = []
    for _ in range(N_SLOPE_PIECES):
        piece = np.float32(rest.astype(jnp.bfloat16))
        pieces.append(float(piece))
        rest = np.float32(rest - piece)
    assert rest == 0.0
    return pieces
W_SCALE = (H_I ** -0.5) * (D_IDX ** -0.5)
SLOPES_A = tuple(2.0 ** (-8.0 * (i + 1) / H_A) for i in range(H_A))
SLOPES_B = tuple(2.0 ** (-8.0 * (i + 1) / H_B) for i in range(H_B))

INT_MIN = -(2 ** 31)
NEG_INIT = -1e30
TQ = 256
VT_ROWS = DV_B + 16
CNT_ROWS = 32
VMEM_LIMIT = 56 * 1024 * 1024

_NT = (((1,), (1,)), ((), ()))


def _dot(a, b):
    return jnp.dot(a, b, preferred_element_type=F32)


def _dot_nt(a, b):
    return lax.dot_general(a, b, _NT, preferred_element_type=F32)


def _rms(x, g):
    return (x * lax.rsqrt(jnp.mean(x * x, axis=-1, keepdims=True) + EPS)) * g


def _to_key(score):
    b = lax.bitcast_convert_type(score, I32)
    return b ^ ((b >> 31) & jnp.int32(0x7FFFFFFF))


def _f32_from_key(key):
    return lax.bitcast_convert_type(key ^ ((key >> 31) & jnp.int32(0x7FFFFFFF)), F32)


def _bf16_from_key(key):
    bits = key ^ ((key >> 15) & jnp.int32(0x7FFF))
    return lax.bitcast_convert_type(jnp.left_shift(bits, 16), F32).astype(BF16)


KEY16_LOWEST = -32640
KEY32_LOWEST = -2139095040
KEY32_HIGHEST = 2139095039


def _wide(x, width):
    reps = width // LANES
    return x if reps == 1 else jnp.concatenate([x] * reps, axis=1)


def _proj_body(x_ref, g_ref, w_ref, wvt_ref, kng_ref, knb_ref,
               qa_ref, qi_ref, qb_ref, kbh_ref, vb_ref, kab_ref, kib_ref, *mode_refs,
               pos0, period, key_major):
    x = x_ref[...]
    hn = _rms(x, g_ref[...]).astype(BF16)
    rows = x.shape[0]

    def mm(lo, width):
        return _dot(hn, w_ref[:, lo:lo + width])

    lane = lax.broadcasted_iota(I32, (rows, 64), 1)
    r = pl.program_id(0) * rows + lax.broadcasted_iota(I32, (rows, 64), 0)
    pos = pos0 + r % period
    pos_cols = jnp.where(lane >= 2 * N_SLOPE_PIECES, 0,
                         jnp.where(lane % 2 == 0, (pos // 256) * 256, pos % 256))
    pos_cols = pos_cols.astype(F32).astype(BF16)

    def slope_cols(slope):
        cols = jnp.zeros((rows, 64), F32)
        for n, piece in enumerate(_bf16_pieces(slope * LOG2E)):
            cols = jnp.where(lane // 2 == n, piece, cols)
        return cols.astype(BF16)

    z = mm(C_QA, 512) * Q_SCALE
    for h in range(H_A):
        qa_ref[h] = jnp.concatenate(
            [z[:, h * DH_A:(h + 1) * DH_A].astype(BF16), slope_cols(SLOPES_A[h])], axis=1)
    z = mm(C_QB, 512) * Q_SCALE
    for hc in range(2 * H_B):
        qb_ref[hc] = jnp.concatenate(
            [z[:, hc * DH_B:(hc + 1) * DH_B].astype(BF16), slope_cols(SLOPES_B[hc // 2])], axis=1)
    if key_major:
        vt_ref, wt_ref, vbt_ref, smt_ref, kbt_ref = mode_refs
    else:
        kb_ref, vbb_ref, sm_ref, vaa_ref = mode_refs
    z = mm(C_KB, 512)
    if not key_major:
        kb_ref[...] = z
    for hc in range(2 * H_B):
        kbh_ref[hc] = jnp.concatenate([z[:, hc * DH_B:(hc + 1) * DH_B].astype(BF16), pos_cols], axis=1)
    z = mm(C_VB, 512)
    for h in range(H_B):
        vb_ref[pl.ds(h, rows, stride=H_B), :] = z[:, h * DV_B:(h + 1) * DV_B]
    if not key_major:
        vbb_ref[...] = z.astype(BF16)
    z = mm(C_QI, 256)
    for h in range(H_I):
        qi_ref[h] = z[:, h * D_IDX:(h + 1) * D_IDX].astype(BF16)
    z = mm(C_SM, 256)
    ka = z[:, SM_KA:SM_KA + 64]
    va = z[:, SM_VA:SM_VA + 64]
    ki = z[:, SM_KI:SM_KI + 64]
    xc = ki - jnp.mean(ki, axis=-1, keepdims=True)
    ki = xc * lax.rsqrt(jnp.mean(xc * xc, axis=-1, keepdims=True) + EPS) * kng_ref[...] + knb_ref[...]
    kab_ref[...] = jnp.concatenate([ka.astype(BF16), pos_cols], axis=1)
    kib_ref[...] = ki.astype(BF16)
    if not key_major:
        sm_ref[:, 0:128] = z[:, 0:128]
        sm_ref[:, SM_KI:SM_KI + 64] = ki
        sm_ref[:, SM_WI:SM_WI + 64] = z[:, SM_WI:SM_WI + 64] * W_SCALE
        vaa_ref[...] = jnp.concatenate([va.astype(BF16), jnp.ones((rows, 64), BF16)], axis=1)
    else:
        kv_t = z[:, 0:128].T
        for c in range(rows // TQ):
            vt_ref[c] = jnp.concatenate(
                [kv_t[64:128, c * TQ:(c + 1) * TQ].astype(BF16), jnp.ones((64, TQ), BF16)], axis=0)
        kw_t = jnp.concatenate([ki, z[:, SM_WI:SM_WI + 64] * W_SCALE], axis=1).T
        wt_ref[...] = kw_t[64:72, :]
        smt_ref[0:128, :] = kv_t
        smt_ref[128:192, :] = kw_t[0:64, :]
        kbt_ref[...] = _dot_nt(wvt_ref[0:512, :], hn)
        vb_t = _dot_nt(wvt_ref[512:1024, :], hn)
        for c in range(rows // TQ):
            for h in range(H_B):
                vbt_ref[c, h] = jnp.concatenate(
                    [vb_t[h * DV_B:(h + 1) * DV_B, c * TQ:(c + 1) * TQ].astype(BF16),
                     jnp.ones((VT_ROWS - DV_B, TQ), BF16)], axis=0)


def _proj(x, g, w, wvt, kng, knb, tm, pos0, period, key_major):
    m = x.shape[0]
    assert m % tm == 0
    row = lambda i: (i, 0)
    head = lambda i: (0, i, 0)
    const = lambda i: (0, 0)
    out_shape = (
        jax.ShapeDtypeStruct((H_A, m, 128), BF16),
        jax.ShapeDtypeStruct((H_I, m, D_IDX), BF16),
        jax.ShapeDtypeStruct((2 * H_B, m, 128), BF16),
        jax.ShapeDtypeStruct((2 * H_B, m, 128), BF16),
        jax.ShapeDtypeStruct((m * H_B, DV_B), F32),
        jax.ShapeDtypeStruct((m, 128), BF16),
        jax.ShapeDtypeStruct((m, 64), BF16),
    )
    out_specs = (
        pl.BlockSpec((H_A, tm, 128), head),
        pl.BlockSpec((H_I, tm, D_IDX), head),
        pl.BlockSpec((2 * H_B, tm, 128), head),
        pl.BlockSpec((2 * H_B, tm, 128), head),
        pl.BlockSpec((tm * H_B, DV_B), row),
        pl.BlockSpec((tm, 128), row),
        pl.BlockSpec((tm, 64), row),
    )
    if not key_major:
        out_shape += (
            jax.ShapeDtypeStruct((m, 512), F32),
            jax.ShapeDtypeStruct((m, 512), BF16),
            jax.ShapeDtypeStruct((m, 256), F32),
            jax.ShapeDtypeStruct((m, 128), BF16),
        )
        out_specs += (
            pl.BlockSpec((tm, 512), row),
            pl.BlockSpec((tm, 512), row),
            pl.BlockSpec((tm, 256), row),
            pl.BlockSpec((tm, 128), row),
        )
    else:
        assert tm % TQ == 0
        out_shape += (
            jax.ShapeDtypeStruct((m // TQ, 128, TQ), BF16),
            jax.ShapeDtypeStruct((8, m), F32),
            jax.ShapeDtypeStruct((m // TQ, H_B, VT_ROWS, TQ), BF16),
            jax.ShapeDtypeStruct((192, m), F32),
            jax.ShapeDtypeStruct((512, m), F32),
        )
        out_specs += (
            pl.BlockSpec((tm // TQ, 128, TQ), lambda i: (i, 0, 0)),
            pl.BlockSpec((8, tm), lambda i: (0, i)),
            pl.BlockSpec((tm // TQ, H_B, VT_ROWS, TQ), lambda i: (i, 0, 0, 0)),
            pl.BlockSpec((192, tm), lambda i: (0, i)),
            pl.BlockSpec((512, tm), lambda i: (0, i)),
        )
    return pl.pallas_call(
        functools.partial(_proj_body, pos0=pos0, period=period, key_major=key_major),
        out_shape=out_shape,
        grid=(m // tm,),
        in_specs=[
            pl.BlockSpec((tm, D_MODEL), row),
            pl.BlockSpec((1, D_MODEL), const),
            pl.BlockSpec((D_MODEL, W_COLS), const),
            pl.BlockSpec((2 * H_B * DV_B, D_MODEL), const),
            pl.BlockSpec((1, D_IDX), const),
            pl.BlockSpec((1, D_IDX), const),
        ],
        out_specs=out_specs,
        compiler_params=pltpu.CompilerParams(
            dimension_semantics=("arbitrary",), vmem_limit_bytes=VMEM_LIMIT),
        name="proj",
    )(x, g, w, wvt, kng, knb)


def _dsa_body(qa_ref, qi_ref, wt_ref, ka_ref, ki_ref, vt_ref, mka_ref, mki_ref, mva_ref,
              o_ref, smeta_ref, smain_ref, bmeta_ref, bmain_ref,
              tsel_ref, m_ref, al_ref, acc_ref, p_ref, s_ref, seen_ref, *, k_top):
    i = pl.program_id(1)
    krow = lax.broadcasted_iota(I32, (TQ, TQ), 0)
    qcol = lax.broadcasted_iota(I32, (TQ, TQ), 1)

    def main_rows(j):
        return pl.ds(pl.multiple_of(j * TQ, TQ), TQ)

    qi_all = qi_ref[...].reshape(H_I * TQ, D_IDX)

    def scores(ki_blk):
        s4 = _dot_nt(ki_blk, qi_all)
        sc = None
        for h in range(H_I):
            t = jnp.maximum(s4[:, h * TQ:(h + 1) * TQ], 0.0) * wt_ref[h:h + 1, :]
            sc = t if sc is None else sc + t
        return sc

    sc = scores(mki_ref[...])
    smeta_ref[...] = sc
    bmeta_ref[...] = sc.astype(BF16)

    def score_block(j, sc):
        smain_ref[j] = sc
        bmain_ref[j] = sc.astype(BF16)

    def score_full(j, carry):
        score_block(j, scores(ki_ref[main_rows(j), :]))
        return carry

    lax.fori_loop(0, i, score_full, 0)
    score_block(i, jnp.where((krow // CHUNK) <= (qcol // CHUNK),
                             scores(ki_ref[main_rows(i), :]), -jnp.inf))

    def count(meta_ref, main_ref, part, pred):
        def body(j, c):
            return c + part(pred(main_ref[j]))

        c = lax.fori_loop(0, i + 1, body, jnp.zeros((CNT_ROWS, TQ), F32))
        cm = part(pred(meta_ref[...]), rows=N_META)
        return jnp.sum(c, axis=0, keepdims=True) + jnp.sum(cm, axis=0, keepdims=True)

    def part32(hit, rows=CNT_ROWS):
        ones = jnp.where(hit, 1.0, 0.0)
        return jnp.sum(ones.reshape(ones.shape[0] // rows, rows, TQ), axis=0)

    def part16(hit, rows=CNT_ROWS):
        ones = jnp.where(hit, jnp.bfloat16(1), jnp.bfloat16(0))
        acc = ones[0:rows]
        for t in range(1, ones.shape[0] // rows):
            acc = acc + ones[t * rows:(t + 1) * rows]
        return acc.astype(F32)

    kf = float(k_top)

    def step16(s, u):
        cand_u = u | jnp.left_shift(jnp.int32(1), 15 - s)
        cand = _bf16_from_key(jnp.maximum(cand_u, KEY16_LOWEST + HALF16) - HALF16)
        cnt = count(bmeta_ref, bmain_ref, part16, lambda x: x >= cand)
        return jnp.where(cnt >= kf, cand_u, u)

    u_hi = lax.fori_loop(0, 16, step16, jnp.zeros((1, TQ), I32))
    t_hi = _bf16_from_key(jnp.maximum(u_hi, KEY16_LOWEST + HALF16) - HALF16).astype(F32)
    base = jnp.maximum(_to_key(t_hi) - 65536, KEY32_LOWEST)

    def step32(s, d):
        cand_d = d | jnp.left_shift(jnp.int32(1), 16 - s)
        cand = _f32_from_key(jnp.minimum(base + cand_d, KEY32_HIGHEST))
        cnt = count(smeta_ref, smain_ref, part32, lambda x: x >= cand)
        return jnp.where(cnt >= kf, cand_d, d)

    d_lo = lax.fori_loop(0, 17, step32, jnp.zeros((1, TQ), I32))
    t_sel = _f32_from_key(jnp.minimum(base + d_lo, KEY32_HIGHEST))
    need = kf - count(smeta_ref, smain_ref, part32, lambda x: x > t_sel)
    tsel_ref[...] = jnp.broadcast_to(t_sel, tsel_ref.shape)

    q_all = qa_ref[...].reshape(H_A * TQ, 128)
    m_ref[...] = jnp.full(m_ref.shape, NEG_INIT, F32)
    acc_ref[...] = jnp.zeros(acc_ref.shape, F32)
    thr = tsel_ref[0:1, :]

    def logits(k_blk):
        return _dot_nt(k_blk, q_all)

    def softmax(sc, s_all, future, seen, slot):
        nk = sc.shape[0]
        tie = sc == thr
        tri = (lax.broadcasted_iota(I32, (nk, nk), 1)
               <= lax.broadcasted_iota(I32, (nk, nk), 0)).astype(BF16)
        rank = _dot(tri, jnp.where(tie, 1.0, 0.0).astype(BF16)) + seen
        negm = jnp.where(sc > thr, 0.0, jnp.where(tie, jnp.where(rank <= need, 0.0, -jnp.inf), -jnp.inf))
        for h in range(H_A):
            cols = slice(h * TQ, (h + 1) * TQ)
            s = s_all[:, cols] + negm
            if future is not None:
                s = s - (2.0 * LOG2E * SLOPES_A[h]) * future
            m_old = m_ref[:, cols]
            m_new = jnp.maximum(m_old, jnp.max(s, axis=0, keepdims=True))
            p_ref[slot, 0:nk, cols] = jnp.exp2(s - m_new).astype(BF16)
            al_ref[slot, :, cols] = jnp.exp2(m_old - m_new)
            m_ref[:, cols] = m_new
        return rank[nk - 1:nk, :]

    def accumulate(vt_blk, slot, nk):
        acc_ref[...] = acc_ref[...] * al_ref[slot] + _dot(vt_blk, p_ref[slot, 0:nk, :])

    eye = (lax.broadcasted_iota(I32, (128, 128), 0)
           == lax.broadcasted_iota(I32, (128, 128), 1)).astype(BF16)
    seen_ref[...] = softmax(smeta_ref[...], logits(mka_ref[...]), None, jnp.zeros((1, TQ), F32), 0)
    accumulate(_dot_nt(eye, mva_ref[...]).astype(BF16), 0, N_META)

    p_ref[1] = jnp.zeros(p_ref.shape[1:], BF16)
    al_ref[1] = jnp.ones(al_ref.shape[1:], F32)
    s_ref[0] = logits(ka_ref[main_rows(0), :])

    def full_step(t, slot):
        s_ref[1 - slot] = logits(ka_ref[main_rows(t + 1), :])
        accumulate(vt_ref[jnp.maximum(t - 1, 0)], 1 - slot, TQ)
        seen_ref[...] = softmax(smain_ref[t], s_ref[slot], None, seen_ref[...], slot)

    def two_steps(u, carry):
        full_step(2 * u, 0)
        full_step(2 * u + 1, 1)
        return carry

    lax.fori_loop(0, i // 2, two_steps, 0)

    def last_step(slot):
        accumulate(vt_ref[jnp.maximum(i - 1, 0)], 1 - slot, TQ)
        softmax(smain_ref[i], s_ref[slot], jnp.maximum(krow - qcol, 0).astype(F32),
                seen_ref[...], slot)
        accumulate(vt_ref[i], slot, TQ)

    @pl.when(i % 2 == 1)
    def _odd():
        full_step(i - 1, 0)
        last_step(1)

    @pl.when(i % 2 == 0)
    def _even():
        last_step(0)

    outs = []
    for h in range(H_A):
        a = acc_ref[:, h * TQ:(h + 1) * TQ].T
        o = a / pltpu.roll(a, DH_A, axis=1)
        outs.append(o[:, :DH_A])
    o_ref[...] = jnp.concatenate(outs, axis=1).astype(BF16)


def _dsa(qa, qi, wt, kab, kib, vt, mka, mki, mva, batch, seq, k_top):
    nq = seq // TQ
    qrow = lambda b, i: (b * nq + i, 0)
    qhead = lambda b, i: (0, b * nq + i, 0)
    kv = lambda b, i: (b, 0)
    const = lambda b, i: (0, 0)
    return pl.pallas_call(
        functools.partial(_dsa_body, k_top=k_top),
        out_shape=jax.ShapeDtypeStruct((batch * seq, H_A * DH_A), BF16),
        grid=(batch, nq),
        in_specs=[
            pl.BlockSpec((H_A, TQ, 128), qhead),
            pl.BlockSpec((H_I, TQ, D_IDX), qhead),
            pl.BlockSpec((8, TQ), lambda b, i: (0, b * nq + i)),
            pl.BlockSpec((seq, 128), kv),
            pl.BlockSpec((seq, D_IDX), kv),
            pl.BlockSpec((nq, 128, TQ), lambda b, i: (b, 0, 0)),
            pl.BlockSpec((N_META, 128), const),
            pl.BlockSpec((N_META, D_IDX), const),
            pl.BlockSpec((N_META, 128), const),
        ],
        out_specs=pl.BlockSpec((TQ, H_A * DH_A), qrow),
        scratch_shapes=[
            pltpu.VMEM((N_META, TQ), F32),
            pltpu.VMEM((nq, TQ, TQ), F32),
            pltpu.VMEM((N_META, TQ), BF16),
            pltpu.VMEM((nq, TQ, TQ), BF16),
            pltpu.VMEM((8, TQ), F32),
            pltpu.VMEM((1, H_A * TQ), F32),
            pltpu.VMEM((2, 1, H_A * TQ), F32),
            pltpu.VMEM((128, H_A * TQ), F32),
            pltpu.VMEM((2, TQ, H_A * TQ), BF16),
            pltpu.VMEM((2, TQ, H_A * TQ), F32),
            pltpu.VMEM((1, TQ), F32),
        ],
        compiler_params=pltpu.CompilerParams(
            dimension_semantics=("arbitrary", "arbitrary"), vmem_limit_bytes=VMEM_LIMIT),
        name="dsa",
    )(qa, qi, wt, kab, kib, vt, mka, mki, mva)


def _lambda(lq1_ref, lk1_ref, lq2_ref, lk2_ref):
    s1 = jnp.sum(lq1_ref[...] * lk1_ref[...], axis=-1, keepdims=True)
    s2 = jnp.sum(lq2_ref[...] * lk2_ref[...], axis=-1, keepdims=True)
    return jnp.exp(s1) - jnp.exp(s2) + LAM_INIT


def _diff_body(qb_ref, kb_ref, vt_ref, mkb_ref, mvb_ref, lq1_ref, lk1_ref, lq2_ref, lk2_ref,
               sg_ref, o_ref, m_ref, al_ref, acc_ref, p_ref, s_ref):
    i = pl.program_id(1)
    n_hc = 2 * H_B
    krow = lax.broadcasted_iota(I32, (TQ, TQ), 0)
    qcol = lax.broadcasted_iota(I32, (TQ, TQ), 1)
    m_ref[...] = jnp.full(m_ref.shape, NEG_INIT, F32)
    acc_ref[...] = jnp.zeros(acc_ref.shape, F32)

    def main_rows(j):
        return pl.ds(pl.multiple_of(j * TQ, TQ), TQ)

    def logits(k_of, slot, nk):
        for hc in range(n_hc):
            s_ref[slot, hc, 0:nk, :] = _dot_nt(k_of(hc), qb_ref[hc])

    def softmax(slot, nk, negm, future):
        for hc in range(n_hc):
            cols = slice(hc * TQ, (hc + 1) * TQ)
            s = s_ref[slot, hc, 0:nk, :]
            if negm is not None:
                s = s + negm
            if future is not None:
                s = s - (2.0 * LOG2E * SLOPES_B[hc // 2]) * future
            m_old = m_ref[:, cols]
            m_new = jnp.maximum(m_old, jnp.max(s, axis=0, keepdims=True))
            p_ref[slot, 0:nk, cols] = jnp.exp2(s - m_new).astype(BF16)
            al_ref[slot, :, cols] = jnp.exp2(m_old - m_new)
            m_ref[:, cols] = m_new

    def accumulate(vt_of, slot, nk):
        for h in range(H_B):
            cols = slice(2 * h * TQ, (2 * h + 2) * TQ)
            acc_ref[h] = acc_ref[h] * al_ref[slot, :, cols] + _dot(vt_of(h), p_ref[slot, 0:nk, cols])

    eye = (lax.broadcasted_iota(I32, (DV_B, DV_B), 0)
           == lax.broadcasted_iota(I32, (DV_B, DV_B), 1)).astype(BF16)

    def meta_vt(h):
        v_t = _dot_nt(eye, mvb_ref[:, h * DV_B:(h + 1) * DV_B]).astype(BF16)
        return jnp.concatenate([v_t, jnp.ones((VT_ROWS - DV_B, N_META), BF16)], axis=0)

    logits(lambda hc: mkb_ref[hc], 0, N_META)
    softmax(0, N_META, None, None)
    accumulate(meta_vt, 0, N_META)

    p_ref[1] = jnp.zeros(p_ref.shape[1:], BF16)
    al_ref[1] = jnp.ones(al_ref.shape[1:], F32)
    logits(lambda hc: kb_ref[hc, main_rows(0), :], 0, TQ)

    def main_vt(t):
        return lambda h: vt_ref[t, h]

    def full_step(t, slot):
        logits(lambda hc: kb_ref[hc, main_rows(t + 1), :], 1 - slot, TQ)
        accumulate(main_vt(jnp.maximum(t - 1, 0)), 1 - slot, TQ)
        softmax(slot, TQ, None, None)

    def two_steps(u, carry):
        full_step(2 * u, 0)
        full_step(2 * u + 1, 1)
        return carry

    lax.fori_loop(0, i // 2, two_steps, 0)

    def last_step(slot):
        accumulate(main_vt(jnp.maximum(i - 1, 0)), 1 - slot, TQ)
        softmax(slot, TQ, jnp.where((krow // CHUNK) <= (qcol // CHUNK), 0.0, -jnp.inf),
                jnp.maximum(krow - qcol, 0).astype(F32))
        accumulate(main_vt(i), slot, TQ)

    @pl.when(i % 2 == 1)
    def _odd():
        full_step(i - 1, 0)
        last_step(1)

    @pl.when(i % 2 == 0)
    def _even():
        last_step(0)

    lam = _lambda(lq1_ref, lk1_ref, lq2_ref, lk2_ref)
    for h in range(H_B):
        a = acc_ref[h]
        o0 = a[0:DV_B, 0:TQ] / a[DV_B:DV_B + 1, 0:TQ]
        o1 = a[0:DV_B, TQ:2 * TQ] / a[DV_B:DV_B + 1, TQ:2 * TQ]
        o = _rms((o0 - lam * o1).T, sg_ref[...]) * (1.0 - LAM_INIT)
        o_ref[:, h * DV_B:(h + 1) * DV_B] = o.astype(BF16)


def _diff(qb, kbh, vbt, mkb, mvb, lq1, lk1, lq2, lk2, sg, batch, seq):
    nq = seq // TQ
    qrow = lambda b, i: (b * nq + i, 0)
    qhead = lambda b, i: (0, b * nq + i, 0)
    const2 = lambda b, i: (0, 0)
    const3 = lambda b, i: (0, 0, 0)
    return pl.pallas_call(
        _diff_body,
        out_shape=jax.ShapeDtypeStruct((batch * seq, H_B * DV_B), BF16),
        grid=(batch, nq),
        in_specs=[
            pl.BlockSpec((2 * H_B, TQ, 128), qhead),
            pl.BlockSpec((2 * H_B, seq, 128), lambda b, i: (0, b, 0)),
            pl.BlockSpec((nq, H_B, VT_ROWS, TQ), lambda b, i: (b, 0, 0, 0)),
            pl.BlockSpec((2 * H_B, N_META, 128), const3),
            pl.BlockSpec((N_META, H_B * DV_B), const2),
            pl.BlockSpec((1, DH_B), const2),
            pl.BlockSpec((1, DH_B), const2),
            pl.BlockSpec((1, DH_B), const2),
            pl.BlockSpec((1, DH_B), const2),
            pl.BlockSpec((1, DV_B), const2),
        ],
        out_specs=pl.BlockSpec((TQ, H_B * DV_B), qrow),
        scratch_shapes=[
            pltpu.VMEM((1, 2 * H_B * TQ), F32),
            pltpu.VMEM((2, 1, 2 * H_B * TQ), F32),
            pltpu.VMEM((H_B, VT_ROWS, 2 * TQ), F32),
            pltpu.VMEM((2, TQ, 2 * H_B * TQ), BF16),
            pltpu.VMEM((2, 2 * H_B, TQ, TQ), F32),
        ],
        compiler_params=pltpu.CompilerParams(
            dimension_semantics=("arbitrary", "arbitrary"), vmem_limit_bytes=VMEM_LIMIT),
        name="diff",
    )(qb, kbh, vbt, mkb, mvb, lq1, lk1, lq2, lk2, sg)


def _sample_body(qa_ref, qi_ref, sm_ref, nka_ref, nva_ref, nki_ref, qb_ref, nkb_ref, nvb_ref,
                 cak_ref, cav_ref, cai_ref, cbk_ref, cbv_ref,
                 lq1_ref, lk1_ref, lq2_ref, lk2_ref, sg_ref,
                 oa_ref, ob_ref, pc_ref, pn_ref, *, k_top, past, ts):
    pad = LANES - ts
    row_c = lax.broadcasted_iota(I32, (ts, past), 0)
    col_c = lax.broadcasted_iota(I32, (ts, past), 1)
    row_n = lax.broadcasted_iota(I32, (ts, LANES), 0)
    col_n = lax.broadcasted_iota(I32, (ts, LANES), 1)
    new_ok = col_n < ts

    def pad_rows(x):
        return jnp.concatenate([x, jnp.zeros((pad,) + x.shape[1:], x.dtype)], axis=0)

    qi_all = qi_ref[...].reshape(H_I * ts, D_IDX)
    w = sm_ref[:, SM_WI:SM_WI + H_I]
    s4c = _dot(qi_all, cai_ref[...].astype(BF16))
    s4n = _dot_nt(qi_all, pad_rows(nki_ref[...]))
    sc_c = None
    sc_n = None
    for h in range(H_I):
        wh = w[:, h:h + 1]
        tc = jnp.maximum(s4c[h * ts:(h + 1) * ts], 0.0) * wh
        tn = jnp.maximum(s4n[h * ts:(h + 1) * ts], 0.0) * wh
        sc_c = tc if sc_c is None else sc_c + tc
        sc_n = tn if sc_n is None else sc_n + tn
    sc_n = jnp.where(new_ok, sc_n, -jnp.inf)

    def count(pred_c, pred_n):
        c = jnp.sum(jnp.where(pred_c, 1.0, 0.0), axis=1, keepdims=True)
        return c + jnp.sum(jnp.where(pred_n, 1.0, 0.0), axis=1, keepdims=True)

    def threshold(key):
        return _f32_from_key(jnp.clip(key, KEY32_LOWEST, KEY32_HIGHEST))

    def value_step(s, key):
        cand_key = key ^ jnp.left_shift(jnp.int32(1), 31 - s)
        cand = threshold(cand_key)
        cnt = count(sc_c >= cand, sc_n >= cand)
        return jnp.where(cnt >= float(k_top), cand_key, key)

    t = threshold(lax.fori_loop(0, 32, value_step, jnp.full((ts, 1), INT_MIN, I32)))
    need = float(k_top) - count(sc_c > t, sc_n > t)
    idx_n = col_n + past

    def tie_step(s, x):
        cand = x | jnp.left_shift(jnp.int32(1), 11 - s)
        cnt = count((sc_c == t) & (col_c < cand), (sc_n == t) & (idx_n < cand))
        return jnp.where(cnt < need, cand, x)

    x = lax.fori_loop(0, 12, tie_step, jnp.zeros((ts, 1), I32))
    negm_c = jnp.where((sc_c > t) | ((sc_c == t) & (col_c <= x)), 0.0, -jnp.inf)
    negm_n = jnp.where((sc_n > t) | ((sc_n == t) & (idx_n <= x)), 0.0, -jnp.inf)

    dist_c = (row_c - col_c + past).astype(F32)
    dist_n = jnp.abs(row_n - col_n).astype(F32)
    q_all = qa_ref[...][:, :, 0:DH_A].reshape(H_A * ts, DH_A)
    s_c = _dot(q_all, cak_ref[...].astype(BF16))
    s_n = _dot_nt(q_all, pad_rows(nka_ref[...][:, 0:DH_A]))
    for h in range(H_A):
        rows = slice(h * ts, (h + 1) * ts)
        lc = s_c[rows] + (negm_c - (LOG2E * SLOPES_A[h]) * dist_c)
        ln = s_n[rows] + (negm_n - (LOG2E * SLOPES_A[h]) * dist_n)
        m = jnp.maximum(jnp.max(lc, axis=1, keepdims=True), jnp.max(ln, axis=1, keepdims=True))
        m = jnp.maximum(m, NEG_INIT)
        pc_ref[rows, :] = jnp.exp2(lc - m).astype(BF16)
        pn_ref[rows, :] = jnp.exp2(ln - m).astype(BF16)
    vc_aug_t = jnp.concatenate([cav_ref[...].astype(BF16), jnp.ones((64, past), BF16)], axis=0)
    pv = _dot_nt(pc_ref[...], vc_aug_t) + _dot(pn_ref[...], pad_rows(nva_ref[...]))
    pv = pv / pltpu.roll(pv, DH_A, axis=1)
    oa_ref[...] = jnp.concatenate(
        [pv[h * ts:(h + 1) * ts, :DH_A] for h in range(H_A)], axis=1).astype(BF16)

    lam = _lambda(lq1_ref, lk1_ref, lq2_ref, lk2_ref)
    neg_new = jnp.where(new_ok, 0.0, -jnp.inf)
    for h in range(H_B):
        a_c = None
        a_n = None
        for c in range(2):
            hc = 2 * h + c
            q = qb_ref[hc][:, 0:DH_B]
            kc_t = cbk_ref[hc * DH_B:(hc + 1) * DH_B, :].astype(BF16)
            lc = _dot(q, kc_t) - (LOG2E * SLOPES_B[h]) * dist_c
            ln = _dot_nt(q, pad_rows(nkb_ref[hc][:, 0:DH_B])) + (neg_new - (LOG2E * SLOPES_B[h]) * dist_n)
            m = jnp.maximum(jnp.max(lc, axis=1, keepdims=True), jnp.max(ln, axis=1, keepdims=True))
            pc = jnp.exp2(lc - m)
            pn = jnp.exp2(ln - m)
            l = jnp.sum(pc, axis=1, keepdims=True) + jnp.sum(pn, axis=1, keepdims=True)
            pc = pc / l
            pn = pn / l
            if c == 0:
                a_c, a_n = pc, pn
            else:
                a_c, a_n = a_c - lam * pc, a_n - lam * pn
        vc = cbv_ref[pl.ds(h, past, stride=H_B), :].astype(BF16)
        vn = pad_rows(nvb_ref[:, h * DV_B:(h + 1) * DV_B])
        o = _dot(a_c.astype(BF16), vc) + _dot(a_n.astype(BF16), vn)
        o = _rms(o, sg_ref[...]) * (1.0 - LAM_INIT)
        ob_ref[:, h * DV_B:(h + 1) * DV_B] = o.astype(BF16)


def _sample(qa, qi, sm, nka, nva, nki, qb, nkb, nvb, cak, cav, cai, cbk, cbv,
            lq1, lk1, lq2, lk2, sg, batch, ts, past, k_top):
    row = lambda b: (b, 0)
    head = lambda b: (0, b, 0)
    cache = lambda b: (b, 0, 0)
    const = lambda b: (0, 0)
    return pl.pallas_call(
        functools.partial(_sample_body, k_top=k_top, past=past, ts=ts),
        out_shape=(jax.ShapeDtypeStruct((batch * ts, H_A * DH_A), BF16),
                   jax.ShapeDtypeStruct((batch * ts, H_B * DV_B), BF16)),
        grid=(batch,),
        in_specs=[
            pl.BlockSpec((H_A, ts, 128), head),
            pl.BlockSpec((H_I, ts, D_IDX), head),
            pl.BlockSpec((ts, 256), row),
            pl.BlockSpec((ts, 128), row),
            pl.BlockSpec((ts, 128), row),
            pl.BlockSpec((ts, D_IDX), row),
            pl.BlockSpec((2 * H_B, ts, 128), head),
            pl.BlockSpec((2 * H_B, ts, 128), head),
            pl.BlockSpec((ts, H_B * DV_B), row),
            pl.BlockSpec((None, DH_A, past), cache),
            pl.BlockSpec((None, DH_A, past), cache),
            pl.BlockSpec((None, D_IDX, past), cache),
            pl.BlockSpec((None, 2 * H_B * DH_B, past), cache),
            pl.BlockSpec((None, past * H_B, DV_B), cache),
            pl.BlockSpec((1, DH_B), const),
            pl.BlockSpec((1, DH_B), const),
            pl.BlockSpec((1, DH_B), const),
            pl.BlockSpec((1, DH_B), const),
            pl.BlockSpec((1, DV_B), const),
        ],
        out_specs=(pl.BlockSpec((ts, H_A * DH_A), row), pl.BlockSpec((ts, H_B * DV_B), row)),
        scratch_shapes=[
            pltpu.VMEM((H_A * ts, past), BF16),
            pltpu.VMEM((H_A * ts, LANES), BF16),
        ],
        compiler_params=pltpu.CompilerParams(
            dimension_semantics=("arbitrary",), vmem_limit_bytes=VMEM_LIMIT),
        name="sample",
    )(qa, qi, sm, nka, nva, nki, qb, nkb, nvb, cak, cav, cai, cbk, cbv, lq1, lk1, lq2, lk2, sg)


def _finish_body(x_ref, oa_ref, ob_ref, wo_ref, gm_ref, wu_ref, wd_ref, gf_ref, y_ref):
    o = jnp.concatenate([oa_ref[...], ob_ref[...]], axis=1)
    h1 = x_ref[...] + _dot(o, wo_ref[...])
    hn = _rms(h1, gm_ref[...]).astype(BF16)
    acc = h1
    for c in range(D_FF // D_MODEL):
        u = jnp.maximum(_dot(hn, wu_ref[:, c * D_MODEL:(c + 1) * D_MODEL]), 0.0)
        acc = acc + _dot((u * u).astype(BF16), wd_ref[c * D_MODEL:(c + 1) * D_MODEL, :])
    y_ref[...] = _rms(acc, gf_ref[...])


def _finish(x, oa, ob, wo, gm, wu, wd, gf, tm):
    m = x.shape[0]
    assert m % tm == 0
    row = lambda i: (i, 0)
    const = lambda i: (0, 0)
    resident = dict(pipeline_mode=pl.Buffered(1))
    return pl.pallas_call(
        _finish_body,
        out_shape=jax.ShapeDtypeStruct((m, D_MODEL), F32),
        grid=(m // tm,),
        in_specs=[
            pl.BlockSpec((tm, D_MODEL), row),
            pl.BlockSpec((tm, 512), row),
            pl.BlockSpec((tm, 512), row),
            pl.BlockSpec((D_MODEL, D_MODEL), const, **resident),
            pl.BlockSpec((1, D_MODEL), const),
            pl.BlockSpec((D_MODEL, D_FF), const, **resident),
            pl.BlockSpec((D_FF, D_MODEL), const, **resident),
            pl.BlockSpec((1, D_MODEL), const),
        ],
        out_specs=pl.BlockSpec((tm, D_MODEL), row),
        compiler_params=pltpu.CompilerParams(
            dimension_semantics=("arbitrary",), vmem_limit_bytes=VMEM_LIMIT),
        name="finish",
    )(x, oa, ob, wo, gm, wu, wd, gf)


def kernel(x_prompt, x_sample, cache_a_k, cache_a_v, cache_a_idx_k, cache_b_k, cache_b_v,
           meta_tokens, attn_norm_g, w_in, idx_k_norm_g, idx_k_norm_b,
           lambda_q1, lambda_k1, lambda_q2, lambda_k2, subln_g, w_o,
           mlp_norm_g, w_up, w_down, final_norm_g):
    batch, seq, _ = x_prompt.shape
    dec_batch, ts, _ = x_sample.shape
    past = cache_a_k.shape[2]
    assert attn_norm_g.shape[0] == 1, "single-layer step"
    assert seq % TQ == 0 and ts == 16 and past % LANES == 0 and past // CHUNK == (past + ts - 1) // CHUNK
    n = N_META + seq
    k_top_p = min(TOPK_MAX, seq // 4)
    k_top_s = min(TOPK_MAX, (past + ts) // 4)

    w_t = w_in[0].T.astype(BF16)
    w = jnp.concatenate(
        [w_t[0:512],
         w_t[964:1476],
         w_t[1476:1988],
         w_t[1988:2500],
         w_t[640:896],
         w_t[512:640],
         w_t[896:964]],
        axis=0)
    w = jnp.pad(w, ((0, W_COLS - 2500), (0, 0))).T
    wvt = w_t[1476:2500]
    wo = w_o[0].astype(BF16)
    wu = w_up[0].astype(BF16)
    wd = w_down[0].astype(BF16)
    g_attn = attn_norm_g[0][None]
    g_mlp = mlp_norm_g[0][None]
    g_fin = final_norm_g[None]
    kng = idx_k_norm_g[0][None]
    knb = idx_k_norm_b[0][None]
    lq1, lk1, lq2, lk2 = lambda_q1[0][None], lambda_k1[0][None], lambda_q2[0][None], lambda_k2[0][None]
    sg = subln_g[0][None]

    xp = x_prompt.reshape(batch * seq, D_MODEL)
    xs = x_sample.reshape(dec_batch * ts, D_MODEL)

    (qa_p, qi_p, qb_p, kbh_p, vb_p, kab_p, kib_p, vt_p, wt_p, vbt_p, smt_p, kbt_p) = _proj(
        xp, g_attn, w, wvt, kng, knb, 512, N_META, seq, True)
    (_, _, _, kbh_m, vb_m, kab_m, kib_m, kb_m, vbb_m, sm_m, vaa_m) = _proj(
        meta_tokens, g_attn, w, wvt, kng, knb, N_META, 0, N_META, False)
    (qa_s, qi_s, qb_s, kbh_s, vb_s, kab_s, kib_s, kb_s, vbb_s, sm_s, vaa_s) = _proj(
        xs, g_attn, w, wvt, kng, knb, dec_batch * ts, past, dec_batch * ts, False)

    oa_p = _dsa(qa_p, qi_p, wt_p, kab_p, kib_p, vt_p, kab_m, kib_m, vaa_m, batch, seq, k_top_p)
    ob_p = _diff(qb_p, kbh_p, vbt_p, kbh_m, vbb_m, lq1, lk1, lq2, lk2, sg, batch, seq)
    y_prompt = _finish(xp, oa_p, ob_p, wo, g_mlp, wu, wd, g_fin, 512).reshape(batch, seq, D_MODEL)

    oa_s, ob_s = _sample(
        qa_s, qi_s, sm_s, kab_s, vaa_s, kib_s, qb_s, kbh_s, vbb_s,
        jnp.swapaxes(cache_a_k[0], 1, 2), jnp.swapaxes(cache_a_v[0], 1, 2),
        jnp.swapaxes(cache_a_idx_k[0], 1, 2),
        jnp.transpose(cache_b_k[0], (0, 2, 3, 4, 1)).reshape(dec_batch, 2 * H_B * DH_B, past),
        cache_b_v[0].reshape(dec_batch, past * H_B, DV_B),
        lq1, lk1, lq2, lk2, sg, dec_batch, ts, past, k_top_s)
    y_sample = _finish(xs, oa_s, ob_s, wo, g_mlp, wu, wd, g_fin, dec_batch * ts).reshape(
        dec_batch, ts, D_MODEL)

    def with_meta_t(meta_rows, main_t):
        meta_b = jnp.broadcast_to(meta_rows.T[None], (batch, 64, N_META))
        main = jnp.swapaxes(main_t.reshape(64, batch, seq), 0, 1)
        return jnp.swapaxes(jnp.concatenate([meta_b, main], axis=2), 1, 2)[None]

    new_a_k_p = with_meta_t(sm_m[:, SM_KA:SM_KA + 64], smt_p[SM_KA:SM_KA + 64])
    new_a_v_p = with_meta_t(sm_m[:, SM_VA:SM_VA + 64], smt_p[SM_VA:SM_VA + 64])
    new_a_i_p = with_meta_t(sm_m[:, SM_KI:SM_KI + 64], smt_p[SM_KI:SM_KI + 64])
    kb_meta_t = jnp.broadcast_to(kb_m.T.reshape(1, H_B, 2, DH_B, N_META), (batch, H_B, 2, DH_B, N_META))
    kb_main_t = jnp.transpose(kbt_p.reshape(H_B, 2, DH_B, batch, seq), (3, 0, 1, 2, 4))
    new_b_k_p = jnp.transpose(
        jnp.concatenate([kb_meta_t, kb_main_t], axis=4), (0, 4, 1, 2, 3))[None]
    new_b_v_p = jnp.concatenate(
        [jnp.broadcast_to(vb_m.reshape(1, N_META, H_B, DV_B), (batch, N_META, H_B, DV_B)),
         vb_p.reshape(batch, seq, H_B, DV_B)], axis=1)[None]
    new_a_k_s = sm_s[:, SM_KA:SM_KA + 64].reshape(1, dec_batch, ts, 64)
    new_a_v_s = sm_s[:, SM_VA:SM_VA + 64].reshape(1, dec_batch, ts, 64)
    new_a_i_s = sm_s[:, SM_KI:SM_KI + 64].reshape(1, dec_batch, ts, 64)
    new_b_k_s = kb_s.reshape(1, dec_batch, ts, H_B, 2, DH_B)
    new_b_v_s = vb_s.reshape(1, dec_batch, ts, H_B, DV_B)
    return (y_prompt, y_sample, new_a_k_p, new_a_v_p, new_a_i_p, new_b_k_p, new_b_v_p,
            new_a_k_s, new_a_v_s, new_a_i_s, new_b_k_s, new_b_v_s)
```

```python
import functools

import jax
import jax.numpy as jnp
import numpy as np
from jax import lax
from jax.experimental import pallas as pl
from jax.experimental.pallas import tpu as pltpu

F32 = jnp.float32
BF16 = jnp.bfloat16
I32 = jnp.int32
I16 = jnp.int16
HALF16 = 32768

D_MODEL = 1024
CHUNK = 64
N_META = 16
H_A = 8
DH_A = 64
H_I = 4
D_IDX = 64
TOPK_MAX = 256
H_B = 4
DH_B = 64
DV_B = 2 * DH_B
D_FF = 4 * D_MODEL
EPS = 1e-6
LAM_INIT = 0.2

LANES = 128
C_QA, C_QB, C_KB, C_VB, C_QI, C_SM = 0, 512, 1024, 1536, 2048, 2304
W_COLS = 2560
SM_KA, SM_VA, SM_KI, SM_WI = 0, 64, 128, 192

LOG2E = float(np.float32(1.4426950408889634))
Q_SCALE = (DH_A ** -0.5) * LOG2E
N_SLOPE_PIECES = 3


def _bf16_pieces(value):
    rest = np.float32(value)
    pieces = []
    for _ in range(N_SLOPE_PIECES):
        piece = np.float32(rest.astype(jnp.bfloat16))
        pieces.append(float(piece))
        rest = np.float32(rest - piece)
    assert rest == 0.0
    return pieces
W_SCALE = (H_I ** -0.5) * (D_IDX ** -0.5)
SLOPES_A = tuple(2.0 ** (-8.0 * (i + 1) / H_A) for i in range(H_A))
SLOPES_B = tuple(2.0 ** (-8.0 * (i + 1) / H_B) for i in range(H_B))

INT_MIN = -(2 ** 31)
NEG_INIT = -1e30
TQ = 256
ONES_ROWS = 16
VT_ROWS = DV_B + ONES_ROWS
VTA_ROWS = DH_A + ONES_ROWS
CNT_ROWS = 32
VMEM_LIMIT = 56 * 1024 * 1024

_NT = (((1,), (1,)), ((), ()))


def _dot(a, b):
    return jnp.dot(a, b, preferred_element_type=F32)


def _dot_nt(a, b):
    return lax.dot_general(a, b, _NT, preferred_element_type=F32)


def _rms(x, g):
    return (x * lax.rsqrt(jnp.mean(x * x, axis=-1, keepdims=True) + EPS)) * g


def _to_key(score):
    b = lax.bitcast_convert_type(score, I32)
    return b ^ ((b >> 31) & jnp.int32(0x7FFFFFFF))


def _f32_from_key(key):
    return lax.bitcast_convert_type(key ^ ((key >> 31) & jnp.int32(0x7FFFFFFF)), F32)


def _bf16_from_key(key):
    bits = key ^ ((key >> 15) & jnp.int32(0x7FFF))
    return lax.bitcast_convert_type(jnp.left_shift(bits, 16), F32).astype(BF16)


KEY16_LOWEST = -32640
KEY32_LOWEST = -2139095040
KEY32_HIGHEST = 2139095039


def _wide(x, width):
    reps = width // LANES
    return x if reps == 1 else jnp.concatenate([x] * reps, axis=1)


def _proj_body(x_ref, g_ref, w_ref, wvt_ref, kng_ref, knb_ref,
               qa_ref, qi_ref, qb_ref, kbh_ref, vb_ref, kab_ref, kib_ref, *mode_refs,
               pos0, period, key_major):
    x = x_ref[...]
    hn = _rms(x, g_ref[...]).astype(BF16)
    rows = x.shape[0]

    def mm(lo, width):
        return _dot(hn, w_ref[:, lo:lo + width])

    lane = lax.broadcasted_iota(I32, (rows, 64), 1)
    r = pl.program_id(0) * rows + lax.broadcasted_iota(I32, (rows, 64), 0)
    pos = pos0 + r % period
    pos_cols = jnp.where(lane >= 2 * N_SLOPE_PIECES, 0,
                         jnp.where(lane % 2 == 0, (pos // 256) * 256, pos % 256))
    pos_cols = pos_cols.astype(F32).astype(BF16)

    def slope_cols(slope):
        cols = jnp.zeros((rows, 64), F32)
        for n, piece in enumerate(_bf16_pieces(slope * LOG2E)):
            cols = jnp.where(lane // 2 == n, piece, cols)
        return cols.astype(BF16)

    z = mm(C_QA, 512) * Q_SCALE
    for h in range(H_A):
        qa_ref[h] = jnp.concatenate(
            [z[:, h * DH_A:(h + 1) * DH_A].astype(BF16), slope_cols(SLOPES_A[h])], axis=1)
    z = mm(C_QB, 512) * Q_SCALE
    for hc in range(2 * H_B):
        qb_ref[hc] = jnp.concatenate(
            [z[:, hc * DH_B:(hc + 1) * DH_B].astype(BF16), slope_cols(SLOPES_B[hc // 2])], axis=1)
    if key_major:
        vt_ref, wt_ref, vbt_ref, smt_ref, kbt_ref = mode_refs
    else:
        kb_ref, vbb_ref, sm_ref, vaa_ref = mode_refs
    z = mm(C_KB, 512)
    if not key_major:
        kb_ref[...] = z
    for hc in range(2 * H_B):
        kbh_ref[hc] = jnp.concatenate([z[:, hc * DH_B:(hc + 1) * DH_B].astype(BF16), pos_cols], axis=1)
    z = mm(C_VB, 512)
    for h in range(H_B):
        vb_ref[pl.ds(h, rows, stride=H_B), :] = z[:, h * DV_B:(h + 1) * DV_B]
    if not key_major:
        vbb_ref[...] = z.astype(BF16)
    z = mm(C_QI, 256)
    for h in range(H_I):
        qi_ref[h] = z[:, h * D_IDX:(h + 1) * D_IDX].astype(BF16)
    z = mm(C_SM, 256)
    ka = z[:, SM_KA:SM_KA + 64]
    va = z[:, SM_VA:SM_VA + 64]
    ki = z[:, SM_KI:SM_KI + 64]
    xc = ki - jnp.mean(ki, axis=-1, keepdims=True)
    ki = xc * lax.rsqrt(jnp.mean(xc * xc, axis=-1, keepdims=True) + EPS) * kng_ref[...] + knb_ref[...]
    kab_ref[...] = jnp.concatenate([ka.astype(BF16), pos_cols], axis=1)
    kib_ref[...] = ki.astype(BF16)
    if not key_major:
        sm_ref[:, 0:128] = z[:, 0:128]
        sm_ref[:, SM_KI:SM_KI + 64] = ki
        sm_ref[:, SM_WI:SM_WI + 64] = z[:, SM_WI:SM_WI + 64] * W_SCALE
        vaa_ref[...] = jnp.concatenate([va.astype(BF16), jnp.ones((rows, 64), BF16)], axis=1)
    else:
        kv_t = z[:, 0:128].T
        for c in range(rows // TQ):
            vt_ref[c] = jnp.concatenate(
                [kv_t[64:128, c * TQ:(c + 1) * TQ].astype(BF16), jnp.ones((ONES_ROWS, TQ), BF16)], axis=0)
        kw_t = jnp.concatenate([ki, z[:, SM_WI:SM_WI + 64] * W_SCALE], axis=1).T
        wt_ref[...] = kw_t[64:72, :]
        smt_ref[0:128, :] = kv_t
        smt_ref[128:192, :] = kw_t[0:64, :]
        kbt_ref[...] = _dot_nt(wvt_ref[0:512, :], hn)
        vb_t = _dot_nt(wvt_ref[512:1024, :], hn)
        for c in range(rows // TQ):
            for h in range(H_B):
                vbt_ref[c, h] = jnp.concatenate(
                    [vb_t[h * DV_B:(h + 1) * DV_B, c * TQ:(c + 1) * TQ].astype(BF16),
                     jnp.ones((VT_ROWS - DV_B, TQ), BF16)], axis=0)


def _proj(x, g, w, wvt, kng, knb, tm, pos0, period, key_major):
    m = x.shape[0]
    assert m % tm == 0
    row = lambda i: (i, 0)
    head = lambda i: (0, i, 0)
    const = lambda i: (0, 0)
    out_shape = (
        jax.ShapeDtypeStruct((H_A, m, 128), BF16),
        jax.ShapeDtypeStruct((H_I, m, D_IDX), BF16),
        jax.ShapeDtypeStruct((2 * H_B, m, 128), BF16),
        jax.ShapeDtypeStruct((2 * H_B, m, 128), BF16),
        jax.ShapeDtypeStruct((m * H_B, DV_B), F32),
        jax.ShapeDtypeStruct((m, 128), BF16),
        jax.ShapeDtypeStruct((m, 64), BF16),
    )
    out_specs = (
        pl.BlockSpec((H_A, tm, 128), head),
        pl.BlockSpec((H_I, tm, D_IDX), head),
        pl.BlockSpec((2 * H_B, tm, 128), head),
        pl.BlockSpec((2 * H_B, tm, 128), head),
        pl.BlockSpec((tm * H_B, DV_B), row),
        pl.BlockSpec((tm, 128), row),
        pl.BlockSpec((tm, 64), row),
    )
    if not key_major:
        out_shape += (
            jax.ShapeDtypeStruct((m, 512), F32),
            jax.ShapeDtypeStruct((m, 512), BF16),
            jax.ShapeDtypeStruct((m, 256), F32),
            jax.ShapeDtypeStruct((m, 128), BF16),
        )
        out_specs += (
            pl.BlockSpec((tm, 512), row),
            pl.BlockSpec((tm, 512), row),
            pl.BlockSpec((tm, 256), row),
            pl.BlockSpec((tm, 128), row),
        )
    else:
        assert tm % TQ == 0 and period % tm == 0 and m % period == 0
        per_stream = period // tm
        stream_cols = lambda i: (i // per_stream, 0, i % per_stream)
        out_shape += (
            jax.ShapeDtypeStruct((m // TQ, VTA_ROWS, TQ), BF16),
            jax.ShapeDtypeStruct((8, m), F32),
            jax.ShapeDtypeStruct((m // TQ, H_B, VT_ROWS, TQ), BF16),
            jax.ShapeDtypeStruct((m // period, 192, period), F32),
            jax.ShapeDtypeStruct((m // period, 512, period), F32),
        )
        out_specs += (
            pl.BlockSpec((tm // TQ, VTA_ROWS, TQ), lambda i: (i, 0, 0)),
            pl.BlockSpec((8, tm), lambda i: (0, i)),
            pl.BlockSpec((tm // TQ, H_B, VT_ROWS, TQ), lambda i: (i, 0, 0, 0)),
            pl.BlockSpec((None, 192, tm), stream_cols),
            pl.BlockSpec((None, 512, tm), stream_cols),
        )
    return pl.pallas_call(
        functools.partial(_proj_body, pos0=pos0, period=period, key_major=key_major),
        out_shape=out_shape,
        grid=(m // tm,),
        in_specs=[
            pl.BlockSpec((tm, D_MODEL), row),
            pl.BlockSpec((1, D_MODEL), const),
            pl.BlockSpec((D_MODEL, W_COLS), const),
            pl.BlockSpec((2 * H_B * DV_B, D_MODEL), const),
            pl.BlockSpec((1, D_IDX), const),
            pl.BlockSpec((1, D_IDX), const),
        ],
        out_specs=out_specs,
        compiler_params=pltpu.CompilerParams(
            dimension_semantics=("arbitrary",), vmem_limit_bytes=VMEM_LIMIT),
        name="proj",
    )(x, g, w, wvt, kng, knb)


def _dsa_body(qa_ref, qi_ref, wt_ref, ka_ref, ki_ref, vt_ref, mka_ref, mki_ref, mva_ref,
              o_ref, smeta_ref, smain_ref, bmeta_ref, bmain_ref,
              tsel_ref, m_ref, al_ref, acc_ref, p_ref, s_ref, seen_ref, *, k_top):
    i = pl.program_id(1)
    krow = lax.broadcasted_iota(I32, (TQ, TQ), 0)
    qcol = lax.broadcasted_iota(I32, (TQ, TQ), 1)

    def main_rows(j):
        return pl.ds(pl.multiple_of(j * TQ, TQ), TQ)

    qi_all = qi_ref[...].reshape(H_I * TQ, D_IDX)

    def scores(ki_blk):
        s4 = _dot_nt(ki_blk, qi_all)
        sc = None
        for h in range(H_I):
            t = jnp.maximum(s4[:, h * TQ:(h + 1) * TQ], 0.0) * wt_ref[h:h + 1, :]
            sc = t if sc is None else sc + t
        return sc

    sc = scores(mki_ref[...])
    smeta_ref[...] = sc
    bmeta_ref[...] = sc.astype(BF16)

    def score_block(j, sc):
        smain_ref[j] = sc
        bmain_ref[j] = sc.astype(BF16)

    def score_full(j, carry):
        score_block(j, scores(ki_ref[main_rows(j), :]))
        return carry

    lax.fori_loop(0, i, score_full, 0)
    score_block(i, jnp.where((krow // CHUNK) <= (qcol // CHUNK),
                             scores(ki_ref[main_rows(i), :]), -jnp.inf))

    def count(meta_ref, main_ref, part, pred):
        def body(j, c):
            return c + part(pred(main_ref[j]))

        c = lax.fori_loop(0, i + 1, body, jnp.zeros((CNT_ROWS, TQ), F32))
        cm = part(pred(meta_ref[...]), rows=N_META)
        return jnp.sum(c, axis=0, keepdims=True) + jnp.sum(cm, axis=0, keepdims=True)

    def part32(hit, rows=CNT_ROWS):
        ones = jnp.where(hit, 1.0, 0.0)
        return jnp.sum(ones.reshape(ones.shape[0] // rows, rows, TQ), axis=0)

    def part16(hit, rows=CNT_ROWS):
        ones = jnp.where(hit, jnp.bfloat16(1), jnp.bfloat16(0))
        acc = ones[0:rows]
        for t in range(1, ones.shape[0] // rows):
            acc = acc + ones[t * rows:(t + 1) * rows]
        return acc.astype(F32)

    kf = float(k_top)

    def step16(s, u):
        cand_u = u | jnp.left_shift(jnp.int32(1), 15 - s)
        cand = _bf16_from_key(jnp.maximum(cand_u, KEY16_LOWEST + HALF16) - HALF16)
        cnt = count(bmeta_ref, bmain_ref, part16, lambda x: x >= cand)
        return jnp.where(cnt >= kf, cand_u, u)

    u_hi = lax.fori_loop(0, 16, step16, jnp.zeros((1, TQ), I32))
    t_hi = _bf16_from_key(jnp.maximum(u_hi, KEY16_LOWEST + HALF16) - HALF16).astype(F32)
    base = jnp.maximum(_to_key(t_hi) - 65536, KEY32_LOWEST)

    def step32(s, d):
        cand_d = d | jnp.left_shift(jnp.int32(1), 16 - s)
        cand = _f32_from_key(jnp.minimum(base + cand_d, KEY32_HIGHEST))
        cnt = count(smeta_ref, smain_ref, part32, lambda x: x >= cand)
        return jnp.where(cnt >= kf, cand_d, d)

    d_lo = lax.fori_loop(0, 17, step32, jnp.zeros((1, TQ), I32))
    t_sel = _f32_from_key(jnp.minimum(base + d_lo, KEY32_HIGHEST))
    need = kf - count(smeta_ref, smain_ref, part32, lambda x: x > t_sel)
    tsel_ref[...] = jnp.broadcast_to(t_sel, tsel_ref.shape)

    q_all = qa_ref[...].reshape(H_A * TQ, 128)
    m_ref[...] = jnp.full(m_ref.shape, NEG_INIT, F32)
    acc_ref[...] = jnp.zeros(acc_ref.shape, F32)
    thr = tsel_ref[0:1, :]

    def logits(k_blk):
        return _dot_nt(k_blk, q_all)

    def softmax(sc, s_all, future, seen, slot):
        nk = sc.shape[0]
        tie = sc == thr
        tri = (lax.broadcasted_iota(I32, (nk, nk), 1)
               <= lax.broadcasted_iota(I32, (nk, nk), 0)).astype(BF16)
        rank = _dot(tri, jnp.where(tie, 1.0, 0.0).astype(BF16)) + seen
        negm = jnp.where(sc > thr, 0.0, jnp.where(tie, jnp.where(rank <= need, 0.0, -jnp.inf), -jnp.inf))
        for h in range(H_A):
            cols = slice(h * TQ, (h + 1) * TQ)
            s = s_all[:, cols] + negm
            if future is not None:
                s = s - (2.0 * LOG2E * SLOPES_A[h]) * future
            m_old = m_ref[:, cols]
            m_new = jnp.maximum(m_old, jnp.max(s, axis=0, keepdims=True))
            p_ref[slot, 0:nk, cols] = jnp.exp2(s - m_new).astype(BF16)
            al_ref[slot, :, cols] = jnp.exp2(m_old - m_new)
            m_ref[:, cols] = m_new
        return rank[nk - 1:nk, :]

    def accumulate(vt_blk, slot, nk):
        acc_ref[...] = acc_ref[...] * al_ref[slot] + _dot(vt_blk, p_ref[slot, 0:nk, :])

    eye = (lax.broadcasted_iota(I32, (128, 128), 0)
           == lax.broadcasted_iota(I32, (128, 128), 1)).astype(BF16)
    seen_ref[...] = softmax(smeta_ref[...], logits(mka_ref[...]), None, jnp.zeros((1, TQ), F32), 0)
    accumulate(_dot_nt(eye, mva_ref[...])[0:VTA_ROWS].astype(BF16), 0, N_META)

    p_ref[1] = jnp.zeros(p_ref.shape[1:], BF16)
    al_ref[1] = jnp.ones(al_ref.shape[1:], F32)
    s_ref[0] = logits(ka_ref[main_rows(0), :])

    def full_step(t, slot):
        s_ref[1 - slot] = logits(ka_ref[main_rows(t + 1), :])
        accumulate(vt_ref[jnp.maximum(t - 1, 0)], 1 - slot, TQ)
        seen_ref[...] = softmax(smain_ref[t], s_ref[slot], None, seen_ref[...], slot)

    def two_steps(u, carry):
        full_step(2 * u, 0)
        full_step(2 * u + 1, 1)
        return carry

    lax.fori_loop(0, i // 2, two_steps, 0)

    def last_step(slot):
        accumulate(vt_ref[jnp.maximum(i - 1, 0)], 1 - slot, TQ)
        softmax(smain_ref[i], s_ref[slot], jnp.maximum(krow - qcol, 0).astype(F32),
                seen_ref[...], slot)
        accumulate(vt_ref[i], slot, TQ)

    @pl.when(i % 2 == 1)
    def _odd():
        full_step(i - 1, 0)
        last_step(1)

    @pl.when(i % 2 == 0)
    def _even():
        last_step(0)

    def head_out_t(h):
        a = acc_ref[:, h * TQ:(h + 1) * TQ]
        return a[0:DH_A] / a[DH_A:DH_A + 1]

    for h in range(0, H_A, 2):
        pair_t = jnp.concatenate([head_out_t(h), head_out_t(h + 1)], axis=0)
        o_ref[:, h * DH_A:(h + 2) * DH_A] = pair_t.T.astype(BF16)


def _dsa(qa, qi, wt, kab, kib, vt, mka, mki, mva, batch, seq, k_top):
    nq = seq // TQ
    qrow = lambda b, i: (b * nq + i, 0)
    qhead = lambda b, i: (0, b * nq + i, 0)
    kv = lambda b, i: (b, 0)
    const = lambda b, i: (0, 0)
    return pl.pallas_call(
        functools.partial(_dsa_body, k_top=k_top),
        out_shape=jax.ShapeDtypeStruct((batch * seq, H_A * DH_A), BF16),
        grid=(batch, nq),
        in_specs=[
            pl.BlockSpec((H_A, TQ, 128), qhead),
            pl.BlockSpec((H_I, TQ, D_IDX), qhead),
            pl.BlockSpec((8, TQ), lambda b, i: (0, b * nq + i)),
            pl.BlockSpec((seq, 128), kv),
            pl.BlockSpec((seq, D_IDX), kv),
            pl.BlockSpec((nq, VTA_ROWS, TQ), lambda b, i: (b, 0, 0)),
            pl.BlockSpec((N_META, 128), const),
            pl.BlockSpec((N_META, D_IDX), const),
            pl.BlockSpec((N_META, 128), const),
        ],
        out_specs=pl.BlockSpec((TQ, H_A * DH_A), qrow),
        scratch_shapes=[
            pltpu.VMEM((N_META, TQ), F32),
            pltpu.VMEM((nq, TQ, TQ), F32),
            pltpu.VMEM((N_META, TQ), BF16),
            pltpu.VMEM((nq, TQ, TQ), BF16),
            pltpu.VMEM((8, TQ), F32),
            pltpu.VMEM((1, H_A * TQ), F32),
            pltpu.VMEM((2, 1, H_A * TQ), F32),
            pltpu.VMEM((VTA_ROWS, H_A * TQ), F32),
            pltpu.VMEM((2, TQ, H_A * TQ), BF16),
            pltpu.VMEM((2, TQ, H_A * TQ), F32),
            pltpu.VMEM((1, TQ), F32),
        ],
        compiler_params=pltpu.CompilerParams(
            dimension_semantics=("arbitrary", "arbitrary"), vmem_limit_bytes=VMEM_LIMIT),
        name="dsa",
    )(qa, qi, wt, kab, kib, vt, mka, mki, mva)


def _lambda(lq1_ref, lk1_ref, lq2_ref, lk2_ref):
    s1 = jnp.sum(lq1_ref[...] * lk1_ref[...], axis=-1, keepdims=True)
    s2 = jnp.sum(lq2_ref[...] * lk2_ref[...], axis=-1, keepdims=True)
    return jnp.exp(s1) - jnp.exp(s2) + LAM_INIT


def _diff_body(qb_ref, kb_ref, vt_ref, mkb_ref, mvb_ref, lq1_ref, lk1_ref, lq2_ref, lk2_ref,
               sg_ref, o_ref, m_ref, al_ref, acc_ref, p_ref, s_ref):
    i = pl.program_id(1)
    n_hc = 2 * H_B
    krow = lax.broadcasted_iota(I32, (TQ, TQ), 0)
    qcol = lax.broadcasted_iota(I32, (TQ, TQ), 1)
    m_ref[...] = jnp.full(m_ref.shape, NEG_INIT, F32)
    acc_ref[...] = jnp.zeros(acc_ref.shape, F32)

    def main_rows(j):
        return pl.ds(pl.multiple_of(j * TQ, TQ), TQ)

    def logits(k_of, slot, nk):
        for hc in range(n_hc):
            s_ref[slot, hc, 0:nk, :] = _dot_nt(k_of(hc), qb_ref[hc])

    def softmax(slot, nk, negm, future):
        for hc in range(n_hc):
            cols = slice(hc * TQ, (hc + 1) * TQ)
            s = s_ref[slot, hc, 0:nk, :]
            if negm is not None:
                s = s + negm
            if future is not None:
                s = s - (2.0 * LOG2E * SLOPES_B[hc // 2]) * future
            m_old = m_ref[:, cols]
            m_new = jnp.maximum(m_old, jnp.max(s, axis=0, keepdims=True))
            p_ref[slot, 0:nk, cols] = jnp.exp2(s - m_new).astype(BF16)
            al_ref[slot, :, cols] = jnp.exp2(m_old - m_new)
            m_ref[:, cols] = m_new

    def accumulate(vt_of, slot, nk):
        for h in range(H_B):
            cols = slice(2 * h * TQ, (2 * h + 2) * TQ)
            acc_ref[h] = acc_ref[h] * al_ref[slot, :, cols] + _dot(vt_of(h), p_ref[slot, 0:nk, cols])

    eye = (lax.broadcasted_iota(I32, (DV_B, DV_B), 0)
           == lax.broadcasted_iota(I32, (DV_B, DV_B), 1)).astype(BF16)

    def meta_vt(h):
        v_t = _dot_nt(eye, mvb_ref[:, h * DV_B:(h + 1) * DV_B]).astype(BF16)
        return jnp.concatenate([v_t, jnp.ones((VT_ROWS - DV_B, N_META), BF16)], axis=0)

    logits(lambda hc: mkb_ref[hc], 0, N_META)
    softmax(0, N_META, None, None)
    accumulate(meta_vt, 0, N_META)

    p_ref[1] = jnp.zeros(p_ref.shape[1:], BF16)
    al_ref[1] = jnp.ones(al_ref.shape[1:], F32)
    logits(lambda hc: kb_ref[hc, main_rows(0), :], 0, TQ)

    def main_vt(t):
        return lambda h: vt_ref[t, h]

    def full_step(t, slot):
        logits(lambda hc: kb_ref[hc, main_rows(t + 1), :], 1 - slot, TQ)
        accumulate(main_vt(jnp.maximum(t - 1, 0)), 1 - slot, TQ)
        softmax(slot, TQ, None, None)

    def two_steps(u, carry):
        full_step(2 * u, 0)
        full_step(2 * u + 1, 1)
        return carry

    lax.fori_loop(0, i // 2, two_steps, 0)

    def last_step(slot):
        accumulate(main_vt(jnp.maximum(i - 1, 0)), 1 - slot, TQ)
        softmax(slot, TQ, jnp.where((krow // CHUNK) <= (qcol // CHUNK), 0.0, -jnp.inf),
                jnp.maximum(krow - qcol, 0).astype(F32))
        accumulate(main_vt(i), slot, TQ)

    @pl.when(i % 2 == 1)
    def _odd():
        full_step(i - 1, 0)
        last_step(1)

    @pl.when(i % 2 == 0)
    def _even():
        last_step(0)

    lam = _lambda(lq1_ref, lk1_ref, lq2_ref, lk2_ref)
    for h in range(H_B):
        a = acc_ref[h]
        o0 = a[0:DV_B, 0:TQ] / a[DV_B:DV_B + 1, 0:TQ]
        o1 = a[0:DV_B, TQ:2 * TQ] / a[DV_B:DV_B + 1, TQ:2 * TQ]
        o = _rms((o0 - lam * o1).T, sg_ref[...]) * (1.0 - LAM_INIT)
        o_ref[:, h * DV_B:(h + 1) * DV_B] = o.astype(BF16)


def _diff(qb, kbh, vbt, mkb, mvb, lq1, lk1, lq2, lk2, sg, batch, seq):
    nq = seq // TQ
    qrow = lambda b, i: (b * nq + i, 0)
    qhead = lambda b, i: (0, b * nq + i, 0)
    const2 = lambda b, i: (0, 0)
    const3 = lambda b, i: (0, 0, 0)
    return pl.pallas_call(
        _diff_body,
        out_shape=jax.ShapeDtypeStruct((batch * seq, H_B * DV_B), BF16),
        grid=(batch, nq),
        in_specs=[
            pl.BlockSpec((2 * H_B, TQ, 128), qhead),
            pl.BlockSpec((2 * H_B, seq, 128), lambda b, i: (0, b, 0)),
            pl.BlockSpec((nq, H_B, VT_ROWS, TQ), lambda b, i: (b, 0, 0, 0)),
            pl.BlockSpec((2 * H_B, N_META, 128), const3),
            pl.BlockSpec((N_META, H_B * DV_B), const2),
            pl.BlockSpec((1, DH_B), const2),
            pl.BlockSpec((1, DH_B), const2),
            pl.BlockSpec((1, DH_B), const2),
            pl.BlockSpec((1, DH_B), const2),
            pl.BlockSpec((1, DV_B), const2),
        ],
        out_specs=pl.BlockSpec((TQ, H_B * DV_B), qrow),
        scratch_shapes=[
            pltpu.VMEM((1, 2 * H_B * TQ), F32),
            pltpu.VMEM((2, 1, 2 * H_B * TQ), F32),
            pltpu.VMEM((H_B, VT_ROWS, 2 * TQ), F32),
            pltpu.VMEM((2, TQ, 2 * H_B * TQ), BF16),
            pltpu.VMEM((2, 2 * H_B, TQ, TQ), F32),
        ],
        compiler_params=pltpu.CompilerParams(
            dimension_semantics=("arbitrary", "arbitrary"), vmem_limit_bytes=VMEM_LIMIT),
        name="diff",
    )(qb, kbh, vbt, mkb, mvb, lq1, lk1, lq2, lk2, sg)


def _sample_body(qa_ref, qi_ref, sm_ref, nka_ref, nva_ref, nki_ref, qb_ref, nkb_ref, nvb_ref,
                 cak_ref, cav_ref, cai_ref, cbk_ref, cbv_ref,
                 lq1_ref, lk1_ref, lq2_ref, lk2_ref, sg_ref,
                 oa_ref, ob_ref, pc_ref, pn_ref, *, k_top, past, ts):
    pad = LANES - ts
    row_c = lax.broadcasted_iota(I32, (ts, past), 0)
    col_c = lax.broadcasted_iota(I32, (ts, past), 1)
    row_n = lax.broadcasted_iota(I32, (ts, LANES), 0)
    col_n = lax.broadcasted_iota(I32, (ts, LANES), 1)
    new_ok = col_n < ts

    def pad_rows(x):
        return jnp.concatenate([x, jnp.zeros((pad,) + x.shape[1:], x.dtype)], axis=0)

    qi_all = qi_ref[...].reshape(H_I * ts, D_IDX)
    w = sm_ref[:, SM_WI:SM_WI + H_I]
    s4c = _dot(qi_all, cai_ref[...].astype(BF16))
    s4n = _dot_nt(qi_all, pad_rows(nki_ref[...]))
    sc_c = None
    sc_n = None
    for h in range(H_I):
        wh = w[:, h:h + 1]
        tc = jnp.maximum(s4c[h * ts:(h + 1) * ts], 0.0) * wh
        tn = jnp.maximum(s4n[h * ts:(h + 1) * ts], 0.0) * wh
        sc_c = tc if sc_c is None else sc_c + tc
        sc_n = tn if sc_n is None else sc_n + tn
    sc_n = jnp.where(new_ok, sc_n, -jnp.inf)

    def count(pred_c, pred_n):
        c = jnp.sum(jnp.where(pred_c, 1.0, 0.0), axis=1, keepdims=True)
        return c + jnp.sum(jnp.where(pred_n, 1.0, 0.0), axis=1, keepdims=True)

    def threshold(key):
        return _f32_from_key(jnp.clip(key, KEY32_LOWEST, KEY32_HIGHEST))

    def value_step(s, key):
        cand_key = key ^ jnp.left_shift(jnp.int32(1), 31 - s)
        cand = threshold(cand_key)
        cnt = count(sc_c >= cand, sc_n >= cand)
        return jnp.where(cnt >= float(k_top), cand_key, key)

    t = threshold(lax.fori_loop(0, 32, value_step, jnp.full((ts, 1), INT_MIN, I32)))
    need = float(k_top) - count(sc_c > t, sc_n > t)
    idx_n = col_n + past

    def tie_step(s, x):
        cand = x | jnp.left_shift(jnp.int32(1), 11 - s)
        cnt = count((sc_c == t) & (col_c < cand), (sc_n == t) & (idx_n < cand))
        return jnp.where(cnt < need, cand, x)

    x = lax.fori_loop(0, 12, tie_step, jnp.zeros((ts, 1), I32))
    negm_c = jnp.where((sc_c > t) | ((sc_c == t) & (col_c <= x)), 0.0, -jnp.inf)
    negm_n = jnp.where((sc_n > t) | ((sc_n == t) & (idx_n <= x)), 0.0, -jnp.inf)

    dist_c = (row_c - col_c + past).astype(F32)
    dist_n = jnp.abs(row_n - col_n).astype(F32)
    q_all = qa_ref[...][:, :, 0:DH_A].reshape(H_A * ts, DH_A)
    s_c = _dot(q_all, cak_ref[...].astype(BF16))
    s_n = _dot_nt(q_all, pad_rows(nka_ref[...][:, 0:DH_A]))
    for h in range(H_A):
        rows = slice(h * ts, (h + 1) * ts)
        lc = s_c[rows] + (negm_c - (LOG2E * SLOPES_A[h]) * dist_c)
        ln = s_n[rows] + (negm_n - (LOG2E * SLOPES_A[h]) * dist_n)
        m = jnp.maximum(jnp.max(lc, axis=1, keepdims=True), jnp.max(ln, axis=1, keepdims=True))
        m = jnp.maximum(m, NEG_INIT)
        pc_ref[rows, :] = jnp.exp2(lc - m).astype(BF16)
        pn_ref[rows, :] = jnp.exp2(ln - m).astype(BF16)
    vc_aug_t = jnp.concatenate([cav_ref[...].astype(BF16), jnp.ones((64, past), BF16)], axis=0)
    pv = _dot_nt(pc_ref[...], vc_aug_t) + _dot(pn_ref[...], pad_rows(nva_ref[...]))
    pv = pv / pltpu.roll(pv, DH_A, axis=1)
    oa_ref[...] = jnp.concatenate(
        [pv[h * ts:(h + 1) * ts, :DH_A] for h in range(H_A)], axis=1).astype(BF16)

    lam = _lambda(lq1_ref, lk1_ref, lq2_ref, lk2_ref)
    neg_new = jnp.where(new_ok, 0.0, -jnp.inf)
    for h in range(H_B):
        a_c = None
        a_n = None
        for c in range(2):
            hc = 2 * h + c
            q = qb_ref[hc][:, 0:DH_B]
            kc_t = cbk_ref[hc * DH_B:(hc + 1) * DH_B, :].astype(BF16)
            lc = _dot(q, kc_t) - (LOG2E * SLOPES_B[h]) * dist_c
            ln = _dot_nt(q, pad_rows(nkb_ref[hc][:, 0:DH_B])) + (neg_new - (LOG2E * SLOPES_B[h]) * dist_n)
            m = jnp.maximum(jnp.max(lc, axis=1, keepdims=True), jnp.max(ln, axis=1, keepdims=True))
            pc = jnp.exp2(lc - m)
            pn = jnp.exp2(ln - m)
            l = jnp.sum(pc, axis=1, keepdims=True) + jnp.sum(pn, axis=1, keepdims=True)
            pc = pc / l
            pn = pn / l
            if c == 0:
                a_c, a_n = pc, pn
            else:
                a_c, a_n = a_c - lam * pc, a_n - lam * pn
        vc = cbv_ref[pl.ds(h, past, stride=H_B), :].astype(BF16)
        vn = pad_rows(nvb_ref[:, h * DV_B:(h + 1) * DV_B])
        o = _dot(a_c.astype(BF16), vc) + _dot(a_n.astype(BF16), vn)
        o = _rms(o, sg_ref[...]) * (1.0 - LAM_INIT)
        ob_ref[:, h * DV_B:(h + 1) * DV_B] = o.astype(BF16)


def _sample(qa, qi, sm, nka, nva, nki, qb, nkb, nvb, cak, cav, cai, cbk, cbv,
            lq1, lk1, lq2, lk2, sg, batch, ts, past, k_top):
    row = lambda b: (b, 0)
    head = lambda b: (0, b, 0)
    cache = lambda b: (b, 0, 0)
    const = lambda b: (0, 0)
    return pl.pallas_call(
        functools.partial(_sample_body, k_top=k_top, past=past, ts=ts),
        out_shape=(jax.ShapeDtypeStruct((batch * ts, H_A * DH_A), BF16),
                   jax.ShapeDtypeStruct((batch * ts, H_B * DV_B), BF16)),
        grid=(batch,),
        in_specs=[
            pl.BlockSpec((H_A, ts, 128), head),
            pl.BlockSpec((H_I, ts, D_IDX), head),
            pl.BlockSpec((ts, 256), row),
            pl.BlockSpec((ts, 128), row),
            pl.BlockSpec((ts, 128), row),
            pl.BlockSpec((ts, D_IDX), row),
            pl.BlockSpec((2 * H_B, ts, 128), head),
            pl.BlockSpec((2 * H_B, ts, 128), head),
            pl.BlockSpec((ts, H_B * DV_B), row),
            pl.BlockSpec((None, DH_A, past), cache),
            pl.BlockSpec((None, DH_A, past), cache),
            pl.BlockSpec((None, D_IDX, past), cache),
            pl.BlockSpec((None, 2 * H_B * DH_B, past), cache),
            pl.BlockSpec((None, past * H_B, DV_B), cache),
            pl.BlockSpec((1, DH_B), const),
            pl.BlockSpec((1, DH_B), const),
            pl.BlockSpec((1, DH_B), const),
            pl.BlockSpec((1, DH_B), const),
            pl.BlockSpec((1, DV_B), const),
        ],
        out_specs=(pl.BlockSpec((ts, H_A * DH_A), row), pl.BlockSpec((ts, H_B * DV_B), row)),
        scratch_shapes=[
            pltpu.VMEM((H_A * ts, past), BF16),
            pltpu.VMEM((H_A * ts, LANES), BF16),
        ],
        compiler_params=pltpu.CompilerParams(
            dimension_semantics=("arbitrary",), vmem_limit_bytes=VMEM_LIMIT),
        name="sample",
    )(qa, qi, sm, nka, nva, nki, qb, nkb, nvb, cak, cav, cai, cbk, cbv, lq1, lk1, lq2, lk2, sg)


def _finish_body(x_ref, oa_ref, ob_ref, wo_ref, gm_ref, wu_ref, wd_ref, gf_ref, y_ref):
    o = jnp.concatenate([oa_ref[...], ob_ref[...]], axis=1)
    h1 = x_ref[...] + _dot(o, wo_ref[...])
    hn = _rms(h1, gm_ref[...]).astype(BF16)
    acc = h1
    for c in range(D_FF // D_MODEL):
        u = jnp.maximum(_dot(hn, wu_ref[:, c * D_MODEL:(c + 1) * D_MODEL]), 0.0)
        acc = acc + _dot((u * u).astype(BF16), wd_ref[c * D_MODEL:(c + 1) * D_MODEL, :])
    y_ref[...] = _rms(acc, gf_ref[...])


def _finish(x, oa, ob, wo, gm, wu, wd, gf, tm):
    m = x.shape[0]
    assert m % tm == 0
    row = lambda i: (i, 0)
    const = lambda i: (0, 0)
    resident = dict(pipeline_mode=pl.Buffered(1))
    return pl.pallas_call(
        _finish_body,
        out_shape=jax.ShapeDtypeStruct((m, D_MODEL), F32),
        grid=(m // tm,),
        in_specs=[
            pl.BlockSpec((tm, D_MODEL), row),
            pl.BlockSpec((tm, 512), row),
            pl.BlockSpec((tm, 512), row),
            pl.BlockSpec((D_MODEL, D_MODEL), const, **resident),
            pl.BlockSpec((1, D_MODEL), const),
            pl.BlockSpec((D_MODEL, D_FF), const, **resident),
            pl.BlockSpec((D_FF, D_MODEL), const, **resident),
            pl.BlockSpec((1, D_MODEL), const),
        ],
        out_specs=pl.BlockSpec((tm, D_MODEL), row),
        compiler_params=pltpu.CompilerParams(
            dimension_semantics=("arbitrary",), vmem_limit_bytes=VMEM_LIMIT),
        name="finish",
    )(x, oa, ob, wo, gm, wu, wd, gf)


def kernel(x_prompt, x_sample, cache_a_k, cache_a_v, cache_a_idx_k, cache_b_k, cache_b_v,
           meta_tokens, attn_norm_g, w_in, idx_k_norm_g, idx_k_norm_b,
           lambda_q1, lambda_k1, lambda_q2, lambda_k2, subln_g, w_o,
           mlp_norm_g, w_up, w_down, final_norm_g):
    batch, seq, _ = x_prompt.shape
    dec_batch, ts, _ = x_sample.shape
    past = cache_a_k.shape[2]
    assert attn_norm_g.shape[0] == 1, "single-layer step"
    assert seq % TQ == 0 and ts == 16 and past % LANES == 0 and past // CHUNK == (past + ts - 1) // CHUNK
    n = N_META + seq
    k_top_p = min(TOPK_MAX, seq // 4)
    k_top_s = min(TOPK_MAX, (past + ts) // 4)

    w_t = w_in[0].T.astype(BF16)
    w = jnp.concatenate(
        [w_t[0:512],
         w_t[964:1476],
         w_t[1476:1988],
         w_t[1988:2500],
         w_t[640:896],
         w_t[512:640],
         w_t[896:964]],
        axis=0)
    w = jnp.pad(w, ((0, W_COLS - 2500), (0, 0))).T
    wvt = w_t[1476:2500]
    wo = w_o[0].astype(BF16)
    wu = w_up[0].astype(BF16)
    wd = w_down[0].astype(BF16)
    g_attn = attn_norm_g[0][None]
    g_mlp = mlp_norm_g[0][None]
    g_fin = final_norm_g[None]
    kng = idx_k_norm_g[0][None]
    knb = idx_k_norm_b[0][None]
    lq1, lk1, lq2, lk2 = lambda_q1[0][None], lambda_k1[0][None], lambda_q2[0][None], lambda_k2[0][None]
    sg = subln_g[0][None]

    xp = x_prompt.reshape(batch * seq, D_MODEL)
    xs = x_sample.reshape(dec_batch * ts, D_MODEL)

    (qa_p, qi_p, qb_p, kbh_p, vb_p, kab_p, kib_p, vt_p, wt_p, vbt_p, smt_p, kbt_p) = _proj(
        xp, g_attn, w, wvt, kng, knb, 512, N_META, seq, True)
    (_, _, _, kbh_m, vb_m, kab_m, kib_m, kb_m, vbb_m, sm_m, vaa_m) = _proj(
        meta_tokens, g_attn, w, wvt, kng, knb, N_META, 0, N_META, False)
    (qa_s, qi_s, qb_s, kbh_s, vb_s, kab_s, kib_s, kb_s, vbb_s, sm_s, vaa_s) = _proj(
        xs, g_attn, w, wvt, kng, knb, dec_batch * ts, past, dec_batch * ts, False)

    oa_p = _dsa(qa_p, qi_p, wt_p, kab_p, kib_p, vt_p, kab_m, kib_m, vaa_m, batch, seq, k_top_p)
    ob_p = _diff(qb_p, kbh_p, vbt_p, kbh_m, vbb_m, lq1, lk1, lq2, lk2, sg, batch, seq)
    y_prompt = _finish(xp, oa_p, ob_p, wo, g_mlp, wu, wd, g_fin, 512).reshape(batch, seq, D_MODEL)

    oa_s, ob_s = _sample(
        qa_s, qi_s, sm_s, kab_s, vaa_s, kib_s, qb_s, kbh_s, vbb_s,
        jnp.swapaxes(cache_a_k[0], 1, 2), jnp.swapaxes(cache_a_v[0], 1, 2),
        jnp.swapaxes(cache_a_idx_k[0], 1, 2),
        jnp.transpose(cache_b_k[0], (0, 2, 3, 4, 1)).reshape(dec_batch, 2 * H_B * DH_B, past),
        cache_b_v[0].reshape(dec_batch, past * H_B, DV_B),
        lq1, lk1, lq2, lk2, sg, dec_batch, ts, past, k_top_s)
    y_sample = _finish(xs, oa_s, ob_s, wo, g_mlp, wu, wd, g_fin, dec_batch * ts).reshape(
        dec_batch, ts, D_MODEL)

    def with_meta_t(meta_rows, main_t):
        meta_b = jnp.broadcast_to(meta_rows.T[None], (batch, 64, N_META))
        return jnp.swapaxes(jnp.concatenate([meta_b, main_t], axis=2), 1, 2)[None]

    new_a_k_p = with_meta_t(sm_m[:, SM_KA:SM_KA + 64], smt_p[:, SM_KA:SM_KA + 64])
    new_a_v_p = with_meta_t(sm_m[:, SM_VA:SM_VA + 64], smt_p[:, SM_VA:SM_VA + 64])
    new_a_i_p = with_meta_t(sm_m[:, SM_KI:SM_KI + 64], smt_p[:, SM_KI:SM_KI + 64])
    kb_meta_t = jnp.broadcast_to(kb_m.T.reshape(1, H_B, 2, DH_B, N_META), (batch, H_B, 2, DH_B, N_META))
    kb_main_t = kbt_p.reshape(batch, H_B, 2, DH_B, seq)
    new_b_k_p = jnp.transpose(
        jnp.concatenate([kb_meta_t, kb_main_t], axis=4), (0, 4, 1, 2, 3))[None]
    new_b_v_p = jnp.concatenate(
        [jnp.broadcast_to(vb_m.reshape(1, N_META, H_B, DV_B), (batch, N_META, H_B, DV_B)),
         vb_p.reshape(batch, seq, H_B, DV_B)], axis=1)[None]
    new_a_k_s = sm_s[:, SM_KA:SM_KA + 64].reshape(1, dec_batch, ts, 64)
    new_a_v_s = sm_s[:, SM_VA:SM_VA + 64].reshape(1, dec_batch, ts, 64)
    new_a_i_s = sm_s[:, SM_KI:SM_KI + 64].reshape(1, dec_batch, ts, 64)
    new_b_k_s = kb_s.reshape(1, dec_batch, ts, H_B, 2, DH_B)
    new_b_v_s = vb_s.reshape(1, dec_batch, ts, H_B, DV_B)
    return (y_prompt, y_sample, new_a_k_p, new_a_v_p, new_a_i_p, new_b_k_p, new_b_v_p,
            new_a_k_s, new_a_v_s, new_a_i_s, new_b_k_s, new_b_v_s)
```

```python
import functools

import jax
import jax.numpy as jnp
import numpy as np
from jax import lax
from jax.experimental import pallas as pl
from jax.experimental.pallas import tpu as pltpu

F32 = jnp.float32
BF16 = jnp.bfloat16
I32 = jnp.int32
I16 = jnp.int16
HALF16 = 32768

D_MODEL = 1024
CHUNK = 64
N_META = 16
H_A = 8
DH_A = 64
H_I = 4
D_IDX = 64
TOPK_MAX = 256
H_B = 4
DH_B = 64
DV_B = 2 * DH_B
D_FF = 4 * D_MODEL
EPS = 1e-6
LAM_INIT = 0.2

LANES = 128
C_QA, C_QB, C_KB, C_VB, C_QI, C_SM = 0, 512, 1024, 1536, 2048, 2304
W_COLS = 2560
SM_KA, SM_VA, SM_KI, SM_WI = 0, 64, 128, 192

LOG2E = float(np.float32(1.4426950408889634))
Q_SCALE = (DH_A ** -0.5) * LOG2E
N_SLOPE_PIECES = 3


def _bf16_pieces(value):
    rest = np.float32(value)
    pieces = []
    for _ in range(N_SLOPE_PIECES):
        piece = np.float32(rest.astype(jnp.bfloat16))
        pieces.append(float(piece))
        rest = np.float32(rest - piece)
    assert rest == 0.0
    return pieces
W_SCALE = (H_I ** -0.5) * (D_IDX ** -0.5)
SLOPES_A = tuple(2.0 ** (-8.0 * (i + 1) / H_A) for i in range(H_A))
SLOPES_B = tuple(2.0 ** (-8.0 * (i + 1) / H_B) for i in range(H_B))

INT_MIN = -(2 ** 31)
NEG_INIT = -1e30
TQ = 256
ONES_ROWS = 16
VT_ROWS = DV_B + ONES_ROWS
VTA_ROWS = DH_A + ONES_ROWS
CNT_ROWS = 32
VMEM_LIMIT = 56 * 1024 * 1024

_NT = (((1,), (1,)), ((), ()))


def _dot(a, b):
    return jnp.dot(a, b, preferred_element_type=F32)


def _dot_nt(a, b):
    return lax.dot_general(a, b, _NT, preferred_element_type=F32)


def _rms(x, g):
    return (x * lax.rsqrt(jnp.mean(x * x, axis=-1, keepdims=True) + EPS)) * g


def _to_key(score):
    b = lax.bitcast_convert_type(score, I32)
    return b ^ ((b >> 31) & jnp.int32(0x7FFFFFFF))


def _f32_from_key(key):
    return lax.bitcast_convert_type(key ^ ((key >> 31) & jnp.int32(0x7FFFFFFF)), F32)


def _bf16_from_key(key):
    bits = key ^ ((key >> 15) & jnp.int32(0x7FFF))
    return lax.bitcast_convert_type(jnp.left_shift(bits, 16), F32).astype(BF16)


KEY16_LOWEST = -32640
KEY32_LOWEST = -2139095040
KEY32_HIGHEST = 2139095039


def _wide(x, width):
    reps = width // LANES
    return x if reps == 1 else jnp.concatenate([x] * reps, axis=1)


def _proj_body(x_ref, g_ref, w_ref, wvt_ref, kng_ref, knb_ref,
               qa_ref, qi_ref, qb_ref, kbh_ref, vb_ref, kab_ref, kib_ref, *mode_refs,
               pos0, period, key_major):
    x = x_ref[...]
    hn = _rms(x, g_ref[...]).astype(BF16)
    rows = x.shape[0]

    def mm(lo, width):
        return _dot(hn, w_ref[:, lo:lo + width])

    lane = lax.broadcasted_iota(I32, (rows, 64), 1)
    r = pl.program_id(0) * rows + lax.broadcasted_iota(I32, (rows, 64), 0)
    pos = pos0 + r % period
    pos_cols = jnp.where(lane >= 2 * N_SLOPE_PIECES, 0,
                         jnp.where(lane % 2 == 0, (pos // 256) * 256, pos % 256))
    pos_cols = pos_cols.astype(F32).astype(BF16)

    def slope_cols(slope):
        cols = jnp.zeros((rows, 64), F32)
        for n, piece in enumerate(_bf16_pieces(slope * LOG2E)):
            cols = jnp.where(lane // 2 == n, piece, cols)
        return cols.astype(BF16)

    z = mm(C_QA, 512) * Q_SCALE
    for h in range(H_A):
        qa_ref[h] = jnp.concatenate(
            [z[:, h * DH_A:(h + 1) * DH_A].astype(BF16), slope_cols(SLOPES_A[h])], axis=1)
    z = mm(C_QB, 512) * Q_SCALE
    for hc in range(2 * H_B):
        qb_ref[hc] = jnp.concatenate(
            [z[:, hc * DH_B:(hc + 1) * DH_B].astype(BF16), slope_cols(SLOPES_B[hc // 2])], axis=1)
    if key_major:
        vt_ref, wt_ref, vbt_ref, smt_ref, kbt_ref = mode_refs
    else:
        kb_ref, vbb_ref, sm_ref, vaa_ref = mode_refs
    z = mm(C_KB, 512)
    if not key_major:
        kb_ref[...] = z
    for hc in range(2 * H_B):
        kbh_ref[hc] = jnp.concatenate([z[:, hc * DH_B:(hc + 1) * DH_B].astype(BF16), pos_cols], axis=1)
    z = mm(C_VB, 512)
    for h in range(H_B):
        vb_ref[pl.ds(h, rows, stride=H_B), :] = z[:, h * DV_B:(h + 1) * DV_B]
    if not key_major:
        vbb_ref[...] = z.astype(BF16)
    z = mm(C_QI, 256)
    for h in range(H_I):
        qi_ref[h] = z[:, h * D_IDX:(h + 1) * D_IDX].astype(BF16)
    z = mm(C_SM, 256)
    ka = z[:, SM_KA:SM_KA + 64]
    va = z[:, SM_VA:SM_VA + 64]
    ki = z[:, SM_KI:SM_KI + 64]
    xc = ki - jnp.mean(ki, axis=-1, keepdims=True)
    ki = xc * lax.rsqrt(jnp.mean(xc * xc, axis=-1, keepdims=True) + EPS) * kng_ref[...] + knb_ref[...]
    kab_ref[...] = jnp.concatenate([ka.astype(BF16), pos_cols], axis=1)
    kib_ref[...] = ki.astype(BF16)
    if not key_major:
        sm_ref[:, 0:128] = z[:, 0:128]
        sm_ref[:, SM_KI:SM_KI + 64] = ki
        sm_ref[:, SM_WI:SM_WI + 64] = z[:, SM_WI:SM_WI + 64] * W_SCALE
        vaa_ref[...] = jnp.concatenate([va.astype(BF16), jnp.ones((rows, 64), BF16)], axis=1)
    else:
        kv_t = z[:, 0:128].T
        for c in range(rows // TQ):
            vt_ref[c] = jnp.concatenate(
                [kv_t[64:128, c * TQ:(c + 1) * TQ].astype(BF16), jnp.ones((ONES_ROWS, TQ), BF16)], axis=0)
        kw_t = jnp.concatenate([ki, z[:, SM_WI:SM_WI + 64] * W_SCALE], axis=1).T
        wt_ref[...] = kw_t[64:72, :]
        smt_ref[0:128, :] = kv_t
        smt_ref[128:192, :] = kw_t[0:64, :]
        kbt_ref[...] = _dot_nt(wvt_ref[0:512, :], hn)
        vb_t = _dot_nt(wvt_ref[512:1024, :], hn)
        for c in range(rows // TQ):
            for h in range(H_B):
                vbt_ref[c, h] = jnp.concatenate(
                    [vb_t[h * DV_B:(h + 1) * DV_B, c * TQ:(c + 1) * TQ].astype(BF16),
                     jnp.ones((VT_ROWS - DV_B, TQ), BF16)], axis=0)


def _proj(x, g, w, wvt, kng, knb, tm, pos0, period, key_major):
    m = x.shape[0]
    assert m % tm == 0
    row = lambda i: (i, 0)
    head = lambda i: (0, i, 0)
    const = lambda i: (0, 0)
    out_shape = (
        jax.ShapeDtypeStruct((H_A, m, 128), BF16),
        jax.ShapeDtypeStruct((H_I, m, D_IDX), BF16),
        jax.ShapeDtypeStruct((2 * H_B, m, 128), BF16),
        jax.ShapeDtypeStruct((2 * H_B, m, 128), BF16),
        jax.ShapeDtypeStruct((m * H_B, DV_B), F32),
        jax.ShapeDtypeStruct((m, 128), BF16),
        jax.ShapeDtypeStruct((m, 64), BF16),
    )
    out_specs = (
        pl.BlockSpec((H_A, tm, 128), head),
        pl.BlockSpec((H_I, tm, D_IDX), head),
        pl.BlockSpec((2 * H_B, tm, 128), head),
        pl.BlockSpec((2 * H_B, tm, 128), head),
        pl.BlockSpec((tm * H_B, DV_B), row),
        pl.BlockSpec((tm, 128), row),
        pl.BlockSpec((tm, 64), row),
    )
    if not key_major:
        out_shape += (
            jax.ShapeDtypeStruct((m, 512), F32),
            jax.ShapeDtypeStruct((m, 512), BF16),
            jax.ShapeDtypeStruct((m, 256), F32),
            jax.ShapeDtypeStruct((m, 128), BF16),
        )
        out_specs += (
            pl.BlockSpec((tm, 512), row),
            pl.BlockSpec((tm, 512), row),
            pl.BlockSpec((tm, 256), row),
            pl.BlockSpec((tm, 128), row),
        )
    else:
        assert tm % TQ == 0 and period % tm == 0 and m % period == 0
        per_stream = period // tm
        stream_cols = lambda i: (i // per_stream, 0, i % per_stream)
        out_shape += (
            jax.ShapeDtypeStruct((m // TQ, VTA_ROWS, TQ), BF16),
            jax.ShapeDtypeStruct((8, m), F32),
            jax.ShapeDtypeStruct((m // TQ, H_B, VT_ROWS, TQ), BF16),
            jax.ShapeDtypeStruct((m // period, 192, period), F32),
            jax.ShapeDtypeStruct((m // period, 512, period), F32),
        )
        out_specs += (
            pl.BlockSpec((tm // TQ, VTA_ROWS, TQ), lambda i: (i, 0, 0)),
            pl.BlockSpec((8, tm), lambda i: (0, i)),
            pl.BlockSpec((tm // TQ, H_B, VT_ROWS, TQ), lambda i: (i, 0, 0, 0)),
            pl.BlockSpec((None, 192, tm), stream_cols),
            pl.BlockSpec((None, 512, tm), stream_cols),
        )
    return pl.pallas_call(
        functools.partial(_proj_body, pos0=pos0, period=period, key_major=key_major),
        out_shape=out_shape,
        grid=(m // tm,),
        in_specs=[
            pl.BlockSpec((tm, D_MODEL), row),
            pl.BlockSpec((1, D_MODEL), const),
            pl.BlockSpec((D_MODEL, W_COLS), const),
            pl.BlockSpec((2 * H_B * DV_B, D_MODEL), const),
            pl.BlockSpec((1, D_IDX), const),
            pl.BlockSpec((1, D_IDX), const),
        ],
        out_specs=out_specs,
        compiler_params=pltpu.CompilerParams(
            dimension_semantics=("arbitrary",), vmem_limit_bytes=VMEM_LIMIT),
        name="proj",
    )(x, g, w, wvt, kng, knb)


def _dsa_body(qa_ref, qi_ref, wt_ref, ka_ref, ki_ref, vt_ref, mka_ref, mki_ref, mva_ref,
              o_ref, smeta_ref, smain_ref, bmeta_ref, bmain_ref,
              tsel_ref, m_ref, al_ref, acc_ref, p_ref, s_ref, seen_ref, *, k_top):
    i = pl.program_id(1)
    krow = lax.broadcasted_iota(I32, (TQ, TQ), 0)
    qcol = lax.broadcasted_iota(I32, (TQ, TQ), 1)

    def main_rows(j):
        return pl.ds(pl.multiple_of(j * TQ, TQ), TQ)

    qi_all = qi_ref[...].reshape(H_I * TQ, D_IDX)

    def scores(ki_blk):
        s4 = _dot_nt(ki_blk, qi_all)
        sc = None
        for h in range(H_I):
            t = jnp.maximum(s4[:, h * TQ:(h + 1) * TQ], 0.0) * wt_ref[h:h + 1, :]
            sc = t if sc is None else sc + t
        return sc

    sc = scores(mki_ref[...])
    smeta_ref[...] = sc
    bmeta_ref[...] = sc.astype(BF16)

    def score_block(j, sc):
        smain_ref[j] = sc
        bmain_ref[j] = sc.astype(BF16)

    def score_full(j, carry):
        score_block(j, scores(ki_ref[main_rows(j), :]))
        return carry

    lax.fori_loop(0, i, score_full, 0)
    score_block(i, jnp.where((krow // CHUNK) <= (qcol // CHUNK),
                             scores(ki_ref[main_rows(i), :]), -jnp.inf))

    def count(meta_ref, main_ref, part, pred):
        def body(j, c):
            return c + part(pred(main_ref[j]))

        c = lax.fori_loop(0, i + 1, body, jnp.zeros((CNT_ROWS, TQ), F32))
        cm = part(pred(meta_ref[...]), rows=N_META)
        return jnp.sum(c, axis=0, keepdims=True) + jnp.sum(cm, axis=0, keepdims=True)

    def part32(hit, rows=CNT_ROWS):
        ones = jnp.where(hit, 1.0, 0.0)
        return jnp.sum(ones.reshape(ones.shape[0] // rows, rows, TQ), axis=0)

    def part16(hit, rows=CNT_ROWS):
        ones = jnp.where(hit, jnp.bfloat16(1), jnp.bfloat16(0))
        acc = ones[0:rows]
        for t in range(1, ones.shape[0] // rows):
            acc = acc + ones[t * rows:(t + 1) * rows]
        return acc.astype(F32)

    kf = float(k_top)

    def step16(s, u):
        cand_u = u | jnp.left_shift(jnp.int32(1), 15 - s)
        cand = _bf16_from_key(jnp.maximum(cand_u, KEY16_LOWEST + HALF16) - HALF16)
        cnt = count(bmeta_ref, bmain_ref, part16, lambda x: x >= cand)
        return jnp.where(cnt >= kf, cand_u, u)

    u_hi = lax.fori_loop(0, 16, step16, jnp.zeros((1, TQ), I32))
    t_hi = _bf16_from_key(jnp.maximum(u_hi, KEY16_LOWEST + HALF16) - HALF16).astype(F32)
    base = jnp.maximum(_to_key(t_hi) - 65536, KEY32_LOWEST)

    def step32(s, d):
        cand_d = d | jnp.left_shift(jnp.int32(1), 16 - s)
        cand = _f32_from_key(jnp.minimum(base + cand_d, KEY32_HIGHEST))
        cnt = count(smeta_ref, smain_ref, part32, lambda x: x >= cand)
        return jnp.where(cnt >= kf, cand_d, d)

    d_lo = lax.fori_loop(0, 17, step32, jnp.zeros((1, TQ), I32))
    t_sel = _f32_from_key(jnp.minimum(base + d_lo, KEY32_HIGHEST))
    need = kf - count(smeta_ref, smain_ref, part32, lambda x: x > t_sel)
    tsel_ref[...] = jnp.broadcast_to(t_sel, tsel_ref.shape)

    q_all = qa_ref[...].reshape(H_A * TQ, 128)
    m_ref[...] = jnp.full(m_ref.shape, NEG_INIT, F32)
    acc_ref[...] = jnp.zeros(acc_ref.shape, F32)
    thr = tsel_ref[0:1, :]

    def masked_logits(k_blk, sc, slot):
        nk = sc.shape[0]
        tie = sc == thr
        tri = (lax.broadcasted_iota(I32, (nk, nk), 1)
               <= lax.broadcasted_iota(I32, (nk, nk), 0)).astype(BF16)
        rank = _dot(tri, jnp.where(tie, 1.0, 0.0).astype(BF16)) + seen_ref[...]
        seen_ref[...] = rank[nk - 1:nk, :]
        negm = jnp.where(sc > thr, 0.0, jnp.where(tie, jnp.where(rank <= need, 0.0, -jnp.inf), -jnp.inf))
        s_ref[slot, 0:nk, :] = _dot_nt(k_blk, q_all) + jnp.concatenate([negm] * H_A, axis=1)

    def softmax(slot, nk, future):
        for h in range(H_A):
            cols = slice(h * TQ, (h + 1) * TQ)

            def logits():
                s = s_ref[slot, 0:nk, cols]
                if future is not None:
                    s = s - (2.0 * LOG2E * SLOPES_A[h]) * future
                return s

            m_old = m_ref[:, cols]
            m_new = jnp.maximum(m_old, jnp.max(logits(), axis=0, keepdims=True))
            p_ref[slot, 0:nk, cols] = jnp.exp2(logits() - m_new).astype(BF16)
            al_ref[slot, :, cols] = jnp.exp2(m_old - m_new)
            m_ref[:, cols] = m_new

    def accumulate(vt_blk, slot, nk):
        acc_ref[...] = acc_ref[...] * al_ref[slot] + _dot(vt_blk, p_ref[slot, 0:nk, :])

    eye = (lax.broadcasted_iota(I32, (128, 128), 0)
           == lax.broadcasted_iota(I32, (128, 128), 1)).astype(BF16)
    seen_ref[...] = jnp.zeros(seen_ref.shape, F32)
    masked_logits(mka_ref[...], smeta_ref[...], 0)
    softmax(0, N_META, None)
    accumulate(_dot_nt(eye, mva_ref[...])[0:VTA_ROWS].astype(BF16), 0, N_META)

    p_ref[1] = jnp.zeros(p_ref.shape[1:], BF16)
    al_ref[1] = jnp.ones(al_ref.shape[1:], F32)
    masked_logits(ka_ref[main_rows(0), :], smain_ref[0], 0)

    def full_step(t, slot):
        masked_logits(ka_ref[main_rows(t + 1), :], smain_ref[t + 1], 1 - slot)
        accumulate(vt_ref[jnp.maximum(t - 1, 0)], 1 - slot, TQ)
        softmax(slot, TQ, None)

    def two_steps(u, carry):
        full_step(2 * u, 0)
        full_step(2 * u + 1, 1)
        return carry

    lax.fori_loop(0, i // 2, two_steps, 0)

    def last_step(slot):
        accumulate(vt_ref[jnp.maximum(i - 1, 0)], 1 - slot, TQ)
        softmax(slot, TQ, jnp.maximum(krow - qcol, 0).astype(F32))
        accumulate(vt_ref[i], slot, TQ)

    @pl.when(i % 2 == 1)
    def _odd():
        full_step(i - 1, 0)
        last_step(1)

    @pl.when(i % 2 == 0)
    def _even():
        last_step(0)

    def head_out_t(h):
        a = acc_ref[:, h * TQ:(h + 1) * TQ]
        return a[0:DH_A] / a[DH_A:DH_A + 1]

    for h in range(0, H_A, 2):
        pair_t = jnp.concatenate([head_out_t(h), head_out_t(h + 1)], axis=0)
        o_ref[:, h * DH_A:(h + 2) * DH_A] = pair_t.T.astype(BF16)


def _dsa(qa, qi, wt, kab, kib, vt, mka, mki, mva, batch, seq, k_top):
    nq = seq // TQ
    qrow = lambda b, i: (b * nq + i, 0)
    qhead = lambda b, i: (0, b * nq + i, 0)
    kv = lambda b, i: (b, 0)
    const = lambda b, i: (0, 0)
    return pl.pallas_call(
        functools.partial(_dsa_body, k_top=k_top),
        out_shape=jax.ShapeDtypeStruct((batch * seq, H_A * DH_A), BF16),
        grid=(batch, nq),
        in_specs=[
            pl.BlockSpec((H_A, TQ, 128), qhead),
            pl.BlockSpec((H_I, TQ, D_IDX), qhead),
            pl.BlockSpec((8, TQ), lambda b, i: (0, b * nq + i)),
            pl.BlockSpec((seq, 128), kv),
            pl.BlockSpec((seq, D_IDX), kv),
            pl.BlockSpec((nq, VTA_ROWS, TQ), lambda b, i: (b, 0, 0)),
            pl.BlockSpec((N_META, 128), const),
            pl.BlockSpec((N_META, D_IDX), const),
            pl.BlockSpec((N_META, 128), const),
        ],
        out_specs=pl.BlockSpec((TQ, H_A * DH_A), qrow),
        scratch_shapes=[
            pltpu.VMEM((N_META, TQ), F32),
            pltpu.VMEM((nq, TQ, TQ), F32),
            pltpu.VMEM((N_META, TQ), BF16),
            pltpu.VMEM((nq, TQ, TQ), BF16),
            pltpu.VMEM((8, TQ), F32),
            pltpu.VMEM((1, H_A * TQ), F32),
            pltpu.VMEM((2, 1, H_A * TQ), F32),
            pltpu.VMEM((VTA_ROWS, H_A * TQ), F32),
            pltpu.VMEM((2, TQ, H_A * TQ), BF16),
            pltpu.VMEM((2, TQ, H_A * TQ), F32),
            pltpu.VMEM((1, TQ), F32),
        ],
        compiler_params=pltpu.CompilerParams(
            dimension_semantics=("arbitrary", "arbitrary"), vmem_limit_bytes=VMEM_LIMIT),
        name="dsa",
    )(qa, qi, wt, kab, kib, vt, mka, mki, mva)


def _lambda(lq1_ref, lk1_ref, lq2_ref, lk2_ref):
    s1 = jnp.sum(lq1_ref[...] * lk1_ref[...], axis=-1, keepdims=True)
    s2 = jnp.sum(lq2_ref[...] * lk2_ref[...], axis=-1, keepdims=True)
    return jnp.exp(s1) - jnp.exp(s2) + LAM_INIT


def _diff_body(qb_ref, kb_ref, vt_ref, mkb_ref, mvb_ref, lq1_ref, lk1_ref, lq2_ref, lk2_ref,
               sg_ref, o_ref, m_ref, al_ref, acc_ref, p_ref, s_ref):
    i = pl.program_id(1)
    n_hc = 2 * H_B
    krow = lax.broadcasted_iota(I32, (TQ, TQ), 0)
    qcol = lax.broadcasted_iota(I32, (TQ, TQ), 1)
    m_ref[...] = jnp.full(m_ref.shape, NEG_INIT, F32)
    acc_ref[...] = jnp.zeros(acc_ref.shape, F32)

    def main_rows(j):
        return pl.ds(pl.multiple_of(j * TQ, TQ), TQ)

    def logits(k_of, slot, nk):
        for hc in range(n_hc):
            s_ref[slot, hc, 0:nk, :] = _dot_nt(k_of(hc), qb_ref[hc])

    def softmax(slot, nk, negm, future):
        for hc in range(n_hc):
            cols = slice(hc * TQ, (hc + 1) * TQ)

            def logits():
                s = s_ref[slot, hc, 0:nk, :]
                if negm is not None:
                    s = s + negm
                if future is not None:
                    s = s - (2.0 * LOG2E * SLOPES_B[hc // 2]) * future
                return s

            m_old = m_ref[:, cols]
            m_new = jnp.maximum(m_old, jnp.max(logits(), axis=0, keepdims=True))
            p_ref[slot, 0:nk, cols] = jnp.exp2(logits() - m_new).astype(BF16)
            al_ref[slot, :, cols] = jnp.exp2(m_old - m_new)
            m_ref[:, cols] = m_new

    def accumulate(vt_of, slot, nk):
        for h in range(H_B):
            cols = slice(2 * h * TQ, (2 * h + 2) * TQ)
            acc_ref[h] = acc_ref[h] * al_ref[slot, :, cols] + _dot(vt_of(h), p_ref[slot, 0:nk, cols])

    eye = (lax.broadcasted_iota(I32, (DV_B, DV_B), 0)
           == lax.broadcasted_iota(I32, (DV_B, DV_B), 1)).astype(BF16)

    def meta_vt(h):
        v_t = _dot_nt(eye, mvb_ref[:, h * DV_B:(h + 1) * DV_B]).astype(BF16)
        return jnp.concatenate([v_t, jnp.ones((VT_ROWS - DV_B, N_META), BF16)], axis=0)

    logits(lambda hc: mkb_ref[hc], 0, N_META)
    softmax(0, N_META, None, None)
    accumulate(meta_vt, 0, N_META)

    p_ref[1] = jnp.zeros(p_ref.shape[1:], BF16)
    al_ref[1] = jnp.ones(al_ref.shape[1:], F32)
    logits(lambda hc: kb_ref[hc, main_rows(0), :], 0, TQ)

    def main_vt(t):
        return lambda h: vt_ref[t, h]

    def full_step(t, slot):
        logits(lambda hc: kb_ref[hc, main_rows(t + 1), :], 1 - slot, TQ)
        accumulate(main_vt(jnp.maximum(t - 1, 0)), 1 - slot, TQ)
        softmax(slot, TQ, None, None)

    def two_steps(u, carry):
        full_step(2 * u, 0)
        full_step(2 * u + 1, 1)
        return carry

    lax.fori_loop(0, i // 2, two_steps, 0)

    def last_step(slot):
        accumulate(main_vt(jnp.maximum(i - 1, 0)), 1 - slot, TQ)
        softmax(slot, TQ, jnp.where((krow // CHUNK) <= (qcol // CHUNK), 0.0, -jnp.inf),
                jnp.maximum(krow - qcol, 0).astype(F32))
        accumulate(main_vt(i), slot, TQ)

    @pl.when(i % 2 == 1)
    def _odd():
        full_step(i - 1, 0)
        last_step(1)

    @pl.when(i % 2 == 0)
    def _even():
        last_step(0)

    lam = _lambda(lq1_ref, lk1_ref, lq2_ref, lk2_ref)
    for h in range(H_B):
        a = acc_ref[h]
        o0 = a[0:DV_B, 0:TQ] / a[DV_B:DV_B + 1, 0:TQ]
        o1 = a[0:DV_B, TQ:2 * TQ] / a[DV_B:DV_B + 1, TQ:2 * TQ]
        o = _rms((o0 - lam * o1).T, sg_ref[...]) * (1.0 - LAM_INIT)
        o_ref[:, h * DV_B:(h + 1) * DV_B] = o.astype(BF16)


def _diff(qb, kbh, vbt, mkb, mvb, lq1, lk1, lq2, lk2, sg, batch, seq):
    nq = seq // TQ
    qrow = lambda b, i: (b * nq + i, 0)
    qhead = lambda b, i: (0, b * nq + i, 0)
    const2 = lambda b, i: (0, 0)
    const3 = lambda b, i: (0, 0, 0)
    return pl.pallas_call(
        _diff_body,
        out_shape=jax.ShapeDtypeStruct((batch * seq, H_B * DV_B), BF16),
        grid=(batch, nq),
        in_specs=[
            pl.BlockSpec((2 * H_B, TQ, 128), qhead),
            pl.BlockSpec((2 * H_B, seq, 128), lambda b, i: (0, b, 0)),
            pl.BlockSpec((nq, H_B, VT_ROWS, TQ), lambda b, i: (b, 0, 0, 0)),
            pl.BlockSpec((2 * H_B, N_META, 128), const3),
            pl.BlockSpec((N_META, H_B * DV_B), const2),
            pl.BlockSpec((1, DH_B), const2),
            pl.BlockSpec((1, DH_B), const2),
            pl.BlockSpec((1, DH_B), const2),
            pl.BlockSpec((1, DH_B), const2),
            pl.BlockSpec((1, DV_B), const2),
        ],
        out_specs=pl.BlockSpec((TQ, H_B * DV_B), qrow),
        scratch_shapes=[
            pltpu.VMEM((1, 2 * H_B * TQ), F32),
            pltpu.VMEM((2, 1, 2 * H_B * TQ), F32),
            pltpu.VMEM((H_B, VT_ROWS, 2 * TQ), F32),
            pltpu.VMEM((2, TQ, 2 * H_B * TQ), BF16),
            pltpu.VMEM((2, 2 * H_B, TQ, TQ), F32),
        ],
        compiler_params=pltpu.CompilerParams(
            dimension_semantics=("arbitrary", "arbitrary"), vmem_limit_bytes=VMEM_LIMIT),
        name="diff",
    )(qb, kbh, vbt, mkb, mvb, lq1, lk1, lq2, lk2, sg)


def _sample_body(qa_ref, qi_ref, sm_ref, nka_ref, nva_ref, nki_ref, qb_ref, nkb_ref, nvb_ref,
                 cak_ref, cav_ref, cai_ref, cbk_ref, cbv_ref,
                 lq1_ref, lk1_ref, lq2_ref, lk2_ref, sg_ref,
                 oa_ref, ob_ref, pc_ref, pn_ref, *, k_top, past, ts):
    pad = LANES - ts
    row_c = lax.broadcasted_iota(I32, (ts, past), 0)
    col_c = lax.broadcasted_iota(I32, (ts, past), 1)
    row_n = lax.broadcasted_iota(I32, (ts, LANES), 0)
    col_n = lax.broadcasted_iota(I32, (ts, LANES), 1)
    new_ok = col_n < ts

    def pad_rows(x):
        return jnp.concatenate([x, jnp.zeros((pad,) + x.shape[1:], x.dtype)], axis=0)

    qi_all = qi_ref[...].reshape(H_I * ts, D_IDX)
    w = sm_ref[:, SM_WI:SM_WI + H_I]
    s4c = _dot(qi_all, cai_ref[...].astype(BF16))
    s4n = _dot_nt(qi_all, pad_rows(nki_ref[...]))
    sc_c = None
    sc_n = None
    for h in range(H_I):
        wh = w[:, h:h + 1]
        tc = jnp.maximum(s4c[h * ts:(h + 1) * ts], 0.0) * wh
        tn = jnp.maximum(s4n[h * ts:(h + 1) * ts], 0.0) * wh
        sc_c = tc if sc_c is None else sc_c + tc
        sc_n = tn if sc_n is None else sc_n + tn
    sc_n = jnp.where(new_ok, sc_n, -jnp.inf)

    def count(pred_c, pred_n):
        c = jnp.sum(jnp.where(pred_c, 1.0, 0.0), axis=1, keepdims=True)
        return c + jnp.sum(jnp.where(pred_n, 1.0, 0.0), axis=1, keepdims=True)

    def threshold(key):
        return _f32_from_key(jnp.clip(key, KEY32_LOWEST, KEY32_HIGHEST))

    def value_step(s, key):
        cand_key = key ^ jnp.left_shift(jnp.int32(1), 31 - s)
        cand = threshold(cand_key)
        cnt = count(sc_c >= cand, sc_n >= cand)
        return jnp.where(cnt >= float(k_top), cand_key, key)

    t = threshold(lax.fori_loop(0, 32, value_step, jnp.full((ts, 1), INT_MIN, I32)))
    need = float(k_top) - count(sc_c > t, sc_n > t)
    idx_n = col_n + past

    def tie_step(s, x):
        cand = x | jnp.left_shift(jnp.int32(1), 11 - s)
        cnt = count((sc_c == t) & (col_c < cand), (sc_n == t) & (idx_n < cand))
        return jnp.where(cnt < need, cand, x)

    x = lax.fori_loop(0, 12, tie_step, jnp.zeros((ts, 1), I32))
    negm_c = jnp.where((sc_c > t) | ((sc_c == t) & (col_c <= x)), 0.0, -jnp.inf)
    negm_n = jnp.where((sc_n > t) | ((sc_n == t) & (idx_n <= x)), 0.0, -jnp.inf)

    dist_c = (row_c - col_c + past).astype(F32)
    dist_n = jnp.abs(row_n - col_n).astype(F32)
    q_all = qa_ref[...][:, :, 0:DH_A].reshape(H_A * ts, DH_A)
    s_c = _dot(q_all, cak_ref[...].astype(BF16))
    s_n = _dot_nt(q_all, pad_rows(nka_ref[...][:, 0:DH_A]))
    for h in range(H_A):
        rows = slice(h * ts, (h + 1) * ts)
        lc = s_c[rows] + (negm_c - (LOG2E * SLOPES_A[h]) * dist_c)
        ln = s_n[rows] + (negm_n - (LOG2E * SLOPES_A[h]) * dist_n)
        m = jnp.maximum(jnp.max(lc, axis=1, keepdims=True), jnp.max(ln, axis=1, keepdims=True))
        m = jnp.maximum(m, NEG_INIT)
        pc_ref[rows, :] = jnp.exp2(lc - m).astype(BF16)
        pn_ref[rows, :] = jnp.exp2(ln - m).astype(BF16)
    vc_aug_t = jnp.concatenate([cav_ref[...].astype(BF16), jnp.ones((64, past), BF16)], axis=0)
    pv = _dot_nt(pc_ref[...], vc_aug_t) + _dot(pn_ref[...], pad_rows(nva_ref[...]))
    pv = pv / pltpu.roll(pv, DH_A, axis=1)
    oa_ref[...] = jnp.concatenate(
        [pv[h * ts:(h + 1) * ts, :DH_A] for h in range(H_A)], axis=1).astype(BF16)

    lam = _lambda(lq1_ref, lk1_ref, lq2_ref, lk2_ref)
    neg_new = jnp.where(new_ok, 0.0, -jnp.inf)
    for h in range(H_B):
        a_c = None
        a_n = None
        for c in range(2):
            hc = 2 * h + c
            q = qb_ref[hc][:, 0:DH_B]
            kc_t = cbk_ref[hc * DH_B:(hc + 1) * DH_B, :].astype(BF16)
            lc = _dot(q, kc_t) - (LOG2E * SLOPES_B[h]) * dist_c
            ln = _dot_nt(q, pad_rows(nkb_ref[hc][:, 0:DH_B])) + (neg_new - (LOG2E * SLOPES_B[h]) * dist_n)
            m = jnp.maximum(jnp.max(lc, axis=1, keepdims=True), jnp.max(ln, axis=1, keepdims=True))
            pc = jnp.exp2(lc - m)
            pn = jnp.exp2(ln - m)
            l = jnp.sum(pc, axis=1, keepdims=True) + jnp.sum(pn, axis=1, keepdims=True)
            pc = pc / l
            pn = pn / l
            if c == 0:
                a_c, a_n = pc, pn
            else:
                a_c, a_n = a_c - lam * pc, a_n - lam * pn
        vc = cbv_ref[pl.ds(h, past, stride=H_B), :].astype(BF16)
        vn = pad_rows(nvb_ref[:, h * DV_B:(h + 1) * DV_B])
        o = _dot(a_c.astype(BF16), vc) + _dot(a_n.astype(BF16), vn)
        o = _rms(o, sg_ref[...]) * (1.0 - LAM_INIT)
        ob_ref[:, h * DV_B:(h + 1) * DV_B] = o.astype(BF16)


def _sample(qa, qi, sm, nka, nva, nki, qb, nkb, nvb, cak, cav, cai, cbk, cbv,
            lq1, lk1, lq2, lk2, sg, batch, ts, past, k_top):
    row = lambda b: (b, 0)
    head = lambda b: (0, b, 0)
    cache = lambda b: (b, 0, 0)
    const = lambda b: (0, 0)
    return pl.pallas_call(
        functools.partial(_sample_body, k_top=k_top, past=past, ts=ts),
        out_shape=(jax.ShapeDtypeStruct((batch * ts, H_A * DH_A), BF16),
                   jax.ShapeDtypeStruct((batch * ts, H_B * DV_B), BF16)),
        grid=(batch,),
        in_specs=[
            pl.BlockSpec((H_A, ts, 128), head),
            pl.BlockSpec((H_I, ts, D_IDX), head),
            pl.BlockSpec((ts, 256), row),
            pl.BlockSpec((ts, 128), row),
            pl.BlockSpec((ts, 128), row),
            pl.BlockSpec((ts, D_IDX), row),
            pl.BlockSpec((2 * H_B, ts, 128), head),
            pl.BlockSpec((2 * H_B, ts, 128), head),
            pl.BlockSpec((ts, H_B * DV_B), row),
            pl.BlockSpec((None, DH_A, past), cache),
            pl.BlockSpec((None, DH_A, past), cache),
            pl.BlockSpec((None, D_IDX, past), cache),
            pl.BlockSpec((None, 2 * H_B * DH_B, past), cache),
            pl.BlockSpec((None, past * H_B, DV_B), cache),
            pl.BlockSpec((1, DH_B), const),
            pl.BlockSpec((1, DH_B), const),
            pl.BlockSpec((1, DH_B), const),
            pl.BlockSpec((1, DH_B), const),
            pl.BlockSpec((1, DV_B), const),
        ],
        out_specs=(pl.BlockSpec((ts, H_A * DH_A), row), pl.BlockSpec((ts, H_B * DV_B), row)),
        scratch_shapes=[
            pltpu.VMEM((H_A * ts, past), BF16),
            pltpu.VMEM((H_A * ts, LANES), BF16),
        ],
        compiler_params=pltpu.CompilerParams(
            dimension_semantics=("arbitrary",), vmem_limit_bytes=VMEM_LIMIT),
        name="sample",
    )(qa, qi, sm, nka, nva, nki, qb, nkb, nvb, cak, cav, cai, cbk, cbv, lq1, lk1, lq2, lk2, sg)


def _finish_body(x_ref, oa_ref, ob_ref, wo_ref, gm_ref, wu_ref, wd_ref, gf_ref, y_ref):
    o = jnp.concatenate([oa_ref[...], ob_ref[...]], axis=1)
    h1 = x_ref[...] + _dot(o, wo_ref[...])
    hn = _rms(h1, gm_ref[...]).astype(BF16)
    acc = h1
    for c in range(D_FF // D_MODEL):
        u = jnp.maximum(_dot(hn, wu_ref[:, c * D_MODEL:(c + 1) * D_MODEL]), 0.0)
        acc = acc + _dot((u * u).astype(BF16), wd_ref[c * D_MODEL:(c + 1) * D_MODEL, :])
    y_ref[...] = _rms(acc, gf_ref[...])


def _finish(x, oa, ob, wo, gm, wu, wd, gf, tm):
    m = x.shape[0]
    assert m % tm == 0
    row = lambda i: (i, 0)
    const = lambda i: (0, 0)
    resident = dict(pipeline_mode=pl.Buffered(1))
    return pl.pallas_call(
        _finish_body,
        out_shape=jax.ShapeDtypeStruct((m, D_MODEL), F32),
        grid=(m // tm,),
        in_specs=[
            pl.BlockSpec((tm, D_MODEL), row),
            pl.BlockSpec((tm, 512), row),
            pl.BlockSpec((tm, 512), row),
            pl.BlockSpec((D_MODEL, D_MODEL), const, **resident),
            pl.BlockSpec((1, D_MODEL), const),
            pl.BlockSpec((D_MODEL, D_FF), const, **resident),
            pl.BlockSpec((D_FF, D_MODEL), const, **resident),
            pl.BlockSpec((1, D_MODEL), const),
        ],
        out_specs=pl.BlockSpec((tm, D_MODEL), row),
        compiler_params=pltpu.CompilerParams(
            dimension_semantics=("arbitrary",), vmem_limit_bytes=VMEM_LIMIT),
        name="finish",
    )(x, oa, ob, wo, gm, wu, wd, gf)


def kernel(x_prompt, x_sample, cache_a_k, cache_a_v, cache_a_idx_k, cache_b_k, cache_b_v,
           meta_tokens, attn_norm_g, w_in, idx_k_norm_g, idx_k_norm_b,
           lambda_q1, lambda_k1, lambda_q2, lambda_k2, subln_g, w_o,
           mlp_norm_g, w_up, w_down, final_norm_g):
    batch, seq, _ = x_prompt.shape
    dec_batch, ts, _ = x_sample.shape
    past = cache_a_k.shape[2]
    assert attn_norm_g.shape[0] == 1, "single-layer step"
    assert seq % TQ == 0 and ts == 16 and past % LANES == 0 and past // CHUNK == (past + ts - 1) // CHUNK
    n = N_META + seq
    k_top_p = min(TOPK_MAX, seq // 4)
    k_top_s = min(TOPK_MAX, (past + ts) // 4)

    w_t = w_in[0].T.astype(BF16)
    w = jnp.concatenate(
        [w_t[0:512],
         w_t[964:1476],
         w_t[1476:1988],
         w_t[1988:2500],
         w_t[640:896],
         w_t[512:640],
         w_t[896:964]],
        axis=0)
    w = jnp.pad(w, ((0, W_COLS - 2500), (0, 0))).T
    wvt = w_t[1476:2500]
    wo = w_o[0].astype(BF16)
    wu = w_up[0].astype(BF16)
    wd = w_down[0].astype(BF16)
    g_attn = attn_norm_g[0][None]
    g_mlp = mlp_norm_g[0][None]
    g_fin = final_norm_g[None]
    kng = idx_k_norm_g[0][None]
    knb = idx_k_norm_b[0][None]
    lq1, lk1, lq2, lk2 = lambda_q1[0][None], lambda_k1[0][None], lambda_q2[0][None], lambda_k2[0][None]
    sg = subln_g[0][None]

    xp = x_prompt.reshape(batch * seq, D_MODEL)
    xs = x_sample.reshape(dec_batch * ts, D_MODEL)

    (qa_p, qi_p, qb_p, kbh_p, vb_p, kab_p, kib_p, vt_p, wt_p, vbt_p, smt_p, kbt_p) = _proj(
        xp, g_attn, w, wvt, kng, knb, 512, N_META, seq, True)
    (_, _, _, kbh_m, vb_m, kab_m, kib_m, kb_m, vbb_m, sm_m, vaa_m) = _proj(
        meta_tokens, g_attn, w, wvt, kng, knb, N_META, 0, N_META, False)
    (qa_s, qi_s, qb_s, kbh_s, vb_s, kab_s, kib_s, kb_s, vbb_s, sm_s, vaa_s) = _proj(
        xs, g_attn, w, wvt, kng, knb, dec_batch * ts, past, dec_batch * ts, False)

    oa_p = _dsa(qa_p, qi_p, wt_p, kab_p, kib_p, vt_p, kab_m, kib_m, vaa_m, batch, seq, k_top_p)
    ob_p = _diff(qb_p, kbh_p, vbt_p, kbh_m, vbb_m, lq1, lk1, lq2, lk2, sg, batch, seq)
    y_prompt = _finish(xp, oa_p, ob_p, wo, g_mlp, wu, wd, g_fin, 512).reshape(batch, seq, D_MODEL)

    oa_s, ob_s = _sample(
        qa_s, qi_s, sm_s, kab_s, vaa_s, kib_s, qb_s, kbh_s, vbb_s,
        jnp.swapaxes(cache_a_k[0], 1, 2), jnp.swapaxes(cache_a_v[0], 1, 2),
        jnp.swapaxes(cache_a_idx_k[0], 1, 2),
        jnp.transpose(cache_b_k[0], (0, 2, 3, 4, 1)).reshape(dec_batch, 2 * H_B * DH_B, past),
        cache_b_v[0].reshape(dec_batch, past * H_B, DV_B),
        lq1, lk1, lq2, lk2, sg, dec_batch, ts, past, k_top_s)
    y_sample = _finish(xs, oa_s, ob_s, wo, g_mlp, wu, wd, g_fin, dec_batch * ts).reshape(
        dec_batch, ts, D_MODEL)

    def with_meta_t(meta_rows, main_t):
        meta_b = jnp.broadcast_to(meta_rows.T[None], (batch, 64, N_META))
        return jnp.swapaxes(jnp.concatenate([meta_b, main_t], axis=2), 1, 2)[None]

    new_a_k_p = with_meta_t(sm_m[:, SM_KA:SM_KA + 64], smt_p[:, SM_KA:SM_KA + 64])
    new_a_v_p = with_meta_t(sm_m[:, SM_VA:SM_VA + 64], smt_p[:, SM_VA:SM_VA + 64])
    new_a_i_p = with_meta_t(sm_m[:, SM_KI:SM_KI + 64], smt_p[:, SM_KI:SM_KI + 64])
    kb_meta_t = jnp.broadcast_to(kb_m.T.reshape(1, H_B, 2, DH_B, N_META), (batch, H_B, 2, DH_B, N_META))
    kb_main_t = kbt_p.reshape(batch, H_B, 2, DH_B, seq)
    new_b_k_p = jnp.transpose(
        jnp.concatenate([kb_meta_t, kb_main_t], axis=4), (0, 4, 1, 2, 3))[None]
    new_b_v_p = jnp.concatenate(
        [jnp.broadcast_to(vb_m.reshape(1, N_META, H_B, DV_B), (batch, N_META, H_B, DV_B)),
         vb_p.reshape(batch, seq, H_B, DV_B)], axis=1)[None]
    new_a_k_s = sm_s[:, SM_KA:SM_KA + 64].reshape(1, dec_batch, ts, 64)
    new_a_v_s = sm_s[:, SM_VA:SM_VA + 64].reshape(1, dec_batch, ts, 64)
    new_a_i_s = sm_s[:, SM_KI:SM_KI + 64].reshape(1, dec_batch, ts, 64)
    new_b_k_s = kb_s.reshape(1, dec_batch, ts, H_B, 2, DH_B)
    new_b_v_s = vb_s.reshape(1, dec_batch, ts, H_B, DV_B)
    return (y_prompt, y_sample, new_a_k_p, new_a_v_p, new_a_i_p, new_b_k_p, new_b_v_p,
            new_a_k_s, new_a_v_s, new_a_i_s, new_b_k_s, new_b_v_s)
```

```python
import functools

import jax
import jax.numpy as jnp
import numpy as np
from jax import lax
from jax.experimental import pallas as pl
from jax.experimental.pallas import tpu as pltpu

F32 = jnp.float32
BF16 = jnp.bfloat16
I32 = jnp.int32
I16 = jnp.int16
HALF16 = 32768

D_MODEL = 1024
CHUNK = 64
N_META = 16
H_A = 8
DH_A = 64
H_I = 4
D_IDX = 64
TOPK_MAX = 256
H_B = 4
DH_B = 64
DV_B = 2 * DH_B
D_FF = 4 * D_MODEL
EPS = 1e-6
LAM_INIT = 0.2

LANES = 128
C_QA, C_QB, C_KB, C_VB, C_QI, C_SM = 0, 512, 1024, 1536, 2048, 2304
W_COLS = 2560
SM_KA, SM_VA, SM_KI, SM_WI = 0, 64, 128, 192

LOG2E = float(np.float32(1.4426950408889634))
Q_SCALE = (DH_A ** -0.5) * LOG2E
N_SLOPE_PIECES = 3


def _bf16_pieces(value):
    rest = np.float32(value)
    pieces = []
    for _ in range(N_SLOPE_PIECES):
        piece = np.float32(rest.astype(jnp.bfloat16))
        pieces.append(float(piece))
        rest = np.float32(rest - piece)
    assert rest == 0.0
    return pieces
W_SCALE = (H_I ** -0.5) * (D_IDX ** -0.5)
SLOPES_A = tuple(2.0 ** (-8.0 * (i + 1) / H_A) for i in range(H_A))
SLOPES_B = tuple(2.0 ** (-8.0 * (i + 1) / H_B) for i in range(H_B))

INT_MIN = -(2 ** 31)
NEG_INIT = -1e30
TQ = 256
ONES_ROWS = 16
VT_ROWS = DV_B + ONES_ROWS
VTA_ROWS = DH_A + ONES_ROWS
CNT_ROWS = 32
VMEM_LIMIT = 56 * 1024 * 1024

_NT = (((1,), (1,)), ((), ()))


def _dot(a, b):
    return jnp.dot(a, b, preferred_element_type=F32)


def _dot_nt(a, b):
    return lax.dot_general(a, b, _NT, preferred_element_type=F32)


def _rms(x, g):
    return (x * lax.rsqrt(jnp.mean(x * x, axis=-1, keepdims=True) + EPS)) * g


def _to_key(score):
    b = lax.bitcast_convert_type(score, I32)
    return b ^ ((b >> 31) & jnp.int32(0x7FFFFFFF))


def _f32_from_key(key):
    return lax.bitcast_convert_type(key ^ ((key >> 31) & jnp.int32(0x7FFFFFFF)), F32)


def _bf16_from_key(key):
    bits = key ^ ((key >> 15) & jnp.int32(0x7FFF))
    return lax.bitcast_convert_type(jnp.left_shift(bits, 16), F32).astype(BF16)


KEY16_LOWEST = -32640
KEY32_LOWEST = -2139095040
KEY32_HIGHEST = 2139095039


def _wide(x, width):
    reps = width // LANES
    return x if reps == 1 else jnp.concatenate([x] * reps, axis=1)


def _proj_body(x_ref, g_ref, w_ref, wvt_ref, kng_ref, knb_ref,
               qa_ref, qi_ref, qb_ref, kbh_ref, vb_ref, kab_ref, kib_ref, *mode_refs,
               pos0, period, key_major):
    x = x_ref[...]
    hn = _rms(x, g_ref[...]).astype(BF16)
    rows = x.shape[0]

    def mm(lo, width):
        return _dot(hn, w_ref[:, lo:lo + width])

    lane = lax.broadcasted_iota(I32, (rows, 64), 1)
    r = pl.program_id(0) * rows + lax.broadcasted_iota(I32, (rows, 64), 0)
    pos = pos0 + r % period
    pos_cols = jnp.where(lane >= 2 * N_SLOPE_PIECES, 0,
                         jnp.where(lane % 2 == 0, (pos // 256) * 256, pos % 256))
    pos_cols = pos_cols.astype(F32).astype(BF16)

    def slope_cols(slope):
        cols = jnp.zeros((rows, 64), F32)
        for n, piece in enumerate(_bf16_pieces(slope * LOG2E)):
            cols = jnp.where(lane // 2 == n, piece, cols)
        return cols.astype(BF16)

    z = mm(C_QA, 512) * Q_SCALE
    for h in range(H_A):
        qa_ref[h] = jnp.concatenate(
            [z[:, h * DH_A:(h + 1) * DH_A].astype(BF16), slope_cols(SLOPES_A[h])], axis=1)
    z = mm(C_QB, 512) * Q_SCALE
    for hc in range(2 * H_B):
        qb_ref[hc] = jnp.concatenate(
            [z[:, hc * DH_B:(hc + 1) * DH_B].astype(BF16), slope_cols(SLOPES_B[hc // 2])], axis=1)
    if key_major:
        vt_ref, wt_ref, vbt_ref, smt_ref, kbt_ref = mode_refs
    else:
        kb_ref, vbb_ref, sm_ref, vaa_ref = mode_refs
    z = mm(C_KB, 512)
    if not key_major:
        kb_ref[...] = z
    for hc in range(2 * H_B):
        kbh_ref[hc] = jnp.concatenate([z[:, hc * DH_B:(hc + 1) * DH_B].astype(BF16), pos_cols], axis=1)
    z = mm(C_VB, 512)
    for h in range(H_B):
        vb_ref[pl.ds(h, rows, stride=H_B), :] = z[:, h * DV_B:(h + 1) * DV_B]
    if not key_major:
        vbb_ref[...] = z.astype(BF16)
    z = mm(C_QI, 256)
    for h in range(H_I):
        qi_ref[h] = z[:, h * D_IDX:(h + 1) * D_IDX].astype(BF16)
    z = mm(C_SM, 256)
    ka = z[:, SM_KA:SM_KA + 64]
    va = z[:, SM_VA:SM_VA + 64]
    ki = z[:, SM_KI:SM_KI + 64]
    xc = ki - jnp.mean(ki, axis=-1, keepdims=True)
    ki = xc * lax.rsqrt(jnp.mean(xc * xc, axis=-1, keepdims=True) + EPS) * kng_ref[...] + knb_ref[...]
    kab_ref[...] = jnp.concatenate([ka.astype(BF16), pos_cols], axis=1)
    kib_ref[...] = ki.astype(BF16)
    if not key_major:
        sm_ref[:, 0:128] = z[:, 0:128]
        sm_ref[:, SM_KI:SM_KI + 64] = ki
        sm_ref[:, SM_WI:SM_WI + 64] = z[:, SM_WI:SM_WI + 64] * W_SCALE
        vaa_ref[...] = jnp.concatenate([va.astype(BF16), jnp.ones((rows, 64), BF16)], axis=1)
    else:
        kv_t = z[:, 0:128].T
        for c in range(rows // TQ):
            vt_ref[c] = jnp.concatenate(
                [kv_t[64:128, c * TQ:(c + 1) * TQ].astype(BF16), jnp.ones((ONES_ROWS, TQ), BF16)], axis=0)
        kw_t = jnp.concatenate([ki, z[:, SM_WI:SM_WI + 64] * W_SCALE], axis=1).T
        wt_ref[...] = kw_t[64:72, :]
        smt_ref[0:128, :] = kv_t
        smt_ref[128:192, :] = kw_t[0:64, :]
        kbt_ref[...] = _dot_nt(wvt_ref[0:512, :], hn)
        vb_t = _dot_nt(wvt_ref[512:1024, :], hn)
        for c in range(rows // TQ):
            for h in range(H_B):
                vbt_ref[c, h] = jnp.concatenate(
                    [vb_t[h * DV_B:(h + 1) * DV_B, c * TQ:(c + 1) * TQ].astype(BF16),
                     jnp.ones((VT_ROWS - DV_B, TQ), BF16)], axis=0)


def _proj(x, g, w, wvt, kng, knb, tm, pos0, period, key_major):
    m = x.shape[0]
    assert m % tm == 0
    row = lambda i: (i, 0)
    head = lambda i: (0, i, 0)
    const = lambda i: (0, 0)
    out_shape = (
        jax.ShapeDtypeStruct((H_A, m, 128), BF16),
        jax.ShapeDtypeStruct((H_I, m, D_IDX), BF16),
        jax.ShapeDtypeStruct((2 * H_B, m, 128), BF16),
        jax.ShapeDtypeStruct((2 * H_B, m, 128), BF16),
        jax.ShapeDtypeStruct((m * H_B, DV_B), F32),
        jax.ShapeDtypeStruct((m, 128), BF16),
        jax.ShapeDtypeStruct((m, 64), BF16),
    )
    out_specs = (
        pl.BlockSpec((H_A, tm, 128), head),
        pl.BlockSpec((H_I, tm, D_IDX), head),
        pl.BlockSpec((2 * H_B, tm, 128), head),
        pl.BlockSpec((2 * H_B, tm, 128), head),
        pl.BlockSpec((tm * H_B, DV_B), row),
        pl.BlockSpec((tm, 128), row),
        pl.BlockSpec((tm, 64), row),
    )
    if not key_major:
        out_shape += (
            jax.ShapeDtypeStruct((m, 512), F32),
            jax.ShapeDtypeStruct((m, 512), BF16),
            jax.ShapeDtypeStruct((m, 256), F32),
            jax.ShapeDtypeStruct((m, 128), BF16),
        )
        out_specs += (
            pl.BlockSpec((tm, 512), row),
            pl.BlockSpec((tm, 512), row),
            pl.BlockSpec((tm, 256), row),
            pl.BlockSpec((tm, 128), row),
        )
    else:
        assert tm % TQ == 0 and period % tm == 0 and m % period == 0
        per_stream = period // tm
        stream_cols = lambda i: (i // per_stream, 0, i % per_stream)
        out_shape += (
            jax.ShapeDtypeStruct((m // TQ, VTA_ROWS, TQ), BF16),
            jax.ShapeDtypeStruct((8, m), F32),
            jax.ShapeDtypeStruct((m // TQ, H_B, VT_ROWS, TQ), BF16),
            jax.ShapeDtypeStruct((m // period, 192, period), F32),
            jax.ShapeDtypeStruct((m // period, 512, period), F32),
        )
        out_specs += (
            pl.BlockSpec((tm // TQ, VTA_ROWS, TQ), lambda i: (i, 0, 0)),
            pl.BlockSpec((8, tm), lambda i: (0, i)),
            pl.BlockSpec((tm // TQ, H_B, VT_ROWS, TQ), lambda i: (i, 0, 0, 0)),
            pl.BlockSpec((None, 192, tm), stream_cols),
            pl.BlockSpec((None, 512, tm), stream_cols),
        )
    return pl.pallas_call(
        functools.partial(_proj_body, pos0=pos0, period=period, key_major=key_major),
        out_shape=out_shape,
        grid=(m // tm,),
        in_specs=[
            pl.BlockSpec((tm, D_MODEL), row),
            pl.BlockSpec((1, D_MODEL), const),
            pl.BlockSpec((D_MODEL, W_COLS), const),
            pl.BlockSpec((2 * H_B * DV_B, D_MODEL), const),
            pl.BlockSpec((1, D_IDX), const),
            pl.BlockSpec((1, D_IDX), const),
        ],
        out_specs=out_specs,
        compiler_params=pltpu.CompilerParams(
            dimension_semantics=("arbitrary",), vmem_limit_bytes=VMEM_LIMIT),
        name="proj",
    )(x, g, w, wvt, kng, knb)


def _dsa_body(qa_ref, qi_ref, wt_ref, ka_ref, ki_ref, vt_ref, mka_ref, mki_ref, mva_ref,
              o_ref, smeta_ref, smain_ref, bmeta_ref, bmain_ref,
              tsel_ref, m_ref, al_ref, acc_ref, p_ref, s_ref, mb_ref, seen_ref, *, k_top):
    i = pl.program_id(1)
    krow = lax.broadcasted_iota(I32, (TQ, TQ), 0)
    qcol = lax.broadcasted_iota(I32, (TQ, TQ), 1)

    def main_rows(j):
        return pl.ds(pl.multiple_of(j * TQ, TQ), TQ)

    qi_all = qi_ref[...].reshape(H_I * TQ, D_IDX)

    def scores(ki_blk):
        s4 = _dot_nt(ki_blk, qi_all)
        sc = None
        for h in range(H_I):
            t = jnp.maximum(s4[:, h * TQ:(h + 1) * TQ], 0.0) * wt_ref[h:h + 1, :]
            sc = t if sc is None else sc + t
        return sc

    sc = scores(mki_ref[...])
    smeta_ref[...] = sc
    bmeta_ref[...] = sc.astype(BF16)

    def score_block(j, sc):
        smain_ref[j] = sc
        bmain_ref[j] = sc.astype(BF16)

    def score_full(j, carry):
        score_block(j, scores(ki_ref[main_rows(j), :]))
        return carry

    lax.fori_loop(0, i, score_full, 0)
    score_block(i, jnp.where((krow // CHUNK) <= (qcol // CHUNK),
                             scores(ki_ref[main_rows(i), :]), -jnp.inf))

    def count(meta_ref, main_ref, part, pred):
        def body(j, c):
            return c + part(pred(main_ref[j]))

        c = lax.fori_loop(0, i + 1, body, jnp.zeros((CNT_ROWS, TQ), F32))
        cm = part(pred(meta_ref[...]), rows=N_META)
        return jnp.sum(c, axis=0, keepdims=True) + jnp.sum(cm, axis=0, keepdims=True)

    def part32(hit, rows=CNT_ROWS):
        ones = jnp.where(hit, 1.0, 0.0)
        return jnp.sum(ones.reshape(ones.shape[0] // rows, rows, TQ), axis=0)

    def part16(hit, rows=CNT_ROWS):
        ones = jnp.where(hit, jnp.bfloat16(1), jnp.bfloat16(0))
        acc = ones[0:rows]
        for t in range(1, ones.shape[0] // rows):
            acc = acc + ones[t * rows:(t + 1) * rows]
        return acc.astype(F32)

    kf = float(k_top)

    def step16(s, u):
        cand_u = u | jnp.left_shift(jnp.int32(1), 15 - s)
        cand = _bf16_from_key(jnp.maximum(cand_u, KEY16_LOWEST + HALF16) - HALF16)
        cnt = count(bmeta_ref, bmain_ref, part16, lambda x: x >= cand)
        return jnp.where(cnt >= kf, cand_u, u)

    u_hi = lax.fori_loop(0, 16, step16, jnp.zeros((1, TQ), I32))
    t_hi = _bf16_from_key(jnp.maximum(u_hi, KEY16_LOWEST + HALF16) - HALF16).astype(F32)
    base = jnp.maximum(_to_key(t_hi) - 65536, KEY32_LOWEST)

    def step32(s, d):
        cand_d = d | jnp.left_shift(jnp.int32(1), 16 - s)
        cand = _f32_from_key(jnp.minimum(base + cand_d, KEY32_HIGHEST))
        cnt = count(smeta_ref, smain_ref, part32, lambda x: x >= cand)
        return jnp.where(cnt >= kf, cand_d, d)

    d_lo = lax.fori_loop(0, 17, step32, jnp.zeros((1, TQ), I32))
    t_sel = _f32_from_key(jnp.minimum(base + d_lo, KEY32_HIGHEST))
    need = kf - count(smeta_ref, smain_ref, part32, lambda x: x > t_sel)
    tsel_ref[...] = jnp.broadcast_to(t_sel, tsel_ref.shape)

    m_ref[...] = jnp.full(m_ref.shape, NEG_INIT, F32)
    acc_ref[...] = jnp.zeros(acc_ref.shape, F32)
    thr = tsel_ref[0:1, :]

    def masked_logits(k_blk, sc, slot):
        nk = sc.shape[0]
        tie = sc == thr
        tri = (lax.broadcasted_iota(I32, (nk, nk), 1)
               <= lax.broadcasted_iota(I32, (nk, nk), 0)).astype(BF16)
        rank = _dot(tri, jnp.where(tie, 1.0, 0.0).astype(BF16)) + seen_ref[...]
        seen_ref[...] = rank[nk - 1:nk, :]
        negm = jnp.where(sc > thr, 0.0, jnp.where(tie, jnp.where(rank <= need, 0.0, -jnp.inf), -jnp.inf))
        for h in range(H_A):
            cols = slice(h * TQ, (h + 1) * TQ)
            s = _dot_nt(k_blk, qa_ref[h]) + negm
            s_ref[slot, 0:nk, cols] = s
            mb_ref[slot, :, cols] = jnp.max(s, axis=0, keepdims=True)

    def softmax(slot, nk, future):
        for h in range(H_A):
            cols = slice(h * TQ, (h + 1) * TQ)
            s = s_ref[slot, 0:nk, cols]
            if future is not None:
                s = s - (2.0 * LOG2E * SLOPES_A[h]) * future
            m_old = m_ref[:, cols]
            m_new = jnp.maximum(m_old, mb_ref[slot, :, cols])
            p_ref[slot, 0:nk, cols] = jnp.exp2(s - m_new).astype(BF16)
            al_ref[slot, :, cols] = jnp.exp2(m_old - m_new)
            m_ref[:, cols] = m_new

    def accumulate(vt_blk, slot, nk):
        acc_ref[...] = acc_ref[...] * al_ref[slot] + _dot(vt_blk, p_ref[slot, 0:nk, :])

    eye = (lax.broadcasted_iota(I32, (128, 128), 0)
           == lax.broadcasted_iota(I32, (128, 128), 1)).astype(BF16)
    seen_ref[...] = jnp.zeros(seen_ref.shape, F32)
    masked_logits(mka_ref[...], smeta_ref[...], 0)
    softmax(0, N_META, None)
    accumulate(_dot_nt(eye, mva_ref[...])[0:VTA_ROWS].astype(BF16), 0, N_META)

    p_ref[1] = jnp.zeros(p_ref.shape[1:], BF16)
    al_ref[1] = jnp.ones(al_ref.shape[1:], F32)
    masked_logits(ka_ref[main_rows(0), :], smain_ref[0], 0)

    def full_step(t, slot):
        masked_logits(ka_ref[main_rows(t + 1), :], smain_ref[t + 1], 1 - slot)
        accumulate(vt_ref[jnp.maximum(t - 1, 0)], 1 - slot, TQ)
        softmax(slot, TQ, None)

    def two_steps(u, carry):
        full_step(2 * u, 0)
        full_step(2 * u + 1, 1)
        return carry

    lax.fori_loop(0, i // 2, two_steps, 0)

    def last_step(slot):
        accumulate(vt_ref[jnp.maximum(i - 1, 0)], 1 - slot, TQ)
        softmax(slot, TQ, jnp.maximum(krow - qcol, 0).astype(F32))
        accumulate(vt_ref[i], slot, TQ)

    @pl.when(i % 2 == 1)
    def _odd():
        full_step(i - 1, 0)
        last_step(1)

    @pl.when(i % 2 == 0)
    def _even():
        last_step(0)

    def head_out_t(h):
        a = acc_ref[:, h * TQ:(h + 1) * TQ]
        return a[0:DH_A] / a[DH_A:DH_A + 1]

    for h in range(0, H_A, 2):
        pair_t = jnp.concatenate([head_out_t(h), head_out_t(h + 1)], axis=0)
        o_ref[:, h * DH_A:(h + 2) * DH_A] = pair_t.T.astype(BF16)


def _dsa(qa, qi, wt, kab, kib, vt, mka, mki, mva, batch, seq, k_top):
    nq = seq // TQ
    qrow = lambda b, i: (b * nq + i, 0)
    qhead = lambda b, i: (0, b * nq + i, 0)
    kv = lambda b, i: (b, 0)
    const = lambda b, i: (0, 0)
    return pl.pallas_call(
        functools.partial(_dsa_body, k_top=k_top),
        out_shape=jax.ShapeDtypeStruct((batch * seq, H_A * DH_A), BF16),
        grid=(batch, nq),
        in_specs=[
            pl.BlockSpec((H_A, TQ, 128), qhead),
            pl.BlockSpec((H_I, TQ, D_IDX), qhead),
            pl.BlockSpec((8, TQ), lambda b, i: (0, b * nq + i)),
            pl.BlockSpec((seq, 128), kv),
            pl.BlockSpec((seq, D_IDX), kv),
            pl.BlockSpec((nq, VTA_ROWS, TQ), lambda b, i: (b, 0, 0)),
            pl.BlockSpec((N_META, 128), const),
            pl.BlockSpec((N_META, D_IDX), const),
            pl.BlockSpec((N_META, 128), const),
        ],
        out_specs=pl.BlockSpec((TQ, H_A * DH_A), qrow),
        scratch_shapes=[
            pltpu.VMEM((N_META, TQ), F32),
            pltpu.VMEM((nq, TQ, TQ), F32),
            pltpu.VMEM((N_META, TQ), BF16),
            pltpu.VMEM((nq, TQ, TQ), BF16),
            pltpu.VMEM((8, TQ), F32),
            pltpu.VMEM((1, H_A * TQ), F32),
            pltpu.VMEM((2, 1, H_A * TQ), F32),
            pltpu.VMEM((VTA_ROWS, H_A * TQ), F32),
            pltpu.VMEM((2, TQ, H_A * TQ), BF16),
            pltpu.VMEM((2, TQ, H_A * TQ), F32),
            pltpu.VMEM((2, 1, H_A * TQ), F32),
            pltpu.VMEM((1, TQ), F32),
        ],
        compiler_params=pltpu.CompilerParams(
            dimension_semantics=("arbitrary", "arbitrary"), vmem_limit_bytes=VMEM_LIMIT),
        name="dsa",
    )(qa, qi, wt, kab, kib, vt, mka, mki, mva)


def _lambda(lq1_ref, lk1_ref, lq2_ref, lk2_ref):
    s1 = jnp.sum(lq1_ref[...] * lk1_ref[...], axis=-1, keepdims=True)
    s2 = jnp.sum(lq2_ref[...] * lk2_ref[...], axis=-1, keepdims=True)
    return jnp.exp(s1) - jnp.exp(s2) + LAM_INIT


def _diff_body(qb_ref, kb_ref, vt_ref, mkb_ref, mvb_ref, lq1_ref, lk1_ref, lq2_ref, lk2_ref,
               sg_ref, o_ref, m_ref, al_ref, acc_ref, p_ref, s_ref, mb_ref):
    i = pl.program_id(1)
    n_hc = 2 * H_B
    krow = lax.broadcasted_iota(I32, (TQ, TQ), 0)
    qcol = lax.broadcasted_iota(I32, (TQ, TQ), 1)
    m_ref[...] = jnp.full(m_ref.shape, NEG_INIT, F32)
    acc_ref[...] = jnp.zeros(acc_ref.shape, F32)

    def main_rows(j):
        return pl.ds(pl.multiple_of(j * TQ, TQ), TQ)

    def logits(k_of, slot, nk):
        for hc in range(n_hc):
            s = _dot_nt(k_of(hc), qb_ref[hc])
            s_ref[slot, hc, 0:nk, :] = s
            mb_ref[slot, :, hc * TQ:(hc + 1) * TQ] = jnp.max(s, axis=0, keepdims=True)

    def softmax(slot, nk, negm, future):
        for hc in range(n_hc):
            cols = slice(hc * TQ, (hc + 1) * TQ)
            s = s_ref[slot, hc, 0:nk, :]
            m_old = m_ref[:, cols]
            if negm is None:
                m_new = jnp.maximum(m_old, mb_ref[slot, :, cols])
            else:
                s = (s + negm) - (2.0 * LOG2E * SLOPES_B[hc // 2]) * future
                m_new = jnp.maximum(m_old, jnp.max(s, axis=0, keepdims=True))
            p_ref[slot, 0:nk, cols] = jnp.exp2(s - m_new).astype(BF16)
            al_ref[slot, :, cols] = jnp.exp2(m_old - m_new)
            m_ref[:, cols] = m_new

    def accumulate(vt_of, slot, nk):
        for h in range(H_B):
            cols = slice(2 * h * TQ, (2 * h + 2) * TQ)
            acc_ref[h] = acc_ref[h] * al_ref[slot, :, cols] + _dot(vt_of(h), p_ref[slot, 0:nk, cols])

    eye = (lax.broadcasted_iota(I32, (DV_B, DV_B), 0)
           == lax.broadcasted_iota(I32, (DV_B, DV_B), 1)).astype(BF16)

    def meta_vt(h):
        v_t = _dot_nt(eye, mvb_ref[:, h * DV_B:(h + 1) * DV_B]).astype(BF16)
        return jnp.concatenate([v_t, jnp.ones((VT_ROWS - DV_B, N_META), BF16)], axis=0)

    logits(lambda hc: mkb_ref[hc], 0, N_META)
    softmax(0, N_META, None, None)
    accumulate(meta_vt, 0, N_META)

    p_ref[1] = jnp.zeros(p_ref.shape[1:], BF16)
    al_ref[1] = jnp.ones(al_ref.shape[1:], F32)
    logits(lambda hc: kb_ref[hc, main_rows(0), :], 0, TQ)

    def main_vt(t):
        return lambda h: vt_ref[t, h]

    def full_step(t, slot):
        logits(lambda hc: kb_ref[hc, main_rows(t + 1), :], 1 - slot, TQ)
        accumulate(main_vt(jnp.maximum(t - 1, 0)), 1 - slot, TQ)
        softmax(slot, TQ, None, None)

    def two_steps(u, carry):
        full_step(2 * u, 0)
        full_step(2 * u + 1, 1)
        return carry

    lax.fori_loop(0, i // 2, two_steps, 0)

    def last_step(slot):
        accumulate(main_vt(jnp.maximum(i - 1, 0)), 1 - slot, TQ)
        softmax(slot, TQ, jnp.where((krow // CHUNK) <= (qcol // CHUNK), 0.0, -jnp.inf),
                jnp.maximum(krow - qcol, 0).astype(F32))
        accumulate(main_vt(i), slot, TQ)

    @pl.when(i % 2 == 1)
    def _odd():
        full_step(i - 1, 0)
        last_step(1)

    @pl.when(i % 2 == 0)
    def _even():
        last_step(0)

    lam = _lambda(lq1_ref, lk1_ref, lq2_ref, lk2_ref)
    for h in range(H_B):
        a = acc_ref[h]
        o0 = a[0:DV_B, 0:TQ] / a[DV_B:DV_B + 1, 0:TQ]
        o1 = a[0:DV_B, TQ:2 * TQ] / a[DV_B:DV_B + 1, TQ:2 * TQ]
        o = _rms((o0 - lam * o1).T, sg_ref[...]) * (1.0 - LAM_INIT)
        o_ref[:, h * DV_B:(h + 1) * DV_B] = o.astype(BF16)


def _diff(qb, kbh, vbt, mkb, mvb, lq1, lk1, lq2, lk2, sg, batch, seq):
    nq = seq // TQ
    qrow = lambda b, i: (b * nq + i, 0)
    qhead = lambda b, i: (0, b * nq + i, 0)
    const2 = lambda b, i: (0, 0)
    const3 = lambda b, i: (0, 0, 0)
    return pl.pallas_call(
        _diff_body,
        out_shape=jax.ShapeDtypeStruct((batch * seq, H_B * DV_B), BF16),
        grid=(batch, nq),
        in_specs=[
            pl.BlockSpec((2 * H_B, TQ, 128), qhead),
            pl.BlockSpec((2 * H_B, seq, 128), lambda b, i: (0, b, 0)),
            pl.BlockSpec((nq, H_B, VT_ROWS, TQ), lambda b, i: (b, 0, 0, 0)),
            pl.BlockSpec((2 * H_B, N_META, 128), const3),
            pl.BlockSpec((N_META, H_B * DV_B), const2),
            pl.BlockSpec((1, DH_B), const2),
            pl.BlockSpec((1, DH_B), const2),
            pl.BlockSpec((1, DH_B), const2),
            pl.BlockSpec((1, DH_B), const2),
            pl.BlockSpec((1, DV_B), const2),
        ],
        out_specs=pl.BlockSpec((TQ, H_B * DV_B), qrow),
        scratch_shapes=[
            pltpu.VMEM((1, 2 * H_B * TQ), F32),
            pltpu.VMEM((2, 1, 2 * H_B * TQ), F32),
            pltpu.VMEM((H_B, VT_ROWS, 2 * TQ), F32),
            pltpu.VMEM((2, TQ, 2 * H_B * TQ), BF16),
            pltpu.VMEM((2, 2 * H_B, TQ, TQ), F32),
            pltpu.VMEM((2, 1, 2 * H_B * TQ), F32),
        ],
        compiler_params=pltpu.CompilerParams(
            dimension_semantics=("arbitrary", "arbitrary"), vmem_limit_bytes=VMEM_LIMIT),
        name="diff",
    )(qb, kbh, vbt, mkb, mvb, lq1, lk1, lq2, lk2, sg)


def _sample_body(qa_ref, qi_ref, sm_ref, nka_ref, nva_ref, nki_ref, qb_ref, nkb_ref, nvb_ref,
                 cak_ref, cav_ref, cai_ref, cbk_ref, cbv_ref,
                 lq1_ref, lk1_ref, lq2_ref, lk2_ref, sg_ref,
                 oa_ref, ob_ref, pc_ref, pn_ref, *, k_top, past, ts):
    pad = LANES - ts
    row_c = lax.broadcasted_iota(I32, (ts, past), 0)
    col_c = lax.broadcasted_iota(I32, (ts, past), 1)
    row_n = lax.broadcasted_iota(I32, (ts, LANES), 0)
    col_n = lax.broadcasted_iota(I32, (ts, LANES), 1)
    new_ok = col_n < ts

    def pad_rows(x):
        return jnp.concatenate([x, jnp.zeros((pad,) + x.shape[1:], x.dtype)], axis=0)

    qi_all = qi_ref[...].reshape(H_I * ts, D_IDX)
    w = sm_ref[:, SM_WI:SM_WI + H_I]
    s4c = _dot(qi_all, cai_ref[...].astype(BF16))
    s4n = _dot_nt(qi_all, pad_rows(nki_ref[...]))
    sc_c = None
    sc_n = None
    for h in range(H_I):
        wh = w[:, h:h + 1]
        tc = jnp.maximum(s4c[h * ts:(h + 1) * ts], 0.0) * wh
        tn = jnp.maximum(s4n[h * ts:(h + 1) * ts], 0.0) * wh
        sc_c = tc if sc_c is None else sc_c + tc
        sc_n = tn if sc_n is None else sc_n + tn
    sc_n = jnp.where(new_ok, sc_n, -jnp.inf)

    def count(pred_c, pred_n):
        c = jnp.sum(jnp.where(pred_c, 1.0, 0.0), axis=1, keepdims=True)
        return c + jnp.sum(jnp.where(pred_n, 1.0, 0.0), axis=1, keepdims=True)

    def threshold(key):
        return _f32_from_key(jnp.clip(key, KEY32_LOWEST, KEY32_HIGHEST))

    def value_step(s, key):
        cand_key = key ^ jnp.left_shift(jnp.int32(1), 31 - s)
        cand = threshold(cand_key)
        cnt = count(sc_c >= cand, sc_n >= cand)
        return jnp.where(cnt >= float(k_top), cand_key, key)

    t = threshold(lax.fori_loop(0, 32, value_step, jnp.full((ts, 1), INT_MIN, I32)))
    need = float(k_top) - count(sc_c > t, sc_n > t)
    idx_n = col_n + past

    def tie_step(s, x):
        cand = x | jnp.left_shift(jnp.int32(1), 11 - s)
        cnt = count((sc_c == t) & (col_c < cand), (sc_n == t) & (idx_n < cand))
        return jnp.where(cnt < need, cand, x)

    x = lax.fori_loop(0, 12, tie_step, jnp.zeros((ts, 1), I32))
    negm_c = jnp.where((sc_c > t) | ((sc_c == t) & (col_c <= x)), 0.0, -jnp.inf)
    negm_n = jnp.where((sc_n > t) | ((sc_n == t) & (idx_n <= x)), 0.0, -jnp.inf)

    dist_c = (row_c - col_c + past).astype(F32)
    dist_n = jnp.abs(row_n - col_n).astype(F32)
    q_all = qa_ref[...][:, :, 0:DH_A].reshape(H_A * ts, DH_A)
    s_c = _dot(q_all, cak_ref[...].astype(BF16))
    s_n = _dot_nt(q_all, pad_rows(nka_ref[...][:, 0:DH_A]))
    for h in range(H_A):
        rows = slice(h * ts, (h + 1) * ts)
        lc = s_c[rows] + (negm_c - (LOG2E * SLOPES_A[h]) * dist_c)
        ln = s_n[rows] + (negm_n - (LOG2E * SLOPES_A[h]) * dist_n)
        m = jnp.maximum(jnp.max(lc, axis=1, keepdims=True), jnp.max(ln, axis=1, keepdims=True))
        m = jnp.maximum(m, NEG_INIT)
        pc_ref[rows, :] = jnp.exp2(lc - m).astype(BF16)
        pn_ref[rows, :] = jnp.exp2(ln - m).astype(BF16)
    vc_aug_t = jnp.concatenate([cav_ref[...].astype(BF16), jnp.ones((64, past), BF16)], axis=0)
    pv = _dot_nt(pc_ref[...], vc_aug_t) + _dot(pn_ref[...], pad_rows(nva_ref[...]))
    pv = pv / pltpu.roll(pv, DH_A, axis=1)
    oa_ref[...] = jnp.concatenate(
        [pv[h * ts:(h + 1) * ts, :DH_A] for h in range(H_A)], axis=1).astype(BF16)

    lam = _lambda(lq1_ref, lk1_ref, lq2_ref, lk2_ref)
    neg_new = jnp.where(new_ok, 0.0, -jnp.inf)
    for h in range(H_B):
        a_c = None
        a_n = None
        for c in range(2):
            hc = 2 * h + c
            q = qb_ref[hc][:, 0:DH_B]
            kc_t = cbk_ref[hc * DH_B:(hc + 1) * DH_B, :].astype(BF16)
            lc = _dot(q, kc_t) - (LOG2E * SLOPES_B[h]) * dist_c
            ln = _dot_nt(q, pad_rows(nkb_ref[hc][:, 0:DH_B])) + (neg_new - (LOG2E * SLOPES_B[h]) * dist_n)
            m = jnp.maximum(jnp.max(lc, axis=1, keepdims=True), jnp.max(ln, axis=1, keepdims=True))
            pc = jnp.exp2(lc - m)
            pn = jnp.exp2(ln - m)
            l = jnp.sum(pc, axis=1, keepdims=True) + jnp.sum(pn, axis=1, keepdims=True)
            pc = pc / l
            pn = pn / l
            if c == 0:
                a_c, a_n = pc, pn
            else:
                a_c, a_n = a_c - lam * pc, a_n - lam * pn
        vc = cbv_ref[pl.ds(h, past, stride=H_B), :].astype(BF16)
        vn = pad_rows(nvb_ref[:, h * DV_B:(h + 1) * DV_B])
        o = _dot(a_c.astype(BF16), vc) + _dot(a_n.astype(BF16), vn)
        o = _rms(o, sg_ref[...]) * (1.0 - LAM_INIT)
        ob_ref[:, h * DV_B:(h + 1) * DV_B] = o.astype(BF16)


def _sample(qa, qi, sm, nka, nva, nki, qb, nkb, nvb, cak, cav, cai, cbk, cbv,
            lq1, lk1, lq2, lk2, sg, batch, ts, past, k_top):
    row = lambda b: (b, 0)
    head = lambda b: (0, b, 0)
    cache = lambda b: (b, 0, 0)
    const = lambda b: (0, 0)
    return pl.pallas_call(
        functools.partial(_sample_body, k_top=k_top, past=past, ts=ts),
        out_shape=(jax.ShapeDtypeStruct((batch * ts, H_A * DH_A), BF16),
                   jax.ShapeDtypeStruct((batch * ts, H_B * DV_B), BF16)),
        grid=(batch,),
        in_specs=[
            pl.BlockSpec((H_A, ts, 128), head),
            pl.BlockSpec((H_I, ts, D_IDX), head),
            pl.BlockSpec((ts, 256), row),
            pl.BlockSpec((ts, 128), row),
            pl.BlockSpec((ts, 128), row),
            pl.BlockSpec((ts, D_IDX), row),
            pl.BlockSpec((2 * H_B, ts, 128), head),
            pl.BlockSpec((2 * H_B, ts, 128), head),
            pl.BlockSpec((ts, H_B * DV_B), row),
            pl.BlockSpec((None, DH_A, past), cache),
            pl.BlockSpec((None, DH_A, past), cache),
            pl.BlockSpec((None, D_IDX, past), cache),
            pl.BlockSpec((None, 2 * H_B * DH_B, past), cache),
            pl.BlockSpec((None, past * H_B, DV_B), cache),
            pl.BlockSpec((1, DH_B), const),
            pl.BlockSpec((1, DH_B), const),
            pl.BlockSpec((1, DH_B), const),
            pl.BlockSpec((1, DH_B), const),
            pl.BlockSpec((1, DV_B), const),
        ],
        out_specs=(pl.BlockSpec((ts, H_A * DH_A), row), pl.BlockSpec((ts, H_B * DV_B), row)),
        scratch_shapes=[
            pltpu.VMEM((H_A * ts, past), BF16),
            pltpu.VMEM((H_A * ts, LANES), BF16),
        ],
        compiler_params=pltpu.CompilerParams(
            dimension_semantics=("arbitrary",), vmem_limit_bytes=VMEM_LIMIT),
        name="sample",
    )(qa, qi, sm, nka, nva, nki, qb, nkb, nvb, cak, cav, cai, cbk, cbv, lq1, lk1, lq2, lk2, sg)


def _finish_body(x_ref, oa_ref, ob_ref, wo_ref, gm_ref, wu_ref, wd_ref, gf_ref, y_ref):
    o = jnp.concatenate([oa_ref[...], ob_ref[...]], axis=1)
    h1 = x_ref[...] + _dot(o, wo_ref[...])
    hn = _rms(h1, gm_ref[...]).astype(BF16)
    acc = h1
    for c in range(D_FF // D_MODEL):
        u = jnp.maximum(_dot(hn, wu_ref[:, c * D_MODEL:(c + 1) * D_MODEL]), 0.0)
        acc = acc + _dot((u * u).astype(BF16), wd_ref[c * D_MODEL:(c + 1) * D_MODEL, :])
    y_ref[...] = _rms(acc, gf_ref[...])


def _finish(x, oa, ob, wo, gm, wu, wd, gf, tm):
    m = x.shape[0]
    assert m % tm == 0
    row = lambda i: (i, 0)
    const = lambda i: (0, 0)
    resident = dict(pipeline_mode=pl.Buffered(1))
    return pl.pallas_call(
        _finish_body,
        out_shape=jax.ShapeDtypeStruct((m, D_MODEL), F32),
        grid=(m // tm,),
        in_specs=[
            pl.BlockSpec((tm, D_MODEL), row),
            pl.BlockSpec((tm, 512), row),
            pl.BlockSpec((tm, 512), row),
            pl.BlockSpec((D_MODEL, D_MODEL), const, **resident),
            pl.BlockSpec((1, D_MODEL), const),
            pl.BlockSpec((D_MODEL, D_FF), const, **resident),
            pl.BlockSpec((D_FF, D_MODEL), const, **resident),
            pl.BlockSpec((1, D_MODEL), const),
        ],
        out_specs=pl.BlockSpec((tm, D_MODEL), row),
        compiler_params=pltpu.CompilerParams(
            dimension_semantics=("arbitrary",), vmem_limit_bytes=VMEM_LIMIT),
        name="finish",
    )(x, oa, ob, wo, gm, wu, wd, gf)


def kernel(x_prompt, x_sample, cache_a_k, cache_a_v, cache_a_idx_k, cache_b_k, cache_b_v,
           meta_tokens, attn_norm_g, w_in, idx_k_norm_g, idx_k_norm_b,
           lambda_q1, lambda_k1, lambda_q2, lambda_k2, subln_g, w_o,
           mlp_norm_g, w_up, w_down, final_norm_g):
    batch, seq, _ = x_prompt.shape
    dec_batch, ts, _ = x_sample.shape
    past = cache_a_k.shape[2]
    assert attn_norm_g.shape[0] == 1, "single-layer step"
    assert seq % TQ == 0 and ts == 16 and past % LANES == 0 and past // CHUNK == (past + ts - 1) // CHUNK
    n = N_META + seq
    k_top_p = min(TOPK_MAX, seq // 4)
    k_top_s = min(TOPK_MAX, (past + ts) // 4)

    w_t = w_in[0].T.astype(BF16)
    w = jnp.concatenate(
        [w_t[0:512],
         w_t[964:1476],
         w_t[1476:1988],
         w_t[1988:2500],
         w_t[640:896],
         w_t[512:640],
         w_t[896:964]],
        axis=0)
    w = jnp.pad(w, ((0, W_COLS - 2500), (0, 0))).T
    wvt = w_t[1476:2500]
    wo = w_o[0].astype(BF16)
    wu = w_up[0].astype(BF16)
    wd = w_down[0].astype(BF16)
    g_attn = attn_norm_g[0][None]
    g_mlp = mlp_norm_g[0][None]
    g_fin = final_norm_g[None]
    kng = idx_k_norm_g[0][None]
    knb = idx_k_norm_b[0][None]
    lq1, lk1, lq2, lk2 = lambda_q1[0][None], lambda_k1[0][None], lambda_q2[0][None], lambda_k2[0][None]
    sg = subln_g[0][None]

    xp = x_prompt.reshape(batch * seq, D_MODEL)
    xs = x_sample.reshape(dec_batch * ts, D_MODEL)

    (qa_p, qi_p, qb_p, kbh_p, vb_p, kab_p, kib_p, vt_p, wt_p, vbt_p, smt_p, kbt_p) = _proj(
        xp, g_attn, w, wvt, kng, knb, 512, N_META, seq, True)
    (_, _, _, kbh_m, vb_m, kab_m, kib_m, kb_m, vbb_m, sm_m, vaa_m) = _proj(
        meta_tokens, g_attn, w, wvt, kng, knb, N_META, 0, N_META, False)
    (qa_s, qi_s, qb_s, kbh_s, vb_s, kab_s, kib_s, kb_s, vbb_s, sm_s, vaa_s) = _proj(
        xs, g_attn, w, wvt, kng, knb, dec_batch * ts, past, dec_batch * ts, False)

    oa_p = _dsa(qa_p, qi_p, wt_p, kab_p, kib_p, vt_p, kab_m, kib_m, vaa_m, batch, seq, k_top_p)
    ob_p = _diff(qb_p, kbh_p, vbt_p, kbh_m, vbb_m, lq1, lk1, lq2, lk2, sg, batch, seq)
    y_prompt = _finish(xp, oa_p, ob_p, wo, g_mlp, wu, wd, g_fin, 512).reshape(batch, seq, D_MODEL)

    oa_s, ob_s = _sample(
        qa_s, qi_s, sm_s, kab_s, vaa_s, kib_s, qb_s, kbh_s, vbb_s,
        jnp.swapaxes(cache_a_k[0], 1, 2), jnp.swapaxes(cache_a_v[0], 1, 2),
        jnp.swapaxes(cache_a_idx_k[0], 1, 2),
        jnp.transpose(cache_b_k[0], (0, 2, 3, 4, 1)).reshape(dec_batch, 2 * H_B * DH_B, past),
        cache_b_v[0].reshape(dec_batch, past * H_B, DV_B),
        lq1, lk1, lq2, lk2, sg, dec_batch, ts, past, k_top_s)
    y_sample = _finish(xs, oa_s, ob_s, wo, g_mlp, wu, wd, g_fin, dec_batch * ts).reshape(
        dec_batch, ts, D_MODEL)

    def with_meta_t(meta_rows, main_t):
        meta_b = jnp.broadcast_to(meta_rows.T[None], (batch, 64, N_META))
        return jnp.swapaxes(jnp.concatenate([meta_b, main_t], axis=2), 1, 2)[None]

    new_a_k_p = with_meta_t(sm_m[:, SM_KA:SM_KA + 64], smt_p[:, SM_KA:SM_KA + 64])
    new_a_v_p = with_meta_t(sm_m[:, SM_VA:SM_VA + 64], smt_p[:, SM_VA:SM_VA + 64])
    new_a_i_p = with_meta_t(sm_m[:, SM_KI:SM_KI + 64], smt_p[:, SM_KI:SM_KI + 64])
    kb_meta_t = jnp.broadcast_to(kb_m.T.reshape(1, H_B, 2, DH_B, N_META), (batch, H_B, 2, DH_B, N_META))
    kb_main_t = kbt_p.reshape(batch, H_B, 2, DH_B, seq)
    new_b_k_p = jnp.transpose(
        jnp.concatenate([kb_meta_t, kb_main_t], axis=4), (0, 4, 1, 2, 3))[None]
    new_b_v_p = jnp.concatenate(
        [jnp.broadcast_to(vb_m.reshape(1, N_META, H_B, DV_B), (batch, N_META, H_B, DV_B)),
         vb_p.reshape(batch, seq, H_B, DV_B)], axis=1)[None]
    new_a_k_s = sm_s[:, SM_KA:SM_KA + 64].reshape(1, dec_batch, ts, 64)
    new_a_v_s = sm_s[:, SM_VA:SM_VA + 64].reshape(1, dec_batch, ts, 64)
    new_a_i_s = sm_s[:, SM_KI:SM_KI + 64].reshape(1, dec_batch, ts, 64)
    new_b_k_s = kb_s.reshape(1, dec_batch, ts, H_B, 2, DH_B)
    new_b_v_s = vb_s.reshape(1, dec_batch, ts, H_B, DV_B)
    return (y_prompt, y_sample, new_a_k_p, new_a_v_p, new_a_i_p, new_b_k_p, new_b_v_p,
            new_a_k_s, new_a_v_s, new_a_i_s, new_b_k_s, new_b_v_s)
```

```python
import functools

import jax
import jax.numpy as jnp
import numpy as np
from jax import lax
from jax.experimental import pallas as pl
from jax.experimental.pallas import tpu as pltpu

F32 = jnp.float32
BF16 = jnp.bfloat16
I32 = jnp.int32
I16 = jnp.int16
HALF16 = 32768

D_MODEL = 1024
CHUNK = 64
N_META = 16
H_A = 8
DH_A = 64
H_I = 4
D_IDX = 64
TOPK_MAX = 256
H_B = 4
DH_B = 64
DV_B = 2 * DH_B
D_FF = 4 * D_MODEL
EPS = 1e-6
LAM_INIT = 0.2

LANES = 128
C_QA, C_QB, C_KB, C_VB, C_QI, C_SM = 0, 512, 1024, 1536, 2048, 2304
W_COLS = 2560
SM_KA, SM_VA, SM_KI, SM_WI = 0, 64, 128, 192

LOG2E = float(np.float32(1.4426950408889634))
Q_SCALE = (DH_A ** -0.5) * LOG2E
N_SLOPE_PIECES = 3


def _bf16_pieces(value):
    rest = np.float32(value)
    pieces = []
    for _ in range(N_SLOPE_PIECES):
        piece = np.float32(rest.astype(jnp.bfloat16))
        pieces.append(float(piece))
        rest = np.float32(rest - piece)
    assert rest == 0.0
    return pieces
W_SCALE = (H_I ** -0.5) * (D_IDX ** -0.5)
SLOPES_A = tuple(2.0 ** (-8.0 * (i + 1) / H_A) for i in range(H_A))
SLOPES_B = tuple(2.0 ** (-8.0 * (i + 1) / H_B) for i in range(H_B))

INT_MIN = -(2 ** 31)
NEG_INIT = -1e30
TQ = 256
ONES_ROWS = 16
VT_ROWS = DV_B + ONES_ROWS
VTA_ROWS = DH_A + ONES_ROWS
CNT_ROWS = 32
VMEM_LIMIT = 56 * 1024 * 1024

_NT = (((1,), (1,)), ((), ()))


def _dot(a, b):
    return jnp.dot(a, b, preferred_element_type=F32)


def _dot_nt(a, b):
    return lax.dot_general(a, b, _NT, preferred_element_type=F32)


def _rms(x, g):
    return (x * lax.rsqrt(jnp.mean(x * x, axis=-1, keepdims=True) + EPS)) * g


def _to_key(score):
    b = lax.bitcast_convert_type(score, I32)
    return b ^ ((b >> 31) & jnp.int32(0x7FFFFFFF))


def _f32_from_key(key):
    return lax.bitcast_convert_type(key ^ ((key >> 31) & jnp.int32(0x7FFFFFFF)), F32)


def _bf16_from_key(key):
    bits = key ^ ((key >> 15) & jnp.int32(0x7FFF))
    return lax.bitcast_convert_type(jnp.left_shift(bits, 16), F32).astype(BF16)


KEY16_LOWEST = -32640
KEY32_LOWEST = -2139095040
KEY32_HIGHEST = 2139095039


def _wide(x, width):
    reps = width // LANES
    return x if reps == 1 else jnp.concatenate([x] * reps, axis=1)


def _proj_body(x_ref, g_ref, w_ref, wvt_ref, kng_ref, knb_ref,
               qa_ref, qi_ref, qb_ref, kbh_ref, vb_ref, kab_ref, kib_ref, *mode_refs,
               pos0, period, key_major):
    x = x_ref[...]
    hn = _rms(x, g_ref[...]).astype(BF16)
    rows = x.shape[0]

    def mm(lo, width):
        return _dot(hn, w_ref[:, lo:lo + width])

    lane = lax.broadcasted_iota(I32, (rows, 64), 1)
    r = pl.program_id(0) * rows + lax.broadcasted_iota(I32, (rows, 64), 0)
    pos = pos0 + r % period
    pos_cols = jnp.where(lane >= 2 * N_SLOPE_PIECES, 0,
                         jnp.where(lane % 2 == 0, (pos // 256) * 256, pos % 256))
    pos_cols = pos_cols.astype(F32).astype(BF16)

    def slope_cols(slope):
        cols = jnp.zeros((rows, 64), F32)
        for n, piece in enumerate(_bf16_pieces(slope * LOG2E)):
            cols = jnp.where(lane // 2 == n, piece, cols)
        return cols.astype(BF16)

    z = mm(C_QA, 512) * Q_SCALE
    for h in range(H_A):
        qa_ref[h] = jnp.concatenate(
            [z[:, h * DH_A:(h + 1) * DH_A].astype(BF16), slope_cols(SLOPES_A[h])], axis=1)
    z = mm(C_QB, 512) * Q_SCALE
    for hc in range(2 * H_B):
        qb_ref[hc] = jnp.concatenate(
            [z[:, hc * DH_B:(hc + 1) * DH_B].astype(BF16), slope_cols(SLOPES_B[hc // 2])], axis=1)
    if key_major:
        vt_ref, wt_ref, vbt_ref, smt_ref, kbt_ref = mode_refs
    else:
        kb_ref, vbb_ref, sm_ref, vaa_ref = mode_refs
    z = mm(C_KB, 512)
    if not key_major:
        kb_ref[...] = z
    for hc in range(2 * H_B):
        kbh_ref[hc] = jnp.concatenate([z[:, hc * DH_B:(hc + 1) * DH_B].astype(BF16), pos_cols], axis=1)
    z = mm(C_VB, 512)
    for h in range(H_B):
        vb_ref[pl.ds(h, rows, stride=H_B), :] = z[:, h * DV_B:(h + 1) * DV_B]
    if not key_major:
        vbb_ref[...] = z.astype(BF16)
    z = mm(C_QI, 256)
    for h in range(H_I):
        qi_ref[h] = z[:, h * D_IDX:(h + 1) * D_IDX].astype(BF16)
    z = mm(C_SM, 256)
    ka = z[:, SM_KA:SM_KA + 64]
    va = z[:, SM_VA:SM_VA + 64]
    ki = z[:, SM_KI:SM_KI + 64]
    xc = ki - jnp.mean(ki, axis=-1, keepdims=True)
    ki = xc * lax.rsqrt(jnp.mean(xc * xc, axis=-1, keepdims=True) + EPS) * kng_ref[...] + knb_ref[...]
    kab_ref[...] = jnp.concatenate([ka.astype(BF16), pos_cols], axis=1)
    kib_ref[...] = ki.astype(BF16)
    if not key_major:
        sm_ref[:, 0:128] = z[:, 0:128]
        sm_ref[:, SM_KI:SM_KI + 64] = ki
        sm_ref[:, SM_WI:SM_WI + 64] = z[:, SM_WI:SM_WI + 64] * W_SCALE
        vaa_ref[...] = jnp.concatenate([va.astype(BF16), jnp.ones((rows, 64), BF16)], axis=1)
    else:
        kv_t = z[:, 0:128].T
        for c in range(rows // TQ):
            vt_ref[c] = jnp.concatenate(
                [kv_t[64:128, c * TQ:(c + 1) * TQ].astype(BF16), jnp.ones((ONES_ROWS, TQ), BF16)], axis=0)
        kw_t = jnp.concatenate([ki, z[:, SM_WI:SM_WI + 64] * W_SCALE], axis=1).T
        wt_ref[...] = kw_t[64:72, :]
        smt_ref[0:128, :] = kv_t
        smt_ref[128:192, :] = kw_t[0:64, :]
        kbt_ref[...] = _dot_nt(wvt_ref[0:512, :], hn)
        vb_t = _dot_nt(wvt_ref[512:1024, :], hn)
        for c in range(rows // TQ):
            for h in range(H_B):
                vbt_ref[c, h] = jnp.concatenate(
                    [vb_t[h * DV_B:(h + 1) * DV_B, c * TQ:(c + 1) * TQ].astype(BF16),
                     jnp.ones((VT_ROWS - DV_B, TQ), BF16)], axis=0)


def _proj(x, g, w, wvt, kng, knb, tm, pos0, period, key_major):
    m = x.shape[0]
    assert m % tm == 0
    row = lambda i: (i, 0)
    head = lambda i: (0, i, 0)
    const = lambda i: (0, 0)
    out_shape = (
        jax.ShapeDtypeStruct((H_A, m, 128), BF16),
        jax.ShapeDtypeStruct((H_I, m, D_IDX), BF16),
        jax.ShapeDtypeStruct((2 * H_B, m, 128), BF16),
        jax.ShapeDtypeStruct((2 * H_B, m, 128), BF16),
        jax.ShapeDtypeStruct((m * H_B, DV_B), F32),
        jax.ShapeDtypeStruct((m, 128), BF16),
        jax.ShapeDtypeStruct((m, 64), BF16),
    )
    out_specs = (
        pl.BlockSpec((H_A, tm, 128), head),
        pl.BlockSpec((H_I, tm, D_IDX), head),
        pl.BlockSpec((2 * H_B, tm, 128), head),
        pl.BlockSpec((2 * H_B, tm, 128), head),
        pl.BlockSpec((tm * H_B, DV_B), row),
        pl.BlockSpec((tm, 128), row),
        pl.BlockSpec((tm, 64), row),
    )
    if not key_major:
        out_shape += (
            jax.ShapeDtypeStruct((m, 512), F32),
            jax.ShapeDtypeStruct((m, 512), BF16),
            jax.ShapeDtypeStruct((m, 256), F32),
            jax.ShapeDtypeStruct((m, 128), BF16),
        )
        out_specs += (
            pl.BlockSpec((tm, 512), row),
            pl.BlockSpec((tm, 512), row),
            pl.BlockSpec((tm, 256), row),
            pl.BlockSpec((tm, 128), row),
        )
    else:
        assert tm % TQ == 0 and period % tm == 0 and m % period == 0
        per_stream = period // tm
        stream_cols = lambda i: (i // per_stream, 0, i % per_stream)
        out_shape += (
            jax.ShapeDtypeStruct((m // TQ, VTA_ROWS, TQ), BF16),
            jax.ShapeDtypeStruct((8, m), F32),
            jax.ShapeDtypeStruct((m // TQ, H_B, VT_ROWS, TQ), BF16),
            jax.ShapeDtypeStruct((m // period, 192, period), F32),
            jax.ShapeDtypeStruct((m // period, 512, period), F32),
        )
        out_specs += (
            pl.BlockSpec((tm // TQ, VTA_ROWS, TQ), lambda i: (i, 0, 0)),
            pl.BlockSpec((8, tm), lambda i: (0, i)),
            pl.BlockSpec((tm // TQ, H_B, VT_ROWS, TQ), lambda i: (i, 0, 0, 0)),
            pl.BlockSpec((None, 192, tm), stream_cols),
            pl.BlockSpec((None, 512, tm), stream_cols),
        )
    return pl.pallas_call(
        functools.partial(_proj_body, pos0=pos0, period=period, key_major=key_major),
        out_shape=out_shape,
        grid=(m // tm,),
        in_specs=[
            pl.BlockSpec((tm, D_MODEL), row),
            pl.BlockSpec((1, D_MODEL), const),
            pl.BlockSpec((D_MODEL, W_COLS), const),
            pl.BlockSpec((2 * H_B * DV_B, D_MODEL), const),
            pl.BlockSpec((1, D_IDX), const),
            pl.BlockSpec((1, D_IDX), const),
        ],
        out_specs=out_specs,
        compiler_params=pltpu.CompilerParams(
            dimension_semantics=("arbitrary",), vmem_limit_bytes=VMEM_LIMIT),
        name="proj",
    )(x, g, w, wvt, kng, knb)


def _dsa_body(qa_ref, qi_ref, wt_ref, ka_ref, ki_ref, vt_ref, mka_ref, mki_ref, mva_ref,
              o_ref, smeta_ref, smain_ref, bmeta_ref, bmain_ref,
              tsel_ref, m_ref, al_ref, acc_ref, p_ref, s_ref, mb_ref, seen_ref, *, k_top):
    i = pl.program_id(1)
    krow = lax.broadcasted_iota(I32, (TQ, TQ), 0)
    qcol = lax.broadcasted_iota(I32, (TQ, TQ), 1)

    def main_rows(j):
        return pl.ds(pl.multiple_of(j * TQ, TQ), TQ)

    qi_all = qi_ref[...].reshape(H_I * TQ, D_IDX)

    def scores(ki_blk):
        s4 = _dot_nt(ki_blk, qi_all)
        sc = None
        for h in range(H_I):
            t = jnp.maximum(s4[:, h * TQ:(h + 1) * TQ], 0.0) * wt_ref[h:h + 1, :]
            sc = t if sc is None else sc + t
        return sc

    sc = scores(mki_ref[...])
    smeta_ref[...] = sc
    bmeta_ref[...] = sc.astype(BF16)

    def score_block(j, sc):
        smain_ref[j] = sc
        bmain_ref[j] = sc.astype(BF16)

    def full_scores(j):
        return scores(ki_ref[main_rows(j), :])

    def last_scores():
        return jnp.where((krow // CHUNK) <= (qcol // CHUNK), full_scores(i), -jnp.inf)

    def score_two(u, carry):
        sc_a, sc_b = full_scores(2 * u), full_scores(2 * u + 1)
        score_block(2 * u, sc_a)
        score_block(2 * u + 1, sc_b)
        return carry

    lax.fori_loop(0, i // 2, score_two, 0)

    @pl.when(i % 2 == 1)
    def _odd_scores():
        sc_a, sc_b = full_scores(i - 1), last_scores()
        score_block(i - 1, sc_a)
        score_block(i, sc_b)

    @pl.when(i % 2 == 0)
    def _even_scores():
        score_block(i, last_scores())

    def count(meta_ref, main_ref, part, pred):
        def body(j, c):
            return c + part(pred(main_ref[j]))

        c = lax.fori_loop(0, i + 1, body, jnp.zeros((CNT_ROWS, TQ), F32))
        cm = part(pred(meta_ref[...]), rows=N_META)
        return jnp.sum(c, axis=0, keepdims=True) + jnp.sum(cm, axis=0, keepdims=True)

    def part32(hit, rows=CNT_ROWS):
        ones = jnp.where(hit, 1.0, 0.0)
        return jnp.sum(ones.reshape(ones.shape[0] // rows, rows, TQ), axis=0)

    def part16(hit, rows=CNT_ROWS):
        ones = jnp.where(hit, jnp.bfloat16(1), jnp.bfloat16(0))
        acc = ones[0:rows]
        for t in range(1, ones.shape[0] // rows):
            acc = acc + ones[t * rows:(t + 1) * rows]
        return acc.astype(F32)

    kf = float(k_top)

    def step16(s, u):
        cand_u = u | jnp.left_shift(jnp.int32(1), 15 - s)
        cand = _bf16_from_key(jnp.maximum(cand_u, KEY16_LOWEST + HALF16) - HALF16)
        cnt = count(bmeta_ref, bmain_ref, part16, lambda x: x >= cand)
        return jnp.where(cnt >= kf, cand_u, u)

    u_hi = lax.fori_loop(0, 16, step16, jnp.zeros((1, TQ), I32))
    t_hi = _bf16_from_key(jnp.maximum(u_hi, KEY16_LOWEST + HALF16) - HALF16).astype(F32)
    base = jnp.maximum(_to_key(t_hi) - 65536, KEY32_LOWEST)

    def step32(s, d):
        cand_d = d | jnp.left_shift(jnp.int32(1), 16 - s)
        cand = _f32_from_key(jnp.minimum(base + cand_d, KEY32_HIGHEST))
        cnt = count(smeta_ref, smain_ref, part32, lambda x: x >= cand)
        return jnp.where(cnt >= kf, cand_d, d)

    d_lo = lax.fori_loop(0, 17, step32, jnp.zeros((1, TQ), I32))
    t_sel = _f32_from_key(jnp.minimum(base + d_lo, KEY32_HIGHEST))
    need = kf - count(smeta_ref, smain_ref, part32, lambda x: x > t_sel)
    tsel_ref[...] = jnp.broadcast_to(t_sel, tsel_ref.shape)

    m_ref[...] = jnp.full(m_ref.shape, NEG_INIT, F32)
    acc_ref[...] = jnp.zeros(acc_ref.shape, F32)
    thr = tsel_ref[0:1, :]

    def masked_logits(k_blk, sc, slot):
        nk = sc.shape[0]
        tie = sc == thr
        tri = (lax.broadcasted_iota(I32, (nk, nk), 1)
               <= lax.broadcasted_iota(I32, (nk, nk), 0)).astype(BF16)
        rank = _dot(tri, jnp.where(tie, 1.0, 0.0).astype(BF16)) + seen_ref[...]
        seen_ref[...] = rank[nk - 1:nk, :]
        negm = jnp.where(sc > thr, 0.0, jnp.where(tie, jnp.where(rank <= need, 0.0, -jnp.inf), -jnp.inf))
        for h in range(H_A):
            cols = slice(h * TQ, (h + 1) * TQ)
            s = _dot_nt(k_blk, qa_ref[h]) + negm
            s_ref[slot, 0:nk, cols] = s
            mb_ref[slot, :, cols] = jnp.max(s, axis=0, keepdims=True)

    def softmax(slot, nk, future):
        for h in range(H_A):
            cols = slice(h * TQ, (h + 1) * TQ)
            s = s_ref[slot, 0:nk, cols]
            if future is not None:
                s = s - (2.0 * LOG2E * SLOPES_A[h]) * future
            m_old = m_ref[:, cols]
            m_new = jnp.maximum(m_old, mb_ref[slot, :, cols])
            p_ref[slot, 0:nk, cols] = jnp.exp2(s - m_new).astype(BF16)
            al_ref[slot, :, cols] = jnp.exp2(m_old - m_new)
            m_ref[:, cols] = m_new

    def accumulate(vt_blk, slot, nk):
        acc_ref[...] = acc_ref[...] * al_ref[slot] + _dot(vt_blk, p_ref[slot, 0:nk, :])

    eye = (lax.broadcasted_iota(I32, (128, 128), 0)
           == lax.broadcasted_iota(I32, (128, 128), 1)).astype(BF16)
    seen_ref[...] = jnp.zeros(seen_ref.shape, F32)
    masked_logits(mka_ref[...], smeta_ref[...], 0)
    softmax(0, N_META, None)
    accumulate(_dot_nt(eye, mva_ref[...])[0:VTA_ROWS].astype(BF16), 0, N_META)

    p_ref[1] = jnp.zeros(p_ref.shape[1:], BF16)
    al_ref[1] = jnp.ones(al_ref.shape[1:], F32)
    masked_logits(ka_ref[main_rows(0), :], smain_ref[0], 0)

    def full_step(t, slot):
        masked_logits(ka_ref[main_rows(t + 1), :], smain_ref[t + 1], 1 - slot)
        accumulate(vt_ref[jnp.maximum(t - 1, 0)], 1 - slot, TQ)
        softmax(slot, TQ, None)

    def two_steps(u, carry):
        full_step(2 * u, 0)
        full_step(2 * u + 1, 1)
        return carry

    lax.fori_loop(0, i // 2, two_steps, 0)

    def last_step(slot):
        accumulate(vt_ref[jnp.maximum(i - 1, 0)], 1 - slot, TQ)
        softmax(slot, TQ, jnp.maximum(krow - qcol, 0).astype(F32))
        accumulate(vt_ref[i], slot, TQ)

    @pl.when(i % 2 == 1)
    def _odd():
        full_step(i - 1, 0)
        last_step(1)

    @pl.when(i % 2 == 0)
    def _even():
        last_step(0)

    def head_out_t(h):
        a = acc_ref[:, h * TQ:(h + 1) * TQ]
        return a[0:DH_A] / a[DH_A:DH_A + 1]

    for h in range(0, H_A, 2):
        pair_t = jnp.concatenate([head_out_t(h), head_out_t(h + 1)], axis=0)
        o_ref[:, h * DH_A:(h + 2) * DH_A] = pair_t.T.astype(BF16)


def _dsa(qa, qi, wt, kab, kib, vt, mka, mki, mva, batch, seq, k_top):
    nq = seq // TQ
    qrow = lambda b, i: (b * nq + i, 0)
    qhead = lambda b, i: (0, b * nq + i, 0)
    kv = lambda b, i: (b, 0)
    const = lambda b, i: (0, 0)
    return pl.pallas_call(
        functools.partial(_dsa_body, k_top=k_top),
        out_shape=jax.ShapeDtypeStruct((batch * seq, H_A * DH_A), BF16),
        grid=(batch, nq),
        in_specs=[
            pl.BlockSpec((H_A, TQ, 128), qhead),
            pl.BlockSpec((H_I, TQ, D_IDX), qhead),
            pl.BlockSpec((8, TQ), lambda b, i: (0, b * nq + i)),
            pl.BlockSpec((seq, 128), kv),
            pl.BlockSpec((seq, D_IDX), kv),
            pl.BlockSpec((nq, VTA_ROWS, TQ), lambda b, i: (b, 0, 0)),
            pl.BlockSpec((N_META, 128), const),
            pl.BlockSpec((N_META, D_IDX), const),
            pl.BlockSpec((N_META, 128), const),
        ],
        out_specs=pl.BlockSpec((TQ, H_A * DH_A), qrow),
        scratch_shapes=[
            pltpu.VMEM((N_META, TQ), F32),
            pltpu.VMEM((nq, TQ, TQ), F32),
            pltpu.VMEM((N_META, TQ), BF16),
            pltpu.VMEM((nq, TQ, TQ), BF16),
            pltpu.VMEM((8, TQ), F32),
            pltpu.VMEM((1, H_A * TQ), F32),
            pltpu.VMEM((2, 1, H_A * TQ), F32),
            pltpu.VMEM((VTA_ROWS, H_A * TQ), F32),
            pltpu.VMEM((2, TQ, H_A * TQ), BF16),
            pltpu.VMEM((2, TQ, H_A * TQ), F32),
            pltpu.VMEM((2, 1, H_A * TQ), F32),
            pltpu.VMEM((1, TQ), F32),
        ],
        compiler_params=pltpu.CompilerParams(
            dimension_semantics=("arbitrary", "arbitrary"), vmem_limit_bytes=VMEM_LIMIT),
        name="dsa",
    )(qa, qi, wt, kab, kib, vt, mka, mki, mva)


def _lambda(lq1_ref, lk1_ref, lq2_ref, lk2_ref):
    s1 = jnp.sum(lq1_ref[...] * lk1_ref[...], axis=-1, keepdims=True)
    s2 = jnp.sum(lq2_ref[...] * lk2_ref[...], axis=-1, keepdims=True)
    return jnp.exp(s1) - jnp.exp(s2) + LAM_INIT


def _diff_body(qb_ref, kb_ref, vt_ref, mkb_ref, mvb_ref, lq1_ref, lk1_ref, lq2_ref, lk2_ref,
               sg_ref, o_ref, m_ref, al_ref, acc_ref, p_ref, s_ref, mb_ref):
    i = pl.program_id(1)
    n_hc = 2 * H_B
    krow = lax.broadcasted_iota(I32, (TQ, TQ), 0)
    qcol = lax.broadcasted_iota(I32, (TQ, TQ), 1)
    m_ref[...] = jnp.full(m_ref.shape, NEG_INIT, F32)
    acc_ref[...] = jnp.zeros(acc_ref.shape, F32)

    def main_rows(j):
        return pl.ds(pl.multiple_of(j * TQ, TQ), TQ)

    def logits(k_of, slot, nk):
        for hc in range(n_hc):
            s = _dot_nt(k_of(hc), qb_ref[hc])
            s_ref[slot, hc, 0:nk, :] = s
            mb_ref[slot, :, hc * TQ:(hc + 1) * TQ] = jnp.max(s, axis=0, keepdims=True)

    def softmax(slot, nk, negm, future):
        for hc in range(n_hc):
            cols = slice(hc * TQ, (hc + 1) * TQ)
            s = s_ref[slot, hc, 0:nk, :]
            m_old = m_ref[:, cols]
            if negm is None:
                m_new = jnp.maximum(m_old, mb_ref[slot, :, cols])
            else:
                s = (s + negm) - (2.0 * LOG2E * SLOPES_B[hc // 2]) * future
                m_new = jnp.maximum(m_old, jnp.max(s, axis=0, keepdims=True))
            p_ref[slot, 0:nk, cols] = jnp.exp2(s - m_new).astype(BF16)
            al_ref[slot, :, cols] = jnp.exp2(m_old - m_new)
            m_ref[:, cols] = m_new

    def accumulate(vt_of, slot, nk):
        for h in range(H_B):
            cols = slice(2 * h * TQ, (2 * h + 2) * TQ)
            acc_ref[h] = acc_ref[h] * al_ref[slot, :, cols] + _dot(vt_of(h), p_ref[slot, 0:nk, cols])

    eye = (lax.broadcasted_iota(I32, (DV_B, DV_B), 0)
           == lax.broadcasted_iota(I32, (DV_B, DV_B), 1)).astype(BF16)

    def meta_vt(h):
        v_t = _dot_nt(eye, mvb_ref[:, h * DV_B:(h + 1) * DV_B]).astype(BF16)
        return jnp.concatenate([v_t, jnp.ones((VT_ROWS - DV_B, N_META), BF16)], axis=0)

    logits(lambda hc: mkb_ref[hc], 0, N_META)
    softmax(0, N_META, None, None)
    accumulate(meta_vt, 0, N_META)

    p_ref[1] = jnp.zeros(p_ref.shape[1:], BF16)
    al_ref[1] = jnp.ones(al_ref.shape[1:], F32)
    logits(lambda hc: kb_ref[hc, main_rows(0), :], 0, TQ)

    def main_vt(t):
        return lambda h: vt_ref[t, h]

    def full_step(t, slot):
        logits(lambda hc: kb_ref[hc, main_rows(t + 1), :], 1 - slot, TQ)
        accumulate(main_vt(jnp.maximum(t - 1, 0)), 1 - slot, TQ)
        softmax(slot, TQ, None, None)

    def two_steps(u, carry):
        full_step(2 * u, 0)
        full_step(2 * u + 1, 1)
        return carry

    lax.fori_loop(0, i // 2, two_steps, 0)

    def last_step(slot):
        accumulate(main_vt(jnp.maximum(i - 1, 0)), 1 - slot, TQ)
        softmax(slot, TQ, jnp.where((krow // CHUNK) <= (qcol // CHUNK), 0.0, -jnp.inf),
                jnp.maximum(krow - qcol, 0).astype(F32))
        accumulate(main_vt(i), slot, TQ)

    @pl.when(i % 2 == 1)
    def _odd():
        full_step(i - 1, 0)
        last_step(1)

    @pl.when(i % 2 == 0)
    def _even():
        last_step(0)

    lam = _lambda(lq1_ref, lk1_ref, lq2_ref, lk2_ref)
    for h in range(H_B):
        a = acc_ref[h]
        o0 = a[0:DV_B, 0:TQ] / a[DV_B:DV_B + 1, 0:TQ]
        o1 = a[0:DV_B, TQ:2 * TQ] / a[DV_B:DV_B + 1, TQ:2 * TQ]
        o = _rms((o0 - lam * o1).T, sg_ref[...]) * (1.0 - LAM_INIT)
        o_ref[:, h * DV_B:(h + 1) * DV_B] = o.astype(BF16)


def _diff(qb, kbh, vbt, mkb, mvb, lq1, lk1, lq2, lk2, sg, batch, seq):
    nq = seq // TQ
    qrow = lambda b, i: (b * nq + i, 0)
    qhead = lambda b, i: (0, b * nq + i, 0)
    const2 = lambda b, i: (0, 0)
    const3 = lambda b, i: (0, 0, 0)
    return pl.pallas_call(
        _diff_body,
        out_shape=jax.ShapeDtypeStruct((batch * seq, H_B * DV_B), BF16),
        grid=(batch, nq),
        in_specs=[
            pl.BlockSpec((2 * H_B, TQ, 128), qhead),
            pl.BlockSpec((2 * H_B, seq, 128), lambda b, i: (0, b, 0)),
            pl.BlockSpec((nq, H_B, VT_ROWS, TQ), lambda b, i: (b, 0, 0, 0)),
            pl.BlockSpec((2 * H_B, N_META, 128), const3),
            pl.BlockSpec((N_META, H_B * DV_B), const2),
            pl.BlockSpec((1, DH_B), const2),
            pl.BlockSpec((1, DH_B), const2),
            pl.BlockSpec((1, DH_B), const2),
            pl.BlockSpec((1, DH_B), const2),
            pl.BlockSpec((1, DV_B), const2),
        ],
        out_specs=pl.BlockSpec((TQ, H_B * DV_B), qrow),
        scratch_shapes=[
            pltpu.VMEM((1, 2 * H_B * TQ), F32),
            pltpu.VMEM((2, 1, 2 * H_B * TQ), F32),
            pltpu.VMEM((H_B, VT_ROWS, 2 * TQ), F32),
            pltpu.VMEM((2, TQ, 2 * H_B * TQ), BF16),
            pltpu.VMEM((2, 2 * H_B, TQ, TQ), F32),
            pltpu.VMEM((2, 1, 2 * H_B * TQ), F32),
        ],
        compiler_params=pltpu.CompilerParams(
            dimension_semantics=("arbitrary", "arbitrary"), vmem_limit_bytes=VMEM_LIMIT),
        name="diff",
    )(qb, kbh, vbt, mkb, mvb, lq1, lk1, lq2, lk2, sg)


def _sample_body(qa_ref, qi_ref, sm_ref, nka_ref, nva_ref, nki_ref, qb_ref, nkb_ref, nvb_ref,
                 cak_ref, cav_ref, cai_ref, cbk_ref, cbv_ref,
                 lq1_ref, lk1_ref, lq2_ref, lk2_ref, sg_ref,
                 oa_ref, ob_ref, pc_ref, pn_ref, *, k_top, past, ts):
    pad = LANES - ts
    row_c = lax.broadcasted_iota(I32, (ts, past), 0)
    col_c = lax.broadcasted_iota(I32, (ts, past), 1)
    row_n = lax.broadcasted_iota(I32, (ts, LANES), 0)
    col_n = lax.broadcasted_iota(I32, (ts, LANES), 1)
    new_ok = col_n < ts

    def pad_rows(x):
        return jnp.concatenate([x, jnp.zeros((pad,) + x.shape[1:], x.dtype)], axis=0)

    qi_all = qi_ref[...].reshape(H_I * ts, D_IDX)
    w = sm_ref[:, SM_WI:SM_WI + H_I]
    s4c = _dot(qi_all, cai_ref[...].astype(BF16))
    s4n = _dot_nt(qi_all, pad_rows(nki_ref[...]))
    sc_c = None
    sc_n = None
    for h in range(H_I):
        wh = w[:, h:h + 1]
        tc = jnp.maximum(s4c[h * ts:(h + 1) * ts], 0.0) * wh
        tn = jnp.maximum(s4n[h * ts:(h + 1) * ts], 0.0) * wh
        sc_c = tc if sc_c is None else sc_c + tc
        sc_n = tn if sc_n is None else sc_n + tn
    sc_n = jnp.where(new_ok, sc_n, -jnp.inf)

    def count(pred_c, pred_n):
        c = jnp.sum(jnp.where(pred_c, 1.0, 0.0), axis=1, keepdims=True)
        return c + jnp.sum(jnp.where(pred_n, 1.0, 0.0), axis=1, keepdims=True)

    def threshold(key):
        return _f32_from_key(jnp.clip(key, KEY32_LOWEST, KEY32_HIGHEST))

    def value_step(s, key):
        cand_key = key ^ jnp.left_shift(jnp.int32(1), 31 - s)
        cand = threshold(cand_key)
        cnt = count(sc_c >= cand, sc_n >= cand)
        return jnp.where(cnt >= float(k_top), cand_key, key)

    t = threshold(lax.fori_loop(0, 32, value_step, jnp.full((ts, 1), INT_MIN, I32)))
    need = float(k_top) - count(sc_c > t, sc_n > t)
    idx_n = col_n + past

    def tie_step(s, x):
        cand = x | jnp.left_shift(jnp.int32(1), 11 - s)
        cnt = count((sc_c == t) & (col_c < cand), (sc_n == t) & (idx_n < cand))
        return jnp.where(cnt < need, cand, x)

    x = lax.fori_loop(0, 12, tie_step, jnp.zeros((ts, 1), I32))
    negm_c = jnp.where((sc_c > t) | ((sc_c == t) & (col_c <= x)), 0.0, -jnp.inf)
    negm_n = jnp.where((sc_n > t) | ((sc_n == t) & (idx_n <= x)), 0.0, -jnp.inf)

    dist_c = (row_c - col_c + past).astype(F32)
    dist_n = jnp.abs(row_n - col_n).astype(F32)
    q_all = qa_ref[...][:, :, 0:DH_A].reshape(H_A * ts, DH_A)
    s_c = _dot(q_all, cak_ref[...].astype(BF16))
    s_n = _dot_nt(q_all, pad_rows(nka_ref[...][:, 0:DH_A]))
    for h in range(H_A):
        rows = slice(h * ts, (h + 1) * ts)
        lc = s_c[rows] + (negm_c - (LOG2E * SLOPES_A[h]) * dist_c)
        ln = s_n[rows] + (negm_n - (LOG2E * SLOPES_A[h]) * dist_n)
        m = jnp.maximum(jnp.max(lc, axis=1, keepdims=True), jnp.max(ln, axis=1, keepdims=True))
        m = jnp.maximum(m, NEG_INIT)
        pc_ref[rows, :] = jnp.exp2(lc - m).astype(BF16)
        pn_ref[rows, :] = jnp.exp2(ln - m).astype(BF16)
    vc_aug_t = jnp.concatenate([cav_ref[...].astype(BF16), jnp.ones((64, past), BF16)], axis=0)
    pv = _dot_nt(pc_ref[...], vc_aug_t) + _dot(pn_ref[...], pad_rows(nva_ref[...]))
    pv = pv / pltpu.roll(pv, DH_A, axis=1)
    oa_ref[...] = jnp.concatenate(
        [pv[h * ts:(h + 1) * ts, :DH_A] for h in range(H_A)], axis=1).astype(BF16)

    lam = _lambda(lq1_ref, lk1_ref, lq2_ref, lk2_ref)
    neg_new = jnp.where(new_ok, 0.0, -jnp.inf)
    for h in range(H_B):
        a_c = None
        a_n = None
        for c in range(2):
            hc = 2 * h + c
            q = qb_ref[hc][:, 0:DH_B]
            kc_t = cbk_ref[hc * DH_B:(hc + 1) * DH_B, :].astype(BF16)
            lc = _dot(q, kc_t) - (LOG2E * SLOPES_B[h]) * dist_c
            ln = _dot_nt(q, pad_rows(nkb_ref[hc][:, 0:DH_B])) + (neg_new - (LOG2E * SLOPES_B[h]) * dist_n)
            m = jnp.maximum(jnp.max(lc, axis=1, keepdims=True), jnp.max(ln, axis=1, keepdims=True))
            pc = jnp.exp2(lc - m)
            pn = jnp.exp2(ln - m)
            l = jnp.sum(pc, axis=1, keepdims=True) + jnp.sum(pn, axis=1, keepdims=True)
            pc = pc / l
            pn = pn / l
            if c == 0:
                a_c, a_n = pc, pn
            else:
                a_c, a_n = a_c - lam * pc, a_n - lam * pn
        vc = cbv_ref[pl.ds(h, past, stride=H_B), :].astype(BF16)
        vn = pad_rows(nvb_ref[:, h * DV_B:(h + 1) * DV_B])
        o = _dot(a_c.astype(BF16), vc) + _dot(a_n.astype(BF16), vn)
        o = _rms(o, sg_ref[...]) * (1.0 - LAM_INIT)
        ob_ref[:, h * DV_B:(h + 1) * DV_B] = o.astype(BF16)


def _sample(qa, qi, sm, nka, nva, nki, qb, nkb, nvb, cak, cav, cai, cbk, cbv,
            lq1, lk1, lq2, lk2, sg, batch, ts, past, k_top):
    row = lambda b: (b, 0)
    head = lambda b: (0, b, 0)
    cache = lambda b: (b, 0, 0)
    const = lambda b: (0, 0)
    return pl.pallas_call(
        functools.partial(_sample_body, k_top=k_top, past=past, ts=ts),
        out_shape=(jax.ShapeDtypeStruct((batch * ts, H_A * DH_A), BF16),
                   jax.ShapeDtypeStruct((batch * ts, H_B * DV_B), BF16)),
        grid=(batch,),
        in_specs=[
            pl.BlockSpec((H_A, ts, 128), head),
            pl.BlockSpec((H_I, ts, D_IDX), head),
            pl.BlockSpec((ts, 256), row),
            pl.BlockSpec((ts, 128), row),
            pl.BlockSpec((ts, 128), row),
            pl.BlockSpec((ts, D_IDX), row),
            pl.BlockSpec((2 * H_B, ts, 128), head),
            pl.BlockSpec((2 * H_B, ts, 128), head),
            pl.BlockSpec((ts, H_B * DV_B), row),
            pl.BlockSpec((None, DH_A, past), cache),
            pl.BlockSpec((None, DH_A, past), cache),
            pl.BlockSpec((None, D_IDX, past), cache),
            pl.BlockSpec((None, 2 * H_B * DH_B, past), cache),
            pl.BlockSpec((None, past * H_B, DV_B), cache),
            pl.BlockSpec((1, DH_B), const),
            pl.BlockSpec((1, DH_B), const),
            pl.BlockSpec((1, DH_B), const),
            pl.BlockSpec((1, DH_B), const),
            pl.BlockSpec((1, DV_B), const),
        ],
        out_specs=(pl.BlockSpec((ts, H_A * DH_A), row), pl.BlockSpec((ts, H_B * DV_B), row)),
        scratch_shapes=[
            pltpu.VMEM((H_A * ts, past), BF16),
            pltpu.VMEM((H_A * ts, LANES), BF16),
        ],
        compiler_params=pltpu.CompilerParams(
            dimension_semantics=("arbitrary",), vmem_limit_bytes=VMEM_LIMIT),
        name="sample",
    )(qa, qi, sm, nka, nva, nki, qb, nkb, nvb, cak, cav, cai, cbk, cbv, lq1, lk1, lq2, lk2, sg)


def _finish_body(x_ref, oa_ref, ob_ref, wo_ref, gm_ref, wu_ref, wd_ref, gf_ref, y_ref):
    o = jnp.concatenate([oa_ref[...], ob_ref[...]], axis=1)
    h1 = x_ref[...] + _dot(o, wo_ref[...])
    hn = _rms(h1, gm_ref[...]).astype(BF16)
    acc = h1
    for c in range(D_FF // D_MODEL):
        u = jnp.maximum(_dot(hn, wu_ref[:, c * D_MODEL:(c + 1) * D_MODEL]), 0.0)
        acc = acc + _dot((u * u).astype(BF16), wd_ref[c * D_MODEL:(c + 1) * D_MODEL, :])
    y_ref[...] = _rms(acc, gf_ref[...])


def _finish(x, oa, ob, wo, gm, wu, wd, gf, tm):
    m = x.shape[0]
    assert m % tm == 0
    row = lambda i: (i, 0)
    const = lambda i: (0, 0)
    resident = dict(pipeline_mode=pl.Buffered(1))
    return pl.pallas_call(
        _finish_body,
        out_shape=jax.ShapeDtypeStruct((m, D_MODEL), F32),
        grid=(m // tm,),
        in_specs=[
            pl.BlockSpec((tm, D_MODEL), row),
            pl.BlockSpec((tm, 512), row),
            pl.BlockSpec((tm, 512), row),
            pl.BlockSpec((D_MODEL, D_MODEL), const, **resident),
            pl.BlockSpec((1, D_MODEL), const),
            pl.BlockSpec((D_MODEL, D_FF), const, **resident),
            pl.BlockSpec((D_FF, D_MODEL), const, **resident),
            pl.BlockSpec((1, D_MODEL), const),
        ],
        out_specs=pl.BlockSpec((tm, D_MODEL), row),
        compiler_params=pltpu.CompilerParams(
            dimension_semantics=("arbitrary",), vmem_limit_bytes=VMEM_LIMIT),
        name="finish",
    )(x, oa, ob, wo, gm, wu, wd, gf)


def kernel(x_prompt, x_sample, cache_a_k, cache_a_v, cache_a_idx_k, cache_b_k, cache_b_v,
           meta_tokens, attn_norm_g, w_in, idx_k_norm_g, idx_k_norm_b,
           lambda_q1, lambda_k1, lambda_q2, lambda_k2, subln_g, w_o,
           mlp_norm_g, w_up, w_down, final_norm_g):
    batch, seq, _ = x_prompt.shape
    dec_batch, ts, _ = x_sample.shape
    past = cache_a_k.shape[2]
    assert attn_norm_g.shape[0] == 1, "single-layer step"
    assert seq % TQ == 0 and ts == 16 and past % LANES == 0 and past // CHUNK == (past + ts - 1) // CHUNK
    n = N_META + seq
    k_top_p = min(TOPK_MAX, seq // 4)
    k_top_s = min(TOPK_MAX, (past + ts) // 4)

    w_t = w_in[0].T.astype(BF16)
    w = jnp.concatenate(
        [w_t[0:512],
         w_t[964:1476],
         w_t[1476:1988],
         w_t[1988:2500],
         w_t[640:896],
         w_t[512:640],
         w_t[896:964]],
        axis=0)
    w = jnp.pad(w, ((0, W_COLS - 2500), (0, 0))).T
    wvt = w_t[1476:2500]
    wo = w_o[0].astype(BF16)
    wu = w_up[0].astype(BF16)
    wd = w_down[0].astype(BF16)
    g_attn = attn_norm_g[0][None]
    g_mlp = mlp_norm_g[0][None]
    g_fin = final_norm_g[None]
    kng = idx_k_norm_g[0][None]
    knb = idx_k_norm_b[0][None]
    lq1, lk1, lq2, lk2 = lambda_q1[0][None], lambda_k1[0][None], lambda_q2[0][None], lambda_k2[0][None]
    sg = subln_g[0][None]

    xp = x_prompt.reshape(batch * seq, D_MODEL)
    xs = x_sample.reshape(dec_batch * ts, D_MODEL)

    (qa_p, qi_p, qb_p, kbh_p, vb_p, kab_p, kib_p, vt_p, wt_p, vbt_p, smt_p, kbt_p) = _proj(
        xp, g_attn, w, wvt, kng, knb, 512, N_META, seq, True)
    (_, _, _, kbh_m, vb_m, kab_m, kib_m, kb_m, vbb_m, sm_m, vaa_m) = _proj(
        meta_tokens, g_attn, w, wvt, kng, knb, N_META, 0, N_META, False)
    (qa_s, qi_s, qb_s, kbh_s, vb_s, kab_s, kib_s, kb_s, vbb_s, sm_s, vaa_s) = _proj(
        xs, g_attn, w, wvt, kng, knb, dec_batch * ts, past, dec_batch * ts, False)

    oa_p = _dsa(qa_p, qi_p, wt_p, kab_p, kib_p, vt_p, kab_m, kib_m, vaa_m, batch, seq, k_top_p)
    ob_p = _diff(qb_p, kbh_p, vbt_p, kbh_m, vbb_m, lq1, lk1, lq2, lk2, sg, batch, seq)
    y_prompt = _finish(xp, oa_p, ob_p, wo, g_mlp, wu, wd, g_fin, 512).reshape(batch, seq, D_MODEL)

    oa_s, ob_s = _sample(
        qa_s, qi_s, sm_s, kab_s, vaa_s, kib_s, qb_s, kbh_s, vbb_s,
        jnp.swapaxes(cache_a_k[0], 1, 2), jnp.swapaxes(cache_a_v[0], 1, 2),
        jnp.swapaxes(cache_a_idx_k[0], 1, 2),
        jnp.transpose(cache_b_k[0], (0, 2, 3, 4, 1)).reshape(dec_batch, 2 * H_B * DH_B, past),
        cache_b_v[0].reshape(dec_batch, past * H_B, DV_B),
        lq1, lk1, lq2, lk2, sg, dec_batch, ts, past, k_top_s)
    y_sample = _finish(xs, oa_s, ob_s, wo, g_mlp, wu, wd, g_fin, dec_batch * ts).reshape(
        dec_batch, ts, D_MODEL)

    def with_meta_t(meta_rows, main_t):
        meta_b = jnp.broadcast_to(meta_rows.T[None], (batch, 64, N_META))
        return jnp.swapaxes(jnp.concatenate([meta_b, main_t], axis=2), 1, 2)[None]

    new_a_k_p = with_meta_t(sm_m[:, SM_KA:SM_KA + 64], smt_p[:, SM_KA:SM_KA + 64])
    new_a_v_p = with_meta_t(sm_m[:, SM_VA:SM_VA + 64], smt_p[:, SM_VA:SM_VA + 64])
    new_a_i_p = with_meta_t(sm_m[:, SM_KI:SM_KI + 64], smt_p[:, SM_KI:SM_KI + 64])
    kb_meta_t = jnp.broadcast_to(kb_m.T.reshape(1, H_B, 2, DH_B, N_META), (batch, H_B, 2, DH_B, N_META))
    kb_main_t = kbt_p.reshape(batch, H_B, 2, DH_B, seq)
    new_b_k_p = jnp.transpose(
        jnp.concatenate([kb_meta_t, kb_main_t], axis=4), (0, 4, 1, 2, 3))[None]
    new_b_v_p = jnp.concatenate(
        [jnp.broadcast_to(vb_m.reshape(1, N_META, H_B, DV_B), (batch, N_META, H_B, DV_B)),
         vb_p.reshape(batch, seq, H_B, DV_B)], axis=1)[None]
    new_a_k_s = sm_s[:, SM_KA:SM_KA + 64].reshape(1, dec_batch, ts, 64)
    new_a_v_s = sm_s[:, SM_VA:SM_VA + 64].reshape(1, dec_batch, ts, 64)
    new_a_i_s = sm_s[:, SM_KI:SM_KI + 64].reshape(1, dec_batch, ts, 64)
    new_b_k_s = kb_s.reshape(1, dec_batch, ts, H_B, 2, DH_B)
    new_b_v_s = vb_s.reshape(1, dec_batch, ts, H_B, DV_B)
    return (y_prompt, y_sample, new_a_k_p, new_a_v_p, new_a_i_p, new_b_k_p, new_b_v_p,
            new_a_k_s, new_a_v_s, new_a_i_s, new_b_k_s, new_b_v_s)
```

```python
import functools

import jax
import jax.numpy as jnp
import numpy as np
from jax import lax
from jax.experimental import pallas as pl
from jax.experimental.pallas import tpu as pltpu

F32 = jnp.float32
BF16 = jnp.bfloat16
I32 = jnp.int32
I16 = jnp.int16
HALF16 = 32768

D_MODEL = 1024
CHUNK = 64
N_META = 16
H_A = 8
DH_A = 64
H_I = 4
D_IDX = 64
TOPK_MAX = 256
H_B = 4
DH_B = 64
DV_B = 2 * DH_B
D_FF = 4 * D_MODEL
EPS = 1e-6
LAM_INIT = 0.2

LANES = 128
C_QA, C_QB, C_KB, C_VB, C_QI, C_SM = 0, 512, 1024, 1536, 2048, 2304
W_COLS = 2560
SM_KA, SM_VA, SM_KI, SM_WI = 0, 64, 128, 192

LOG2E = float(np.float32(1.4426950408889634))
Q_SCALE = (DH_A ** -0.5) * LOG2E
N_SLOPE_PIECES = 3


def _bf16_pieces(value):
    rest = np.float32(value)
    pieces = []
    for _ in range(N_SLOPE_PIECES):
        piece = np.float32(rest.astype(jnp.bfloat16))
        pieces.append(float(piece))
        rest = np.float32(rest - piece)
    assert rest == 0.0
    return pieces
W_SCALE = (H_I ** -0.5) * (D_IDX ** -0.5)
SLOPES_A = tuple(2.0 ** (-8.0 * (i + 1) / H_A) for i in range(H_A))
SLOPES_B = tuple(2.0 ** (-8.0 * (i + 1) / H_B) for i in range(H_B))

INT_MIN = -(2 ** 31)
NEG_INIT = -1e30
TQ = 256
ONES_ROWS = 16
VT_ROWS = DV_B + ONES_ROWS
VTA_ROWS = DH_A + ONES_ROWS
CNT_ROWS = 32
VMEM_LIMIT = 56 * 1024 * 1024

_NT = (((1,), (1,)), ((), ()))


def _dot(a, b):
    return jnp.dot(a, b, preferred_element_type=F32)


def _dot_nt(a, b):
    return lax.dot_general(a, b, _NT, preferred_element_type=F32)


def _rms(x, g):
    return (x * lax.rsqrt(jnp.mean(x * x, axis=-1, keepdims=True) + EPS)) * g


def _to_key(score):
    b = lax.bitcast_convert_type(score, I32)
    return b ^ ((b >> 31) & jnp.int32(0x7FFFFFFF))


def _f32_from_key(key):
    return lax.bitcast_convert_type(key ^ ((key >> 31) & jnp.int32(0x7FFFFFFF)), F32)


def _bf16_from_key(key):
    bits = key ^ ((key >> 15) & jnp.int32(0x7FFF))
    return lax.bitcast_convert_type(jnp.left_shift(bits, 16), F32).astype(BF16)


KEY16_LOWEST = -32640
KEY32_LOWEST = -2139095040
KEY32_HIGHEST = 2139095039


def _wide(x, width):
    reps = width // LANES
    return x if reps == 1 else jnp.concatenate([x] * reps, axis=1)


def _proj_body(x_ref, g_ref, w_ref, wvt_ref, kng_ref, knb_ref,
               qa_ref, qi_ref, qb_ref, kbh_ref, vb_ref, kab_ref, kib_ref, *mode_refs,
               pos0, period, key_major):
    x = x_ref[...]
    hn = _rms(x, g_ref[...]).astype(BF16)
    rows = x.shape[0]

    def mm(lo, width):
        return _dot(hn, w_ref[:, lo:lo + width])

    lane = lax.broadcasted_iota(I32, (rows, 64), 1)
    r = pl.program_id(0) * rows + lax.broadcasted_iota(I32, (rows, 64), 0)
    pos = pos0 + r % period
    pos_cols = jnp.where(lane >= 2 * N_SLOPE_PIECES, 0,
                         jnp.where(lane % 2 == 0, (pos // 256) * 256, pos % 256))
    pos_cols = pos_cols.astype(F32).astype(BF16)

    def slope_cols(slope):
        cols = jnp.zeros((rows, 64), F32)
        for n, piece in enumerate(_bf16_pieces(slope * LOG2E)):
            cols = jnp.where(lane // 2 == n, piece, cols)
        return cols.astype(BF16)

    z = mm(C_QA, 512) * Q_SCALE
    for h in range(H_A):
        qa_ref[h] = jnp.concatenate(
            [z[:, h * DH_A:(h + 1) * DH_A].astype(BF16), slope_cols(SLOPES_A[h])], axis=1)
    z = mm(C_QB, 512) * Q_SCALE
    for hc in range(2 * H_B):
        qb_ref[hc] = jnp.concatenate(
            [z[:, hc * DH_B:(hc + 1) * DH_B].astype(BF16), slope_cols(SLOPES_B[hc // 2])], axis=1)
    if key_major:
        vt_ref, wt_ref, vbt_ref, smt_ref, kbt_ref = mode_refs
    else:
        kb_ref, vbb_ref, sm_ref, vaa_ref = mode_refs
    z = mm(C_KB, 512)
    if not key_major:
        kb_ref[...] = z
    for hc in range(2 * H_B):
        kbh_ref[hc] = jnp.concatenate([z[:, hc * DH_B:(hc + 1) * DH_B].astype(BF16), pos_cols], axis=1)
    z = mm(C_VB, 512)
    for h in range(H_B):
        vb_ref[pl.ds(h, rows, stride=H_B), :] = z[:, h * DV_B:(h + 1) * DV_B]
    if not key_major:
        vbb_ref[...] = z.astype(BF16)
    z = mm(C_QI, 256)
    for h in range(H_I):
        qi_ref[h] = z[:, h * D_IDX:(h + 1) * D_IDX].astype(BF16)
    z = mm(C_SM, 256)
    ka = z[:, SM_KA:SM_KA + 64]
    va = z[:, SM_VA:SM_VA + 64]
    ki = z[:, SM_KI:SM_KI + 64]
    xc = ki - jnp.mean(ki, axis=-1, keepdims=True)
    ki = xc * lax.rsqrt(jnp.mean(xc * xc, axis=-1, keepdims=True) + EPS) * kng_ref[...] + knb_ref[...]
    kab_ref[...] = jnp.concatenate([ka.astype(BF16), pos_cols], axis=1)
    kib_ref[...] = ki.astype(BF16)
    if not key_major:
        sm_ref[:, 0:128] = z[:, 0:128]
        sm_ref[:, SM_KI:SM_KI + 64] = ki
        sm_ref[:, SM_WI:SM_WI + 64] = z[:, SM_WI:SM_WI + 64] * W_SCALE
        vaa_ref[...] = jnp.concatenate([va.astype(BF16), jnp.ones((rows, 64), BF16)], axis=1)
    else:
        kv_t = z[:, 0:128].T
        for c in range(rows // TQ):
            vt_ref[c] = jnp.concatenate(
                [kv_t[64:128, c * TQ:(c + 1) * TQ].astype(BF16), jnp.ones((ONES_ROWS, TQ), BF16)], axis=0)
        kw_t = jnp.concatenate([ki, z[:, SM_WI:SM_WI + 64] * W_SCALE], axis=1).T
        wt_ref[...] = kw_t[64:72, :]
        smt_ref[0:128, :] = kv_t
        smt_ref[128:192, :] = kw_t[0:64, :]
        kbt_ref[...] = _dot_nt(wvt_ref[0:512, :], hn)
        vb_t = _dot_nt(wvt_ref[512:1024, :], hn)
        for c in range(rows // TQ):
            for h in range(H_B):
                vbt_ref[c, h] = jnp.concatenate(
                    [vb_t[h * DV_B:(h + 1) * DV_B, c * TQ:(c + 1) * TQ].astype(BF16),
                     jnp.ones((VT_ROWS - DV_B, TQ), BF16)], axis=0)


def _proj(x, g, w, wvt, kng, knb, tm, pos0, period, key_major):
    m = x.shape[0]
    assert m % tm == 0
    row = lambda i: (i, 0)
    head = lambda i: (0, i, 0)
    const = lambda i: (0, 0)
    out_shape = (
        jax.ShapeDtypeStruct((H_A, m, 128), BF16),
        jax.ShapeDtypeStruct((H_I, m, D_IDX), BF16),
        jax.ShapeDtypeStruct((2 * H_B, m, 128), BF16),
        jax.ShapeDtypeStruct((2 * H_B, m, 128), BF16),
        jax.ShapeDtypeStruct((m * H_B, DV_B), F32),
        jax.ShapeDtypeStruct((m, 128), BF16),
        jax.ShapeDtypeStruct((m, 64), BF16),
    )
    out_specs = (
        pl.BlockSpec((H_A, tm, 128), head),
        pl.BlockSpec((H_I, tm, D_IDX), head),
        pl.BlockSpec((2 * H_B, tm, 128), head),
        pl.BlockSpec((2 * H_B, tm, 128), head),
        pl.BlockSpec((tm * H_B, DV_B), row),
        pl.BlockSpec((tm, 128), row),
        pl.BlockSpec((tm, 64), row),
    )
    if not key_major:
        out_shape += (
            jax.ShapeDtypeStruct((m, 512), F32),
            jax.ShapeDtypeStruct((m, 512), BF16),
            jax.ShapeDtypeStruct((m, 256), F32),
            jax.ShapeDtypeStruct((m, 128), BF16),
        )
        out_specs += (
            pl.BlockSpec((tm, 512), row),
            pl.BlockSpec((tm, 512), row),
            pl.BlockSpec((tm, 256), row),
            pl.BlockSpec((tm, 128), row),
        )
    else:
        assert tm % TQ == 0 and period % tm == 0 and m % period == 0
        per_stream = period // tm
        stream_cols = lambda i: (i // per_stream, 0, i % per_stream)
        out_shape += (
            jax.ShapeDtypeStruct((m // TQ, VTA_ROWS, TQ), BF16),
            jax.ShapeDtypeStruct((8, m), F32),
            jax.ShapeDtypeStruct((m // TQ, H_B, VT_ROWS, TQ), BF16),
            jax.ShapeDtypeStruct((m // period, 192, period), F32),
            jax.ShapeDtypeStruct((m // period, 512, period), F32),
        )
        out_specs += (
            pl.BlockSpec((tm // TQ, VTA_ROWS, TQ), lambda i: (i, 0, 0)),
            pl.BlockSpec((8, tm), lambda i: (0, i)),
            pl.BlockSpec((tm // TQ, H_B, VT_ROWS, TQ), lambda i: (i, 0, 0, 0)),
            pl.BlockSpec((None, 192, tm), stream_cols),
            pl.BlockSpec((None, 512, tm), stream_cols),
        )
    return pl.pallas_call(
        functools.partial(_proj_body, pos0=pos0, period=period, key_major=key_major),
        out_shape=out_shape,
        grid=(m // tm,),
        in_specs=[
            pl.BlockSpec((tm, D_MODEL), row),
            pl.BlockSpec((1, D_MODEL), const),
            pl.BlockSpec((D_MODEL, W_COLS), const),
            pl.BlockSpec((2 * H_B * DV_B, D_MODEL), const),
            pl.BlockSpec((1, D_IDX), const),
            pl.BlockSpec((1, D_IDX), const),
        ],
        out_specs=out_specs,
        compiler_params=pltpu.CompilerParams(
            dimension_semantics=("arbitrary",), vmem_limit_bytes=VMEM_LIMIT),
        name="proj",
    )(x, g, w, wvt, kng, knb)


def _dsa_body(qa_ref, qi_ref, wt_ref, ka_ref, ki_ref, vt_ref, mka_ref, mki_ref, mva_ref,
              o_ref, smeta_ref, smain_ref, bmeta_ref, bmain_ref,
              tsel_ref, m_ref, al_ref, acc_ref, p_ref, s_ref, mb_ref, seen_ref, *, k_top):
    i = pl.program_id(1)
    krow = lax.broadcasted_iota(I32, (TQ, TQ), 0)
    qcol = lax.broadcasted_iota(I32, (TQ, TQ), 1)

    def main_rows(j):
        return pl.ds(pl.multiple_of(j * TQ, TQ), TQ)

    qi_all = qi_ref[...].reshape(H_I * TQ, D_IDX)

    def scores(ki_blk):
        s4 = _dot_nt(ki_blk, qi_all)
        sc = None
        for h in range(H_I):
            t = jnp.maximum(s4[:, h * TQ:(h + 1) * TQ], 0.0) * wt_ref[h:h + 1, :]
            sc = t if sc is None else sc + t
        return sc

    sc = scores(mki_ref[...])
    smeta_ref[...] = sc
    bmeta_ref[...] = sc.astype(BF16)

    def score_block(j, sc):
        smain_ref[j] = sc
        bmain_ref[j] = sc.astype(BF16)

    def full_scores(j):
        return scores(ki_ref[main_rows(j), :])

    def last_scores():
        return jnp.where((krow // CHUNK) <= (qcol // CHUNK), full_scores(i), -jnp.inf)

    def score_two(u, carry):
        sc_a, sc_b = full_scores(2 * u), full_scores(2 * u + 1)
        score_block(2 * u, sc_a)
        score_block(2 * u + 1, sc_b)
        return carry

    lax.fori_loop(0, i // 2, score_two, 0)

    @pl.when(i % 2 == 1)
    def _odd_scores():
        sc_a, sc_b = full_scores(i - 1), last_scores()
        score_block(i - 1, sc_a)
        score_block(i, sc_b)

    @pl.when(i % 2 == 0)
    def _even_scores():
        score_block(i, last_scores())

    def count(meta_ref, main_ref, part, pred):
        def one(j):
            return part(pred(main_ref[j]))

        def two(u, c):
            return c + (one(2 * u) + one(2 * u + 1))

        c = lax.fori_loop(0, (i + 1) // 2, two, jnp.zeros((CNT_ROWS, TQ), F32))
        c = lax.cond(i % 2 == 0, lambda c: c + one(i), lambda c: c, c)
        cm = part(pred(meta_ref[...]), rows=N_META)
        return jnp.sum(c, axis=0, keepdims=True) + jnp.sum(cm, axis=0, keepdims=True)

    def part32(hit, rows=CNT_ROWS):
        ones = jnp.where(hit, 1.0, 0.0)
        return jnp.sum(ones.reshape(ones.shape[0] // rows, rows, TQ), axis=0)

    def part16(hit, rows=CNT_ROWS):
        ones = jnp.where(hit, jnp.bfloat16(1), jnp.bfloat16(0))
        acc = ones[0:rows]
        for t in range(1, ones.shape[0] // rows):
            acc = acc + ones[t * rows:(t + 1) * rows]
        return acc.astype(F32)

    kf = float(k_top)

    def step16(s, u):
        cand_u = u | jnp.left_shift(jnp.int32(1), 15 - s)
        cand = _bf16_from_key(jnp.maximum(cand_u, KEY16_LOWEST + HALF16) - HALF16)
        cnt = count(bmeta_ref, bmain_ref, part16, lambda x: x >= cand)
        return jnp.where(cnt >= kf, cand_u, u)

    u_hi = lax.fori_loop(0, 16, step16, jnp.zeros((1, TQ), I32))
    t_hi = _bf16_from_key(jnp.maximum(u_hi, KEY16_LOWEST + HALF16) - HALF16).astype(F32)
    base = jnp.maximum(_to_key(t_hi) - 65536, KEY32_LOWEST)

    def step32(s, d):
        cand_d = d | jnp.left_shift(jnp.int32(1), 16 - s)
        cand = _f32_from_key(jnp.minimum(base + cand_d, KEY32_HIGHEST))
        cnt = count(smeta_ref, smain_ref, part32, lambda x: x >= cand)
        return jnp.where(cnt >= kf, cand_d, d)

    d_lo = lax.fori_loop(0, 17, step32, jnp.zeros((1, TQ), I32))
    t_sel = _f32_from_key(jnp.minimum(base + d_lo, KEY32_HIGHEST))
    need = kf - count(smeta_ref, smain_ref, part32, lambda x: x > t_sel)
    tsel_ref[...] = jnp.broadcast_to(t_sel, tsel_ref.shape)

    m_ref[...] = jnp.full(m_ref.shape, NEG_INIT, F32)
    acc_ref[...] = jnp.zeros(acc_ref.shape, F32)
    thr = tsel_ref[0:1, :]

    def masked_logits(k_blk, sc, slot):
        nk = sc.shape[0]
        tie = sc == thr
        tri = (lax.broadcasted_iota(I32, (nk, nk), 1)
               <= lax.broadcasted_iota(I32, (nk, nk), 0)).astype(BF16)
        rank = _dot(tri, jnp.where(tie, 1.0, 0.0).astype(BF16)) + seen_ref[...]
        seen_ref[...] = rank[nk - 1:nk, :]
        negm = jnp.where(sc > thr, 0.0, jnp.where(tie, jnp.where(rank <= need, 0.0, -jnp.inf), -jnp.inf))
        for h in range(H_A):
            cols = slice(h * TQ, (h + 1) * TQ)
            s = _dot_nt(k_blk, qa_ref[h]) + negm
            s_ref[slot, 0:nk, cols] = s
            mb_ref[slot, :, cols] = jnp.max(s, axis=0, keepdims=True)

    def softmax(slot, nk, future):
        for h in range(H_A):
            cols = slice(h * TQ, (h + 1) * TQ)
            s = s_ref[slot, 0:nk, cols]
            if future is not None:
                s = s - (2.0 * LOG2E * SLOPES_A[h]) * future
            m_old = m_ref[:, cols]
            m_new = jnp.maximum(m_old, mb_ref[slot, :, cols])
            p_ref[slot, 0:nk, cols] = jnp.exp2(s - m_new).astype(BF16)
            al_ref[slot, :, cols] = jnp.exp2(m_old - m_new)
            m_ref[:, cols] = m_new

    def accumulate(vt_blk, slot, nk):
        acc_ref[...] = acc_ref[...] * al_ref[slot] + _dot(vt_blk, p_ref[slot, 0:nk, :])

    eye = (lax.broadcasted_iota(I32, (128, 128), 0)
           == lax.broadcasted_iota(I32, (128, 128), 1)).astype(BF16)
    seen_ref[...] = jnp.zeros(seen_ref.shape, F32)
    masked_logits(mka_ref[...], smeta_ref[...], 0)
    softmax(0, N_META, None)
    accumulate(_dot_nt(eye, mva_ref[...])[0:VTA_ROWS].astype(BF16), 0, N_META)

    p_ref[1] = jnp.zeros(p_ref.shape[1:], BF16)
    al_ref[1] = jnp.ones(al_ref.shape[1:], F32)
    masked_logits(ka_ref[main_rows(0), :], smain_ref[0], 0)

    def full_step(t, slot):
        masked_logits(ka_ref[main_rows(t + 1), :], smain_ref[t + 1], 1 - slot)
        accumulate(vt_ref[jnp.maximum(t - 1, 0)], 1 - slot, TQ)
        softmax(slot, TQ, None)

    def two_steps(u, carry):
        full_step(2 * u, 0)
        full_step(2 * u + 1, 1)
        return carry

    lax.fori_loop(0, i // 2, two_steps, 0)

    def last_step(slot):
        accumulate(vt_ref[jnp.maximum(i - 1, 0)], 1 - slot, TQ)
        softmax(slot, TQ, jnp.maximum(krow - qcol, 0).astype(F32))
        accumulate(vt_ref[i], slot, TQ)

    @pl.when(i % 2 == 1)
    def _odd():
        full_step(i - 1, 0)
        last_step(1)

    @pl.when(i % 2 == 0)
    def _even():
        last_step(0)

    def head_out_t(h):
        a = acc_ref[:, h * TQ:(h + 1) * TQ]
        return a[0:DH_A] / a[DH_A:DH_A + 1]

    for h in range(0, H_A, 2):
        pair_t = jnp.concatenate([head_out_t(h), head_out_t(h + 1)], axis=0)
        o_ref[:, h * DH_A:(h + 2) * DH_A] = pair_t.T.astype(BF16)


def _dsa(qa, qi, wt, kab, kib, vt, mka, mki, mva, batch, seq, k_top):
    nq = seq // TQ
    qrow = lambda b, i: (b * nq + i, 0)
    qhead = lambda b, i: (0, b * nq + i, 0)
    kv = lambda b, i: (b, 0)
    const = lambda b, i: (0, 0)
    return pl.pallas_call(
        functools.partial(_dsa_body, k_top=k_top),
        out_shape=jax.ShapeDtypeStruct((batch * seq, H_A * DH_A), BF16),
        grid=(batch, nq),
        in_specs=[
            pl.BlockSpec((H_A, TQ, 128), qhead),
            pl.BlockSpec((H_I, TQ, D_IDX), qhead),
            pl.BlockSpec((8, TQ), lambda b, i: (0, b * nq + i)),
            pl.BlockSpec((seq, 128), kv),
            pl.BlockSpec((seq, D_IDX), kv),
            pl.BlockSpec((nq, VTA_ROWS, TQ), lambda b, i: (b, 0, 0)),
            pl.BlockSpec((N_META, 128), const),
            pl.BlockSpec((N_META, D_IDX), const),
            pl.BlockSpec((N_META, 128), const),
        ],
        out_specs=pl.BlockSpec((TQ, H_A * DH_A), qrow),
        scratch_shapes=[
            pltpu.VMEM((N_META, TQ), F32),
            pltpu.VMEM((nq, TQ, TQ), F32),
            pltpu.VMEM((N_META, TQ), BF16),
            pltpu.VMEM((nq, TQ, TQ), BF16),
            pltpu.VMEM((8, TQ), F32),
            pltpu.VMEM((1, H_A * TQ), F32),
            pltpu.VMEM((2, 1, H_A * TQ), F32),
            pltpu.VMEM((VTA_ROWS, H_A * TQ), F32),
            pltpu.VMEM((2, TQ, H_A * TQ), BF16),
            pltpu.VMEM((2, TQ, H_A * TQ), F32),
            pltpu.VMEM((2, 1, H_A * TQ), F32),
            pltpu.VMEM((1, TQ), F32),
        ],
        compiler_params=pltpu.CompilerParams(
            dimension_semantics=("arbitrary", "arbitrary"), vmem_limit_bytes=VMEM_LIMIT),
        name="dsa",
    )(qa, qi, wt, kab, kib, vt, mka, mki, mva)


def _lambda(lq1_ref, lk1_ref, lq2_ref, lk2_ref):
    s1 = jnp.sum(lq1_ref[...] * lk1_ref[...], axis=-1, keepdims=True)
    s2 = jnp.sum(lq2_ref[...] * lk2_ref[...], axis=-1, keepdims=True)
    return jnp.exp(s1) - jnp.exp(s2) + LAM_INIT


def _diff_body(qb_ref, kb_ref, vt_ref, mkb_ref, mvb_ref, lq1_ref, lk1_ref, lq2_ref, lk2_ref,
               sg_ref, o_ref, m_ref, al_ref, acc_ref, p_ref, s_ref, mb_ref):
    i = pl.program_id(1)
    n_hc = 2 * H_B
    krow = lax.broadcasted_iota(I32, (TQ, TQ), 0)
    qcol = lax.broadcasted_iota(I32, (TQ, TQ), 1)
    m_ref[...] = jnp.full(m_ref.shape, NEG_INIT, F32)
    acc_ref[...] = jnp.zeros(acc_ref.shape, F32)

    def main_rows(j):
        return pl.ds(pl.multiple_of(j * TQ, TQ), TQ)

    def logits(k_of, slot, nk):
        for hc in range(n_hc):
            s = _dot_nt(k_of(hc), qb_ref[hc])
            s_ref[slot, hc, 0:nk, :] = s
            mb_ref[slot, :, hc * TQ:(hc + 1) * TQ] = jnp.max(s, axis=0, keepdims=True)

    def softmax(slot, nk, negm, future):
        for hc in range(n_hc):
            cols = slice(hc * TQ, (hc + 1) * TQ)
            s = s_ref[slot, hc, 0:nk, :]
            m_old = m_ref[:, cols]
            if negm is None:
                m_new = jnp.maximum(m_old, mb_ref[slot, :, cols])
            else:
                s = (s + negm) - (2.0 * LOG2E * SLOPES_B[hc // 2]) * future
                m_new = jnp.maximum(m_old, jnp.max(s, axis=0, keepdims=True))
            p_ref[slot, 0:nk, cols] = jnp.exp2(s - m_new).astype(BF16)
            al_ref[slot, :, cols] = jnp.exp2(m_old - m_new)
            m_ref[:, cols] = m_new

    def accumulate(vt_of, slot, nk):
        for h in range(H_B):
            cols = slice(2 * h * TQ, (2 * h + 2) * TQ)
            acc_ref[h] = acc_ref[h] * al_ref[slot, :, cols] + _dot(vt_of(h), p_ref[slot, 0:nk, cols])

    eye = (lax.broadcasted_iota(I32, (DV_B, DV_B), 0)
           == lax.broadcasted_iota(I32, (DV_B, DV_B), 1)).astype(BF16)

    def meta_vt(h):
        v_t = _dot_nt(eye, mvb_ref[:, h * DV_B:(h + 1) * DV_B]).astype(BF16)
        return jnp.concatenate([v_t, jnp.ones((VT_ROWS - DV_B, N_META), BF16)], axis=0)

    logits(lambda hc: mkb_ref[hc], 0, N_META)
    softmax(0, N_META, None, None)
    accumulate(meta_vt, 0, N_META)

    p_ref[1] = jnp.zeros(p_ref.shape[1:], BF16)
    al_ref[1] = jnp.ones(al_ref.shape[1:], F32)
    logits(lambda hc: kb_ref[hc, main_rows(0), :], 0, TQ)

    def main_vt(t):
        return lambda h: vt_ref[t, h]

    def full_step(t, slot):
        logits(lambda hc: kb_ref[hc, main_rows(t + 1), :], 1 - slot, TQ)
        accumulate(main_vt(jnp.maximum(t - 1, 0)), 1 - slot, TQ)
        softmax(slot, TQ, None, None)

    def two_steps(u, carry):
        full_step(2 * u, 0)
        full_step(2 * u + 1, 1)
        return carry

    lax.fori_loop(0, i // 2, two_steps, 0)

    def last_step(slot):
        accumulate(main_vt(jnp.maximum(i - 1, 0)), 1 - slot, TQ)
        softmax(slot, TQ, jnp.where((krow // CHUNK) <= (qcol // CHUNK), 0.0, -jnp.inf),
                jnp.maximum(krow - qcol, 0).astype(F32))
        accumulate(main_vt(i), slot, TQ)

    @pl.when(i % 2 == 1)
    def _odd():
        full_step(i - 1, 0)
        last_step(1)

    @pl.when(i % 2 == 0)
    def _even():
        last_step(0)

    lam = _lambda(lq1_ref, lk1_ref, lq2_ref, lk2_ref)
    for h in range(H_B):
        a = acc_ref[h]
        o0 = a[0:DV_B, 0:TQ] / a[DV_B:DV_B + 1, 0:TQ]
        o1 = a[0:DV_B, TQ:2 * TQ] / a[DV_B:DV_B + 1, TQ:2 * TQ]
        o = _rms((o0 - lam * o1).T, sg_ref[...]) * (1.0 - LAM_INIT)
        o_ref[:, h * DV_B:(h + 1) * DV_B] = o.astype(BF16)


def _diff(qb, kbh, vbt, mkb, mvb, lq1, lk1, lq2, lk2, sg, batch, seq):
    nq = seq // TQ
    qrow = lambda b, i: (b * nq + i, 0)
    qhead = lambda b, i: (0, b * nq + i, 0)
    const2 = lambda b, i: (0, 0)
    const3 = lambda b, i: (0, 0, 0)
    return pl.pallas_call(
        _diff_body,
        out_shape=jax.ShapeDtypeStruct((batch * seq, H_B * DV_B), BF16),
        grid=(batch, nq),
        in_specs=[
            pl.BlockSpec((2 * H_B, TQ, 128), qhead),
            pl.BlockSpec((2 * H_B, seq, 128), lambda b, i: (0, b, 0)),
            pl.BlockSpec((nq, H_B, VT_ROWS, TQ), lambda b, i: (b, 0, 0, 0)),
            pl.BlockSpec((2 * H_B, N_META, 128), const3),
            pl.BlockSpec((N_META, H_B * DV_B), const2),
            pl.BlockSpec((1, DH_B), const2),
            pl.BlockSpec((1, DH_B), const2),
            pl.BlockSpec((1, DH_B), const2),
            pl.BlockSpec((1, DH_B), const2),
            pl.BlockSpec((1, DV_B), const2),
        ],
        out_specs=pl.BlockSpec((TQ, H_B * DV_B), qrow),
        scratch_shapes=[
            pltpu.VMEM((1, 2 * H_B * TQ), F32),
            pltpu.VMEM((2, 1, 2 * H_B * TQ), F32),
            pltpu.VMEM((H_B, VT_ROWS, 2 * TQ), F32),
            pltpu.VMEM((2, TQ, 2 * H_B * TQ), BF16),
            pltpu.VMEM((2, 2 * H_B, TQ, TQ), F32),
            pltpu.VMEM((2, 1, 2 * H_B * TQ), F32),
        ],
        compiler_params=pltpu.CompilerParams(
            dimension_semantics=("arbitrary", "arbitrary"), vmem_limit_bytes=VMEM_LIMIT),
        name="diff",
    )(qb, kbh, vbt, mkb, mvb, lq1, lk1, lq2, lk2, sg)


def _sample_body(qa_ref, qi_ref, sm_ref, nka_ref, nva_ref, nki_ref, qb_ref, nkb_ref, nvb_ref,
                 cak_ref, cav_ref, cai_ref, cbk_ref, cbv_ref,
                 lq1_ref, lk1_ref, lq2_ref, lk2_ref, sg_ref,
                 oa_ref, ob_ref, pc_ref, pn_ref, *, k_top, past, ts):
    pad = LANES - ts
    row_c = lax.broadcasted_iota(I32, (ts, past), 0)
    col_c = lax.broadcasted_iota(I32, (ts, past), 1)
    row_n = lax.broadcasted_iota(I32, (ts, LANES), 0)
    col_n = lax.broadcasted_iota(I32, (ts, LANES), 1)
    new_ok = col_n < ts

    def pad_rows(x):
        return jnp.concatenate([x, jnp.zeros((pad,) + x.shape[1:], x.dtype)], axis=0)

    qi_all = qi_ref[...].reshape(H_I * ts, D_IDX)
    w = sm_ref[:, SM_WI:SM_WI + H_I]
    s4c = _dot(qi_all, cai_ref[...].astype(BF16))
    s4n = _dot_nt(qi_all, pad_rows(nki_ref[...]))
    sc_c = None
    sc_n = None
    for h in range(H_I):
        wh = w[:, h:h + 1]
        tc = jnp.maximum(s4c[h * ts:(h + 1) * ts], 0.0) * wh
        tn = jnp.maximum(s4n[h * ts:(h + 1) * ts], 0.0) * wh
        sc_c = tc if sc_c is None else sc_c + tc
        sc_n = tn if sc_n is None else sc_n + tn
    sc_n = jnp.where(new_ok, sc_n, -jnp.inf)

    def count(pred_c, pred_n):
        c = jnp.sum(jnp.where(pred_c, 1.0, 0.0), axis=1, keepdims=True)
        return c + jnp.sum(jnp.where(pred_n, 1.0, 0.0), axis=1, keepdims=True)

    def threshold(key):
        return _f32_from_key(jnp.clip(key, KEY32_LOWEST, KEY32_HIGHEST))

    def value_step(s, key):
        cand_key = key ^ jnp.left_shift(jnp.int32(1), 31 - s)
        cand = threshold(cand_key)
        cnt = count(sc_c >= cand, sc_n >= cand)
        return jnp.where(cnt >= float(k_top), cand_key, key)

    t = threshold(lax.fori_loop(0, 32, value_step, jnp.full((ts, 1), INT_MIN, I32)))
    need = float(k_top) - count(sc_c > t, sc_n > t)
    idx_n = col_n + past

    def tie_step(s, x):
        cand = x | jnp.left_shift(jnp.int32(1), 11 - s)
        cnt = count((sc_c == t) & (col_c < cand), (sc_n == t) & (idx_n < cand))
        return jnp.where(cnt < need, cand, x)

    x = lax.fori_loop(0, 12, tie_step, jnp.zeros((ts, 1), I32))
    negm_c = jnp.where((sc_c > t) | ((sc_c == t) & (col_c <= x)), 0.0, -jnp.inf)
    negm_n = jnp.where((sc_n > t) | ((sc_n == t) & (idx_n <= x)), 0.0, -jnp.inf)

    dist_c = (row_c - col_c + past).astype(F32)
    dist_n = jnp.abs(row_n - col_n).astype(F32)
    q_all = qa_ref[...][:, :, 0:DH_A].reshape(H_A * ts, DH_A)
    s_c = _dot(q_all, cak_ref[...].astype(BF16))
    s_n = _dot_nt(q_all, pad_rows(nka_ref[...][:, 0:DH_A]))
    for h in range(H_A):
        rows = slice(h * ts, (h + 1) * ts)
        lc = s_c[rows] + (negm_c - (LOG2E * SLOPES_A[h]) * dist_c)
        ln = s_n[rows] + (negm_n - (LOG2E * SLOPES_A[h]) * dist_n)
        m = jnp.maximum(jnp.max(lc, axis=1, keepdims=True), jnp.max(ln, axis=1, keepdims=True))
        m = jnp.maximum(m, NEG_INIT)
        pc_ref[rows, :] = jnp.exp2(lc - m).astype(BF16)
        pn_ref[rows, :] = jnp.exp2(ln - m).astype(BF16)
    vc_aug_t = jnp.concatenate([cav_ref[...].astype(BF16), jnp.ones((64, past), BF16)], axis=0)
    pv = _dot_nt(pc_ref[...], vc_aug_t) + _dot(pn_ref[...], pad_rows(nva_ref[...]))
    pv = pv / pltpu.roll(pv, DH_A, axis=1)
    oa_ref[...] = jnp.concatenate(
        [pv[h * ts:(h + 1) * ts, :DH_A] for h in range(H_A)], axis=1).astype(BF16)

    lam = _lambda(lq1_ref, lk1_ref, lq2_ref, lk2_ref)
    neg_new = jnp.where(new_ok, 0.0, -jnp.inf)
    for h in range(H_B):
        a_c = None
        a_n = None
        for c in range(2):
            hc = 2 * h + c
            q = qb_ref[hc][:, 0:DH_B]
            kc_t = cbk_ref[hc * DH_B:(hc + 1) * DH_B, :].astype(BF16)
            lc = _dot(q, kc_t) - (LOG2E * SLOPES_B[h]) * dist_c
            ln = _dot_nt(q, pad_rows(nkb_ref[hc][:, 0:DH_B])) + (neg_new - (LOG2E * SLOPES_B[h]) * dist_n)
            m = jnp.maximum(jnp.max(lc, axis=1, keepdims=True), jnp.max(ln, axis=1, keepdims=True))
            pc = jnp.exp2(lc - m)
            pn = jnp.exp2(ln - m)
            l = jnp.sum(pc, axis=1, keepdims=True) + jnp.sum(pn, axis=1, keepdims=True)
            pc = pc / l
            pn = pn / l
            if c == 0:
                a_c, a_n = pc, pn
            else:
                a_c, a_n = a_c - lam * pc, a_n - lam * pn
        vc = cbv_ref[pl.ds(h, past, stride=H_B), :].astype(BF16)
        vn = pad_rows(nvb_ref[:, h * DV_B:(h + 1) * DV_B])
        o = _dot(a_c.astype(BF16), vc) + _dot(a_n.astype(BF16), vn)
        o = _rms(o, sg_ref[...]) * (1.0 - LAM_INIT)
        ob_ref[:, h * DV_B:(h + 1) * DV_B] = o.astype(BF16)


def _sample(qa, qi, sm, nka, nva, nki, qb, nkb, nvb, cak, cav, cai, cbk, cbv,
            lq1, lk1, lq2, lk2, sg, batch, ts, past, k_top):
    row = lambda b: (b, 0)
    head = lambda b: (0, b, 0)
    cache = lambda b: (b, 0, 0)
    const = lambda b: (0, 0)
    return pl.pallas_call(
        functools.partial(_sample_body, k_top=k_top, past=past, ts=ts),
        out_shape=(jax.ShapeDtypeStruct((batch * ts, H_A * DH_A), BF16),
                   jax.ShapeDtypeStruct((batch * ts, H_B * DV_B), BF16)),
        grid=(batch,),
        in_specs=[
            pl.BlockSpec((H_A, ts, 128), head),
            pl.BlockSpec((H_I, ts, D_IDX), head),
            pl.BlockSpec((ts, 256), row),
            pl.BlockSpec((ts, 128), row),
            pl.BlockSpec((ts, 128), row),
            pl.BlockSpec((ts, D_IDX), row),
            pl.BlockSpec((2 * H_B, ts, 128), head),
            pl.BlockSpec((2 * H_B, ts, 128), head),
            pl.BlockSpec((ts, H_B * DV_B), row),
            pl.BlockSpec((None, DH_A, past), cache),
            pl.BlockSpec((None, DH_A, past), cache),
            pl.BlockSpec((None, D_IDX, past), cache),
            pl.BlockSpec((None, 2 * H_B * DH_B, past), cache),
            pl.BlockSpec((None, past * H_B, DV_B), cache),
            pl.BlockSpec((1, DH_B), const),
            pl.BlockSpec((1, DH_B), const),
            pl.BlockSpec((1, DH_B), const),
            pl.BlockSpec((1, DH_B), const),
            pl.BlockSpec((1, DV_B), const),
        ],
        out_specs=(pl.BlockSpec((ts, H_A * DH_A), row), pl.BlockSpec((ts, H_B * DV_B), row)),
        scratch_shapes=[
            pltpu.VMEM((H_A * ts, past), BF16),
            pltpu.VMEM((H_A * ts, LANES), BF16),
        ],
        compiler_params=pltpu.CompilerParams(
            dimension_semantics=("arbitrary",), vmem_limit_bytes=VMEM_LIMIT),
        name="sample",
    )(qa, qi, sm, nka, nva, nki, qb, nkb, nvb, cak, cav, cai, cbk, cbv, lq1, lk1, lq2, lk2, sg)


def _finish_body(x_ref, oa_ref, ob_ref, wo_ref, gm_ref, wu_ref, wd_ref, gf_ref, y_ref):
    o = jnp.concatenate([oa_ref[...], ob_ref[...]], axis=1)
    h1 = x_ref[...] + _dot(o, wo_ref[...])
    hn = _rms(h1, gm_ref[...]).astype(BF16)
    acc = h1
    for c in range(D_FF // D_MODEL):
        u = jnp.maximum(_dot(hn, wu_ref[:, c * D_MODEL:(c + 1) * D_MODEL]), 0.0)
        acc = acc + _dot((u * u).astype(BF16), wd_ref[c * D_MODEL:(c + 1) * D_MODEL, :])
    y_ref[...] = _rms(acc, gf_ref[...])


def _finish(x, oa, ob, wo, gm, wu, wd, gf, tm):
    m = x.shape[0]
    assert m % tm == 0
    row = lambda i: (i, 0)
    const = lambda i: (0, 0)
    resident = dict(pipeline_mode=pl.Buffered(1))
    return pl.pallas_call(
        _finish_body,
        out_shape=jax.ShapeDtypeStruct((m, D_MODEL), F32),
        grid=(m // tm,),
        in_specs=[
            pl.BlockSpec((tm, D_MODEL), row),
            pl.BlockSpec((tm, 512), row),
            pl.BlockSpec((tm, 512), row),
            pl.BlockSpec((D_MODEL, D_MODEL), const, **resident),
            pl.BlockSpec((1, D_MODEL), const),
            pl.BlockSpec((D_MODEL, D_FF), const, **resident),
            pl.BlockSpec((D_FF, D_MODEL), const, **resident),
            pl.BlockSpec((1, D_MODEL), const),
        ],
        out_specs=pl.BlockSpec((tm, D_MODEL), row),
        compiler_params=pltpu.CompilerParams(
            dimension_semantics=("arbitrary",), vmem_limit_bytes=VMEM_LIMIT),
        name="finish",
    )(x, oa, ob, wo, gm, wu, wd, gf)


def kernel(x_prompt, x_sample, cache_a_k, cache_a_v, cache_a_idx_k, cache_b_k, cache_b_v,
           meta_tokens, attn_norm_g, w_in, idx_k_norm_g, idx_k_norm_b,
           lambda_q1, lambda_k1, lambda_q2, lambda_k2, subln_g, w_o,
           mlp_norm_g, w_up, w_down, final_norm_g):
    batch, seq, _ = x_prompt.shape
    dec_batch, ts, _ = x_sample.shape
    past = cache_a_k.shape[2]
    assert attn_norm_g.shape[0] == 1, "single-layer step"
    assert seq % TQ == 0 and ts == 16 and past % LANES == 0 and past // CHUNK == (past + ts - 1) // CHUNK
    n = N_META + seq
    k_top_p = min(TOPK_MAX, seq // 4)
    k_top_s = min(TOPK_MAX, (past + ts) // 4)

    w_t = w_in[0].T.astype(BF16)
    w = jnp.concatenate(
        [w_t[0:512],
         w_t[964:1476],
         w_t[1476:1988],
         w_t[1988:2500],
         w_t[640:896],
         w_t[512:640],
         w_t[896:964]],
        axis=0)
    w = jnp.pad(w, ((0, W_COLS - 2500), (0, 0))).T
    wvt = w_t[1476:2500]
    wo = w_o[0].astype(BF16)
    wu = w_up[0].astype(BF16)
    wd = w_down[0].astype(BF16)
    g_attn = attn_norm_g[0][None]
    g_mlp = mlp_norm_g[0][None]
    g_fin = final_norm_g[None]
    kng = idx_k_norm_g[0][None]
    knb = idx_k_norm_b[0][None]
    lq1, lk1, lq2, lk2 = lambda_q1[0][None], lambda_k1[0][None], lambda_q2[0][None], lambda_k2[0][None]
    sg = subln_g[0][None]

    xp = x_prompt.reshape(batch * seq, D_MODEL)
    xs = x_sample.reshape(dec_batch * ts, D_MODEL)

    (qa_p, qi_p, qb_p, kbh_p, vb_p, kab_p, kib_p, vt_p, wt_p, vbt_p, smt_p, kbt_p) = _proj(
        xp, g_attn, w, wvt, kng, knb, 512, N_META, seq, True)
    (_, _, _, kbh_m, vb_m, kab_m, kib_m, kb_m, vbb_m, sm_m, vaa_m) = _proj(
        meta_tokens, g_attn, w, wvt, kng, knb, N_META, 0, N_META, False)
    (qa_s, qi_s, qb_s, kbh_s, vb_s, kab_s, kib_s, kb_s, vbb_s, sm_s, vaa_s) = _proj(
        xs, g_attn, w, wvt, kng, knb, dec_batch * ts, past, dec_batch * ts, False)

    oa_p = _dsa(qa_p, qi_p, wt_p, kab_p, kib_p, vt_p, kab_m, kib_m, vaa_m, batch, seq, k_top_p)
    ob_p = _diff(qb_p, kbh_p, vbt_p, kbh_m, vbb_m, lq1, lk1, lq2, lk2, sg, batch, seq)
    y_prompt = _finish(xp, oa_p, ob_p, wo, g_mlp, wu, wd, g_fin, 512).reshape(batch, seq, D_MODEL)

    oa_s, ob_s = _sample(
        qa_s, qi_s, sm_s, kab_s, vaa_s, kib_s, qb_s, kbh_s, vbb_s,
        jnp.swapaxes(cache_a_k[0], 1, 2), jnp.swapaxes(cache_a_v[0], 1, 2),
        jnp.swapaxes(cache_a_idx_k[0], 1, 2),
        jnp.transpose(cache_b_k[0], (0, 2, 3, 4, 1)).reshape(dec_batch, 2 * H_B * DH_B, past),
        cache_b_v[0].reshape(dec_batch, past * H_B, DV_B),
        lq1, lk1, lq2, lk2, sg, dec_batch, ts, past, k_top_s)
    y_sample = _finish(xs, oa_s, ob_s, wo, g_mlp, wu, wd, g_fin, dec_batch * ts).reshape(
        dec_batch, ts, D_MODEL)

    def with_meta_t(meta_rows, main_t):
        meta_b = jnp.broadcast_to(meta_rows.T[None], (batch, 64, N_META))
        return jnp.swapaxes(jnp.concatenate([meta_b, main_t], axis=2), 1, 2)[None]

    new_a_k_p = with_meta_t(sm_m[:, SM_KA:SM_KA + 64], smt_p[:, SM_KA:SM_KA + 64])
    new_a_v_p = with_meta_t(sm_m[:, SM_VA:SM_VA + 64], smt_p[:, SM_VA:SM_VA + 64])
    new_a_i_p = with_meta_t(sm_m[:, SM_KI:SM_KI + 64], smt_p[:, SM_KI:SM_KI + 64])
    kb_meta_t = jnp.broadcast_to(kb_m.T.reshape(1, H_B, 2, DH_B, N_META), (batch, H_B, 2, DH_B, N_META))
    kb_main_t = kbt_p.reshape(batch, H_B, 2, DH_B, seq)
    new_b_k_p = jnp.transpose(
        jnp.concatenate([kb_meta_t, kb_main_t], axis=4), (0, 4, 1, 2, 3))[None]
    new_b_v_p = jnp.concatenate(
        [jnp.broadcast_to(vb_m.reshape(1, N_META, H_B, DV_B), (batch, N_META, H_B, DV_B)),
         vb_p.reshape(batch, seq, H_B, DV_B)], axis=1)[None]
    new_a_k_s = sm_s[:, SM_KA:SM_KA + 64].reshape(1, dec_batch, ts, 64)
    new_a_v_s = sm_s[:, SM_VA:SM_VA + 64].reshape(1, dec_batch, ts, 64)
    new_a_i_s = sm_s[:, SM_KI:SM_KI + 64].reshape(1, dec_batch, ts, 64)
    new_b_k_s = kb_s.reshape(1, dec_batch, ts, H_B, 2, DH_B)
    new_b_v_s = vb_s.reshape(1, dec_batch, ts, H_B, DV_B)
    return (y_prompt, y_sample, new_a_k_p, new_a_v_p, new_a_i_p, new_b_k_p, new_b_v_p,
            new_a_k_s, new_a_v_s, new_a_i_s, new_b_k_s, new_b_v_s)
```

```python
import functools

import jax
import jax.numpy as jnp
import numpy as np
from jax import lax
from jax.experimental import pallas as pl
from jax.experimental.pallas import tpu as pltpu

F32 = jnp.float32
BF16 = jnp.bfloat16
I32 = jnp.int32
I16 = jnp.int16
HALF16 = 32768

D_MODEL = 1024
CHUNK = 64
N_META = 16
H_A = 8
DH_A = 64
H_I = 4
D_IDX = 64
TOPK_MAX = 256
H_B = 4
DH_B = 64
DV_B = 2 * DH_B
D_FF = 4 * D_MODEL
EPS = 1e-6
LAM_INIT = 0.2

LANES = 128
C_QA, C_QB, C_KB, C_VB, C_QI, C_SM = 0, 512, 1024, 1536, 2048, 2304
W_COLS = 2560
SM_KA, SM_VA, SM_KI, SM_WI = 0, 64, 128, 192

LOG2E = float(np.float32(1.4426950408889634))
Q_SCALE = (DH_A ** -0.5) * LOG2E
N_SLOPE_PIECES = 3


def _bf16_pieces(value):
    rest = np.float32(value)
    pieces = []
    for _ in range(N_SLOPE_PIECES):
        piece = np.float32(rest.astype(jnp.bfloat16))
        pieces.append(float(piece))
        rest = np.float32(rest - piece)
    assert rest == 0.0
    return pieces
W_SCALE = (H_I ** -0.5) * (D_IDX ** -0.5)
SLOPES_A = tuple(2.0 ** (-8.0 * (i + 1) / H_A) for i in range(H_A))
SLOPES_B = tuple(2.0 ** (-8.0 * (i + 1) / H_B) for i in range(H_B))

INT_MIN = -(2 ** 31)
NEG_INIT = -1e30
TQ = 256
ONES_ROWS = 16
VT_ROWS = DV_B + ONES_ROWS
VTA_ROWS = DH_A + ONES_ROWS
CNT_ROWS = 32
VMEM_LIMIT = 56 * 1024 * 1024

_NT = (((1,), (1,)), ((), ()))


def _dot(a, b):
    return jnp.dot(a, b, preferred_element_type=F32)


def _dot_nt(a, b):
    return lax.dot_general(a, b, _NT, preferred_element_type=F32)


def _rms(x, g):
    return (x * lax.rsqrt(jnp.mean(x * x, axis=-1, keepdims=True) + EPS)) * g


def _to_key(score):
    b = lax.bitcast_convert_type(score, I32)
    return b ^ ((b >> 31) & jnp.int32(0x7FFFFFFF))


def _f32_from_key(key):
    return lax.bitcast_convert_type(key ^ ((key >> 31) & jnp.int32(0x7FFFFFFF)), F32)


def _bf16_from_key(key):
    bits = key ^ ((key >> 15) & jnp.int32(0x7FFF))
    return lax.bitcast_convert_type(jnp.left_shift(bits, 16), F32).astype(BF16)


KEY16_LOWEST = -32640
KEY32_LOWEST = -2139095040
KEY32_HIGHEST = 2139095039


def _wide(x, width):
    reps = width // LANES
    return x if reps == 1 else jnp.concatenate([x] * reps, axis=1)


def _proj_body(x_ref, g_ref, w_ref, wvt_ref, kng_ref, knb_ref,
               qa_ref, qi_ref, qb_ref, kbh_ref, vb_ref, kab_ref, kib_ref, *mode_refs,
               pos0, period, key_major):
    x = x_ref[...]
    hn = _rms(x, g_ref[...]).astype(BF16)
    rows = x.shape[0]

    def mm(lo, width):
        return _dot(hn, w_ref[:, lo:lo + width])

    lane = lax.broadcasted_iota(I32, (rows, 64), 1)
    r = pl.program_id(0) * rows + lax.broadcasted_iota(I32, (rows, 64), 0)
    pos = pos0 + r % period
    pos_cols = jnp.where(lane >= 2 * N_SLOPE_PIECES, 0,
                         jnp.where(lane % 2 == 0, (pos // 256) * 256, pos % 256))
    pos_cols = pos_cols.astype(F32).astype(BF16)

    def slope_cols(slope):
        cols = jnp.zeros((rows, 64), F32)
        for n, piece in enumerate(_bf16_pieces(slope * LOG2E)):
            cols = jnp.where(lane // 2 == n, piece, cols)
        return cols.astype(BF16)

    z = mm(C_QA, 512) * Q_SCALE
    for h in range(H_A):
        qa_ref[h] = jnp.concatenate(
            [z[:, h * DH_A:(h + 1) * DH_A].astype(BF16), slope_cols(SLOPES_A[h])], axis=1)
    z = mm(C_QB, 512) * Q_SCALE
    for hc in range(2 * H_B):
        qb_ref[hc] = jnp.concatenate(
            [z[:, hc * DH_B:(hc + 1) * DH_B].astype(BF16), slope_cols(SLOPES_B[hc // 2])], axis=1)
    if key_major:
        vt_ref, wt_ref, vbt_ref, smt_ref, kbt_ref = mode_refs
    else:
        kb_ref, vbb_ref, sm_ref, vaa_ref = mode_refs
    z = mm(C_KB, 512)
    if not key_major:
        kb_ref[...] = z
    for hc in range(2 * H_B):
        kbh_ref[hc] = jnp.concatenate([z[:, hc * DH_B:(hc + 1) * DH_B].astype(BF16), pos_cols], axis=1)
    z = mm(C_VB, 512)
    for h in range(H_B):
        vb_ref[pl.ds(h, rows, stride=H_B), :] = z[:, h * DV_B:(h + 1) * DV_B]
    if not key_major:
        vbb_ref[...] = z.astype(BF16)
    z = mm(C_QI, 256)
    for h in range(H_I):
        qi_ref[h] = z[:, h * D_IDX:(h + 1) * D_IDX].astype(BF16)
    z = mm(C_SM, 256)
    ka = z[:, SM_KA:SM_KA + 64]
    va = z[:, SM_VA:SM_VA + 64]
    ki = z[:, SM_KI:SM_KI + 64]
    xc = ki - jnp.mean(ki, axis=-1, keepdims=True)
    ki = xc * lax.rsqrt(jnp.mean(xc * xc, axis=-1, keepdims=True) + EPS) * kng_ref[...] + knb_ref[...]
    kab_ref[...] = jnp.concatenate([ka.astype(BF16), pos_cols], axis=1)
    kib_ref[...] = ki.astype(BF16)
    if not key_major:
        sm_ref[:, 0:128] = z[:, 0:128]
        sm_ref[:, SM_KI:SM_KI + 64] = ki
        sm_ref[:, SM_WI:SM_WI + 64] = z[:, SM_WI:SM_WI + 64] * W_SCALE
        vaa_ref[...] = jnp.concatenate([va.astype(BF16), jnp.ones((rows, 64), BF16)], axis=1)
    else:
        kv_t = z[:, 0:128].T
        for c in range(rows // TQ):
            vt_ref[c] = jnp.concatenate(
                [kv_t[64:128, c * TQ:(c + 1) * TQ].astype(BF16), jnp.ones((ONES_ROWS, TQ), BF16)], axis=0)
        kw_t = jnp.concatenate([ki, z[:, SM_WI:SM_WI + 64] * W_SCALE], axis=1).T
        wt_ref[...] = kw_t[64:72, :]
        smt_ref[0:128, :] = kv_t
        smt_ref[128:192, :] = kw_t[0:64, :]
        kbt_ref[...] = _dot_nt(wvt_ref[0:512, :], hn)
        vb_t = _dot_nt(wvt_ref[512:1024, :], hn)
        for c in range(rows // TQ):
            for h in range(H_B):
                vbt_ref[c, h] = jnp.concatenate(
                    [vb_t[h * DV_B:(h + 1) * DV_B, c * TQ:(c + 1) * TQ].astype(BF16),
                     jnp.ones((VT_ROWS - DV_B, TQ), BF16)], axis=0)


def _proj(x, g, w, wvt, kng, knb, tm, pos0, period, key_major):
    m = x.shape[0]
    assert m % tm == 0
    row = lambda i: (i, 0)
    head = lambda i: (0, i, 0)
    const = lambda i: (0, 0)
    out_shape = (
        jax.ShapeDtypeStruct((H_A, m, 128), BF16),
        jax.ShapeDtypeStruct((H_I, m, D_IDX), BF16),
        jax.ShapeDtypeStruct((2 * H_B, m, 128), BF16),
        jax.ShapeDtypeStruct((2 * H_B, m, 128), BF16),
        jax.ShapeDtypeStruct((m * H_B, DV_B), F32),
        jax.ShapeDtypeStruct((m, 128), BF16),
        jax.ShapeDtypeStruct((m, 64), BF16),
    )
    out_specs = (
        pl.BlockSpec((H_A, tm, 128), head),
        pl.BlockSpec((H_I, tm, D_IDX), head),
        pl.BlockSpec((2 * H_B, tm, 128), head),
        pl.BlockSpec((2 * H_B, tm, 128), head),
        pl.BlockSpec((tm * H_B, DV_B), row),
        pl.BlockSpec((tm, 128), row),
        pl.BlockSpec((tm, 64), row),
    )
    if not key_major:
        out_shape += (
            jax.ShapeDtypeStruct((m, 512), F32),
            jax.ShapeDtypeStruct((m, 512), BF16),
            jax.ShapeDtypeStruct((m, 256), F32),
            jax.ShapeDtypeStruct((m, 128), BF16),
        )
        out_specs += (
            pl.BlockSpec((tm, 512), row),
            pl.BlockSpec((tm, 512), row),
            pl.BlockSpec((tm, 256), row),
            pl.BlockSpec((tm, 128), row),
        )
    else:
        assert tm % TQ == 0 and period % tm == 0 and m % period == 0
        per_stream = period // tm
        stream_cols = lambda i: (i // per_stream, 0, i % per_stream)
        out_shape += (
            jax.ShapeDtypeStruct((m // TQ, VTA_ROWS, TQ), BF16),
            jax.ShapeDtypeStruct((8, m), F32),
            jax.ShapeDtypeStruct((m // TQ, H_B, VT_ROWS, TQ), BF16),
            jax.ShapeDtypeStruct((m // period, 192, period), F32),
            jax.ShapeDtypeStruct((m // period, 512, period), F32),
        )
        out_specs += (
            pl.BlockSpec((tm // TQ, VTA_ROWS, TQ), lambda i: (i, 0, 0)),
            pl.BlockSpec((8, tm), lambda i: (0, i)),
            pl.BlockSpec((tm // TQ, H_B, VT_ROWS, TQ), lambda i: (i, 0, 0, 0)),
            pl.BlockSpec((None, 192, tm), stream_cols),
            pl.BlockSpec((None, 512, tm), stream_cols),
        )
    return pl.pallas_call(
        functools.partial(_proj_body, pos0=pos0, period=period, key_major=key_major),
        out_shape=out_shape,
        grid=(m // tm,),
        in_specs=[
            pl.BlockSpec((tm, D_MODEL), row),
            pl.BlockSpec((1, D_MODEL), const),
            pl.BlockSpec((D_MODEL, W_COLS), const),
            pl.BlockSpec((2 * H_B * DV_B, D_MODEL), const),
            pl.BlockSpec((1, D_IDX), const),
            pl.BlockSpec((1, D_IDX), const),
        ],
        out_specs=out_specs,
        compiler_params=pltpu.CompilerParams(
            dimension_semantics=("arbitrary",), vmem_limit_bytes=VMEM_LIMIT),
        name="proj",
    )(x, g, w, wvt, kng, knb)


def _dsa_body(qa_ref, qi_ref, wt_ref, ka_ref, ki_ref, vt_ref, mka_ref, mki_ref, mva_ref,
              o_ref, smeta_ref, smain_ref, bmeta_ref, bmain_ref,
              tsel_ref, m_ref, al_ref, acc_ref, p_ref, s_ref, mb_ref, seen_ref, *, k_top):
    i = pl.program_id(1)
    krow = lax.broadcasted_iota(I32, (TQ, TQ), 0)
    qcol = lax.broadcasted_iota(I32, (TQ, TQ), 1)

    def main_rows(j):
        return pl.ds(pl.multiple_of(j * TQ, TQ), TQ)

    qi_all = qi_ref[...].reshape(H_I * TQ, D_IDX)

    def scores(ki_blk):
        s4 = _dot_nt(ki_blk, qi_all)
        sc = None
        for h in range(H_I):
            t = jnp.maximum(s4[:, h * TQ:(h + 1) * TQ], 0.0) * wt_ref[h:h + 1, :]
            sc = t if sc is None else sc + t
        return sc

    sc = scores(mki_ref[...])
    smeta_ref[...] = sc
    bmeta_ref[...] = sc.astype(BF16)

    def score_block(j, sc):
        smain_ref[j] = sc
        bmain_ref[j] = sc.astype(BF16)

    def full_scores(j):
        return scores(ki_ref[main_rows(j), :])

    def last_scores():
        return jnp.where((krow // CHUNK) <= (qcol // CHUNK), full_scores(i), -jnp.inf)

    def score_two(u, carry):
        sc_a, sc_b = full_scores(2 * u), full_scores(2 * u + 1)
        score_block(2 * u, sc_a)
        score_block(2 * u + 1, sc_b)
        return carry

    lax.fori_loop(0, i // 2, score_two, 0)

    @pl.when(i % 2 == 1)
    def _odd_scores():
        sc_a, sc_b = full_scores(i - 1), last_scores()
        score_block(i - 1, sc_a)
        score_block(i, sc_b)

    @pl.when(i % 2 == 0)
    def _even_scores():
        score_block(i, last_scores())

    def count(meta_ref, main_ref, part, pred):
        def one(j):
            return part(pred(main_ref[j]))

        def two(u, c):
            return c + (one(2 * u) + one(2 * u + 1))

        c = lax.fori_loop(0, (i + 1) // 2, two, jnp.zeros((CNT_ROWS, TQ), F32))
        c = lax.cond(i % 2 == 0, lambda c: c + one(i), lambda c: c, c)
        cm = part(pred(meta_ref[...]), rows=N_META)
        return jnp.sum(c, axis=0, keepdims=True) + jnp.sum(cm, axis=0, keepdims=True)

    def part32(hit, rows=CNT_ROWS):
        ones = jnp.where(hit, 1.0, 0.0)
        return jnp.sum(ones.reshape(ones.shape[0] // rows, rows, TQ), axis=0)

    def part16(hit, rows=CNT_ROWS):
        ones = jnp.where(hit, jnp.bfloat16(1), jnp.bfloat16(0))
        acc = ones[0:rows]
        for t in range(1, ones.shape[0] // rows):
            acc = acc + ones[t * rows:(t + 1) * rows]
        return acc.astype(F32)

    kf = float(k_top)

    def step16(s, u):
        cand_u = u | jnp.left_shift(jnp.int32(1), 15 - s)
        cand = _bf16_from_key(jnp.maximum(cand_u, KEY16_LOWEST + HALF16) - HALF16)
        cnt = count(bmeta_ref, bmain_ref, part16, lambda x: x >= cand)
        return jnp.where(cnt >= kf, cand_u, u)

    u_hi = lax.fori_loop(0, 16, step16, jnp.zeros((1, TQ), I32))
    t_hi = _bf16_from_key(jnp.maximum(u_hi, KEY16_LOWEST + HALF16) - HALF16).astype(F32)
    base = jnp.maximum(_to_key(t_hi) - 65536, KEY32_LOWEST)

    def step32(s, d):
        cand_d = d | jnp.left_shift(jnp.int32(1), 16 - s)
        cand = _f32_from_key(jnp.minimum(base + cand_d, KEY32_HIGHEST))
        cnt = count(smeta_ref, smain_ref, part32, lambda x: x >= cand)
        return jnp.where(cnt >= kf, cand_d, d)

    d_lo = lax.fori_loop(0, 17, step32, jnp.zeros((1, TQ), I32))
    t_sel = _f32_from_key(jnp.minimum(base + d_lo, KEY32_HIGHEST))
    need = kf - count(smeta_ref, smain_ref, part32, lambda x: x > t_sel)
    tsel_ref[...] = jnp.broadcast_to(t_sel, tsel_ref.shape)

    m_ref[...] = jnp.full(m_ref.shape, NEG_INIT, F32)
    acc_ref[...] = jnp.zeros(acc_ref.shape, F32)
    thr = tsel_ref[0:1, :]

    def masked_logits(k_blk, sc, slot):
        nk = sc.shape[0]
        tie = sc == thr
        tri = (lax.broadcasted_iota(I32, (nk, nk), 1)
               <= lax.broadcasted_iota(I32, (nk, nk), 0)).astype(BF16)
        rank = _dot(tri, jnp.where(tie, 1.0, 0.0).astype(BF16)) + seen_ref[...]
        seen_ref[...] = rank[nk - 1:nk, :]
        negm = jnp.where(sc > thr, 0.0, jnp.where(tie, jnp.where(rank <= need, 0.0, -jnp.inf), -jnp.inf))
        for h in range(H_A):
            cols = slice(h * TQ, (h + 1) * TQ)
            s = _dot_nt(k_blk, qa_ref[h]) + negm
            s_ref[slot, 0:nk, cols] = s
            mb_ref[slot, :, cols] = jnp.max(s, axis=0, keepdims=True)

    def softmax(slot, nk, future):
        for h in range(H_A):
            cols = slice(h * TQ, (h + 1) * TQ)
            s = s_ref[slot, 0:nk, cols]
            if future is not None:
                s = s - (2.0 * LOG2E * SLOPES_A[h]) * future
            m_old = m_ref[:, cols]
            m_new = jnp.maximum(m_old, mb_ref[slot, :, cols])
            p_ref[slot, 0:nk, cols] = jnp.exp2(s - m_new).astype(BF16)
            al_ref[slot, :, cols] = jnp.exp2(m_old - m_new)
            m_ref[:, cols] = m_new

    def accumulate(vt_blk, slot, nk):
        acc_ref[...] = acc_ref[...] * al_ref[slot] + _dot(vt_blk, p_ref[slot, 0:nk, :])

    eye = (lax.broadcasted_iota(I32, (128, 128), 0)
           == lax.broadcasted_iota(I32, (128, 128), 1)).astype(BF16)
    seen_ref[...] = jnp.zeros(seen_ref.shape, F32)
    masked_logits(mka_ref[...], smeta_ref[...], 1)
    masked_logits(ka_ref[main_rows(0), :], smain_ref[0], 0)
    softmax(1, N_META, None)
    accumulate(_dot_nt(eye, mva_ref[...])[0:VTA_ROWS].astype(BF16), 1, N_META)

    p_ref[1] = jnp.zeros(p_ref.shape[1:], BF16)
    al_ref[1] = jnp.ones(al_ref.shape[1:], F32)

    def full_step(t, slot):
        masked_logits(ka_ref[main_rows(t + 1), :], smain_ref[t + 1], 1 - slot)
        accumulate(vt_ref[jnp.maximum(t - 1, 0)], 1 - slot, TQ)
        softmax(slot, TQ, None)

    def two_steps(u, carry):
        full_step(2 * u, 0)
        full_step(2 * u + 1, 1)
        return carry

    lax.fori_loop(0, i // 2, two_steps, 0)

    def last_step(slot):
        accumulate(vt_ref[jnp.maximum(i - 1, 0)], 1 - slot, TQ)
        softmax(slot, TQ, jnp.maximum(krow - qcol, 0).astype(F32))
        accumulate(vt_ref[i], slot, TQ)

    @pl.when(i % 2 == 1)
    def _odd():
        full_step(i - 1, 0)
        last_step(1)

    @pl.when(i % 2 == 0)
    def _even():
        last_step(0)

    def head_out_t(h):
        a = acc_ref[:, h * TQ:(h + 1) * TQ]
        return a[0:DH_A] / a[DH_A:DH_A + 1]

    for h in range(0, H_A, 2):
        pair_t = jnp.concatenate([head_out_t(h), head_out_t(h + 1)], axis=0)
        o_ref[:, h * DH_A:(h + 2) * DH_A] = pair_t.T.astype(BF16)


def _dsa(qa, qi, wt, kab, kib, vt, mka, mki, mva, batch, seq, k_top):
    nq = seq // TQ
    qrow = lambda b, i: (b * nq + i, 0)
    qhead = lambda b, i: (0, b * nq + i, 0)
    kv = lambda b, i: (b, 0)
    const = lambda b, i: (0, 0)
    return pl.pallas_call(
        functools.partial(_dsa_body, k_top=k_top),
        out_shape=jax.ShapeDtypeStruct((batch * seq, H_A * DH_A), BF16),
        grid=(batch, nq),
        in_specs=[
            pl.BlockSpec((H_A, TQ, 128), qhead),
            pl.BlockSpec((H_I, TQ, D_IDX), qhead),
            pl.BlockSpec((8, TQ), lambda b, i: (0, b * nq + i)),
            pl.BlockSpec((seq, 128), kv),
            pl.BlockSpec((seq, D_IDX), kv),
            pl.BlockSpec((nq, VTA_ROWS, TQ), lambda b, i: (b, 0, 0)),
            pl.BlockSpec((N_META, 128), const),
            pl.BlockSpec((N_META, D_IDX), const),
            pl.BlockSpec((N_META, 128), const),
        ],
        out_specs=pl.BlockSpec((TQ, H_A * DH_A), qrow),
        scratch_shapes=[
            pltpu.VMEM((N_META, TQ), F32),
            pltpu.VMEM((nq, TQ, TQ), F32),
            pltpu.VMEM((N_META, TQ), BF16),
            pltpu.VMEM((nq, TQ, TQ), BF16),
            pltpu.VMEM((8, TQ), F32),
            pltpu.VMEM((1, H_A * TQ), F32),
            pltpu.VMEM((2, 1, H_A * TQ), F32),
            pltpu.VMEM((VTA_ROWS, H_A * TQ), F32),
            pltpu.VMEM((2, TQ, H_A * TQ), BF16),
            pltpu.VMEM((2, TQ, H_A * TQ), F32),
            pltpu.VMEM((2, 1, H_A * TQ), F32),
            pltpu.VMEM((1, TQ), F32),
        ],
        compiler_params=pltpu.CompilerParams(
            dimension_semantics=("arbitrary", "arbitrary"), vmem_limit_bytes=VMEM_LIMIT),
        name="dsa",
    )(qa, qi, wt, kab, kib, vt, mka, mki, mva)


def _lambda(lq1_ref, lk1_ref, lq2_ref, lk2_ref):
    s1 = jnp.sum(lq1_ref[...] * lk1_ref[...], axis=-1, keepdims=True)
    s2 = jnp.sum(lq2_ref[...] * lk2_ref[...], axis=-1, keepdims=True)
    return jnp.exp(s1) - jnp.exp(s2) + LAM_INIT


def _diff_body(qb_ref, kb_ref, vt_ref, mkb_ref, mvb_ref, lq1_ref, lk1_ref, lq2_ref, lk2_ref,
               sg_ref, o_ref, m_ref, al_ref, acc_ref, p_ref, s_ref, mb_ref):
    i = pl.program_id(1)
    n_hc = 2 * H_B
    krow = lax.broadcasted_iota(I32, (TQ, TQ), 0)
    qcol = lax.broadcasted_iota(I32, (TQ, TQ), 1)
    m_ref[...] = jnp.full(m_ref.shape, NEG_INIT, F32)
    acc_ref[...] = jnp.zeros(acc_ref.shape, F32)

    def main_rows(j):
        return pl.ds(pl.multiple_of(j * TQ, TQ), TQ)

    def logits(k_of, slot, nk):
        for hc in range(n_hc):
            s = _dot_nt(k_of(hc), qb_ref[hc])
            s_ref[slot, hc, 0:nk, :] = s
            mb_ref[slot, :, hc * TQ:(hc + 1) * TQ] = jnp.max(s, axis=0, keepdims=True)

    def softmax(slot, nk, negm, future):
        for hc in range(n_hc):
            cols = slice(hc * TQ, (hc + 1) * TQ)
            s = s_ref[slot, hc, 0:nk, :]
            m_old = m_ref[:, cols]
            if negm is None:
                m_new = jnp.maximum(m_old, mb_ref[slot, :, cols])
            else:
                s = (s + negm) - (2.0 * LOG2E * SLOPES_B[hc // 2]) * future
                m_new = jnp.maximum(m_old, jnp.max(s, axis=0, keepdims=True))
            p_ref[slot, 0:nk, cols] = jnp.exp2(s - m_new).astype(BF16)
            al_ref[slot, :, cols] = jnp.exp2(m_old - m_new)
            m_ref[:, cols] = m_new

    def accumulate(vt_of, slot, nk):
        for h in range(H_B):
            cols = slice(2 * h * TQ, (2 * h + 2) * TQ)
            acc_ref[h] = acc_ref[h] * al_ref[slot, :, cols] + _dot(vt_of(h), p_ref[slot, 0:nk, cols])

    eye = (lax.broadcasted_iota(I32, (DV_B, DV_B), 0)
           == lax.broadcasted_iota(I32, (DV_B, DV_B), 1)).astype(BF16)

    def meta_vt(h):
        v_t = _dot_nt(eye, mvb_ref[:, h * DV_B:(h + 1) * DV_B]).astype(BF16)
        return jnp.concatenate([v_t, jnp.ones((VT_ROWS - DV_B, N_META), BF16)], axis=0)

    logits(lambda hc: mkb_ref[hc], 1, N_META)
    logits(lambda hc: kb_ref[hc, main_rows(0), :], 0, TQ)
    softmax(1, N_META, None, None)
    accumulate(meta_vt, 1, N_META)

    p_ref[1] = jnp.zeros(p_ref.shape[1:], BF16)
    al_ref[1] = jnp.ones(al_ref.shape[1:], F32)

    def main_vt(t):
        return lambda h: vt_ref[t, h]

    def full_step(t, slot):
        logits(lambda hc: kb_ref[hc, main_rows(t + 1), :], 1 - slot, TQ)
        accumulate(main_vt(jnp.maximum(t - 1, 0)), 1 - slot, TQ)
        softmax(slot, TQ, None, None)

    def two_steps(u, carry):
        full_step(2 * u, 0)
        full_step(2 * u + 1, 1)
        return carry

    lax.fori_loop(0, i // 2, two_steps, 0)

    def last_step(slot):
        accumulate(main_vt(jnp.maximum(i - 1, 0)), 1 - slot, TQ)
        softmax(slot, TQ, jnp.where((krow // CHUNK) <= (qcol // CHUNK), 0.0, -jnp.inf),
                jnp.maximum(krow - qcol, 0).astype(F32))
        accumulate(main_vt(i), slot, TQ)

    @pl.when(i % 2 == 1)
    def _odd():
        full_step(i - 1, 0)
        last_step(1)

    @pl.when(i % 2 == 0)
    def _even():
        last_step(0)

    lam = _lambda(lq1_ref, lk1_ref, lq2_ref, lk2_ref)
    for h in range(H_B):
        a = acc_ref[h]
        o0 = a[0:DV_B, 0:TQ] / a[DV_B:DV_B + 1, 0:TQ]
        o1 = a[0:DV_B, TQ:2 * TQ] / a[DV_B:DV_B + 1, TQ:2 * TQ]
        o = _rms((o0 - lam * o1).T, sg_ref[...]) * (1.0 - LAM_INIT)
        o_ref[:, h * DV_B:(h + 1) * DV_B] = o.astype(BF16)


def _diff(qb, kbh, vbt, mkb, mvb, lq1, lk1, lq2, lk2, sg, batch, seq):
    nq = seq // TQ
    qrow = lambda b, i: (b * nq + i, 0)
    qhead = lambda b, i: (0, b * nq + i, 0)
    const2 = lambda b, i: (0, 0)
    const3 = lambda b, i: (0, 0, 0)
    return pl.pallas_call(
        _diff_body,
        out_shape=jax.ShapeDtypeStruct((batch * seq, H_B * DV_B), BF16),
        grid=(batch, nq),
        in_specs=[
            pl.BlockSpec((2 * H_B, TQ, 128), qhead),
            pl.BlockSpec((2 * H_B, seq, 128), lambda b, i: (0, b, 0)),
            pl.BlockSpec((nq, H_B, VT_ROWS, TQ), lambda b, i: (b, 0, 0, 0)),
            pl.BlockSpec((2 * H_B, N_META, 128), const3),
            pl.BlockSpec((N_META, H_B * DV_B), const2),
            pl.BlockSpec((1, DH_B), const2),
            pl.BlockSpec((1, DH_B), const2),
            pl.BlockSpec((1, DH_B), const2),
            pl.BlockSpec((1, DH_B), const2),
            pl.BlockSpec((1, DV_B), const2),
        ],
        out_specs=pl.BlockSpec((TQ, H_B * DV_B), qrow),
        scratch_shapes=[
            pltpu.VMEM((1, 2 * H_B * TQ), F32),
            pltpu.VMEM((2, 1, 2 * H_B * TQ), F32),
            pltpu.VMEM((H_B, VT_ROWS, 2 * TQ), F32),
            pltpu.VMEM((2, TQ, 2 * H_B * TQ), BF16),
            pltpu.VMEM((2, 2 * H_B, TQ, TQ), F32),
            pltpu.VMEM((2, 1, 2 * H_B * TQ), F32),
        ],
        compiler_params=pltpu.CompilerParams(
            dimension_semantics=("arbitrary", "arbitrary"), vmem_limit_bytes=VMEM_LIMIT),
        name="diff",
    )(qb, kbh, vbt, mkb, mvb, lq1, lk1, lq2, lk2, sg)


def _sample_body(qa_ref, qi_ref, sm_ref, nka_ref, nva_ref, nki_ref, qb_ref, nkb_ref, nvb_ref,
                 cak_ref, cav_ref, cai_ref, cbk_ref, cbv_ref,
                 lq1_ref, lk1_ref, lq2_ref, lk2_ref, sg_ref,
                 oa_ref, ob_ref, pc_ref, pn_ref, *, k_top, past, ts):
    pad = LANES - ts
    row_c = lax.broadcasted_iota(I32, (ts, past), 0)
    col_c = lax.broadcasted_iota(I32, (ts, past), 1)
    row_n = lax.broadcasted_iota(I32, (ts, LANES), 0)
    col_n = lax.broadcasted_iota(I32, (ts, LANES), 1)
    new_ok = col_n < ts

    def pad_rows(x):
        return jnp.concatenate([x, jnp.zeros((pad,) + x.shape[1:], x.dtype)], axis=0)

    qi_all = qi_ref[...].reshape(H_I * ts, D_IDX)
    w = sm_ref[:, SM_WI:SM_WI + H_I]
    s4c = _dot(qi_all, cai_ref[...].astype(BF16))
    s4n = _dot_nt(qi_all, pad_rows(nki_ref[...]))
    sc_c = None
    sc_n = None
    for h in range(H_I):
        wh = w[:, h:h + 1]
        tc = jnp.maximum(s4c[h * ts:(h + 1) * ts], 0.0) * wh
        tn = jnp.maximum(s4n[h * ts:(h + 1) * ts], 0.0) * wh
        sc_c = tc if sc_c is None else sc_c + tc
        sc_n = tn if sc_n is None else sc_n + tn
    sc_n = jnp.where(new_ok, sc_n, -jnp.inf)

    def count(pred_c, pred_n):
        c = jnp.sum(jnp.where(pred_c, 1.0, 0.0), axis=1, keepdims=True)
        return c + jnp.sum(jnp.where(pred_n, 1.0, 0.0), axis=1, keepdims=True)

    def threshold(key):
        return _f32_from_key(jnp.clip(key, KEY32_LOWEST, KEY32_HIGHEST))

    def value_step(s, key):
        cand_key = key ^ jnp.left_shift(jnp.int32(1), 31 - s)
        cand = threshold(cand_key)
        cnt = count(sc_c >= cand, sc_n >= cand)
        return jnp.where(cnt >= float(k_top), cand_key, key)

    t = threshold(lax.fori_loop(0, 32, value_step, jnp.full((ts, 1), INT_MIN, I32)))
    need = float(k_top) - count(sc_c > t, sc_n > t)
    idx_n = col_n + past

    def tie_step(s, x):
        cand = x | jnp.left_shift(jnp.int32(1), 11 - s)
        cnt = count((sc_c == t) & (col_c < cand), (sc_n == t) & (idx_n < cand))
        return jnp.where(cnt < need, cand, x)

    x = lax.fori_loop(0, 12, tie_step, jnp.zeros((ts, 1), I32))
    negm_c = jnp.where((sc_c > t) | ((sc_c == t) & (col_c <= x)), 0.0, -jnp.inf)
    negm_n = jnp.where((sc_n > t) | ((sc_n == t) & (idx_n <= x)), 0.0, -jnp.inf)

    dist_c = (row_c - col_c + past).astype(F32)
    dist_n = jnp.abs(row_n - col_n).astype(F32)
    q_all = qa_ref[...][:, :, 0:DH_A].reshape(H_A * ts, DH_A)
    s_c = _dot(q_all, cak_ref[...].astype(BF16))
    s_n = _dot_nt(q_all, pad_rows(nka_ref[...][:, 0:DH_A]))
    for h in range(H_A):
        rows = slice(h * ts, (h + 1) * ts)
        lc = s_c[rows] + (negm_c - (LOG2E * SLOPES_A[h]) * dist_c)
        ln = s_n[rows] + (negm_n - (LOG2E * SLOPES_A[h]) * dist_n)
        m = jnp.maximum(jnp.max(lc, axis=1, keepdims=True), jnp.max(ln, axis=1, keepdims=True))
        m = jnp.maximum(m, NEG_INIT)
        pc_ref[rows, :] = jnp.exp2(lc - m).astype(BF16)
        pn_ref[rows, :] = jnp.exp2(ln - m).astype(BF16)
    vc_aug_t = jnp.concatenate([cav_ref[...].astype(BF16), jnp.ones((64, past), BF16)], axis=0)
    pv = _dot_nt(pc_ref[...], vc_aug_t) + _dot(pn_ref[...], pad_rows(nva_ref[...]))
    pv = pv / pltpu.roll(pv, DH_A, axis=1)
    oa_ref[...] = jnp.concatenate(
        [pv[h * ts:(h + 1) * ts, :DH_A] for h in range(H_A)], axis=1).astype(BF16)

    lam = _lambda(lq1_ref, lk1_ref, lq2_ref, lk2_ref)
    neg_new = jnp.where(new_ok, 0.0, -jnp.inf)
    for h in range(H_B):
        a_c = None
        a_n = None
        for c in range(2):
            hc = 2 * h + c
            q = qb_ref[hc][:, 0:DH_B]
            kc_t = cbk_ref[hc * DH_B:(hc + 1) * DH_B, :].astype(BF16)
            lc = _dot(q, kc_t) - (LOG2E * SLOPES_B[h]) * dist_c
            ln = _dot_nt(q, pad_rows(nkb_ref[hc][:, 0:DH_B])) + (neg_new - (LOG2E * SLOPES_B[h]) * dist_n)
            m = jnp.maximum(jnp.max(lc, axis=1, keepdims=True), jnp.max(ln, axis=1, keepdims=True))
            pc = jnp.exp2(lc - m)
            pn = jnp.exp2(ln - m)
            l = jnp.sum(pc, axis=1, keepdims=True) + jnp.sum(pn, axis=1, keepdims=True)
            pc = pc / l
            pn = pn / l
            if c == 0:
                a_c, a_n = pc, pn
            else:
                a_c, a_n = a_c - lam * pc, a_n - lam * pn
        vc = cbv_ref[pl.ds(h, past, stride=H_B), :].astype(BF16)
        vn = pad_rows(nvb_ref[:, h * DV_B:(h + 1) * DV_B])
        o = _dot(a_c.astype(BF16), vc) + _dot(a_n.astype(BF16), vn)
        o = _rms(o, sg_ref[...]) * (1.0 - LAM_INIT)
        ob_ref[:, h * DV_B:(h + 1) * DV_B] = o.astype(BF16)


def _sample(qa, qi, sm, nka, nva, nki, qb, nkb, nvb, cak, cav, cai, cbk, cbv,
            lq1, lk1, lq2, lk2, sg, batch, ts, past, k_top):
    row = lambda b: (b, 0)
    head = lambda b: (0, b, 0)
    cache = lambda b: (b, 0, 0)
    const = lambda b: (0, 0)
    return pl.pallas_call(
        functools.partial(_sample_body, k_top=k_top, past=past, ts=ts),
        out_shape=(jax.ShapeDtypeStruct((batch * ts, H_A * DH_A), BF16),
                   jax.ShapeDtypeStruct((batch * ts, H_B * DV_B), BF16)),
        grid=(batch,),
        in_specs=[
            pl.BlockSpec((H_A, ts, 128), head),
            pl.BlockSpec((H_I, ts, D_IDX), head),
            pl.BlockSpec((ts, 256), row),
            pl.BlockSpec((ts, 128), row),
            pl.BlockSpec((ts, 128), row),
            pl.BlockSpec((ts, D_IDX), row),
            pl.BlockSpec((2 * H_B, ts, 128), head),
            pl.BlockSpec((2 * H_B, ts, 128), head),
            pl.BlockSpec((ts, H_B * DV_B), row),
            pl.BlockSpec((None, DH_A, past), cache),
            pl.BlockSpec((None, DH_A, past), cache),
            pl.BlockSpec((None, D_IDX, past), cache),
            pl.BlockSpec((None, 2 * H_B * DH_B, past), cache),
            pl.BlockSpec((None, past * H_B, DV_B), cache),
            pl.BlockSpec((1, DH_B), const),
            pl.BlockSpec((1, DH_B), const),
            pl.BlockSpec((1, DH_B), const),
            pl.BlockSpec((1, DH_B), const),
            pl.BlockSpec((1, DV_B), const),
        ],
        out_specs=(pl.BlockSpec((ts, H_A * DH_A), row), pl.BlockSpec((ts, H_B * DV_B), row)),
        scratch_shapes=[
            pltpu.VMEM((H_A * ts, past), BF16),
            pltpu.VMEM((H_A * ts, LANES), BF16),
        ],
        compiler_params=pltpu.CompilerParams(
            dimension_semantics=("arbitrary",), vmem_limit_bytes=VMEM_LIMIT),
        name="sample",
    )(qa, qi, sm, nka, nva, nki, qb, nkb, nvb, cak, cav, cai, cbk, cbv, lq1, lk1, lq2, lk2, sg)


def _finish_body(x_ref, oa_ref, ob_ref, wo_ref, gm_ref, wu_ref, wd_ref, gf_ref, y_ref):
    o = jnp.concatenate([oa_ref[...], ob_ref[...]], axis=1)
    h1 = x_ref[...] + _dot(o, wo_ref[...])
    hn = _rms(h1, gm_ref[...]).astype(BF16)
    acc = h1
    for c in range(D_FF // D_MODEL):
        u = jnp.maximum(_dot(hn, wu_ref[:, c * D_MODEL:(c + 1) * D_MODEL]), 0.0)
        acc = acc + _dot((u * u).astype(BF16), wd_ref[c * D_MODEL:(c + 1) * D_MODEL, :])
    y_ref[...] = _rms(acc, gf_ref[...])


def _finish(x, oa, ob, wo, gm, wu, wd, gf, tm):
    m = x.shape[0]
    assert m % tm == 0
    row = lambda i: (i, 0)
    const = lambda i: (0, 0)
    resident = dict(pipeline_mode=pl.Buffered(1))
    return pl.pallas_call(
        _finish_body,
        out_shape=jax.ShapeDtypeStruct((m, D_MODEL), F32),
        grid=(m // tm,),
        in_specs=[
            pl.BlockSpec((tm, D_MODEL), row),
            pl.BlockSpec((tm, 512), row),
            pl.BlockSpec((tm, 512), row),
            pl.BlockSpec((D_MODEL, D_MODEL), const, **resident),
            pl.BlockSpec((1, D_MODEL), const),
            pl.BlockSpec((D_MODEL, D_FF), const, **resident),
            pl.BlockSpec((D_FF, D_MODEL), const, **resident),
            pl.BlockSpec((1, D_MODEL), const),
        ],
        out_specs=pl.BlockSpec((tm, D_MODEL), row),
        compiler_params=pltpu.CompilerParams(
            dimension_semantics=("arbitrary",), vmem_limit_bytes=VMEM_LIMIT),
        name="finish",
    )(x, oa, ob, wo, gm, wu, wd, gf)


def kernel(x_prompt, x_sample, cache_a_k, cache_a_v, cache_a_idx_k, cache_b_k, cache_b_v,
           meta_tokens, attn_norm_g, w_in, idx_k_norm_g, idx_k_norm_b,
           lambda_q1, lambda_k1, lambda_q2, lambda_k2, subln_g, w_o,
           mlp_norm_g, w_up, w_down, final_norm_g):
    batch, seq, _ = x_prompt.shape
    dec_batch, ts, _ = x_sample.shape
    past = cache_a_k.shape[2]
    assert attn_norm_g.shape[0] == 1, "single-layer step"
    assert seq % TQ == 0 and ts == 16 and past % LANES == 0 and past // CHUNK == (past + ts - 1) // CHUNK
    n = N_META + seq
    k_top_p = min(TOPK_MAX, seq // 4)
    k_top_s = min(TOPK_MAX, (past + ts) // 4)

    w_t = w_in[0].T.astype(BF16)
    w = jnp.concatenate(
        [w_t[0:512],
         w_t[964:1476],
         w_t[1476:1988],
         w_t[1988:2500],
         w_t[640:896],
         w_t[512:640],
         w_t[896:964]],
        axis=0)
    w = jnp.pad(w, ((0, W_COLS - 2500), (0, 0))).T
    wvt = w_t[1476:2500]
    wo = w_o[0].astype(BF16)
    wu = w_up[0].astype(BF16)
    wd = w_down[0].astype(BF16)
    g_attn = attn_norm_g[0][None]
    g_mlp = mlp_norm_g[0][None]
    g_fin = final_norm_g[None]
    kng = idx_k_norm_g[0][None]
    knb = idx_k_norm_b[0][None]
    lq1, lk1, lq2, lk2 = lambda_q1[0][None], lambda_k1[0][None], lambda_q2[0][None], lambda_k2[0][None]
    sg = subln_g[0][None]

    xp = x_prompt.reshape(batch * seq, D_MODEL)
    xs = x_sample.reshape(dec_batch * ts, D_MODEL)

    (qa_p, qi_p, qb_p, kbh_p, vb_p, kab_p, kib_p, vt_p, wt_p, vbt_p, smt_p, kbt_p) = _proj(
        xp, g_attn, w, wvt, kng, knb, 512, N_META, seq, True)
    (_, _, _, kbh_m, vb_m, kab_m, kib_m, kb_m, vbb_m, sm_m, vaa_m) = _proj(
        meta_tokens, g_attn, w, wvt, kng, knb, N_META, 0, N_META, False)
    (qa_s, qi_s, qb_s, kbh_s, vb_s, kab_s, kib_s, kb_s, vbb_s, sm_s, vaa_s) = _proj(
        xs, g_attn, w, wvt, kng, knb, dec_batch * ts, past, dec_batch * ts, False)

    oa_p = _dsa(qa_p, qi_p, wt_p, kab_p, kib_p, vt_p, kab_m, kib_m, vaa_m, batch, seq, k_top_p)
    ob_p = _diff(qb_p, kbh_p, vbt_p, kbh_m, vbb_m, lq1, lk1, lq2, lk2, sg, batch, seq)
    y_prompt = _finish(xp, oa_p, ob_p, wo, g_mlp, wu, wd, g_fin, 512).reshape(batch, seq, D_MODEL)

    oa_s, ob_s = _sample(
        qa_s, qi_s, sm_s, kab_s, vaa_s, kib_s, qb_s, kbh_s, vbb_s,
        jnp.swapaxes(cache_a_k[0], 1, 2), jnp.swapaxes(cache_a_v[0], 1, 2),
        jnp.swapaxes(cache_a_idx_k[0], 1, 2),
        jnp.transpose(cache_b_k[0], (0, 2, 3, 4, 1)).reshape(dec_batch, 2 * H_B * DH_B, past),
        cache_b_v[0].reshape(dec_batch, past * H_B, DV_B),
        lq1, lk1, lq2, lk2, sg, dec_batch, ts, past, k_top_s)
    y_sample = _finish(xs, oa_s, ob_s, wo, g_mlp, wu, wd, g_fin, dec_batch * ts).reshape(
        dec_batch, ts, D_MODEL)

    def with_meta_t(meta_rows, main_t):
        meta_b = jnp.broadcast_to(meta_rows.T[None], (batch, 64, N_META))
        return jnp.swapaxes(jnp.concatenate([meta_b, main_t], axis=2), 1, 2)[None]

    new_a_k_p = with_meta_t(sm_m[:, SM_KA:SM_KA + 64], smt_p[:, SM_KA:SM_KA + 64])
    new_a_v_p = with_meta_t(sm_m[:, SM_VA:SM_VA + 64], smt_p[:, SM_VA:SM_VA + 64])
    new_a_i_p = with_meta_t(sm_m[:, SM_KI:SM_KI + 64], smt_p[:, SM_KI:SM_KI + 64])
    kb_meta_t = jnp.broadcast_to(kb_m.T.reshape(1, H_B, 2, DH_B, N_META), (batch, H_B, 2, DH_B, N_META))
    kb_main_t = kbt_p.reshape(batch, H_B, 2, DH_B, seq)
    new_b_k_p = jnp.transpose(
        jnp.concatenate([kb_meta_t, kb_main_t], axis=4), (0, 4, 1, 2, 3))[None]
    new_b_v_p = jnp.concatenate(
        [jnp.broadcast_to(vb_m.reshape(1, N_META, H_B, DV_B), (batch, N_META, H_B, DV_B)),
         vb_p.reshape(batch, seq, H_B, DV_B)], axis=1)[None]
    new_a_k_s = sm_s[:, SM_KA:SM_KA + 64].reshape(1, dec_batch, ts, 64)
    new_a_v_s = sm_s[:, SM_VA:SM_VA + 64].reshape(1, dec_batch, ts, 64)
    new_a_i_s = sm_s[:, SM_KI:SM_KI + 64].reshape(1, dec_batch, ts, 64)
    new_b_k_s = kb_s.reshape(1, dec_batch, ts, H_B, 2, DH_B)
    new_b_v_s = vb_s.reshape(1, dec_batch, ts, H_B, DV_B)
    return (y_prompt, y_sample, new_a_k_p, new_a_v_p, new_a_i_p, new_b_k_p, new_b_v_p,
            new_a_k_s, new_a_v_s, new_a_i_s, new_b_k_s, new_b_v_s)
```

```python
import functools

import jax
import jax.numpy as jnp
import numpy as np
from jax import lax
from jax.experimental import pallas as pl
from jax.experimental.pallas import tpu as pltpu

F32 = jnp.float32
BF16 = jnp.bfloat16
I32 = jnp.int32
I16 = jnp.int16
HALF16 = 32768

D_MODEL = 1024
CHUNK = 64
N_META = 16
H_A = 8
DH_A = 64
H_I = 4
D_IDX = 64
TOPK_MAX = 256
H_B = 4
DH_B = 64
DV_B = 2 * DH_B
D_FF = 4 * D_MODEL
EPS = 1e-6
LAM_INIT = 0.2

LANES = 128
C_QA, C_QB, C_KB, C_VB, C_QI, C_SM = 0, 512, 1024, 1536, 2048, 2304
W_COLS = 2560
SM_KA, SM_VA, SM_KI, SM_WI = 0, 64, 128, 192

LOG2E = float(np.float32(1.4426950408889634))
Q_SCALE = (DH_A ** -0.5) * LOG2E
N_SLOPE_PIECES = 3


def _bf16_pieces(value):
    rest = np.float32(value)
    pieces = []
    for _ in range(N_SLOPE_PIECES):
        piece = np.float32(rest.astype(jnp.bfloat16))
        pieces.append(float(piece))
        rest = np.float32(rest - piece)
    assert rest == 0.0
    return pieces
W_SCALE = (H_I ** -0.5) * (D_IDX ** -0.5)
SLOPES_A = tuple(2.0 ** (-8.0 * (i + 1) / H_A) for i in range(H_A))
SLOPES_B = tuple(2.0 ** (-8.0 * (i + 1) / H_B) for i in range(H_B))

INT_MIN = -(2 ** 31)
NEG_INIT = -1e30
TQ = 256
ONES_ROWS = 16
VT_ROWS = DV_B + ONES_ROWS
VTA_ROWS = DH_A + ONES_ROWS
CNT_ROWS = 32
VMEM_LIMIT = 56 * 1024 * 1024

_NT = (((1,), (1,)), ((), ()))


def _dot(a, b):
    return jnp.dot(a, b, preferred_element_type=F32)


def _dot_nt(a, b):
    return lax.dot_general(a, b, _NT, preferred_element_type=F32)


def _rms(x, g):
    return (x * lax.rsqrt(jnp.mean(x * x, axis=-1, keepdims=True) + EPS)) * g


def _to_key(score):
    b = lax.bitcast_convert_type(score, I32)
    return b ^ ((b >> 31) & jnp.int32(0x7FFFFFFF))


def _f32_from_key(key):
    return lax.bitcast_convert_type(key ^ ((key >> 31) & jnp.int32(0x7FFFFFFF)), F32)


def _bf16_from_key(key):
    bits = key ^ ((key >> 15) & jnp.int32(0x7FFF))
    return lax.bitcast_convert_type(jnp.left_shift(bits, 16), F32).astype(BF16)


KEY16_LOWEST = -32640
KEY32_LOWEST = -2139095040
KEY32_HIGHEST = 2139095039


def _wide(x, width):
    reps = width // LANES
    return x if reps == 1 else jnp.concatenate([x] * reps, axis=1)


def _proj_body(x_ref, g_ref, w_ref, wvt_ref, kng_ref, knb_ref,
               qa_ref, qi_ref, qb_ref, kbh_ref, vb_ref, kab_ref, kib_ref, *mode_refs,
               pos0, period, key_major):
    x = x_ref[...]
    hn = _rms(x, g_ref[...]).astype(BF16)
    rows = x.shape[0]

    def mm(lo, width):
        return _dot(hn, w_ref[:, lo:lo + width])

    lane = lax.broadcasted_iota(I32, (rows, 64), 1)
    r = pl.program_id(0) * rows + lax.broadcasted_iota(I32, (rows, 64), 0)
    pos = pos0 + r % period
    pos_cols = jnp.where(lane >= 2 * N_SLOPE_PIECES, 0,
                         jnp.where(lane % 2 == 0, (pos // 256) * 256, pos % 256))
    pos_cols = pos_cols.astype(F32).astype(BF16)

    def slope_cols(slope):
        cols = jnp.zeros((rows, 64), F32)
        for n, piece in enumerate(_bf16_pieces(slope * LOG2E)):
            cols = jnp.where(lane // 2 == n, piece, cols)
        return cols.astype(BF16)

    z = mm(C_QA, 512) * Q_SCALE
    for h in range(H_A):
        qa_ref[h] = jnp.concatenate(
            [z[:, h * DH_A:(h + 1) * DH_A].astype(BF16), slope_cols(SLOPES_A[h])], axis=1)
    z = mm(C_QB, 512) * Q_SCALE
    for hc in range(2 * H_B):
        qb_ref[hc] = jnp.concatenate(
            [z[:, hc * DH_B:(hc + 1) * DH_B].astype(BF16), slope_cols(SLOPES_B[hc // 2])], axis=1)
    if key_major:
        vt_ref, wt_ref, vbt_ref, smt_ref, kbt_ref = mode_refs
    else:
        kb_ref, vbb_ref, sm_ref, vaa_ref = mode_refs
    z = mm(C_KB, 512)
    if not key_major:
        kb_ref[...] = z
    for hc in range(2 * H_B):
        kbh_ref[hc] = jnp.concatenate([z[:, hc * DH_B:(hc + 1) * DH_B].astype(BF16), pos_cols], axis=1)
    z = mm(C_VB, 512)
    for h in range(H_B):
        vb_ref[pl.ds(h, rows, stride=H_B), :] = z[:, h * DV_B:(h + 1) * DV_B]
    if not key_major:
        vbb_ref[...] = z.astype(BF16)
    z = mm(C_QI, 256)
    for h in range(H_I):
        qi_ref[h] = z[:, h * D_IDX:(h + 1) * D_IDX].astype(BF16)
    z = mm(C_SM, 256)
    ka = z[:, SM_KA:SM_KA + 64]
    va = z[:, SM_VA:SM_VA + 64]
    ki = z[:, SM_KI:SM_KI + 64]
    xc = ki - jnp.mean(ki, axis=-1, keepdims=True)
    ki = xc * lax.rsqrt(jnp.mean(xc * xc, axis=-1, keepdims=True) + EPS) * kng_ref[...] + knb_ref[...]
    kab_ref[...] = jnp.concatenate([ka.astype(BF16), pos_cols], axis=1)
    kib_ref[...] = ki.astype(BF16)
    if not key_major:
        sm_ref[:, 0:128] = z[:, 0:128]
        sm_ref[:, SM_KI:SM_KI + 64] = ki
        sm_ref[:, SM_WI:SM_WI + 64] = z[:, SM_WI:SM_WI + 64] * W_SCALE
        vaa_ref[...] = jnp.concatenate([va.astype(BF16), jnp.ones((rows, 64), BF16)], axis=1)
    else:
        kv_t = z[:, 0:128].T
        for c in range(rows // TQ):
            vt_ref[c] = jnp.concatenate(
                [kv_t[64:128, c * TQ:(c + 1) * TQ].astype(BF16), jnp.ones((ONES_ROWS, TQ), BF16)], axis=0)
        kw_t = jnp.concatenate([ki, z[:, SM_WI:SM_WI + 64] * W_SCALE], axis=1).T
        wt_ref[...] = kw_t[64:72, :]
        smt_ref[0:128, :] = kv_t
        smt_ref[128:192, :] = kw_t[0:64, :]
        kbt_ref[...] = _dot_nt(wvt_ref[0:512, :], hn)
        vb_t = _dot_nt(wvt_ref[512:1024, :], hn)
        for c in range(rows // TQ):
            for h in range(H_B):
                vbt_ref[c, h] = jnp.concatenate(
                    [vb_t[h * DV_B:(h + 1) * DV_B, c * TQ:(c + 1) * TQ].astype(BF16),
                     jnp.ones((VT_ROWS - DV_B, TQ), BF16)], axis=0)


def _proj(x, g, w, wvt, kng, knb, tm, pos0, period, key_major):
    m = x.shape[0]
    assert m % tm == 0
    row = lambda i: (i, 0)
    head = lambda i: (0, i, 0)
    const = lambda i: (0, 0)
    out_shape = (
        jax.ShapeDtypeStruct((H_A, m, 128), BF16),
        jax.ShapeDtypeStruct((H_I, m, D_IDX), BF16),
        jax.ShapeDtypeStruct((2 * H_B, m, 128), BF16),
        jax.ShapeDtypeStruct((2 * H_B, m, 128), BF16),
        jax.ShapeDtypeStruct((m * H_B, DV_B), F32),
        jax.ShapeDtypeStruct((m, 128), BF16),
        jax.ShapeDtypeStruct((m, 64), BF16),
    )
    out_specs = (
        pl.BlockSpec((H_A, tm, 128), head),
        pl.BlockSpec((H_I, tm, D_IDX), head),
        pl.BlockSpec((2 * H_B, tm, 128), head),
        pl.BlockSpec((2 * H_B, tm, 128), head),
        pl.BlockSpec((tm * H_B, DV_B), row),
        pl.BlockSpec((tm, 128), row),
        pl.BlockSpec((tm, 64), row),
    )
    if not key_major:
        out_shape += (
            jax.ShapeDtypeStruct((m, 512), F32),
            jax.ShapeDtypeStruct((m, 512), BF16),
            jax.ShapeDtypeStruct((m, 256), F32),
            jax.ShapeDtypeStruct((m, 128), BF16),
        )
        out_specs += (
            pl.BlockSpec((tm, 512), row),
            pl.BlockSpec((tm, 512), row),
            pl.BlockSpec((tm, 256), row),
            pl.BlockSpec((tm, 128), row),
        )
    else:
        assert tm % TQ == 0 and period % tm == 0 and m % period == 0
        per_stream = period // tm
        stream_cols = lambda i: (i // per_stream, 0, i % per_stream)
        out_shape += (
            jax.ShapeDtypeStruct((m // TQ, VTA_ROWS, TQ), BF16),
            jax.ShapeDtypeStruct((8, m), F32),
            jax.ShapeDtypeStruct((m // TQ, H_B, VT_ROWS, TQ), BF16),
            jax.ShapeDtypeStruct((m // period, 192, period), F32),
            jax.ShapeDtypeStruct((m // period, 512, period), F32),
        )
        out_specs += (
            pl.BlockSpec((tm // TQ, VTA_ROWS, TQ), lambda i: (i, 0, 0)),
            pl.BlockSpec((8, tm), lambda i: (0, i)),
            pl.BlockSpec((tm // TQ, H_B, VT_ROWS, TQ), lambda i: (i, 0, 0, 0)),
            pl.BlockSpec((None, 192, tm), stream_cols),
            pl.BlockSpec((None, 512, tm), stream_cols),
        )
    return pl.pallas_call(
        functools.partial(_proj_body, pos0=pos0, period=period, key_major=key_major),
        out_shape=out_shape,
        grid=(m // tm,),
        in_specs=[
            pl.BlockSpec((tm, D_MODEL), row),
            pl.BlockSpec((1, D_MODEL), const),
            pl.BlockSpec((D_MODEL, W_COLS), const),
            pl.BlockSpec((2 * H_B * DV_B, D_MODEL), const),
            pl.BlockSpec((1, D_IDX), const),
            pl.BlockSpec((1, D_IDX), const),
        ],
        out_specs=out_specs,
        compiler_params=pltpu.CompilerParams(
            dimension_semantics=("arbitrary",), vmem_limit_bytes=VMEM_LIMIT),
        name="proj",
    )(x, g, w, wvt, kng, knb)


def _dsa_body(qa_ref, qi_ref, wt_ref, ka_ref, ki_ref, vt_ref, mka_ref, mki_ref, mva_ref,
              o_ref, smeta_ref, smain_ref, bmeta_ref, bmain_ref,
              tsel_ref, m_ref, al_ref, acc_ref, p_ref, s_ref, mb_ref, seen_ref, *, k_top):
    i = pl.program_id(1)
    krow = lax.broadcasted_iota(I32, (TQ, TQ), 0)
    qcol = lax.broadcasted_iota(I32, (TQ, TQ), 1)

    def main_rows(j):
        return pl.ds(pl.multiple_of(j * TQ, TQ), TQ)

    qi_all = qi_ref[...].reshape(H_I * TQ, D_IDX)

    def scores(ki_blk):
        s4 = _dot_nt(ki_blk, qi_all)
        sc = None
        for h in range(H_I):
            t = jnp.maximum(s4[:, h * TQ:(h + 1) * TQ], 0.0) * wt_ref[h:h + 1, :]
            sc = t if sc is None else sc + t
        return sc

    sc = scores(mki_ref[...])
    smeta_ref[...] = sc
    bmeta_ref[...] = sc.astype(BF16)

    def score_block(j, sc):
        smain_ref[j] = sc
        bmain_ref[j] = sc.astype(BF16)

    def full_scores(j):
        return scores(ki_ref[main_rows(j), :])

    def last_scores():
        return jnp.where((krow // CHUNK) <= (qcol // CHUNK), full_scores(i), -jnp.inf)

    def score_two(u, carry):
        sc_a, sc_b = full_scores(2 * u), full_scores(2 * u + 1)
        score_block(2 * u, sc_a)
        score_block(2 * u + 1, sc_b)
        return carry

    lax.fori_loop(0, i // 2, score_two, 0)

    @pl.when(i % 2 == 1)
    def _odd_scores():
        sc_a, sc_b = full_scores(i - 1), last_scores()
        score_block(i - 1, sc_a)
        score_block(i, sc_b)

    @pl.when(i % 2 == 0)
    def _even_scores():
        score_block(i, last_scores())

    def count(meta_ref, main_ref, part, pred):
        def one(j):
            return part(pred(main_ref[j]))

        def two(u, c):
            return c + (one(2 * u) + one(2 * u + 1))

        c = lax.fori_loop(0, (i + 1) // 2, two, jnp.zeros((CNT_ROWS, TQ), F32))
        c = lax.cond(i % 2 == 0, lambda c: c + one(i), lambda c: c, c)
        cm = part(pred(meta_ref[...]), rows=N_META)
        return jnp.sum(c, axis=0, keepdims=True) + jnp.sum(cm, axis=0, keepdims=True)

    def part32(hit, rows=CNT_ROWS):
        ones = jnp.where(hit, 1.0, 0.0)
        return jnp.sum(ones.reshape(ones.shape[0] // rows, rows, TQ), axis=0)

    def part16(hit, rows=CNT_ROWS):
        ones = jnp.where(hit, jnp.bfloat16(1), jnp.bfloat16(0))
        acc = ones[0:rows]
        for t in range(1, ones.shape[0] // rows):
            acc = acc + ones[t * rows:(t + 1) * rows]
        return acc.astype(F32)

    kf = float(k_top)

    def step16(s, u):
        cand_u = u | jnp.left_shift(jnp.int32(1), 15 - s)
        cand = _bf16_from_key(jnp.maximum(cand_u, KEY16_LOWEST + HALF16) - HALF16)
        cnt = count(bmeta_ref, bmain_ref, part16, lambda x: x >= cand)
        return jnp.where(cnt >= kf, cand_u, u)

    u_hi = lax.fori_loop(0, 16, step16, jnp.zeros((1, TQ), I32))
    t_hi = _bf16_from_key(jnp.maximum(u_hi, KEY16_LOWEST + HALF16) - HALF16).astype(F32)
    base = jnp.maximum(_to_key(t_hi) - 65536, KEY32_LOWEST)

    def step32(s, d):
        cand_d = d | jnp.left_shift(jnp.int32(1), 16 - s)
        cand = _f32_from_key(jnp.minimum(base + cand_d, KEY32_HIGHEST))
        cnt = count(smeta_ref, smain_ref, part32, lambda x: x >= cand)
        return jnp.where(cnt >= kf, cand_d, d)

    d_lo = lax.fori_loop(0, 17, step32, jnp.zeros((1, TQ), I32))
    t_sel = _f32_from_key(jnp.minimum(base + d_lo, KEY32_HIGHEST))
    need = kf - count(smeta_ref, smain_ref, part32, lambda x: x > t_sel)
    tsel_ref[...] = jnp.broadcast_to(t_sel, tsel_ref.shape)

    m_ref[...] = jnp.full(m_ref.shape, NEG_INIT, F32)
    acc_ref[...] = jnp.zeros(acc_ref.shape, F32)
    thr = tsel_ref[0:1, :]

    def masked_logits(k_blk, sc, slot):
        nk = sc.shape[0]
        tie = sc == thr
        tri = (lax.broadcasted_iota(I32, (nk, nk), 1)
               <= lax.broadcasted_iota(I32, (nk, nk), 0)).astype(BF16)
        rank = _dot(tri, jnp.where(tie, 1.0, 0.0).astype(BF16)) + seen_ref[...]
        seen_ref[...] = rank[nk - 1:nk, :]
        negm = jnp.where(sc > thr, 0.0, jnp.where(tie, jnp.where(rank <= need, 0.0, -jnp.inf), -jnp.inf))
        for h in range(H_A):
            cols = slice(h * TQ, (h + 1) * TQ)
            s = _dot_nt(k_blk, qa_ref[h]) + negm
            s_ref[slot, 0:nk, cols] = s
            mb_ref[slot, :, cols] = jnp.max(s, axis=0, keepdims=True)

    def softmax(slot, nk, future):
        for h in range(H_A):
            cols = slice(h * TQ, (h + 1) * TQ)
            s = s_ref[slot, 0:nk, cols]
            if future is not None:
                s = s - (2.0 * LOG2E * SLOPES_A[h]) * future
            m_old = m_ref[:, cols]
            m_new = jnp.maximum(m_old, mb_ref[slot, :, cols])
            p_ref[slot, 0:nk, cols] = jnp.exp2(s - m_new).astype(BF16)
            al_ref[slot, :, cols] = jnp.exp2(m_old - m_new)
            m_ref[:, cols] = m_new

    def accumulate(vt_blk, slot, nk):
        acc_ref[...] = acc_ref[...] * al_ref[slot] + _dot(vt_blk, p_ref[slot, 0:nk, :])

    eye = (lax.broadcasted_iota(I32, (128, 128), 0)
           == lax.broadcasted_iota(I32, (128, 128), 1)).astype(BF16)
    seen_ref[...] = jnp.zeros(seen_ref.shape, F32)
    masked_logits(mka_ref[...], smeta_ref[...], 1)
    masked_logits(ka_ref[main_rows(0), :], smain_ref[0], 0)
    softmax(1, N_META, None)
    accumulate(_dot_nt(eye, mva_ref[...])[0:VTA_ROWS].astype(BF16), 1, N_META)

    p_ref[1] = jnp.zeros(p_ref.shape[1:], BF16)
    al_ref[1] = jnp.ones(al_ref.shape[1:], F32)

    def full_step(t, slot):
        masked_logits(ka_ref[main_rows(t + 1), :], smain_ref[t + 1], 1 - slot)
        accumulate(vt_ref[jnp.maximum(t - 1, 0)], 1 - slot, TQ)
        softmax(slot, TQ, None)

    def two_steps(u, carry):
        full_step(2 * u, 0)
        full_step(2 * u + 1, 1)
        return carry

    lax.fori_loop(0, i // 2, two_steps, 0)

    def last_step(slot):
        accumulate(vt_ref[jnp.maximum(i - 1, 0)], 1 - slot, TQ)
        softmax(slot, TQ, jnp.maximum(krow - qcol, 0).astype(F32))
        accumulate(vt_ref[i], slot, TQ)

    @pl.when(i % 2 == 1)
    def _odd():
        full_step(i - 1, 0)
        last_step(1)

    @pl.when(i % 2 == 0)
    def _even():
        last_step(0)

    def head_out_t(h):
        a = acc_ref[:, h * TQ:(h + 1) * TQ]
        return a[0:DH_A] / a[DH_A:DH_A + 1]

    for h in range(0, H_A, 2):
        pair_t = jnp.concatenate([head_out_t(h), head_out_t(h + 1)], axis=0)
        o_ref[:, h * DH_A:(h + 2) * DH_A] = pair_t.T.astype(BF16)


def _dsa(qa, qi, wt, kab, kib, vt, mka, mki, mva, batch, seq, k_top):
    nq = seq // TQ
    qrow = lambda b, i: (b * nq + i, 0)
    qhead = lambda b, i: (0, b * nq + i, 0)
    kv = lambda b, i: (b, 0)
    const = lambda b, i: (0, 0)
    return pl.pallas_call(
        functools.partial(_dsa_body, k_top=k_top),
        out_shape=jax.ShapeDtypeStruct((batch * seq, H_A * DH_A), BF16),
        grid=(batch, nq),
        in_specs=[
            pl.BlockSpec((H_A, TQ, 128), qhead),
            pl.BlockSpec((H_I, TQ, D_IDX), qhead),
            pl.BlockSpec((8, TQ), lambda b, i: (0, b * nq + i)),
            pl.BlockSpec((seq, 128), kv),
            pl.BlockSpec((seq, D_IDX), kv),
            pl.BlockSpec((nq, VTA_ROWS, TQ), lambda b, i: (b, 0, 0)),
            pl.BlockSpec((N_META, 128), const),
            pl.BlockSpec((N_META, D_IDX), const),
            pl.BlockSpec((N_META, 128), const),
        ],
        out_specs=pl.BlockSpec((TQ, H_A * DH_A), qrow),
        scratch_shapes=[
            pltpu.VMEM((N_META, TQ), F32),
            pltpu.VMEM((nq, TQ, TQ), F32),
            pltpu.VMEM((N_META, TQ), BF16),
            pltpu.VMEM((nq, TQ, TQ), BF16),
            pltpu.VMEM((8, TQ), F32),
            pltpu.VMEM((1, H_A * TQ), F32),
            pltpu.VMEM((2, 1, H_A * TQ), F32),
            pltpu.VMEM((VTA_ROWS, H_A * TQ), F32),
            pltpu.VMEM((2, TQ, H_A * TQ), BF16),
            pltpu.VMEM((2, TQ, H_A * TQ), F32),
            pltpu.VMEM((2, 1, H_A * TQ), F32),
            pltpu.VMEM((1, TQ), F32),
        ],
        compiler_params=pltpu.CompilerParams(
            dimension_semantics=("arbitrary", "arbitrary"), vmem_limit_bytes=VMEM_LIMIT),
        name="dsa",
    )(qa, qi, wt, kab, kib, vt, mka, mki, mva)


def _lambda(lq1_ref, lk1_ref, lq2_ref, lk2_ref):
    s1 = jnp.sum(lq1_ref[...] * lk1_ref[...], axis=-1, keepdims=True)
    s2 = jnp.sum(lq2_ref[...] * lk2_ref[...], axis=-1, keepdims=True)
    return jnp.exp(s1) - jnp.exp(s2) + LAM_INIT


def _diff_body(qb_ref, kb_ref, vt_ref, mkb_ref, mvb_ref, lq1_ref, lk1_ref, lq2_ref, lk2_ref,
               sg_ref, o_ref, m_ref, al_ref, acc_ref, p_ref, s_ref, mb_ref):
    i = pl.program_id(1)
    n_hc = 2 * H_B
    krow = lax.broadcasted_iota(I32, (TQ, TQ), 0)
    qcol = lax.broadcasted_iota(I32, (TQ, TQ), 1)
    m_ref[...] = jnp.full(m_ref.shape, NEG_INIT, F32)
    acc_ref[...] = jnp.zeros(acc_ref.shape, F32)

    def main_rows(j):
        return pl.ds(pl.multiple_of(j * TQ, TQ), TQ)

    def logits(k_of, slot, nk, last=False):
        if last:
            negm = jnp.where((krow // CHUNK) <= (qcol // CHUNK), 0.0, -jnp.inf)
            future = jnp.maximum(krow - qcol, 0).astype(F32)
        for hc in range(n_hc):
            s = _dot_nt(k_of(hc), qb_ref[hc])
            if last:
                s = (s + negm) - (2.0 * LOG2E * SLOPES_B[hc // 2]) * future
            s_ref[slot, hc, 0:nk, :] = s
            mb_ref[slot, :, hc * TQ:(hc + 1) * TQ] = jnp.max(s, axis=0, keepdims=True)

    def softmax(slot, nk):
        for hc in range(n_hc):
            cols = slice(hc * TQ, (hc + 1) * TQ)
            m_old = m_ref[:, cols]
            m_new = jnp.maximum(m_old, mb_ref[slot, :, cols])
            p_ref[slot, 0:nk, cols] = jnp.exp2(s_ref[slot, hc, 0:nk, :] - m_new).astype(BF16)
            al_ref[slot, :, cols] = jnp.exp2(m_old - m_new)
            m_ref[:, cols] = m_new

    def accumulate(vt_of, slot, nk):
        for h in range(H_B):
            cols = slice(2 * h * TQ, (2 * h + 2) * TQ)
            acc_ref[h] = acc_ref[h] * al_ref[slot, :, cols] + _dot(vt_of(h), p_ref[slot, 0:nk, cols])

    eye = (lax.broadcasted_iota(I32, (DV_B, DV_B), 0)
           == lax.broadcasted_iota(I32, (DV_B, DV_B), 1)).astype(BF16)

    def meta_vt(h):
        v_t = _dot_nt(eye, mvb_ref[:, h * DV_B:(h + 1) * DV_B]).astype(BF16)
        return jnp.concatenate([v_t, jnp.ones((VT_ROWS - DV_B, N_META), BF16)], axis=0)

    def main_vt(t):
        return lambda h: vt_ref[t, h]

    def start(first_is_last):
        logits(lambda hc: mkb_ref[hc], 1, N_META)
        logits(lambda hc: kb_ref[hc, main_rows(0), :], 0, TQ, last=first_is_last)
        softmax(1, N_META)
        accumulate(meta_vt, 1, N_META)
        p_ref[1] = jnp.zeros(p_ref.shape[1:], BF16)
        al_ref[1] = jnp.ones(al_ref.shape[1:], F32)

    def full_step(t, slot, next_is_last=False):
        logits(lambda hc: kb_ref[hc, main_rows(t + 1), :], 1 - slot, TQ, last=next_is_last)
        accumulate(main_vt(jnp.maximum(t - 1, 0)), 1 - slot, TQ)
        softmax(slot, TQ)

    def two_steps(u, carry):
        full_step(2 * u, 0)
        full_step(2 * u + 1, 1)
        return carry

    def last_step(slot):
        accumulate(main_vt(jnp.maximum(i - 1, 0)), 1 - slot, TQ)
        softmax(slot, TQ)
        accumulate(main_vt(i), slot, TQ)

    @pl.when(i == 0)
    def _first_tile():
        start(True)
        last_step(0)

    @pl.when(i % 2 == 1)
    def _odd():
        start(False)
        lax.fori_loop(0, i // 2, two_steps, 0)
        full_step(i - 1, 0, next_is_last=True)
        last_step(1)

    @pl.when((i > 0) & (i % 2 == 0))
    def _even():
        start(False)
        lax.fori_loop(0, i // 2 - 1, two_steps, 0)
        full_step(i - 2, 0)
        full_step(i - 1, 1, next_is_last=True)
        last_step(0)

    lam = _lambda(lq1_ref, lk1_ref, lq2_ref, lk2_ref)
    for h in range(H_B):
        a = acc_ref[h]
        o0 = a[0:DV_B, 0:TQ] / a[DV_B:DV_B + 1, 0:TQ]
        o1 = a[0:DV_B, TQ:2 * TQ] / a[DV_B:DV_B + 1, TQ:2 * TQ]
        o = _rms((o0 - lam * o1).T, sg_ref[...]) * (1.0 - LAM_INIT)
        o_ref[:, h * DV_B:(h + 1) * DV_B] = o.astype(BF16)


def _diff(qb, kbh, vbt, mkb, mvb, lq1, lk1, lq2, lk2, sg, batch, seq):
    nq = seq // TQ
    qrow = lambda b, i: (b * nq + i, 0)
    qhead = lambda b, i: (0, b * nq + i, 0)
    const2 = lambda b, i: (0, 0)
    const3 = lambda b, i: (0, 0, 0)
    return pl.pallas_call(
        _diff_body,
        out_shape=jax.ShapeDtypeStruct((batch * seq, H_B * DV_B), BF16),
        grid=(batch, nq),
        in_specs=[
            pl.BlockSpec((2 * H_B, TQ, 128), qhead),
            pl.BlockSpec((2 * H_B, seq, 128), lambda b, i: (0, b, 0)),
            pl.BlockSpec((nq, H_B, VT_ROWS, TQ), lambda b, i: (b, 0, 0, 0)),
            pl.BlockSpec((2 * H_B, N_META, 128), const3),
            pl.BlockSpec((N_META, H_B * DV_B), const2),
            pl.BlockSpec((1, DH_B), const2),
            pl.BlockSpec((1, DH_B), const2),
            pl.BlockSpec((1, DH_B), const2),
            pl.BlockSpec((1, DH_B), const2),
            pl.BlockSpec((1, DV_B), const2),
        ],
        out_specs=pl.BlockSpec((TQ, H_B * DV_B), qrow),
        scratch_shapes=[
            pltpu.VMEM((1, 2 * H_B * TQ), F32),
            pltpu.VMEM((2, 1, 2 * H_B * TQ), F32),
            pltpu.VMEM((H_B, VT_ROWS, 2 * TQ), F32),
            pltpu.VMEM((2, TQ, 2 * H_B * TQ), BF16),
            pltpu.VMEM((2, 2 * H_B, TQ, TQ), F32),
            pltpu.VMEM((2, 1, 2 * H_B * TQ), F32),
        ],
        compiler_params=pltpu.CompilerParams(
            dimension_semantics=("arbitrary", "arbitrary"), vmem_limit_bytes=VMEM_LIMIT),
        name="diff",
    )(qb, kbh, vbt, mkb, mvb, lq1, lk1, lq2, lk2, sg)


def _sample_body(qa_ref, qi_ref, sm_ref, nka_ref, nva_ref, nki_ref, qb_ref, nkb_ref, nvb_ref,
                 cak_ref, cav_ref, cai_ref, cbk_ref, cbv_ref,
                 lq1_ref, lk1_ref, lq2_ref, lk2_ref, sg_ref,
                 oa_ref, ob_ref, pc_ref, pn_ref, *, k_top, past, ts):
    pad = LANES - ts
    row_c = lax.broadcasted_iota(I32, (ts, past), 0)
    col_c = lax.broadcasted_iota(I32, (ts, past), 1)
    row_n = lax.broadcasted_iota(I32, (ts, LANES), 0)
    col_n = lax.broadcasted_iota(I32, (ts, LANES), 1)
    new_ok = col_n < ts

    def pad_rows(x):
        return jnp.concatenate([x, jnp.zeros((pad,) + x.shape[1:], x.dtype)], axis=0)

    qi_all = qi_ref[...].reshape(H_I * ts, D_IDX)
    w = sm_ref[:, SM_WI:SM_WI + H_I]
    s4c = _dot(qi_all, cai_ref[...].astype(BF16))
    s4n = _dot_nt(qi_all, pad_rows(nki_ref[...]))
    sc_c = None
    sc_n = None
    for h in range(H_I):
        wh = w[:, h:h + 1]
        tc = jnp.maximum(s4c[h * ts:(h + 1) * ts], 0.0) * wh
        tn = jnp.maximum(s4n[h * ts:(h + 1) * ts], 0.0) * wh
        sc_c = tc if sc_c is None else sc_c + tc
        sc_n = tn if sc_n is None else sc_n + tn
    sc_n = jnp.where(new_ok, sc_n, -jnp.inf)

    def count(pred_c, pred_n):
        c = jnp.sum(jnp.where(pred_c, 1.0, 0.0), axis=1, keepdims=True)
        return c + jnp.sum(jnp.where(pred_n, 1.0, 0.0), axis=1, keepdims=True)

    def threshold(key):
        return _f32_from_key(jnp.clip(key, KEY32_LOWEST, KEY32_HIGHEST))

    def value_step(s, key):
        cand_key = key ^ jnp.left_shift(jnp.int32(1), 31 - s)
        cand = threshold(cand_key)
        cnt = count(sc_c >= cand, sc_n >= cand)
        return jnp.where(cnt >= float(k_top), cand_key, key)

    t = threshold(lax.fori_loop(0, 32, value_step, jnp.full((ts, 1), INT_MIN, I32)))
    need = float(k_top) - count(sc_c > t, sc_n > t)
    idx_n = col_n + past

    def tie_step(s, x):
        cand = x | jnp.left_shift(jnp.int32(1), 11 - s)
        cnt = count((sc_c == t) & (col_c < cand), (sc_n == t) & (idx_n < cand))
        return jnp.where(cnt < need, cand, x)

    x = lax.fori_loop(0, 12, tie_step, jnp.zeros((ts, 1), I32))
    negm_c = jnp.where((sc_c > t) | ((sc_c == t) & (col_c <= x)), 0.0, -jnp.inf)
    negm_n = jnp.where((sc_n > t) | ((sc_n == t) & (idx_n <= x)), 0.0, -jnp.inf)

    dist_c = (row_c - col_c + past).astype(F32)
    dist_n = jnp.abs(row_n - col_n).astype(F32)
    q_all = qa_ref[...][:, :, 0:DH_A].reshape(H_A * ts, DH_A)
    s_c = _dot(q_all, cak_ref[...].astype(BF16))
    s_n = _dot_nt(q_all, pad_rows(nka_ref[...][:, 0:DH_A]))
    for h in range(H_A):
        rows = slice(h * ts, (h + 1) * ts)
        lc = s_c[rows] + (negm_c - (LOG2E * SLOPES_A[h]) * dist_c)
        ln = s_n[rows] + (negm_n - (LOG2E * SLOPES_A[h]) * dist_n)
        m = jnp.maximum(jnp.max(lc, axis=1, keepdims=True), jnp.max(ln, axis=1, keepdims=True))
        m = jnp.maximum(m, NEG_INIT)
        pc_ref[rows, :] = jnp.exp2(lc - m).astype(BF16)
        pn_ref[rows, :] = jnp.exp2(ln - m).astype(BF16)
    vc_aug_t = jnp.concatenate([cav_ref[...].astype(BF16), jnp.ones((64, past), BF16)], axis=0)
    pv = _dot_nt(pc_ref[...], vc_aug_t) + _dot(pn_ref[...], pad_rows(nva_ref[...]))
    pv = pv / pltpu.roll(pv, DH_A, axis=1)
    oa_ref[...] = jnp.concatenate(
        [pv[h * ts:(h + 1) * ts, :DH_A] for h in range(H_A)], axis=1).astype(BF16)

    lam = _lambda(lq1_ref, lk1_ref, lq2_ref, lk2_ref)
    neg_new = jnp.where(new_ok, 0.0, -jnp.inf)
    for h in range(H_B):
        a_c = None
        a_n = None
        for c in range(2):
            hc = 2 * h + c
            q = qb_ref[hc][:, 0:DH_B]
            kc_t = cbk_ref[hc * DH_B:(hc + 1) * DH_B, :].astype(BF16)
            lc = _dot(q, kc_t) - (LOG2E * SLOPES_B[h]) * dist_c
            ln = _dot_nt(q, pad_rows(nkb_ref[hc][:, 0:DH_B])) + (neg_new - (LOG2E * SLOPES_B[h]) * dist_n)
            m = jnp.maximum(jnp.max(lc, axis=1, keepdims=True), jnp.max(ln, axis=1, keepdims=True))
            pc = jnp.exp2(lc - m)
            pn = jnp.exp2(ln - m)
            l = jnp.sum(pc, axis=1, keepdims=True) + jnp.sum(pn, axis=1, keepdims=True)
            pc = pc / l
            pn = pn / l
            if c == 0:
                a_c, a_n = pc, pn
            else:
                a_c, a_n = a_c - lam * pc, a_n - lam * pn
        vc = cbv_ref[pl.ds(h, past, stride=H_B), :].astype(BF16)
        vn = pad_rows(nvb_ref[:, h * DV_B:(h + 1) * DV_B])
        o = _dot(a_c.astype(BF16), vc) + _dot(a_n.astype(BF16), vn)
        o = _rms(o, sg_ref[...]) * (1.0 - LAM_INIT)
        ob_ref[:, h * DV_B:(h + 1) * DV_B] = o.astype(BF16)


def _sample(qa, qi, sm, nka, nva, nki, qb, nkb, nvb, cak, cav, cai, cbk, cbv,
            lq1, lk1, lq2, lk2, sg, batch, ts, past, k_top):
    row = lambda b: (b, 0)
    head = lambda b: (0, b, 0)
    cache = lambda b: (b, 0, 0)
    const = lambda b: (0, 0)
    return pl.pallas_call(
        functools.partial(_sample_body, k_top=k_top, past=past, ts=ts),
        out_shape=(jax.ShapeDtypeStruct((batch * ts, H_A * DH_A), BF16),
                   jax.ShapeDtypeStruct((batch * ts, H_B * DV_B), BF16)),
        grid=(batch,),
        in_specs=[
            pl.BlockSpec((H_A, ts, 128), head),
            pl.BlockSpec((H_I, ts, D_IDX), head),
            pl.BlockSpec((ts, 256), row),
            pl.BlockSpec((ts, 128), row),
            pl.BlockSpec((ts, 128), row),
            pl.BlockSpec((ts, D_IDX), row),
            pl.BlockSpec((2 * H_B, ts, 128), head),
            pl.BlockSpec((2 * H_B, ts, 128), head),
            pl.BlockSpec((ts, H_B * DV_B), row),
            pl.BlockSpec((None, DH_A, past), cache),
            pl.BlockSpec((None, DH_A, past), cache),
            pl.BlockSpec((None, D_IDX, past), cache),
            pl.BlockSpec((None, 2 * H_B * DH_B, past), cache),
            pl.BlockSpec((None, past * H_B, DV_B), cache),
            pl.BlockSpec((1, DH_B), const),
            pl.BlockSpec((1, DH_B), const),
            pl.BlockSpec((1, DH_B), const),
            pl.BlockSpec((1, DH_B), const),
            pl.BlockSpec((1, DV_B), const),
        ],
        out_specs=(pl.BlockSpec((ts, H_A * DH_A), row), pl.BlockSpec((ts, H_B * DV_B), row)),
        scratch_shapes=[
            pltpu.VMEM((H_A * ts, past), BF16),
            pltpu.VMEM((H_A * ts, LANES), BF16),
        ],
        compiler_params=pltpu.CompilerParams(
            dimension_semantics=("arbitrary",), vmem_limit_bytes=VMEM_LIMIT),
        name="sample",
    )(qa, qi, sm, nka, nva, nki, qb, nkb, nvb, cak, cav, cai, cbk, cbv, lq1, lk1, lq2, lk2, sg)


def _finish_body(x_ref, oa_ref, ob_ref, wo_ref, gm_ref, wu_ref, wd_ref, gf_ref, y_ref):
    o = jnp.concatenate([oa_ref[...], ob_ref[...]], axis=1)
    h1 = x_ref[...] + _dot(o, wo_ref[...])
    hn = _rms(h1, gm_ref[...]).astype(BF16)
    acc = h1
    for c in range(D_FF // D_MODEL):
        u = jnp.maximum(_dot(hn, wu_ref[:, c * D_MODEL:(c + 1) * D_MODEL]), 0.0)
        acc = acc + _dot((u * u).astype(BF16), wd_ref[c * D_MODEL:(c + 1) * D_MODEL, :])
    y_ref[...] = _rms(acc, gf_ref[...])


def _finish(x, oa, ob, wo, gm, wu, wd, gf, tm):
    m = x.shape[0]
    assert m % tm == 0
    row = lambda i: (i, 0)
    const = lambda i: (0, 0)
    resident = dict(pipeline_mode=pl.Buffered(1))
    return pl.pallas_call(
        _finish_body,
        out_shape=jax.ShapeDtypeStruct((m, D_MODEL), F32),
        grid=(m // tm,),
        in_specs=[
            pl.BlockSpec((tm, D_MODEL), row),
            pl.BlockSpec((tm, 512), row),
            pl.BlockSpec((tm, 512), row),
            pl.BlockSpec((D_MODEL, D_MODEL), const, **resident),
            pl.BlockSpec((1, D_MODEL), const),
            pl.BlockSpec((D_MODEL, D_FF), const, **resident),
            pl.BlockSpec((D_FF, D_MODEL), const, **resident),
            pl.BlockSpec((1, D_MODEL), const),
        ],
        out_specs=pl.BlockSpec((tm, D_MODEL), row),
        compiler_params=pltpu.CompilerParams(
            dimension_semantics=("arbitrary",), vmem_limit_bytes=VMEM_LIMIT),
        name="finish",
    )(x, oa, ob, wo, gm, wu, wd, gf)


def kernel(x_prompt, x_sample, cache_a_k, cache_a_v, cache_a_idx_k, cache_b_k, cache_b_v,
           meta_tokens, attn_norm_g, w_in, idx_k_norm_g, idx_k_norm_b,
           lambda_q1, lambda_k1, lambda_q2, lambda_k2, subln_g, w_o,
           mlp_norm_g, w_up, w_down, final_norm_g):
    batch, seq, _ = x_prompt.shape
    dec_batch, ts, _ = x_sample.shape
    past = cache_a_k.shape[2]
    assert attn_norm_g.shape[0] == 1, "single-layer step"
    assert seq % TQ == 0 and ts == 16 and past % LANES == 0 and past // CHUNK == (past + ts - 1) // CHUNK
    n = N_META + seq
    k_top_p = min(TOPK_MAX, seq // 4)
    k_top_s = min(TOPK_MAX, (past + ts) // 4)

    w_t = w_in[0].T.astype(BF16)
    w = jnp.concatenate(
        [w_t[0:512],
         w_t[964:1476],
         w_t[1476:1988],
         w_t[1988:2500],
         w_t[640:896],
         w_t[512:640],
         w_t[896:964]],
        axis=0)
    w = jnp.pad(w, ((0, W_COLS - 2500), (0, 0))).T
    wvt = w_t[1476:2500]
    wo = w_o[0].astype(BF16)
    wu = w_up[0].astype(BF16)
    wd = w_down[0].astype(BF16)
    g_attn = attn_norm_g[0][None]
    g_mlp = mlp_norm_g[0][None]
    g_fin = final_norm_g[None]
    kng = idx_k_norm_g[0][None]
    knb = idx_k_norm_b[0][None]
    lq1, lk1, lq2, lk2 = lambda_q1[0][None], lambda_k1[0][None], lambda_q2[0][None], lambda_k2[0][None]
    sg = subln_g[0][None]

    xp = x_prompt.reshape(batch * seq, D_MODEL)
    xs = x_sample.reshape(dec_batch * ts, D_MODEL)

    (qa_p, qi_p, qb_p, kbh_p, vb_p, kab_p, kib_p, vt_p, wt_p, vbt_p, smt_p, kbt_p) = _proj(
        xp, g_attn, w, wvt, kng, knb, 512, N_META, seq, True)
    (_, _, _, kbh_m, vb_m, kab_m, kib_m, kb_m, vbb_m, sm_m, vaa_m) = _proj(
        meta_tokens, g_attn, w, wvt, kng, knb, N_META, 0, N_META, False)
    (qa_s, qi_s, qb_s, kbh_s, vb_s, kab_s, kib_s, kb_s, vbb_s, sm_s, vaa_s) = _proj(
        xs, g_attn, w, wvt, kng, knb, dec_batch * ts, past, dec_batch * ts, False)

    oa_p = _dsa(qa_p, qi_p, wt_p, kab_p, kib_p, vt_p, kab_m, kib_m, vaa_m, batch, seq, k_top_p)
    ob_p = _diff(qb_p, kbh_p, vbt_p, kbh_m, vbb_m, lq1, lk1, lq2, lk2, sg, batch, seq)
    y_prompt = _finish(xp, oa_p, ob_p, wo, g_mlp, wu, wd, g_fin, 512).reshape(batch, seq, D_MODEL)

    oa_s, ob_s = _sample(
        qa_s, qi_s, sm_s, kab_s, vaa_s, kib_s, qb_s, kbh_s, vbb_s,
        jnp.swapaxes(cache_a_k[0], 1, 2), jnp.swapaxes(cache_a_v[0], 1, 2),
        jnp.swapaxes(cache_a_idx_k[0], 1, 2),
        jnp.transpose(cache_b_k[0], (0, 2, 3, 4, 1)).reshape(dec_batch, 2 * H_B * DH_B, past),
        cache_b_v[0].reshape(dec_batch, past * H_B, DV_B),
        lq1, lk1, lq2, lk2, sg, dec_batch, ts, past, k_top_s)
    y_sample = _finish(xs, oa_s, ob_s, wo, g_mlp, wu, wd, g_fin, dec_batch * ts).reshape(
        dec_batch, ts, D_MODEL)

    def with_meta_t(meta_rows, main_t):
        meta_b = jnp.broadcast_to(meta_rows.T[None], (batch, 64, N_META))
        return jnp.swapaxes(jnp.concatenate([meta_b, main_t], axis=2), 1, 2)[None]

    new_a_k_p = with_meta_t(sm_m[:, SM_KA:SM_KA + 64], smt_p[:, SM_KA:SM_KA + 64])
    new_a_v_p = with_meta_t(sm_m[:, SM_VA:SM_VA + 64], smt_p[:, SM_VA:SM_VA + 64])
    new_a_i_p = with_meta_t(sm_m[:, SM_KI:SM_KI + 64], smt_p[:, SM_KI:SM_KI + 64])
    kb_meta_t = jnp.broadcast_to(kb_m.T.reshape(1, H_B, 2, DH_B, N_META), (batch, H_B, 2, DH_B, N_META))
    kb_main_t = kbt_p.reshape(batch, H_B, 2, DH_B, seq)
    new_b_k_p = jnp.transpose(
        jnp.concatenate([kb_meta_t, kb_main_t], axis=4), (0, 4, 1, 2, 3))[None]
    new_b_v_p = jnp.concatenate(
        [jnp.broadcast_to(vb_m.reshape(1, N_META, H_B, DV_B), (batch, N_META, H_B, DV_B)),
         vb_p.reshape(batch, seq, H_B, DV_B)], axis=1)[None]
    new_a_k_s = sm_s[:, SM_KA:SM_KA + 64].reshape(1, dec_batch, ts, 64)
    new_a_v_s = sm_s[:, SM_VA:SM_VA + 64].reshape(1, dec_batch, ts, 64)
    new_a_i_s = sm_s[:, SM_KI:SM_KI + 64].reshape(1, dec_batch, ts, 64)
    new_b_k_s = kb_s.reshape(1, dec_batch, ts, H_B, 2, DH_B)
    new_b_v_s = vb_s.reshape(1, dec_batch, ts, H_B, DV_B)
    return (y_prompt, y_sample, new_a_k_p, new_a_v_p, new_a_i_p, new_b_k_p, new_b_v_p,
            new_a_k_s, new_a_v_s, new_a_i_s, new_b_k_s, new_b_v_s)
```

```python
import functools

import jax
import jax.numpy as jnp
import numpy as np
from jax import lax
from jax.experimental import pallas as pl
from jax.experimental.pallas import tpu as pltpu

F32 = jnp.float32
BF16 = jnp.bfloat16
I32 = jnp.int32
I16 = jnp.int16
HALF16 = 32768

D_MODEL = 1024
CHUNK = 64
N_META = 16
H_A = 8
DH_A = 64
H_I = 4
D_IDX = 64
TOPK_MAX = 256
H_B = 4
DH_B = 64
DV_B = 2 * DH_B
D_FF = 4 * D_MODEL
EPS = 1e-6
LAM_INIT = 0.2

LANES = 128
C_QA, C_QB, C_KB, C_VB, C_QI, C_SM = 0, 512, 1024, 1536, 2048, 2304
W_COLS = 2560
SM_KA, SM_VA, SM_KI, SM_WI = 0, 64, 128, 192

LOG2E = float(np.float32(1.4426950408889634))
Q_SCALE = (DH_A ** -0.5) * LOG2E
N_SLOPE_PIECES = 3


def _bf16_pieces(value):
    rest = np.float32(value)
    pieces = []
    for _ in range(N_SLOPE_PIECES):
        piece = np.float32(rest.astype(jnp.bfloat16))
        pieces.append(float(piece))
        rest = np.float32(rest - piece)
    assert rest == 0.0
    return pieces
W_SCALE = (H_I ** -0.5) * (D_IDX ** -0.5)
SLOPES_A = tuple(2.0 ** (-8.0 * (i + 1) / H_A) for i in range(H_A))
SLOPES_B = tuple(2.0 ** (-8.0 * (i + 1) / H_B) for i in range(H_B))

INT_MIN = -(2 ** 31)
NEG_INIT = -1e30
TQ = 256
ONES_ROWS = 16
VT_ROWS = DV_B + ONES_ROWS
VTA_ROWS = DH_A + ONES_ROWS
CNT_ROWS = 32
SAMPLE_GROUP = 2
VMEM_LIMIT = 56 * 1024 * 1024

_NT = (((1,), (1,)), ((), ()))


def _dot(a, b):
    return jnp.dot(a, b, preferred_element_type=F32)


def _dot_nt(a, b):
    return lax.dot_general(a, b, _NT, preferred_element_type=F32)


def _rms(x, g):
    return (x * lax.rsqrt(jnp.mean(x * x, axis=-1, keepdims=True) + EPS)) * g


def _to_key(score):
    b = lax.bitcast_convert_type(score, I32)
    return b ^ ((b >> 31) & jnp.int32(0x7FFFFFFF))


def _f32_from_key(key):
    return lax.bitcast_convert_type(key ^ ((key >> 31) & jnp.int32(0x7FFFFFFF)), F32)


def _bf16_from_key(key):
    bits = key ^ ((key >> 15) & jnp.int32(0x7FFF))
    return lax.bitcast_convert_type(jnp.left_shift(bits, 16), F32).astype(BF16)


KEY16_LOWEST = -32640
KEY32_LOWEST = -2139095040
KEY32_HIGHEST = 2139095039


def _wide(x, width):
    reps = width // LANES
    return x if reps == 1 else jnp.concatenate([x] * reps, axis=1)


def _proj_body(x_ref, g_ref, w_ref, wvt_ref, kng_ref, knb_ref,
               qa_ref, qi_ref, qb_ref, kbh_ref, vb_ref, kab_ref, kib_ref, *mode_refs,
               pos0, period, key_major):
    x = x_ref[...]
    hn = _rms(x, g_ref[...]).astype(BF16)
    rows = x.shape[0]

    def mm(lo, width):
        return _dot(hn, w_ref[:, lo:lo + width])

    lane = lax.broadcasted_iota(I32, (rows, 64), 1)
    r = pl.program_id(0) * rows + lax.broadcasted_iota(I32, (rows, 64), 0)
    pos = pos0 + r % period
    pos_cols = jnp.where(lane >= 2 * N_SLOPE_PIECES, 0,
                         jnp.where(lane % 2 == 0, (pos // 256) * 256, pos % 256))
    pos_cols = pos_cols.astype(F32).astype(BF16)

    def slope_cols(slope):
        cols = jnp.zeros((rows, 64), F32)
        for n, piece in enumerate(_bf16_pieces(slope * LOG2E)):
            cols = jnp.where(lane // 2 == n, piece, cols)
        return cols.astype(BF16)

    z = mm(C_QA, 512) * Q_SCALE
    for h in range(H_A):
        qa_ref[h] = jnp.concatenate(
            [z[:, h * DH_A:(h + 1) * DH_A].astype(BF16), slope_cols(SLOPES_A[h])], axis=1)
    z = mm(C_QB, 512) * Q_SCALE
    for hc in range(2 * H_B):
        qb_ref[hc] = jnp.concatenate(
            [z[:, hc * DH_B:(hc + 1) * DH_B].astype(BF16), slope_cols(SLOPES_B[hc // 2])], axis=1)
    if key_major:
        vt_ref, wt_ref, vbt_ref, smt_ref, kbt_ref = mode_refs
    else:
        kb_ref, vbb_ref, sm_ref, vaa_ref = mode_refs
    z = mm(C_KB, 512)
    if not key_major:
        kb_ref[...] = z
    for hc in range(2 * H_B):
        kbh_ref[hc] = jnp.concatenate([z[:, hc * DH_B:(hc + 1) * DH_B].astype(BF16), pos_cols], axis=1)
    z = mm(C_VB, 512)
    for h in range(H_B):
        vb_ref[pl.ds(h, rows, stride=H_B), :] = z[:, h * DV_B:(h + 1) * DV_B]
    if not key_major:
        vbb_ref[...] = z.astype(BF16)
    z = mm(C_QI, 256)
    for h in range(H_I):
        qi_ref[h] = z[:, h * D_IDX:(h + 1) * D_IDX].astype(BF16)
    z = mm(C_SM, 256)
    ka = z[:, SM_KA:SM_KA + 64]
    va = z[:, SM_VA:SM_VA + 64]
    ki = z[:, SM_KI:SM_KI + 64]
    xc = ki - jnp.mean(ki, axis=-1, keepdims=True)
    ki = xc * lax.rsqrt(jnp.mean(xc * xc, axis=-1, keepdims=True) + EPS) * kng_ref[...] + knb_ref[...]
    kab_ref[...] = jnp.concatenate([ka.astype(BF16), pos_cols], axis=1)
    kib_ref[...] = ki.astype(BF16)
    if not key_major:
        sm_ref[:, 0:128] = z[:, 0:128]
        sm_ref[:, SM_KI:SM_KI + 64] = ki
        sm_ref[:, SM_WI:SM_WI + 64] = z[:, SM_WI:SM_WI + 64] * W_SCALE
        vaa_ref[...] = jnp.concatenate([va.astype(BF16), jnp.ones((rows, 64), BF16)], axis=1)
    else:
        kv_t = z[:, 0:128].T
        for c in range(rows // TQ):
            vt_ref[c] = jnp.concatenate(
                [kv_t[64:128, c * TQ:(c + 1) * TQ].astype(BF16), jnp.ones((ONES_ROWS, TQ), BF16)], axis=0)
        kw_t = jnp.concatenate([ki, z[:, SM_WI:SM_WI + 64] * W_SCALE], axis=1).T
        wt_ref[...] = kw_t[64:72, :]
        smt_ref[0:128, :] = kv_t
        smt_ref[128:192, :] = kw_t[0:64, :]
        kbt_ref[...] = _dot_nt(wvt_ref[0:512, :], hn)
        vb_t = _dot_nt(wvt_ref[512:1024, :], hn)
        for c in range(rows // TQ):
            for h in range(H_B):
                vbt_ref[c, h] = jnp.concatenate(
                    [vb_t[h * DV_B:(h + 1) * DV_B, c * TQ:(c + 1) * TQ].astype(BF16),
                     jnp.ones((VT_ROWS - DV_B, TQ), BF16)], axis=0)


def _proj(x, g, w, wvt, kng, knb, tm, pos0, period, key_major):
    m = x.shape[0]
    assert m % tm == 0
    row = lambda i: (i, 0)
    head = lambda i: (0, i, 0)
    const = lambda i: (0, 0)
    out_shape = (
        jax.ShapeDtypeStruct((H_A, m, 128), BF16),
        jax.ShapeDtypeStruct((H_I, m, D_IDX), BF16),
        jax.ShapeDtypeStruct((2 * H_B, m, 128), BF16),
        jax.ShapeDtypeStruct((2 * H_B, m, 128), BF16),
        jax.ShapeDtypeStruct((m * H_B, DV_B), F32),
        jax.ShapeDtypeStruct((m, 128), BF16),
        jax.ShapeDtypeStruct((m, 64), BF16),
    )
    out_specs = (
        pl.BlockSpec((H_A, tm, 128), head),
        pl.BlockSpec((H_I, tm, D_IDX), head),
        pl.BlockSpec((2 * H_B, tm, 128), head),
        pl.BlockSpec((2 * H_B, tm, 128), head),
        pl.BlockSpec((tm * H_B, DV_B), row),
        pl.BlockSpec((tm, 128), row),
        pl.BlockSpec((tm, 64), row),
    )
    if not key_major:
        out_shape += (
            jax.ShapeDtypeStruct((m, 512), F32),
            jax.ShapeDtypeStruct((m, 512), BF16),
            jax.ShapeDtypeStruct((m, 256), F32),
            jax.ShapeDtypeStruct((m, 128), BF16),
        )
        out_specs += (
            pl.BlockSpec((tm, 512), row),
            pl.BlockSpec((tm, 512), row),
            pl.BlockSpec((tm, 256), row),
            pl.BlockSpec((tm, 128), row),
        )
    else:
        assert tm % TQ == 0 and period % tm == 0 and m % period == 0
        per_stream = period // tm
        stream_cols = lambda i: (i // per_stream, 0, i % per_stream)
        out_shape += (
            jax.ShapeDtypeStruct((m // TQ, VTA_ROWS, TQ), BF16),
            jax.ShapeDtypeStruct((8, m), F32),
            jax.ShapeDtypeStruct((m // TQ, H_B, VT_ROWS, TQ), BF16),
            jax.ShapeDtypeStruct((m // period, 192, period), F32),
            jax.ShapeDtypeStruct((m // period, 512, period), F32),
        )
        out_specs += (
            pl.BlockSpec((tm // TQ, VTA_ROWS, TQ), lambda i: (i, 0, 0)),
            pl.BlockSpec((8, tm), lambda i: (0, i)),
            pl.BlockSpec((tm // TQ, H_B, VT_ROWS, TQ), lambda i: (i, 0, 0, 0)),
            pl.BlockSpec((None, 192, tm), stream_cols),
            pl.BlockSpec((None, 512, tm), stream_cols),
        )
    return pl.pallas_call(
        functools.partial(_proj_body, pos0=pos0, period=period, key_major=key_major),
        out_shape=out_shape,
        grid=(m // tm,),
        in_specs=[
            pl.BlockSpec((tm, D_MODEL), row),
            pl.BlockSpec((1, D_MODEL), const),
            pl.BlockSpec((D_MODEL, W_COLS), const),
            pl.BlockSpec((2 * H_B * DV_B, D_MODEL), const),
            pl.BlockSpec((1, D_IDX), const),
            pl.BlockSpec((1, D_IDX), const),
        ],
        out_specs=out_specs,
        compiler_params=pltpu.CompilerParams(
            dimension_semantics=("arbitrary",), vmem_limit_bytes=VMEM_LIMIT),
        name="proj",
    )(x, g, w, wvt, kng, knb)


def _dsa_body(qa_ref, qi_ref, wt_ref, ka_ref, ki_ref, vt_ref, mka_ref, mki_ref, mva_ref,
              o_ref, smeta_ref, smain_ref, bmeta_ref, bmain_ref,
              tsel_ref, m_ref, al_ref, acc_ref, p_ref, s_ref, mb_ref, seen_ref, *, k_top):
    i = pl.program_id(1)
    krow = lax.broadcasted_iota(I32, (TQ, TQ), 0)
    qcol = lax.broadcasted_iota(I32, (TQ, TQ), 1)

    def main_rows(j):
        return pl.ds(pl.multiple_of(j * TQ, TQ), TQ)

    qi_all = qi_ref[...].reshape(H_I * TQ, D_IDX)

    def scores(ki_blk):
        s4 = _dot_nt(ki_blk, qi_all)
        sc = None
        for h in range(H_I):
            t = jnp.maximum(s4[:, h * TQ:(h + 1) * TQ], 0.0) * wt_ref[h:h + 1, :]
            sc = t if sc is None else sc + t
        return sc

    sc = scores(mki_ref[...])
    smeta_ref[...] = sc
    bmeta_ref[...] = sc.astype(BF16)

    def score_block(j, sc):
        smain_ref[j] = sc
        bmain_ref[j] = sc.astype(BF16)

    def full_scores(j):
        return scores(ki_ref[main_rows(j), :])

    def last_scores():
        return jnp.where((krow // CHUNK) <= (qcol // CHUNK), full_scores(i), -jnp.inf)

    def score_two(u, carry):
        sc_a, sc_b = full_scores(2 * u), full_scores(2 * u + 1)
        score_block(2 * u, sc_a)
        score_block(2 * u + 1, sc_b)
        return carry

    lax.fori_loop(0, i // 2, score_two, 0)

    @pl.when(i % 2 == 1)
    def _odd_scores():
        sc_a, sc_b = full_scores(i - 1), last_scores()
        score_block(i - 1, sc_a)
        score_block(i, sc_b)

    @pl.when(i % 2 == 0)
    def _even_scores():
        score_block(i, last_scores())

    def count(meta_ref, main_ref, part, pred):
        def one(j):
            return part(pred(main_ref[j]))

        def two(u, c):
            return c + (one(2 * u) + one(2 * u + 1))

        c = lax.fori_loop(0, (i + 1) // 2, two, jnp.zeros((CNT_ROWS, TQ), F32))
        c = lax.cond(i % 2 == 0, lambda c: c + one(i), lambda c: c, c)
        cm = part(pred(meta_ref[...]), rows=N_META)
        return jnp.sum(c, axis=0, keepdims=True) + jnp.sum(cm, axis=0, keepdims=True)

    def part32(hit, rows=CNT_ROWS):
        ones = jnp.where(hit, 1.0, 0.0)
        return jnp.sum(ones.reshape(ones.shape[0] // rows, rows, TQ), axis=0)

    def part16(hit, rows=CNT_ROWS):
        ones = jnp.where(hit, jnp.bfloat16(1), jnp.bfloat16(0))
        acc = ones[0:rows]
        for t in range(1, ones.shape[0] // rows):
            acc = acc + ones[t * rows:(t + 1) * rows]
        return acc.astype(F32)

    kf = float(k_top)

    def step16(s, u):
        cand_u = u | jnp.left_shift(jnp.int32(1), 15 - s)
        cand = _bf16_from_key(jnp.maximum(cand_u, KEY16_LOWEST + HALF16) - HALF16)
        cnt = count(bmeta_ref, bmain_ref, part16, lambda x: x >= cand)
        return jnp.where(cnt >= kf, cand_u, u)

    u_hi = lax.fori_loop(0, 16, step16, jnp.zeros((1, TQ), I32))
    t_hi = _bf16_from_key(jnp.maximum(u_hi, KEY16_LOWEST + HALF16) - HALF16).astype(F32)
    base = jnp.maximum(_to_key(t_hi) - 65536, KEY32_LOWEST)

    def step32(s, d):
        cand_d = d | jnp.left_shift(jnp.int32(1), 16 - s)
        cand = _f32_from_key(jnp.minimum(base + cand_d, KEY32_HIGHEST))
        cnt = count(smeta_ref, smain_ref, part32, lambda x: x >= cand)
        return jnp.where(cnt >= kf, cand_d, d)

    d_lo = lax.fori_loop(0, 17, step32, jnp.zeros((1, TQ), I32))
    t_sel = _f32_from_key(jnp.minimum(base + d_lo, KEY32_HIGHEST))
    need = kf - count(smeta_ref, smain_ref, part32, lambda x: x > t_sel)
    tsel_ref[...] = jnp.broadcast_to(t_sel, tsel_ref.shape)

    m_ref[...] = jnp.full(m_ref.shape, NEG_INIT, F32)
    acc_ref[...] = jnp.zeros(acc_ref.shape, F32)
    thr = tsel_ref[0:1, :]

    def masked_logits(k_blk, sc, slot):
        nk = sc.shape[0]
        tie = sc == thr
        tri = (lax.broadcasted_iota(I32, (nk, nk), 1)
               <= lax.broadcasted_iota(I32, (nk, nk), 0)).astype(BF16)
        rank = _dot(tri, jnp.where(tie, 1.0, 0.0).astype(BF16)) + seen_ref[...]
        seen_ref[...] = rank[nk - 1:nk, :]
        negm = jnp.where(sc > thr, 0.0, jnp.where(tie, jnp.where(rank <= need, 0.0, -jnp.inf), -jnp.inf))
        for h in range(H_A):
            cols = slice(h * TQ, (h + 1) * TQ)
            s = _dot_nt(k_blk, qa_ref[h]) + negm
            s_ref[slot, 0:nk, cols] = s
            mb_ref[slot, :, cols] = jnp.max(s, axis=0, keepdims=True)

    def softmax(slot, nk, future):
        for h in range(H_A):
            cols = slice(h * TQ, (h + 1) * TQ)
            s = s_ref[slot, 0:nk, cols]
            if future is not None:
                s = s - (2.0 * LOG2E * SLOPES_A[h]) * future
            m_old = m_ref[:, cols]
            m_new = jnp.maximum(m_old, mb_ref[slot, :, cols])
            p_ref[slot, 0:nk, cols] = jnp.exp2(s - m_new).astype(BF16)
            al_ref[slot, :, cols] = jnp.exp2(m_old - m_new)
            m_ref[:, cols] = m_new

    def accumulate(vt_blk, slot, nk):
        acc_ref[...] = acc_ref[...] * al_ref[slot] + _dot(vt_blk, p_ref[slot, 0:nk, :])

    eye = (lax.broadcasted_iota(I32, (128, 128), 0)
           == lax.broadcasted_iota(I32, (128, 128), 1)).astype(BF16)
    seen_ref[...] = jnp.zeros(seen_ref.shape, F32)
    masked_logits(mka_ref[...], smeta_ref[...], 1)
    masked_logits(ka_ref[main_rows(0), :], smain_ref[0], 0)
    softmax(1, N_META, None)
    accumulate(_dot_nt(eye, mva_ref[...])[0:VTA_ROWS].astype(BF16), 1, N_META)

    p_ref[1] = jnp.zeros(p_ref.shape[1:], BF16)
    al_ref[1] = jnp.ones(al_ref.shape[1:], F32)

    def full_step(t, slot):
        masked_logits(ka_ref[main_rows(t + 1), :], smain_ref[t + 1], 1 - slot)
        accumulate(vt_ref[jnp.maximum(t - 1, 0)], 1 - slot, TQ)
        softmax(slot, TQ, None)

    def two_steps(u, carry):
        full_step(2 * u, 0)
        full_step(2 * u + 1, 1)
        return carry

    lax.fori_loop(0, i // 2, two_steps, 0)

    def last_step(slot):
        accumulate(vt_ref[jnp.maximum(i - 1, 0)], 1 - slot, TQ)
        softmax(slot, TQ, jnp.maximum(krow - qcol, 0).astype(F32))
        accumulate(vt_ref[i], slot, TQ)

    @pl.when(i % 2 == 1)
    def _odd():
        full_step(i - 1, 0)
        last_step(1)

    @pl.when(i % 2 == 0)
    def _even():
        last_step(0)

    def head_out_t(h):
        a = acc_ref[:, h * TQ:(h + 1) * TQ]
        return a[0:DH_A] / a[DH_A:DH_A + 1]

    for h in range(0, H_A, 2):
        pair_t = jnp.concatenate([head_out_t(h), head_out_t(h + 1)], axis=0)
        o_ref[:, h * DH_A:(h + 2) * DH_A] = pair_t.T.astype(BF16)


def _dsa(qa, qi, wt, kab, kib, vt, mka, mki, mva, batch, seq, k_top):
    nq = seq // TQ
    qrow = lambda b, i: (b * nq + i, 0)
    qhead = lambda b, i: (0, b * nq + i, 0)
    kv = lambda b, i: (b, 0)
    const = lambda b, i: (0, 0)
    return pl.pallas_call(
        functools.partial(_dsa_body, k_top=k_top),
        out_shape=jax.ShapeDtypeStruct((batch * seq, H_A * DH_A), BF16),
        grid=(batch, nq),
        in_specs=[
            pl.BlockSpec((H_A, TQ, 128), qhead),
            pl.BlockSpec((H_I, TQ, D_IDX), qhead),
            pl.BlockSpec((8, TQ), lambda b, i: (0, b * nq + i)),
            pl.BlockSpec((seq, 128), kv),
            pl.BlockSpec((seq, D_IDX), kv),
            pl.BlockSpec((nq, VTA_ROWS, TQ), lambda b, i: (b, 0, 0)),
            pl.BlockSpec((N_META, 128), const),
            pl.BlockSpec((N_META, D_IDX), const),
            pl.BlockSpec((N_META, 128), const),
        ],
        out_specs=pl.BlockSpec((TQ, H_A * DH_A), qrow),
        scratch_shapes=[
            pltpu.VMEM((N_META, TQ), F32),
            pltpu.VMEM((nq, TQ, TQ), F32),
            pltpu.VMEM((N_META, TQ), BF16),
            pltpu.VMEM((nq, TQ, TQ), BF16),
            pltpu.VMEM((8, TQ), F32),
            pltpu.VMEM((1, H_A * TQ), F32),
            pltpu.VMEM((2, 1, H_A * TQ), F32),
            pltpu.VMEM((VTA_ROWS, H_A * TQ), F32),
            pltpu.VMEM((2, TQ, H_A * TQ), BF16),
            pltpu.VMEM((2, TQ, H_A * TQ), F32),
            pltpu.VMEM((2, 1, H_A * TQ), F32),
            pltpu.VMEM((1, TQ), F32),
        ],
        compiler_params=pltpu.CompilerParams(
            dimension_semantics=("arbitrary", "arbitrary"), vmem_limit_bytes=VMEM_LIMIT),
        name="dsa",
    )(qa, qi, wt, kab, kib, vt, mka, mki, mva)


def _lambda(lq1_ref, lk1_ref, lq2_ref, lk2_ref):
    s1 = jnp.sum(lq1_ref[...] * lk1_ref[...], axis=-1, keepdims=True)
    s2 = jnp.sum(lq2_ref[...] * lk2_ref[...], axis=-1, keepdims=True)
    return jnp.exp(s1) - jnp.exp(s2) + LAM_INIT


def _diff_body(qb_ref, kb_ref, vt_ref, mkb_ref, mvb_ref, lq1_ref, lk1_ref, lq2_ref, lk2_ref,
               sg_ref, o_ref, m_ref, al_ref, acc_ref, p_ref, s_ref, mb_ref):
    i = pl.program_id(1)
    n_hc = 2 * H_B
    krow = lax.broadcasted_iota(I32, (TQ, TQ), 0)
    qcol = lax.broadcasted_iota(I32, (TQ, TQ), 1)
    m_ref[...] = jnp.full(m_ref.shape, NEG_INIT, F32)
    acc_ref[...] = jnp.zeros(acc_ref.shape, F32)

    def main_rows(j):
        return pl.ds(pl.multiple_of(j * TQ, TQ), TQ)

    def logits(k_of, slot, nk, last=False):
        if last:
            negm = jnp.where((krow // CHUNK) <= (qcol // CHUNK), 0.0, -jnp.inf)
            future = jnp.maximum(krow - qcol, 0).astype(F32)
        for hc in range(n_hc):
            s = _dot_nt(k_of(hc), qb_ref[hc])
            if last:
                s = (s + negm) - (2.0 * LOG2E * SLOPES_B[hc // 2]) * future
            s_ref[slot, hc, 0:nk, :] = s
            mb_ref[slot, :, hc * TQ:(hc + 1) * TQ] = jnp.max(s, axis=0, keepdims=True)

    def softmax(slot, nk):
        for hc in range(n_hc):
            cols = slice(hc * TQ, (hc + 1) * TQ)
            m_old = m_ref[:, cols]
            m_new = jnp.maximum(m_old, mb_ref[slot, :, cols])
            p_ref[slot, 0:nk, cols] = jnp.exp2(s_ref[slot, hc, 0:nk, :] - m_new).astype(BF16)
            al_ref[slot, :, cols] = jnp.exp2(m_old - m_new)
            m_ref[:, cols] = m_new

    def accumulate(vt_of, slot, nk):
        for h in range(H_B):
            cols = slice(2 * h * TQ, (2 * h + 2) * TQ)
            acc_ref[h] = acc_ref[h] * al_ref[slot, :, cols] + _dot(vt_of(h), p_ref[slot, 0:nk, cols])

    eye = (lax.broadcasted_iota(I32, (DV_B, DV_B), 0)
           == lax.broadcasted_iota(I32, (DV_B, DV_B), 1)).astype(BF16)

    def meta_vt(h):
        v_t = _dot_nt(eye, mvb_ref[:, h * DV_B:(h + 1) * DV_B]).astype(BF16)
        return jnp.concatenate([v_t, jnp.ones((VT_ROWS - DV_B, N_META), BF16)], axis=0)

    def main_vt(t):
        return lambda h: vt_ref[t, h]

    def start(first_is_last):
        logits(lambda hc: mkb_ref[hc], 1, N_META)
        logits(lambda hc: kb_ref[hc, main_rows(0), :], 0, TQ, last=first_is_last)
        softmax(1, N_META)
        accumulate(meta_vt, 1, N_META)
        p_ref[1] = jnp.zeros(p_ref.shape[1:], BF16)
        al_ref[1] = jnp.ones(al_ref.shape[1:], F32)

    def full_step(t, slot, next_is_last=False):
        logits(lambda hc: kb_ref[hc, main_rows(t + 1), :], 1 - slot, TQ, last=next_is_last)
        accumulate(main_vt(jnp.maximum(t - 1, 0)), 1 - slot, TQ)
        softmax(slot, TQ)

    def two_steps(u, carry):
        full_step(2 * u, 0)
        full_step(2 * u + 1, 1)
        return carry

    def last_step(slot):
        accumulate(main_vt(jnp.maximum(i - 1, 0)), 1 - slot, TQ)
        softmax(slot, TQ)
        accumulate(main_vt(i), slot, TQ)

    @pl.when(i == 0)
    def _first_tile():
        start(True)
        last_step(0)

    @pl.when(i % 2 == 1)
    def _odd():
        start(False)
        lax.fori_loop(0, i // 2, two_steps, 0)
        full_step(i - 1, 0, next_is_last=True)
        last_step(1)

    @pl.when((i > 0) & (i % 2 == 0))
    def _even():
        start(False)
        lax.fori_loop(0, i // 2 - 1, two_steps, 0)
        full_step(i - 2, 0)
        full_step(i - 1, 1, next_is_last=True)
        last_step(0)

    lam = _lambda(lq1_ref, lk1_ref, lq2_ref, lk2_ref)
    for h in range(H_B):
        a = acc_ref[h]
        o0 = a[0:DV_B, 0:TQ] / a[DV_B:DV_B + 1, 0:TQ]
        o1 = a[0:DV_B, TQ:2 * TQ] / a[DV_B:DV_B + 1, TQ:2 * TQ]
        o = _rms((o0 - lam * o1).T, sg_ref[...]) * (1.0 - LAM_INIT)
        o_ref[:, h * DV_B:(h + 1) * DV_B] = o.astype(BF16)


def _diff(qb, kbh, vbt, mkb, mvb, lq1, lk1, lq2, lk2, sg, batch, seq):
    nq = seq // TQ
    qrow = lambda b, i: (b * nq + i, 0)
    qhead = lambda b, i: (0, b * nq + i, 0)
    const2 = lambda b, i: (0, 0)
    const3 = lambda b, i: (0, 0, 0)
    return pl.pallas_call(
        _diff_body,
        out_shape=jax.ShapeDtypeStruct((batch * seq, H_B * DV_B), BF16),
        grid=(batch, nq),
        in_specs=[
            pl.BlockSpec((2 * H_B, TQ, 128), qhead),
            pl.BlockSpec((2 * H_B, seq, 128), lambda b, i: (0, b, 0)),
            pl.BlockSpec((nq, H_B, VT_ROWS, TQ), lambda b, i: (b, 0, 0, 0)),
            pl.BlockSpec((2 * H_B, N_META, 128), const3),
            pl.BlockSpec((N_META, H_B * DV_B), const2),
            pl.BlockSpec((1, DH_B), const2),
            pl.BlockSpec((1, DH_B), const2),
            pl.BlockSpec((1, DH_B), const2),
            pl.BlockSpec((1, DH_B), const2),
            pl.BlockSpec((1, DV_B), const2),
        ],
        out_specs=pl.BlockSpec((TQ, H_B * DV_B), qrow),
        scratch_shapes=[
            pltpu.VMEM((1, 2 * H_B * TQ), F32),
            pltpu.VMEM((2, 1, 2 * H_B * TQ), F32),
            pltpu.VMEM((H_B, VT_ROWS, 2 * TQ), F32),
            pltpu.VMEM((2, TQ, 2 * H_B * TQ), BF16),
            pltpu.VMEM((2, 2 * H_B, TQ, TQ), F32),
            pltpu.VMEM((2, 1, 2 * H_B * TQ), F32),
        ],
        compiler_params=pltpu.CompilerParams(
            dimension_semantics=("arbitrary", "arbitrary"), vmem_limit_bytes=VMEM_LIMIT),
        name="diff",
    )(qb, kbh, vbt, mkb, mvb, lq1, lk1, lq2, lk2, sg)


def _sample_body(qa_ref, qi_ref, sm_ref, nka_ref, nva_ref, nki_ref, qb_ref, nkb_ref, nvb_ref,
                 cak_ref, cav_ref, cai_ref, cbk_ref, cbv_ref,
                 lq1_ref, lk1_ref, lq2_ref, lk2_ref, sg_ref,
                 oa_ref, ob_ref, pc_ref, pn_ref, *, k_top, past, ts, group):
    pad = LANES - ts
    row_c = lax.broadcasted_iota(I32, (ts, past), 0)
    col_c1 = lax.broadcasted_iota(I32, (ts, past), 1)
    row_n = lax.broadcasted_iota(I32, (ts, LANES), 0)
    col_n1 = lax.broadcasted_iota(I32, (ts, LANES), 1)
    new_ok = col_n1 < ts
    col_c = lax.broadcasted_iota(I32, (group * ts, past), 1)
    col_n = lax.broadcasted_iota(I32, (group * ts, LANES), 1)

    def pad_rows(x):
        return jnp.concatenate([x, jnp.zeros((pad,) + x.shape[1:], x.dtype)], axis=0)

    def stream_rows(g):
        return slice(g * ts, (g + 1) * ts)

    def indexer_scores(g):
        r = stream_rows(g)
        qi_all = qi_ref[:, r, :].reshape(H_I * ts, D_IDX)
        w = sm_ref[r, SM_WI:SM_WI + H_I]
        s4c = _dot(qi_all, cai_ref[g].astype(BF16))
        s4n = _dot_nt(qi_all, pad_rows(nki_ref[r, :]))
        sc_c = None
        sc_n = None
        for h in range(H_I):
            wh = w[:, h:h + 1]
            tc = jnp.maximum(s4c[h * ts:(h + 1) * ts], 0.0) * wh
            tn = jnp.maximum(s4n[h * ts:(h + 1) * ts], 0.0) * wh
            sc_c = tc if sc_c is None else sc_c + tc
            sc_n = tn if sc_n is None else sc_n + tn
        return sc_c, jnp.where(new_ok, sc_n, -jnp.inf)

    per_stream = [indexer_scores(g) for g in range(group)]
    sc_c = jnp.concatenate([p[0] for p in per_stream], axis=0)
    sc_n = jnp.concatenate([p[1] for p in per_stream], axis=0)

    def count(pred_c, pred_n):
        c = jnp.sum(jnp.where(pred_c, 1.0, 0.0), axis=1, keepdims=True)
        return c + jnp.sum(jnp.where(pred_n, 1.0, 0.0), axis=1, keepdims=True)

    def threshold(key):
        return _f32_from_key(jnp.clip(key, KEY32_LOWEST, KEY32_HIGHEST))

    def value_step(s, key):
        cand_key = key ^ jnp.left_shift(jnp.int32(1), 31 - s)
        cand = threshold(cand_key)
        cnt = count(sc_c >= cand, sc_n >= cand)
        return jnp.where(cnt >= float(k_top), cand_key, key)

    t = threshold(lax.fori_loop(0, 32, value_step, jnp.full((group * ts, 1), INT_MIN, I32)))
    need = float(k_top) - count(sc_c > t, sc_n > t)
    idx_n = col_n + past

    def tie_step(s, x):
        cand = x | jnp.left_shift(jnp.int32(1), 11 - s)
        cnt = count((sc_c == t) & (col_c < cand), (sc_n == t) & (idx_n < cand))
        return jnp.where(cnt < need, cand, x)

    x = lax.fori_loop(0, 12, tie_step, jnp.zeros((group * ts, 1), I32))
    negm_c_all = jnp.where((sc_c > t) | ((sc_c == t) & (col_c <= x)), 0.0, -jnp.inf)
    negm_n_all = jnp.where((sc_n > t) | ((sc_n == t) & (idx_n <= x)), 0.0, -jnp.inf)

    dist_c = (row_c - col_c1 + past).astype(F32)
    dist_n = jnp.abs(row_n - col_n1).astype(F32)
    lam = _lambda(lq1_ref, lk1_ref, lq2_ref, lk2_ref)
    neg_new = jnp.where(new_ok, 0.0, -jnp.inf)
    for g in range(group):
        _sample_attend(g, stream_rows(g), negm_c_all, negm_n_all, dist_c, dist_n, lam, neg_new, pad_rows,
                       qa_ref, nka_ref, nva_ref, qb_ref, nkb_ref, nvb_ref,
                       cak_ref, cav_ref, cbk_ref, cbv_ref, sg_ref, oa_ref, ob_ref, pc_ref, pn_ref,
                       past=past, ts=ts)


def _sample_attend(g, r, negm_c_all, negm_n_all, dist_c, dist_n, lam, neg_new, pad_rows,
                   qa_ref, nka_ref, nva_ref, qb_ref, nkb_ref, nvb_ref,
                   cak_ref, cav_ref, cbk_ref, cbv_ref, sg_ref, oa_ref, ob_ref, pc_ref, pn_ref,
                   *, past, ts):
    negm_c = negm_c_all[r]
    negm_n = negm_n_all[r]
    q_all = qa_ref[:, r, :][:, :, 0:DH_A].reshape(H_A * ts, DH_A)
    s_c = _dot(q_all, cak_ref[g].astype(BF16))
    s_n = _dot_nt(q_all, pad_rows(nka_ref[r, :][:, 0:DH_A]))
    for h in range(H_A):
        rows = slice(h * ts, (h + 1) * ts)
        lc = s_c[rows] + (negm_c - (LOG2E * SLOPES_A[h]) * dist_c)
        ln = s_n[rows] + (negm_n - (LOG2E * SLOPES_A[h]) * dist_n)
        m = jnp.maximum(jnp.max(lc, axis=1, keepdims=True), jnp.max(ln, axis=1, keepdims=True))
        m = jnp.maximum(m, NEG_INIT)
        pc_ref[g, rows, :] = jnp.exp2(lc - m).astype(BF16)
        pn_ref[g, rows, :] = jnp.exp2(ln - m).astype(BF16)
    vc_aug_t = jnp.concatenate([cav_ref[g].astype(BF16), jnp.ones((64, past), BF16)], axis=0)
    pv = _dot_nt(pc_ref[g], vc_aug_t) + _dot(pn_ref[g], pad_rows(nva_ref[r, :]))
    pv = pv / pltpu.roll(pv, DH_A, axis=1)
    oa_ref[r, :] = jnp.concatenate(
        [pv[h * ts:(h + 1) * ts, :DH_A] for h in range(H_A)], axis=1).astype(BF16)

    for h in range(H_B):
        a_c = None
        a_n = None
        for c in range(2):
            hc = 2 * h + c
            q = qb_ref[hc, r, :][:, 0:DH_B]
            kc_t = cbk_ref[g, hc * DH_B:(hc + 1) * DH_B, :].astype(BF16)
            lc = _dot(q, kc_t) - (LOG2E * SLOPES_B[h]) * dist_c
            ln = _dot_nt(q, pad_rows(nkb_ref[hc, r, :][:, 0:DH_B])) + (neg_new - (LOG2E * SLOPES_B[h]) * dist_n)
            m = jnp.maximum(jnp.max(lc, axis=1, keepdims=True), jnp.max(ln, axis=1, keepdims=True))
            pc = jnp.exp2(lc - m)
            pn = jnp.exp2(ln - m)
            l = jnp.sum(pc, axis=1, keepdims=True) + jnp.sum(pn, axis=1, keepdims=True)
            pc = pc / l
            pn = pn / l
            if c == 0:
                a_c, a_n = pc, pn
            else:
                a_c, a_n = a_c - lam * pc, a_n - lam * pn
        vc = cbv_ref[g, pl.ds(h, past, stride=H_B), :].astype(BF16)
        vn = pad_rows(nvb_ref[r, h * DV_B:(h + 1) * DV_B])
        o = _dot(a_c.astype(BF16), vc) + _dot(a_n.astype(BF16), vn)
        o = _rms(o, sg_ref[...]) * (1.0 - LAM_INIT)
        ob_ref[r, h * DV_B:(h + 1) * DV_B] = o.astype(BF16)


def _sample(qa, qi, sm, nka, nva, nki, qb, nkb, nvb, cak, cav, cai, cbk, cbv,
            lq1, lk1, lq2, lk2, sg, batch, ts, past, k_top):
    group = SAMPLE_GROUP
    assert batch % group == 0
    rows = group * ts
    row = lambda b: (b, 0)
    head = lambda b: (0, b, 0)
    cache = lambda b: (b, 0, 0)
    const = lambda b: (0, 0)
    return pl.pallas_call(
        functools.partial(_sample_body, k_top=k_top, past=past, ts=ts, group=group),
        out_shape=(jax.ShapeDtypeStruct((batch * ts, H_A * DH_A), BF16),
                   jax.ShapeDtypeStruct((batch * ts, H_B * DV_B), BF16)),
        grid=(batch // group,),
        in_specs=[
            pl.BlockSpec((H_A, rows, 128), head),
            pl.BlockSpec((H_I, rows, D_IDX), head),
            pl.BlockSpec((rows, 256), row),
            pl.BlockSpec((rows, 128), row),
            pl.BlockSpec((rows, 128), row),
            pl.BlockSpec((rows, D_IDX), row),
            pl.BlockSpec((2 * H_B, rows, 128), head),
            pl.BlockSpec((2 * H_B, rows, 128), head),
            pl.BlockSpec((rows, H_B * DV_B), row),
            pl.BlockSpec((group, DH_A, past), cache),
            pl.BlockSpec((group, DH_A, past), cache),
            pl.BlockSpec((group, D_IDX, past), cache),
            pl.BlockSpec((group, 2 * H_B * DH_B, past), cache),
            pl.BlockSpec((group, past * H_B, DV_B), cache),
            pl.BlockSpec((1, DH_B), const),
            pl.BlockSpec((1, DH_B), const),
            pl.BlockSpec((1, DH_B), const),
            pl.BlockSpec((1, DH_B), const),
            pl.BlockSpec((1, DV_B), const),
        ],
        out_specs=(pl.BlockSpec((rows, H_A * DH_A), row), pl.BlockSpec((rows, H_B * DV_B), row)),
        scratch_shapes=[
            pltpu.VMEM((group, H_A * ts, past), BF16),
            pltpu.VMEM((group, H_A * ts, LANES), BF16),
        ],
        compiler_params=pltpu.CompilerParams(
            dimension_semantics=("arbitrary",), vmem_limit_bytes=VMEM_LIMIT),
        name="sample",
    )(qa, qi, sm, nka, nva, nki, qb, nkb, nvb, cak, cav, cai, cbk, cbv, lq1, lk1, lq2, lk2, sg)


def _finish_body(x_ref, oa_ref, ob_ref, wo_ref, gm_ref, wu_ref, wd_ref, gf_ref, y_ref):
    o = jnp.concatenate([oa_ref[...], ob_ref[...]], axis=1)
    h1 = x_ref[...] + _dot(o, wo_ref[...])
    hn = _rms(h1, gm_ref[...]).astype(BF16)
    acc = h1
    for c in range(D_FF // D_MODEL):
        u = jnp.maximum(_dot(hn, wu_ref[:, c * D_MODEL:(c + 1) * D_MODEL]), 0.0)
        acc = acc + _dot((u * u).astype(BF16), wd_ref[c * D_MODEL:(c + 1) * D_MODEL, :])
    y_ref[...] = _rms(acc, gf_ref[...])


def _finish(x, oa, ob, wo, gm, wu, wd, gf, tm):
    m = x.shape[0]
    assert m % tm == 0
    row = lambda i: (i, 0)
    const = lambda i: (0, 0)
    resident = dict(pipeline_mode=pl.Buffered(1))
    return pl.pallas_call(
        _finish_body,
        out_shape=jax.ShapeDtypeStruct((m, D_MODEL), F32),
        grid=(m // tm,),
        in_specs=[
            pl.BlockSpec((tm, D_MODEL), row),
            pl.BlockSpec((tm, 512), row),
            pl.BlockSpec((tm, 512), row),
            pl.BlockSpec((D_MODEL, D_MODEL), const, **resident),
            pl.BlockSpec((1, D_MODEL), const),
            pl.BlockSpec((D_MODEL, D_FF), const, **resident),
            pl.BlockSpec((D_FF, D_MODEL), const, **resident),
            pl.BlockSpec((1, D_MODEL), const),
        ],
        out_specs=pl.BlockSpec((tm, D_MODEL), row),
        compiler_params=pltpu.CompilerParams(
            dimension_semantics=("arbitrary",), vmem_limit_bytes=VMEM_LIMIT),
        name="finish",
    )(x, oa, ob, wo, gm, wu, wd, gf)


def kernel(x_prompt, x_sample, cache_a_k, cache_a_v, cache_a_idx_k, cache_b_k, cache_b_v,
           meta_tokens, attn_norm_g, w_in, idx_k_norm_g, idx_k_norm_b,
           lambda_q1, lambda_k1, lambda_q2, lambda_k2, subln_g, w_o,
           mlp_norm_g, w_up, w_down, final_norm_g):
    batch, seq, _ = x_prompt.shape
    dec_batch, ts, _ = x_sample.shape
    past = cache_a_k.shape[2]
    assert attn_norm_g.shape[0] == 1, "single-layer step"
    assert seq % TQ == 0 and ts == 16 and past % LANES == 0 and past // CHUNK == (past + ts - 1) // CHUNK
    n = N_META + seq
    k_top_p = min(TOPK_MAX, seq // 4)
    k_top_s = min(TOPK_MAX, (past + ts) // 4)

    w_t = w_in[0].T.astype(BF16)
    w = jnp.concatenate(
        [w_t[0:512],
         w_t[964:1476],
         w_t[1476:1988],
         w_t[1988:2500],
         w_t[640:896],
         w_t[512:640],
         w_t[896:964]],
        axis=0)
    w = jnp.pad(w, ((0, W_COLS - 2500), (0, 0))).T
    wvt = w_t[1476:2500]
    wo = w_o[0].astype(BF16)
    wu = w_up[0].astype(BF16)
    wd = w_down[0].astype(BF16)
    g_attn = attn_norm_g[0][None]
    g_mlp = mlp_norm_g[0][None]
    g_fin = final_norm_g[None]
    kng = idx_k_norm_g[0][None]
    knb = idx_k_norm_b[0][None]
    lq1, lk1, lq2, lk2 = lambda_q1[0][None], lambda_k1[0][None], lambda_q2[0][None], lambda_k2[0][None]
    sg = subln_g[0][None]

    xp = x_prompt.reshape(batch * seq, D_MODEL)
    xs = x_sample.reshape(dec_batch * ts, D_MODEL)

    (qa_p, qi_p, qb_p, kbh_p, vb_p, kab_p, kib_p, vt_p, wt_p, vbt_p, smt_p, kbt_p) = _proj(
        xp, g_attn, w, wvt, kng, knb, 512, N_META, seq, True)
    (_, _, _, kbh_m, vb_m, kab_m, kib_m, kb_m, vbb_m, sm_m, vaa_m) = _proj(
        meta_tokens, g_attn, w, wvt, kng, knb, N_META, 0, N_META, False)
    (qa_s, qi_s, qb_s, kbh_s, vb_s, kab_s, kib_s, kb_s, vbb_s, sm_s, vaa_s) = _proj(
        xs, g_attn, w, wvt, kng, knb, dec_batch * ts, past, dec_batch * ts, False)

    oa_p = _dsa(qa_p, qi_p, wt_p, kab_p, kib_p, vt_p, kab_m, kib_m, vaa_m, batch, seq, k_top_p)
    ob_p = _diff(qb_p, kbh_p, vbt_p, kbh_m, vbb_m, lq1, lk1, lq2, lk2, sg, batch, seq)
    y_prompt = _finish(xp, oa_p, ob_p, wo, g_mlp, wu, wd, g_fin, 512).reshape(batch, seq, D_MODEL)

    oa_s, ob_s = _sample(
        qa_s, qi_s, sm_s, kab_s, vaa_s, kib_s, qb_s, kbh_s, vbb_s,
        jnp.swapaxes(cache_a_k[0], 1, 2), jnp.swapaxes(cache_a_v[0], 1, 2),
        jnp.swapaxes(cache_a_idx_k[0], 1, 2),
        jnp.transpose(cache_b_k[0], (0, 2, 3, 4, 1)).reshape(dec_batch, 2 * H_B * DH_B, past),
        cache_b_v[0].reshape(dec_batch, past * H_B, DV_B),
        lq1, lk1, lq2, lk2, sg, dec_batch, ts, past, k_top_s)
    y_sample = _finish(xs, oa_s, ob_s, wo, g_mlp, wu, wd, g_fin, dec_batch * ts).reshape(
        dec_batch, ts, D_MODEL)

    def with_meta_t(meta_rows, main_t):
        meta_b = jnp.broadcast_to(meta_rows.T[None], (batch, 64, N_META))
        return jnp.swapaxes(jnp.concatenate([meta_b, main_t], axis=2), 1, 2)[None]

    new_a_k_p = with_meta_t(sm_m[:, SM_KA:SM_KA + 64], smt_p[:, SM_KA:SM_KA + 64])
    new_a_v_p = with_meta_t(sm_m[:, SM_VA:SM_VA + 64], smt_p[:, SM_VA:SM_VA + 64])
    new_a_i_p = with_meta_t(sm_m[:, SM_KI:SM_KI + 64], smt_p[:, SM_KI:SM_KI + 64])
    kb_meta_t = jnp.broadcast_to(kb_m.T.reshape(1, H_B, 2, DH_B, N_META), (batch, H_B, 2, DH_B, N_META))
    kb_main_t = kbt_p.reshape(batch, H_B, 2, DH_B, seq)
    new_b_k_p = jnp.transpose(
        jnp.concatenate([kb_meta_t, kb_main_t], axis=4), (0, 4, 1, 2, 3))[None]
    new_b_v_p = jnp.concatenate(
        [jnp.broadcast_to(vb_m.reshape(1, N_META, H_B, DV_B), (batch, N_META, H_B, DV_B)),
         vb_p.reshape(batch, seq, H_B, DV_B)], axis=1)[None]
    new_a_k_s = sm_s[:, SM_KA:SM_KA + 64].reshape(1, dec_batch, ts, 64)
    new_a_v_s = sm_s[:, SM_VA:SM_VA + 64].reshape(1, dec_batch, ts, 64)
    new_a_i_s = sm_s[:, SM_KI:SM_KI + 64].reshape(1, dec_batch, ts, 64)
    new_b_k_s = kb_s.reshape(1, dec_batch, ts, H_B, 2, DH_B)
    new_b_v_s = vb_s.reshape(1, dec_batch, ts, H_B, DV_B)
    return (y_prompt, y_sample, new_a_k_p, new_a_v_p, new_a_i_p, new_b_k_p, new_b_v_p,
            new_a_k_s, new_a_v_s, new_a_i_s, new_b_k_s, new_b_v_s)
```

```python
import functools

import jax
import jax.numpy as jnp
import numpy as np
from jax import lax
from jax.experimental import pallas as pl
from jax.experimental.pallas import tpu as pltpu

F32 = jnp.float32
BF16 = jnp.bfloat16
I32 = jnp.int32
I16 = jnp.int16
HALF16 = 32768

D_MODEL = 1024
CHUNK = 64
N_META = 16
H_A = 8
DH_A = 64
H_I = 4
D_IDX = 64
TOPK_MAX = 256
H_B = 4
DH_B = 64
DV_B = 2 * DH_B
D_FF = 4 * D_MODEL
EPS = 1e-6
LAM_INIT = 0.2

LANES = 128
C_QA, C_QB, C_KB, C_VB, C_QI, C_SM = 0, 512, 1024, 1536, 2048, 2304
W_COLS = 2560
SM_KA, SM_VA, SM_KI, SM_WI = 0, 64, 128, 192

LOG2E = float(np.float32(1.4426950408889634))
Q_SCALE = (DH_A ** -0.5) * LOG2E
N_SLOPE_PIECES = 3


def _bf16_pieces(value):
    rest = np.float32(value)
    pieces = []
    for _ in range(N_SLOPE_PIECES):
        piece = np.float32(rest.astype(jnp.bfloat16))
        pieces.append(float(piece))
        rest = np.float32(rest - piece)
    assert rest == 0.0
    return pieces
W_SCALE = (H_I ** -0.5) * (D_IDX ** -0.5)
SLOPES_A = tuple(2.0 ** (-8.0 * (i + 1) / H_A) for i in range(H_A))
SLOPES_B = tuple(2.0 ** (-8.0 * (i + 1) / H_B) for i in range(H_B))

INT_MIN = -(2 ** 31)
NEG_INIT = -1e30
TQ = 256
ONES_ROWS = 16
VT_ROWS = DV_B + ONES_ROWS
VTA_ROWS = DH_A + ONES_ROWS
CNT_ROWS = 32
SAMPLE_GROUP = 4
VMEM_LIMIT = 56 * 1024 * 1024

_NT = (((1,), (1,)), ((), ()))


def _dot(a, b):
    return jnp.dot(a, b, preferred_element_type=F32)


def _dot_nt(a, b):
    return lax.dot_general(a, b, _NT, preferred_element_type=F32)


def _rms(x, g):
    return (x * lax.rsqrt(jnp.mean(x * x, axis=-1, keepdims=True) + EPS)) * g


def _to_key(score):
    b = lax.bitcast_convert_type(score, I32)
    return b ^ ((b >> 31) & jnp.int32(0x7FFFFFFF))


def _f32_from_key(key):
    return lax.bitcast_convert_type(key ^ ((key >> 31) & jnp.int32(0x7FFFFFFF)), F32)


def _bf16_from_key(key):
    bits = key ^ ((key >> 15) & jnp.int32(0x7FFF))
    return lax.bitcast_convert_type(jnp.left_shift(bits, 16), F32).astype(BF16)


KEY16_LOWEST = -32640
KEY32_LOWEST = -2139095040
KEY32_HIGHEST = 2139095039


def _wide(x, width):
    reps = width // LANES
    return x if reps == 1 else jnp.concatenate([x] * reps, axis=1)


def _proj_body(x_ref, g_ref, w_ref, wvt_ref, kng_ref, knb_ref,
               qa_ref, qi_ref, qb_ref, kbh_ref, vb_ref, kab_ref, kib_ref, *mode_refs,
               pos0, period, key_major):
    x = x_ref[...]
    hn = _rms(x, g_ref[...]).astype(BF16)
    rows = x.shape[0]

    def mm(lo, width):
        return _dot(hn, w_ref[:, lo:lo + width])

    lane = lax.broadcasted_iota(I32, (rows, 64), 1)
    r = pl.program_id(0) * rows + lax.broadcasted_iota(I32, (rows, 64), 0)
    pos = pos0 + r % period
    pos_cols = jnp.where(lane >= 2 * N_SLOPE_PIECES, 0,
                         jnp.where(lane % 2 == 0, (pos // 256) * 256, pos % 256))
    pos_cols = pos_cols.astype(F32).astype(BF16)

    def slope_cols(slope):
        cols = jnp.zeros((rows, 64), F32)
        for n, piece in enumerate(_bf16_pieces(slope * LOG2E)):
            cols = jnp.where(lane // 2 == n, piece, cols)
        return cols.astype(BF16)

    z = mm(C_QA, 512) * Q_SCALE
    for h in range(H_A):
        qa_ref[h] = jnp.concatenate(
            [z[:, h * DH_A:(h + 1) * DH_A].astype(BF16), slope_cols(SLOPES_A[h])], axis=1)
    z = mm(C_QB, 512) * Q_SCALE
    for hc in range(2 * H_B):
        qb_ref[hc] = jnp.concatenate(
            [z[:, hc * DH_B:(hc + 1) * DH_B].astype(BF16), slope_cols(SLOPES_B[hc // 2])], axis=1)
    if key_major:
        vt_ref, wt_ref, vbt_ref, smt_ref, kbt_ref = mode_refs
    else:
        kb_ref, vbb_ref, sm_ref, vaa_ref = mode_refs
    z = mm(C_KB, 512)
    if not key_major:
        kb_ref[...] = z
    for hc in range(2 * H_B):
        kbh_ref[hc] = jnp.concatenate([z[:, hc * DH_B:(hc + 1) * DH_B].astype(BF16), pos_cols], axis=1)
    z = mm(C_VB, 512)
    for h in range(H_B):
        vb_ref[pl.ds(h, rows, stride=H_B), :] = z[:, h * DV_B:(h + 1) * DV_B]
    if not key_major:
        vbb_ref[...] = z.astype(BF16)
    z = mm(C_QI, 256)
    for h in range(H_I):
        qi_ref[h] = z[:, h * D_IDX:(h + 1) * D_IDX].astype(BF16)
    z = mm(C_SM, 256)
    ka = z[:, SM_KA:SM_KA + 64]
    va = z[:, SM_VA:SM_VA + 64]
    ki = z[:, SM_KI:SM_KI + 64]
    xc = ki - jnp.mean(ki, axis=-1, keepdims=True)
    ki = xc * lax.rsqrt(jnp.mean(xc * xc, axis=-1, keepdims=True) + EPS) * kng_ref[...] + knb_ref[...]
    kab_ref[...] = jnp.concatenate([ka.astype(BF16), pos_cols], axis=1)
    kib_ref[...] = ki.astype(BF16)
    if not key_major:
        sm_ref[:, 0:128] = z[:, 0:128]
        sm_ref[:, SM_KI:SM_KI + 64] = ki
        sm_ref[:, SM_WI:SM_WI + 64] = z[:, SM_WI:SM_WI + 64] * W_SCALE
        vaa_ref[...] = jnp.concatenate([va.astype(BF16), jnp.ones((rows, 64), BF16)], axis=1)
    else:
        kv_t = z[:, 0:128].T
        for c in range(rows // TQ):
            vt_ref[c] = jnp.concatenate(
                [kv_t[64:128, c * TQ:(c + 1) * TQ].astype(BF16), jnp.ones((ONES_ROWS, TQ), BF16)], axis=0)
        kw_t = jnp.concatenate([ki, z[:, SM_WI:SM_WI + 64] * W_SCALE], axis=1).T
        wt_ref[...] = kw_t[64:72, :]
        smt_ref[0:128, :] = kv_t
        smt_ref[128:192, :] = kw_t[0:64, :]
        kbt_ref[...] = _dot_nt(wvt_ref[0:512, :], hn)
        vb_t = _dot_nt(wvt_ref[512:1024, :], hn)
        for c in range(rows // TQ):
            for h in range(H_B):
                vbt_ref[c, h] = jnp.concatenate(
                    [vb_t[h * DV_B:(h + 1) * DV_B, c * TQ:(c + 1) * TQ].astype(BF16),
                     jnp.ones((VT_ROWS - DV_B, TQ), BF16)], axis=0)


def _proj(x, g, w, wvt, kng, knb, tm, pos0, period, key_major):
    m = x.shape[0]
    assert m % tm == 0
    row = lambda i: (i, 0)
    head = lambda i: (0, i, 0)
    const = lambda i: (0, 0)
    out_shape = (
        jax.ShapeDtypeStruct((H_A, m, 128), BF16),
        jax.ShapeDtypeStruct((H_I, m, D_IDX), BF16),
        jax.ShapeDtypeStruct((2 * H_B, m, 128), BF16),
        jax.ShapeDtypeStruct((2 * H_B, m, 128), BF16),
        jax.ShapeDtypeStruct((m * H_B, DV_B), F32),
        jax.ShapeDtypeStruct((m, 128), BF16),
        jax.ShapeDtypeStruct((m, 64), BF16),
    )
    out_specs = (
        pl.BlockSpec((H_A, tm, 128), head),
        pl.BlockSpec((H_I, tm, D_IDX), head),
        pl.BlockSpec((2 * H_B, tm, 128), head),
        pl.BlockSpec((2 * H_B, tm, 128), head),
        pl.BlockSpec((tm * H_B, DV_B), row),
        pl.BlockSpec((tm, 128), row),
        pl.BlockSpec((tm, 64), row),
    )
    if not key_major:
        out_shape += (
            jax.ShapeDtypeStruct((m, 512), F32),
            jax.ShapeDtypeStruct((m, 512), BF16),
            jax.ShapeDtypeStruct((m, 256), F32),
            jax.ShapeDtypeStruct((m, 128), BF16),
        )
        out_specs += (
            pl.BlockSpec((tm, 512), row),
            pl.BlockSpec((tm, 512), row),
            pl.BlockSpec((tm, 256), row),
            pl.BlockSpec((tm, 128), row),
        )
    else:
        assert tm % TQ == 0 and period % tm == 0 and m % period == 0
        per_stream = period // tm
        stream_cols = lambda i: (i // per_stream, 0, i % per_stream)
        out_shape += (
            jax.ShapeDtypeStruct((m // TQ, VTA_ROWS, TQ), BF16),
            jax.ShapeDtypeStruct((8, m), F32),
            jax.ShapeDtypeStruct((m // TQ, H_B, VT_ROWS, TQ), BF16),
            jax.ShapeDtypeStruct((m // period, 192, period), F32),
            jax.ShapeDtypeStruct((m // period, 512, period), F32),
        )
        out_specs += (
            pl.BlockSpec((tm // TQ, VTA_ROWS, TQ), lambda i: (i, 0, 0)),
            pl.BlockSpec((8, tm), lambda i: (0, i)),
            pl.BlockSpec((tm // TQ, H_B, VT_ROWS, TQ), lambda i: (i, 0, 0, 0)),
            pl.BlockSpec((None, 192, tm), stream_cols),
            pl.BlockSpec((None, 512, tm), stream_cols),
        )
    return pl.pallas_call(
        functools.partial(_proj_body, pos0=pos0, period=period, key_major=key_major),
        out_shape=out_shape,
        grid=(m // tm,),
        in_specs=[
            pl.BlockSpec((tm, D_MODEL), row),
            pl.BlockSpec((1, D_MODEL), const),
            pl.BlockSpec((D_MODEL, W_COLS), const),
            pl.BlockSpec((2 * H_B * DV_B, D_MODEL), const),
            pl.BlockSpec((1, D_IDX), const),
            pl.BlockSpec((1, D_IDX), const),
        ],
        out_specs=out_specs,
        compiler_params=pltpu.CompilerParams(
            dimension_semantics=("arbitrary",), vmem_limit_bytes=VMEM_LIMIT),
        name="proj",
    )(x, g, w, wvt, kng, knb)


def _dsa_body(qa_ref, qi_ref, wt_ref, ka_ref, ki_ref, vt_ref, mka_ref, mki_ref, mva_ref,
              o_ref, smeta_ref, smain_ref, bmeta_ref, bmain_ref,
              tsel_ref, m_ref, al_ref, acc_ref, p_ref, s_ref, mb_ref, seen_ref, *, k_top):
    i = pl.program_id(1)
    krow = lax.broadcasted_iota(I32, (TQ, TQ), 0)
    qcol = lax.broadcasted_iota(I32, (TQ, TQ), 1)

    def main_rows(j):
        return pl.ds(pl.multiple_of(j * TQ, TQ), TQ)

    qi_all = qi_ref[...].reshape(H_I * TQ, D_IDX)

    def scores(ki_blk):
        s4 = _dot_nt(ki_blk, qi_all)
        sc = None
        for h in range(H_I):
            t = jnp.maximum(s4[:, h * TQ:(h + 1) * TQ], 0.0) * wt_ref[h:h + 1, :]
            sc = t if sc is None else sc + t
        return sc

    sc = scores(mki_ref[...])
    smeta_ref[...] = sc
    bmeta_ref[...] = sc.astype(BF16)

    def score_block(j, sc):
        smain_ref[j] = sc
        bmain_ref[j] = sc.astype(BF16)

    def full_scores(j):
        return scores(ki_ref[main_rows(j), :])

    def last_scores():
        return jnp.where((krow // CHUNK) <= (qcol // CHUNK), full_scores(i), -jnp.inf)

    def score_two(u, carry):
        sc_a, sc_b = full_scores(2 * u), full_scores(2 * u + 1)
        score_block(2 * u, sc_a)
        score_block(2 * u + 1, sc_b)
        return carry

    lax.fori_loop(0, i // 2, score_two, 0)

    @pl.when(i % 2 == 1)
    def _odd_scores():
        sc_a, sc_b = full_scores(i - 1), last_scores()
        score_block(i - 1, sc_a)
        score_block(i, sc_b)

    @pl.when(i % 2 == 0)
    def _even_scores():
        score_block(i, last_scores())

    def count(meta_ref, main_ref, part, pred):
        def one(j):
            return part(pred(main_ref[j]))

        def two(u, c):
            return c + (one(2 * u) + one(2 * u + 1))

        c = lax.fori_loop(0, (i + 1) // 2, two, jnp.zeros((CNT_ROWS, TQ), F32))
        c = lax.cond(i % 2 == 0, lambda c: c + one(i), lambda c: c, c)
        cm = part(pred(meta_ref[...]), rows=N_META)
        return jnp.sum(c, axis=0, keepdims=True) + jnp.sum(cm, axis=0, keepdims=True)

    def part32(hit, rows=CNT_ROWS):
        ones = jnp.where(hit, 1.0, 0.0)
        return jnp.sum(ones.reshape(ones.shape[0] // rows, rows, TQ), axis=0)

    def part16(hit, rows=CNT_ROWS):
        ones = jnp.where(hit, jnp.bfloat16(1), jnp.bfloat16(0))
        acc = ones[0:rows]
        for t in range(1, ones.shape[0] // rows):
            acc = acc + ones[t * rows:(t + 1) * rows]
        return acc.astype(F32)

    kf = float(k_top)

    def step16(s, u):
        cand_u = u | jnp.left_shift(jnp.int32(1), 15 - s)
        cand = _bf16_from_key(jnp.maximum(cand_u, KEY16_LOWEST + HALF16) - HALF16)
        cnt = count(bmeta_ref, bmain_ref, part16, lambda x: x >= cand)
        return jnp.where(cnt >= kf, cand_u, u)

    u_hi = lax.fori_loop(0, 16, step16, jnp.zeros((1, TQ), I32))
    t_hi = _bf16_from_key(jnp.maximum(u_hi, KEY16_LOWEST + HALF16) - HALF16).astype(F32)
    base = jnp.maximum(_to_key(t_hi) - 65536, KEY32_LOWEST)

    def step32(s, d):
        cand_d = d | jnp.left_shift(jnp.int32(1), 16 - s)
        cand = _f32_from_key(jnp.minimum(base + cand_d, KEY32_HIGHEST))
        cnt = count(smeta_ref, smain_ref, part32, lambda x: x >= cand)
        return jnp.where(cnt >= kf, cand_d, d)

    d_lo = lax.fori_loop(0, 17, step32, jnp.zeros((1, TQ), I32))
    t_sel = _f32_from_key(jnp.minimum(base + d_lo, KEY32_HIGHEST))
    need = kf - count(smeta_ref, smain_ref, part32, lambda x: x > t_sel)
    tsel_ref[...] = jnp.broadcast_to(t_sel, tsel_ref.shape)

    m_ref[...] = jnp.full(m_ref.shape, NEG_INIT, F32)
    acc_ref[...] = jnp.zeros(acc_ref.shape, F32)
    thr = tsel_ref[0:1, :]

    def masked_logits(k_blk, sc, slot):
        nk = sc.shape[0]
        tie = sc == thr
        tri = (lax.broadcasted_iota(I32, (nk, nk), 1)
               <= lax.broadcasted_iota(I32, (nk, nk), 0)).astype(BF16)
        rank = _dot(tri, jnp.where(tie, 1.0, 0.0).astype(BF16)) + seen_ref[...]
        seen_ref[...] = rank[nk - 1:nk, :]
        negm = jnp.where(sc > thr, 0.0, jnp.where(tie, jnp.where(rank <= need, 0.0, -jnp.inf), -jnp.inf))
        for h in range(H_A):
            cols = slice(h * TQ, (h + 1) * TQ)
            s = _dot_nt(k_blk, qa_ref[h]) + negm
            s_ref[slot, 0:nk, cols] = s
            mb_ref[slot, :, cols] = jnp.max(s, axis=0, keepdims=True)

    def softmax(slot, nk, future):
        for h in range(H_A):
            cols = slice(h * TQ, (h + 1) * TQ)
            s = s_ref[slot, 0:nk, cols]
            if future is not None:
                s = s - (2.0 * LOG2E * SLOPES_A[h]) * future
            m_old = m_ref[:, cols]
            m_new = jnp.maximum(m_old, mb_ref[slot, :, cols])
            p_ref[slot, 0:nk, cols] = jnp.exp2(s - m_new).astype(BF16)
            al_ref[slot, :, cols] = jnp.exp2(m_old - m_new)
            m_ref[:, cols] = m_new

    def accumulate(vt_blk, slot, nk):
        acc_ref[...] = acc_ref[...] * al_ref[slot] + _dot(vt_blk, p_ref[slot, 0:nk, :])

    eye = (lax.broadcasted_iota(I32, (128, 128), 0)
           == lax.broadcasted_iota(I32, (128, 128), 1)).astype(BF16)
    seen_ref[...] = jnp.zeros(seen_ref.shape, F32)
    masked_logits(mka_ref[...], smeta_ref[...], 1)
    masked_logits(ka_ref[main_rows(0), :], smain_ref[0], 0)
    softmax(1, N_META, None)
    accumulate(_dot_nt(eye, mva_ref[...])[0:VTA_ROWS].astype(BF16), 1, N_META)

    p_ref[1] = jnp.zeros(p_ref.shape[1:], BF16)
    al_ref[1] = jnp.ones(al_ref.shape[1:], F32)

    def full_step(t, slot):
        masked_logits(ka_ref[main_rows(t + 1), :], smain_ref[t + 1], 1 - slot)
        accumulate(vt_ref[jnp.maximum(t - 1, 0)], 1 - slot, TQ)
        softmax(slot, TQ, None)

    def two_steps(u, carry):
        full_step(2 * u, 0)
        full_step(2 * u + 1, 1)
        return carry

    lax.fori_loop(0, i // 2, two_steps, 0)

    def last_step(slot):
        accumulate(vt_ref[jnp.maximum(i - 1, 0)], 1 - slot, TQ)
        softmax(slot, TQ, jnp.maximum(krow - qcol, 0).astype(F32))
        accumulate(vt_ref[i], slot, TQ)

    @pl.when(i % 2 == 1)
    def _odd():
        full_step(i - 1, 0)
        last_step(1)

    @pl.when(i % 2 == 0)
    def _even():
        last_step(0)

    def head_out_t(h):
        a = acc_ref[:, h * TQ:(h + 1) * TQ]
        return a[0:DH_A] / a[DH_A:DH_A + 1]

    for h in range(0, H_A, 2):
        pair_t = jnp.concatenate([head_out_t(h), head_out_t(h + 1)], axis=0)
        o_ref[:, h * DH_A:(h + 2) * DH_A] = pair_t.T.astype(BF16)


def _dsa(qa, qi, wt, kab, kib, vt, mka, mki, mva, batch, seq, k_top):
    nq = seq // TQ
    qrow = lambda b, i: (b * nq + i, 0)
    qhead = lambda b, i: (0, b * nq + i, 0)
    kv = lambda b, i: (b, 0)
    const = lambda b, i: (0, 0)
    return pl.pallas_call(
        functools.partial(_dsa_body, k_top=k_top),
        out_shape=jax.ShapeDtypeStruct((batch * seq, H_A * DH_A), BF16),
        grid=(batch, nq),
        in_specs=[
            pl.BlockSpec((H_A, TQ, 128), qhead),
            pl.BlockSpec((H_I, TQ, D_IDX), qhead),
            pl.BlockSpec((8, TQ), lambda b, i: (0, b * nq + i)),
            pl.BlockSpec((seq, 128), kv),
            pl.BlockSpec((seq, D_IDX), kv),
            pl.BlockSpec((nq, VTA_ROWS, TQ), lambda b, i: (b, 0, 0)),
            pl.BlockSpec((N_META, 128), const),
            pl.BlockSpec((N_META, D_IDX), const),
            pl.BlockSpec((N_META, 128), const),
        ],
        out_specs=pl.BlockSpec((TQ, H_A * DH_A), qrow),
        scratch_shapes=[
            pltpu.VMEM((N_META, TQ), F32),
            pltpu.VMEM((nq, TQ, TQ), F32),
            pltpu.VMEM((N_META, TQ), BF16),
            pltpu.VMEM((nq, TQ, TQ), BF16),
            pltpu.VMEM((8, TQ), F32),
            pltpu.VMEM((1, H_A * TQ), F32),
            pltpu.VMEM((2, 1, H_A * TQ), F32),
            pltpu.VMEM((VTA_ROWS, H_A * TQ), F32),
            pltpu.VMEM((2, TQ, H_A * TQ), BF16),
            pltpu.VMEM((2, TQ, H_A * TQ), F32),
            pltpu.VMEM((2, 1, H_A * TQ), F32),
            pltpu.VMEM((1, TQ), F32),
        ],
        compiler_params=pltpu.CompilerParams(
            dimension_semantics=("arbitrary", "arbitrary"), vmem_limit_bytes=VMEM_LIMIT),
        name="dsa",
    )(qa, qi, wt, kab, kib, vt, mka, mki, mva)


def _lambda(lq1_ref, lk1_ref, lq2_ref, lk2_ref):
    s1 = jnp.sum(lq1_ref[...] * lk1_ref[...], axis=-1, keepdims=True)
    s2 = jnp.sum(lq2_ref[...] * lk2_ref[...], axis=-1, keepdims=True)
    return jnp.exp(s1) - jnp.exp(s2) + LAM_INIT


def _diff_body(qb_ref, kb_ref, vt_ref, mkb_ref, mvb_ref, lq1_ref, lk1_ref, lq2_ref, lk2_ref,
               sg_ref, o_ref, m_ref, al_ref, acc_ref, p_ref, s_ref, mb_ref):
    i = pl.program_id(1)
    n_hc = 2 * H_B
    krow = lax.broadcasted_iota(I32, (TQ, TQ), 0)
    qcol = lax.broadcasted_iota(I32, (TQ, TQ), 1)
    m_ref[...] = jnp.full(m_ref.shape, NEG_INIT, F32)
    acc_ref[...] = jnp.zeros(acc_ref.shape, F32)

    def main_rows(j):
        return pl.ds(pl.multiple_of(j * TQ, TQ), TQ)

    def logits(k_of, slot, nk, last=False):
        if last:
            negm = jnp.where((krow // CHUNK) <= (qcol // CHUNK), 0.0, -jnp.inf)
            future = jnp.maximum(krow - qcol, 0).astype(F32)
        for hc in range(n_hc):
            s = _dot_nt(k_of(hc), qb_ref[hc])
            if last:
                s = (s + negm) - (2.0 * LOG2E * SLOPES_B[hc // 2]) * future
            s_ref[slot, hc, 0:nk, :] = s
            mb_ref[slot, :, hc * TQ:(hc + 1) * TQ] = jnp.max(s, axis=0, keepdims=True)

    def softmax(slot, nk):
        for hc in range(n_hc):
            cols = slice(hc * TQ, (hc + 1) * TQ)
            m_old = m_ref[:, cols]
            m_new = jnp.maximum(m_old, mb_ref[slot, :, cols])
            p_ref[slot, 0:nk, cols] = jnp.exp2(s_ref[slot, hc, 0:nk, :] - m_new).astype(BF16)
            al_ref[slot, :, cols] = jnp.exp2(m_old - m_new)
            m_ref[:, cols] = m_new

    def accumulate(vt_of, slot, nk):
        for h in range(H_B):
            cols = slice(2 * h * TQ, (2 * h + 2) * TQ)
            acc_ref[h] = acc_ref[h] * al_ref[slot, :, cols] + _dot(vt_of(h), p_ref[slot, 0:nk, cols])

    eye = (lax.broadcasted_iota(I32, (DV_B, DV_B), 0)
           == lax.broadcasted_iota(I32, (DV_B, DV_B), 1)).astype(BF16)

    def meta_vt(h):
        v_t = _dot_nt(eye, mvb_ref[:, h * DV_B:(h + 1) * DV_B]).astype(BF16)
        return jnp.concatenate([v_t, jnp.ones((VT_ROWS - DV_B, N_META), BF16)], axis=0)

    def main_vt(t):
        return lambda h: vt_ref[t, h]

    def start(first_is_last):
        logits(lambda hc: mkb_ref[hc], 1, N_META)
        logits(lambda hc: kb_ref[hc, main_rows(0), :], 0, TQ, last=first_is_last)
        softmax(1, N_META)
        accumulate(meta_vt, 1, N_META)
        p_ref[1] = jnp.zeros(p_ref.shape[1:], BF16)
        al_ref[1] = jnp.ones(al_ref.shape[1:], F32)

    def full_step(t, slot, next_is_last=False):
        logits(lambda hc: kb_ref[hc, main_rows(t + 1), :], 1 - slot, TQ, last=next_is_last)
        accumulate(main_vt(jnp.maximum(t - 1, 0)), 1 - slot, TQ)
        softmax(slot, TQ)

    def two_steps(u, carry):
        full_step(2 * u, 0)
        full_step(2 * u + 1, 1)
        return carry

    def last_step(slot):
        accumulate(main_vt(jnp.maximum(i - 1, 0)), 1 - slot, TQ)
        softmax(slot, TQ)
        accumulate(main_vt(i), slot, TQ)

    @pl.when(i == 0)
    def _first_tile():
        start(True)
        last_step(0)

    @pl.when(i % 2 == 1)
    def _odd():
        start(False)
        lax.fori_loop(0, i // 2, two_steps, 0)
        full_step(i - 1, 0, next_is_last=True)
        last_step(1)

    @pl.when((i > 0) & (i % 2 == 0))
    def _even():
        start(False)
        lax.fori_loop(0, i // 2 - 1, two_steps, 0)
        full_step(i - 2, 0)
        full_step(i - 1, 1, next_is_last=True)
        last_step(0)

    lam = _lambda(lq1_ref, lk1_ref, lq2_ref, lk2_ref)
    for h in range(H_B):
        a = acc_ref[h]
        o0 = a[0:DV_B, 0:TQ] / a[DV_B:DV_B + 1, 0:TQ]
        o1 = a[0:DV_B, TQ:2 * TQ] / a[DV_B:DV_B + 1, TQ:2 * TQ]
        o = _rms((o0 - lam * o1).T, sg_ref[...]) * (1.0 - LAM_INIT)
        o_ref[:, h * DV_B:(h + 1) * DV_B] = o.astype(BF16)


def _diff(qb, kbh, vbt, mkb, mvb, lq1, lk1, lq2, lk2, sg, batch, seq):
    nq = seq // TQ
    qrow = lambda b, i: (b * nq + i, 0)
    qhead = lambda b, i: (0, b * nq + i, 0)
    const2 = lambda b, i: (0, 0)
    const3 = lambda b, i: (0, 0, 0)
    return pl.pallas_call(
        _diff_body,
        out_shape=jax.ShapeDtypeStruct((batch * seq, H_B * DV_B), BF16),
        grid=(batch, nq),
        in_specs=[
            pl.BlockSpec((2 * H_B, TQ, 128), qhead),
            pl.BlockSpec((2 * H_B, seq, 128), lambda b, i: (0, b, 0)),
            pl.BlockSpec((nq, H_B, VT_ROWS, TQ), lambda b, i: (b, 0, 0, 0)),
            pl.BlockSpec((2 * H_B, N_META, 128), const3),
            pl.BlockSpec((N_META, H_B * DV_B), const2),
            pl.BlockSpec((1, DH_B), const2),
            pl.BlockSpec((1, DH_B), const2),
            pl.BlockSpec((1, DH_B), const2),
            pl.BlockSpec((1, DH_B), const2),
            pl.BlockSpec((1, DV_B), const2),
        ],
        out_specs=pl.BlockSpec((TQ, H_B * DV_B), qrow),
        scratch_shapes=[
            pltpu.VMEM((1, 2 * H_B * TQ), F32),
            pltpu.VMEM((2, 1, 2 * H_B * TQ), F32),
            pltpu.VMEM((H_B, VT_ROWS, 2 * TQ), F32),
            pltpu.VMEM((2, TQ, 2 * H_B * TQ), BF16),
            pltpu.VMEM((2, 2 * H_B, TQ, TQ), F32),
            pltpu.VMEM((2, 1, 2 * H_B * TQ), F32),
        ],
        compiler_params=pltpu.CompilerParams(
            dimension_semantics=("arbitrary", "arbitrary"), vmem_limit_bytes=VMEM_LIMIT),
        name="diff",
    )(qb, kbh, vbt, mkb, mvb, lq1, lk1, lq2, lk2, sg)


def _sample_body(qa_ref, qi_ref, sm_ref, nka_ref, nva_ref, nki_ref, qb_ref, nkb_ref, nvb_ref,
                 cak_ref, cav_ref, cai_ref, cbk_ref, cbv_ref,
                 lq1_ref, lk1_ref, lq2_ref, lk2_ref, sg_ref,
                 oa_ref, ob_ref, pc_ref, pn_ref, *, k_top, past, ts, group):
    pad = LANES - ts
    row_c = lax.broadcasted_iota(I32, (ts, past), 0)
    col_c1 = lax.broadcasted_iota(I32, (ts, past), 1)
    row_n = lax.broadcasted_iota(I32, (ts, LANES), 0)
    col_n1 = lax.broadcasted_iota(I32, (ts, LANES), 1)
    new_ok = col_n1 < ts
    col_c = lax.broadcasted_iota(I32, (group * ts, past), 1)
    col_n = lax.broadcasted_iota(I32, (group * ts, LANES), 1)

    def pad_rows(x):
        return jnp.concatenate([x, jnp.zeros((pad,) + x.shape[1:], x.dtype)], axis=0)

    def stream_rows(g):
        return slice(g * ts, (g + 1) * ts)

    def indexer_scores(g):
        r = stream_rows(g)
        qi_all = qi_ref[:, r, :].reshape(H_I * ts, D_IDX)
        w = sm_ref[r, SM_WI:SM_WI + H_I]
        s4c = _dot(qi_all, cai_ref[g].astype(BF16))
        s4n = _dot_nt(qi_all, pad_rows(nki_ref[r, :]))
        sc_c = None
        sc_n = None
        for h in range(H_I):
            wh = w[:, h:h + 1]
            tc = jnp.maximum(s4c[h * ts:(h + 1) * ts], 0.0) * wh
            tn = jnp.maximum(s4n[h * ts:(h + 1) * ts], 0.0) * wh
            sc_c = tc if sc_c is None else sc_c + tc
            sc_n = tn if sc_n is None else sc_n + tn
        return sc_c, jnp.where(new_ok, sc_n, -jnp.inf)

    per_stream = [indexer_scores(g) for g in range(group)]
    sc_c = jnp.concatenate([p[0] for p in per_stream], axis=0)
    sc_n = jnp.concatenate([p[1] for p in per_stream], axis=0)

    def count(pred_c, pred_n):
        c = jnp.sum(jnp.where(pred_c, 1.0, 0.0), axis=1, keepdims=True)
        return c + jnp.sum(jnp.where(pred_n, 1.0, 0.0), axis=1, keepdims=True)

    def threshold(key):
        return _f32_from_key(jnp.clip(key, KEY32_LOWEST, KEY32_HIGHEST))

    def value_step(s, key):
        cand_key = key ^ jnp.left_shift(jnp.int32(1), 31 - s)
        cand = threshold(cand_key)
        cnt = count(sc_c >= cand, sc_n >= cand)
        return jnp.where(cnt >= float(k_top), cand_key, key)

    t = threshold(lax.fori_loop(0, 32, value_step, jnp.full((group * ts, 1), INT_MIN, I32)))
    need = float(k_top) - count(sc_c > t, sc_n > t)
    idx_n = col_n + past

    def tie_step(s, x):
        cand = x | jnp.left_shift(jnp.int32(1), 11 - s)
        cnt = count((sc_c == t) & (col_c < cand), (sc_n == t) & (idx_n < cand))
        return jnp.where(cnt < need, cand, x)

    x = lax.fori_loop(0, 12, tie_step, jnp.zeros((group * ts, 1), I32))
    negm_c_all = jnp.where((sc_c > t) | ((sc_c == t) & (col_c <= x)), 0.0, -jnp.inf)
    negm_n_all = jnp.where((sc_n > t) | ((sc_n == t) & (idx_n <= x)), 0.0, -jnp.inf)

    dist_c = (row_c - col_c1 + past).astype(F32)
    dist_n = jnp.abs(row_n - col_n1).astype(F32)
    lam = _lambda(lq1_ref, lk1_ref, lq2_ref, lk2_ref)
    neg_new = jnp.where(new_ok, 0.0, -jnp.inf)
    for g in range(group):
        _sample_attend(g, stream_rows(g), negm_c_all, negm_n_all, dist_c, dist_n, lam, neg_new, pad_rows,
                       qa_ref, nka_ref, nva_ref, qb_ref, nkb_ref, nvb_ref,
                       cak_ref, cav_ref, cbk_ref, cbv_ref, sg_ref, oa_ref, ob_ref, pc_ref, pn_ref,
                       past=past, ts=ts)


def _sample_attend(g, r, negm_c_all, negm_n_all, dist_c, dist_n, lam, neg_new, pad_rows,
                   qa_ref, nka_ref, nva_ref, qb_ref, nkb_ref, nvb_ref,
                   cak_ref, cav_ref, cbk_ref, cbv_ref, sg_ref, oa_ref, ob_ref, pc_ref, pn_ref,
                   *, past, ts):
    negm_c = negm_c_all[r]
    negm_n = negm_n_all[r]
    q_all = qa_ref[:, r, :][:, :, 0:DH_A].reshape(H_A * ts, DH_A)
    s_c = _dot(q_all, cak_ref[g].astype(BF16))
    s_n = _dot_nt(q_all, pad_rows(nka_ref[r, :][:, 0:DH_A]))
    for h in range(H_A):
        rows = slice(h * ts, (h + 1) * ts)
        lc = s_c[rows] + (negm_c - (LOG2E * SLOPES_A[h]) * dist_c)
        ln = s_n[rows] + (negm_n - (LOG2E * SLOPES_A[h]) * dist_n)
        m = jnp.maximum(jnp.max(lc, axis=1, keepdims=True), jnp.max(ln, axis=1, keepdims=True))
        m = jnp.maximum(m, NEG_INIT)
        pc_ref[g, rows, :] = jnp.exp2(lc - m).astype(BF16)
        pn_ref[g, rows, :] = jnp.exp2(ln - m).astype(BF16)
    vc_aug_t = jnp.concatenate([cav_ref[g].astype(BF16), jnp.ones((64, past), BF16)], axis=0)
    pv = _dot_nt(pc_ref[g], vc_aug_t) + _dot(pn_ref[g], pad_rows(nva_ref[r, :]))
    pv = pv / pltpu.roll(pv, DH_A, axis=1)
    oa_ref[r, :] = jnp.concatenate(
        [pv[h * ts:(h + 1) * ts, :DH_A] for h in range(H_A)], axis=1).astype(BF16)

    for h in range(H_B):
        a_c = None
        a_n = None
        for c in range(2):
            hc = 2 * h + c
            q = qb_ref[hc, r, :][:, 0:DH_B]
            kc_t = cbk_ref[g, hc * DH_B:(hc + 1) * DH_B, :].astype(BF16)
            lc = _dot(q, kc_t) - (LOG2E * SLOPES_B[h]) * dist_c
            ln = _dot_nt(q, pad_rows(nkb_ref[hc, r, :][:, 0:DH_B])) + (neg_new - (LOG2E * SLOPES_B[h]) * dist_n)
            m = jnp.maximum(jnp.max(lc, axis=1, keepdims=True), jnp.max(ln, axis=1, keepdims=True))
            pc = jnp.exp2(lc - m)
            pn = jnp.exp2(ln - m)
            l = jnp.sum(pc, axis=1, keepdims=True) + jnp.sum(pn, axis=1, keepdims=True)
            pc = pc / l
            pn = pn / l
            if c == 0:
                a_c, a_n = pc, pn
            else:
                a_c, a_n = a_c - lam * pc, a_n - lam * pn
        vc = cbv_ref[g, pl.ds(h, past, stride=H_B), :].astype(BF16)
        vn = pad_rows(nvb_ref[r, h * DV_B:(h + 1) * DV_B])
        o = _dot(a_c.astype(BF16), vc) + _dot(a_n.astype(BF16), vn)
        o = _rms(o, sg_ref[...]) * (1.0 - LAM_INIT)
        ob_ref[r, h * DV_B:(h + 1) * DV_B] = o.astype(BF16)


def _sample(qa, qi, sm, nka, nva, nki, qb, nkb, nvb, cak, cav, cai, cbk, cbv,
            lq1, lk1, lq2, lk2, sg, batch, ts, past, k_top):
    group = SAMPLE_GROUP
    assert batch % group == 0
    rows = group * ts
    row = lambda b: (b, 0)
    head = lambda b: (0, b, 0)
    cache = lambda b: (b, 0, 0)
    const = lambda b: (0, 0)
    return pl.pallas_call(
        functools.partial(_sample_body, k_top=k_top, past=past, ts=ts, group=group),
        out_shape=(jax.ShapeDtypeStruct((batch * ts, H_A * DH_A), BF16),
                   jax.ShapeDtypeStruct((batch * ts, H_B * DV_B), BF16)),
        grid=(batch // group,),
        in_specs=[
            pl.BlockSpec((H_A, rows, 128), head),
            pl.BlockSpec((H_I, rows, D_IDX), head),
            pl.BlockSpec((rows, 256), row),
            pl.BlockSpec((rows, 128), row),
            pl.BlockSpec((rows, 128), row),
            pl.BlockSpec((rows, D_IDX), row),
            pl.BlockSpec((2 * H_B, rows, 128), head),
            pl.BlockSpec((2 * H_B, rows, 128), head),
            pl.BlockSpec((rows, H_B * DV_B), row),
            pl.BlockSpec((group, DH_A, past), cache),
            pl.BlockSpec((group, DH_A, past), cache),
            pl.BlockSpec((group, D_IDX, past), cache),
            pl.BlockSpec((group, 2 * H_B * DH_B, past), cache),
            pl.BlockSpec((group, past * H_B, DV_B), cache),
            pl.BlockSpec((1, DH_B), const),
            pl.BlockSpec((1, DH_B), const),
            pl.BlockSpec((1, DH_B), const),
            pl.BlockSpec((1, DH_B), const),
            pl.BlockSpec((1, DV_B), const),
        ],
        out_specs=(pl.BlockSpec((rows, H_A * DH_A), row), pl.BlockSpec((rows, H_B * DV_B), row)),
        scratch_shapes=[
            pltpu.VMEM((group, H_A * ts, past), BF16),
            pltpu.VMEM((group, H_A * ts, LANES), BF16),
        ],
        compiler_params=pltpu.CompilerParams(
            dimension_semantics=("arbitrary",), vmem_limit_bytes=VMEM_LIMIT),
        name="sample",
    )(qa, qi, sm, nka, nva, nki, qb, nkb, nvb, cak, cav, cai, cbk, cbv, lq1, lk1, lq2, lk2, sg)


def _finish_body(x_ref, oa_ref, ob_ref, wo_ref, gm_ref, wu_ref, wd_ref, gf_ref, y_ref):
    o = jnp.concatenate([oa_ref[...], ob_ref[...]], axis=1)
    h1 = x_ref[...] + _dot(o, wo_ref[...])
    hn = _rms(h1, gm_ref[...]).astype(BF16)
    acc = h1
    for c in range(D_FF // D_MODEL):
        u = jnp.maximum(_dot(hn, wu_ref[:, c * D_MODEL:(c + 1) * D_MODEL]), 0.0)
        acc = acc + _dot((u * u).astype(BF16), wd_ref[c * D_MODEL:(c + 1) * D_MODEL, :])
    y_ref[...] = _rms(acc, gf_ref[...])


def _finish(x, oa, ob, wo, gm, wu, wd, gf, tm):
    m = x.shape[0]
    assert m % tm == 0
    row = lambda i: (i, 0)
    const = lambda i: (0, 0)
    resident = dict(pipeline_mode=pl.Buffered(1))
    return pl.pallas_call(
        _finish_body,
        out_shape=jax.ShapeDtypeStruct((m, D_MODEL), F32),
        grid=(m // tm,),
        in_specs=[
            pl.BlockSpec((tm, D_MODEL), row),
            pl.BlockSpec((tm, 512), row),
            pl.BlockSpec((tm, 512), row),
            pl.BlockSpec((D_MODEL, D_MODEL), const, **resident),
            pl.BlockSpec((1, D_MODEL), const),
            pl.BlockSpec((D_MODEL, D_FF), const, **resident),
            pl.BlockSpec((D_FF, D_MODEL), const, **resident),
            pl.BlockSpec((1, D_MODEL), const),
        ],
        out_specs=pl.BlockSpec((tm, D_MODEL), row),
        compiler_params=pltpu.CompilerParams(
            dimension_semantics=("arbitrary",), vmem_limit_bytes=VMEM_LIMIT),
        name="finish",
    )(x, oa, ob, wo, gm, wu, wd, gf)


def kernel(x_prompt, x_sample, cache_a_k, cache_a_v, cache_a_idx_k, cache_b_k, cache_b_v,
           meta_tokens, attn_norm_g, w_in, idx_k_norm_g, idx_k_norm_b,
           lambda_q1, lambda_k1, lambda_q2, lambda_k2, subln_g, w_o,
           mlp_norm_g, w_up, w_down, final_norm_g):
    batch, seq, _ = x_prompt.shape
    dec_batch, ts, _ = x_sample.shape
    past = cache_a_k.shape[2]
    assert attn_norm_g.shape[0] == 1, "single-layer step"
    assert seq % TQ == 0 and ts == 16 and past % LANES == 0 and past // CHUNK == (past + ts - 1) // CHUNK
    n = N_META + seq
    k_top_p = min(TOPK_MAX, seq // 4)
    k_top_s = min(TOPK_MAX, (past + ts) // 4)

    w_t = w_in[0].T.astype(BF16)
    w = jnp.concatenate(
        [w_t[0:512],
         w_t[964:1476],
         w_t[1476:1988],
         w_t[1988:2500],
         w_t[640:896],
         w_t[512:640],
         w_t[896:964]],
        axis=0)
    w = jnp.pad(w, ((0, W_COLS - 2500), (0, 0))).T
    wvt = w_t[1476:2500]
    wo = w_o[0].astype(BF16)
    wu = w_up[0].astype(BF16)
    wd = w_down[0].astype(BF16)
    g_attn = attn_norm_g[0][None]
    g_mlp = mlp_norm_g[0][None]
    g_fin = final_norm_g[None]
    kng = idx_k_norm_g[0][None]
    knb = idx_k_norm_b[0][None]
    lq1, lk1, lq2, lk2 = lambda_q1[0][None], lambda_k1[0][None], lambda_q2[0][None], lambda_k2[0][None]
    sg = subln_g[0][None]

    xp = x_prompt.reshape(batch * seq, D_MODEL)
    xs = x_sample.reshape(dec_batch * ts, D_MODEL)

    (qa_p, qi_p, qb_p, kbh_p, vb_p, kab_p, kib_p, vt_p, wt_p, vbt_p, smt_p, kbt_p) = _proj(
        xp, g_attn, w, wvt, kng, knb, 512, N_META, seq, True)
    (_, _, _, kbh_m, vb_m, kab_m, kib_m, kb_m, vbb_m, sm_m, vaa_m) = _proj(
        meta_tokens, g_attn, w, wvt, kng, knb, N_META, 0, N_META, False)
    (qa_s, qi_s, qb_s, kbh_s, vb_s, kab_s, kib_s, kb_s, vbb_s, sm_s, vaa_s) = _proj(
        xs, g_attn, w, wvt, kng, knb, dec_batch * ts, past, dec_batch * ts, False)

    oa_p = _dsa(qa_p, qi_p, wt_p, kab_p, kib_p, vt_p, kab_m, kib_m, vaa_m, batch, seq, k_top_p)
    ob_p = _diff(qb_p, kbh_p, vbt_p, kbh_m, vbb_m, lq1, lk1, lq2, lk2, sg, batch, seq)
    y_prompt = _finish(xp, oa_p, ob_p, wo, g_mlp, wu, wd, g_fin, 512).reshape(batch, seq, D_MODEL)

    oa_s, ob_s = _sample(
        qa_s, qi_s, sm_s, kab_s, vaa_s, kib_s, qb_s, kbh_s, vbb_s,
        jnp.swapaxes(cache_a_k[0], 1, 2), jnp.swapaxes(cache_a_v[0], 1, 2),
        jnp.swapaxes(cache_a_idx_k[0], 1, 2),
        jnp.transpose(cache_b_k[0], (0, 2, 3, 4, 1)).reshape(dec_batch, 2 * H_B * DH_B, past),
        cache_b_v[0].reshape(dec_batch, past * H_B, DV_B),
        lq1, lk1, lq2, lk2, sg, dec_batch, ts, past, k_top_s)
    y_sample = _finish(xs, oa_s, ob_s, wo, g_mlp, wu, wd, g_fin, dec_batch * ts).reshape(
        dec_batch, ts, D_MODEL)

    def with_meta_t(meta_rows, main_t):
        meta_b = jnp.broadcast_to(meta_rows.T[None], (batch, 64, N_META))
        return jnp.swapaxes(jnp.concatenate([meta_b, main_t], axis=2), 1, 2)[None]

    new_a_k_p = with_meta_t(sm_m[:, SM_KA:SM_KA + 64], smt_p[:, SM_KA:SM_KA + 64])
    new_a_v_p = with_meta_t(sm_m[:, SM_VA:SM_VA + 64], smt_p[:, SM_VA:SM_VA + 64])
    new_a_i_p = with_meta_t(sm_m[:, SM_KI:SM_KI + 64], smt_p[:, SM_KI:SM_KI + 64])
    kb_meta_t = jnp.broadcast_to(kb_m.T.reshape(1, H_B, 2, DH_B, N_META), (batch, H_B, 2, DH_B, N_META))
    kb_main_t = kbt_p.reshape(batch, H_B, 2, DH_B, seq)
    new_b_k_p = jnp.transpose(
        jnp.concatenate([kb_meta_t, kb_main_t], axis=4), (0, 4, 1, 2, 3))[None]
    new_b_v_p = jnp.concatenate(
        [jnp.broadcast_to(vb_m.reshape(1, N_META, H_B, DV_B), (batch, N_META, H_B, DV_B)),
         vb_p.reshape(batch, seq, H_B, DV_B)], axis=1)[None]
    new_a_k_s = sm_s[:, SM_KA:SM_KA + 64].reshape(1, dec_batch, ts, 64)
    new_a_v_s = sm_s[:, SM_VA:SM_VA + 64].reshape(1, dec_batch, ts, 64)
    new_a_i_s = sm_s[:, SM_KI:SM_KI + 64].reshape(1, dec_batch, ts, 64)
    new_b_k_s = kb_s.reshape(1, dec_batch, ts, H_B, 2, DH_B)
    new_b_v_s = vb_s.reshape(1, dec_batch, ts, H_B, DV_B)
    return (y_prompt, y_sample, new_a_k_p, new_a_v_p, new_a_i_p, new_b_k_p, new_b_v_p,
            new_a_k_s, new_a_v_s, new_a_i_s, new_b_k_s, new_b_v_s)
```

```python
import functools

import jax
import jax.numpy as jnp
import numpy as np
from jax import lax
from jax.experimental import pallas as pl
from jax.experimental.pallas import tpu as pltpu

F32 = jnp.float32
BF16 = jnp.bfloat16
I32 = jnp.int32
HALF16 = 32768

D_MODEL = 1024
CHUNK = 64
N_META = 16
H_A = 8
DH_A = 64
H_I = 4
D_IDX = 64
TOPK_MAX = 256
H_B = 4
DH_B = 64
DV_B = 2 * DH_B
D_FF = 4 * D_MODEL
EPS = 1e-6
LAM_INIT = 0.2

LANES = 128
C_QA, C_QB, C_KB, C_VB, C_QI, C_SM = 0, 512, 1024, 1536, 2048, 2304
W_COLS = 2560
SM_KA, SM_VA, SM_KI, SM_WI = 0, 64, 128, 192

LOG2E = float(np.float32(1.4426950408889634))
Q_SCALE = (DH_A ** -0.5) * LOG2E
N_SLOPE_PIECES = 3


def _bf16_pieces(value):
    rest = np.float32(value)
    pieces = []
    for _ in range(N_SLOPE_PIECES):
        piece = np.float32(rest.astype(jnp.bfloat16))
        pieces.append(float(piece))
        rest = np.float32(rest - piece)
    assert rest == 0.0
    return pieces
W_SCALE = (H_I ** -0.5) * (D_IDX ** -0.5)
SLOPES_A = tuple(2.0 ** (-8.0 * (i + 1) / H_A) for i in range(H_A))
SLOPES_B = tuple(2.0 ** (-8.0 * (i + 1) / H_B) for i in range(H_B))

INT_MIN = -(2 ** 31)
NEG_INIT = -1e30
TQ = 256
ONES_ROWS = 16
VT_ROWS = DV_B + ONES_ROWS
VTA_ROWS = DH_A + ONES_ROWS
CNT_ROWS = 32
SAMPLE_GROUP = 4
VMEM_LIMIT = 56 * 1024 * 1024

_NT = (((1,), (1,)), ((), ()))


def _dot(a, b):
    return jnp.dot(a, b, preferred_element_type=F32)


def _dot_nt(a, b):
    return lax.dot_general(a, b, _NT, preferred_element_type=F32)


def _rms(x, g):
    return (x * lax.rsqrt(jnp.mean(x * x, axis=-1, keepdims=True) + EPS)) * g


def _to_key(score):
    b = lax.bitcast_convert_type(score, I32)
    return b ^ ((b >> 31) & jnp.int32(0x7FFFFFFF))


def _f32_from_key(key):
    return lax.bitcast_convert_type(key ^ ((key >> 31) & jnp.int32(0x7FFFFFFF)), F32)


def _bf16_from_key(key):
    bits = key ^ ((key >> 15) & jnp.int32(0x7FFF))
    return lax.bitcast_convert_type(jnp.left_shift(bits, 16), F32).astype(BF16)


KEY16_LOWEST = -32640
KEY32_LOWEST = -2139095040
KEY32_HIGHEST = 2139095039


def _proj_body(x_ref, g_ref, w_ref, wvt_ref, kng_ref, knb_ref,
               qa_ref, qi_ref, qb_ref, kbh_ref, vb_ref, kab_ref, kib_ref, *mode_refs,
               pos0, period, key_major):
    x = x_ref[...]
    hn = _rms(x, g_ref[...]).astype(BF16)
    rows = x.shape[0]

    def mm(lo, width):
        return _dot(hn, w_ref[:, lo:lo + width])

    lane = lax.broadcasted_iota(I32, (rows, 64), 1)
    r = pl.program_id(0) * rows + lax.broadcasted_iota(I32, (rows, 64), 0)
    pos = pos0 + r % period
    pos_cols = jnp.where(lane >= 2 * N_SLOPE_PIECES, 0,
                         jnp.where(lane % 2 == 0, (pos // 256) * 256, pos % 256))
    pos_cols = pos_cols.astype(F32).astype(BF16)

    def slope_cols(slope):
        cols = jnp.zeros((rows, 64), F32)
        for n, piece in enumerate(_bf16_pieces(slope * LOG2E)):
            cols = jnp.where(lane // 2 == n, piece, cols)
        return cols.astype(BF16)

    z = mm(C_QA, 512) * Q_SCALE
    for h in range(H_A):
        qa_ref[h] = jnp.concatenate(
            [z[:, h * DH_A:(h + 1) * DH_A].astype(BF16), slope_cols(SLOPES_A[h])], axis=1)
    z = mm(C_QB, 512) * Q_SCALE
    for hc in range(2 * H_B):
        qb_ref[hc] = jnp.concatenate(
            [z[:, hc * DH_B:(hc + 1) * DH_B].astype(BF16), slope_cols(SLOPES_B[hc // 2])], axis=1)
    if key_major:
        vt_ref, wt_ref, vbt_ref, smt_ref, kbt_ref = mode_refs
    else:
        kb_ref, vbb_ref, sm_ref, vaa_ref = mode_refs
    z = mm(C_KB, 512)
    if not key_major:
        kb_ref[...] = z
    for hc in range(2 * H_B):
        kbh_ref[hc] = jnp.concatenate([z[:, hc * DH_B:(hc + 1) * DH_B].astype(BF16), pos_cols], axis=1)
    z = mm(C_VB, 512)
    for h in range(H_B):
        vb_ref[pl.ds(h, rows, stride=H_B), :] = z[:, h * DV_B:(h + 1) * DV_B]
    if not key_major:
        vbb_ref[...] = z.astype(BF16)
    z = mm(C_QI, 256)
    for h in range(H_I):
        qi_ref[h] = z[:, h * D_IDX:(h + 1) * D_IDX].astype(BF16)
    z = mm(C_SM, 256)
    ka = z[:, SM_KA:SM_KA + 64]
    va = z[:, SM_VA:SM_VA + 64]
    ki = z[:, SM_KI:SM_KI + 64]
    xc = ki - jnp.mean(ki, axis=-1, keepdims=True)
    ki = xc * lax.rsqrt(jnp.mean(xc * xc, axis=-1, keepdims=True) + EPS) * kng_ref[...] + knb_ref[...]
    kab_ref[...] = jnp.concatenate([ka.astype(BF16), pos_cols], axis=1)
    kib_ref[...] = ki.astype(BF16)
    if not key_major:
        sm_ref[:, 0:128] = z[:, 0:128]
        sm_ref[:, SM_KI:SM_KI + 64] = ki
        sm_ref[:, SM_WI:SM_WI + 64] = z[:, SM_WI:SM_WI + 64] * W_SCALE
        vaa_ref[...] = jnp.concatenate([va.astype(BF16), jnp.ones((rows, 64), BF16)], axis=1)
    else:
        kv_t = z[:, 0:128].T
        for c in range(rows // TQ):
            vt_ref[c] = jnp.concatenate(
                [kv_t[64:128, c * TQ:(c + 1) * TQ].astype(BF16), jnp.ones((ONES_ROWS, TQ), BF16)], axis=0)
        kw_t = jnp.concatenate([ki, z[:, SM_WI:SM_WI + 64] * W_SCALE], axis=1).T
        wt_ref[...] = kw_t[64:72, :]
        smt_ref[0:128, :] = kv_t
        smt_ref[128:192, :] = kw_t[0:64, :]
        kbt_ref[...] = _dot_nt(wvt_ref[0:512, :], hn)
        vb_t = _dot_nt(wvt_ref[512:1024, :], hn)
        for c in range(rows // TQ):
            for h in range(H_B):
                vbt_ref[c, h] = jnp.concatenate(
                    [vb_t[h * DV_B:(h + 1) * DV_B, c * TQ:(c + 1) * TQ].astype(BF16),
                     jnp.ones((VT_ROWS - DV_B, TQ), BF16)], axis=0)


def _proj(x, g, w, wvt, kng, knb, tm, pos0, period, key_major):
    m = x.shape[0]
    assert m % tm == 0
    row = lambda i: (i, 0)
    head = lambda i: (0, i, 0)
    const = lambda i: (0, 0)
    out_shape = (
        jax.ShapeDtypeStruct((H_A, m, 128), BF16),
        jax.ShapeDtypeStruct((H_I, m, D_IDX), BF16),
        jax.ShapeDtypeStruct((2 * H_B, m, 128), BF16),
        jax.ShapeDtypeStruct((2 * H_B, m, 128), BF16),
        jax.ShapeDtypeStruct((m * H_B, DV_B), F32),
        jax.ShapeDtypeStruct((m, 128), BF16),
        jax.ShapeDtypeStruct((m, 64), BF16),
    )
    out_specs = (
        pl.BlockSpec((H_A, tm, 128), head),
        pl.BlockSpec((H_I, tm, D_IDX), head),
        pl.BlockSpec((2 * H_B, tm, 128), head),
        pl.BlockSpec((2 * H_B, tm, 128), head),
        pl.BlockSpec((tm * H_B, DV_B), row),
        pl.BlockSpec((tm, 128), row),
        pl.BlockSpec((tm, 64), row),
    )
    if not key_major:
        out_shape += (
            jax.ShapeDtypeStruct((m, 512), F32),
            jax.ShapeDtypeStruct((m, 512), BF16),
            jax.ShapeDtypeStruct((m, 256), F32),
            jax.ShapeDtypeStruct((m, 128), BF16),
        )
        out_specs += (
            pl.BlockSpec((tm, 512), row),
            pl.BlockSpec((tm, 512), row),
            pl.BlockSpec((tm, 256), row),
            pl.BlockSpec((tm, 128), row),
        )
    else:
        assert tm % TQ == 0 and period % tm == 0 and m % period == 0
        per_stream = period // tm
        stream_cols = lambda i: (i // per_stream, 0, i % per_stream)
        out_shape += (
            jax.ShapeDtypeStruct((m // TQ, VTA_ROWS, TQ), BF16),
            jax.ShapeDtypeStruct((8, m), F32),
            jax.ShapeDtypeStruct((m // TQ, H_B, VT_ROWS, TQ), BF16),
            jax.ShapeDtypeStruct((m // period, 192, period), F32),
            jax.ShapeDtypeStruct((m // period, 512, period), F32),
        )
        out_specs += (
            pl.BlockSpec((tm // TQ, VTA_ROWS, TQ), lambda i: (i, 0, 0)),
            pl.BlockSpec((8, tm), lambda i: (0, i)),
            pl.BlockSpec((tm // TQ, H_B, VT_ROWS, TQ), lambda i: (i, 0, 0, 0)),
            pl.BlockSpec((None, 192, tm), stream_cols),
            pl.BlockSpec((None, 512, tm), stream_cols),
        )
    return pl.pallas_call(
        functools.partial(_proj_body, pos0=pos0, period=period, key_major=key_major),
        out_shape=out_shape,
        grid=(m // tm,),
        in_specs=[
            pl.BlockSpec((tm, D_MODEL), row),
            pl.BlockSpec((1, D_MODEL), const),
            pl.BlockSpec((D_MODEL, W_COLS), const),
            pl.BlockSpec((2 * H_B * DV_B, D_MODEL), const),
            pl.BlockSpec((1, D_IDX), const),
            pl.BlockSpec((1, D_IDX), const),
        ],
        out_specs=out_specs,
        compiler_params=pltpu.CompilerParams(
            dimension_semantics=("arbitrary",), vmem_limit_bytes=VMEM_LIMIT),
        name="proj",
    )(x, g, w, wvt, kng, knb)


def _dsa_body(qa_ref, qi_ref, wt_ref, ka_ref, ki_ref, vt_ref, mka_ref, mki_ref, mva_ref,
              o_ref, smeta_ref, smain_ref, bmeta_ref, bmain_ref,
              tsel_ref, m_ref, al_ref, acc_ref, p_ref, s_ref, mb_ref, seen_ref, *, k_top):
    i = pl.program_id(1)
    krow = lax.broadcasted_iota(I32, (TQ, TQ), 0)
    qcol = lax.broadcasted_iota(I32, (TQ, TQ), 1)

    def main_rows(j):
        return pl.ds(pl.multiple_of(j * TQ, TQ), TQ)

    qi_all = qi_ref[...].reshape(H_I * TQ, D_IDX)

    def scores(ki_blk):
        s4 = _dot_nt(ki_blk, qi_all)
        sc = None
        for h in range(H_I):
            t = jnp.maximum(s4[:, h * TQ:(h + 1) * TQ], 0.0) * wt_ref[h:h + 1, :]
            sc = t if sc is None else sc + t
        return sc

    sc = scores(mki_ref[...])
    smeta_ref[...] = sc
    bmeta_ref[...] = sc.astype(BF16)

    def score_block(j, sc):
        smain_ref[j] = sc
        bmain_ref[j] = sc.astype(BF16)

    def full_scores(j):
        return scores(ki_ref[main_rows(j), :])

    def last_scores():
        return jnp.where((krow // CHUNK) <= (qcol // CHUNK), full_scores(i), -jnp.inf)

    def score_two(u, carry):
        sc_a, sc_b = full_scores(2 * u), full_scores(2 * u + 1)
        score_block(2 * u, sc_a)
        score_block(2 * u + 1, sc_b)
        return carry

    lax.fori_loop(0, i // 2, score_two, 0)

    @pl.when(i % 2 == 1)
    def _odd_scores():
        sc_a, sc_b = full_scores(i - 1), last_scores()
        score_block(i - 1, sc_a)
        score_block(i, sc_b)

    @pl.when(i % 2 == 0)
    def _even_scores():
        score_block(i, last_scores())

    def count(meta_ref, main_ref, part, pred):
        def one(j):
            return part(pred(main_ref[j]))

        def two(u, c):
            return c + (one(2 * u) + one(2 * u + 1))

        c = lax.fori_loop(0, (i + 1) // 2, two, jnp.zeros((CNT_ROWS, TQ), F32))
        c = lax.cond(i % 2 == 0, lambda c: c + one(i), lambda c: c, c)
        cm = part(pred(meta_ref[...]), rows=N_META)
        return jnp.sum(c, axis=0, keepdims=True) + jnp.sum(cm, axis=0, keepdims=True)

    def part32(hit, rows=CNT_ROWS):
        ones = jnp.where(hit, 1.0, 0.0)
        return jnp.sum(ones.reshape(ones.shape[0] // rows, rows, TQ), axis=0)

    def part16(hit, rows=CNT_ROWS):
        ones = jnp.where(hit, jnp.bfloat16(1), jnp.bfloat16(0))
        acc = ones[0:rows]
        for t in range(1, ones.shape[0] // rows):
            acc = acc + ones[t * rows:(t + 1) * rows]
        return acc.astype(F32)

    kf = float(k_top)

    def step16(s, u):
        cand_u = u | jnp.left_shift(jnp.int32(1), 15 - s)
        cand = _bf16_from_key(jnp.maximum(cand_u, KEY16_LOWEST + HALF16) - HALF16)
        cnt = count(bmeta_ref, bmain_ref, part16, lambda x: x >= cand)
        return jnp.where(cnt >= kf, cand_u, u)

    u_hi = lax.fori_loop(0, 16, step16, jnp.zeros((1, TQ), I32))
    t_hi = _bf16_from_key(jnp.maximum(u_hi, KEY16_LOWEST + HALF16) - HALF16).astype(F32)
    base = jnp.maximum(_to_key(t_hi) - 65536, KEY32_LOWEST)

    def step32(s, d):
        cand_d = d | jnp.left_shift(jnp.int32(1), 16 - s)
        cand = _f32_from_key(jnp.minimum(base + cand_d, KEY32_HIGHEST))
        cnt = count(smeta_ref, smain_ref, part32, lambda x: x >= cand)
        return jnp.where(cnt >= kf, cand_d, d)

    d_lo = lax.fori_loop(0, 17, step32, jnp.zeros((1, TQ), I32))
    t_sel = _f32_from_key(jnp.minimum(base + d_lo, KEY32_HIGHEST))
    need = kf - count(smeta_ref, smain_ref, part32, lambda x: x > t_sel)
    tsel_ref[...] = jnp.broadcast_to(t_sel, tsel_ref.shape)

    m_ref[...] = jnp.full(m_ref.shape, NEG_INIT, F32)
    acc_ref[...] = jnp.zeros(acc_ref.shape, F32)
    thr = tsel_ref[0:1, :]

    def masked_logits(k_blk, sc, slot):
        nk = sc.shape[0]
        tie = sc == thr
        tri = (lax.broadcasted_iota(I32, (nk, nk), 1)
               <= lax.broadcasted_iota(I32, (nk, nk), 0)).astype(BF16)
        rank = _dot(tri, jnp.where(tie, 1.0, 0.0).astype(BF16)) + seen_ref[...]
        seen_ref[...] = rank[nk - 1:nk, :]
        negm = jnp.where(sc > thr, 0.0, jnp.where(tie, jnp.where(rank <= need, 0.0, -jnp.inf), -jnp.inf))
        for h in range(H_A):
            cols = slice(h * TQ, (h + 1) * TQ)
            s = _dot_nt(k_blk, qa_ref[h]) + negm
            s_ref[slot, 0:nk, cols] = s
            mb_ref[slot, :, cols] = jnp.max(s, axis=0, keepdims=True)

    def softmax(slot, nk, future):
        for h in range(H_A):
            cols = slice(h * TQ, (h + 1) * TQ)
            s = s_ref[slot, 0:nk, cols]
            if future is not None:
                s = s - (2.0 * LOG2E * SLOPES_A[h]) * future
            m_old = m_ref[:, cols]
            m_new = jnp.maximum(m_old, mb_ref[slot, :, cols])
            p_ref[slot, 0:nk, cols] = jnp.exp2(s - m_new).astype(BF16)
            al_ref[slot, :, cols] = jnp.exp2(m_old - m_new)
            m_ref[:, cols] = m_new

    def accumulate(vt_blk, slot, nk):
        acc_ref[...] = acc_ref[...] * al_ref[slot] + _dot(vt_blk, p_ref[slot, 0:nk, :])

    eye = (lax.broadcasted_iota(I32, (128, 128), 0)
           == lax.broadcasted_iota(I32, (128, 128), 1)).astype(BF16)
    seen_ref[...] = jnp.zeros(seen_ref.shape, F32)
    masked_logits(mka_ref[...], smeta_ref[...], 1)
    masked_logits(ka_ref[main_rows(0), :], smain_ref[0], 0)
    softmax(1, N_META, None)
    accumulate(_dot_nt(eye, mva_ref[...])[0:VTA_ROWS].astype(BF16), 1, N_META)

    p_ref[1] = jnp.zeros(p_ref.shape[1:], BF16)
    al_ref[1] = jnp.ones(al_ref.shape[1:], F32)

    def full_step(t, slot):
        masked_logits(ka_ref[main_rows(t + 1), :], smain_ref[t + 1], 1 - slot)
        accumulate(vt_ref[jnp.maximum(t - 1, 0)], 1 - slot, TQ)
        softmax(slot, TQ, None)

    def two_steps(u, carry):
        full_step(2 * u, 0)
        full_step(2 * u + 1, 1)
        return carry

    lax.fori_loop(0, i // 2, two_steps, 0)

    def last_step(slot):
        accumulate(vt_ref[jnp.maximum(i - 1, 0)], 1 - slot, TQ)
        softmax(slot, TQ, jnp.maximum(krow - qcol, 0).astype(F32))
        accumulate(vt_ref[i], slot, TQ)

    @pl.when(i % 2 == 1)
    def _odd():
        full_step(i - 1, 0)
        last_step(1)

    @pl.when(i % 2 == 0)
    def _even():
        last_step(0)

    def head_out_t(h):
        a = acc_ref[:, h * TQ:(h + 1) * TQ]
        return a[0:DH_A] / a[DH_A:DH_A + 1]

    for h in range(0, H_A, 2):
        pair_t = jnp.concatenate([head_out_t(h), head_out_t(h + 1)], axis=0)
        o_ref[:, h * DH_A:(h + 2) * DH_A] = pair_t.T.astype(BF16)


def _dsa(qa, qi, wt, kab, kib, vt, mka, mki, mva, batch, seq, k_top):
    nq = seq // TQ
    qrow = lambda b, i: (b * nq + i, 0)
    qhead = lambda b, i: (0, b * nq + i, 0)
    kv = lambda b, i: (b, 0)
    const = lambda b, i: (0, 0)
    return pl.pallas_call(
        functools.partial(_dsa_body, k_top=k_top),
        out_shape=jax.ShapeDtypeStruct((batch * seq, H_A * DH_A), BF16),
        grid=(batch, nq),
        in_specs=[
            pl.BlockSpec((H_A, TQ, 128), qhead),
            pl.BlockSpec((H_I, TQ, D_IDX), qhead),
            pl.BlockSpec((8, TQ), lambda b, i: (0, b * nq + i)),
            pl.BlockSpec((seq, 128), kv),
            pl.BlockSpec((seq, D_IDX), kv),
            pl.BlockSpec((nq, VTA_ROWS, TQ), lambda b, i: (b, 0, 0)),
            pl.BlockSpec((N_META, 128), const),
            pl.BlockSpec((N_META, D_IDX), const),
            pl.BlockSpec((N_META, 128), const),
        ],
        out_specs=pl.BlockSpec((TQ, H_A * DH_A), qrow),
        scratch_shapes=[
            pltpu.VMEM((N_META, TQ), F32),
            pltpu.VMEM((nq, TQ, TQ), F32),
            pltpu.VMEM((N_META, TQ), BF16),
            pltpu.VMEM((nq, TQ, TQ), BF16),
            pltpu.VMEM((8, TQ), F32),
            pltpu.VMEM((1, H_A * TQ), F32),
            pltpu.VMEM((2, 1, H_A * TQ), F32),
            pltpu.VMEM((VTA_ROWS, H_A * TQ), F32),
            pltpu.VMEM((2, TQ, H_A * TQ), BF16),
            pltpu.VMEM((2, TQ, H_A * TQ), F32),
            pltpu.VMEM((2, 1, H_A * TQ), F32),
            pltpu.VMEM((1, TQ), F32),
        ],
        compiler_params=pltpu.CompilerParams(
            dimension_semantics=("arbitrary", "arbitrary"), vmem_limit_bytes=VMEM_LIMIT),
        name="dsa",
    )(qa, qi, wt, kab, kib, vt, mka, mki, mva)


def _lambda(lq1_ref, lk1_ref, lq2_ref, lk2_ref):
    s1 = jnp.sum(lq1_ref[...] * lk1_ref[...], axis=-1, keepdims=True)
    s2 = jnp.sum(lq2_ref[...] * lk2_ref[...], axis=-1, keepdims=True)
    return jnp.exp(s1) - jnp.exp(s2) + LAM_INIT


def _diff_body(qb_ref, kb_ref, vt_ref, mkb_ref, mvb_ref, lq1_ref, lk1_ref, lq2_ref, lk2_ref,
               sg_ref, o_ref, m_ref, al_ref, acc_ref, p_ref, s_ref, mb_ref):
    i = pl.program_id(1)
    n_hc = 2 * H_B
    krow = lax.broadcasted_iota(I32, (TQ, TQ), 0)
    qcol = lax.broadcasted_iota(I32, (TQ, TQ), 1)
    m_ref[...] = jnp.full(m_ref.shape, NEG_INIT, F32)
    acc_ref[...] = jnp.zeros(acc_ref.shape, F32)

    def main_rows(j):
        return pl.ds(pl.multiple_of(j * TQ, TQ), TQ)

    def logits(k_of, slot, nk, last=False):
        if last:
            negm = jnp.where((krow // CHUNK) <= (qcol // CHUNK), 0.0, -jnp.inf)
            future = jnp.maximum(krow - qcol, 0).astype(F32)
        for hc in range(n_hc):
            s = _dot_nt(k_of(hc), qb_ref[hc])
            if last:
                s = (s + negm) - (2.0 * LOG2E * SLOPES_B[hc // 2]) * future
            s_ref[slot, hc, 0:nk, :] = s
            mb_ref[slot, :, hc * TQ:(hc + 1) * TQ] = jnp.max(s, axis=0, keepdims=True)

    def softmax(slot, nk):
        for hc in range(n_hc):
            cols = slice(hc * TQ, (hc + 1) * TQ)
            m_old = m_ref[:, cols]
            m_new = jnp.maximum(m_old, mb_ref[slot, :, cols])
            p_ref[slot, 0:nk, cols] = jnp.exp2(s_ref[slot, hc, 0:nk, :] - m_new).astype(BF16)
            al_ref[slot, :, cols] = jnp.exp2(m_old - m_new)
            m_ref[:, cols] = m_new

    def accumulate(vt_of, slot, nk):
        for h in range(H_B):
            cols = slice(2 * h * TQ, (2 * h + 2) * TQ)
            acc_ref[h] = acc_ref[h] * al_ref[slot, :, cols] + _dot(vt_of(h), p_ref[slot, 0:nk, cols])

    eye = (lax.broadcasted_iota(I32, (DV_B, DV_B), 0)
           == lax.broadcasted_iota(I32, (DV_B, DV_B), 1)).astype(BF16)

    def meta_vt(h):
        v_t = _dot_nt(eye, mvb_ref[:, h * DV_B:(h + 1) * DV_B]).astype(BF16)
        return jnp.concatenate([v_t, jnp.ones((VT_ROWS - DV_B, N_META), BF16)], axis=0)

    def main_vt(t):
        return lambda h: vt_ref[t, h]

    def start(first_is_last):
        logits(lambda hc: mkb_ref[hc], 1, N_META)
        logits(lambda hc: kb_ref[hc, main_rows(0), :], 0, TQ, last=first_is_last)
        softmax(1, N_META)
        accumulate(meta_vt, 1, N_META)
        p_ref[1] = jnp.zeros(p_ref.shape[1:], BF16)
        al_ref[1] = jnp.ones(al_ref.shape[1:], F32)

    def full_step(t, slot, next_is_last=False):
        logits(lambda hc: kb_ref[hc, main_rows(t + 1), :], 1 - slot, TQ, last=next_is_last)
        accumulate(main_vt(jnp.maximum(t - 1, 0)), 1 - slot, TQ)
        softmax(slot, TQ)

    def two_steps(u, carry):
        full_step(2 * u, 0)
        full_step(2 * u + 1, 1)
        return carry

    def last_step(slot):
        accumulate(main_vt(jnp.maximum(i - 1, 0)), 1 - slot, TQ)
        softmax(slot, TQ)
        accumulate(main_vt(i), slot, TQ)

    @pl.when(i == 0)
    def _first_tile():
        start(True)
        last_step(0)

    @pl.when(i % 2 == 1)
    def _odd():
        start(False)
        lax.fori_loop(0, i // 2, two_steps, 0)
        full_step(i - 1, 0, next_is_last=True)
        last_step(1)

    @pl.when((i > 0) & (i % 2 == 0))
    def _even():
        start(False)
        lax.fori_loop(0, i // 2 - 1, two_steps, 0)
        full_step(i - 2, 0)
        full_step(i - 1, 1, next_is_last=True)
        last_step(0)

    lam = _lambda(lq1_ref, lk1_ref, lq2_ref, lk2_ref)
    for h in range(H_B):
        a = acc_ref[h]
        o0 = a[0:DV_B, 0:TQ] / a[DV_B:DV_B + 1, 0:TQ]
        o1 = a[0:DV_B, TQ:2 * TQ] / a[DV_B:DV_B + 1, TQ:2 * TQ]
        o = _rms((o0 - lam * o1).T, sg_ref[...]) * (1.0 - LAM_INIT)
        o_ref[:, h * DV_B:(h + 1) * DV_B] = o.astype(BF16)


def _diff(qb, kbh, vbt, mkb, mvb, lq1, lk1, lq2, lk2, sg, batch, seq):
    nq = seq // TQ
    qrow = lambda b, i: (b * nq + i, 0)
    qhead = lambda b, i: (0, b * nq + i, 0)
    const2 = lambda b, i: (0, 0)
    const3 = lambda b, i: (0, 0, 0)
    return pl.pallas_call(
        _diff_body,
        out_shape=jax.ShapeDtypeStruct((batch * seq, H_B * DV_B), BF16),
        grid=(batch, nq),
        in_specs=[
            pl.BlockSpec((2 * H_B, TQ, 128), qhead),
            pl.BlockSpec((2 * H_B, seq, 128), lambda b, i: (0, b, 0)),
            pl.BlockSpec((nq, H_B, VT_ROWS, TQ), lambda b, i: (b, 0, 0, 0)),
            pl.BlockSpec((2 * H_B, N_META, 128), const3),
            pl.BlockSpec((N_META, H_B * DV_B), const2),
            pl.BlockSpec((1, DH_B), const2),
            pl.BlockSpec((1, DH_B), const2),
            pl.BlockSpec((1, DH_B), const2),
            pl.BlockSpec((1, DH_B), const2),
            pl.BlockSpec((1, DV_B), const2),
        ],
        out_specs=pl.BlockSpec((TQ, H_B * DV_B), qrow),
        scratch_shapes=[
            pltpu.VMEM((1, 2 * H_B * TQ), F32),
            pltpu.VMEM((2, 1, 2 * H_B * TQ), F32),
            pltpu.VMEM((H_B, VT_ROWS, 2 * TQ), F32),
            pltpu.VMEM((2, TQ, 2 * H_B * TQ), BF16),
            pltpu.VMEM((2, 2 * H_B, TQ, TQ), F32),
            pltpu.VMEM((2, 1, 2 * H_B * TQ), F32),
        ],
        compiler_params=pltpu.CompilerParams(
            dimension_semantics=("arbitrary", "arbitrary"), vmem_limit_bytes=VMEM_LIMIT),
        name="diff",
    )(qb, kbh, vbt, mkb, mvb, lq1, lk1, lq2, lk2, sg)


def _sample_body(qa_ref, qi_ref, sm_ref, nka_ref, nva_ref, nki_ref, qb_ref, nkb_ref, nvb_ref,
                 cak_ref, cav_ref, cai_ref, cbk_ref, cbv_ref,
                 lq1_ref, lk1_ref, lq2_ref, lk2_ref, sg_ref,
                 oa_ref, ob_ref, pc_ref, pn_ref, *, k_top, past, ts, group):
    pad = LANES - ts
    row_c = lax.broadcasted_iota(I32, (ts, past), 0)
    col_c1 = lax.broadcasted_iota(I32, (ts, past), 1)
    row_n = lax.broadcasted_iota(I32, (ts, LANES), 0)
    col_n1 = lax.broadcasted_iota(I32, (ts, LANES), 1)
    new_ok = col_n1 < ts
    col_c = lax.broadcasted_iota(I32, (group * ts, past), 1)
    col_n = lax.broadcasted_iota(I32, (group * ts, LANES), 1)

    def pad_rows(x):
        return jnp.concatenate([x, jnp.zeros((pad,) + x.shape[1:], x.dtype)], axis=0)

    def stream_rows(g):
        return slice(g * ts, (g + 1) * ts)

    def indexer_scores(g):
        r = stream_rows(g)
        qi_all = qi_ref[:, r, :].reshape(H_I * ts, D_IDX)
        w = sm_ref[r, SM_WI:SM_WI + H_I]
        s4c = _dot(qi_all, cai_ref[g].astype(BF16))
        s4n = _dot_nt(qi_all, pad_rows(nki_ref[r, :]))
        sc_c = None
        sc_n = None
        for h in range(H_I):
            wh = w[:, h:h + 1]
            tc = jnp.maximum(s4c[h * ts:(h + 1) * ts], 0.0) * wh
            tn = jnp.maximum(s4n[h * ts:(h + 1) * ts], 0.0) * wh
            sc_c = tc if sc_c is None else sc_c + tc
            sc_n = tn if sc_n is None else sc_n + tn
        return sc_c, jnp.where(new_ok, sc_n, -jnp.inf)

    per_stream = [indexer_scores(g) for g in range(group)]
    sc_c = jnp.concatenate([p[0] for p in per_stream], axis=0)
    sc_n = jnp.concatenate([p[1] for p in per_stream], axis=0)

    def count(pred_c, pred_n):
        c = jnp.sum(jnp.where(pred_c, 1.0, 0.0), axis=1, keepdims=True)
        return c + jnp.sum(jnp.where(pred_n, 1.0, 0.0), axis=1, keepdims=True)

    def threshold(key):
        return _f32_from_key(jnp.clip(key, KEY32_LOWEST, KEY32_HIGHEST))

    def value_step(s, key):
        cand_key = key ^ jnp.left_shift(jnp.int32(1), 31 - s)
        cand = threshold(cand_key)
        cnt = count(sc_c >= cand, sc_n >= cand)
        return jnp.where(cnt >= float(k_top), cand_key, key)

    t = threshold(lax.fori_loop(0, 32, value_step, jnp.full((group * ts, 1), INT_MIN, I32)))
    need = float(k_top) - count(sc_c > t, sc_n > t)
    idx_n = col_n + past

    def tie_step(s, x):
        cand = x | jnp.left_shift(jnp.int32(1), 11 - s)
        cnt = count((sc_c == t) & (col_c < cand), (sc_n == t) & (idx_n < cand))
        return jnp.where(cnt < need, cand, x)

    x = lax.fori_loop(0, 12, tie_step, jnp.zeros((group * ts, 1), I32))
    negm_c_all = jnp.where((sc_c > t) | ((sc_c == t) & (col_c <= x)), 0.0, -jnp.inf)
    negm_n_all = jnp.where((sc_n > t) | ((sc_n == t) & (idx_n <= x)), 0.0, -jnp.inf)

    dist_c = (row_c - col_c1 + past).astype(F32)
    dist_n = jnp.abs(row_n - col_n1).astype(F32)
    lam = _lambda(lq1_ref, lk1_ref, lq2_ref, lk2_ref)
    neg_new = jnp.where(new_ok, 0.0, -jnp.inf)
    for g in range(group):
        _sample_attend(g, stream_rows(g), negm_c_all, negm_n_all, dist_c, dist_n, lam, neg_new, pad_rows,
                       qa_ref, nka_ref, nva_ref, qb_ref, nkb_ref, nvb_ref,
                       cak_ref, cav_ref, cbk_ref, cbv_ref, sg_ref, oa_ref, ob_ref, pc_ref, pn_ref,
                       past=past, ts=ts)


def _sample_attend(g, r, negm_c_all, negm_n_all, dist_c, dist_n, lam, neg_new, pad_rows,
                   qa_ref, nka_ref, nva_ref, qb_ref, nkb_ref, nvb_ref,
                   cak_ref, cav_ref, cbk_ref, cbv_ref, sg_ref, oa_ref, ob_ref, pc_ref, pn_ref,
                   *, past, ts):
    negm_c = negm_c_all[r]
    negm_n = negm_n_all[r]
    q_all = qa_ref[:, r, :][:, :, 0:DH_A].reshape(H_A * ts, DH_A)
    s_c = _dot(q_all, cak_ref[g].astype(BF16))
    s_n = _dot_nt(q_all, pad_rows(nka_ref[r, :][:, 0:DH_A]))
    for h in range(H_A):
        rows = slice(h * ts, (h + 1) * ts)
        lc = s_c[rows] + (negm_c - (LOG2E * SLOPES_A[h]) * dist_c)
        ln = s_n[rows] + (negm_n - (LOG2E * SLOPES_A[h]) * dist_n)
        m = jnp.maximum(jnp.max(lc, axis=1, keepdims=True), jnp.max(ln, axis=1, keepdims=True))
        m = jnp.maximum(m, NEG_INIT)
        pc_ref[g, rows, :] = jnp.exp2(lc - m).astype(BF16)
        pn_ref[g, rows, :] = jnp.exp2(ln - m).astype(BF16)
    vc_aug_t = jnp.concatenate([cav_ref[g].astype(BF16), jnp.ones((64, past), BF16)], axis=0)
    pv = _dot_nt(pc_ref[g], vc_aug_t) + _dot(pn_ref[g], pad_rows(nva_ref[r, :]))
    pv = pv / pltpu.roll(pv, DH_A, axis=1)
    oa_ref[r, :] = jnp.concatenate(
        [pv[h * ts:(h + 1) * ts, :DH_A] for h in range(H_A)], axis=1).astype(BF16)

    for h in range(H_B):
        a_c = None
        a_n = None
        for c in range(2):
            hc = 2 * h + c
            q = qb_ref[hc, r, :][:, 0:DH_B]
            kc_t = cbk_ref[g, hc * DH_B:(hc + 1) * DH_B, :].astype(BF16)
            lc = _dot(q, kc_t) - (LOG2E * SLOPES_B[h]) * dist_c
            ln = _dot_nt(q, pad_rows(nkb_ref[hc, r, :][:, 0:DH_B])) + (neg_new - (LOG2E * SLOPES_B[h]) * dist_n)
            m = jnp.maximum(jnp.max(lc, axis=1, keepdims=True), jnp.max(ln, axis=1, keepdims=True))
            pc = jnp.exp2(lc - m)
            pn = jnp.exp2(ln - m)
            l = jnp.sum(pc, axis=1, keepdims=True) + jnp.sum(pn, axis=1, keepdims=True)
            pc = pc / l
            pn = pn / l
            if c == 0:
                a_c, a_n = pc, pn
            else:
                a_c, a_n = a_c - lam * pc, a_n - lam * pn
        vc = cbv_ref[g, pl.ds(h, past, stride=H_B), :].astype(BF16)
        vn = pad_rows(nvb_ref[r, h * DV_B:(h + 1) * DV_B])
        o = _dot(a_c.astype(BF16), vc) + _dot(a_n.astype(BF16), vn)
        o = _rms(o, sg_ref[...]) * (1.0 - LAM_INIT)
        ob_ref[r, h * DV_B:(h + 1) * DV_B] = o.astype(BF16)


def _sample(qa, qi, sm, nka, nva, nki, qb, nkb, nvb, cak, cav, cai, cbk, cbv,
            lq1, lk1, lq2, lk2, sg, batch, ts, past, k_top):
    group = SAMPLE_GROUP
    assert batch % group == 0
    rows = group * ts
    row = lambda b: (b, 0)
    head = lambda b: (0, b, 0)
    cache = lambda b: (b, 0, 0)
    const = lambda b: (0, 0)
    return pl.pallas_call(
        functools.partial(_sample_body, k_top=k_top, past=past, ts=ts, group=group),
        out_shape=(jax.ShapeDtypeStruct((batch * ts, H_A * DH_A), BF16),
                   jax.ShapeDtypeStruct((batch * ts, H_B * DV_B), BF16)),
        grid=(batch // group,),
        in_specs=[
            pl.BlockSpec((H_A, rows, 128), head),
            pl.BlockSpec((H_I, rows, D_IDX), head),
            pl.BlockSpec((rows, 256), row),
            pl.BlockSpec((rows, 128), row),
            pl.BlockSpec((rows, 128), row),
            pl.BlockSpec((rows, D_IDX), row),
            pl.BlockSpec((2 * H_B, rows, 128), head),
            pl.BlockSpec((2 * H_B, rows, 128), head),
            pl.BlockSpec((rows, H_B * DV_B), row),
            pl.BlockSpec((group, DH_A, past), cache),
            pl.BlockSpec((group, DH_A, past), cache),
            pl.BlockSpec((group, D_IDX, past), cache),
            pl.BlockSpec((group, 2 * H_B * DH_B, past), cache),
            pl.BlockSpec((group, past * H_B, DV_B), cache),
            pl.BlockSpec((1, DH_B), const),
            pl.BlockSpec((1, DH_B), const),
            pl.BlockSpec((1, DH_B), const),
            pl.BlockSpec((1, DH_B), const),
            pl.BlockSpec((1, DV_B), const),
        ],
        out_specs=(pl.BlockSpec((rows, H_A * DH_A), row), pl.BlockSpec((rows, H_B * DV_B), row)),
        scratch_shapes=[
            pltpu.VMEM((group, H_A * ts, past), BF16),
            pltpu.VMEM((group, H_A * ts, LANES), BF16),
        ],
        compiler_params=pltpu.CompilerParams(
            dimension_semantics=("arbitrary",), vmem_limit_bytes=VMEM_LIMIT),
        name="sample",
    )(qa, qi, sm, nka, nva, nki, qb, nkb, nvb, cak, cav, cai, cbk, cbv, lq1, lk1, lq2, lk2, sg)


def _finish_body(x_ref, oa_ref, ob_ref, wo_ref, gm_ref, wu_ref, wd_ref, gf_ref, y_ref):
    o = jnp.concatenate([oa_ref[...], ob_ref[...]], axis=1)
    h1 = x_ref[...] + _dot(o, wo_ref[...])
    hn = _rms(h1, gm_ref[...]).astype(BF16)
    acc = h1
    for c in range(D_FF // D_MODEL):
        u = jnp.maximum(_dot(hn, wu_ref[:, c * D_MODEL:(c + 1) * D_MODEL]), 0.0)
        acc = acc + _dot((u * u).astype(BF16), wd_ref[c * D_MODEL:(c + 1) * D_MODEL, :])
    y_ref[...] = _rms(acc, gf_ref[...])


def _finish(x, oa, ob, wo, gm, wu, wd, gf, tm):
    m = x.shape[0]
    assert m % tm == 0
    row = lambda i: (i, 0)
    const = lambda i: (0, 0)
    resident = dict(pipeline_mode=pl.Buffered(1))
    return pl.pallas_call(
        _finish_body,
        out_shape=jax.ShapeDtypeStruct((m, D_MODEL), F32),
        grid=(m // tm,),
        in_specs=[
            pl.BlockSpec((tm, D_MODEL), row),
            pl.BlockSpec((tm, 512), row),
            pl.BlockSpec((tm, 512), row),
            pl.BlockSpec((D_MODEL, D_MODEL), const, **resident),
            pl.BlockSpec((1, D_MODEL), const),
            pl.BlockSpec((D_MODEL, D_FF), const, **resident),
            pl.BlockSpec((D_FF, D_MODEL), const, **resident),
            pl.BlockSpec((1, D_MODEL), const),
        ],
        out_specs=pl.BlockSpec((tm, D_MODEL), row),
        compiler_params=pltpu.CompilerParams(
            dimension_semantics=("arbitrary",), vmem_limit_bytes=VMEM_LIMIT),
        name="finish",
    )(x, oa, ob, wo, gm, wu, wd, gf)


def kernel(x_prompt, x_sample, cache_a_k, cache_a_v, cache_a_idx_k, cache_b_k, cache_b_v,
           meta_tokens, attn_norm_g, w_in, idx_k_norm_g, idx_k_norm_b,
           lambda_q1, lambda_k1, lambda_q2, lambda_k2, subln_g, w_o,
           mlp_norm_g, w_up, w_down, final_norm_g):
    batch, seq, _ = x_prompt.shape
    dec_batch, ts, _ = x_sample.shape
    past = cache_a_k.shape[2]
    assert attn_norm_g.shape[0] == 1, "single-layer step"
    assert seq % TQ == 0 and ts == 16 and past % LANES == 0 and past // CHUNK == (past + ts - 1) // CHUNK
    n = N_META + seq
    k_top_p = min(TOPK_MAX, seq // 4)
    k_top_s = min(TOPK_MAX, (past + ts) // 4)

    w_t = w_in[0].T.astype(BF16)
    w = jnp.concatenate(
        [w_t[0:512],
         w_t[964:1476],
         w_t[1476:1988],
         w_t[1988:2500],
         w_t[640:896],
         w_t[512:640],
         w_t[896:964]],
        axis=0)
    w = jnp.pad(w, ((0, W_COLS - 2500), (0, 0))).T
    wvt = w_t[1476:2500]
    wo = w_o[0].astype(BF16)
    wu = w_up[0].astype(BF16)
    wd = w_down[0].astype(BF16)
    g_attn = attn_norm_g[0][None]
    g_mlp = mlp_norm_g[0][None]
    g_fin = final_norm_g[None]
    kng = idx_k_norm_g[0][None]
    knb = idx_k_norm_b[0][None]
    lq1, lk1, lq2, lk2 = lambda_q1[0][None], lambda_k1[0][None], lambda_q2[0][None], lambda_k2[0][None]
    sg = subln_g[0][None]

    xp = x_prompt.reshape(batch * seq, D_MODEL)
    xs = x_sample.reshape(dec_batch * ts, D_MODEL)

    (qa_p, qi_p, qb_p, kbh_p, vb_p, kab_p, kib_p, vt_p, wt_p, vbt_p, smt_p, kbt_p) = _proj(
        xp, g_attn, w, wvt, kng, knb, 512, N_META, seq, True)
    (_, _, _, kbh_m, vb_m, kab_m, kib_m, kb_m, vbb_m, sm_m, vaa_m) = _proj(
        meta_tokens, g_attn, w, wvt, kng, knb, N_META, 0, N_META, False)
    (qa_s, qi_s, qb_s, kbh_s, vb_s, kab_s, kib_s, kb_s, vbb_s, sm_s, vaa_s) = _proj(
        xs, g_attn, w, wvt, kng, knb, dec_batch * ts, past, dec_batch * ts, False)

    oa_p = _dsa(qa_p, qi_p, wt_p, kab_p, kib_p, vt_p, kab_m, kib_m, vaa_m, batch, seq, k_top_p)
    ob_p = _diff(qb_p, kbh_p, vbt_p, kbh_m, vbb_m, lq1, lk1, lq2, lk2, sg, batch, seq)
    y_prompt = _finish(xp, oa_p, ob_p, wo, g_mlp, wu, wd, g_fin, 512).reshape(batch, seq, D_MODEL)

    oa_s, ob_s = _sample(
        qa_s, qi_s, sm_s, kab_s, vaa_s, kib_s, qb_s, kbh_s, vbb_s,
        jnp.swapaxes(cache_a_k[0], 1, 2), jnp.swapaxes(cache_a_v[0], 1, 2),
        jnp.swapaxes(cache_a_idx_k[0], 1, 2),
        jnp.transpose(cache_b_k[0], (0, 2, 3, 4, 1)).reshape(dec_batch, 2 * H_B * DH_B, past),
        cache_b_v[0].reshape(dec_batch, past * H_B, DV_B),
        lq1, lk1, lq2, lk2, sg, dec_batch, ts, past, k_top_s)
    y_sample = _finish(xs, oa_s, ob_s, wo, g_mlp, wu, wd, g_fin, dec_batch * ts).reshape(
        dec_batch, ts, D_MODEL)

    def with_meta_t(meta_rows, main_t):
        meta_b = jnp.broadcast_to(meta_rows.T[None], (batch, 64, N_META))
        return jnp.swapaxes(jnp.concatenate([meta_b, main_t], axis=2), 1, 2)[None]

    new_a_k_p = with_meta_t(sm_m[:, SM_KA:SM_KA + 64], smt_p[:, SM_KA:SM_KA + 64])
    new_a_v_p = with_meta_t(sm_m[:, SM_VA:SM_VA + 64], smt_p[:, SM_VA:SM_VA + 64])
    new_a_i_p = with_meta_t(sm_m[:, SM_KI:SM_KI + 64], smt_p[:, SM_KI:SM_KI + 64])
    kb_meta_t = jnp.broadcast_to(kb_m.T.reshape(1, H_B, 2, DH_B, N_META), (batch, H_B, 2, DH_B, N_META))
    kb_main_t = kbt_p.reshape(batch, H_B, 2, DH_B, seq)
    new_b_k_p = jnp.transpose(
        jnp.concatenate([kb_meta_t, kb_main_t], axis=4), (0, 4, 1, 2, 3))[None]
    new_b_v_p = jnp.concatenate(
        [jnp.broadcast_to(vb_m.reshape(1, N_META, H_B, DV_B), (batch, N_META, H_B, DV_B)),
         vb_p.reshape(batch, seq, H_B, DV_B)], axis=1)[None]
    new_a_k_s = sm_s[:, SM_KA:SM_KA + 64].reshape(1, dec_batch, ts, 64)
    new_a_v_s = sm_s[:, SM_VA:SM_VA + 64].reshape(1, dec_batch, ts, 64)
    new_a_i_s = sm_s[:, SM_KI:SM_KI + 64].reshape(1, dec_batch, ts, 64)
    new_b_k_s = kb_s.reshape(1, dec_batch, ts, H_B, 2, DH_B)
    new_b_v_s = vb_s.reshape(1, dec_batch, ts, H_B, DV_B)
    return (y_prompt, y_sample, new_a_k_p, new_a_v_p, new_a_i_p, new_b_k_p, new_b_v_p,
            new_a_k_s, new_a_v_s, new_a_i_s, new_b_k_s, new_b_v_s)
```

```python
import functools

import jax
import jax.numpy as jnp
import numpy as np
from jax import lax
from jax.experimental import pallas as pl
from jax.experimental.pallas import tpu as pltpu

F32 = jnp.float32
BF16 = jnp.bfloat16
I32 = jnp.int32
HALF16 = 32768

D_MODEL = 1024
CHUNK = 64
N_META = 16
H_A = 8
DH_A = 64
H_I = 4
D_IDX = 64
TOPK_MAX = 256
H_B = 4
DH_B = 64
DV_B = 2 * DH_B
D_FF = 4 * D_MODEL
EPS = 1e-6
LAM_INIT = 0.2

LANES = 128
C_QA, C_QB, C_KB, C_VB, C_QI, C_SM = 0, 512, 1024, 1536, 2048, 2304
W_COLS = 2560
SM_KA, SM_VA, SM_KI, SM_WI = 0, 64, 128, 192

LOG2E = float(np.float32(1.4426950408889634))
Q_SCALE = (DH_A ** -0.5) * LOG2E
N_SLOPE_PIECES = 3


def _bf16_pieces(value):
    rest = np.float32(value)
    pieces = []
    for _ in range(N_SLOPE_PIECES):
        piece = np.float32(rest.astype(jnp.bfloat16))
        pieces.append(float(piece))
        rest = np.float32(rest - piece)
    assert rest == 0.0
    return pieces
W_SCALE = (H_I ** -0.5) * (D_IDX ** -0.5)
SLOPES_A = tuple(2.0 ** (-8.0 * (i + 1) / H_A) for i in range(H_A))
SLOPES_B = tuple(2.0 ** (-8.0 * (i + 1) / H_B) for i in range(H_B))

INT_MIN = -(2 ** 31)
NEG_INIT = -1e30
TQ = 256
ONES_ROWS = 16
VT_ROWS = DV_B + ONES_ROWS
VTA_ROWS = DH_A + ONES_ROWS
CNT_ROWS = 32
SAMPLE_GROUP = 4
VMEM_LIMIT = 56 * 1024 * 1024

_NT = (((1,), (1,)), ((), ()))


def _dot(a, b):
    return jnp.dot(a, b, preferred_element_type=F32)


def _dot_nt(a, b):
    return lax.dot_general(a, b, _NT, preferred_element_type=F32)


def _rms(x, g):
    return (x * lax.rsqrt(jnp.mean(x * x, axis=-1, keepdims=True) + EPS)) * g


def _to_key(score):
    b = lax.bitcast_convert_type(score, I32)
    return b ^ ((b >> 31) & jnp.int32(0x7FFFFFFF))


def _f32_from_key(key):
    return lax.bitcast_convert_type(key ^ ((key >> 31) & jnp.int32(0x7FFFFFFF)), F32)


def _bf16_from_key(key):
    bits = key ^ ((key >> 15) & jnp.int32(0x7FFF))
    return lax.bitcast_convert_type(jnp.left_shift(bits, 16), F32).astype(BF16)


KEY16_LOWEST = -32640
KEY32_LOWEST = -2139095040
KEY32_HIGHEST = 2139095039


def _proj_body(x_ref, g_ref, w_ref, wvt_ref, kng_ref, knb_ref,
               qa_ref, qi_ref, qb_ref, kbh_ref, vb_ref, kab_ref, kib_ref, *mode_refs,
               pos0, period, key_major):
    x = x_ref[...]
    hn = _rms(x, g_ref[...]).astype(BF16)
    rows = x.shape[0]

    def mm(lo, width):
        return _dot(hn, w_ref[:, lo:lo + width])

    lane = lax.broadcasted_iota(I32, (rows, 64), 1)
    r = pl.program_id(0) * rows + lax.broadcasted_iota(I32, (rows, 64), 0)
    pos = pos0 + r % period
    pos_cols = jnp.where(lane >= 2 * N_SLOPE_PIECES, 0,
                         jnp.where(lane % 2 == 0, (pos // 256) * 256, pos % 256))
    pos_cols = pos_cols.astype(F32).astype(BF16)

    def slope_cols(slope):
        cols = jnp.zeros((rows, 64), F32)
        for n, piece in enumerate(_bf16_pieces(slope * LOG2E)):
            cols = jnp.where(lane // 2 == n, piece, cols)
        return cols.astype(BF16)

    z = mm(C_QA, 512) * Q_SCALE
    for h in range(H_A):
        qa_ref[h] = jnp.concatenate(
            [z[:, h * DH_A:(h + 1) * DH_A].astype(BF16), slope_cols(SLOPES_A[h])], axis=1)
    z = mm(C_QB, 512) * Q_SCALE
    for hc in range(2 * H_B):
        qb_ref[hc] = jnp.concatenate(
            [z[:, hc * DH_B:(hc + 1) * DH_B].astype(BF16), slope_cols(SLOPES_B[hc // 2])], axis=1)
    if key_major:
        vt_ref, wt_ref, vbt_ref, smt_ref, kbt_ref = mode_refs
    else:
        kb_ref, vbb_ref, sm_ref, vaa_ref = mode_refs
    z = mm(C_KB, 512)
    if not key_major:
        kb_ref[...] = z
    for hc in range(2 * H_B):
        kbh_ref[hc] = jnp.concatenate([z[:, hc * DH_B:(hc + 1) * DH_B].astype(BF16), pos_cols], axis=1)
    z = mm(C_VB, 512)
    for h in range(H_B):
        vb_ref[pl.ds(h, rows, stride=H_B), :] = z[:, h * DV_B:(h + 1) * DV_B]
    if not key_major:
        vbb_ref[...] = z.astype(BF16)
    z = mm(C_QI, 256)
    for h in range(H_I):
        qi_ref[h] = z[:, h * D_IDX:(h + 1) * D_IDX].astype(BF16)
    z = mm(C_SM, 256)
    ka = z[:, SM_KA:SM_KA + 64]
    va = z[:, SM_VA:SM_VA + 64]
    ki = z[:, SM_KI:SM_KI + 64]
    xc = ki - jnp.mean(ki, axis=-1, keepdims=True)
    ki = xc * lax.rsqrt(jnp.mean(xc * xc, axis=-1, keepdims=True) + EPS) * kng_ref[...] + knb_ref[...]
    kab_ref[...] = jnp.concatenate([ka.astype(BF16), pos_cols], axis=1)
    kib_ref[...] = ki.astype(BF16)
    if not key_major:
        sm_ref[:, 0:128] = z[:, 0:128]
        sm_ref[:, SM_KI:SM_KI + 64] = ki
        sm_ref[:, SM_WI:SM_WI + 64] = z[:, SM_WI:SM_WI + 64] * W_SCALE
        vaa_ref[...] = jnp.concatenate([va.astype(BF16), jnp.ones((rows, 64), BF16)], axis=1)
    else:
        kv_t = z[:, 0:128].T
        for c in range(rows // TQ):
            vt_ref[c] = jnp.concatenate(
                [kv_t[64:128, c * TQ:(c + 1) * TQ].astype(BF16), jnp.ones((ONES_ROWS, TQ), BF16)], axis=0)
        kw_t = jnp.concatenate([ki, z[:, SM_WI:SM_WI + 64] * W_SCALE], axis=1).T
        wt_ref[...] = kw_t[64:72, :]
        smt_ref[0:128, :] = kv_t
        smt_ref[128:192, :] = kw_t[0:64, :]
        kbt_ref[...] = _dot_nt(wvt_ref[0:512, :], hn)
        vb_t = _dot_nt(wvt_ref[512:1024, :], hn)
        for c in range(rows // TQ):
            for h in range(H_B):
                vbt_ref[c, h] = jnp.concatenate(
                    [vb_t[h * DV_B:(h + 1) * DV_B, c * TQ:(c + 1) * TQ].astype(BF16),
                     jnp.ones((VT_ROWS - DV_B, TQ), BF16)], axis=0)


def _proj(x, g, w, wvt, kng, knb, tm, pos0, period, key_major):
    m = x.shape[0]
    assert m % tm == 0
    row = lambda i: (i, 0)
    head = lambda i: (0, i, 0)
    const = lambda i: (0, 0)
    out_shape = (
        jax.ShapeDtypeStruct((H_A, m, 128), BF16),
        jax.ShapeDtypeStruct((H_I, m, D_IDX), BF16),
        jax.ShapeDtypeStruct((2 * H_B, m, 128), BF16),
        jax.ShapeDtypeStruct((2 * H_B, m, 128), BF16),
        jax.ShapeDtypeStruct((m * H_B, DV_B), F32),
        jax.ShapeDtypeStruct((m, 128), BF16),
        jax.ShapeDtypeStruct((m, 64), BF16),
    )
    out_specs = (
        pl.BlockSpec((H_A, tm, 128), head),
        pl.BlockSpec((H_I, tm, D_IDX), head),
        pl.BlockSpec((2 * H_B, tm, 128), head),
        pl.BlockSpec((2 * H_B, tm, 128), head),
        pl.BlockSpec((tm * H_B, DV_B), row),
        pl.BlockSpec((tm, 128), row),
        pl.BlockSpec((tm, 64), row),
    )
    if not key_major:
        out_shape += (
            jax.ShapeDtypeStruct((m, 512), F32),
            jax.ShapeDtypeStruct((m, 512), BF16),
            jax.ShapeDtypeStruct((m, 256), F32),
            jax.ShapeDtypeStruct((m, 128), BF16),
        )
        out_specs += (
            pl.BlockSpec((tm, 512), row),
            pl.BlockSpec((tm, 512), row),
            pl.BlockSpec((tm, 256), row),
            pl.BlockSpec((tm, 128), row),
        )
    else:
        assert tm % TQ == 0 and period % tm == 0 and m % period == 0
        per_stream = period // tm
        stream_cols = lambda i: (i // per_stream, 0, i % per_stream)
        out_shape += (
            jax.ShapeDtypeStruct((m // TQ, VTA_ROWS, TQ), BF16),
            jax.ShapeDtypeStruct((8, m), F32),
            jax.ShapeDtypeStruct((m // TQ, H_B, VT_ROWS, TQ), BF16),
            jax.ShapeDtypeStruct((m // period, 192, period), F32),
            jax.ShapeDtypeStruct((m // period, 512, period), F32),
        )
        out_specs += (
            pl.BlockSpec((tm // TQ, VTA_ROWS, TQ), lambda i: (i, 0, 0)),
            pl.BlockSpec((8, tm), lambda i: (0, i)),
            pl.BlockSpec((tm // TQ, H_B, VT_ROWS, TQ), lambda i: (i, 0, 0, 0)),
            pl.BlockSpec((None, 192, tm), stream_cols),
            pl.BlockSpec((None, 512, tm), stream_cols),
        )
    return pl.pallas_call(
        functools.partial(_proj_body, pos0=pos0, period=period, key_major=key_major),
        out_shape=out_shape,
        grid=(m // tm,),
        in_specs=[
            pl.BlockSpec((tm, D_MODEL), row),
            pl.BlockSpec((1, D_MODEL), const),
            pl.BlockSpec((D_MODEL, W_COLS), const),
            pl.BlockSpec((2 * H_B * DV_B, D_MODEL), const),
            pl.BlockSpec((1, D_IDX), const),
            pl.BlockSpec((1, D_IDX), const),
        ],
        out_specs=out_specs,
        compiler_params=pltpu.CompilerParams(
            dimension_semantics=("arbitrary",), vmem_limit_bytes=VMEM_LIMIT),
        name="proj",
    )(x, g, w, wvt, kng, knb)


def _dsa_body(qa_ref, qi_ref, wt_ref, ka_ref, ki_ref, vt_ref, mka_ref, mki_ref, mva_ref,
              o_ref, smeta_ref, smain_ref, bmeta_ref, bmain_ref,
              tsel_ref, m_ref, al_ref, acc_ref, p_ref, s_ref, mb_ref, seen_ref, *, k_top):
    i = pl.program_id(1)
    krow = lax.broadcasted_iota(I32, (TQ, TQ), 0)
    qcol = lax.broadcasted_iota(I32, (TQ, TQ), 1)

    def main_rows(j):
        return pl.ds(pl.multiple_of(j * TQ, TQ), TQ)

    qi_all = qi_ref[...].reshape(H_I * TQ, D_IDX)

    def scores(ki_blk):
        s4 = _dot_nt(ki_blk, qi_all)
        sc = None
        for h in range(H_I):
            t = jnp.maximum(s4[:, h * TQ:(h + 1) * TQ], 0.0) * wt_ref[h:h + 1, :]
            sc = t if sc is None else sc + t
        return sc

    sc = scores(mki_ref[...])
    smeta_ref[...] = sc
    bmeta_ref[...] = sc.astype(BF16)

    def score_block(j, sc):
        smain_ref[j] = sc
        bmain_ref[j] = sc.astype(BF16)

    def full_scores(j):
        return scores(ki_ref[main_rows(j), :])

    def last_scores():
        return jnp.where((krow // CHUNK) <= (qcol // CHUNK), full_scores(i), -jnp.inf)

    def score_two(u, carry):
        sc_a, sc_b = full_scores(2 * u), full_scores(2 * u + 1)
        score_block(2 * u, sc_a)
        score_block(2 * u + 1, sc_b)
        return carry

    lax.fori_loop(0, i // 2, score_two, 0)

    @pl.when(i % 2 == 1)
    def _odd_scores():
        sc_a, sc_b = full_scores(i - 1), last_scores()
        score_block(i - 1, sc_a)
        score_block(i, sc_b)

    @pl.when(i % 2 == 0)
    def _even_scores():
        score_block(i, last_scores())

    def search(odd_blocks):
        def count(meta_ref, main_ref, part, pred):
            def one(j):
                return part(pred(main_ref[j]))

            def two(u, c):
                return c + (one(2 * u) + one(2 * u + 1))

            c = lax.fori_loop(0, (i + 1) // 2, two, jnp.zeros((CNT_ROWS, TQ), F32))
            if odd_blocks:
                c = c + one(i)
            cm = part(pred(meta_ref[...]), rows=N_META)
            return jnp.sum(c, axis=0, keepdims=True) + jnp.sum(cm, axis=0, keepdims=True)

        def part32(hit, rows=CNT_ROWS):
            ones = jnp.where(hit, 1.0, 0.0)
            return jnp.sum(ones.reshape(ones.shape[0] // rows, rows, TQ), axis=0)

        def part16(hit, rows=CNT_ROWS):
            ones = jnp.where(hit, jnp.bfloat16(1), jnp.bfloat16(0))
            acc = ones[0:rows]
            for t in range(1, ones.shape[0] // rows):
                acc = acc + ones[t * rows:(t + 1) * rows]
            return acc.astype(F32)

        kf = float(k_top)

        def step16(s, u):
            cand_u = u | jnp.left_shift(jnp.int32(1), 15 - s)
            cand = _bf16_from_key(jnp.maximum(cand_u, KEY16_LOWEST + HALF16) - HALF16)
            cnt = count(bmeta_ref, bmain_ref, part16, lambda x: x >= cand)
            return jnp.where(cnt >= kf, cand_u, u)

        u_hi = lax.fori_loop(0, 16, step16, jnp.zeros((1, TQ), I32))
        t_hi = _bf16_from_key(jnp.maximum(u_hi, KEY16_LOWEST + HALF16) - HALF16).astype(F32)
        base = jnp.maximum(_to_key(t_hi) - 65536, KEY32_LOWEST)

        def step32(s, d):
            cand_d = d | jnp.left_shift(jnp.int32(1), 16 - s)
            cand = _f32_from_key(jnp.minimum(base + cand_d, KEY32_HIGHEST))
            cnt = count(smeta_ref, smain_ref, part32, lambda x: x >= cand)
            return jnp.where(cnt >= kf, cand_d, d)

        d_lo = lax.fori_loop(0, 17, step32, jnp.zeros((1, TQ), I32))
        t_sel = _f32_from_key(jnp.minimum(base + d_lo, KEY32_HIGHEST))
        tsel_ref[0:1, :] = t_sel
        tsel_ref[1:2, :] = kf - count(smeta_ref, smain_ref, part32, lambda x: x > t_sel)

    pl.when(i % 2 == 0)(lambda: search(True))
    pl.when(i % 2 == 1)(lambda: search(False))

    m_ref[...] = jnp.full(m_ref.shape, NEG_INIT, F32)
    acc_ref[...] = jnp.zeros(acc_ref.shape, F32)
    thr = tsel_ref[0:1, :]
    need = tsel_ref[1:2, :]

    def masked_logits(k_blk, sc, slot):
        nk = sc.shape[0]
        tie = sc == thr
        tri = (lax.broadcasted_iota(I32, (nk, nk), 1)
               <= lax.broadcasted_iota(I32, (nk, nk), 0)).astype(BF16)
        rank = _dot(tri, jnp.where(tie, 1.0, 0.0).astype(BF16)) + seen_ref[...]
        seen_ref[...] = rank[nk - 1:nk, :]
        negm = jnp.where(sc > thr, 0.0, jnp.where(tie, jnp.where(rank <= need, 0.0, -jnp.inf), -jnp.inf))
        for h in range(H_A):
            cols = slice(h * TQ, (h + 1) * TQ)
            s = _dot_nt(k_blk, qa_ref[h]) + negm
            s_ref[slot, 0:nk, cols] = s
            mb_ref[slot, :, cols] = jnp.max(s, axis=0, keepdims=True)

    def softmax(slot, nk, future):
        for h in range(H_A):
            cols = slice(h * TQ, (h + 1) * TQ)
            s = s_ref[slot, 0:nk, cols]
            if future is not None:
                s = s - (2.0 * LOG2E * SLOPES_A[h]) * future
            m_old = m_ref[:, cols]
            m_new = jnp.maximum(m_old, mb_ref[slot, :, cols])
            p_ref[slot, 0:nk, cols] = jnp.exp2(s - m_new).astype(BF16)
            al_ref[slot, :, cols] = jnp.exp2(m_old - m_new)
            m_ref[:, cols] = m_new

    def accumulate(vt_blk, slot, nk):
        acc_ref[...] = acc_ref[...] * al_ref[slot] + _dot(vt_blk, p_ref[slot, 0:nk, :])

    eye = (lax.broadcasted_iota(I32, (128, 128), 0)
           == lax.broadcasted_iota(I32, (128, 128), 1)).astype(BF16)
    seen_ref[...] = jnp.zeros(seen_ref.shape, F32)
    masked_logits(mka_ref[...], smeta_ref[...], 1)
    masked_logits(ka_ref[main_rows(0), :], smain_ref[0], 0)
    softmax(1, N_META, None)
    accumulate(_dot_nt(eye, mva_ref[...])[0:VTA_ROWS].astype(BF16), 1, N_META)

    p_ref[1] = jnp.zeros(p_ref.shape[1:], BF16)
    al_ref[1] = jnp.ones(al_ref.shape[1:], F32)

    def full_step(t, slot):
        masked_logits(ka_ref[main_rows(t + 1), :], smain_ref[t + 1], 1 - slot)
        accumulate(vt_ref[jnp.maximum(t - 1, 0)], 1 - slot, TQ)
        softmax(slot, TQ, None)

    def two_steps(u, carry):
        full_step(2 * u, 0)
        full_step(2 * u + 1, 1)
        return carry

    lax.fori_loop(0, i // 2, two_steps, 0)

    def last_step(slot):
        accumulate(vt_ref[jnp.maximum(i - 1, 0)], 1 - slot, TQ)
        softmax(slot, TQ, jnp.maximum(krow - qcol, 0).astype(F32))
        accumulate(vt_ref[i], slot, TQ)

    @pl.when(i % 2 == 1)
    def _odd():
        full_step(i - 1, 0)
        last_step(1)

    @pl.when(i % 2 == 0)
    def _even():
        last_step(0)

    def head_out_t(h):
        a = acc_ref[:, h * TQ:(h + 1) * TQ]
        return a[0:DH_A] / a[DH_A:DH_A + 1]

    for h in range(0, H_A, 2):
        pair_t = jnp.concatenate([head_out_t(h), head_out_t(h + 1)], axis=0)
        o_ref[:, h * DH_A:(h + 2) * DH_A] = pair_t.T.astype(BF16)


def _dsa(qa, qi, wt, kab, kib, vt, mka, mki, mva, batch, seq, k_top):
    nq = seq // TQ
    qrow = lambda b, i: (b * nq + i, 0)
    qhead = lambda b, i: (0, b * nq + i, 0)
    kv = lambda b, i: (b, 0)
    const = lambda b, i: (0, 0)
    return pl.pallas_call(
        functools.partial(_dsa_body, k_top=k_top),
        out_shape=jax.ShapeDtypeStruct((batch * seq, H_A * DH_A), BF16),
        grid=(batch, nq),
        in_specs=[
            pl.BlockSpec((H_A, TQ, 128), qhead),
            pl.BlockSpec((H_I, TQ, D_IDX), qhead),
            pl.BlockSpec((8, TQ), lambda b, i: (0, b * nq + i)),
            pl.BlockSpec((seq, 128), kv),
            pl.BlockSpec((seq, D_IDX), kv),
            pl.BlockSpec((nq, VTA_ROWS, TQ), lambda b, i: (b, 0, 0)),
            pl.BlockSpec((N_META, 128), const),
            pl.BlockSpec((N_META, D_IDX), const),
            pl.BlockSpec((N_META, 128), const),
        ],
        out_specs=pl.BlockSpec((TQ, H_A * DH_A), qrow),
        scratch_shapes=[
            pltpu.VMEM((N_META, TQ), F32),
            pltpu.VMEM((nq, TQ, TQ), F32),
            pltpu.VMEM((N_META, TQ), BF16),
            pltpu.VMEM((nq, TQ, TQ), BF16),
            pltpu.VMEM((8, TQ), F32),
            pltpu.VMEM((1, H_A * TQ), F32),
            pltpu.VMEM((2, 1, H_A * TQ), F32),
            pltpu.VMEM((VTA_ROWS, H_A * TQ), F32),
            pltpu.VMEM((2, TQ, H_A * TQ), BF16),
            pltpu.VMEM((2, TQ, H_A * TQ), F32),
            pltpu.VMEM((2, 1, H_A * TQ), F32),
            pltpu.VMEM((1, TQ), F32),
        ],
        compiler_params=pltpu.CompilerParams(
            dimension_semantics=("arbitrary", "arbitrary"), vmem_limit_bytes=VMEM_LIMIT),
        name="dsa",
    )(qa, qi, wt, kab, kib, vt, mka, mki, mva)


def _lambda(lq1_ref, lk1_ref, lq2_ref, lk2_ref):
    s1 = jnp.sum(lq1_ref[...] * lk1_ref[...], axis=-1, keepdims=True)
    s2 = jnp.sum(lq2_ref[...] * lk2_ref[...], axis=-1, keepdims=True)
    return jnp.exp(s1) - jnp.exp(s2) + LAM_INIT


def _diff_body(qb_ref, kb_ref, vt_ref, mkb_ref, mvb_ref, lq1_ref, lk1_ref, lq2_ref, lk2_ref,
               sg_ref, o_ref, m_ref, al_ref, acc_ref, p_ref, s_ref, mb_ref):
    i = pl.program_id(1)
    n_hc = 2 * H_B
    krow = lax.broadcasted_iota(I32, (TQ, TQ), 0)
    qcol = lax.broadcasted_iota(I32, (TQ, TQ), 1)
    m_ref[...] = jnp.full(m_ref.shape, NEG_INIT, F32)
    acc_ref[...] = jnp.zeros(acc_ref.shape, F32)

    def main_rows(j):
        return pl.ds(pl.multiple_of(j * TQ, TQ), TQ)

    def logits(k_of, slot, nk, last=False):
        if last:
            negm = jnp.where((krow // CHUNK) <= (qcol // CHUNK), 0.0, -jnp.inf)
            future = jnp.maximum(krow - qcol, 0).astype(F32)
        for hc in range(n_hc):
            s = _dot_nt(k_of(hc), qb_ref[hc])
            if last:
                s = (s + negm) - (2.0 * LOG2E * SLOPES_B[hc // 2]) * future
            s_ref[slot, hc, 0:nk, :] = s
            mb_ref[slot, :, hc * TQ:(hc + 1) * TQ] = jnp.max(s, axis=0, keepdims=True)

    def softmax(slot, nk):
        for hc in range(n_hc):
            cols = slice(hc * TQ, (hc + 1) * TQ)
            m_old = m_ref[:, cols]
            m_new = jnp.maximum(m_old, mb_ref[slot, :, cols])
            p_ref[slot, 0:nk, cols] = jnp.exp2(s_ref[slot, hc, 0:nk, :] - m_new).astype(BF16)
            al_ref[slot, :, cols] = jnp.exp2(m_old - m_new)
            m_ref[:, cols] = m_new

    def accumulate(vt_of, slot, nk):
        for h in range(H_B):
            cols = slice(2 * h * TQ, (2 * h + 2) * TQ)
            acc_ref[h] = acc_ref[h] * al_ref[slot, :, cols] + _dot(vt_of(h), p_ref[slot, 0:nk, cols])

    eye = (lax.broadcasted_iota(I32, (DV_B, DV_B), 0)
           == lax.broadcasted_iota(I32, (DV_B, DV_B), 1)).astype(BF16)

    def meta_vt(h):
        v_t = _dot_nt(eye, mvb_ref[:, h * DV_B:(h + 1) * DV_B]).astype(BF16)
        return jnp.concatenate([v_t, jnp.ones((VT_ROWS - DV_B, N_META), BF16)], axis=0)

    def main_vt(t):
        return lambda h: vt_ref[t, h]

    def start(first_is_last):
        logits(lambda hc: mkb_ref[hc], 1, N_META)
        logits(lambda hc: kb_ref[hc, main_rows(0), :], 0, TQ, last=first_is_last)
        softmax(1, N_META)
        accumulate(meta_vt, 1, N_META)
        p_ref[1] = jnp.zeros(p_ref.shape[1:], BF16)
        al_ref[1] = jnp.ones(al_ref.shape[1:], F32)

    def full_step(t, slot, next_is_last=False):
        logits(lambda hc: kb_ref[hc, main_rows(t + 1), :], 1 - slot, TQ, last=next_is_last)
        accumulate(main_vt(jnp.maximum(t - 1, 0)), 1 - slot, TQ)
        softmax(slot, TQ)

    def two_steps(u, carry):
        full_step(2 * u, 0)
        full_step(2 * u + 1, 1)
        return carry

    def last_step(slot):
        accumulate(main_vt(jnp.maximum(i - 1, 0)), 1 - slot, TQ)
        softmax(slot, TQ)
        accumulate(main_vt(i), slot, TQ)

    @pl.when(i == 0)
    def _first_tile():
        start(True)
        last_step(0)

    @pl.when(i % 2 == 1)
    def _odd():
        start(False)
        lax.fori_loop(0, i // 2, two_steps, 0)
        full_step(i - 1, 0, next_is_last=True)
        last_step(1)

    @pl.when((i > 0) & (i % 2 == 0))
    def _even():
        start(False)
        lax.fori_loop(0, i // 2 - 1, two_steps, 0)
        full_step(i - 2, 0)
        full_step(i - 1, 1, next_is_last=True)
        last_step(0)

    lam = _lambda(lq1_ref, lk1_ref, lq2_ref, lk2_ref)
    for h in range(H_B):
        a = acc_ref[h]
        o0 = a[0:DV_B, 0:TQ] / a[DV_B:DV_B + 1, 0:TQ]
        o1 = a[0:DV_B, TQ:2 * TQ] / a[DV_B:DV_B + 1, TQ:2 * TQ]
        o = _rms((o0 - lam * o1).T, sg_ref[...]) * (1.0 - LAM_INIT)
        o_ref[:, h * DV_B:(h + 1) * DV_B] = o.astype(BF16)


def _diff(qb, kbh, vbt, mkb, mvb, lq1, lk1, lq2, lk2, sg, batch, seq):
    nq = seq // TQ
    qrow = lambda b, i: (b * nq + i, 0)
    qhead = lambda b, i: (0, b * nq + i, 0)
    const2 = lambda b, i: (0, 0)
    const3 = lambda b, i: (0, 0, 0)
    return pl.pallas_call(
        _diff_body,
        out_shape=jax.ShapeDtypeStruct((batch * seq, H_B * DV_B), BF16),
        grid=(batch, nq),
        in_specs=[
            pl.BlockSpec((2 * H_B, TQ, 128), qhead),
            pl.BlockSpec((2 * H_B, seq, 128), lambda b, i: (0, b, 0)),
            pl.BlockSpec((nq, H_B, VT_ROWS, TQ), lambda b, i: (b, 0, 0, 0)),
            pl.BlockSpec((2 * H_B, N_META, 128), const3),
            pl.BlockSpec((N_META, H_B * DV_B), const2),
            pl.BlockSpec((1, DH_B), const2),
            pl.BlockSpec((1, DH_B), const2),
            pl.BlockSpec((1, DH_B), const2),
            pl.BlockSpec((1, DH_B), const2),
            pl.BlockSpec((1, DV_B), const2),
        ],
        out_specs=pl.BlockSpec((TQ, H_B * DV_B), qrow),
        scratch_shapes=[
            pltpu.VMEM((1, 2 * H_B * TQ), F32),
            pltpu.VMEM((2, 1, 2 * H_B * TQ), F32),
            pltpu.VMEM((H_B, VT_ROWS, 2 * TQ), F32),
            pltpu.VMEM((2, TQ, 2 * H_B * TQ), BF16),
            pltpu.VMEM((2, 2 * H_B, TQ, TQ), F32),
            pltpu.VMEM((2, 1, 2 * H_B * TQ), F32),
        ],
        compiler_params=pltpu.CompilerParams(
            dimension_semantics=("arbitrary", "arbitrary"), vmem_limit_bytes=VMEM_LIMIT),
        name="diff",
    )(qb, kbh, vbt, mkb, mvb, lq1, lk1, lq2, lk2, sg)


def _sample_body(qa_ref, qi_ref, sm_ref, nka_ref, nva_ref, nki_ref, qb_ref, nkb_ref, nvb_ref,
                 cak_ref, cav_ref, cai_ref, cbk_ref, cbv_ref,
                 lq1_ref, lk1_ref, lq2_ref, lk2_ref, sg_ref,
                 oa_ref, ob_ref, pc_ref, pn_ref, *, k_top, past, ts, group):
    pad = LANES - ts
    row_c = lax.broadcasted_iota(I32, (ts, past), 0)
    col_c1 = lax.broadcasted_iota(I32, (ts, past), 1)
    row_n = lax.broadcasted_iota(I32, (ts, LANES), 0)
    col_n1 = lax.broadcasted_iota(I32, (ts, LANES), 1)
    new_ok = col_n1 < ts
    col_c = lax.broadcasted_iota(I32, (group * ts, past), 1)
    col_n = lax.broadcasted_iota(I32, (group * ts, LANES), 1)

    def pad_rows(x):
        return jnp.concatenate([x, jnp.zeros((pad,) + x.shape[1:], x.dtype)], axis=0)

    def stream_rows(g):
        return slice(g * ts, (g + 1) * ts)

    def indexer_scores(g):
        r = stream_rows(g)
        qi_all = qi_ref[:, r, :].reshape(H_I * ts, D_IDX)
        w = sm_ref[r, SM_WI:SM_WI + H_I]
        s4c = _dot(qi_all, cai_ref[g].astype(BF16))
        s4n = _dot_nt(qi_all, pad_rows(nki_ref[r, :]))
        sc_c = None
        sc_n = None
        for h in range(H_I):
            wh = w[:, h:h + 1]
            tc = jnp.maximum(s4c[h * ts:(h + 1) * ts], 0.0) * wh
            tn = jnp.maximum(s4n[h * ts:(h + 1) * ts], 0.0) * wh
            sc_c = tc if sc_c is None else sc_c + tc
            sc_n = tn if sc_n is None else sc_n + tn
        return sc_c, jnp.where(new_ok, sc_n, -jnp.inf)

    per_stream = [indexer_scores(g) for g in range(group)]
    sc_c = jnp.concatenate([p[0] for p in per_stream], axis=0)
    sc_n = jnp.concatenate([p[1] for p in per_stream], axis=0)

    def count(pred_c, pred_n):
        c = jnp.sum(jnp.where(pred_c, 1.0, 0.0), axis=1, keepdims=True)
        return c + jnp.sum(jnp.where(pred_n, 1.0, 0.0), axis=1, keepdims=True)

    def threshold(key):
        return _f32_from_key(jnp.clip(key, KEY32_LOWEST, KEY32_HIGHEST))

    def value_step(s, key):
        cand_key = key ^ jnp.left_shift(jnp.int32(1), 31 - s)
        cand = threshold(cand_key)
        cnt = count(sc_c >= cand, sc_n >= cand)
        return jnp.where(cnt >= float(k_top), cand_key, key)

    t = threshold(lax.fori_loop(0, 32, value_step, jnp.full((group * ts, 1), INT_MIN, I32)))
    need = float(k_top) - count(sc_c > t, sc_n > t)
    idx_n = col_n + past

    def tie_step(s, x):
        cand = x | jnp.left_shift(jnp.int32(1), 11 - s)
        cnt = count((sc_c == t) & (col_c < cand), (sc_n == t) & (idx_n < cand))
        return jnp.where(cnt < need, cand, x)

    x = lax.fori_loop(0, 12, tie_step, jnp.zeros((group * ts, 1), I32))
    negm_c_all = jnp.where((sc_c > t) | ((sc_c == t) & (col_c <= x)), 0.0, -jnp.inf)
    negm_n_all = jnp.where((sc_n > t) | ((sc_n == t) & (idx_n <= x)), 0.0, -jnp.inf)

    dist_c = (row_c - col_c1 + past).astype(F32)
    dist_n = jnp.abs(row_n - col_n1).astype(F32)
    lam = _lambda(lq1_ref, lk1_ref, lq2_ref, lk2_ref)
    neg_new = jnp.where(new_ok, 0.0, -jnp.inf)
    for g in range(group):
        _sample_attend(g, stream_rows(g), negm_c_all, negm_n_all, dist_c, dist_n, lam, neg_new, pad_rows,
                       qa_ref, nka_ref, nva_ref, qb_ref, nkb_ref, nvb_ref,
                       cak_ref, cav_ref, cbk_ref, cbv_ref, sg_ref, oa_ref, ob_ref, pc_ref, pn_ref,
                       past=past, ts=ts)


def _sample_attend(g, r, negm_c_all, negm_n_all, dist_c, dist_n, lam, neg_new, pad_rows,
                   qa_ref, nka_ref, nva_ref, qb_ref, nkb_ref, nvb_ref,
                   cak_ref, cav_ref, cbk_ref, cbv_ref, sg_ref, oa_ref, ob_ref, pc_ref, pn_ref,
                   *, past, ts):
    negm_c = negm_c_all[r]
    negm_n = negm_n_all[r]
    q_all = qa_ref[:, r, :][:, :, 0:DH_A].reshape(H_A * ts, DH_A)
    s_c = _dot(q_all, cak_ref[g].astype(BF16))
    s_n = _dot_nt(q_all, pad_rows(nka_ref[r, :][:, 0:DH_A]))
    for h in range(H_A):
        rows = slice(h * ts, (h + 1) * ts)
        lc = s_c[rows] + (negm_c - (LOG2E * SLOPES_A[h]) * dist_c)
        ln = s_n[rows] + (negm_n - (LOG2E * SLOPES_A[h]) * dist_n)
        m = jnp.maximum(jnp.max(lc, axis=1, keepdims=True), jnp.max(ln, axis=1, keepdims=True))
        m = jnp.maximum(m, NEG_INIT)
        pc_ref[g, rows, :] = jnp.exp2(lc - m).astype(BF16)
        pn_ref[g, rows, :] = jnp.exp2(ln - m).astype(BF16)
    vc_aug_t = jnp.concatenate([cav_ref[g].astype(BF16), jnp.ones((64, past), BF16)], axis=0)
    pv = _dot_nt(pc_ref[g], vc_aug_t) + _dot(pn_ref[g], pad_rows(nva_ref[r, :]))
    pv = pv / pltpu.roll(pv, DH_A, axis=1)
    oa_ref[r, :] = jnp.concatenate(
        [pv[h * ts:(h + 1) * ts, :DH_A] for h in range(H_A)], axis=1).astype(BF16)

    for h in range(H_B):
        a_c = None
        a_n = None
        for c in range(2):
            hc = 2 * h + c
            q = qb_ref[hc, r, :][:, 0:DH_B]
            kc_t = cbk_ref[g, hc * DH_B:(hc + 1) * DH_B, :].astype(BF16)
            lc = _dot(q, kc_t) - (LOG2E * SLOPES_B[h]) * dist_c
            ln = _dot_nt(q, pad_rows(nkb_ref[hc, r, :][:, 0:DH_B])) + (neg_new - (LOG2E * SLOPES_B[h]) * dist_n)
            m = jnp.maximum(jnp.max(lc, axis=1, keepdims=True), jnp.max(ln, axis=1, keepdims=True))
            pc = jnp.exp2(lc - m)
            pn = jnp.exp2(ln - m)
            l = jnp.sum(pc, axis=1, keepdims=True) + jnp.sum(pn, axis=1, keepdims=True)
            pc = pc / l
            pn = pn / l
            if c == 0:
                a_c, a_n = pc, pn
            else:
                a_c, a_n = a_c - lam * pc, a_n - lam * pn
        vc = cbv_ref[g, pl.ds(h, past, stride=H_B), :].astype(BF16)
        vn = pad_rows(nvb_ref[r, h * DV_B:(h + 1) * DV_B])
        o = _dot(a_c.astype(BF16), vc) + _dot(a_n.astype(BF16), vn)
        o = _rms(o, sg_ref[...]) * (1.0 - LAM_INIT)
        ob_ref[r, h * DV_B:(h + 1) * DV_B] = o.astype(BF16)


def _sample(qa, qi, sm, nka, nva, nki, qb, nkb, nvb, cak, cav, cai, cbk, cbv,
            lq1, lk1, lq2, lk2, sg, batch, ts, past, k_top):
    group = SAMPLE_GROUP
    assert batch % group == 0
    rows = group * ts
    row = lambda b: (b, 0)
    head = lambda b: (0, b, 0)
    cache = lambda b: (b, 0, 0)
    const = lambda b: (0, 0)
    return pl.pallas_call(
        functools.partial(_sample_body, k_top=k_top, past=past, ts=ts, group=group),
        out_shape=(jax.ShapeDtypeStruct((batch * ts, H_A * DH_A), BF16),
                   jax.ShapeDtypeStruct((batch * ts, H_B * DV_B), BF16)),
        grid=(batch // group,),
        in_specs=[
            pl.BlockSpec((H_A, rows, 128), head),
            pl.BlockSpec((H_I, rows, D_IDX), head),
            pl.BlockSpec((rows, 256), row),
            pl.BlockSpec((rows, 128), row),
            pl.BlockSpec((rows, 128), row),
            pl.BlockSpec((rows, D_IDX), row),
            pl.BlockSpec((2 * H_B, rows, 128), head),
            pl.BlockSpec((2 * H_B, rows, 128), head),
            pl.BlockSpec((rows, H_B * DV_B), row),
            pl.BlockSpec((group, DH_A, past), cache),
            pl.BlockSpec((group, DH_A, past), cache),
            pl.BlockSpec((group, D_IDX, past), cache),
            pl.BlockSpec((group, 2 * H_B * DH_B, past), cache),
            pl.BlockSpec((group, past * H_B, DV_B), cache),
            pl.BlockSpec((1, DH_B), const),
            pl.BlockSpec((1, DH_B), const),
            pl.BlockSpec((1, DH_B), const),
            pl.BlockSpec((1, DH_B), const),
            pl.BlockSpec((1, DV_B), const),
        ],
        out_specs=(pl.BlockSpec((rows, H_A * DH_A), row), pl.BlockSpec((rows, H_B * DV_B), row)),
        scratch_shapes=[
            pltpu.VMEM((group, H_A * ts, past), BF16),
            pltpu.VMEM((group, H_A * ts, LANES), BF16),
        ],
        compiler_params=pltpu.CompilerParams(
            dimension_semantics=("arbitrary",), vmem_limit_bytes=VMEM_LIMIT),
        name="sample",
    )(qa, qi, sm, nka, nva, nki, qb, nkb, nvb, cak, cav, cai, cbk, cbv, lq1, lk1, lq2, lk2, sg)


def _finish_body(x_ref, oa_ref, ob_ref, wo_ref, gm_ref, wu_ref, wd_ref, gf_ref, y_ref):
    o = jnp.concatenate([oa_ref[...], ob_ref[...]], axis=1)
    h1 = x_ref[...] + _dot(o, wo_ref[...])
    hn = _rms(h1, gm_ref[...]).astype(BF16)
    acc = h1
    for c in range(D_FF // D_MODEL):
        u = jnp.maximum(_dot(hn, wu_ref[:, c * D_MODEL:(c + 1) * D_MODEL]), 0.0)
        acc = acc + _dot((u * u).astype(BF16), wd_ref[c * D_MODEL:(c + 1) * D_MODEL, :])
    y_ref[...] = _rms(acc, gf_ref[...])


def _finish(x, oa, ob, wo, gm, wu, wd, gf, tm):
    m = x.shape[0]
    assert m % tm == 0
    row = lambda i: (i, 0)
    const = lambda i: (0, 0)
    resident = dict(pipeline_mode=pl.Buffered(1))
    return pl.pallas_call(
        _finish_body,
        out_shape=jax.ShapeDtypeStruct((m, D_MODEL), F32),
        grid=(m // tm,),
        in_specs=[
            pl.BlockSpec((tm, D_MODEL), row),
            pl.BlockSpec((tm, 512), row),
            pl.BlockSpec((tm, 512), row),
            pl.BlockSpec((D_MODEL, D_MODEL), const, **resident),
            pl.BlockSpec((1, D_MODEL), const),
            pl.BlockSpec((D_MODEL, D_FF), const, **resident),
            pl.BlockSpec((D_FF, D_MODEL), const, **resident),
            pl.BlockSpec((1, D_MODEL), const),
        ],
        out_specs=pl.BlockSpec((tm, D_MODEL), row),
        compiler_params=pltpu.CompilerParams(
            dimension_semantics=("arbitrary",), vmem_limit_bytes=VMEM_LIMIT),
        name="finish",
    )(x, oa, ob, wo, gm, wu, wd, gf)


def kernel(x_prompt, x_sample, cache_a_k, cache_a_v, cache_a_idx_k, cache_b_k, cache_b_v,
           meta_tokens, attn_norm_g, w_in, idx_k_norm_g, idx_k_norm_b,
           lambda_q1, lambda_k1, lambda_q2, lambda_k2, subln_g, w_o,
           mlp_norm_g, w_up, w_down, final_norm_g):
    batch, seq, _ = x_prompt.shape
    dec_batch, ts, _ = x_sample.shape
    past = cache_a_k.shape[2]
    assert attn_norm_g.shape[0] == 1, "single-layer step"
    assert seq % TQ == 0 and ts == 16 and past % LANES == 0 and past // CHUNK == (past + ts - 1) // CHUNK
    n = N_META + seq
    k_top_p = min(TOPK_MAX, seq // 4)
    k_top_s = min(TOPK_MAX, (past + ts) // 4)

    w_t = w_in[0].T.astype(BF16)
    w = jnp.concatenate(
        [w_t[0:512],
         w_t[964:1476],
         w_t[1476:1988],
         w_t[1988:2500],
         w_t[640:896],
         w_t[512:640],
         w_t[896:964]],
        axis=0)
    w = jnp.pad(w, ((0, W_COLS - 2500), (0, 0))).T
    wvt = w_t[1476:2500]
    wo = w_o[0].astype(BF16)
    wu = w_up[0].astype(BF16)
    wd = w_down[0].astype(BF16)
    g_attn = attn_norm_g[0][None]
    g_mlp = mlp_norm_g[0][None]
    g_fin = final_norm_g[None]
    kng = idx_k_norm_g[0][None]
    knb = idx_k_norm_b[0][None]
    lq1, lk1, lq2, lk2 = lambda_q1[0][None], lambda_k1[0][None], lambda_q2[0][None], lambda_k2[0][None]
    sg = subln_g[0][None]

    xp = x_prompt.reshape(batch * seq, D_MODEL)
    xs = x_sample.reshape(dec_batch * ts, D_MODEL)

    (qa_p, qi_p, qb_p, kbh_p, vb_p, kab_p, kib_p, vt_p, wt_p, vbt_p, smt_p, kbt_p) = _proj(
        xp, g_attn, w, wvt, kng, knb, 512, N_META, seq, True)
    (_, _, _, kbh_m, vb_m, kab_m, kib_m, kb_m, vbb_m, sm_m, vaa_m) = _proj(
        meta_tokens, g_attn, w, wvt, kng, knb, N_META, 0, N_META, False)
    (qa_s, qi_s, qb_s, kbh_s, vb_s, kab_s, kib_s, kb_s, vbb_s, sm_s, vaa_s) = _proj(
        xs, g_attn, w, wvt, kng, knb, dec_batch * ts, past, dec_batch * ts, False)

    oa_p = _dsa(qa_p, qi_p, wt_p, kab_p, kib_p, vt_p, kab_m, kib_m, vaa_m, batch, seq, k_top_p)
    ob_p = _diff(qb_p, kbh_p, vbt_p, kbh_m, vbb_m, lq1, lk1, lq2, lk2, sg, batch, seq)
    y_prompt = _finish(xp, oa_p, ob_p, wo, g_mlp, wu, wd, g_fin, 512).reshape(batch, seq, D_MODEL)

    oa_s, ob_s = _sample(
        qa_s, qi_s, sm_s, kab_s, vaa_s, kib_s, qb_s, kbh_s, vbb_s,
        jnp.swapaxes(cache_a_k[0], 1, 2), jnp.swapaxes(cache_a_v[0], 1, 2),
        jnp.swapaxes(cache_a_idx_k[0], 1, 2),
        jnp.transpose(cache_b_k[0], (0, 2, 3, 4, 1)).reshape(dec_batch, 2 * H_B * DH_B, past),
        cache_b_v[0].reshape(dec_batch, past * H_B, DV_B),
        lq1, lk1, lq2, lk2, sg, dec_batch, ts, past, k_top_s)
    y_sample = _finish(xs, oa_s, ob_s, wo, g_mlp, wu, wd, g_fin, dec_batch * ts).reshape(
        dec_batch, ts, D_MODEL)

    def with_meta_t(meta_rows, main_t):
        meta_b = jnp.broadcast_to(meta_rows.T[None], (batch, 64, N_META))
        return jnp.swapaxes(jnp.concatenate([meta_b, main_t], axis=2), 1, 2)[None]

    new_a_k_p = with_meta_t(sm_m[:, SM_KA:SM_KA + 64], smt_p[:, SM_KA:SM_KA + 64])
    new_a_v_p = with_meta_t(sm_m[:, SM_VA:SM_VA + 64], smt_p[:, SM_VA:SM_VA + 64])
    new_a_i_p = with_meta_t(sm_m[:, SM_KI:SM_KI + 64], smt_p[:, SM_KI:SM_KI + 64])
    kb_meta_t = jnp.broadcast_to(kb_m.T.reshape(1, H_B, 2, DH_B, N_META), (batch, H_B, 2, DH_B, N_META))
    kb_main_t = kbt_p.reshape(batch, H_B, 2, DH_B, seq)
    new_b_k_p = jnp.transpose(
        jnp.concatenate([kb_meta_t, kb_main_t], axis=4), (0, 4, 1, 2, 3))[None]
    new_b_v_p = jnp.concatenate(
        [jnp.broadcast_to(vb_m.reshape(1, N_META, H_B, DV_B), (batch, N_META, H_B, DV_B)),
         vb_p.reshape(batch, seq, H_B, DV_B)], axis=1)[None]
    new_a_k_s = sm_s[:, SM_KA:SM_KA + 64].reshape(1, dec_batch, ts, 64)
    new_a_v_s = sm_s[:, SM_VA:SM_VA + 64].reshape(1, dec_batch, ts, 64)
    new_a_i_s = sm_s[:, SM_KI:SM_KI + 64].reshape(1, dec_batch, ts, 64)
    new_b_k_s = kb_s.reshape(1, dec_batch, ts, H_B, 2, DH_B)
    new_b_v_s = vb_s.reshape(1, dec_batch, ts, H_B, DV_B)
    return (y_prompt, y_sample, new_a_k_p, new_a_v_p, new_a_i_p, new_b_k_p, new_b_v_p,
            new_a_k_s, new_a_v_s, new_a_i_s, new_b_k_s, new_b_v_s)
```

```python
import functools

import jax
import jax.numpy as jnp
import numpy as np
from jax import lax
from jax.experimental import pallas as pl
from jax.experimental.pallas import tpu as pltpu

F32 = jnp.float32
BF16 = jnp.bfloat16
I32 = jnp.int32
HALF16 = 32768

D_MODEL = 1024
CHUNK = 64
N_META = 16
H_A = 8
DH_A = 64
H_I = 4
D_IDX = 64
TOPK_MAX = 256
H_B = 4
DH_B = 64
DV_B = 2 * DH_B
D_FF = 4 * D_MODEL
EPS = 1e-6
LAM_INIT = 0.2

LANES = 128
C_QA, C_QB, C_KB, C_VB, C_QI, C_SM = 0, 512, 1024, 1536, 2048, 2304
W_COLS = 2560
SM_KA, SM_VA, SM_KI, SM_WI = 0, 64, 128, 192

LOG2E = float(np.float32(1.4426950408889634))
Q_SCALE = (DH_A ** -0.5) * LOG2E
N_SLOPE_PIECES = 3


def _bf16_pieces(value):
    rest = np.float32(value)
    pieces = []
    for _ in range(N_SLOPE_PIECES):
        piece = np.float32(rest.astype(jnp.bfloat16))
        pieces.append(float(piece))
        rest = np.float32(rest - piece)
    assert rest == 0.0
    return pieces
W_SCALE = (H_I ** -0.5) * (D_IDX ** -0.5)
SLOPES_A = tuple(2.0 ** (-8.0 * (i + 1) / H_A) for i in range(H_A))
SLOPES_B = tuple(2.0 ** (-8.0 * (i + 1) / H_B) for i in range(H_B))

INT_MIN = -(2 ** 31)
NEG_INIT = -1e30
TQ = 256
ONES_ROWS = 16
VT_ROWS = DV_B + ONES_ROWS
VTA_ROWS = DH_A + ONES_ROWS
CNT_ROWS = 32
SAMPLE_GROUP = 4
VMEM_LIMIT = 56 * 1024 * 1024

_NT = (((1,), (1,)), ((), ()))


def _dot(a, b):
    return jnp.dot(a, b, preferred_element_type=F32)


def _dot_nt(a, b):
    return lax.dot_general(a, b, _NT, preferred_element_type=F32)


def _rms(x, g):
    return (x * lax.rsqrt(jnp.mean(x * x, axis=-1, keepdims=True) + EPS)) * g


def _to_key(score):
    b = lax.bitcast_convert_type(score, I32)
    return b ^ ((b >> 31) & jnp.int32(0x7FFFFFFF))


def _f32_from_key(key):
    return lax.bitcast_convert_type(key ^ ((key >> 31) & jnp.int32(0x7FFFFFFF)), F32)


def _bf16_from_key(key):
    bits = key ^ ((key >> 15) & jnp.int32(0x7FFF))
    return lax.bitcast_convert_type(jnp.left_shift(bits, 16), F32).astype(BF16)


KEY16_LOWEST = -32640
KEY32_LOWEST = -2139095040
KEY32_HIGHEST = 2139095039


def _proj_body(x_ref, g_ref, w_ref, wvt_ref, kng_ref, knb_ref,
               qa_ref, qi_ref, qb_ref, kbh_ref, vb_ref, kab_ref, kib_ref, *mode_refs,
               pos0, period, key_major):
    tm = x_ref.shape[0]
    halves = 2 if key_major else 1
    hr = tm // halves
    for half in range(halves):
        rs = slice(half * hr, (half + 1) * hr)
        if key_major:
            vt_ref, wt_ref, vbt_ref, smt_ref, kbt_ref = mode_refs
            cs = slice(half * hr // TQ, (half + 1) * hr // TQ)
            mode_views = (vt_ref.at[cs], wt_ref.at[:, rs], vbt_ref.at[cs], smt_ref.at[:, rs], kbt_ref.at[:, rs])
        else:
            mode_views = mode_refs
        _proj_rows(x_ref.at[rs, :], g_ref, w_ref, wvt_ref, kng_ref, knb_ref,
                   qa_ref.at[:, rs, :], qi_ref.at[:, rs, :], qb_ref.at[:, rs, :], kbh_ref.at[:, rs, :],
                   vb_ref.at[half * hr * H_B:(half + 1) * hr * H_B, :], kab_ref.at[rs, :], kib_ref.at[rs, :],
                   *mode_views, row0=pl.program_id(0) * tm + half * hr,
                   pos0=pos0, period=period, key_major=key_major)


def _proj_rows(x_ref, g_ref, w_ref, wvt_ref, kng_ref, knb_ref,
               qa_ref, qi_ref, qb_ref, kbh_ref, vb_ref, kab_ref, kib_ref, *mode_refs,
               row0, pos0, period, key_major):
    x = x_ref[...]
    hn = _rms(x, g_ref[...]).astype(BF16)
    rows = x.shape[0]

    def mm(lo, width):
        return _dot(hn, w_ref[:, lo:lo + width])

    lane = lax.broadcasted_iota(I32, (rows, 64), 1)
    r = row0 + lax.broadcasted_iota(I32, (rows, 64), 0)
    pos = pos0 + r % period
    pos_cols = jnp.where(lane >= 2 * N_SLOPE_PIECES, 0,
                         jnp.where(lane % 2 == 0, (pos // 256) * 256, pos % 256))
    pos_cols = pos_cols.astype(F32).astype(BF16)

    def slope_cols(slope):
        cols = jnp.zeros((rows, 64), F32)
        for n, piece in enumerate(_bf16_pieces(slope * LOG2E)):
            cols = jnp.where(lane // 2 == n, piece, cols)
        return cols.astype(BF16)

    z = mm(C_QA, 512) * Q_SCALE
    for h in range(H_A):
        qa_ref[h] = jnp.concatenate(
            [z[:, h * DH_A:(h + 1) * DH_A].astype(BF16), slope_cols(SLOPES_A[h])], axis=1)
    z = mm(C_QB, 512) * Q_SCALE
    for hc in range(2 * H_B):
        qb_ref[hc] = jnp.concatenate(
            [z[:, hc * DH_B:(hc + 1) * DH_B].astype(BF16), slope_cols(SLOPES_B[hc // 2])], axis=1)
    if key_major:
        vt_ref, wt_ref, vbt_ref, smt_ref, kbt_ref = mode_refs
    else:
        kb_ref, vbb_ref, sm_ref, vaa_ref = mode_refs
    z = mm(C_KB, 512)
    if not key_major:
        kb_ref[...] = z
    for hc in range(2 * H_B):
        kbh_ref[hc] = jnp.concatenate([z[:, hc * DH_B:(hc + 1) * DH_B].astype(BF16), pos_cols], axis=1)
    z = mm(C_VB, 512)
    for h in range(H_B):
        vb_ref[pl.ds(h, rows, stride=H_B), :] = z[:, h * DV_B:(h + 1) * DV_B]
    if not key_major:
        vbb_ref[...] = z.astype(BF16)
    z = mm(C_QI, 256)
    for h in range(H_I):
        qi_ref[h] = z[:, h * D_IDX:(h + 1) * D_IDX].astype(BF16)
    z = mm(C_SM, 256)
    ka = z[:, SM_KA:SM_KA + 64]
    va = z[:, SM_VA:SM_VA + 64]
    ki = z[:, SM_KI:SM_KI + 64]
    xc = ki - jnp.mean(ki, axis=-1, keepdims=True)
    ki = xc * lax.rsqrt(jnp.mean(xc * xc, axis=-1, keepdims=True) + EPS) * kng_ref[...] + knb_ref[...]
    kab_ref[...] = jnp.concatenate([ka.astype(BF16), pos_cols], axis=1)
    kib_ref[...] = ki.astype(BF16)
    if not key_major:
        sm_ref[:, 0:128] = z[:, 0:128]
        sm_ref[:, SM_KI:SM_KI + 64] = ki
        sm_ref[:, SM_WI:SM_WI + 64] = z[:, SM_WI:SM_WI + 64] * W_SCALE
        vaa_ref[...] = jnp.concatenate([va.astype(BF16), jnp.ones((rows, 64), BF16)], axis=1)
    else:
        kv_t = z[:, 0:128].T
        for c in range(rows // TQ):
            vt_ref[c] = jnp.concatenate(
                [kv_t[64:128, c * TQ:(c + 1) * TQ].astype(BF16), jnp.ones((ONES_ROWS, TQ), BF16)], axis=0)
        kw_t = jnp.concatenate([ki, z[:, SM_WI:SM_WI + 64] * W_SCALE], axis=1).T
        wt_ref[...] = kw_t[64:72, :]
        smt_ref[0:128, :] = kv_t
        smt_ref[128:192, :] = kw_t[0:64, :]
        kbt_ref[...] = _dot_nt(wvt_ref[0:512, :], hn)
        vb_t = _dot_nt(wvt_ref[512:1024, :], hn)
        for c in range(rows // TQ):
            for h in range(H_B):
                vbt_ref[c, h] = jnp.concatenate(
                    [vb_t[h * DV_B:(h + 1) * DV_B, c * TQ:(c + 1) * TQ].astype(BF16),
                     jnp.ones((VT_ROWS - DV_B, TQ), BF16)], axis=0)


def _proj(x, g, w, wvt, kng, knb, tm, pos0, period, key_major):
    m = x.shape[0]
    assert m % tm == 0
    row = lambda i: (i, 0)
    head = lambda i: (0, i, 0)
    const = lambda i: (0, 0)
    out_shape = (
        jax.ShapeDtypeStruct((H_A, m, 128), BF16),
        jax.ShapeDtypeStruct((H_I, m, D_IDX), BF16),
        jax.ShapeDtypeStruct((2 * H_B, m, 128), BF16),
        jax.ShapeDtypeStruct((2 * H_B, m, 128), BF16),
        jax.ShapeDtypeStruct((m * H_B, DV_B), F32),
        jax.ShapeDtypeStruct((m, 128), BF16),
        jax.ShapeDtypeStruct((m, 64), BF16),
    )
    out_specs = (
        pl.BlockSpec((H_A, tm, 128), head),
        pl.BlockSpec((H_I, tm, D_IDX), head),
        pl.BlockSpec((2 * H_B, tm, 128), head),
        pl.BlockSpec((2 * H_B, tm, 128), head),
        pl.BlockSpec((tm * H_B, DV_B), row),
        pl.BlockSpec((tm, 128), row),
        pl.BlockSpec((tm, 64), row),
    )
    if not key_major:
        out_shape += (
            jax.ShapeDtypeStruct((m, 512), F32),
            jax.ShapeDtypeStruct((m, 512), BF16),
            jax.ShapeDtypeStruct((m, 256), F32),
            jax.ShapeDtypeStruct((m, 128), BF16),
        )
        out_specs += (
            pl.BlockSpec((tm, 512), row),
            pl.BlockSpec((tm, 512), row),
            pl.BlockSpec((tm, 256), row),
            pl.BlockSpec((tm, 128), row),
        )
    else:
        assert tm % TQ == 0 and period % tm == 0 and m % period == 0
        per_stream = period // tm
        stream_cols = lambda i: (i // per_stream, 0, i % per_stream)
        out_shape += (
            jax.ShapeDtypeStruct((m // TQ, VTA_ROWS, TQ), BF16),
            jax.ShapeDtypeStruct((8, m), F32),
            jax.ShapeDtypeStruct((m // TQ, H_B, VT_ROWS, TQ), BF16),
            jax.ShapeDtypeStruct((m // period, 192, period), F32),
            jax.ShapeDtypeStruct((m // period, 512, period), F32),
        )
        out_specs += (
            pl.BlockSpec((tm // TQ, VTA_ROWS, TQ), lambda i: (i, 0, 0)),
            pl.BlockSpec((8, tm), lambda i: (0, i)),
            pl.BlockSpec((tm // TQ, H_B, VT_ROWS, TQ), lambda i: (i, 0, 0, 0)),
            pl.BlockSpec((None, 192, tm), stream_cols),
            pl.BlockSpec((None, 512, tm), stream_cols),
        )
    return pl.pallas_call(
        functools.partial(_proj_body, pos0=pos0, period=period, key_major=key_major),
        out_shape=out_shape,
        grid=(m // tm,),
        in_specs=[
            pl.BlockSpec((tm, D_MODEL), row),
            pl.BlockSpec((1, D_MODEL), const),
            pl.BlockSpec((D_MODEL, W_COLS), const),
            pl.BlockSpec((2 * H_B * DV_B, D_MODEL), const),
            pl.BlockSpec((1, D_IDX), const),
            pl.BlockSpec((1, D_IDX), const),
        ],
        out_specs=out_specs,
        compiler_params=pltpu.CompilerParams(
            dimension_semantics=("arbitrary",), vmem_limit_bytes=VMEM_LIMIT),
        name="proj",
    )(x, g, w, wvt, kng, knb)


def _dsa_body(qa_ref, qi_ref, wt_ref, ka_ref, ki_ref, vt_ref, mka_ref, mki_ref, mva_ref,
              o_ref, smeta_ref, smain_ref, bmeta_ref, bmain_ref,
              tsel_ref, m_ref, al_ref, acc_ref, p_ref, s_ref, mb_ref, seen_ref, *, k_top):
    i = pl.program_id(1)
    krow = lax.broadcasted_iota(I32, (TQ, TQ), 0)
    qcol = lax.broadcasted_iota(I32, (TQ, TQ), 1)

    def main_rows(j):
        return pl.ds(pl.multiple_of(j * TQ, TQ), TQ)

    qi_all = qi_ref[...].reshape(H_I * TQ, D_IDX)

    def scores(ki_blk):
        s4 = _dot_nt(ki_blk, qi_all)
        sc = None
        for h in range(H_I):
            t = jnp.maximum(s4[:, h * TQ:(h + 1) * TQ], 0.0) * wt_ref[h:h + 1, :]
            sc = t if sc is None else sc + t
        return sc

    sc = scores(mki_ref[...])
    smeta_ref[...] = sc
    bmeta_ref[...] = sc.astype(BF16)

    def score_block(j, sc):
        smain_ref[j] = sc
        bmain_ref[j] = sc.astype(BF16)

    def full_scores(j):
        return scores(ki_ref[main_rows(j), :])

    def last_scores():
        return jnp.where((krow // CHUNK) <= (qcol // CHUNK), full_scores(i), -jnp.inf)

    def score_two(u, carry):
        sc_a, sc_b = full_scores(2 * u), full_scores(2 * u + 1)
        score_block(2 * u, sc_a)
        score_block(2 * u + 1, sc_b)
        return carry

    lax.fori_loop(0, i // 2, score_two, 0)

    @pl.when(i % 2 == 1)
    def _odd_scores():
        sc_a, sc_b = full_scores(i - 1), last_scores()
        score_block(i - 1, sc_a)
        score_block(i, sc_b)

    @pl.when(i % 2 == 0)
    def _even_scores():
        score_block(i, last_scores())

    def search(odd_blocks):
        def count(meta_ref, main_ref, part, pred):
            def one(j):
                return part(pred(main_ref[j]))

            def two(u, c):
                return c + (one(2 * u) + one(2 * u + 1))

            c = lax.fori_loop(0, (i + 1) // 2, two, jnp.zeros((CNT_ROWS, TQ), F32))
            if odd_blocks:
                c = c + one(i)
            cm = part(pred(meta_ref[...]), rows=N_META)
            return jnp.sum(c, axis=0, keepdims=True) + jnp.sum(cm, axis=0, keepdims=True)

        def part32(hit, rows=CNT_ROWS):
            ones = jnp.where(hit, 1.0, 0.0)
            return jnp.sum(ones.reshape(ones.shape[0] // rows, rows, TQ), axis=0)

        def part16(hit, rows=CNT_ROWS):
            ones = jnp.where(hit, jnp.bfloat16(1), jnp.bfloat16(0))
            acc = ones[0:rows]
            for t in range(1, ones.shape[0] // rows):
                acc = acc + ones[t * rows:(t + 1) * rows]
            return acc.astype(F32)

        kf = float(k_top)

        def step16(s, u):
            cand_u = u | jnp.left_shift(jnp.int32(1), 15 - s)
            cand = _bf16_from_key(jnp.maximum(cand_u, KEY16_LOWEST + HALF16) - HALF16)
            cnt = count(bmeta_ref, bmain_ref, part16, lambda x: x >= cand)
            return jnp.where(cnt >= kf, cand_u, u)

        u_hi = lax.fori_loop(0, 16, step16, jnp.zeros((1, TQ), I32))
        t_hi = _bf16_from_key(jnp.maximum(u_hi, KEY16_LOWEST + HALF16) - HALF16).astype(F32)
        base = jnp.maximum(_to_key(t_hi) - 65536, KEY32_LOWEST)

        def step32(s, d):
            cand_d = d | jnp.left_shift(jnp.int32(1), 16 - s)
            cand = _f32_from_key(jnp.minimum(base + cand_d, KEY32_HIGHEST))
            cnt = count(smeta_ref, smain_ref, part32, lambda x: x >= cand)
            return jnp.where(cnt >= kf, cand_d, d)

        d_lo = lax.fori_loop(0, 17, step32, jnp.zeros((1, TQ), I32))
        t_sel = _f32_from_key(jnp.minimum(base + d_lo, KEY32_HIGHEST))
        tsel_ref[0:1, :] = t_sel
        tsel_ref[1:2, :] = kf - count(smeta_ref, smain_ref, part32, lambda x: x > t_sel)

    pl.when(i % 2 == 0)(lambda: search(True))
    pl.when(i % 2 == 1)(lambda: search(False))

    m_ref[...] = jnp.full(m_ref.shape, NEG_INIT, F32)
    acc_ref[...] = jnp.zeros(acc_ref.shape, F32)
    thr = tsel_ref[0:1, :]
    need = tsel_ref[1:2, :]

    def masked_logits(k_blk, sc, slot):
        nk = sc.shape[0]
        tie = sc == thr
        tri = (lax.broadcasted_iota(I32, (nk, nk), 1)
               <= lax.broadcasted_iota(I32, (nk, nk), 0)).astype(BF16)
        rank = _dot(tri, jnp.where(tie, 1.0, 0.0).astype(BF16)) + seen_ref[...]
        seen_ref[...] = rank[nk - 1:nk, :]
        negm = jnp.where(sc > thr, 0.0, jnp.where(tie, jnp.where(rank <= need, 0.0, -jnp.inf), -jnp.inf))
        for h in range(H_A):
            cols = slice(h * TQ, (h + 1) * TQ)
            s = _dot_nt(k_blk, qa_ref[h]) + negm
            s_ref[slot, 0:nk, cols] = s
            mb_ref[slot, :, cols] = jnp.max(s, axis=0, keepdims=True)

    def softmax(slot, nk, future):
        for h in range(H_A):
            cols = slice(h * TQ, (h + 1) * TQ)
            s = s_ref[slot, 0:nk, cols]
            if future is not None:
                s = s - (2.0 * LOG2E * SLOPES_A[h]) * future
            m_old = m_ref[:, cols]
            m_new = jnp.maximum(m_old, mb_ref[slot, :, cols])
            p_ref[slot, 0:nk, cols] = jnp.exp2(s - m_new).astype(BF16)
            al_ref[slot, :, cols] = jnp.exp2(m_old - m_new)
            m_ref[:, cols] = m_new

    def accumulate(vt_blk, slot, nk):
        acc_ref[...] = acc_ref[...] * al_ref[slot] + _dot(vt_blk, p_ref[slot, 0:nk, :])

    eye = (lax.broadcasted_iota(I32, (128, 128), 0)
           == lax.broadcasted_iota(I32, (128, 128), 1)).astype(BF16)
    seen_ref[...] = jnp.zeros(seen_ref.shape, F32)
    masked_logits(mka_ref[...], smeta_ref[...], 1)
    masked_logits(ka_ref[main_rows(0), :], smain_ref[0], 0)
    softmax(1, N_META, None)
    accumulate(_dot_nt(eye, mva_ref[...])[0:VTA_ROWS].astype(BF16), 1, N_META)

    p_ref[1] = jnp.zeros(p_ref.shape[1:], BF16)
    al_ref[1] = jnp.ones(al_ref.shape[1:], F32)

    def full_step(t, slot):
        masked_logits(ka_ref[main_rows(t + 1), :], smain_ref[t + 1], 1 - slot)
        accumulate(vt_ref[jnp.maximum(t - 1, 0)], 1 - slot, TQ)
        softmax(slot, TQ, None)

    def two_steps(u, carry):
        full_step(2 * u, 0)
        full_step(2 * u + 1, 1)
        return carry

    lax.fori_loop(0, i // 2, two_steps, 0)

    def last_step(slot):
        accumulate(vt_ref[jnp.maximum(i - 1, 0)], 1 - slot, TQ)
        softmax(slot, TQ, jnp.maximum(krow - qcol, 0).astype(F32))
        accumulate(vt_ref[i], slot, TQ)

    @pl.when(i % 2 == 1)
    def _odd():
        full_step(i - 1, 0)
        last_step(1)

    @pl.when(i % 2 == 0)
    def _even():
        last_step(0)

    def head_out_t(h):
        a = acc_ref[:, h * TQ:(h + 1) * TQ]
        return a[0:DH_A] / a[DH_A:DH_A + 1]

    for h in range(0, H_A, 2):
        pair_t = jnp.concatenate([head_out_t(h), head_out_t(h + 1)], axis=0)
        o_ref[:, h * DH_A:(h + 2) * DH_A] = pair_t.T.astype(BF16)


def _dsa(qa, qi, wt, kab, kib, vt, mka, mki, mva, batch, seq, k_top):
    nq = seq // TQ
    qrow = lambda b, i: (b * nq + i, 0)
    qhead = lambda b, i: (0, b * nq + i, 0)
    kv = lambda b, i: (b, 0)
    const = lambda b, i: (0, 0)
    return pl.pallas_call(
        functools.partial(_dsa_body, k_top=k_top),
        out_shape=jax.ShapeDtypeStruct((batch * seq, H_A * DH_A), BF16),
        grid=(batch, nq),
        in_specs=[
            pl.BlockSpec((H_A, TQ, 128), qhead),
            pl.BlockSpec((H_I, TQ, D_IDX), qhead),
            pl.BlockSpec((8, TQ), lambda b, i: (0, b * nq + i)),
            pl.BlockSpec((seq, 128), kv),
            pl.BlockSpec((seq, D_IDX), kv),
            pl.BlockSpec((nq, VTA_ROWS, TQ), lambda b, i: (b, 0, 0)),
            pl.BlockSpec((N_META, 128), const),
            pl.BlockSpec((N_META, D_IDX), const),
            pl.BlockSpec((N_META, 128), const),
        ],
        out_specs=pl.BlockSpec((TQ, H_A * DH_A), qrow),
        scratch_shapes=[
            pltpu.VMEM((N_META, TQ), F32),
            pltpu.VMEM((nq, TQ, TQ), F32),
            pltpu.VMEM((N_META, TQ), BF16),
            pltpu.VMEM((nq, TQ, TQ), BF16),
            pltpu.VMEM((8, TQ), F32),
            pltpu.VMEM((1, H_A * TQ), F32),
            pltpu.VMEM((2, 1, H_A * TQ), F32),
            pltpu.VMEM((VTA_ROWS, H_A * TQ), F32),
            pltpu.VMEM((2, TQ, H_A * TQ), BF16),
            pltpu.VMEM((2, TQ, H_A * TQ), F32),
            pltpu.VMEM((2, 1, H_A * TQ), F32),
            pltpu.VMEM((1, TQ), F32),
        ],
        compiler_params=pltpu.CompilerParams(
            dimension_semantics=("arbitrary", "arbitrary"), vmem_limit_bytes=VMEM_LIMIT),
        name="dsa",
    )(qa, qi, wt, kab, kib, vt, mka, mki, mva)


def _lambda(lq1_ref, lk1_ref, lq2_ref, lk2_ref):
    s1 = jnp.sum(lq1_ref[...] * lk1_ref[...], axis=-1, keepdims=True)
    s2 = jnp.sum(lq2_ref[...] * lk2_ref[...], axis=-1, keepdims=True)
    return jnp.exp(s1) - jnp.exp(s2) + LAM_INIT


def _diff_body(qb_ref, kb_ref, vt_ref, mkb_ref, mvb_ref, lq1_ref, lk1_ref, lq2_ref, lk2_ref,
               sg_ref, o_ref, m_ref, al_ref, acc_ref, p_ref, s_ref, mb_ref):
    i = pl.program_id(1)
    n_hc = 2 * H_B
    krow = lax.broadcasted_iota(I32, (TQ, TQ), 0)
    qcol = lax.broadcasted_iota(I32, (TQ, TQ), 1)
    m_ref[...] = jnp.full(m_ref.shape, NEG_INIT, F32)
    acc_ref[...] = jnp.zeros(acc_ref.shape, F32)

    def main_rows(j):
        return pl.ds(pl.multiple_of(j * TQ, TQ), TQ)

    def logits(k_of, slot, nk, last=False):
        if last:
            negm = jnp.where((krow // CHUNK) <= (qcol // CHUNK), 0.0, -jnp.inf)
            future = jnp.maximum(krow - qcol, 0).astype(F32)
        for hc in range(n_hc):
            s = _dot_nt(k_of(hc), qb_ref[hc])
            if last:
                s = (s + negm) - (2.0 * LOG2E * SLOPES_B[hc // 2]) * future
            s_ref[slot, hc, 0:nk, :] = s
            mb_ref[slot, :, hc * TQ:(hc + 1) * TQ] = jnp.max(s, axis=0, keepdims=True)

    def softmax(slot, nk):
        for hc in range(n_hc):
            cols = slice(hc * TQ, (hc + 1) * TQ)
            m_old = m_ref[:, cols]
            m_new = jnp.maximum(m_old, mb_ref[slot, :, cols])
            p_ref[slot, 0:nk, cols] = jnp.exp2(s_ref[slot, hc, 0:nk, :] - m_new).astype(BF16)
            al_ref[slot, :, cols] = jnp.exp2(m_old - m_new)
            m_ref[:, cols] = m_new

    def accumulate(vt_of, slot, nk):
        for h in range(H_B):
            cols = slice(2 * h * TQ, (2 * h + 2) * TQ)
            acc_ref[h] = acc_ref[h] * al_ref[slot, :, cols] + _dot(vt_of(h), p_ref[slot, 0:nk, cols])

    eye = (lax.broadcasted_iota(I32, (DV_B, DV_B), 0)
           == lax.broadcasted_iota(I32, (DV_B, DV_B), 1)).astype(BF16)

    def meta_vt(h):
        v_t = _dot_nt(eye, mvb_ref[:, h * DV_B:(h + 1) * DV_B]).astype(BF16)
        return jnp.concatenate([v_t, jnp.ones((VT_ROWS - DV_B, N_META), BF16)], axis=0)

    def main_vt(t):
        return lambda h: vt_ref[t, h]

    def start(first_is_last):
        logits(lambda hc: mkb_ref[hc], 1, N_META)
        logits(lambda hc: kb_ref[hc, main_rows(0), :], 0, TQ, last=first_is_last)
        softmax(1, N_META)
        accumulate(meta_vt, 1, N_META)
        p_ref[1] = jnp.zeros(p_ref.shape[1:], BF16)
        al_ref[1] = jnp.ones(al_ref.shape[1:], F32)

    def full_step(t, slot, next_is_last=False):
        logits(lambda hc: kb_ref[hc, main_rows(t + 1), :], 1 - slot, TQ, last=next_is_last)
        accumulate(main_vt(jnp.maximum(t - 1, 0)), 1 - slot, TQ)
        softmax(slot, TQ)

    def two_steps(u, carry):
        full_step(2 * u, 0)
        full_step(2 * u + 1, 1)
        return carry

    def last_step(slot):
        accumulate(main_vt(jnp.maximum(i - 1, 0)), 1 - slot, TQ)
        softmax(slot, TQ)
        accumulate(main_vt(i), slot, TQ)

    @pl.when(i == 0)
    def _first_tile():
        start(True)
        last_step(0)

    @pl.when(i % 2 == 1)
    def _odd():
        start(False)
        lax.fori_loop(0, i // 2, two_steps, 0)
        full_step(i - 1, 0, next_is_last=True)
        last_step(1)

    @pl.when((i > 0) & (i % 2 == 0))
    def _even():
        start(False)
        lax.fori_loop(0, i // 2 - 1, two_steps, 0)
        full_step(i - 2, 0)
        full_step(i - 1, 1, next_is_last=True)
        last_step(0)

    lam = _lambda(lq1_ref, lk1_ref, lq2_ref, lk2_ref)
    for h in range(H_B):
        a = acc_ref[h]
        o0 = a[0:DV_B, 0:TQ] / a[DV_B:DV_B + 1, 0:TQ]
        o1 = a[0:DV_B, TQ:2 * TQ] / a[DV_B:DV_B + 1, TQ:2 * TQ]
        o = _rms((o0 - lam * o1).T, sg_ref[...]) * (1.0 - LAM_INIT)
        o_ref[:, h * DV_B:(h + 1) * DV_B] = o.astype(BF16)


def _diff(qb, kbh, vbt, mkb, mvb, lq1, lk1, lq2, lk2, sg, batch, seq):
    nq = seq // TQ
    qrow = lambda b, i: (b * nq + i, 0)
    qhead = lambda b, i: (0, b * nq + i, 0)
    const2 = lambda b, i: (0, 0)
    const3 = lambda b, i: (0, 0, 0)
    return pl.pallas_call(
        _diff_body,
        out_shape=jax.ShapeDtypeStruct((batch * seq, H_B * DV_B), BF16),
        grid=(batch, nq),
        in_specs=[
            pl.BlockSpec((2 * H_B, TQ, 128), qhead),
            pl.BlockSpec((2 * H_B, seq, 128), lambda b, i: (0, b, 0)),
            pl.BlockSpec((nq, H_B, VT_ROWS, TQ), lambda b, i: (b, 0, 0, 0)),
            pl.BlockSpec((2 * H_B, N_META, 128), const3),
            pl.BlockSpec((N_META, H_B * DV_B), const2),
            pl.BlockSpec((1, DH_B), const2),
            pl.BlockSpec((1, DH_B), const2),
            pl.BlockSpec((1, DH_B), const2),
            pl.BlockSpec((1, DH_B), const2),
            pl.BlockSpec((1, DV_B), const2),
        ],
        out_specs=pl.BlockSpec((TQ, H_B * DV_B), qrow),
        scratch_shapes=[
            pltpu.VMEM((1, 2 * H_B * TQ), F32),
            pltpu.VMEM((2, 1, 2 * H_B * TQ), F32),
            pltpu.VMEM((H_B, VT_ROWS, 2 * TQ), F32),
            pltpu.VMEM((2, TQ, 2 * H_B * TQ), BF16),
            pltpu.VMEM((2, 2 * H_B, TQ, TQ), F32),
            pltpu.VMEM((2, 1, 2 * H_B * TQ), F32),
        ],
        compiler_params=pltpu.CompilerParams(
            dimension_semantics=("arbitrary", "arbitrary"), vmem_limit_bytes=VMEM_LIMIT),
        name="diff",
    )(qb, kbh, vbt, mkb, mvb, lq1, lk1, lq2, lk2, sg)


def _sample_body(qa_ref, qi_ref, sm_ref, nka_ref, nva_ref, nki_ref, qb_ref, nkb_ref, nvb_ref,
                 cak_ref, cav_ref, cai_ref, cbk_ref, cbv_ref,
                 lq1_ref, lk1_ref, lq2_ref, lk2_ref, sg_ref,
                 oa_ref, ob_ref, pc_ref, pn_ref, *, k_top, past, ts, group):
    pad = LANES - ts
    row_c = lax.broadcasted_iota(I32, (ts, past), 0)
    col_c1 = lax.broadcasted_iota(I32, (ts, past), 1)
    row_n = lax.broadcasted_iota(I32, (ts, LANES), 0)
    col_n1 = lax.broadcasted_iota(I32, (ts, LANES), 1)
    new_ok = col_n1 < ts
    col_c = lax.broadcasted_iota(I32, (group * ts, past), 1)
    col_n = lax.broadcasted_iota(I32, (group * ts, LANES), 1)

    def pad_rows(x):
        return jnp.concatenate([x, jnp.zeros((pad,) + x.shape[1:], x.dtype)], axis=0)

    def stream_rows(g):
        return slice(g * ts, (g + 1) * ts)

    def indexer_scores(g):
        r = stream_rows(g)
        qi_all = qi_ref[:, r, :].reshape(H_I * ts, D_IDX)
        w = sm_ref[r, SM_WI:SM_WI + H_I]
        s4c = _dot(qi_all, cai_ref[g].astype(BF16))
        s4n = _dot_nt(qi_all, pad_rows(nki_ref[r, :]))
        sc_c = None
        sc_n = None
        for h in range(H_I):
            wh = w[:, h:h + 1]
            tc = jnp.maximum(s4c[h * ts:(h + 1) * ts], 0.0) * wh
            tn = jnp.maximum(s4n[h * ts:(h + 1) * ts], 0.0) * wh
            sc_c = tc if sc_c is None else sc_c + tc
            sc_n = tn if sc_n is None else sc_n + tn
        return sc_c, jnp.where(new_ok, sc_n, -jnp.inf)

    per_stream = [indexer_scores(g) for g in range(group)]
    sc_c = jnp.concatenate([p[0] for p in per_stream], axis=0)
    sc_n = jnp.concatenate([p[1] for p in per_stream], axis=0)

    def count(pred_c, pred_n):
        c = jnp.sum(jnp.where(pred_c, 1.0, 0.0), axis=1, keepdims=True)
        return c + jnp.sum(jnp.where(pred_n, 1.0, 0.0), axis=1, keepdims=True)

    def threshold(key):
        return _f32_from_key(jnp.clip(key, KEY32_LOWEST, KEY32_HIGHEST))

    def value_step(s, key):
        cand_key = key ^ jnp.left_shift(jnp.int32(1), 31 - s)
        cand = threshold(cand_key)
        cnt = count(sc_c >= cand, sc_n >= cand)
        return jnp.where(cnt >= float(k_top), cand_key, key)

    t = threshold(lax.fori_loop(0, 32, value_step, jnp.full((group * ts, 1), INT_MIN, I32)))
    need = float(k_top) - count(sc_c > t, sc_n > t)
    idx_n = col_n + past

    def tie_step(s, x):
        cand = x | jnp.left_shift(jnp.int32(1), 11 - s)
        cnt = count((sc_c == t) & (col_c < cand), (sc_n == t) & (idx_n < cand))
        return jnp.where(cnt < need, cand, x)

    x = lax.fori_loop(0, 12, tie_step, jnp.zeros((group * ts, 1), I32))
    negm_c_all = jnp.where((sc_c > t) | ((sc_c == t) & (col_c <= x)), 0.0, -jnp.inf)
    negm_n_all = jnp.where((sc_n > t) | ((sc_n == t) & (idx_n <= x)), 0.0, -jnp.inf)

    dist_c = (row_c - col_c1 + past).astype(F32)
    dist_n = jnp.abs(row_n - col_n1).astype(F32)
    lam = _lambda(lq1_ref, lk1_ref, lq2_ref, lk2_ref)
    neg_new = jnp.where(new_ok, 0.0, -jnp.inf)
    for g in range(group):
        _sample_attend(g, stream_rows(g), negm_c_all, negm_n_all, dist_c, dist_n, lam, neg_new, pad_rows,
                       qa_ref, nka_ref, nva_ref, qb_ref, nkb_ref, nvb_ref,
                       cak_ref, cav_ref, cbk_ref, cbv_ref, sg_ref, oa_ref, ob_ref, pc_ref, pn_ref,
                       past=past, ts=ts)


def _sample_attend(g, r, negm_c_all, negm_n_all, dist_c, dist_n, lam, neg_new, pad_rows,
                   qa_ref, nka_ref, nva_ref, qb_ref, nkb_ref, nvb_ref,
                   cak_ref, cav_ref, cbk_ref, cbv_ref, sg_ref, oa_ref, ob_ref, pc_ref, pn_ref,
                   *, past, ts):
    negm_c = negm_c_all[r]
    negm_n = negm_n_all[r]
    q_all = qa_ref[:, r, :][:, :, 0:DH_A].reshape(H_A * ts, DH_A)
    s_c = _dot(q_all, cak_ref[g].astype(BF16))
    s_n = _dot_nt(q_all, pad_rows(nka_ref[r, :][:, 0:DH_A]))
    for h in range(H_A):
        rows = slice(h * ts, (h + 1) * ts)
        lc = s_c[rows] + (negm_c - (LOG2E * SLOPES_A[h]) * dist_c)
        ln = s_n[rows] + (negm_n - (LOG2E * SLOPES_A[h]) * dist_n)
        m = jnp.maximum(jnp.max(lc, axis=1, keepdims=True), jnp.max(ln, axis=1, keepdims=True))
        m = jnp.maximum(m, NEG_INIT)
        pc_ref[g, rows, :] = jnp.exp2(lc - m).astype(BF16)
        pn_ref[g, rows, :] = jnp.exp2(ln - m).astype(BF16)
    vc_aug_t = jnp.concatenate([cav_ref[g].astype(BF16), jnp.ones((64, past), BF16)], axis=0)
    pv = _dot_nt(pc_ref[g], vc_aug_t) + _dot(pn_ref[g], pad_rows(nva_ref[r, :]))
    pv = pv / pltpu.roll(pv, DH_A, axis=1)
    oa_ref[r, :] = jnp.concatenate(
        [pv[h * ts:(h + 1) * ts, :DH_A] for h in range(H_A)], axis=1).astype(BF16)

    for h in range(H_B):
        a_c = None
        a_n = None
        for c in range(2):
            hc = 2 * h + c
            q = qb_ref[hc, r, :][:, 0:DH_B]
            kc_t = cbk_ref[g, hc * DH_B:(hc + 1) * DH_B, :].astype(BF16)
            lc = _dot(q, kc_t) - (LOG2E * SLOPES_B[h]) * dist_c
            ln = _dot_nt(q, pad_rows(nkb_ref[hc, r, :][:, 0:DH_B])) + (neg_new - (LOG2E * SLOPES_B[h]) * dist_n)
            m = jnp.maximum(jnp.max(lc, axis=1, keepdims=True), jnp.max(ln, axis=1, keepdims=True))
            pc = jnp.exp2(lc - m)
            pn = jnp.exp2(ln - m)
            l = jnp.sum(pc, axis=1, keepdims=True) + jnp.sum(pn, axis=1, keepdims=True)
            pc = pc / l
            pn = pn / l
            if c == 0:
                a_c, a_n = pc, pn
            else:
                a_c, a_n = a_c - lam * pc, a_n - lam * pn
        vc = cbv_ref[g, pl.ds(h, past, stride=H_B), :].astype(BF16)
        vn = pad_rows(nvb_ref[r, h * DV_B:(h + 1) * DV_B])
        o = _dot(a_c.astype(BF16), vc) + _dot(a_n.astype(BF16), vn)
        o = _rms(o, sg_ref[...]) * (1.0 - LAM_INIT)
        ob_ref[r, h * DV_B:(h + 1) * DV_B] = o.astype(BF16)


def _sample(qa, qi, sm, nka, nva, nki, qb, nkb, nvb, cak, cav, cai, cbk, cbv,
            lq1, lk1, lq2, lk2, sg, batch, ts, past, k_top):
    group = SAMPLE_GROUP
    assert batch % group == 0
    rows = group * ts
    row = lambda b: (b, 0)
    head = lambda b: (0, b, 0)
    cache = lambda b: (b, 0, 0)
    const = lambda b: (0, 0)
    return pl.pallas_call(
        functools.partial(_sample_body, k_top=k_top, past=past, ts=ts, group=group),
        out_shape=(jax.ShapeDtypeStruct((batch * ts, H_A * DH_A), BF16),
                   jax.ShapeDtypeStruct((batch * ts, H_B * DV_B), BF16)),
        grid=(batch // group,),
        in_specs=[
            pl.BlockSpec((H_A, rows, 128), head),
            pl.BlockSpec((H_I, rows, D_IDX), head),
            pl.BlockSpec((rows, 256), row),
            pl.BlockSpec((rows, 128), row),
            pl.BlockSpec((rows, 128), row),
            pl.BlockSpec((rows, D_IDX), row),
            pl.BlockSpec((2 * H_B, rows, 128), head),
            pl.BlockSpec((2 * H_B, rows, 128), head),
            pl.BlockSpec((rows, H_B * DV_B), row),
            pl.BlockSpec((group, DH_A, past), cache),
            pl.BlockSpec((group, DH_A, past), cache),
            pl.BlockSpec((group, D_IDX, past), cache),
            pl.BlockSpec((group, 2 * H_B * DH_B, past), cache),
            pl.BlockSpec((group, past * H_B, DV_B), cache),
            pl.BlockSpec((1, DH_B), const),
            pl.BlockSpec((1, DH_B), const),
            pl.BlockSpec((1, DH_B), const),
            pl.BlockSpec((1, DH_B), const),
            pl.BlockSpec((1, DV_B), const),
        ],
        out_specs=(pl.BlockSpec((rows, H_A * DH_A), row), pl.BlockSpec((rows, H_B * DV_B), row)),
        scratch_shapes=[
            pltpu.VMEM((group, H_A * ts, past), BF16),
            pltpu.VMEM((group, H_A * ts, LANES), BF16),
        ],
        compiler_params=pltpu.CompilerParams(
            dimension_semantics=("arbitrary",), vmem_limit_bytes=VMEM_LIMIT),
        name="sample",
    )(qa, qi, sm, nka, nva, nki, qb, nkb, nvb, cak, cav, cai, cbk, cbv, lq1, lk1, lq2, lk2, sg)


def _finish_body(x_ref, oa_ref, ob_ref, wo_ref, gm_ref, wu_ref, wd_ref, gf_ref, y_ref):
    o = jnp.concatenate([oa_ref[...], ob_ref[...]], axis=1)
    h1 = x_ref[...] + _dot(o, wo_ref[...])
    hn = _rms(h1, gm_ref[...]).astype(BF16)
    acc = h1
    for c in range(D_FF // D_MODEL):
        u = jnp.maximum(_dot(hn, wu_ref[:, c * D_MODEL:(c + 1) * D_MODEL]), 0.0)
        acc = acc + _dot((u * u).astype(BF16), wd_ref[c * D_MODEL:(c + 1) * D_MODEL, :])
    y_ref[...] = _rms(acc, gf_ref[...])


def _finish(x, oa, ob, wo, gm, wu, wd, gf, tm):
    m = x.shape[0]
    assert m % tm == 0
    row = lambda i: (i, 0)
    const = lambda i: (0, 0)
    resident = dict(pipeline_mode=pl.Buffered(1))
    return pl.pallas_call(
        _finish_body,
        out_shape=jax.ShapeDtypeStruct((m, D_MODEL), F32),
        grid=(m // tm,),
        in_specs=[
            pl.BlockSpec((tm, D_MODEL), row),
            pl.BlockSpec((tm, 512), row),
            pl.BlockSpec((tm, 512), row),
            pl.BlockSpec((D_MODEL, D_MODEL), const, **resident),
            pl.BlockSpec((1, D_MODEL), const),
            pl.BlockSpec((D_MODEL, D_FF), const, **resident),
            pl.BlockSpec((D_FF, D_MODEL), const, **resident),
            pl.BlockSpec((1, D_MODEL), const),
        ],
        out_specs=pl.BlockSpec((tm, D_MODEL), row),
        compiler_params=pltpu.CompilerParams(
            dimension_semantics=("arbitrary",), vmem_limit_bytes=VMEM_LIMIT),
        name="finish",
    )(x, oa, ob, wo, gm, wu, wd, gf)


def kernel(x_prompt, x_sample, cache_a_k, cache_a_v, cache_a_idx_k, cache_b_k, cache_b_v,
           meta_tokens, attn_norm_g, w_in, idx_k_norm_g, idx_k_norm_b,
           lambda_q1, lambda_k1, lambda_q2, lambda_k2, subln_g, w_o,
           mlp_norm_g, w_up, w_down, final_norm_g):
    batch, seq, _ = x_prompt.shape
    dec_batch, ts, _ = x_sample.shape
    past = cache_a_k.shape[2]
    assert attn_norm_g.shape[0] == 1, "single-layer step"
    assert seq % TQ == 0 and ts == 16 and past % LANES == 0 and past // CHUNK == (past + ts - 1) // CHUNK
    n = N_META + seq
    k_top_p = min(TOPK_MAX, seq // 4)
    k_top_s = min(TOPK_MAX, (past + ts) // 4)

    w_t = w_in[0].T.astype(BF16)
    w = jnp.concatenate(
        [w_t[0:512],
         w_t[964:1476],
         w_t[1476:1988],
         w_t[1988:2500],
         w_t[640:896],
         w_t[512:640],
         w_t[896:964]],
        axis=0)
    w = jnp.pad(w, ((0, W_COLS - 2500), (0, 0))).T
    wvt = w_t[1476:2500]
    wo = w_o[0].astype(BF16)
    wu = w_up[0].astype(BF16)
    wd = w_down[0].astype(BF16)
    g_attn = attn_norm_g[0][None]
    g_mlp = mlp_norm_g[0][None]
    g_fin = final_norm_g[None]
    kng = idx_k_norm_g[0][None]
    knb = idx_k_norm_b[0][None]
    lq1, lk1, lq2, lk2 = lambda_q1[0][None], lambda_k1[0][None], lambda_q2[0][None], lambda_k2[0][None]
    sg = subln_g[0][None]

    xp = x_prompt.reshape(batch * seq, D_MODEL)
    xs = x_sample.reshape(dec_batch * ts, D_MODEL)

    (qa_p, qi_p, qb_p, kbh_p, vb_p, kab_p, kib_p, vt_p, wt_p, vbt_p, smt_p, kbt_p) = _proj(
        xp, g_attn, w, wvt, kng, knb, 512, N_META, seq, True)
    (_, _, _, kbh_m, vb_m, kab_m, kib_m, kb_m, vbb_m, sm_m, vaa_m) = _proj(
        meta_tokens, g_attn, w, wvt, kng, knb, N_META, 0, N_META, False)
    (qa_s, qi_s, qb_s, kbh_s, vb_s, kab_s, kib_s, kb_s, vbb_s, sm_s, vaa_s) = _proj(
        xs, g_attn, w, wvt, kng, knb, dec_batch * ts, past, dec_batch * ts, False)

    oa_p = _dsa(qa_p, qi_p, wt_p, kab_p, kib_p, vt_p, kab_m, kib_m, vaa_m, batch, seq, k_top_p)
    ob_p = _diff(qb_p, kbh_p, vbt_p, kbh_m, vbb_m, lq1, lk1, lq2, lk2, sg, batch, seq)
    y_prompt = _finish(xp, oa_p, ob_p, wo, g_mlp, wu, wd, g_fin, 512).reshape(batch, seq, D_MODEL)

    oa_s, ob_s = _sample(
        qa_s, qi_s, sm_s, kab_s, vaa_s, kib_s, qb_s, kbh_s, vbb_s,
        jnp.swapaxes(cache_a_k[0], 1, 2), jnp.swapaxes(cache_a_v[0], 1, 2),
        jnp.swapaxes(cache_a_idx_k[0], 1, 2),
        jnp.transpose(cache_b_k[0], (0, 2, 3, 4, 1)).reshape(dec_batch, 2 * H_B * DH_B, past),
        cache_b_v[0].reshape(dec_batch, past * H_B, DV_B),
        lq1, lk1, lq2, lk2, sg, dec_batch, ts, past, k_top_s)
    y_sample = _finish(xs, oa_s, ob_s, wo, g_mlp, wu, wd, g_fin, dec_batch * ts).reshape(
        dec_batch, ts, D_MODEL)

    def with_meta_t(meta_rows, main_t):
        meta_b = jnp.broadcast_to(meta_rows.T[None], (batch, 64, N_META))
        return jnp.swapaxes(jnp.concatenate([meta_b, main_t], axis=2), 1, 2)[None]

    new_a_k_p = with_meta_t(sm_m[:, SM_KA:SM_KA + 64], smt_p[:, SM_KA:SM_KA + 64])
    new_a_v_p = with_meta_t(sm_m[:, SM_VA:SM_VA + 64], smt_p[:, SM_VA:SM_VA + 64])
    new_a_i_p = with_meta_t(sm_m[:, SM_KI:SM_KI + 64], smt_p[:, SM_KI:SM_KI + 64])
    kb_meta_t = jnp.broadcast_to(kb_m.T.reshape(1, H_B, 2, DH_B, N_META), (batch, H_B, 2, DH_B, N_META))
    kb_main_t = kbt_p.reshape(batch, H_B, 2, DH_B, seq)
    new_b_k_p = jnp.transpose(
        jnp.concatenate([kb_meta_t, kb_main_t], axis=4), (0, 4, 1, 2, 3))[None]
    new_b_v_p = jnp.concatenate(
        [jnp.broadcast_to(vb_m.reshape(1, N_META, H_B, DV_B), (batch, N_META, H_B, DV_B)),
         vb_p.reshape(batch, seq, H_B, DV_B)], axis=1)[None]
    new_a_k_s = sm_s[:, SM_KA:SM_KA + 64].reshape(1, dec_batch, ts, 64)
    new_a_v_s = sm_s[:, SM_VA:SM_VA + 64].reshape(1, dec_batch, ts, 64)
    new_a_i_s = sm_s[:, SM_KI:SM_KI + 64].reshape(1, dec_batch, ts, 64)
    new_b_k_s = kb_s.reshape(1, dec_batch, ts, H_B, 2, DH_B)
    new_b_v_s = vb_s.reshape(1, dec_batch, ts, H_B, DV_B)
    return (y_prompt, y_sample, new_a_k_p, new_a_v_p, new_a_i_p, new_b_k_p, new_b_v_p,
            new_a_k_s, new_a_v_s, new_a_i_s, new_b_k_s, new_b_v_s)
```
